```python
import math
import jax, jax.numpy as jnp
from jax import lax
import numpy as np

D_MODEL = 1024
BATCH = 8
SEQ = 4096
DEPTH = 2

MIX_WIDTH = D_MODEL
SGU_WIDTH = MIX_WIDTH // 2
SGU_HEADS = 4
SGU_HEAD_DIM = SGU_WIDTH // SGU_HEADS
CHUNK = 128
ATT_WIDTH = MIX_WIDTH - SGU_WIDTH
ATT_HEADS = 8
ATT_HEAD_DIM = ATT_WIDTH // ATT_HEADS
DILATED_PATTERNS = ((128, 1), (512, 4), (2048, 16))
ATT_BLOCK = 128
ROPE_THETA = 10000.0
D_FF = 2816
IN_WIDTH = 2 * SGU_WIDTH + 3 * ATT_WIDTH
N_ADA = 9
EPS = 1e-6

kernel_name = "hybrid_sgu_dilated_macaron_adaln"


def rmsnorm(x, g):
    xf = x.astype(jnp.float32)
    y = xf * lax.rsqrt(jnp.mean(xf * xf, axis=-1, keepdims=True) + EPS)
    return (y * g.astype(jnp.float32)).astype(x.dtype)


def modulate(h, shift, scale):
    return h * (1.0 + scale[:, None, :]) + shift[:, None, :]


def swiglu(y, w_gate, w_up, w_down):
    return (jax.nn.silu(y @ w_gate) * (y @ w_up)) @ w_down


def rope_tables(S, dh, dtype):
    inv = ROPE_THETA ** (-jnp.arange(0, dh, 2, dtype=jnp.float32) / dh)
    ang = jnp.arange(S, dtype=jnp.float32)[:, None] * inv[None, :]
    ang = jnp.concatenate([ang, ang], axis=-1)
    return jnp.cos(ang)[:, None, :].astype(dtype), jnp.sin(ang)[:, None, :].astype(dtype)


def apply_rope(t, cos, sin):
    half = t.shape[-1] // 2
    rot = jnp.concatenate([-t[..., half:], t[..., :half]], axis=-1)
    return t * cos + rot * sin


def spatial_gating(u, v, ln_g, ln_b, w_s, b_s):
    B, S, H, dh = u.shape
    u = jax.nn.gelu(u)
    v = jax.nn.gelu(v)
    vf = v.astype(jnp.float32)
    mu = jnp.mean(vf, axis=-1, keepdims=True)
    var = jnp.mean(jnp.square(vf - mu), axis=-1, keepdims=True)
    vn = ((vf - mu) * lax.rsqrt(var + EPS)).astype(v.dtype) * ln_g + ln_b
    vc = vn.reshape(B, S // CHUNK, CHUNK, H, dh)
    causal = jnp.tril(jnp.ones((CHUNK, CHUNK), dtype=bool))
    ws = jnp.where(causal[None], w_s, jnp.zeros_like(w_s))
    z = jnp.einsum('hij,bnjhc->bnihc', ws, vc) + b_s.T[None, None, :, :, None]
    return u * z.reshape(B, S, H, dh)


def dilated_branch(q, k, v, window, dil):
    B, S, H, dh = q.shape
    n_back = window // dil
    span = dil * ATT_BLOCK
    S_pad = -(-S // span) * span
    L = S_pad // dil
    nb = L // ATT_BLOCK

    def to_blocks(t):
        t = jnp.pad(t, ((0, 0), (0, S_pad - S), (0, 0), (0, 0)))
        t = t.reshape(B, L, dil, H, dh).transpose(0, 2, 3, 1, 4)
        return t.reshape(B, dil, H, nb, ATT_BLOCK, dh)

    def with_prev(t):
        prev = jnp.concatenate([jnp.zeros_like(t[:, :, :, :1]), t[:, :, :, :-1]], axis=3)
        return jnp.concatenate([prev, t], axis=4)

    qb = to_blocks(q)
    kk = with_prev(to_blocks(k))
    vv = with_prev(to_blocks(v))
    s = jnp.einsum('brhnqd,brhnkd->brhnqk', qb, kk,
                   preferred_element_type=jnp.float32) * (1.0 / math.sqrt(dh))
    qi = jnp.arange(ATT_BLOCK)[:, None]
    ki = jnp.arange(2 * ATT_BLOCK)[None, :]
    diff = qi + ATT_BLOCK - ki
    band = (diff >= 0) & (diff <= n_back)
    valid = (jnp.arange(nb)[:, None, None] > 0) | (ki[None] >= ATT_BLOCK)
    mask = band[None] & valid
    s = jnp.where(mask, s, -jnp.inf)
    m = jnp.max(s, axis=-1, keepdims=True)
    p = jnp.exp(s - m)
    den = jnp.sum(p, axis=-1, keepdims=True)
    o = jnp.einsum('brhnqk,brhnkd->brhnqd', p, vv.astype(jnp.float32)) / den
    lse = (m + jnp.log(den))[..., 0]
    o = o.reshape(B, dil, H, L, dh).transpose(0, 3, 1, 2, 4).reshape(B, S_pad, H, dh)[:, :S]
    lse = lse.reshape(B, dil, H, L).transpose(0, 3, 1, 2).reshape(B, S_pad, H)[:, :S]
    return o, lse


def dilated_mixture(q, k, v):
    outs, lses = [], []
    for window, dil in DILATED_PATTERNS:
        o, lse = dilated_branch(q, k, v, window, dil)
        outs.append(o)
        lses.append(lse)
    w = jax.nn.softmax(jnp.stack(lses, axis=0), axis=0)
    o = jnp.sum(w[..., None] * jnp.stack(outs, axis=0), axis=0)
    return o.astype(q.dtype)


def _fwd_setup_inputs(seed: int = 0) -> dict:
    key = jax.random.key(seed)
    ks = jax.random.split(key, 20)
    f32 = jnp.float32
    nrm = lambda k, shape, scale: jax.random.normal(k, shape, f32) * scale
    return {
        "x": nrm(ks[0], (BATCH, SEQ, D_MODEL), 1.0),
        "c": nrm(ks[1], (BATCH, D_MODEL), 1.0),
        "ada_w": nrm(ks[2], (DEPTH, D_MODEL, N_ADA * D_MODEL), D_MODEL ** -0.5),
        "ada_b": nrm(ks[3], (DEPTH, N_ADA * D_MODEL), 0.02),
        "norm_g": 1.0 + nrm(ks[4], (DEPTH, 3, D_MODEL), 0.02),
        "ffn1_wg": nrm(ks[5], (DEPTH, D_MODEL, D_FF), D_MODEL ** -0.5),
        "ffn1_wu": nrm(ks[6], (DEPTH, D_MODEL, D_FF), D_MODEL ** -0.5),
        "ffn1_wd": nrm(ks[7], (DEPTH, D_FF, D_MODEL), D_FF ** -0.5),
        "ffn2_wg": nrm(ks[8], (DEPTH, D_MODEL, D_FF), D_MODEL ** -0.5),
        "ffn2_wu": nrm(ks[9], (DEPTH, D_MODEL, D_FF), D_MODEL ** -0.5),
        "ffn2_wd": nrm(ks[10], (DEPTH, D_FF, D_MODEL), D_FF ** -0.5),
        "w_in": nrm(ks[11], (DEPTH, D_MODEL, IN_WIDTH), D_MODEL ** -0.5),
        "sgu_ln_g": 1.0 + nrm(ks[12], (DEPTH, SGU_HEADS, SGU_HEAD_DIM), 0.02),
        "sgu_ln_b": nrm(ks[13], (DEPTH, SGU_HEADS, SGU_HEAD_DIM), 0.02),
        "sgu_w": nrm(ks[14], (DEPTH, SGU_HEADS, CHUNK, CHUNK), CHUNK ** -0.5),
        "sgu_b": 1.0 + nrm(ks[15], (DEPTH, SGU_HEADS, CHUNK), 0.02),
        "w_out": nrm(ks[16], (DEPTH, MIX_WIDTH, D_MODEL), MIX_WIDTH ** -0.5),
        "final_g": 1.0 + nrm(ks[17], (D_MODEL,), 0.02),
    }


def _fwd_reference(x, c, ada_w, ada_b, norm_g, ffn1_wg, ffn1_wu, ffn1_wd, ffn2_wg, ffn2_wu, ffn2_wd,
              w_in, sgu_ln_g, sgu_ln_b, sgu_w, sgu_b, w_out, final_g):
    B, S, D = x.shape
    cos, sin = rope_tables(S, ATT_HEAD_DIM, x.dtype)
    c_act = jax.nn.silu(c)
    h = x
    for l in range(DEPTH):
        mod = c_act @ ada_w[l] + ada_b[l]
        sh1, sc1, g1, sh2, sc2, g2, sh3, sc3, g3 = jnp.split(mod, N_ADA, axis=-1)

        y = modulate(rmsnorm(h, norm_g[l, 0]), sh1, sc1)
        h = h + 0.5 * g1[:, None, :] * swiglu(y, ffn1_wg[l], ffn1_wu[l], ffn1_wd[l])

        y = modulate(rmsnorm(h, norm_g[l, 1]), sh2, sc2)
        proj = y @ w_in[l]
        u_a, v_a, q_b, k_b, v_b = jnp.split(
            proj, [SGU_WIDTH, 2 * SGU_WIDTH, 2 * SGU_WIDTH + ATT_WIDTH, 2 * SGU_WIDTH + 2 * ATT_WIDTH], axis=-1)
        u_a = u_a.reshape(B, S, SGU_HEADS, SGU_HEAD_DIM)
        v_a = v_a.reshape(B, S, SGU_HEADS, SGU_HEAD_DIM)
        out_a = spatial_gating(u_a, v_a, sgu_ln_g[l], sgu_ln_b[l], sgu_w[l], sgu_b[l])
        q_b = apply_rope(q_b.reshape(B, S, ATT_HEADS, ATT_HEAD_DIM), cos, sin)
        k_b = apply_rope(k_b.reshape(B, S, ATT_HEADS, ATT_HEAD_DIM), cos, sin)
        v_b = v_b.reshape(B, S, ATT_HEADS, ATT_HEAD_DIM)
        out_b = dilated_mixture(q_b, k_b, v_b)
        mixed = jnp.concatenate([out_a.reshape(B, S, SGU_WIDTH), out_b.reshape(B, S, ATT_WIDTH)], axis=-1)
        h = h + g2[:, None, :] * (mixed @ w_out[l])

        y = modulate(rmsnorm(h, norm_g[l, 2]), sh3, sc3)
        h = h + 0.5 * g3[:, None, :] * swiglu(y, ffn2_wg[l], ffn2_wu[l], ffn2_wd[l])
    return rmsnorm(h, final_g)


import jax as _jax
import jax.numpy as _jnp

TWIN_FORMAT = 'train_step'
FWD_PARAMS = ['x', 'c', 'ada_w', 'ada_b', 'norm_g', 'ffn1_wg', 'ffn1_wu', 'ffn1_wd', 'ffn2_wg', 'ffn2_wu', 'ffn2_wd', 'w_in', 'sgu_ln_g', 'sgu_ln_b', 'sgu_w', 'sgu_b', 'w_out', 'final_g']
TWIN_WEIGHTS = ['ada_w', 'ada_b', 'norm_g', 'ffn1_wg', 'ffn1_wu', 'ffn1_wd', 'ffn2_wg', 'ffn2_wu', 'ffn2_wd', 'w_in', 'sgu_ln_g', 'sgu_ln_b', 'sgu_w', 'sgu_b', 'w_out', 'final_g']
TWIN_DIFF_INPUT = 'x'
TWIN_INPUTS = ['x', 'c', 'ada_w', 'ada_b', 'norm_g', 'ffn1_wg', 'ffn1_wu', 'ffn1_wd', 'ffn2_wg', 'ffn2_wu', 'ffn2_wd', 'w_in', 'sgu_ln_g', 'sgu_ln_b', 'sgu_w', 'sgu_b', 'w_out', 'final_g', 'loss_target', 'm_ada_w', 'm_ada_b', 'm_norm_g', 'm_ffn1_wg', 'm_ffn1_wu', 'm_ffn1_wd', 'm_ffn2_wg', 'm_ffn2_wu', 'm_ffn2_wd', 'm_w_in', 'm_sgu_ln_g', 'm_sgu_ln_b', 'm_sgu_w', 'm_sgu_b', 'm_w_out', 'm_final_g', 'v_ada_w', 'v_ada_b', 'v_norm_g', 'v_ffn1_wg', 'v_ffn1_wu', 'v_ffn1_wd', 'v_ffn2_wg', 'v_ffn2_wu', 'v_ffn2_wd', 'v_w_in', 'v_sgu_ln_g', 'v_sgu_ln_b', 'v_sgu_w', 'v_sgu_b', 'v_w_out', 'v_final_g']
TWIN_OUTPUTS = ['loss', 'grad_x', 'grad_ada_w', 'grad_ada_b', 'grad_norm_g', 'grad_ffn1_wg', 'grad_ffn1_wu', 'grad_ffn1_wd', 'grad_ffn2_wg', 'grad_ffn2_wu', 'grad_ffn2_wd', 'grad_w_in', 'grad_sgu_ln_g', 'grad_sgu_ln_b', 'grad_sgu_w', 'grad_sgu_b', 'grad_w_out', 'grad_final_g', 'delta_ada_w', 'delta_ada_b', 'delta_norm_g', 'delta_ffn1_wg', 'delta_ffn1_wu', 'delta_ffn1_wd', 'delta_ffn2_wg', 'delta_ffn2_wu', 'delta_ffn2_wd', 'delta_w_in', 'delta_sgu_ln_g', 'delta_sgu_ln_b', 'delta_sgu_w', 'delta_sgu_b', 'delta_w_out', 'delta_final_g', 'new_m_ada_w', 'new_m_ada_b', 'new_m_norm_g', 'new_m_ffn1_wg', 'new_m_ffn1_wu', 'new_m_ffn1_wd', 'new_m_ffn2_wg', 'new_m_ffn2_wu', 'new_m_ffn2_wd', 'new_m_w_in', 'new_m_sgu_ln_g', 'new_m_sgu_ln_b', 'new_m_sgu_w', 'new_m_sgu_b', 'new_m_w_out', 'new_m_final_g', 'new_v_ada_w', 'new_v_ada_b', 'new_v_norm_g', 'new_v_ffn1_wg', 'new_v_ffn1_wu', 'new_v_ffn1_wd', 'new_v_ffn2_wg', 'new_v_ffn2_wu', 'new_v_ffn2_wd', 'new_v_w_in', 'new_v_sgu_ln_g', 'new_v_sgu_ln_b', 'new_v_sgu_w', 'new_v_sgu_b', 'new_v_w_out', 'new_v_final_g']
TWIN_LEAF_KINDS = {'loss': 'loss', 'grad_x': 'grad_x', 'grad_ada_w': 'grad_w', 'grad_ada_b': 'grad_w', 'grad_norm_g': 'grad_w', 'grad_ffn1_wg': 'grad_w', 'grad_ffn1_wu': 'grad_w', 'grad_ffn1_wd': 'grad_w', 'grad_ffn2_wg': 'grad_w', 'grad_ffn2_wu': 'grad_w', 'grad_ffn2_wd': 'grad_w', 'grad_w_in': 'grad_w', 'grad_sgu_ln_g': 'grad_w', 'grad_sgu_ln_b': 'grad_w', 'grad_sgu_w': 'grad_w', 'grad_sgu_b': 'grad_w', 'grad_w_out': 'grad_w', 'grad_final_g': 'grad_w', 'delta_ada_w': 'delta_w', 'delta_ada_b': 'delta_w', 'delta_norm_g': 'delta_w', 'delta_ffn1_wg': 'delta_w', 'delta_ffn1_wu': 'delta_w', 'delta_ffn1_wd': 'delta_w', 'delta_ffn2_wg': 'delta_w', 'delta_ffn2_wu': 'delta_w', 'delta_ffn2_wd': 'delta_w', 'delta_w_in': 'delta_w', 'delta_sgu_ln_g': 'delta_w', 'delta_sgu_ln_b': 'delta_w', 'delta_sgu_w': 'delta_w', 'delta_sgu_b': 'delta_w', 'delta_w_out': 'delta_w', 'delta_final_g': 'delta_w', 'new_m_ada_w': 'new_m', 'new_m_ada_b': 'new_m', 'new_m_norm_g': 'new_m', 'new_m_ffn1_wg': 'new_m', 'new_m_ffn1_wu': 'new_m', 'new_m_ffn1_wd': 'new_m', 'new_m_ffn2_wg': 'new_m', 'new_m_ffn2_wu': 'new_m', 'new_m_ffn2_wd': 'new_m', 'new_m_w_in': 'new_m', 'new_m_sgu_ln_g': 'new_m', 'new_m_sgu_ln_b': 'new_m', 'new_m_sgu_w': 'new_m', 'new_m_sgu_b': 'new_m', 'new_m_w_out': 'new_m', 'new_m_final_g': 'new_m', 'new_v_ada_w': 'new_v', 'new_v_ada_b': 'new_v', 'new_v_norm_g': 'new_v', 'new_v_ffn1_wg': 'new_v', 'new_v_ffn1_wu': 'new_v', 'new_v_ffn1_wd': 'new_v', 'new_v_ffn2_wg': 'new_v', 'new_v_ffn2_wu': 'new_v', 'new_v_ffn2_wd': 'new_v', 'new_v_w_in': 'new_v', 'new_v_sgu_ln_g': 'new_v', 'new_v_sgu_ln_b': 'new_v', 'new_v_sgu_w': 'new_v', 'new_v_sgu_b': 'new_v', 'new_v_w_out': 'new_v', 'new_v_final_g': 'new_v'}


def _forward(args):
    return _fwd_reference(*[args[k] for k in FWD_PARAMS])


def _output_shape():
    out = _jax.eval_shape(lambda: _forward(_fwd_setup_inputs(0)))
    return out.shape, out.dtype

N_MICROBATCH = 1
ADAM_LR = 0.001
ADAM_B1 = 0.9
ADAM_B2 = 0.999
ADAM_EPS = 1e-08
ADAM_WD = 0.01
ADAM_STEP = 10
PER_EXAMPLE_BATCH_AXIS = {'x': 0, 'c': 0, 'loss_target': 0}
SHARED_INPUTS = []
_WEIGHT_DTYPES = {'ada_w': _jnp.float32, 'ada_b': _jnp.float32, 'norm_g': _jnp.float32, 'ffn1_wg': _jnp.float32, 'ffn1_wu': _jnp.float32, 'ffn1_wd': _jnp.float32, 'ffn2_wg': _jnp.float32, 'ffn2_wu': _jnp.float32, 'ffn2_wd': _jnp.float32, 'w_in': _jnp.float32, 'sgu_ln_g': _jnp.float32, 'sgu_ln_b': _jnp.float32, 'sgu_w': _jnp.float32, 'sgu_b': _jnp.float32, 'w_out': _jnp.float32, 'final_g': _jnp.float32}
MOMENT_SCALE = {'ada_w': 5.851183e-02, 'ada_b': 9.958459e-02, 'norm_g': 7.205603e-02, 'ffn1_wg': 3.426566e-02, 'ffn1_wu': 3.348080e-02, 'ffn1_wd': 5.536314e-02, 'ffn2_wg': 2.987171e-02, 'ffn2_wu': 2.940203e-02, 'ffn2_wd': 4.848697e-02, 'w_in': 6.856377e-02, 'sgu_ln_g': 5.215788e-02, 'sgu_ln_b': 5.187924e-02, 'sgu_w': 5.570354e-02, 'sgu_b': 8.348843e-02, 'w_out': 9.796163e-02, 'final_g': 3.291741e+01}


def _to_microbatches(a, axis):
    t = _jnp.moveaxis(a, axis, 0)
    t = t.reshape((N_MICROBATCH, t.shape[0] // N_MICROBATCH) + t.shape[1:])
    return _jnp.moveaxis(t, 1, axis + 1)


def setup_inputs(seed: int = 0) -> dict:
    inp = _fwd_setup_inputs(seed)
    key = _jax.random.fold_in(_jax.random.key(seed), 7919)
    shape, _ = _output_shape()
    out = dict(inp)
    out["loss_target"] = _jax.random.normal(_jax.random.fold_in(key, 0), shape, _jnp.float32)
    for i, name in enumerate(TWIN_WEIGHTS):
        w = inp[name].astype(_jnp.float32)
        if MOMENT_SCALE is None:
            s = _jnp.sqrt(_jnp.mean(_jnp.square(w)) + 1e-30)
        else:
            s = MOMENT_SCALE[name]
        km, kv = _jax.random.split(_jax.random.fold_in(key, i + 1))
        out[name] = w
        out["m_" + name] = s * _jax.random.normal(km, w.shape, _jnp.float32)
        out["v_" + name] = (s * s) * _jax.random.uniform(kv, w.shape, _jnp.float32, 0.5, 1.5)
    if N_MICROBATCH > 1:
        for name, axis in PER_EXAMPLE_BATCH_AXIS.items():
            out[name] = _to_microbatches(out[name], axis)
    return {'x': out['x'], 'c': out['c'], 'ada_w': out['ada_w'], 'ada_b': out['ada_b'], 'norm_g': out['norm_g'], 'ffn1_wg': out['ffn1_wg'], 'ffn1_wu': out['ffn1_wu'], 'ffn1_wd': out['ffn1_wd'], 'ffn2_wg': out['ffn2_wg'], 'ffn2_wu': out['ffn2_wu'], 'ffn2_wd': out['ffn2_wd'], 'w_in': out['w_in'], 'sgu_ln_g': out['sgu_ln_g'], 'sgu_ln_b': out['sgu_ln_b'], 'sgu_w': out['sgu_w'], 'sgu_b': out['sgu_b'], 'w_out': out['w_out'], 'final_g': out['final_g'], 'loss_target': out['loss_target'], 'm_ada_w': out['m_ada_w'], 'm_ada_b': out['m_ada_b'], 'm_norm_g': out['m_norm_g'], 'm_ffn1_wg': out['m_ffn1_wg'], 'm_ffn1_wu': out['m_ffn1_wu'], 'm_ffn1_wd': out['m_ffn1_wd'], 'm_ffn2_wg': out['m_ffn2_wg'], 'm_ffn2_wu': out['m_ffn2_wu'], 'm_ffn2_wd': out['m_ffn2_wd'], 'm_w_in': out['m_w_in'], 'm_sgu_ln_g': out['m_sgu_ln_g'], 'm_sgu_ln_b': out['m_sgu_ln_b'], 'm_sgu_w': out['m_sgu_w'], 'm_sgu_b': out['m_sgu_b'], 'm_w_out': out['m_w_out'], 'm_final_g': out['m_final_g'], 'v_ada_w': out['v_ada_w'], 'v_ada_b': out['v_ada_b'], 'v_norm_g': out['v_norm_g'], 'v_ffn1_wg': out['v_ffn1_wg'], 'v_ffn1_wu': out['v_ffn1_wu'], 'v_ffn1_wd': out['v_ffn1_wd'], 'v_ffn2_wg': out['v_ffn2_wg'], 'v_ffn2_wu': out['v_ffn2_wu'], 'v_ffn2_wd': out['v_ffn2_wd'], 'v_w_in': out['v_w_in'], 'v_sgu_ln_g': out['v_sgu_ln_g'], 'v_sgu_ln_b': out['v_sgu_ln_b'], 'v_sgu_w': out['v_sgu_w'], 'v_sgu_b': out['v_sgu_b'], 'v_w_out': out['v_w_out'], 'v_final_g': out['v_final_g']}


def _loss(weights, diff, rest, loss_target):
    with _jax.named_scope("forward"):
        args = {**rest, TWIN_DIFF_INPUT: diff, **{k: w.astype(_WEIGHT_DTYPES[k]) for k, w in weights.items()}}
        y = _forward(args)
    with _jax.named_scope("loss_head"):
        err = _jnp.square(y.astype(_jnp.float32) - loss_target)
        return 0.5 * _jnp.sum(_jnp.mean(err, axis=-1)) if err.ndim else 0.5 * err


def _adamw(w, g, m, v):
    m = ADAM_B1 * m + (1.0 - ADAM_B1) * g
    v = ADAM_B2 * v + (1.0 - ADAM_B2) * _jnp.square(g)
    m_hat = m / (1.0 - ADAM_B1 ** ADAM_STEP)
    v_hat = v / (1.0 - ADAM_B2 ** ADAM_STEP)
    delta = -ADAM_LR * (m_hat / (_jnp.sqrt(v_hat) + ADAM_EPS) + ADAM_WD * w)
    return delta, m, v


def reference(x, c, ada_w, ada_b, norm_g, ffn1_wg, ffn1_wu, ffn1_wd, ffn2_wg, ffn2_wu, ffn2_wd, w_in, sgu_ln_g, sgu_ln_b, sgu_w, sgu_b, w_out, final_g, loss_target, m_ada_w, m_ada_b, m_norm_g, m_ffn1_wg, m_ffn1_wu, m_ffn1_wd, m_ffn2_wg, m_ffn2_wu, m_ffn2_wd, m_w_in, m_sgu_ln_g, m_sgu_ln_b, m_sgu_w, m_sgu_b, m_w_out, m_final_g, v_ada_w, v_ada_b, v_norm_g, v_ffn1_wg, v_ffn1_wu, v_ffn1_wd, v_ffn2_wg, v_ffn2_wu, v_ffn2_wd, v_w_in, v_sgu_ln_g, v_sgu_ln_b, v_sgu_w, v_sgu_b, v_w_out, v_final_g):
    given = dict(x=x, c=c, ada_w=ada_w, ada_b=ada_b, norm_g=norm_g, ffn1_wg=ffn1_wg, ffn1_wu=ffn1_wu, ffn1_wd=ffn1_wd, ffn2_wg=ffn2_wg, ffn2_wu=ffn2_wu, ffn2_wd=ffn2_wd, w_in=w_in, sgu_ln_g=sgu_ln_g, sgu_ln_b=sgu_ln_b, sgu_w=sgu_w, sgu_b=sgu_b, w_out=w_out, final_g=final_g, loss_target=loss_target, m_ada_w=m_ada_w, m_ada_b=m_ada_b, m_norm_g=m_norm_g, m_ffn1_wg=m_ffn1_wg, m_ffn1_wu=m_ffn1_wu, m_ffn1_wd=m_ffn1_wd, m_ffn2_wg=m_ffn2_wg, m_ffn2_wu=m_ffn2_wu, m_ffn2_wd=m_ffn2_wd, m_w_in=m_w_in, m_sgu_ln_g=m_sgu_ln_g, m_sgu_ln_b=m_sgu_ln_b, m_sgu_w=m_sgu_w, m_sgu_b=m_sgu_b, m_w_out=m_w_out, m_final_g=m_final_g, v_ada_w=v_ada_w, v_ada_b=v_ada_b, v_norm_g=v_norm_g, v_ffn1_wg=v_ffn1_wg, v_ffn1_wu=v_ffn1_wu, v_ffn1_wd=v_ffn1_wd, v_ffn2_wg=v_ffn2_wg, v_ffn2_wu=v_ffn2_wu, v_ffn2_wd=v_ffn2_wd, v_w_in=v_w_in, v_sgu_ln_g=v_sgu_ln_g, v_sgu_ln_b=v_sgu_ln_b, v_sgu_w=v_sgu_w, v_sgu_b=v_sgu_b, v_w_out=v_w_out, v_final_g=v_final_g)
    weights = {n: given[n] for n in TWIN_WEIGHTS}
    shared = {n: given[n] for n in SHARED_INPUTS}
    per_example = {n: given[n] for n in ['x', 'c']}
    grad_fn = _jax.value_and_grad(_loss, argnums=(0, 1))

    def one_microbatch(ex, loss_target):
        ex = dict(ex)
        diff = ex.pop(TWIN_DIFF_INPUT)
        return grad_fn(weights, diff, {**shared, **ex}, loss_target)

    if N_MICROBATCH == 1:
        loss, (grad_w, grad_x) = one_microbatch(per_example, given["loss_target"])
    else:
        def body(carry, xs):
            loss_sum, grad_sum = carry
            l_k, (gw_k, gx_k) = one_microbatch(xs[0], xs[1])
            with _jax.named_scope("update"):
                return (loss_sum + l_k, _jax.tree.map(_jnp.add, grad_sum, gw_k)), gx_k

        init = (_jnp.zeros((), _jnp.float32), _jax.tree.map(_jnp.zeros_like, weights))
        (loss, grad_w), grad_x = _jax.lax.scan(body, init, (per_example, given["loss_target"]))
    with _jax.named_scope("update"):
        delta_w, new_m, new_v = {}, {}, {}
        for n in TWIN_WEIGHTS:
            delta_w[n], new_m[n], new_v[n] = _adamw(weights[n], grad_w[n], given["m_" + n], given["v_" + n])
    return (loss, grad_x, *[grad_w[n] for n in TWIN_WEIGHTS], *[delta_w[n] for n in TWIN_WEIGHTS],
            *[new_m[n] for n in TWIN_WEIGHTS], *[new_v[n] for n in TWIN_WEIGHTS])
```

```python
import functools
import math

import jax
import jax.numpy as jnp
from jax import lax
from jax.experimental import pallas as pl
from jax.experimental.pallas import tpu as pltpu

f32, bf16 = jnp.float32, jnp.bfloat16
SDS = jax.ShapeDtypeStruct

T, D, F = 4096, 1024, 2816
NDEV, DEPTH = 8, 2
HW = 512
INW = 5 * HW
FS, INS, OUTS, ADAS = F // NDEV, INW // NDEV, D // NDEV, 9 * D // NDEV
CH = 128
PATTERN_DILATIONS = (1, 4, 16)
ROPE_THETA = 10000.0
EPS = 1e-6
LR, B1, B2, AEPS, WD, STEP = 0.001, 0.9, 0.999, 1e-08, 0.01, 10
MESH = pl.DeviceIdType.MESH


def _cp(sems, vmem_mb):
    return pltpu.CompilerParams(dimension_semantics=sems, vmem_limit_bytes=vmem_mb << 20)


def _nn(a, b):
    return lax.dot_general(a, b, (((1,), (0,)), ((), ())), preferred_element_type=f32)


def _nt(a, b):
    return lax.dot_general(a, b, (((1,), (1,)), ((), ())), preferred_element_type=f32)


def _tn(a, b):
    return lax.dot_general(a, b, (((0,), (0,)), ((), ())), preferred_element_type=f32)


def _colsum(a):
    return jnp.sum(a, axis=0, keepdims=True)


def _rowmean(a):
    return jnp.mean(a, axis=-1, keepdims=True)


def _norm_mod(x, par):
    rstd = lax.rsqrt(_rowmean(x * x) + EPS)
    xhat = x * rstd
    n = xhat * par[3:4, :]
    y = n * (1.0 + par[1:2, :]) + par[0:1, :]
    return y, n, xhat, rstd


def _norm_mod_bwd(dy, x, par):
    _, n, xhat, rstd = _norm_mod(x, par)
    dn = dy * (1.0 + par[1:2, :])
    dxhat = dn * par[3:4, :]
    dx = rstd * (dxhat - xhat * _rowmean(dxhat * xhat))
    return dx, _colsum(dy), _colsum(dy * n), _colsum(dn * xhat)


_GK = math.sqrt(2.0 / math.pi)


def _gelu(x):
    return 0.5 * x * (1.0 + jnp.tanh(_GK * (x + 0.044715 * x * x * x)))


def _gelu_grad(x):
    t = jnp.tanh(_GK * (x + 0.044715 * x * x * x))
    return 0.5 * (1.0 + t) + 0.5 * x * (1.0 - t * t) * (_GK * (1.0 + 3.0 * 0.044715 * x * x))


def _rows(ref, idx, r):
    if len(ref.shape) == 3:
        return ref.at[:, pl.ds(idx * r, r), :]
    return ref.at[pl.ds(idx * r, r), :]


def _flip(v, bit):
    return 1 - v if bit else v


def _all_gather_small(x, name):
    R, C = x.shape

    def body(x_ref, out_ref, send_sems, recv_sems):
        mx, my, mc = lax.axis_index("x"), lax.axis_index("y"), lax.axis_index("c")
        me = 4 * mx + 2 * my + mc
        out_ref[me] = x_ref[...]
        sent = []
        for k in range(1, NDEV):
            peer = (_flip(mx, k & 4), _flip(my, k & 2), _flip(mc, k & 1))
            cp = pltpu.make_async_remote_copy(
                src_ref=x_ref, dst_ref=out_ref.at[me], send_sem=send_sems.at[k - 1],
                recv_sem=recv_sems.at[k - 1], device_id=peer, device_id_type=MESH)
            cp.start()
            sent.append(cp)
        for k in range(1, NDEV):
            peer = (_flip(mx, k & 4), _flip(my, k & 2), _flip(mc, k & 1))
            pidx = 4 * peer[0] + 2 * peer[1] + peer[2]
            pltpu.make_async_remote_copy(
                src_ref=x_ref, dst_ref=out_ref.at[pidx], send_sem=send_sems.at[k - 1],
                recv_sem=recv_sems.at[k - 1], device_id=peer, device_id_type=MESH).wait_recv()
        for cp in sent:
            cp.wait_send()

    vm = pl.BlockSpec(memory_space=pltpu.VMEM)
    return pl.pallas_call(
        body, name=name, out_shape=SDS((NDEV, R, C), f32), in_specs=[vm], out_specs=vm,
        scratch_shapes=[pltpu.SemaphoreType.DMA((NDEV - 1,)), pltpu.SemaphoreType.DMA((NDEV - 1,))],
        compiler_params=pltpu.CompilerParams(vmem_limit_bytes=32 << 20),
    )(x)


def _all_gather_rows(locs, name):
    n = len(locs)
    rs = [a.shape[-2] for a in locs]

    def body(*refs):
        src, out = refs[:n], refs[n:2 * n]
        send_sems, recv_sems, loc_sems = refs[2 * n:]
        mx, my, mc = lax.axis_index("x"), lax.axis_index("y"), lax.axis_index("c")
        me, sib = (mx, my, mc), (mx, my, 1 - mc)
        chips = [(1 - mx, my), (mx, 1 - my), (1 - mx, 1 - my)]

        def blk(a, p):
            return _rows(out[a], 4 * p[0] + 2 * p[1] + p[2], rs[a])

        def copy(k, a, block, to, from_src=False):
            return pltpu.make_async_remote_copy(
                src_ref=src[a] if from_src else blk(a, block), dst_ref=blk(a, block),
                send_sem=send_sems.at[k * n + a], recv_sem=recv_sems.at[k * n + a],
                device_id=to, device_id_type=MESH)

        mine = [pltpu.make_async_copy(src[a], blk(a, me), loc_sems.at[a]) for a in range(n)]
        for m in mine:
            m.start()
        first = []
        for j, chip in enumerate(chips):
            first += [copy(1 + j, a, me, (*chip, mc), True) for a in range(n)]
        first += [copy(0, a, me, sib, True) for a in range(n)]
        for cp in first:
            cp.start()
        passed = []
        for j, chip in enumerate(chips):
            for a in range(n):
                copy(1 + j, a, (*chip, mc), me).wait_recv()
            fwd = [copy(4 + j, a, (*chip, mc), sib) for a in range(n)]
            for cp in fwd:
                cp.start()
            passed += fwd
        for a in range(n):
            copy(0, a, sib, me).wait_recv()
        for j, chip in enumerate(chips):
            for a in range(n):
                copy(4 + j, a, (*chip, 1 - mc), me).wait_recv()
        for cp in first + passed:
            cp.wait_send()
        for m in mine:
            m.wait()

    hbm = pl.BlockSpec(memory_space=pl.ANY)
    out_shape = [SDS(a.shape[:-2] + (NDEV * a.shape[-2], a.shape[-1]), a.dtype) for a in locs]
    return pl.pallas_call(
        body, name=name, out_shape=out_shape, in_specs=[hbm] * n, out_specs=[hbm] * n,
        scratch_shapes=[pltpu.SemaphoreType.DMA((7 * n,)), pltpu.SemaphoreType.DMA((7 * n,)),
                        pltpu.SemaphoreType.DMA((n,))],
    )(*locs)


def _scatter_rows(fulls, name):
    n = len(fulls)
    rs = [a.shape[-2] // NDEV for a in fulls]

    def body(*refs):
        g, land = refs[:n], refs[n:2 * n]
        send_sems, recv_sems, loc_sems = refs[2 * n:]
        mx, my, mc = lax.axis_index("x"), lax.axis_index("y"), lax.axis_index("c")
        me = 4 * mx + 2 * my + mc
        mine = [pltpu.make_async_copy(_rows(g[a], me, rs[a]), land[a].at[me], loc_sems.at[a]) for a in range(n)]
        for m in mine:
            m.start()
        sent = []
        for k in range(1, NDEV):
            peer = (_flip(mx, k & 4), _flip(my, k & 2), _flip(mc, k & 1))
            pidx = 4 * peer[0] + 2 * peer[1] + peer[2]
            for a in range(n):
                cp = pltpu.make_async_remote_copy(
                    src_ref=_rows(g[a], pidx, rs[a]), dst_ref=land[a].at[me],
                    send_sem=send_sems.at[(k - 1) * n + a], recv_sem=recv_sems.at[(k - 1) * n + a],
                    device_id=peer, device_id_type=MESH)
                cp.start()
                sent.append(cp)
        for k in range(1, NDEV):
            peer = (_flip(mx, k & 4), _flip(my, k & 2), _flip(mc, k & 1))
            pidx = 4 * peer[0] + 2 * peer[1] + peer[2]
            for a in range(n):
                pltpu.make_async_remote_copy(
                    src_ref=_rows(g[a], pidx, rs[a]), dst_ref=land[a].at[pidx],
                    send_sem=send_sems.at[(k - 1) * n + a], recv_sem=recv_sems.at[(k - 1) * n + a],
                    device_id=peer, device_id_type=MESH).wait_recv()
        for cp in sent:
            cp.wait_send()
        for m in mine:
            m.wait()

    hbm = pl.BlockSpec(memory_space=pl.ANY)
    out_shape = [SDS((NDEV,) + a.shape[:-2] + (a.shape[-2] // NDEV, a.shape[-1]), a.dtype) for a in fulls]
    return pl.pallas_call(
        body, name=name, out_shape=out_shape, in_specs=[hbm] * n, out_specs=[hbm] * n,
        scratch_shapes=[pltpu.SemaphoreType.DMA((7 * n,)), pltpu.SemaphoreType.DMA((7 * n,)),
                        pltpu.SemaphoreType.DMA((n,))],
    )(*fulls)


def _mod_fwd(c_all, ada_w, ada_b_loc):
    def body(c_ref, w_ref, b_ref, o_ref):
        ca = c_ref[...]
        ca = ca * jax.nn.sigmoid(ca)
        o_ref[...] = jnp.dot(ca, w_ref[...], precision=lax.Precision.HIGHEST,
                             preferred_element_type=f32) + b_ref[...]

    return pl.pallas_call(
        body, name="mod_fwd", grid=(DEPTH,), out_shape=SDS((DEPTH, NDEV, ADAS), f32),
        in_specs=[pl.BlockSpec((NDEV, D), lambda l: (0, 0)),
                  pl.BlockSpec((None, D, ADAS), lambda l: (l, 0, 0)),
                  pl.BlockSpec((None, 1, ADAS), lambda l: (l, 0, 0))],
        out_specs=pl.BlockSpec((None, NDEV, ADAS), lambda l: (l, 0, 0)),
        compiler_params=_cp(("arbitrary",), 32),
    )(c_all, ada_w, ada_b_loc)


def _ada_grad(cact_t, dmod_cols):
    def body(c_ref, d_ref, o_ref):
        acc = c_ref[:, 0:1] * d_ref[0:1, :]
        for b in range(1, NDEV):
            acc = acc + c_ref[:, b:b + 1] * d_ref[b:b + 1, :]
        o_ref[...] = acc

    tr = 256
    return pl.pallas_call(
        body, name="ada_grad", grid=(DEPTH, D // tr), out_shape=SDS((DEPTH, D, ADAS), f32),
        in_specs=[pl.BlockSpec((tr, NDEV), lambda l, i: (i, 0)),
                  pl.BlockSpec((None, NDEV, ADAS), lambda l, i: (l, 0, 0))],
        out_specs=pl.BlockSpec((None, tr, ADAS), lambda l, i: (l, i, 0)),
        compiler_params=_cp(("arbitrary", "arbitrary"), 32),
    )(cact_t, dmod_cols)


def _ffn_fwd(h, par, w3):
    tm, tf = 1024, 256
    nj = F // tf

    def body(h_ref, par_ref, wg_ref, wu_ref, wd_ref, ho_ref, fo_ref, g_ref, u_ref, a_ref, y_scr, acc):
        j = pl.program_id(1)

        @pl.when(j == 0)
        def _():
            y_scr[...] = _norm_mod(h_ref[...], par_ref[...])[0].astype(bf16)
            acc[...] = jnp.zeros_like(acc)

        y = y_scr[...]
        g = _nt(y, wg_ref[...])
        u = _nt(y, wu_ref[...])
        a = ((g * jax.nn.sigmoid(g)) * u).astype(bf16)
        g_ref[...] = g.astype(bf16)
        u_ref[...] = u.astype(bf16)
        a_ref[...] = a
        acc[...] += _nn(a, wd_ref[...])

        @pl.when(j == nj - 1)
        def _():
            fo_ref[...] = acc[...].astype(bf16)
            ho_ref[...] = h_ref[...] + (0.5 * par_ref[2:3, :]) * acc[...]

    row = pl.BlockSpec((tm, D), lambda i, j: (i, 0))
    hid = pl.BlockSpec((tm, tf), lambda i, j: (i, j))
    wspec = [pl.BlockSpec((None, tf, D), lambda i, j, k=k: (k, j, 0)) for k in range(3)]
    return pl.pallas_call(
        body, name="ffn_fwd", grid=(T // tm, nj),
        in_specs=[row, pl.BlockSpec((8, D), lambda i, j: (0, 0))] + wspec,
        out_specs=[row, row, hid, hid, hid],
        out_shape=[SDS((T, D), f32), SDS((T, D), bf16), SDS((T, F), bf16), SDS((T, F), bf16), SDS((T, F), bf16)],
        scratch_shapes=[pltpu.VMEM((tm, D), bf16), pltpu.VMEM((tm, D), f32)],
        compiler_params=_cp(("arbitrary", "arbitrary"), 52),
    )(h, par, w3, w3, w3)


def _ffn_bwd_tok(dh, h, par, fo, gs, us, w3):
    tm, tf = 512, 256
    nj = F // tf

    def body(dh_ref, h_ref, par_ref, fo_ref, g_ref, u_ref, wg_ref, wu_ref, wd_ref,
             dhin_ref, dg_ref, du_ref, y_ref, dfb_ref, dpar_ref, df_scr, dyacc):
        i, j = pl.program_id(0), pl.program_id(1)

        @pl.when(jnp.logical_and(i == 0, j == 0))
        def _():
            dpar_ref[...] = jnp.zeros_like(dpar_ref)

        @pl.when(j == 0)
        def _():
            dh_v, par_v = dh_ref[...], par_ref[...]
            dfb = ((0.5 * par_v[2:3, :]) * dh_v).astype(bf16)
            df_scr[...] = dfb
            dfb_ref[...] = dfb
            dpar_ref[2:3, :] += 0.5 * _colsum(dh_v * fo_ref[...].astype(f32))
            y_ref[...] = _norm_mod(h_ref[...], par_v)[0].astype(bf16)
            dyacc[...] = jnp.zeros_like(dyacc)

        da = _nt(df_scr[...], wd_ref[...])
        g = g_ref[...].astype(f32)
        u = u_ref[...].astype(f32)
        sig = jax.nn.sigmoid(g)
        du = (da * (g * sig)).astype(bf16)
        dg = (da * u * (sig * (1.0 + g * (1.0 - sig)))).astype(bf16)
        dg_ref[...] = dg
        du_ref[...] = du
        dyacc[...] += _nn(dg, wg_ref[...]) + _nn(du, wu_ref[...])

        @pl.when(j == nj - 1)
        def _():
            dx, dsh, dsc, dng = _norm_mod_bwd(dyacc[...], h_ref[...], par_ref[...])
            dhin_ref[...] = dh_ref[...] + dx
            dpar_ref[0:1, :] += dsh
            dpar_ref[1:2, :] += dsc
            dpar_ref[3:4, :] += dng

    row = pl.BlockSpec((tm, D), lambda i, j: (i, 0))
    hid = pl.BlockSpec((tm, tf), lambda i, j: (i, j))
    one = pl.BlockSpec((8, D), lambda i, j: (0, 0))
    wspec = [pl.BlockSpec((None, tf, D), lambda i, j, k=k: (k, j, 0)) for k in range(3)]
    return pl.pallas_call(
        body, name="ffn_bwd_tok", grid=(T // tm, nj),
        in_specs=[row, row, one, row, hid, hid] + wspec,
        out_specs=[row, hid, hid, row, row, one],
        out_shape=[SDS((T, D), f32), SDS((T, F), bf16), SDS((T, F), bf16), SDS((T, D), bf16),
                   SDS((T, D), bf16), SDS((8, D), f32)],
        scratch_shapes=[pltpu.VMEM((tm, D), bf16), pltpu.VMEM((tm, D), f32)],
        compiler_params=_cp(("arbitrary", "arbitrary"), 48),
    )(dh, h, par, fo, gs, us, w3, w3, w3)


def _ffn_bwd_w(a_s, dg_s, du_s, y, dfb):
    tf, tk = F // 2, 256
    nk = T // tk

    def body(a_ref, dg_ref, du_ref, y_ref, df_ref, o_ref, accg, accu, accd):
        kk = pl.program_id(1)

        @pl.when(kk == 0)
        def _():
            accg[...] = jnp.zeros_like(accg)
            accu[...] = jnp.zeros_like(accu)
            accd[...] = jnp.zeros_like(accd)

        yv = y_ref[...]
        accg[...] += _tn(dg_ref[...], yv)
        accu[...] += _tn(du_ref[...], yv)
        accd[...] += _tn(a_ref[...], df_ref[...])

        @pl.when(kk == nk - 1)
        def _():
            o_ref[0] = accg[...].astype(bf16)
            o_ref[1] = accu[...].astype(bf16)
            o_ref[2] = accd[...].astype(bf16)

    hid = pl.BlockSpec((tk, tf), lambda j, kk: (kk, j))
    row = pl.BlockSpec((tk, D), lambda j, kk: (kk, 0))
    return pl.pallas_call(
        body, name="ffn_bwd_w", grid=(F // tf, nk), in_specs=[hid, hid, hid, row, row],
        out_specs=pl.BlockSpec((3, tf, D), lambda j, kk: (0, j, 0)), out_shape=SDS((3, F, D), bf16),
        scratch_shapes=[pltpu.VMEM((tf, D), f32)] * 3,
        compiler_params=_cp(("arbitrary", "arbitrary"), 48),
    )(a_s, dg_s, du_s, y, dfb)


def _tn_matmul(a, b, bm, name):
    M, N = a.shape[1], b.shape[1]
    tk = 512
    nk = T // tk

    def body(a_ref, b_ref, o_ref, acc):
        kk = pl.program_id(1)

        @pl.when(kk == 0)
        def _():
            acc[...] = jnp.zeros_like(acc)

        acc[...] += _tn(a_ref[...], b_ref[...])

        @pl.when(kk == nk - 1)
        def _():
            o_ref[...] = acc[...].astype(bf16)

    return pl.pallas_call(
        body, name=name, grid=(M // bm, nk),
        in_specs=[pl.BlockSpec((tk, bm), lambda i, kk: (kk, i)), pl.BlockSpec((tk, N), lambda i, kk: (kk, 0))],
        out_specs=pl.BlockSpec((bm, N), lambda i, kk: (i, 0)), out_shape=SDS((M, N), bf16),
        scratch_shapes=[pltpu.VMEM((bm, N), f32)],
        compiler_params=_cp(("arbitrary", "arbitrary"), 40),
    )(a, b)


def _rope_tables():
    inv = ROPE_THETA ** (-jnp.arange(0, 64, 2, dtype=f32) / 64)
    ang = jnp.arange(T, dtype=f32)[:, None] * inv[None, :]
    ang = jnp.concatenate([ang, ang], axis=-1)
    cos, sin = jnp.tile(jnp.cos(ang), (1, 8)), jnp.tile(jnp.sin(ang), (1, 8))
    low = (jnp.arange(HW) % 64 < 32)[None, :]
    return cos, jnp.where(low, -sin, 0.0), jnp.where(low, 0.0, sin)


def _rope(t, cos, sin_lo, sin_hi):
    return t * cos + pltpu.roll(t, HW - 32, 1) * sin_lo + pltpu.roll(t, 32, 1) * sin_hi


def _rope_t(g, cos, sin_lo, sin_hi):
    return g * cos + pltpu.roll(g * sin_lo, 32, 1) + pltpu.roll(g * sin_hi, HW - 32, 1)


def _inproj_fwd(h, par, w_in_t, tabs):
    tm = 512

    def body(h_ref, par_ref, w_ref, cos_ref, slo_ref, shi_ref, ua_ref, va_ref, q_ref, k_ref, v_ref):
        y = _norm_mod(h_ref[...], par_ref[...])[0].astype(bf16)
        proj = lambda c: _nt(y, w_ref[c * HW:(c + 1) * HW, :])
        ua_ref[...] = proj(0)
        va_ref[...] = proj(1)
        cos, slo, shi = cos_ref[...], slo_ref[...], shi_ref[...]
        q_ref[...] = (_rope(proj(2), cos, slo, shi) * 0.125).astype(bf16)
        k_ref[...] = _rope(proj(3), cos, slo, shi).astype(bf16)
        v_ref[...] = proj(4).astype(bf16)

    row = pl.BlockSpec((tm, D), lambda i: (i, 0))
    half = pl.BlockSpec((tm, HW), lambda i: (i, 0))
    return pl.pallas_call(
        body, name="inproj_fwd", grid=(T // tm,),
        in_specs=[row, pl.BlockSpec((8, D), lambda i: (0, 0)), pl.BlockSpec((INW, D), lambda i: (0, 0)), half, half, half],
        out_specs=[half] * 5,
        out_shape=[SDS((T, HW), f32), SDS((T, HW), f32), SDS((T, HW), bf16), SDS((T, HW), bf16), SDS((T, HW), bf16)],
        compiler_params=_cp(("arbitrary",), 48),
    )(h, par, w_in_t, *tabs)


def _head_masks():
    lane = lax.broadcasted_iota(jnp.int32, (1, CH), 1)
    return lane < 64, lane >= 64


def _attn_fwd(q, k, v, d):
    L = T // d
    nb = L // CH

    def body(q_ref, kc_ref, kp_ref, vc_ref, vp_ref, o_ref, lse_ref):
        n = pl.program_id(1)
        row = lax.broadcasted_iota(jnp.int32, (CH, CH), 0)
        col = lax.broadcasted_iota(jnp.int32, (CH, CH), 1)
        mask_c = col <= row
        mask_p = jnp.logical_and(col >= row, n > 0)
        lse_tile = jnp.zeros((CH, CH), f32)
        for hp in range(4):
            sl = slice(hp * CH, (hp + 1) * CH)
            qv, kc, kp, vc, vp = q_ref[:, sl], kc_ref[:, sl], kp_ref[:, sl], vc_ref[:, sl], vp_ref[:, sl]
            o_pair = jnp.zeros((CH, CH), f32)
            for hh, hm in enumerate(_head_masks()):
                qm = jnp.where(hm, qv, jnp.zeros_like(qv))
                sc = jnp.where(mask_c, _nt(qm, kc), -jnp.inf)
                sp = jnp.where(mask_p, _nt(qm, kp), -jnp.inf)
                m = jnp.maximum(jnp.max(sc, axis=-1, keepdims=True), jnp.max(sp, axis=-1, keepdims=True))
                pc, pp = jnp.exp(sc - m), jnp.exp(sp - m)
                den = jnp.sum(pc, axis=-1, keepdims=True) + jnp.sum(pp, axis=-1, keepdims=True)
                acc = _nn(pc.astype(bf16), vc) + _nn(pp.astype(bf16), vp)
                o_pair = jnp.where(hm, acc / den, o_pair)
                lse_tile = jnp.where(col == 2 * hp + hh, m + jnp.log(den), lse_tile)
            o_ref[:, sl] = o_pair
        lse_ref[...] = lse_tile

    cur = pl.BlockSpec((CH, HW), lambda r, n: (n, r))
    prev = pl.BlockSpec((CH, HW), lambda r, n: (jnp.maximum(n - 1, 0), r))
    q2, k2, v2 = (t.reshape(L, d * HW) for t in (q, k, v))
    o, lse = pl.pallas_call(
        body, name="attn_fwd_d%d" % d, grid=(d, nb), in_specs=[cur, cur, prev, cur, prev],
        out_specs=[cur, pl.BlockSpec((CH, CH), lambda r, n: (n, r))],
        out_shape=[SDS((L, d * HW), f32), SDS((L, d * CH), f32)],
        compiler_params=_cp(("arbitrary", "arbitrary"), 32),
    )(q2, k2, k2, v2, v2)
    return o.reshape(T, HW), lse.reshape(T, CH)


def _attn_combine(os_, lses):
    tm = 512

    def body(o0_ref, o1_ref, o2_ref, l0_ref, l1_ref, l2_ref, o_ref, lse_ref):
        l0, l1, l2 = l0_ref[...], l1_ref[...], l2_ref[...]
        m = jnp.maximum(jnp.maximum(l0, l1), l2)
        e = [jnp.exp(l0 - m), jnp.exp(l1 - m), jnp.exp(l2 - m)]
        s = e[0] + e[1] + e[2]
        w = [ei / s for ei in e]
        lse_ref[...] = m + jnp.log(s)
        lo, _ = _head_masks()
        for hp in range(4):
            sl = slice(hp * CH, (hp + 1) * CH)
            acc = jnp.zeros((tm, CH), f32)
            for wp, op_ref in zip(w, (o0_ref, o1_ref, o2_ref)):
                wexp = jnp.where(lo, wp[:, 2 * hp:2 * hp + 1], wp[:, 2 * hp + 1:2 * hp + 2])
                acc = acc + wexp * op_ref[:, sl]
            o_ref[:, sl] = acc.astype(bf16)

    half = pl.BlockSpec((tm, HW), lambda i: (i, 0))
    stat = pl.BlockSpec((tm, CH), lambda i: (i, 0))
    return pl.pallas_call(
        body, name="attn_combine", grid=(T // tm,), in_specs=[half] * 3 + [stat] * 3, out_specs=[half, stat],
        out_shape=[SDS((T, HW), bf16), SDS((T, CH), f32)],
        compiler_params=_cp(("arbitrary",), 32),
    )(*os_, *lses)


def _attn_bwd(q, k, v, mixed, do, lse, d):
    L = T // d
    nb = L // CH

    def body(qc_ref, qn_ref, kc_ref, kp_ref, vc_ref, vp_ref, oc_ref, on_ref, dc_ref, dn_ref, lc_ref, ln_ref,
             dq_ref, dk_ref, dv_ref):
        n = pl.program_id(1)
        row = lax.broadcasted_iota(jnp.int32, (CH, CH), 0)
        col = lax.broadcasted_iota(jnp.int32, (CH, CH), 1)
        mask_c = col <= row
        mask_p = jnp.logical_and(col >= row, n > 0)
        mask_n = jnp.logical_and(col >= row, n < nb - 1)
        lse_c, lse_n = lc_ref[...], ln_ref[...]
        for hp in range(4):
            sl = slice(hp * CH, (hp + 1) * CH)
            qc, qn, kc, kp, vc, vp = qc_ref[:, sl], qn_ref[:, sl], kc_ref[:, sl], kp_ref[:, sl], vc_ref[:, sl], vp_ref[:, sl]
            doc, don = dc_ref[:, sl], dn_ref[:, sl]
            prod_c = doc.astype(f32) * oc_ref[:, sl].astype(f32)
            prod_n = don.astype(f32) * on_ref[:, sl].astype(f32)
            dq_pair = jnp.zeros((CH, CH), f32)
            dk_pair = jnp.zeros((CH, CH), f32)
            dv_pair = jnp.zeros((CH, CH), f32)
            for hh, hm in enumerate(_head_masks()):
                h_idx = 2 * hp + hh
                zero = jnp.zeros_like(qc)
                qcm, qnm = jnp.where(hm, qc, zero), jnp.where(hm, qn, zero)
                dcm, dnm = jnp.where(hm, doc, zero), jnp.where(hm, don, zero)
                delta_c = jnp.sum(jnp.where(hm, prod_c, 0.0), axis=-1, keepdims=True)
                delta_n = jnp.sum(jnp.where(hm, prod_n, 0.0), axis=-1, keepdims=True)
                lc, ln = lse_c[:, h_idx:h_idx + 1], lse_n[:, h_idx:h_idx + 1]
                p = jnp.where(mask_c, jnp.exp(_nt(qcm, kc) - lc), 0.0)
                ds = (p * (_nt(dcm, vc) - delta_c)).astype(bf16)
                dq_h = _nn(ds, kc)
                dk_h = _tn(ds, qcm)
                dv_h = _tn(p.astype(bf16), dcm)
                p = jnp.where(mask_p, jnp.exp(_nt(qcm, kp) - lc), 0.0)
                ds = (p * (_nt(dcm, vp) - delta_c)).astype(bf16)
                dq_h = dq_h + _nn(ds, kp)
                p = jnp.where(mask_n, jnp.exp(_nt(qnm, kc) - ln), 0.0)
                ds = (p * (_nt(dnm, vc) - delta_n)).astype(bf16)
                dk_h = dk_h + _tn(ds, qnm)
                dv_h = dv_h + _tn(p.astype(bf16), dnm)
                dq_pair = jnp.where(hm, dq_h, dq_pair)
                dk_pair = dk_pair + dk_h
                dv_pair = dv_pair + dv_h
            dq_ref[:, sl] = dq_pair
            dk_ref[:, sl] = dk_pair
            dv_ref[:, sl] = dv_pair

    cur = pl.BlockSpec((CH, HW), lambda r, n: (n, r))
    prev = pl.BlockSpec((CH, HW), lambda r, n: (jnp.maximum(n - 1, 0), r))
    nxt = pl.BlockSpec((CH, HW), lambda r, n: (jnp.minimum(n + 1, nb - 1), r))
    ocur = pl.BlockSpec((CH, HW), lambda r, n: (n, 2 * r + 1))
    onxt = pl.BlockSpec((CH, HW), lambda r, n: (jnp.minimum(n + 1, nb - 1), 2 * r + 1))
    scur = pl.BlockSpec((CH, CH), lambda r, n: (n, r))
    snxt = pl.BlockSpec((CH, CH), lambda r, n: (jnp.minimum(n + 1, nb - 1), r))
    q2, k2, v2, do2 = (t.reshape(L, d * HW) for t in (q, k, v, do))
    m2, l2 = mixed.reshape(L, d * 2 * HW), lse.reshape(L, d * CH)
    outs = pl.pallas_call(
        body, name="attn_bwd_d%d" % d, grid=(d, nb),
        in_specs=[cur, nxt, cur, prev, cur, prev, ocur, onxt, cur, nxt, scur, snxt],
        out_specs=[cur] * 3, out_shape=[SDS((L, d * HW), f32)] * 3,
        compiler_params=_cp(("arbitrary", "arbitrary"), 32),
    )(q2, q2, k2, k2, v2, v2, m2, m2, do2, do2, l2, l2)
    return [t.reshape(T, HW) for t in outs]


def _sgu_fwd(ua, va, o, lng, lnb, ws, bs_t):
    tm = 512

    def body(ua_ref, va_ref, o_ref, lg_ref, lb_ref, ws_ref, bs_ref, mix_ref):
        for hd in range(4):
            sl = slice(hd * CH, (hd + 1) * CH)
            w = ws_ref[hd]
            for cc in range(tm // CH):
                rs = slice(cc * CH, (cc + 1) * CH)
                u = _gelu(ua_ref[rs, sl])
                v = _gelu(va_ref[rs, sl])
                vc = v - _rowmean(v)
                vn = vc * lax.rsqrt(_rowmean(vc * vc) + EPS) * lg_ref[:, sl] + lb_ref[:, sl]
                z = _nn(w, vn.astype(bf16)) + bs_ref[:, hd:hd + 1]
                mix_ref[rs, sl] = (u * z).astype(bf16)
        mix_ref[:, HW:] = o_ref[...]

    half = pl.BlockSpec((tm, HW), lambda i: (i, 0))
    vec = pl.BlockSpec((1, HW), lambda i: (0, 0))
    return pl.pallas_call(
        body, name="sgu_fwd", grid=(T // tm,),
        in_specs=[half, half, half, vec, vec, pl.BlockSpec((4, CH, CH), lambda i: (0, 0, 0)),
                  pl.BlockSpec((CH, 4), lambda i: (0, 0))],
        out_specs=pl.BlockSpec((tm, 2 * HW), lambda i: (i, 0)), out_shape=SDS((T, 2 * HW), bf16),
        compiler_params=_cp(("arbitrary",), 32),
    )(ua, va, o, lng, lnb, ws, bs_t)


def _sgu_bwd(ua, va, d_a, lng, lnb, ws, ws_t, bs_t):
    tm = 512

    def body(ua_ref, va_ref, d_ref, lg_ref, lb_ref, ws_ref, wst_ref, bs_ref, dsg_ref, dln_ref, dws_ref, db_ref):
        @pl.when(pl.program_id(0) == 0)
        def _():
            dln_ref[...] = jnp.zeros_like(dln_ref)
            dws_ref[...] = jnp.zeros_like(dws_ref)
            db_ref[...] = jnp.zeros_like(db_ref)

        for hd in range(4):
            sl = slice(hd * CH, (hd + 1) * CH)
            w, wt = ws_ref[hd], wst_ref[hd]
            lg = lg_ref[:, sl]
            for cc in range(tm // CH):
                rs = slice(cc * CH, (cc + 1) * CH)
                xa, xv, dd = ua_ref[rs, sl], va_ref[rs, sl], d_ref[rs, sl]
                u, v = _gelu(xa), _gelu(xv)
                vc = v - _rowmean(v)
                rstd = lax.rsqrt(_rowmean(vc * vc) + EPS)
                xh = vc * rstd
                vnb = (xh * lg + lb_ref[:, sl]).astype(bf16)
                z = _nn(w, vnb) + bs_ref[:, hd:hd + 1]
                dz = dd * u
                dzb = dz.astype(bf16)
                dsg_ref[rs, sl] = (dd * z * _gelu_grad(xa)).astype(bf16)
                dws_ref[hd] += _nt(dzb, vnb)
                db_ref[hd] += dz
                dvn = _nn(wt, dzb)
                dln_ref[0:1, sl] += _colsum(dvn * xh)
                dln_ref[1:2, sl] += _colsum(dvn)
                dxh = dvn * lg
                dv = rstd * (dxh - _rowmean(dxh) - xh * _rowmean(dxh * xh))
                dsg_ref[rs, HW + hd * CH:HW + (hd + 1) * CH] = (dv * _gelu_grad(xv)).astype(bf16)

    half = pl.BlockSpec((tm, HW), lambda i: (i, 0))
    vec = pl.BlockSpec((1, HW), lambda i: (0, 0))
    mat = pl.BlockSpec((4, CH, CH), lambda i: (0, 0, 0))
    return pl.pallas_call(
        body, name="sgu_bwd", grid=(T // tm,),
        in_specs=[half, half, half, vec, vec, mat, mat, pl.BlockSpec((CH, 4), lambda i: (0, 0))],
        out_specs=[pl.BlockSpec((tm, 2 * HW), lambda i: (i, 0)), pl.BlockSpec((8, HW), lambda i: (0, 0)), mat, mat],
        out_shape=[SDS((T, 2 * HW), bf16), SDS((8, HW), f32), SDS((4, CH, CH), f32), SDS((4, CH, CH), f32)],
        compiler_params=_cp(("arbitrary",), 32),
    )(ua, va, d_a, lng, lnb, ws, ws_t, bs_t)


def _outproj_fwd(mixed, w_out, h, par):
    tm = 512

    def body(mix_ref, w_ref, h_ref, par_ref, ho_ref, po_ref):
        p = _nn(mix_ref[...], w_ref[...])
        po_ref[...] = p.astype(bf16)
        ho_ref[...] = h_ref[...] + par_ref[2:3, :] * p

    row = pl.BlockSpec((tm, D), lambda i: (i, 0))
    return pl.pallas_call(
        body, name="outproj_fwd", grid=(T // tm,),
        in_specs=[row, pl.BlockSpec((D, D), lambda i: (0, 0)), row, pl.BlockSpec((8, D), lambda i: (0, 0))],
        out_specs=[row, row], out_shape=[SDS((T, D), f32), SDS((T, D), bf16)],
        compiler_params=_cp(("arbitrary",), 32),
    )(mixed, w_out, h, par)


def _outproj_bwd(dh, po, w_out, par):
    tm = 512

    def body(dh_ref, po_ref, w_ref, par_ref, do_ref, da_ref, db_ref, dpar_ref):
        @pl.when(pl.program_id(0) == 0)
        def _():
            dpar_ref[...] = jnp.zeros_like(dpar_ref)

        dh_v = dh_ref[...]
        dob = (par_ref[2:3, :] * dh_v).astype(bf16)
        do_ref[...] = dob
        dpar_ref[2:3, :] += _colsum(dh_v * po_ref[...].astype(f32))
        dm = _nt(dob, w_ref[...])
        da_ref[...] = dm[:, :HW]
        db_ref[...] = dm[:, HW:].astype(bf16)

    row = pl.BlockSpec((tm, D), lambda i: (i, 0))
    half = pl.BlockSpec((tm, HW), lambda i: (i, 0))
    one = pl.BlockSpec((8, D), lambda i: (0, 0))
    return pl.pallas_call(
        body, name="outproj_bwd", grid=(T // tm,),
        in_specs=[row, row, pl.BlockSpec((D, D), lambda i: (0, 0)), one],
        out_specs=[row, half, half, one],
        out_shape=[SDS((T, D), bf16), SDS((T, HW), f32), SDS((T, HW), bf16), SDS((8, D), f32)],
        compiler_params=_cp(("arbitrary",), 32),
    )(dh, po, w_out, par)


def _inproj_bwd_tok(dh, h, par, w_in_t, dsg, dqs, dks, dvs, tabs):
    tm = 256

    def body(dh_ref, h_ref, par_ref, w_ref, dsg_ref, dq0, dq1, dq2, dk0, dk1, dk2, dv0, dv1, dv2,
             cos_ref, slo_ref, shi_ref, dhin_ref, dp_ref, y_ref, dpar_ref):
        @pl.when(pl.program_id(0) == 0)
        def _():
            dpar_ref[...] = jnp.zeros_like(dpar_ref)

        cos, slo, shi = cos_ref[...], slo_ref[...], shi_ref[...]
        dq = (dq0[...] + dq1[...] + dq2[...]) * 0.125
        dk = dk0[...] + dk1[...] + dk2[...]
        dp_ref[:, :2 * HW] = dsg_ref[...]
        dp_ref[:, 2 * HW:3 * HW] = _rope_t(dq, cos, slo, shi).astype(bf16)
        dp_ref[:, 3 * HW:4 * HW] = _rope_t(dk, cos, slo, shi).astype(bf16)
        dp_ref[:, 4 * HW:] = (dv0[...] + dv1[...] + dv2[...]).astype(bf16)
        dy = _nn(dp_ref[...], w_ref[...])
        par_v, h_v = par_ref[...], h_ref[...]
        y_ref[...] = _norm_mod(h_v, par_v)[0].astype(bf16)
        dx, dsh, dsc, dng = _norm_mod_bwd(dy, h_v, par_v)
        dhin_ref[...] = dh_ref[...] + dx
        dpar_ref[0:1, :] += dsh
        dpar_ref[1:2, :] += dsc
        dpar_ref[3:4, :] += dng

    row = pl.BlockSpec((tm, D), lambda i: (i, 0))
    half = pl.BlockSpec((tm, HW), lambda i: (i, 0))
    one = pl.BlockSpec((8, D), lambda i: (0, 0))
    return pl.pallas_call(
        body, name="inproj_bwd_tok", grid=(T // tm,),
        in_specs=[row, row, one, pl.BlockSpec((INW, D), lambda i: (0, 0)), row] + [half] * 12,
        out_specs=[row, pl.BlockSpec((tm, INW), lambda i: (i, 0)), row, one],
        out_shape=[SDS((T, D), f32), SDS((T, INW), bf16), SDS((T, D), bf16), SDS((8, D), f32)],
        compiler_params=_cp(("arbitrary",), 48),
    )(dh, h, par, w_in_t, dsg, *dqs, *dks, *dvs, *tabs)


def _loss_head(h, par, target):
    tm = 512

    def body(h_ref, par_ref, t_ref, dh_ref, acc_ref):
        @pl.when(pl.program_id(0) == 0)
        def _():
            acc_ref[...] = jnp.zeros_like(acc_ref)

        x, g = h_ref[...], par_ref[3:4, :]
        rstd = lax.rsqrt(_rowmean(x * x) + EPS)
        xhat = x * rstd
        err = xhat * g - t_ref[...]
        acc_ref[0:1, :] += _colsum(err * err)
        dy = err * (1.0 / D)
        acc_ref[3:4, :] += _colsum(dy * xhat)
        dxhat = dy * g
        dh_ref[...] = rstd * (dxhat - xhat * _rowmean(dxhat * xhat))

    row = pl.BlockSpec((tm, D), lambda i: (i, 0))
    one = pl.BlockSpec((8, D), lambda i: (0, 0))
    return pl.pallas_call(
        body, name="loss_head", grid=(T // tm,), in_specs=[row, one, row], out_specs=[row, one],
        out_shape=[SDS((T, D), f32), SDS((8, D), f32)], compiler_params=_cp(("arbitrary",), 32),
    )(h, par, target)


ELEMENTWISE_VMEM_BUDGET = 20 << 20


def _block_rows(rows, bytes_per_row):
    cap = ELEMENTWISE_VMEM_BUDGET // bytes_per_row
    if rows <= cap:
        return rows
    return next(b for b in range(cap - cap % 16, 0, -16) if rows % b == 0)


def _sum_slots(land, name):
    _, R, C = land.shape
    br = _block_rows(R, 2 * NDEV * C * land.dtype.itemsize + 2 * C * 4)

    def body(l_ref, o_ref):
        acc = l_ref[0].astype(f32)
        for s in range(1, NDEV):
            acc = acc + l_ref[s].astype(f32)
        o_ref[...] = acc

    return pl.pallas_call(
        body, name=name, grid=(R // br,), in_specs=[pl.BlockSpec((NDEV, br, C), lambda i: (0, i, 0))],
        out_specs=pl.BlockSpec((br, C), lambda i: (i, 0)), out_shape=SDS((R, C), f32),
        compiler_params=_cp(("arbitrary",), 32),
    )(land)


def _adamw(w, g, m, v, name):
    R, C = w.shape
    br = _block_rows(R, 2 * 7 * C * 4)

    def body(w_ref, g_ref, m_ref, v_ref, d_ref, mo_ref, vo_ref):
        gv = g_ref[...]
        m2 = B1 * m_ref[...] + (1.0 - B1) * gv
        v2 = B2 * v_ref[...] + (1.0 - B2) * (gv * gv)
        mo_ref[...] = m2
        vo_ref[...] = v2
        m_hat = m2 / (1.0 - B1 ** STEP)
        v_hat = v2 / (1.0 - B2 ** STEP)
        d_ref[...] = -LR * (m_hat / (jnp.sqrt(v_hat) + AEPS) + WD * w_ref[...])

    blk = pl.BlockSpec((br, C), lambda i: (i, 0))
    return pl.pallas_call(
        body, name=name, grid=(R // br,), in_specs=[blk] * 4, out_specs=[blk] * 3,
        out_shape=[SDS((R, C), f32)] * 3, compiler_params=_cp(("arbitrary",), 32),
    )(w, g, m, v)


def _adamw_nd(w, g, m, v, name):
    shp = w.shape
    r2 = (-1, shp[-1]) if w.ndim > 1 else (8, shp[0] // 8)
    outs = _adamw(w.reshape(r2), g.reshape(r2), m.reshape(r2), v.reshape(r2), name)
    return [o.reshape(shp) for o in outs]


def _par_rows(mod_l, s, gain):
    rows = jnp.pad(mod_l.reshape(9, D)[3 * s:3 * s + 3], ((0, 5), (0, 0)))
    return rows + jnp.pad(gain[None, :], ((3, 4), (0, 0)))


def _pad_rows(a):
    a = a.reshape(-1, CH)
    return jnp.pad(a, ((0, (-a.shape[0]) % 8), (0, 0)))


def _prepare(c, ada_w, ada_b, norm_g, ffn1_wg, ffn1_wu, ffn1_wd, ffn2_wg, ffn2_wu, ffn2_wd, w_in, w_out):
    me = 4 * lax.axis_index("x") + 2 * lax.axis_index("y") + lax.axis_index("c")

    pay = jnp.pad(c, ((0, 7), (0, 0)))
    pay = jnp.concatenate([pay, jnp.pad(norm_g.reshape(6, OUTS), ((0, 2), (0, D - OUTS)))], axis=0)
    got = _all_gather_small(pay, "gather_c")
    c_all = got[:, 0, :]
    gains = got[:, 8:14, :OUTS].transpose(1, 0, 2).reshape(DEPTH, 3, D)

    ada_b_loc = lax.dynamic_slice(ada_b, (0, me * ADAS), (DEPTH, ADAS)).reshape(DEPTH, 1, ADAS)
    mod_cols = _mod_fwd(c_all, ada_w, ada_b_loc)
    got = _all_gather_small(mod_cols.reshape(DEPTH * NDEV, ADAS), "gather_mod").reshape(NDEV, DEPTH, NDEV, ADAS)
    mod = lax.dynamic_index_in_dim(got, me, axis=2, keepdims=False).transpose(1, 0, 2).reshape(DEPTH, 9 * D)
    pars = [[_par_rows(mod[l], s, gains[l, s]) for s in range(3)] for l in range(DEPTH)]

    tr = lambda w: jnp.swapaxes(w, -1, -2).astype(bf16)
    locs = []
    for l in range(DEPTH):
        locs.append(jnp.stack([tr(ffn1_wg[l]), tr(ffn1_wu[l]), ffn1_wd[l].astype(bf16)]))
        locs.append(jnp.stack([tr(ffn2_wg[l]), tr(ffn2_wu[l]), ffn2_wd[l].astype(bf16)]))
        locs.append(tr(w_in[l]))
        locs.append(w_out[l].astype(bf16))
    full = _all_gather_rows(locs, "gather_weights")
    wts = [full[4 * l:4 * l + 4] for l in range(DEPTH)]

    return me, c_all, pars, wts


def _fwd_bwd(x2, target, pars, wts, sgu_ln_g, sgu_ln_b, sgu_w, sgu_b, final_g):
    tabs = _rope_tables()
    tril = jnp.tril(jnp.ones((CH, CH), dtype=bool))
    ws_m = jnp.where(tril[None, None], sgu_w, 0.0).astype(bf16)
    ws_mt = jnp.swapaxes(ws_m, -1, -2)

    h = x2
    saved = []
    for l in range(DEPTH):
        w_f1, w_f2, w_i, w_o = wts[l]
        lng, lnb = sgu_ln_g[l].reshape(1, HW), sgu_ln_b[l].reshape(1, HW)
        bs_t = sgu_b[l].T
        h0 = h
        h1, fo1, g1, u1, a1 = _ffn_fwd(h0, pars[l][0], w_f1)
        ua, va, q, k, v = _inproj_fwd(h1, pars[l][1], w_i, tabs)
        branches = [_attn_fwd(q, k, v, d) for d in PATTERN_DILATIONS]
        o, lse = _attn_combine([b[0] for b in branches], [b[1] for b in branches])
        mixed = _sgu_fwd(ua, va, o, lng, lnb, ws_m[l], bs_t)
        h2, po = _outproj_fwd(mixed, w_o, h1, pars[l][1])
        h3, fo2, g2, u2, a2 = _ffn_fwd(h2, pars[l][2], w_f2)
        saved.append((h0, h1, h2, fo1, g1, u1, a1, ua, va, q, k, v, mixed, lse, po, fo2, g2, u2, a2))
        h = h3

    par_f = jnp.pad(final_g[None, :], ((3, 4), (0, 0)))
    dh, head = _loss_head(h, par_f, target)

    grads = [None] * (4 * DEPTH)
    dmods, dgains, dsgu = [None] * DEPTH, [None] * DEPTH, [None] * DEPTH
    for l in reversed(range(DEPTH)):
        w_f1, w_f2, w_i, w_o = wts[l]
        h0, h1, h2, fo1, g1, u1, a1, ua, va, q, k, v, mixed, lse, po, fo2, g2, u2, a2 = saved[l]
        lng, lnb = sgu_ln_g[l].reshape(1, HW), sgu_ln_b[l].reshape(1, HW)
        bs_t = sgu_b[l].T

        dh, dg_s, du_s, y, dfb, dpar3 = _ffn_bwd_tok(dh, h2, pars[l][2], fo2, g2, u2, w_f2)
        grads[4 * l + 1] = _ffn_bwd_w(a2, dg_s, du_s, y, dfb)

        dob, d_a, d_b, dpar2g = _outproj_bwd(dh, po, w_o, pars[l][1])
        grads[4 * l + 3] = _tn_matmul(mixed, dob, 512, "w_out_grad")
        parts = [_attn_bwd(q, k, v, mixed, d_b, lse, d) for d in PATTERN_DILATIONS]
        dsg, dln, dws, dbl = _sgu_bwd(ua, va, d_a, lng, lnb, ws_m[l], ws_mt[l], bs_t)
        dh, dp, y2, dpar2 = _inproj_bwd_tok(dh, h1, pars[l][1], w_i, dsg, [p[0] for p in parts],
                                            [p[1] for p in parts], [p[2] for p in parts], tabs)
        grads[4 * l + 2] = _tn_matmul(dp, y2, 640, "w_in_grad")

        dh, dg_s, du_s, y, dfb, dpar1 = _ffn_bwd_tok(dh, h0, pars[l][0], fo1, g1, u1, w_f1)
        grads[4 * l] = _ffn_bwd_w(a1, dg_s, du_s, y, dfb)

        dmods[l] = jnp.concatenate([dpar1[0:3], dpar2[0:2], dpar2g[2:3], dpar3[0:3]], axis=0).reshape(9 * D)
        dgains[l] = jnp.stack([dpar1[3], dpar2[3], dpar3[3]])
        dsgu[l] = (dln[0], dln[1], jnp.where(tril[None], dws, 0.0), jnp.sum(dbl, axis=-1))
    return dh, head, grads, dmods, dgains, dsgu


def kernel(x, c, ada_w, ada_b, norm_g, ffn1_wg, ffn1_wu, ffn1_wd, ffn2_wg, ffn2_wu, ffn2_wd, w_in, sgu_ln_g, sgu_ln_b, sgu_w, sgu_b, w_out, final_g, loss_target, m_ada_w, m_ada_b, m_norm_g, m_ffn1_wg, m_ffn1_wu, m_ffn1_wd, m_ffn2_wg, m_ffn2_wu, m_ffn2_wd, m_w_in, m_sgu_ln_g, m_sgu_ln_b, m_sgu_w, m_sgu_b, m_w_out, m_final_g, v_ada_w, v_ada_b, v_norm_g, v_ffn1_wg, v_ffn1_wu, v_ffn1_wd, v_ffn2_wg, v_ffn2_wu, v_ffn2_wd, v_w_in, v_sgu_ln_g, v_sgu_ln_b, v_sgu_w, v_sgu_b, v_w_out, v_final_g):
    me, c_all, pars, wts = _prepare(c, ada_w, ada_b, norm_g, ffn1_wg, ffn1_wu, ffn1_wd, ffn2_wg, ffn2_wu, ffn2_wd, w_in,
                                    w_out)
    dh, head, grads, dmods, dgains, dsgu = _fwd_bwd(x[0], loss_target[0], pars, wts, sgu_ln_g, sgu_ln_b, sgu_w, sgu_b,
                                                    final_g)
    grad_x = dh[None]

    loss_part = 0.5 * jnp.sum(head[0]) / D
    small = jnp.concatenate([
        _pad_rows(jnp.stack(dmods)), _pad_rows(jnp.stack(dgains)),
        _pad_rows(jnp.stack([s[0] for s in dsgu])), _pad_rows(jnp.stack([s[1] for s in dsgu])),
        _pad_rows(jnp.stack([s[3] for s in dsgu])), _pad_rows(jnp.stack([s[2] for s in dsgu])),
        _pad_rows(head[3]), _pad_rows(jnp.pad(loss_part[None], (0, CH - 1)))], axis=0)
    got = _all_gather_small(small, "gather_small")
    tot = _sum_slots(got, "sum_small")
    n_mod, n_gain, n_sw = DEPTH * 9 * D // CH, DEPTH * 3 * D // CH, DEPTH * 4 * CH
    offs = [0, n_mod, n_mod + n_gain, n_mod + n_gain + 8, n_mod + n_gain + 16, n_mod + n_gain + 24]
    g_ada_b = tot[offs[0]:offs[1]].reshape(DEPTH, 9 * D)
    g_gain_full = tot[offs[1]:offs[2]].reshape(DEPTH, 3, D)
    g_ln_g = tot[offs[2]:offs[3]].reshape(DEPTH, 4, CH)
    g_ln_b = tot[offs[3]:offs[4]].reshape(DEPTH, 4, CH)
    g_sb = tot[offs[4]:offs[5]].reshape(DEPTH, 4, CH)
    g_sw = tot[offs[5]:offs[5] + n_sw].reshape(DEPTH, 4, CH, CH)
    g_final = tot[offs[5] + n_sw:offs[5] + n_sw + 8].reshape(D)
    loss = tot[offs[5] + n_sw + 8, 0]
    g_norm = lax.dynamic_slice(g_gain_full, (0, 0, me * OUTS), (DEPTH, 3, OUTS))

    dmod_all = got[:, offs[0]:offs[1]].reshape(NDEV, DEPTH, 9 * D)
    dmod_cols = lax.dynamic_slice(dmod_all, (0, 0, me * ADAS), (NDEV, DEPTH, ADAS)).transpose(1, 0, 2)
    g_ada_w = _ada_grad((c_all * jax.nn.sigmoid(c_all)).T, dmod_cols)

    land = _scatter_rows(grads, "scatter_grads")
    sums = [_sum_slots(a.reshape(NDEV, -1, D), "sum_grads") for a in land]
    back = lambda t: jnp.swapaxes(t, -1, -2)
    f1 = jnp.stack([sums[4 * l].reshape(3, FS, D) for l in range(DEPTH)])
    f2 = jnp.stack([sums[4 * l + 1].reshape(3, FS, D) for l in range(DEPTH)])
    g_w_in = back(jnp.stack([sums[4 * l + 2] for l in range(DEPTH)]))
    g_w_out = jnp.stack([sums[4 * l + 3] for l in range(DEPTH)])

    gw = [g_ada_w, g_ada_b, g_norm, back(f1[:, 0]), back(f1[:, 1]), f1[:, 2], back(f2[:, 0]), back(f2[:, 1]), f2[:, 2],
          g_w_in, g_ln_g, g_ln_b, g_sw, g_sb, g_w_out, g_final]
    ws = [ada_w, ada_b, norm_g, ffn1_wg, ffn1_wu, ffn1_wd, ffn2_wg, ffn2_wu, ffn2_wd, w_in, sgu_ln_g, sgu_ln_b, sgu_w,
          sgu_b, w_out, final_g]
    ms = [m_ada_w, m_ada_b, m_norm_g, m_ffn1_wg, m_ffn1_wu, m_ffn1_wd, m_ffn2_wg, m_ffn2_wu, m_ffn2_wd, m_w_in,
          m_sgu_ln_g, m_sgu_ln_b, m_sgu_w, m_sgu_b, m_w_out, m_final_g]
    vs = [v_ada_w, v_ada_b, v_norm_g, v_ffn1_wg, v_ffn1_wu, v_ffn1_wd, v_ffn2_wg, v_ffn2_wu, v_ffn2_wd, v_w_in,
          v_sgu_ln_g, v_sgu_ln_b, v_sgu_w, v_sgu_b, v_w_out, v_final_g]
    upd = [_adamw_nd(w, g, m, v, "adamw") for w, g, m, v in zip(ws, gw, ms, vs)]
    return (loss, grad_x, *gw, *[u[0] for u in upd], *[u[1] for u in upd], *[u[2] for u in upd])
```

```python
import functools
import math

import jax
import jax.numpy as jnp
from jax import lax
from jax.experimental import pallas as pl
from jax.experimental.pallas import tpu as pltpu

f32, bf16 = jnp.float32, jnp.bfloat16
SDS = jax.ShapeDtypeStruct

T, D, F = 4096, 1024, 2816
NDEV, DEPTH = 8, 2
HW = 512
INW = 5 * HW
FS, INS, OUTS, ADAS = F // NDEV, INW // NDEV, D // NDEV, 9 * D // NDEV
CH = 128
PATTERN_DILATIONS = (1, 4, 16)
ROPE_THETA = 10000.0
EPS = 1e-6
LR, B1, B2, AEPS, WD, STEP = 0.001, 0.9, 0.999, 1e-08, 0.01, 10
MESH = pl.DeviceIdType.MESH


def _cp(sems, vmem_mb):
    return pltpu.CompilerParams(dimension_semantics=sems, vmem_limit_bytes=vmem_mb << 20)


def _nn(a, b):
    return lax.dot_general(a, b, (((1,), (0,)), ((), ())), preferred_element_type=f32)


def _nt(a, b):
    return lax.dot_general(a, b, (((1,), (1,)), ((), ())), preferred_element_type=f32)


def _tn(a, b):
    return lax.dot_general(a, b, (((0,), (0,)), ((), ())), preferred_element_type=f32)


def _colsum(a):
    return jnp.sum(a, axis=0, keepdims=True)


def _rowmean(a):
    return jnp.mean(a, axis=-1, keepdims=True)


def _norm_mod(x, par):
    rstd = lax.rsqrt(_rowmean(x * x) + EPS)
    xhat = x * rstd
    n = xhat * par[3:4, :]
    y = n * (1.0 + par[1:2, :]) + par[0:1, :]
    return y, n, xhat, rstd


def _norm_mod_bwd(dy, x, par):
    _, n, xhat, rstd = _norm_mod(x, par)
    dn = dy * (1.0 + par[1:2, :])
    dxhat = dn * par[3:4, :]
    dx = rstd * (dxhat - xhat * _rowmean(dxhat * xhat))
    return dx, _colsum(dy), _colsum(dy * n), _colsum(dn * xhat)


_GK = math.sqrt(2.0 / math.pi)


def _gelu(x):
    return 0.5 * x * (1.0 + jnp.tanh(_GK * (x + 0.044715 * x * x * x)))


def _gelu_grad(x):
    t = jnp.tanh(_GK * (x + 0.044715 * x * x * x))
    return 0.5 * (1.0 + t) + 0.5 * x * (1.0 - t * t) * (_GK * (1.0 + 3.0 * 0.044715 * x * x))


def _rows(ref, idx, r):
    if len(ref.shape) == 3:
        return ref.at[:, pl.ds(idx * r, r), :]
    return ref.at[pl.ds(idx * r, r), :]


def _flip(v, bit):
    return 1 - v if bit else v


def _all_gather_small(x, name):
    R, C = x.shape

    def body(x_ref, out_ref, send_sems, recv_sems):
        mx, my, mc = lax.axis_index("x"), lax.axis_index("y"), lax.axis_index("c")
        me = 4 * mx + 2 * my + mc
        out_ref[me] = x_ref[...]
        sent = []
        for k in range(1, NDEV):
            peer = (_flip(mx, k & 4), _flip(my, k & 2), _flip(mc, k & 1))
            cp = pltpu.make_async_remote_copy(
                src_ref=x_ref, dst_ref=out_ref.at[me], send_sem=send_sems.at[k - 1],
                recv_sem=recv_sems.at[k - 1], device_id=peer, device_id_type=MESH)
            cp.start()
            sent.append(cp)
        for k in range(1, NDEV):
            peer = (_flip(mx, k & 4), _flip(my, k & 2), _flip(mc, k & 1))
            pidx = 4 * peer[0] + 2 * peer[1] + peer[2]
            pltpu.make_async_remote_copy(
                src_ref=x_ref, dst_ref=out_ref.at[pidx], send_sem=send_sems.at[k - 1],
                recv_sem=recv_sems.at[k - 1], device_id=peer, device_id_type=MESH).wait_recv()
        for cp in sent:
            cp.wait_send()

    vm = pl.BlockSpec(memory_space=pltpu.VMEM)
    return pl.pallas_call(
        body, name=name, out_shape=SDS((NDEV, R, C), f32), in_specs=[vm], out_specs=vm,
        scratch_shapes=[pltpu.SemaphoreType.DMA((NDEV - 1,)), pltpu.SemaphoreType.DMA((NDEV - 1,))],
        compiler_params=pltpu.CompilerParams(vmem_limit_bytes=32 << 20),
    )(x)


def _all_gather_rows(locs, name):
    n = len(locs)
    rs = [a.shape[-2] for a in locs]

    def body(*refs):
        src, out = refs[:n], refs[n:2 * n]
        send_sems, recv_sems, loc_sems = refs[2 * n:]
        mx, my, mc = lax.axis_index("x"), lax.axis_index("y"), lax.axis_index("c")
        me, sib = (mx, my, mc), (mx, my, 1 - mc)
        chips = [(1 - mx, my), (mx, 1 - my), (1 - mx, 1 - my)]

        def blk(a, p):
            return _rows(out[a], 4 * p[0] + 2 * p[1] + p[2], rs[a])

        def copy(k, a, block, to, from_src=False):
            return pltpu.make_async_remote_copy(
                src_ref=src[a] if from_src else blk(a, block), dst_ref=blk(a, block),
                send_sem=send_sems.at[k * n + a], recv_sem=recv_sems.at[k * n + a],
                device_id=to, device_id_type=MESH)

        mine = [pltpu.make_async_copy(src[a], blk(a, me), loc_sems.at[a]) for a in range(n)]
        for m in mine:
            m.start()
        first = []
        for j, chip in enumerate(chips):
            first += [copy(1 + j, a, me, (*chip, mc), True) for a in range(n)]
        first += [copy(0, a, me, sib, True) for a in range(n)]
        for cp in first:
            cp.start()
        passed = []
        for j, chip in enumerate(chips):
            for a in range(n):
                copy(1 + j, a, (*chip, mc), me).wait_recv()
            fwd = [copy(4 + j, a, (*chip, mc), sib) for a in range(n)]
            for cp in fwd:
                cp.start()
            passed += fwd
        for a in range(n):
            copy(0, a, sib, me).wait_recv()
        for j, chip in enumerate(chips):
            for a in range(n):
                copy(4 + j, a, (*chip, 1 - mc), me).wait_recv()
        for cp in first + passed:
            cp.wait_send()
        for m in mine:
            m.wait()

    hbm = pl.BlockSpec(memory_space=pl.ANY)
    out_shape = [SDS(a.shape[:-2] + (NDEV * a.shape[-2], a.shape[-1]), a.dtype) for a in locs]
    return pl.pallas_call(
        body, name=name, out_shape=out_shape, in_specs=[hbm] * n, out_specs=[hbm] * n,
        scratch_shapes=[pltpu.SemaphoreType.DMA((7 * n,)), pltpu.SemaphoreType.DMA((7 * n,)),
                        pltpu.SemaphoreType.DMA((n,))],
    )(*locs)


HBM_SPEC = pl.BlockSpec(memory_space=pltpu.HBM)
SEM_SPEC = pl.BlockSpec(memory_space=pltpu.SEMAPHORE)
DATAFLOW_EFFECT = pltpu.SideEffectType.DATAFLOW_SIDE_EFFECTING


def _exchange_copies(scatter, bufs, n, send_sems, recv_sems):
    mx, my, mc = lax.axis_index("x"), lax.axis_index("y"), lax.axis_index("c")
    me = 4 * mx + 2 * my + mc
    out = []
    for k in range(1, NDEV):
        peer = (_flip(mx, k & 4), _flip(my, k & 2), _flip(mc, k & 1))
        pidx = 4 * peer[0] + 2 * peer[1] + peer[2]
        for a in range(n):
            r = bufs[a].shape[-2] // NDEV
            if scatter:
                src, dst, arrive = _rows(bufs[a], pidx, r), bufs[n + a].at[me], bufs[n + a].at[pidx]
            else:
                src, dst, arrive = _rows(bufs[a], me, r), _rows(bufs[a], me, r), _rows(bufs[a], pidx, r)
            sems = dict(send_sem=send_sems.at[(k - 1) * n + a], recv_sem=recv_sems.at[(k - 1) * n + a],
                        device_id=peer, device_id_type=MESH)
            out.append((pltpu.make_async_remote_copy(src_ref=src, dst_ref=dst, **sems),
                        pltpu.make_async_remote_copy(src_ref=src, dst_ref=arrive, **sems)))
    return out


def _exchange_start(scatter, arrays, name):
    m = len(arrays)
    n = m // 2 if scatter else m

    def body(*refs):
        send_sems, recv_sems, token = refs[m], refs[m + 1], refs[-1]
        for go, _ in _exchange_copies(scatter, refs[:m], n, send_sems, recv_sems):
            go.start()
        token[...] = jnp.zeros_like(token)

    sems = pltpu.SemaphoreType.DMA((7 * n,))
    return pl.pallas_call(
        body, name=name, in_specs=[HBM_SPEC] * m,
        out_shape=(sems, sems, *[pltpu.HBM(a.shape, a.dtype) for a in arrays], SDS((8, CH), f32)),
        out_specs=(SEM_SPEC, SEM_SPEC, *[HBM_SPEC] * m, pl.BlockSpec(memory_space=pltpu.VMEM)),
        input_output_aliases={a: 2 + a for a in range(m)},
        compiler_params=pltpu.CompilerParams(has_side_effects=DATAFLOW_EFFECT),
    )(*[pltpu.with_memory_space_constraint(a, pltpu.HBM) for a in arrays])


def _exchange_wait(scatter, started, after, name):
    send_sems, recv_sems, *arrays = started
    m = len(arrays)
    n = m // 2 if scatter else m

    def body(*refs):
        for go, arrive in _exchange_copies(scatter, refs[:m], n, refs[m], refs[m + 1]):
            go.wait_send()
            arrive.wait_recv()

    return pl.pallas_call(
        body, name=name, in_specs=[HBM_SPEC] * m + [SEM_SPEC, SEM_SPEC, pl.BlockSpec(memory_space=pl.ANY)],
        out_shape=[pltpu.HBM(a.shape, a.dtype) for a in arrays], out_specs=[HBM_SPEC] * m,
        input_output_aliases={a: a for a in range(m)},
        compiler_params=pltpu.CompilerParams(has_side_effects=DATAFLOW_EFFECT),
    )(*arrays, send_sems, recv_sems, after)


def _mod_fwd(c_all, ada_w, ada_b_loc):
    def body(c_ref, w_ref, b_ref, o_ref):
        ca = c_ref[...]
        ca = ca * jax.nn.sigmoid(ca)
        o_ref[...] = jnp.dot(ca, w_ref[...], precision=lax.Precision.HIGHEST,
                             preferred_element_type=f32) + b_ref[...]

    return pl.pallas_call(
        body, name="mod_fwd", grid=(DEPTH,), out_shape=SDS((DEPTH, NDEV, ADAS), f32),
        in_specs=[pl.BlockSpec((NDEV, D), lambda l: (0, 0)),
                  pl.BlockSpec((None, D, ADAS), lambda l: (l, 0, 0)),
                  pl.BlockSpec((None, 1, ADAS), lambda l: (l, 0, 0))],
        out_specs=pl.BlockSpec((None, NDEV, ADAS), lambda l: (l, 0, 0)),
        compiler_params=_cp(("arbitrary",), 32),
    )(c_all, ada_w, ada_b_loc)


def _ada_grad(cact_t, dmod_cols):
    def body(c_ref, d_ref, o_ref):
        acc = c_ref[:, 0:1] * d_ref[0:1, :]
        for b in range(1, NDEV):
            acc = acc + c_ref[:, b:b + 1] * d_ref[b:b + 1, :]
        o_ref[...] = acc

    tr = 256
    return pl.pallas_call(
        body, name="ada_grad", grid=(DEPTH, D // tr), out_shape=SDS((DEPTH, D, ADAS), f32),
        in_specs=[pl.BlockSpec((tr, NDEV), lambda l, i: (i, 0)),
                  pl.BlockSpec((None, NDEV, ADAS), lambda l, i: (l, 0, 0))],
        out_specs=pl.BlockSpec((None, tr, ADAS), lambda l, i: (l, i, 0)),
        compiler_params=_cp(("arbitrary", "arbitrary"), 32),
    )(cact_t, dmod_cols)


def _ffn_fwd(h, par, w3):
    tm, tf = 1024, 256
    nj = F // tf

    def body(h_ref, par_ref, wg_ref, wu_ref, wd_ref, ho_ref, fo_ref, g_ref, u_ref, a_ref, y_scr, acc):
        j = pl.program_id(1)

        @pl.when(j == 0)
        def _():
            y_scr[...] = _norm_mod(h_ref[...], par_ref[...])[0].astype(bf16)
            acc[...] = jnp.zeros_like(acc)

        y = y_scr[...]
        g = _nt(y, wg_ref[...])
        u = _nt(y, wu_ref[...])
        a = ((g * jax.nn.sigmoid(g)) * u).astype(bf16)
        g_ref[...] = g.astype(bf16)
        u_ref[...] = u.astype(bf16)
        a_ref[...] = a
        acc[...] += _nn(a, wd_ref[...])

        @pl.when(j == nj - 1)
        def _():
            fo_ref[...] = acc[...].astype(bf16)
            ho_ref[...] = h_ref[...] + (0.5 * par_ref[2:3, :]) * acc[...]

    row = pl.BlockSpec((tm, D), lambda i, j: (i, 0))
    hid = pl.BlockSpec((tm, tf), lambda i, j: (i, j))
    wspec = [pl.BlockSpec((None, tf, D), lambda i, j, k=k: (k, j, 0)) for k in range(3)]
    return pl.pallas_call(
        body, name="ffn_fwd", grid=(T // tm, nj),
        in_specs=[row, pl.BlockSpec((8, D), lambda i, j: (0, 0))] + wspec,
        out_specs=[row, row, hid, hid, hid],
        out_shape=[SDS((T, D), f32), SDS((T, D), bf16), SDS((T, F), bf16), SDS((T, F), bf16), SDS((T, F), bf16)],
        scratch_shapes=[pltpu.VMEM((tm, D), bf16), pltpu.VMEM((tm, D), f32)],
        compiler_params=_cp(("arbitrary", "arbitrary"), 52),
    )(h, par, w3, w3, w3)


def _ffn_bwd_tok(dh, h, par, fo, gs, us, w3):
    tm, tf = 512, 256
    nj = F // tf

    def body(dh_ref, h_ref, par_ref, fo_ref, g_ref, u_ref, wg_ref, wu_ref, wd_ref,
             dhin_ref, dg_ref, du_ref, y_ref, dfb_ref, dpar_ref, df_scr, dyacc):
        i, j = pl.program_id(0), pl.program_id(1)

        @pl.when(jnp.logical_and(i == 0, j == 0))
        def _():
            dpar_ref[...] = jnp.zeros_like(dpar_ref)

        @pl.when(j == 0)
        def _():
            dh_v, par_v = dh_ref[...], par_ref[...]
            dfb = ((0.5 * par_v[2:3, :]) * dh_v).astype(bf16)
            df_scr[...] = dfb
            dfb_ref[...] = dfb
            dpar_ref[2:3, :] += 0.5 * _colsum(dh_v * fo_ref[...].astype(f32))
            y_ref[...] = _norm_mod(h_ref[...], par_v)[0].astype(bf16)
            dyacc[...] = jnp.zeros_like(dyacc)

        da = _nt(df_scr[...], wd_ref[...])
        g = g_ref[...].astype(f32)
        u = u_ref[...].astype(f32)
        sig = jax.nn.sigmoid(g)
        du = (da * (g * sig)).astype(bf16)
        dg = (da * u * (sig * (1.0 + g * (1.0 - sig)))).astype(bf16)
        dg_ref[...] = dg
        du_ref[...] = du
        dyacc[...] += _nn(dg, wg_ref[...]) + _nn(du, wu_ref[...])

        @pl.when(j == nj - 1)
        def _():
            dx, dsh, dsc, dng = _norm_mod_bwd(dyacc[...], h_ref[...], par_ref[...])
            dhin_ref[...] = dh_ref[...] + dx
            dpar_ref[0:1, :] += dsh
            dpar_ref[1:2, :] += dsc
            dpar_ref[3:4, :] += dng

    row = pl.BlockSpec((tm, D), lambda i, j: (i, 0))
    hid = pl.BlockSpec((tm, tf), lambda i, j: (i, j))
    one = pl.BlockSpec((8, D), lambda i, j: (0, 0))
    wspec = [pl.BlockSpec((None, tf, D), lambda i, j, k=k: (k, j, 0)) for k in range(3)]
    return pl.pallas_call(
        body, name="ffn_bwd_tok", grid=(T // tm, nj),
        in_specs=[row, row, one, row, hid, hid] + wspec,
        out_specs=[row, hid, hid, row, row, one],
        out_shape=[SDS((T, D), f32), SDS((T, F), bf16), SDS((T, F), bf16), SDS((T, D), bf16),
                   SDS((T, D), bf16), SDS((8, D), f32)],
        scratch_shapes=[pltpu.VMEM((tm, D), bf16), pltpu.VMEM((tm, D), f32)],
        compiler_params=_cp(("arbitrary", "arbitrary"), 48),
    )(dh, h, par, fo, gs, us, w3, w3, w3)


def _ffn_bwd_w(a_s, dg_s, du_s, y, dfb):
    tf, tk = F // 2, 256
    nk = T // tk

    def body(a_ref, dg_ref, du_ref, y_ref, df_ref, o_ref, accg, accu, accd):
        kk = pl.program_id(1)

        @pl.when(kk == 0)
        def _():
            accg[...] = jnp.zeros_like(accg)
            accu[...] = jnp.zeros_like(accu)
            accd[...] = jnp.zeros_like(accd)

        yv = y_ref[...]
        accg[...] += _tn(dg_ref[...], yv)
        accu[...] += _tn(du_ref[...], yv)
        accd[...] += _tn(a_ref[...], df_ref[...])

        @pl.when(kk == nk - 1)
        def _():
            o_ref[0] = accg[...].astype(bf16)
            o_ref[1] = accu[...].astype(bf16)
            o_ref[2] = accd[...].astype(bf16)

    hid = pl.BlockSpec((tk, tf), lambda j, kk: (kk, j))
    row = pl.BlockSpec((tk, D), lambda j, kk: (kk, 0))
    return pl.pallas_call(
        body, name="ffn_bwd_w", grid=(F // tf, nk), in_specs=[hid, hid, hid, row, row],
        out_specs=pl.BlockSpec((3, tf, D), lambda j, kk: (0, j, 0)), out_shape=SDS((3, F, D), bf16),
        scratch_shapes=[pltpu.VMEM((tf, D), f32)] * 3,
        compiler_params=_cp(("arbitrary", "arbitrary"), 48),
    )(a_s, dg_s, du_s, y, dfb)


def _tn_matmul(a, b, bm, name):
    M, N = a.shape[1], b.shape[1]
    tk = 512
    nk = T // tk

    def body(a_ref, b_ref, o_ref, acc):
        kk = pl.program_id(1)

        @pl.when(kk == 0)
        def _():
            acc[...] = jnp.zeros_like(acc)

        acc[...] += _tn(a_ref[...], b_ref[...])

        @pl.when(kk == nk - 1)
        def _():
            o_ref[...] = acc[...].astype(bf16)

    return pl.pallas_call(
        body, name=name, grid=(M // bm, nk),
        in_specs=[pl.BlockSpec((tk, bm), lambda i, kk: (kk, i)), pl.BlockSpec((tk, N), lambda i, kk: (kk, 0))],
        out_specs=pl.BlockSpec((bm, N), lambda i, kk: (i, 0)), out_shape=SDS((M, N), bf16),
        scratch_shapes=[pltpu.VMEM((bm, N), f32)],
        compiler_params=_cp(("arbitrary", "arbitrary"), 40),
    )(a, b)


def _rope_tables():
    inv = ROPE_THETA ** (-jnp.arange(0, 64, 2, dtype=f32) / 64)
    ang = jnp.arange(T, dtype=f32)[:, None] * inv[None, :]
    ang = jnp.concatenate([ang, ang], axis=-1)
    cos, sin = jnp.tile(jnp.cos(ang), (1, 8)), jnp.tile(jnp.sin(ang), (1, 8))
    low = (jnp.arange(HW) % 64 < 32)[None, :]
    return cos, jnp.where(low, -sin, 0.0), jnp.where(low, 0.0, sin)


def _rope(t, cos, sin_lo, sin_hi):
    return t * cos + pltpu.roll(t, HW - 32, 1) * sin_lo + pltpu.roll(t, 32, 1) * sin_hi


def _rope_t(g, cos, sin_lo, sin_hi):
    return g * cos + pltpu.roll(g * sin_lo, 32, 1) + pltpu.roll(g * sin_hi, HW - 32, 1)


def _inproj_fwd(h, par, w_in_t, tabs):
    tm = 512

    def body(h_ref, par_ref, w_ref, cos_ref, slo_ref, shi_ref, ua_ref, va_ref, q_ref, k_ref, v_ref):
        y = _norm_mod(h_ref[...], par_ref[...])[0].astype(bf16)
        proj = lambda c: _nt(y, w_ref[c * HW:(c + 1) * HW, :])
        ua_ref[...] = proj(0)
        va_ref[...] = proj(1)
        cos, slo, shi = cos_ref[...], slo_ref[...], shi_ref[...]
        q_ref[...] = (_rope(proj(2), cos, slo, shi) * 0.125).astype(bf16)
        k_ref[...] = _rope(proj(3), cos, slo, shi).astype(bf16)
        v_ref[...] = proj(4).astype(bf16)

    row = pl.BlockSpec((tm, D), lambda i: (i, 0))
    half = pl.BlockSpec((tm, HW), lambda i: (i, 0))
    return pl.pallas_call(
        body, name="inproj_fwd", grid=(T // tm,),
        in_specs=[row, pl.BlockSpec((8, D), lambda i: (0, 0)), pl.BlockSpec((INW, D), lambda i: (0, 0)), half, half, half],
        out_specs=[half] * 5,
        out_shape=[SDS((T, HW), f32), SDS((T, HW), f32), SDS((T, HW), bf16), SDS((T, HW), bf16), SDS((T, HW), bf16)],
        compiler_params=_cp(("arbitrary",), 48),
    )(h, par, w_in_t, *tabs)


def _head_masks():
    lane = lax.broadcasted_iota(jnp.int32, (1, CH), 1)
    return lane < 64, lane >= 64


def _attn_fwd(q, k, v, d):
    L = T // d
    nb = L // CH

    def body(q_ref, kc_ref, kp_ref, vc_ref, vp_ref, o_ref, lse_ref):
        n = pl.program_id(1)
        row = lax.broadcasted_iota(jnp.int32, (CH, CH), 0)
        col = lax.broadcasted_iota(jnp.int32, (CH, CH), 1)
        mask_c = col <= row
        mask_p = jnp.logical_and(col >= row, n > 0)
        lse_tile = jnp.zeros((CH, CH), f32)
        for hp in range(4):
            sl = slice(hp * CH, (hp + 1) * CH)
            qv, kc, kp, vc, vp = q_ref[:, sl], kc_ref[:, sl], kp_ref[:, sl], vc_ref[:, sl], vp_ref[:, sl]
            o_pair = jnp.zeros((CH, CH), f32)
            for hh, hm in enumerate(_head_masks()):
                qm = jnp.where(hm, qv, jnp.zeros_like(qv))
                sc = jnp.where(mask_c, _nt(qm, kc), -jnp.inf)
                sp = jnp.where(mask_p, _nt(qm, kp), -jnp.inf)
                m = jnp.maximum(jnp.max(sc, axis=-1, keepdims=True), jnp.max(sp, axis=-1, keepdims=True))
                pc, pp = jnp.exp(sc - m), jnp.exp(sp - m)
                den = jnp.sum(pc, axis=-1, keepdims=True) + jnp.sum(pp, axis=-1, keepdims=True)
                acc = _nn(pc.astype(bf16), vc) + _nn(pp.astype(bf16), vp)
                o_pair = jnp.where(hm, acc / den, o_pair)
                lse_tile = jnp.where(col == 2 * hp + hh, m + jnp.log(den), lse_tile)
            o_ref[:, sl] = o_pair
        lse_ref[...] = lse_tile

    cur = pl.BlockSpec((CH, HW), lambda r, n: (n, r))
    prev = pl.BlockSpec((CH, HW), lambda r, n: (jnp.maximum(n - 1, 0), r))
    q2, k2, v2 = (t.reshape(L, d * HW) for t in (q, k, v))
    o, lse = pl.pallas_call(
        body, name="attn_fwd_d%d" % d, grid=(d, nb), in_specs=[cur, cur, prev, cur, prev],
        out_specs=[cur, pl.BlockSpec((CH, CH), lambda r, n: (n, r))],
        out_shape=[SDS((L, d * HW), f32), SDS((L, d * CH), f32)],
        compiler_params=_cp(("arbitrary", "arbitrary"), 32),
    )(q2, k2, k2, v2, v2)
    return o.reshape(T, HW), lse.reshape(T, CH)


def _attn_combine(os_, lses):
    tm = 512

    def body(o0_ref, o1_ref, o2_ref, l0_ref, l1_ref, l2_ref, o_ref, lse_ref):
        l0, l1, l2 = l0_ref[...], l1_ref[...], l2_ref[...]
        m = jnp.maximum(jnp.maximum(l0, l1), l2)
        e = [jnp.exp(l0 - m), jnp.exp(l1 - m), jnp.exp(l2 - m)]
        s = e[0] + e[1] + e[2]
        w = [ei / s for ei in e]
        lse_ref[...] = m + jnp.log(s)
        lo, _ = _head_masks()
        for hp in range(4):
            sl = slice(hp * CH, (hp + 1) * CH)
            acc = jnp.zeros((tm, CH), f32)
            for wp, op_ref in zip(w, (o0_ref, o1_ref, o2_ref)):
                wexp = jnp.where(lo, wp[:, 2 * hp:2 * hp + 1], wp[:, 2 * hp + 1:2 * hp + 2])
                acc = acc + wexp * op_ref[:, sl]
            o_ref[:, sl] = acc.astype(bf16)

    half = pl.BlockSpec((tm, HW), lambda i: (i, 0))
    stat = pl.BlockSpec((tm, CH), lambda i: (i, 0))
    return pl.pallas_call(
        body, name="attn_combine", grid=(T // tm,), in_specs=[half] * 3 + [stat] * 3, out_specs=[half, stat],
        out_shape=[SDS((T, HW), bf16), SDS((T, CH), f32)],
        compiler_params=_cp(("arbitrary",), 32),
    )(*os_, *lses)


def _attn_bwd(q, k, v, mixed, do, lse, d):
    L = T // d
    nb = L // CH

    def body(qc_ref, qn_ref, kc_ref, kp_ref, vc_ref, vp_ref, oc_ref, on_ref, dc_ref, dn_ref, lc_ref, ln_ref,
             dq_ref, dk_ref, dv_ref):
        n = pl.program_id(1)
        row = lax.broadcasted_iota(jnp.int32, (CH, CH), 0)
        col = lax.broadcasted_iota(jnp.int32, (CH, CH), 1)
        mask_c = col <= row
        mask_p = jnp.logical_and(col >= row, n > 0)
        mask_n = jnp.logical_and(col >= row, n < nb - 1)
        lse_c, lse_n = lc_ref[...], ln_ref[...]
        for hp in range(4):
            sl = slice(hp * CH, (hp + 1) * CH)
            qc, qn, kc, kp, vc, vp = qc_ref[:, sl], qn_ref[:, sl], kc_ref[:, sl], kp_ref[:, sl], vc_ref[:, sl], vp_ref[:, sl]
            doc, don = dc_ref[:, sl], dn_ref[:, sl]
            prod_c = doc.astype(f32) * oc_ref[:, sl].astype(f32)
            prod_n = don.astype(f32) * on_ref[:, sl].astype(f32)
            dq_pair = jnp.zeros((CH, CH), f32)
            dk_pair = jnp.zeros((CH, CH), f32)
            dv_pair = jnp.zeros((CH, CH), f32)
            for hh, hm in enumerate(_head_masks()):
                h_idx = 2 * hp + hh
                zero = jnp.zeros_like(qc)
                qcm, qnm = jnp.where(hm, qc, zero), jnp.where(hm, qn, zero)
                dcm, dnm = jnp.where(hm, doc, zero), jnp.where(hm, don, zero)
                delta_c = jnp.sum(jnp.where(hm, prod_c, 0.0), axis=-1, keepdims=True)
                delta_n = jnp.sum(jnp.where(hm, prod_n, 0.0), axis=-1, keepdims=True)
                lc, ln = lse_c[:, h_idx:h_idx + 1], lse_n[:, h_idx:h_idx + 1]
                p = jnp.where(mask_c, jnp.exp(_nt(qcm, kc) - lc), 0.0)
                ds = (p * (_nt(dcm, vc) - delta_c)).astype(bf16)
                dq_h = _nn(ds, kc)
                dk_h = _tn(ds, qcm)
                dv_h = _tn(p.astype(bf16), dcm)
                p = jnp.where(mask_p, jnp.exp(_nt(qcm, kp) - lc), 0.0)
                ds = (p * (_nt(dcm, vp) - delta_c)).astype(bf16)
                dq_h = dq_h + _nn(ds, kp)
                p = jnp.where(mask_n, jnp.exp(_nt(qnm, kc) - ln), 0.0)
                ds = (p * (_nt(dnm, vc) - delta_n)).astype(bf16)
                dk_h = dk_h + _tn(ds, qnm)
                dv_h = dv_h + _tn(p.astype(bf16), dnm)
                dq_pair = jnp.where(hm, dq_h, dq_pair)
                dk_pair = dk_pair + dk_h
                dv_pair = dv_pair + dv_h
            dq_ref[:, sl] = dq_pair
            dk_ref[:, sl] = dk_pair
            dv_ref[:, sl] = dv_pair

    cur = pl.BlockSpec((CH, HW), lambda r, n: (n, r))
    prev = pl.BlockSpec((CH, HW), lambda r, n: (jnp.maximum(n - 1, 0), r))
    nxt = pl.BlockSpec((CH, HW), lambda r, n: (jnp.minimum(n + 1, nb - 1), r))
    ocur = pl.BlockSpec((CH, HW), lambda r, n: (n, 2 * r + 1))
    onxt = pl.BlockSpec((CH, HW), lambda r, n: (jnp.minimum(n + 1, nb - 1), 2 * r + 1))
    scur = pl.BlockSpec((CH, CH), lambda r, n: (n, r))
    snxt = pl.BlockSpec((CH, CH), lambda r, n: (jnp.minimum(n + 1, nb - 1), r))
    q2, k2, v2, do2 = (t.reshape(L, d * HW) for t in (q, k, v, do))
    m2, l2 = mixed.reshape(L, d * 2 * HW), lse.reshape(L, d * CH)
    outs = pl.pallas_call(
        body, name="attn_bwd_d%d" % d, grid=(d, nb),
        in_specs=[cur, nxt, cur, prev, cur, prev, ocur, onxt, cur, nxt, scur, snxt],
        out_specs=[cur] * 3, out_shape=[SDS((L, d * HW), f32)] * 3,
        compiler_params=_cp(("arbitrary", "arbitrary"), 32),
    )(q2, q2, k2, k2, v2, v2, m2, m2, do2, do2, l2, l2)
    return [t.reshape(T, HW) for t in outs]


def _sgu_fwd(ua, va, o, lng, lnb, ws, bs_t):
    tm = 512

    def body(ua_ref, va_ref, o_ref, lg_ref, lb_ref, ws_ref, bs_ref, mix_ref):
        for hd in range(4):
            sl = slice(hd * CH, (hd + 1) * CH)
            w = ws_ref[hd]
            for cc in range(tm // CH):
                rs = slice(cc * CH, (cc + 1) * CH)
                u = _gelu(ua_ref[rs, sl])
                v = _gelu(va_ref[rs, sl])
                vc = v - _rowmean(v)
                vn = vc * lax.rsqrt(_rowmean(vc * vc) + EPS) * lg_ref[:, sl] + lb_ref[:, sl]
                z = _nn(w, vn.astype(bf16)) + bs_ref[:, hd:hd + 1]
                mix_ref[rs, sl] = (u * z).astype(bf16)
        mix_ref[:, HW:] = o_ref[...]

    half = pl.BlockSpec((tm, HW), lambda i: (i, 0))
    vec = pl.BlockSpec((1, HW), lambda i: (0, 0))
    return pl.pallas_call(
        body, name="sgu_fwd", grid=(T // tm,),
        in_specs=[half, half, half, vec, vec, pl.BlockSpec((4, CH, CH), lambda i: (0, 0, 0)),
                  pl.BlockSpec((CH, 4), lambda i: (0, 0))],
        out_specs=pl.BlockSpec((tm, 2 * HW), lambda i: (i, 0)), out_shape=SDS((T, 2 * HW), bf16),
        compiler_params=_cp(("arbitrary",), 32),
    )(ua, va, o, lng, lnb, ws, bs_t)


def _sgu_bwd(ua, va, d_a, lng, lnb, ws, ws_t, bs_t):
    tm = 512

    def body(ua_ref, va_ref, d_ref, lg_ref, lb_ref, ws_ref, wst_ref, bs_ref, dsg_ref, dln_ref, dws_ref, db_ref):
        @pl.when(pl.program_id(0) == 0)
        def _():
            dln_ref[...] = jnp.zeros_like(dln_ref)
            dws_ref[...] = jnp.zeros_like(dws_ref)
            db_ref[...] = jnp.zeros_like(db_ref)

        for hd in range(4):
            sl = slice(hd * CH, (hd + 1) * CH)
            w, wt = ws_ref[hd], wst_ref[hd]
            lg = lg_ref[:, sl]
            for cc in range(tm // CH):
                rs = slice(cc * CH, (cc + 1) * CH)
                xa, xv, dd = ua_ref[rs, sl], va_ref[rs, sl], d_ref[rs, sl]
                u, v = _gelu(xa), _gelu(xv)
                vc = v - _rowmean(v)
                rstd = lax.rsqrt(_rowmean(vc * vc) + EPS)
                xh = vc * rstd
                vnb = (xh * lg + lb_ref[:, sl]).astype(bf16)
                z = _nn(w, vnb) + bs_ref[:, hd:hd + 1]
                dz = dd * u
                dzb = dz.astype(bf16)
                dsg_ref[rs, sl] = (dd * z * _gelu_grad(xa)).astype(bf16)
                dws_ref[hd] += _nt(dzb, vnb)
                db_ref[hd] += dz
                dvn = _nn(wt, dzb)
                dln_ref[0:1, sl] += _colsum(dvn * xh)
                dln_ref[1:2, sl] += _colsum(dvn)
                dxh = dvn * lg
                dv = rstd * (dxh - _rowmean(dxh) - xh * _rowmean(dxh * xh))
                dsg_ref[rs, HW + hd * CH:HW + (hd + 1) * CH] = (dv * _gelu_grad(xv)).astype(bf16)

    half = pl.BlockSpec((tm, HW), lambda i: (i, 0))
    vec = pl.BlockSpec((1, HW), lambda i: (0, 0))
    mat = pl.BlockSpec((4, CH, CH), lambda i: (0, 0, 0))
    return pl.pallas_call(
        body, name="sgu_bwd", grid=(T // tm,),
        in_specs=[half, half, half, vec, vec, mat, mat, pl.BlockSpec((CH, 4), lambda i: (0, 0))],
        out_specs=[pl.BlockSpec((tm, 2 * HW), lambda i: (i, 0)), pl.BlockSpec((8, HW), lambda i: (0, 0)), mat, mat],
        out_shape=[SDS((T, 2 * HW), bf16), SDS((8, HW), f32), SDS((4, CH, CH), f32), SDS((4, CH, CH), f32)],
        compiler_params=_cp(("arbitrary",), 32),
    )(ua, va, d_a, lng, lnb, ws, ws_t, bs_t)


def _outproj_fwd(mixed, w_out, h, par):
    tm = 512

    def body(mix_ref, w_ref, h_ref, par_ref, ho_ref, po_ref):
        p = _nn(mix_ref[...], w_ref[...])
        po_ref[...] = p.astype(bf16)
        ho_ref[...] = h_ref[...] + par_ref[2:3, :] * p

    row = pl.BlockSpec((tm, D), lambda i: (i, 0))
    return pl.pallas_call(
        body, name="outproj_fwd", grid=(T // tm,),
        in_specs=[row, pl.BlockSpec((D, D), lambda i: (0, 0)), row, pl.BlockSpec((8, D), lambda i: (0, 0))],
        out_specs=[row, row], out_shape=[SDS((T, D), f32), SDS((T, D), bf16)],
        compiler_params=_cp(("arbitrary",), 32),
    )(mixed, w_out, h, par)


def _outproj_bwd(dh, po, w_out, par):
    tm = 512

    def body(dh_ref, po_ref, w_ref, par_ref, do_ref, da_ref, db_ref, dpar_ref):
        @pl.when(pl.program_id(0) == 0)
        def _():
            dpar_ref[...] = jnp.zeros_like(dpar_ref)

        dh_v = dh_ref[...]
        dob = (par_ref[2:3, :] * dh_v).astype(bf16)
        do_ref[...] = dob
        dpar_ref[2:3, :] += _colsum(dh_v * po_ref[...].astype(f32))
        dm = _nt(dob, w_ref[...])
        da_ref[...] = dm[:, :HW]
        db_ref[...] = dm[:, HW:].astype(bf16)

    row = pl.BlockSpec((tm, D), lambda i: (i, 0))
    half = pl.BlockSpec((tm, HW), lambda i: (i, 0))
    one = pl.BlockSpec((8, D), lambda i: (0, 0))
    return pl.pallas_call(
        body, name="outproj_bwd", grid=(T // tm,),
        in_specs=[row, row, pl.BlockSpec((D, D), lambda i: (0, 0)), one],
        out_specs=[row, half, half, one],
        out_shape=[SDS((T, D), bf16), SDS((T, HW), f32), SDS((T, HW), bf16), SDS((8, D), f32)],
        compiler_params=_cp(("arbitrary",), 32),
    )(dh, po, w_out, par)


def _inproj_bwd_tok(dh, h, par, w_in_t, dsg, dqs, dks, dvs, tabs):
    tm = 256

    def body(dh_ref, h_ref, par_ref, w_ref, dsg_ref, dq0, dq1, dq2, dk0, dk1, dk2, dv0, dv1, dv2,
             cos_ref, slo_ref, shi_ref, dhin_ref, dp_ref, y_ref, dpar_ref):
        @pl.when(pl.program_id(0) == 0)
        def _():
            dpar_ref[...] = jnp.zeros_like(dpar_ref)

        cos, slo, shi = cos_ref[...], slo_ref[...], shi_ref[...]
        dq = (dq0[...] + dq1[...] + dq2[...]) * 0.125
        dk = dk0[...] + dk1[...] + dk2[...]
        dp_ref[:, :2 * HW] = dsg_ref[...]
        dp_ref[:, 2 * HW:3 * HW] = _rope_t(dq, cos, slo, shi).astype(bf16)
        dp_ref[:, 3 * HW:4 * HW] = _rope_t(dk, cos, slo, shi).astype(bf16)
        dp_ref[:, 4 * HW:] = (dv0[...] + dv1[...] + dv2[...]).astype(bf16)
        dy = _nn(dp_ref[...], w_ref[...])
        par_v, h_v = par_ref[...], h_ref[...]
        y_ref[...] = _norm_mod(h_v, par_v)[0].astype(bf16)
        dx, dsh, dsc, dng = _norm_mod_bwd(dy, h_v, par_v)
        dhin_ref[...] = dh_ref[...] + dx
        dpar_ref[0:1, :] += dsh
        dpar_ref[1:2, :] += dsc
        dpar_ref[3:4, :] += dng

    row = pl.BlockSpec((tm, D), lambda i: (i, 0))
    half = pl.BlockSpec((tm, HW), lambda i: (i, 0))
    one = pl.BlockSpec((8, D), lambda i: (0, 0))
    return pl.pallas_call(
        body, name="inproj_bwd_tok", grid=(T // tm,),
        in_specs=[row, row, one, pl.BlockSpec((INW, D), lambda i: (0, 0)), row] + [half] * 12,
        out_specs=[row, pl.BlockSpec((tm, INW), lambda i: (i, 0)), row, one],
        out_shape=[SDS((T, D), f32), SDS((T, INW), bf16), SDS((T, D), bf16), SDS((8, D), f32)],
        compiler_params=_cp(("arbitrary",), 48),
    )(dh, h, par, w_in_t, dsg, *dqs, *dks, *dvs, *tabs)


def _loss_head(h, par, target):
    tm = 512

    def body(h_ref, par_ref, t_ref, dh_ref, acc_ref):
        @pl.when(pl.program_id(0) == 0)
        def _():
            acc_ref[...] = jnp.zeros_like(acc_ref)

        x, g = h_ref[...], par_ref[3:4, :]
        rstd = lax.rsqrt(_rowmean(x * x) + EPS)
        xhat = x * rstd
        err = xhat * g - t_ref[...]
        acc_ref[0:1, :] += _colsum(err * err)
        dy = err * (1.0 / D)
        acc_ref[3:4, :] += _colsum(dy * xhat)
        dxhat = dy * g
        dh_ref[...] = rstd * (dxhat - xhat * _rowmean(dxhat * xhat))

    row = pl.BlockSpec((tm, D), lambda i: (i, 0))
    one = pl.BlockSpec((8, D), lambda i: (0, 0))
    return pl.pallas_call(
        body, name="loss_head", grid=(T // tm,), in_specs=[row, one, row], out_specs=[row, one],
        out_shape=[SDS((T, D), f32), SDS((8, D), f32)], compiler_params=_cp(("arbitrary",), 32),
    )(h, par, target)


ELEMENTWISE_VMEM_BUDGET = 20 << 20


def _block_rows(rows, bytes_per_row):
    cap = ELEMENTWISE_VMEM_BUDGET // bytes_per_row
    if rows <= cap:
        return rows
    return next(b for b in range(cap - cap % 16, 0, -16) if rows % b == 0)


def _sum_slots(land, name):
    _, R, C = land.shape
    br = _block_rows(R, 2 * NDEV * C * land.dtype.itemsize + 2 * C * 4)

    def body(l_ref, o_ref):
        acc = l_ref[0].astype(f32)
        for s in range(1, NDEV):
            acc = acc + l_ref[s].astype(f32)
        o_ref[...] = acc

    return pl.pallas_call(
        body, name=name, grid=(R // br,), in_specs=[pl.BlockSpec((NDEV, br, C), lambda i: (0, i, 0))],
        out_specs=pl.BlockSpec((br, C), lambda i: (i, 0)), out_shape=SDS((R, C), f32),
        compiler_params=_cp(("arbitrary",), 32),
    )(land)


def _sum_partials(land, own, me, name):
    r = land.shape[-2]
    land4, own4 = land.reshape(NDEV, -1, r, D), own.reshape(-1, NDEV, r, D)
    kk = own4.shape[0]

    def body(me_ref, l_ref, own_ref, o_ref):
        mine = own_ref[...].astype(f32)
        acc = jnp.where(me_ref[0] == 0, mine, l_ref[0].astype(f32))
        for s in range(1, NDEV):
            acc = acc + jnp.where(me_ref[0] == s, mine, l_ref[s].astype(f32))
        o_ref[...] = acc

    out = pl.pallas_call(
        body, name=name, out_shape=SDS((kk, r, D), f32),
        grid_spec=pltpu.PrefetchScalarGridSpec(
            num_scalar_prefetch=1, grid=(kk,),
            in_specs=[pl.BlockSpec((NDEV, None, r, D), lambda i, me_ref: (0, i, 0, 0)),
                      pl.BlockSpec((None, None, r, D), lambda i, me_ref: (i, me_ref[0], 0, 0))],
            out_specs=pl.BlockSpec((None, r, D), lambda i, me_ref: (i, 0, 0))),
        compiler_params=_cp(("arbitrary",), 32),
    )(me.reshape(1), land4, own4)
    return out.reshape(own.shape[:-2] + (r, D))


def _adamw(w, g, m, v, name):
    R, C = w.shape
    br = _block_rows(R, 2 * 7 * C * 4)

    def body(w_ref, g_ref, m_ref, v_ref, d_ref, mo_ref, vo_ref):
        gv = g_ref[...]
        m2 = B1 * m_ref[...] + (1.0 - B1) * gv
        v2 = B2 * v_ref[...] + (1.0 - B2) * (gv * gv)
        mo_ref[...] = m2
        vo_ref[...] = v2
        m_hat = m2 / (1.0 - B1 ** STEP)
        v_hat = v2 / (1.0 - B2 ** STEP)
        d_ref[...] = -LR * (m_hat / (jnp.sqrt(v_hat) + AEPS) + WD * w_ref[...])

    blk = pl.BlockSpec((br, C), lambda i: (i, 0))
    return pl.pallas_call(
        body, name=name, grid=(R // br,), in_specs=[blk] * 4, out_specs=[blk] * 3,
        out_shape=[SDS((R, C), f32)] * 3, compiler_params=_cp(("arbitrary",), 32),
    )(w, g, m, v)


def _adamw_nd(w, g, m, v, name):
    shp = w.shape
    r2 = (-1, shp[-1]) if w.ndim > 1 else (8, shp[0] // 8)
    outs = _adamw(w.reshape(r2), g.reshape(r2), m.reshape(r2), v.reshape(r2), name)
    return [o.reshape(shp) for o in outs]


def _par_rows(mod_l, s, gain):
    rows = jnp.pad(mod_l.reshape(9, D)[3 * s:3 * s + 3], ((0, 5), (0, 0)))
    return rows + jnp.pad(gain[None, :], ((3, 4), (0, 0)))


def _pad_rows(a):
    a = a.reshape(-1, CH)
    return jnp.pad(a, ((0, (-a.shape[0]) % 8), (0, 0)))


def _prepare(c, ada_w, ada_b, norm_g):
    me = 4 * lax.axis_index("x") + 2 * lax.axis_index("y") + lax.axis_index("c")

    pay = jnp.pad(c, ((0, 7), (0, 0)))
    pay = jnp.concatenate([pay, jnp.pad(norm_g.reshape(6, OUTS), ((0, 2), (0, D - OUTS)))], axis=0)
    got = _all_gather_small(pay, "gather_c")
    c_all = got[:, 0, :]
    gains = got[:, 8:14, :OUTS].transpose(1, 0, 2).reshape(DEPTH, 3, D)

    ada_b_loc = lax.dynamic_slice(ada_b, (0, me * ADAS), (DEPTH, ADAS)).reshape(DEPTH, 1, ADAS)
    mod_cols = _mod_fwd(c_all, ada_w, ada_b_loc)
    got = _all_gather_small(mod_cols.reshape(DEPTH * NDEV, ADAS), "gather_mod").reshape(NDEV, DEPTH, NDEV, ADAS)
    mod = lax.dynamic_index_in_dim(got, me, axis=2, keepdims=False).transpose(1, 0, 2).reshape(DEPTH, 9 * D)
    pars = [[_par_rows(mod[l], s, gains[l, s]) for s in range(3)] for l in range(DEPTH)]
    return me, c_all, pars


def _fwd_bwd(x2, target, pars, get_w, put_g, sgu_ln_g, sgu_ln_b, sgu_w, sgu_b, final_g):
    tabs = _rope_tables()
    tril = jnp.tril(jnp.ones((CH, CH), dtype=bool))
    ws_m = jnp.where(tril[None, None], sgu_w, 0.0).astype(bf16)
    ws_mt = jnp.swapaxes(ws_m, -1, -2)

    h = x2
    saved = []
    for l in range(DEPTH):
        lng, lnb = sgu_ln_g[l].reshape(1, HW), sgu_ln_b[l].reshape(1, HW)
        bs_t = sgu_b[l].T
        h0 = h
        h1, fo1, g1, u1, a1 = _ffn_fwd(h0, pars[l][0], get_w(4 * l, h0))
        ua, va, q, k, v = _inproj_fwd(h1, pars[l][1], get_w(4 * l + 2, h1), tabs)
        branches = [_attn_fwd(q, k, v, d) for d in PATTERN_DILATIONS]
        o, lse = _attn_combine([b[0] for b in branches], [b[1] for b in branches])
        mixed = _sgu_fwd(ua, va, o, lng, lnb, ws_m[l], bs_t)
        h2, po = _outproj_fwd(mixed, get_w(4 * l + 3, mixed), h1, pars[l][1])
        h3, fo2, g2, u2, a2 = _ffn_fwd(h2, pars[l][2], get_w(4 * l + 1, h2))
        saved.append((h0, h1, h2, fo1, g1, u1, a1, ua, va, q, k, v, mixed, lse, po, fo2, g2, u2, a2))
        h = h3

    par_f = jnp.pad(final_g[None, :], ((3, 4), (0, 0)))
    dh, head = _loss_head(h, par_f, target)

    dmods, dgains, dsgu = [None] * DEPTH, [None] * DEPTH, [None] * DEPTH
    hold = 0.0
    for l in reversed(range(DEPTH)):
        w_f1, w_f2, w_i, w_o = (get_w(4 * l + j, None) for j in (0, 1, 2, 3))
        h0, h1, h2, fo1, g1, u1, a1, ua, va, q, k, v, mixed, lse, po, fo2, g2, u2, a2 = saved[l]
        lng, lnb = sgu_ln_g[l].reshape(1, HW), sgu_ln_b[l].reshape(1, HW)
        bs_t = sgu_b[l].T

        dh, dg_s, du_s, y, dfb, dpar3 = _ffn_bwd_tok(dh, h2, pars[l][2] + hold, fo2, g2, u2, w_f2)
        hold = put_g(4 * l + 1, _ffn_bwd_w(a2, dg_s, du_s, y, dfb))

        dob, d_a, d_b, dpar2g = _outproj_bwd(dh, po, w_o, pars[l][1] + hold)
        hold = put_g(4 * l + 3, _tn_matmul(mixed, dob, 512, "w_out_grad"))
        parts = [_attn_bwd(q, k, v, mixed, d_b, lse, d) for d in PATTERN_DILATIONS]
        dsg, dln, dws, dbl = _sgu_bwd(ua, va, d_a, lng, lnb, ws_m[l], ws_mt[l], bs_t)
        dh, dp, y2, dpar2 = _inproj_bwd_tok(dh, h1, pars[l][1] + hold, w_i, dsg, [p[0] for p in parts],
                                            [p[1] for p in parts], [p[2] for p in parts], tabs)
        hold = put_g(4 * l + 2, _tn_matmul(dp, y2, 640, "w_in_grad"))

        dh, dg_s, du_s, y, dfb, dpar1 = _ffn_bwd_tok(dh, h0, pars[l][0] + hold, fo1, g1, u1, w_f1)
        hold = put_g(4 * l, _ffn_bwd_w(a1, dg_s, du_s, y, dfb))

        dmods[l] = jnp.concatenate([dpar1[0:3], dpar2[0:2], dpar2g[2:3], dpar3[0:3]], axis=0).reshape(9 * D)
        dgains[l] = jnp.stack([dpar1[3], dpar2[3], dpar3[3]])
        dsgu[l] = (dln[0], dln[1], jnp.where(tril[None], dws, 0.0), jnp.sum(dbl, axis=-1))
    return dh, head, dmods, dgains, dsgu


def kernel(x, c, ada_w, ada_b, norm_g, ffn1_wg, ffn1_wu, ffn1_wd, ffn2_wg, ffn2_wu, ffn2_wd, w_in, sgu_ln_g, sgu_ln_b, sgu_w, sgu_b, w_out, final_g, loss_target, m_ada_w, m_ada_b, m_norm_g, m_ffn1_wg, m_ffn1_wu, m_ffn1_wd, m_ffn2_wg, m_ffn2_wu, m_ffn2_wd, m_w_in, m_sgu_ln_g, m_sgu_ln_b, m_sgu_w, m_sgu_b, m_w_out, m_final_g, v_ada_w, v_ada_b, v_norm_g, v_ffn1_wg, v_ffn1_wu, v_ffn1_wd, v_ffn2_wg, v_ffn2_wu, v_ffn2_wd, v_w_in, v_sgu_ln_g, v_sgu_ln_b, v_sgu_w, v_sgu_b, v_w_out, v_final_g):
    me, c_all, pars = _prepare(c, ada_w, ada_b, norm_g)

    tr = lambda w: jnp.swapaxes(w, -1, -2).astype(bf16)
    locs = []
    for l in range(DEPTH):
        locs.append(jnp.stack([tr(ffn1_wg[l]), tr(ffn1_wu[l]), ffn1_wd[l].astype(bf16)]))
        locs.append(jnp.stack([tr(ffn2_wg[l]), tr(ffn2_wu[l]), ffn2_wd[l].astype(bf16)]))
        locs.append(tr(w_in[l]))
        locs.append(w_out[l].astype(bf16))

    def in_place(loc):
        full = lax.empty(loc.shape[:-2] + (NDEV * loc.shape[-2], D), bf16)
        return lax.dynamic_update_slice_in_dim(full, loc, me * loc.shape[-2], axis=loc.ndim - 2)

    ready = {0: _all_gather_rows([locs[0]], "gather_first")[0]}
    groups = ([2, 3], [1], [4, 6, 7], [5])
    flying, hold = {}, 0.0
    for gi, pieces in enumerate(groups):
        started = _exchange_start(False, [in_place(locs[p]) for p in pieces], "gather_start_%d" % gi)
        flying.update({p: (gi, pieces, started[:-1]) for p in pieces})
        hold = hold + started[-1][0, 0]
    pars[0][0] = pars[0][0] + hold

    def get_w(piece, after):
        if piece not in ready:
            gi, pieces, started = flying[piece]
            ready.update(zip(pieces, _exchange_wait(False, started, after, "gather_wait_%d" % gi)))
        return ready[piece]

    sent = {}

    def put_g(piece, grad):
        land = lax.empty((NDEV,) + grad.shape[:-2] + (grad.shape[-2] // NDEV, D), bf16)
        started = _exchange_start(True, [grad, land], "scatter_start_%d" % piece)
        sent[piece] = started[:-1]
        return started[-1][0, 0]

    dh, head, dmods, dgains, dsgu = _fwd_bwd(x[0], loss_target[0], pars, get_w, put_g, sgu_ln_g, sgu_ln_b, sgu_w, sgu_b,
                                             final_g)
    grad_x = dh[None]

    loss_part = 0.5 * jnp.sum(head[0]) / D
    small = jnp.concatenate([
        _pad_rows(jnp.stack(dmods)), _pad_rows(jnp.stack(dgains)),
        _pad_rows(jnp.stack([s[0] for s in dsgu])), _pad_rows(jnp.stack([s[1] for s in dsgu])),
        _pad_rows(jnp.stack([s[3] for s in dsgu])), _pad_rows(jnp.stack([s[2] for s in dsgu])),
        _pad_rows(head[3]), _pad_rows(jnp.pad(loss_part[None], (0, CH - 1)))], axis=0)
    got = _all_gather_small(small, "gather_small")
    tot = _sum_slots(got, "sum_small")
    n_mod, n_gain, n_sw = DEPTH * 9 * D // CH, DEPTH * 3 * D // CH, DEPTH * 4 * CH
    offs = [0, n_mod, n_mod + n_gain, n_mod + n_gain + 8, n_mod + n_gain + 16, n_mod + n_gain + 24]
    g_ada_b = tot[offs[0]:offs[1]].reshape(DEPTH, 9 * D)
    g_gain_full = tot[offs[1]:offs[2]].reshape(DEPTH, 3, D)
    g_ln_g = tot[offs[2]:offs[3]].reshape(DEPTH, 4, CH)
    g_ln_b = tot[offs[3]:offs[4]].reshape(DEPTH, 4, CH)
    g_sb = tot[offs[4]:offs[5]].reshape(DEPTH, 4, CH)
    g_sw = tot[offs[5]:offs[5] + n_sw].reshape(DEPTH, 4, CH, CH)
    g_final = tot[offs[5] + n_sw:offs[5] + n_sw + 8].reshape(D)
    loss = tot[offs[5] + n_sw + 8, 0]
    g_norm = lax.dynamic_slice(g_gain_full, (0, 0, me * OUTS), (DEPTH, 3, OUTS))

    dmod_all = got[:, offs[0]:offs[1]].reshape(NDEV, DEPTH, 9 * D)
    dmod_cols = lax.dynamic_slice(dmod_all, (0, 0, me * ADAS), (NDEV, DEPTH, ADAS)).transpose(1, 0, 2)
    g_ada_w = _ada_grad((c_all * jax.nn.sigmoid(c_all)).T, dmod_cols)

    sums = []
    for piece in range(4 * DEPTH):
        own, land = _exchange_wait(True, sent[piece], tot, "scatter_wait_%d" % piece)
        sums.append(_sum_partials(land, own, me, "sum_grads"))
    back = lambda t: jnp.swapaxes(t, -1, -2)
    f1 = jnp.stack([sums[4 * l] for l in range(DEPTH)])
    f2 = jnp.stack([sums[4 * l + 1] for l in range(DEPTH)])
    g_w_in = back(jnp.stack([sums[4 * l + 2] for l in range(DEPTH)]))
    g_w_out = jnp.stack([sums[4 * l + 3] for l in range(DEPTH)])

    gw = [g_ada_w, g_ada_b, g_norm, back(f1[:, 0]), back(f1[:, 1]), f1[:, 2], back(f2[:, 0]), back(f2[:, 1]), f2[:, 2],
          g_w_in, g_ln_g, g_ln_b, g_sw, g_sb, g_w_out, g_final]
    ws = [ada_w, ada_b, norm_g, ffn1_wg, ffn1_wu, ffn1_wd, ffn2_wg, ffn2_wu, ffn2_wd, w_in, sgu_ln_g, sgu_ln_b, sgu_w,
          sgu_b, w_out, final_g]
    ms = [m_ada_w, m_ada_b, m_norm_g, m_ffn1_wg, m_ffn1_wu, m_ffn1_wd, m_ffn2_wg, m_ffn2_wu, m_ffn2_wd, m_w_in,
          m_sgu_ln_g, m_sgu_ln_b, m_sgu_w, m_sgu_b, m_w_out, m_final_g]
    vs = [v_ada_w, v_ada_b, v_norm_g, v_ffn1_wg, v_ffn1_wu, v_ffn1_wd, v_ffn2_wg, v_ffn2_wu, v_ffn2_wd, v_w_in,
          v_sgu_ln_g, v_sgu_ln_b, v_sgu_w, v_sgu_b, v_w_out, v_final_g]
    upd = [_adamw_nd(w, g, m, v, "adamw") for w, g, m, v in zip(ws, gw, ms, vs)]
    return (loss, grad_x, *gw, *[u[0] for u in upd], *[u[1] for u in upd], *[u[2] for u in upd])
```

```python
import functools
import math

import jax
import jax.numpy as jnp
from jax import lax
from jax.experimental import pallas as pl
from jax.experimental.pallas import tpu as pltpu

f32, bf16 = jnp.float32, jnp.bfloat16
SDS = jax.ShapeDtypeStruct

T, D, F = 4096, 1024, 2816
NDEV, DEPTH = 8, 2
HW = 512
INW = 5 * HW
FS, INS, OUTS, ADAS = F // NDEV, INW // NDEV, D // NDEV, 9 * D // NDEV
CH = 128
PATTERN_DILATIONS = (1, 4, 16)
ROPE_THETA = 10000.0
EPS = 1e-6
LR, B1, B2, AEPS, WD, STEP = 0.001, 0.9, 0.999, 1e-08, 0.01, 10
MESH = pl.DeviceIdType.MESH


def _cp(sems, vmem_mb):
    return pltpu.CompilerParams(dimension_semantics=sems, vmem_limit_bytes=vmem_mb << 20)


def _nn(a, b):
    return lax.dot_general(a, b, (((1,), (0,)), ((), ())), preferred_element_type=f32)


def _nt(a, b):
    return lax.dot_general(a, b, (((1,), (1,)), ((), ())), preferred_element_type=f32)


def _tn(a, b):
    return lax.dot_general(a, b, (((0,), (0,)), ((), ())), preferred_element_type=f32)


def _colsum(a):
    return jnp.sum(a, axis=0, keepdims=True)


def _rowmean(a):
    return jnp.mean(a, axis=-1, keepdims=True)


def _norm_mod(x, par):
    rstd = lax.rsqrt(_rowmean(x * x) + EPS)
    xhat = x * rstd
    n = xhat * par[3:4, :]
    y = n * (1.0 + par[1:2, :]) + par[0:1, :]
    return y, n, xhat, rstd


def _norm_mod_bwd(dy, x, par):
    _, n, xhat, rstd = _norm_mod(x, par)
    dn = dy * (1.0 + par[1:2, :])
    dxhat = dn * par[3:4, :]
    dx = rstd * (dxhat - xhat * _rowmean(dxhat * xhat))
    return dx, _colsum(dy), _colsum(dy * n), _colsum(dn * xhat)


_GK = math.sqrt(2.0 / math.pi)


def _gelu(x):
    return 0.5 * x * (1.0 + jnp.tanh(_GK * (x + 0.044715 * x * x * x)))


def _gelu_grad(x):
    t = jnp.tanh(_GK * (x + 0.044715 * x * x * x))
    return 0.5 * (1.0 + t) + 0.5 * x * (1.0 - t * t) * (_GK * (1.0 + 3.0 * 0.044715 * x * x))


def _rows(ref, idx, r):
    if len(ref.shape) == 3:
        return ref.at[:, pl.ds(idx * r, r), :]
    return ref.at[pl.ds(idx * r, r), :]


def _flip(v, bit):
    return 1 - v if bit else v


def _all_gather_small(x, name):
    R, C = x.shape

    def body(x_ref, out_ref, send_sems, recv_sems):
        mx, my, mc = lax.axis_index("x"), lax.axis_index("y"), lax.axis_index("c")
        me = 4 * mx + 2 * my + mc
        out_ref[me] = x_ref[...]
        sent = []
        for k in range(1, NDEV):
            peer = (_flip(mx, k & 4), _flip(my, k & 2), _flip(mc, k & 1))
            cp = pltpu.make_async_remote_copy(
                src_ref=x_ref, dst_ref=out_ref.at[me], send_sem=send_sems.at[k - 1],
                recv_sem=recv_sems.at[k - 1], device_id=peer, device_id_type=MESH)
            cp.start()
            sent.append(cp)
        for k in range(1, NDEV):
            peer = (_flip(mx, k & 4), _flip(my, k & 2), _flip(mc, k & 1))
            pidx = 4 * peer[0] + 2 * peer[1] + peer[2]
            pltpu.make_async_remote_copy(
                src_ref=x_ref, dst_ref=out_ref.at[pidx], send_sem=send_sems.at[k - 1],
                recv_sem=recv_sems.at[k - 1], device_id=peer, device_id_type=MESH).wait_recv()
        for cp in sent:
            cp.wait_send()

    vm = pl.BlockSpec(memory_space=pltpu.VMEM)
    return pl.pallas_call(
        body, name=name, out_shape=SDS((NDEV, R, C), f32), in_specs=[vm], out_specs=vm,
        scratch_shapes=[pltpu.SemaphoreType.DMA((NDEV - 1,)), pltpu.SemaphoreType.DMA((NDEV - 1,))],
        compiler_params=pltpu.CompilerParams(vmem_limit_bytes=32 << 20),
    )(x)


def _all_gather_rows(locs, name):
    n = len(locs)
    rs = [a.shape[-2] for a in locs]

    def body(*refs):
        src, out = refs[:n], refs[n:2 * n]
        send_sems, recv_sems, loc_sems = refs[2 * n:]
        mx, my, mc = lax.axis_index("x"), lax.axis_index("y"), lax.axis_index("c")
        me, sib = (mx, my, mc), (mx, my, 1 - mc)
        chips = [(1 - mx, my), (mx, 1 - my), (1 - mx, 1 - my)]

        def blk(a, p):
            return _rows(out[a], 4 * p[0] + 2 * p[1] + p[2], rs[a])

        def copy(k, a, block, to, from_src=False):
            return pltpu.make_async_remote_copy(
                src_ref=src[a] if from_src else blk(a, block), dst_ref=blk(a, block),
                send_sem=send_sems.at[k * n + a], recv_sem=recv_sems.at[k * n + a],
                device_id=to, device_id_type=MESH)

        mine = [pltpu.make_async_copy(src[a], blk(a, me), loc_sems.at[a]) for a in range(n)]
        for m in mine:
            m.start()
        first = []
        for j, chip in enumerate(chips):
            first += [copy(1 + j, a, me, (*chip, mc), True) for a in range(n)]
        first += [copy(0, a, me, sib, True) for a in range(n)]
        for cp in first:
            cp.start()
        passed = []
        for j, chip in enumerate(chips):
            for a in range(n):
                copy(1 + j, a, (*chip, mc), me).wait_recv()
            fwd = [copy(4 + j, a, (*chip, mc), sib) for a in range(n)]
            for cp in fwd:
                cp.start()
            passed += fwd
        for a in range(n):
            copy(0, a, sib, me).wait_recv()
        for j, chip in enumerate(chips):
            for a in range(n):
                copy(4 + j, a, (*chip, 1 - mc), me).wait_recv()
        for cp in first + passed:
            cp.wait_send()
        for m in mine:
            m.wait()

    hbm = pl.BlockSpec(memory_space=pl.ANY)
    out_shape = [SDS(a.shape[:-2] + (NDEV * a.shape[-2], a.shape[-1]), a.dtype) for a in locs]
    return pl.pallas_call(
        body, name=name, out_shape=out_shape, in_specs=[hbm] * n, out_specs=[hbm] * n,
        scratch_shapes=[pltpu.SemaphoreType.DMA((7 * n,)), pltpu.SemaphoreType.DMA((7 * n,)),
                        pltpu.SemaphoreType.DMA((n,))],
    )(*locs)


HBM_SPEC = pl.BlockSpec(memory_space=pltpu.HBM)
SEM_SPEC = pl.BlockSpec(memory_space=pltpu.SEMAPHORE)
DATAFLOW_EFFECT = pltpu.SideEffectType.DATAFLOW_SIDE_EFFECTING


def _exchange_copies(scatter, bufs, n, send_sems, recv_sems):
    mx, my, mc = lax.axis_index("x"), lax.axis_index("y"), lax.axis_index("c")
    me = 4 * mx + 2 * my + mc
    out = []
    for k in range(1, NDEV):
        peer = (_flip(mx, k & 4), _flip(my, k & 2), _flip(mc, k & 1))
        pidx = 4 * peer[0] + 2 * peer[1] + peer[2]
        for a in range(n):
            r = bufs[a].shape[-2] // NDEV
            if scatter:
                src, dst, arrive = _rows(bufs[a], pidx, r), bufs[n + a].at[me], bufs[n + a].at[pidx]
            else:
                src, dst, arrive = _rows(bufs[a], me, r), _rows(bufs[a], me, r), _rows(bufs[a], pidx, r)
            sems = dict(send_sem=send_sems.at[(k - 1) * n + a], recv_sem=recv_sems.at[(k - 1) * n + a],
                        device_id=peer, device_id_type=MESH)
            out.append((pltpu.make_async_remote_copy(src_ref=src, dst_ref=dst, **sems),
                        pltpu.make_async_remote_copy(src_ref=src, dst_ref=arrive, **sems)))
    return out


def _exchange_start(scatter, arrays, name):
    m = len(arrays)
    n = m // 2 if scatter else m

    def body(*refs):
        send_sems, recv_sems, token = refs[m], refs[m + 1], refs[-1]
        for go, _ in _exchange_copies(scatter, refs[:m], n, send_sems, recv_sems):
            go.start()
        token[...] = jnp.zeros_like(token)

    sems = pltpu.SemaphoreType.DMA((7 * n,))
    return pl.pallas_call(
        body, name=name, in_specs=[HBM_SPEC] * m,
        out_shape=(sems, sems, *[pltpu.HBM(a.shape, a.dtype) for a in arrays], SDS((8, CH), f32)),
        out_specs=(SEM_SPEC, SEM_SPEC, *[HBM_SPEC] * m, pl.BlockSpec(memory_space=pltpu.VMEM)),
        input_output_aliases={a: 2 + a for a in range(m)},
        compiler_params=pltpu.CompilerParams(has_side_effects=DATAFLOW_EFFECT),
    )(*[pltpu.with_memory_space_constraint(a, pltpu.HBM) for a in arrays])


def _exchange_wait(scatter, started, after, name):
    send_sems, recv_sems, *arrays = started
    m = len(arrays)
    n = m // 2 if scatter else m

    def body(*refs):
        for go, arrive in _exchange_copies(scatter, refs[:m], n, refs[m], refs[m + 1]):
            go.wait_send()
            arrive.wait_recv()

    return pl.pallas_call(
        body, name=name, in_specs=[HBM_SPEC] * m + [SEM_SPEC, SEM_SPEC, pl.BlockSpec(memory_space=pl.ANY)],
        out_shape=[pltpu.HBM(a.shape, a.dtype) for a in arrays], out_specs=[HBM_SPEC] * m,
        input_output_aliases={a: a for a in range(m)},
        compiler_params=pltpu.CompilerParams(has_side_effects=DATAFLOW_EFFECT),
    )(*arrays, send_sems, recv_sems, after)


def _mod_fwd(c_all, ada_w, ada_b_loc):
    def body(c_ref, w_ref, b_ref, o_ref):
        ca = c_ref[...]
        ca = ca * jax.nn.sigmoid(ca)
        o_ref[...] = jnp.dot(ca, w_ref[...], precision=lax.Precision.HIGHEST,
                             preferred_element_type=f32) + b_ref[...]

    return pl.pallas_call(
        body, name="mod_fwd", grid=(DEPTH,), out_shape=SDS((DEPTH, NDEV, ADAS), f32),
        in_specs=[pl.BlockSpec((NDEV, D), lambda l: (0, 0)),
                  pl.BlockSpec((None, D, ADAS), lambda l: (l, 0, 0)),
                  pl.BlockSpec((None, 1, ADAS), lambda l: (l, 0, 0))],
        out_specs=pl.BlockSpec((None, NDEV, ADAS), lambda l: (l, 0, 0)),
        compiler_params=_cp(("arbitrary",), 32),
    )(c_all, ada_w, ada_b_loc)


def _ada_grad(cact_t, dmod_cols):
    def body(c_ref, d_ref, o_ref):
        acc = c_ref[:, 0:1] * d_ref[0:1, :]
        for b in range(1, NDEV):
            acc = acc + c_ref[:, b:b + 1] * d_ref[b:b + 1, :]
        o_ref[...] = acc

    tr = 256
    return pl.pallas_call(
        body, name="ada_grad", grid=(DEPTH, D // tr), out_shape=SDS((DEPTH, D, ADAS), f32),
        in_specs=[pl.BlockSpec((tr, NDEV), lambda l, i: (i, 0)),
                  pl.BlockSpec((None, NDEV, ADAS), lambda l, i: (l, 0, 0))],
        out_specs=pl.BlockSpec((None, tr, ADAS), lambda l, i: (l, i, 0)),
        compiler_params=_cp(("arbitrary", "arbitrary"), 32),
    )(cact_t, dmod_cols)


def _ffn_fwd(h, par, w3):
    tm, tf = 1024, 256
    nj = F // tf

    def body(h_ref, par_ref, wg_ref, wu_ref, wd_ref, ho_ref, fo_ref, g_ref, u_ref, a_ref, y_scr, acc):
        j = pl.program_id(1)

        @pl.when(j == 0)
        def _():
            y_scr[...] = _norm_mod(h_ref[...], par_ref[...])[0].astype(bf16)
            acc[...] = jnp.zeros_like(acc)

        y = y_scr[...]
        g = _nt(y, wg_ref[...])
        u = _nt(y, wu_ref[...])
        a = ((g * jax.nn.sigmoid(g)) * u).astype(bf16)
        g_ref[...] = g.astype(bf16)
        u_ref[...] = u.astype(bf16)
        a_ref[...] = a
        acc[...] += _nn(a, wd_ref[...])

        @pl.when(j == nj - 1)
        def _():
            fo_ref[...] = acc[...].astype(bf16)
            ho_ref[...] = h_ref[...] + (0.5 * par_ref[2:3, :]) * acc[...]

    row = pl.BlockSpec((tm, D), lambda i, j: (i, 0))
    hid = pl.BlockSpec((tm, tf), lambda i, j: (i, j))
    wspec = [pl.BlockSpec((None, tf, D), lambda i, j, k=k: (k, j, 0)) for k in range(3)]
    return pl.pallas_call(
        body, name="ffn_fwd", grid=(T // tm, nj),
        in_specs=[row, pl.BlockSpec((8, D), lambda i, j: (0, 0))] + wspec,
        out_specs=[row, row, hid, hid, hid],
        out_shape=[SDS((T, D), f32), SDS((T, D), bf16), SDS((T, F), bf16), SDS((T, F), bf16), SDS((T, F), bf16)],
        scratch_shapes=[pltpu.VMEM((tm, D), bf16), pltpu.VMEM((tm, D), f32)],
        compiler_params=_cp(("arbitrary", "arbitrary"), 52),
    )(h, par, w3, w3, w3)


def _ffn_bwd_tok(dh, h, par, fo, gs, us, w3):
    tm, tf = 512, 256
    nj = F // tf

    def body(dh_ref, h_ref, par_ref, fo_ref, g_ref, u_ref, wg_ref, wu_ref, wd_ref,
             dhin_ref, dg_ref, du_ref, y_ref, dfb_ref, dpar_ref, df_scr, dyacc):
        i, j = pl.program_id(0), pl.program_id(1)

        @pl.when(jnp.logical_and(i == 0, j == 0))
        def _():
            dpar_ref[...] = jnp.zeros_like(dpar_ref)

        @pl.when(j == 0)
        def _():
            dh_v, par_v = dh_ref[...], par_ref[...]
            dfb = ((0.5 * par_v[2:3, :]) * dh_v).astype(bf16)
            df_scr[...] = dfb
            dfb_ref[...] = dfb
            dpar_ref[2:3, :] += 0.5 * _colsum(dh_v * fo_ref[...].astype(f32))
            y_ref[...] = _norm_mod(h_ref[...], par_v)[0].astype(bf16)
            dyacc[...] = jnp.zeros_like(dyacc)

        da = _nt(df_scr[...], wd_ref[...])
        g = g_ref[...].astype(f32)
        u = u_ref[...].astype(f32)
        sig = jax.nn.sigmoid(g)
        du = (da * (g * sig)).astype(bf16)
        dg = (da * u * (sig * (1.0 + g * (1.0 - sig)))).astype(bf16)
        dg_ref[...] = dg
        du_ref[...] = du
        dyacc[...] += _nn(dg, wg_ref[...]) + _nn(du, wu_ref[...])

        @pl.when(j == nj - 1)
        def _():
            dx, dsh, dsc, dng = _norm_mod_bwd(dyacc[...], h_ref[...], par_ref[...])
            dhin_ref[...] = dh_ref[...] + dx
            dpar_ref[0:1, :] += dsh
            dpar_ref[1:2, :] += dsc
            dpar_ref[3:4, :] += dng

    row = pl.BlockSpec((tm, D), lambda i, j: (i, 0))
    hid = pl.BlockSpec((tm, tf), lambda i, j: (i, j))
    one = pl.BlockSpec((8, D), lambda i, j: (0, 0))
    wspec = [pl.BlockSpec((None, tf, D), lambda i, j, k=k: (k, j, 0)) for k in range(3)]
    return pl.pallas_call(
        body, name="ffn_bwd_tok", grid=(T // tm, nj),
        in_specs=[row, row, one, row, hid, hid] + wspec,
        out_specs=[row, hid, hid, row, row, one],
        out_shape=[SDS((T, D), f32), SDS((T, F), bf16), SDS((T, F), bf16), SDS((T, D), bf16),
                   SDS((T, D), bf16), SDS((8, D), f32)],
        scratch_shapes=[pltpu.VMEM((tm, D), bf16), pltpu.VMEM((tm, D), f32)],
        compiler_params=_cp(("arbitrary", "arbitrary"), 48),
    )(dh, h, par, fo, gs, us, w3, w3, w3)


def _ffn_bwd_w(a_s, dg_s, du_s, y, dfb):
    tf, tk = F // 2, 256
    nk = T // tk

    def body(a_ref, dg_ref, du_ref, y_ref, df_ref, o_ref, accg, accu, accd):
        kk = pl.program_id(1)

        @pl.when(kk == 0)
        def _():
            accg[...] = jnp.zeros_like(accg)
            accu[...] = jnp.zeros_like(accu)
            accd[...] = jnp.zeros_like(accd)

        yv = y_ref[...]
        accg[...] += _tn(dg_ref[...], yv)
        accu[...] += _tn(du_ref[...], yv)
        accd[...] += _tn(a_ref[...], df_ref[...])

        @pl.when(kk == nk - 1)
        def _():
            o_ref[0] = accg[...].astype(bf16)
            o_ref[1] = accu[...].astype(bf16)
            o_ref[2] = accd[...].astype(bf16)

    hid = pl.BlockSpec((tk, tf), lambda j, kk: (kk, j))
    row = pl.BlockSpec((tk, D), lambda j, kk: (kk, 0))
    return pl.pallas_call(
        body, name="ffn_bwd_w", grid=(F // tf, nk), in_specs=[hid, hid, hid, row, row],
        out_specs=pl.BlockSpec((3, tf, D), lambda j, kk: (0, j, 0)), out_shape=SDS((3, F, D), bf16),
        scratch_shapes=[pltpu.VMEM((tf, D), f32)] * 3,
        compiler_params=_cp(("arbitrary", "arbitrary"), 48),
    )(a_s, dg_s, du_s, y, dfb)


def _tn_matmul(a, b, bm, name):
    M, N = a.shape[1], b.shape[1]
    tk = 512
    nk = T // tk

    def body(a_ref, b_ref, o_ref, acc):
        kk = pl.program_id(1)

        @pl.when(kk == 0)
        def _():
            acc[...] = jnp.zeros_like(acc)

        acc[...] += _tn(a_ref[...], b_ref[...])

        @pl.when(kk == nk - 1)
        def _():
            o_ref[...] = acc[...].astype(bf16)

    return pl.pallas_call(
        body, name=name, grid=(M // bm, nk),
        in_specs=[pl.BlockSpec((tk, bm), lambda i, kk: (kk, i)), pl.BlockSpec((tk, N), lambda i, kk: (kk, 0))],
        out_specs=pl.BlockSpec((bm, N), lambda i, kk: (i, 0)), out_shape=SDS((M, N), bf16),
        scratch_shapes=[pltpu.VMEM((bm, N), f32)],
        compiler_params=_cp(("arbitrary", "arbitrary"), 40),
    )(a, b)


def _rope_tables():
    inv = ROPE_THETA ** (-jnp.arange(0, 64, 2, dtype=f32) / 64)
    ang = jnp.arange(T, dtype=f32)[:, None] * inv[None, :]
    ang = jnp.concatenate([ang, ang], axis=-1)
    cos, sin = jnp.tile(jnp.cos(ang), (1, 8)), jnp.tile(jnp.sin(ang), (1, 8))
    low = (jnp.arange(HW) % 64 < 32)[None, :]
    return cos, jnp.where(low, -sin, 0.0), jnp.where(low, 0.0, sin)


def _rope(t, cos, sin_lo, sin_hi):
    return t * cos + pltpu.roll(t, HW - 32, 1) * sin_lo + pltpu.roll(t, 32, 1) * sin_hi


def _rope_t(g, cos, sin_lo, sin_hi):
    return g * cos + pltpu.roll(g * sin_lo, 32, 1) + pltpu.roll(g * sin_hi, HW - 32, 1)


def _inproj_fwd(h, par, w_in_t, tabs):
    tm = 512

    def body(h_ref, par_ref, w_ref, cos_ref, slo_ref, shi_ref, ua_ref, va_ref, q_ref, k_ref, v_ref):
        y = _norm_mod(h_ref[...], par_ref[...])[0].astype(bf16)
        proj = lambda c: _nt(y, w_ref[c * HW:(c + 1) * HW, :])
        ua_ref[...] = proj(0)
        va_ref[...] = proj(1)
        cos, slo, shi = cos_ref[...], slo_ref[...], shi_ref[...]
        q_ref[...] = (_rope(proj(2), cos, slo, shi) * 0.125).astype(bf16)
        k_ref[...] = _rope(proj(3), cos, slo, shi).astype(bf16)
        v_ref[...] = proj(4).astype(bf16)

    row = pl.BlockSpec((tm, D), lambda i: (i, 0))
    half = pl.BlockSpec((tm, HW), lambda i: (i, 0))
    return pl.pallas_call(
        body, name="inproj_fwd", grid=(T // tm,),
        in_specs=[row, pl.BlockSpec((8, D), lambda i: (0, 0)), pl.BlockSpec((INW, D), lambda i: (0, 0)), half, half, half],
        out_specs=[half] * 5,
        out_shape=[SDS((T, HW), f32), SDS((T, HW), f32), SDS((T, HW), bf16), SDS((T, HW), bf16), SDS((T, HW), bf16)],
        compiler_params=_cp(("arbitrary",), 48),
    )(h, par, w_in_t, *tabs)


def _head_masks():
    lane = lax.broadcasted_iota(jnp.int32, (1, CH), 1)
    return lane < 64, lane >= 64


def _attn_fwd(q, k, v, d):
    L = T // d
    nb = L // CH

    def body(q_ref, kc_ref, kp_ref, vc_ref, vp_ref, o_ref, lse_ref):
        n = pl.program_id(1)
        row = lax.broadcasted_iota(jnp.int32, (CH, CH), 0)
        col = lax.broadcasted_iota(jnp.int32, (CH, CH), 1)
        mask_c = col <= row
        mask_p = jnp.logical_and(col >= row, n > 0)
        lse_tile = jnp.zeros((CH, CH), f32)
        for hp in range(4):
            sl = slice(hp * CH, (hp + 1) * CH)
            qv, kc, kp, vc, vp = q_ref[:, sl], kc_ref[:, sl], kp_ref[:, sl], vc_ref[:, sl], vp_ref[:, sl]
            o_pair = jnp.zeros((CH, CH), f32)
            for hh, hm in enumerate(_head_masks()):
                qm = jnp.where(hm, qv, jnp.zeros_like(qv))
                sc = jnp.where(mask_c, _nt(qm, kc), -jnp.inf)
                sp = jnp.where(mask_p, _nt(qm, kp), -jnp.inf)
                m = jnp.maximum(jnp.max(sc, axis=-1, keepdims=True), jnp.max(sp, axis=-1, keepdims=True))
                pc, pp = jnp.exp(sc - m), jnp.exp(sp - m)
                den = jnp.sum(pc, axis=-1, keepdims=True) + jnp.sum(pp, axis=-1, keepdims=True)
                acc = _nn(pc.astype(bf16), vc) + _nn(pp.astype(bf16), vp)
                o_pair = jnp.where(hm, acc / den, o_pair)
                lse_tile = jnp.where(col == 2 * hp + hh, m + jnp.log(den), lse_tile)
            o_ref[:, sl] = o_pair
        lse_ref[...] = lse_tile

    cur = pl.BlockSpec((CH, HW), lambda r, n: (n, r))
    prev = pl.BlockSpec((CH, HW), lambda r, n: (jnp.maximum(n - 1, 0), r))
    q2, k2, v2 = (t.reshape(L, d * HW) for t in (q, k, v))
    o, lse = pl.pallas_call(
        body, name="attn_fwd_d%d" % d, grid=(d, nb), in_specs=[cur, cur, prev, cur, prev],
        out_specs=[cur, pl.BlockSpec((CH, CH), lambda r, n: (n, r))],
        out_shape=[SDS((L, d * HW), f32), SDS((L, d * CH), f32)],
        compiler_params=_cp(("arbitrary", "arbitrary"), 32),
    )(q2, k2, k2, v2, v2)
    return o.reshape(T, HW), lse.reshape(T, CH)


def _attn_combine(os_, lses):
    tm = 512

    def body(o0_ref, o1_ref, o2_ref, l0_ref, l1_ref, l2_ref, o_ref, lse_ref):
        l0, l1, l2 = l0_ref[...], l1_ref[...], l2_ref[...]
        m = jnp.maximum(jnp.maximum(l0, l1), l2)
        e = [jnp.exp(l0 - m), jnp.exp(l1 - m), jnp.exp(l2 - m)]
        s = e[0] + e[1] + e[2]
        w = [ei / s for ei in e]
        lse_ref[...] = m + jnp.log(s)
        lo, _ = _head_masks()
        for hp in range(4):
            sl = slice(hp * CH, (hp + 1) * CH)
            acc = jnp.zeros((tm, CH), f32)
            for wp, op_ref in zip(w, (o0_ref, o1_ref, o2_ref)):
                wexp = jnp.where(lo, wp[:, 2 * hp:2 * hp + 1], wp[:, 2 * hp + 1:2 * hp + 2])
                acc = acc + wexp * op_ref[:, sl]
            o_ref[:, sl] = acc.astype(bf16)

    half = pl.BlockSpec((tm, HW), lambda i: (i, 0))
    stat = pl.BlockSpec((tm, CH), lambda i: (i, 0))
    return pl.pallas_call(
        body, name="attn_combine", grid=(T // tm,), in_specs=[half] * 3 + [stat] * 3, out_specs=[half, stat],
        out_shape=[SDS((T, HW), bf16), SDS((T, CH), f32)],
        compiler_params=_cp(("arbitrary",), 32),
    )(*os_, *lses)


def _attn_bwd(q, k, v, mixed, do, lse, d):
    L = T // d
    nb = L // CH

    def body(qc_ref, qn_ref, kc_ref, kp_ref, vc_ref, vp_ref, oc_ref, on_ref, dc_ref, dn_ref, lc_ref, ln_ref,
             dq_ref, dk_ref, dv_ref):
        n = pl.program_id(1)
        row = lax.broadcasted_iota(jnp.int32, (CH, CH), 0)
        col = lax.broadcasted_iota(jnp.int32, (CH, CH), 1)
        mask_c = col <= row
        mask_p = jnp.logical_and(col >= row, n > 0)
        mask_n = jnp.logical_and(col >= row, n < nb - 1)
        lse_c, lse_n = lc_ref[...], ln_ref[...]
        for hp in range(4):
            sl = slice(hp * CH, (hp + 1) * CH)
            qc, qn, kc, kp, vc, vp = qc_ref[:, sl], qn_ref[:, sl], kc_ref[:, sl], kp_ref[:, sl], vc_ref[:, sl], vp_ref[:, sl]
            doc, don = dc_ref[:, sl], dn_ref[:, sl]
            prod_c = doc.astype(f32) * oc_ref[:, sl].astype(f32)
            prod_n = don.astype(f32) * on_ref[:, sl].astype(f32)
            dq_pair = jnp.zeros((CH, CH), f32)
            dk_pair = jnp.zeros((CH, CH), f32)
            dv_pair = jnp.zeros((CH, CH), f32)
            for hh, hm in enumerate(_head_masks()):
                h_idx = 2 * hp + hh
                zero = jnp.zeros_like(qc)
                qcm, qnm = jnp.where(hm, qc, zero), jnp.where(hm, qn, zero)
                dcm, dnm = jnp.where(hm, doc, zero), jnp.where(hm, don, zero)
                delta_c = jnp.sum(jnp.where(hm, prod_c, 0.0), axis=-1, keepdims=True)
                delta_n = jnp.sum(jnp.where(hm, prod_n, 0.0), axis=-1, keepdims=True)
                lc, ln = lse_c[:, h_idx:h_idx + 1], lse_n[:, h_idx:h_idx + 1]
                p = jnp.where(mask_c, jnp.exp(_nt(qcm, kc) - lc), 0.0)
                ds = (p * (_nt(dcm, vc) - delta_c)).astype(bf16)
                dq_h = _nn(ds, kc)
                dk_h = _tn(ds, qcm)
                dv_h = _tn(p.astype(bf16), dcm)
                p = jnp.where(mask_p, jnp.exp(_nt(qcm, kp) - lc), 0.0)
                ds = (p * (_nt(dcm, vp) - delta_c)).astype(bf16)
                dq_h = dq_h + _nn(ds, kp)
                p = jnp.where(mask_n, jnp.exp(_nt(qnm, kc) - ln), 0.0)
                ds = (p * (_nt(dnm, vc) - delta_n)).astype(bf16)
                dk_h = dk_h + _tn(ds, qnm)
                dv_h = dv_h + _tn(p.astype(bf16), dnm)
                dq_pair = jnp.where(hm, dq_h, dq_pair)
                dk_pair = dk_pair + dk_h
                dv_pair = dv_pair + dv_h
            dq_ref[:, sl] = dq_pair
            dk_ref[:, sl] = dk_pair
            dv_ref[:, sl] = dv_pair

    cur = pl.BlockSpec((CH, HW), lambda r, n: (n, r))
    prev = pl.BlockSpec((CH, HW), lambda r, n: (jnp.maximum(n - 1, 0), r))
    nxt = pl.BlockSpec((CH, HW), lambda r, n: (jnp.minimum(n + 1, nb - 1), r))
    ocur = pl.BlockSpec((CH, HW), lambda r, n: (n, 2 * r + 1))
    onxt = pl.BlockSpec((CH, HW), lambda r, n: (jnp.minimum(n + 1, nb - 1), 2 * r + 1))
    scur = pl.BlockSpec((CH, CH), lambda r, n: (n, r))
    snxt = pl.BlockSpec((CH, CH), lambda r, n: (jnp.minimum(n + 1, nb - 1), r))
    q2, k2, v2, do2 = (t.reshape(L, d * HW) for t in (q, k, v, do))
    m2, l2 = mixed.reshape(L, d * 2 * HW), lse.reshape(L, d * CH)
    outs = pl.pallas_call(
        body, name="attn_bwd_d%d" % d, grid=(d, nb),
        in_specs=[cur, nxt, cur, prev, cur, prev, ocur, onxt, cur, nxt, scur, snxt],
        out_specs=[cur] * 3, out_shape=[SDS((L, d * HW), f32)] * 3,
        compiler_params=_cp(("arbitrary", "arbitrary"), 32),
    )(q2, q2, k2, k2, v2, v2, m2, m2, do2, do2, l2, l2)
    return [t.reshape(T, HW) for t in outs]


def _sgu_fwd(ua, va, o, lng, lnb, ws, bs_t):
    tm = 512

    def body(ua_ref, va_ref, o_ref, lg_ref, lb_ref, ws_ref, bs_ref, mix_ref):
        for hd in range(4):
            sl = slice(hd * CH, (hd + 1) * CH)
            w = ws_ref[hd]
            for cc in range(tm // CH):
                rs = slice(cc * CH, (cc + 1) * CH)
                u = _gelu(ua_ref[rs, sl])
                v = _gelu(va_ref[rs, sl])
                vc = v - _rowmean(v)
                vn = vc * lax.rsqrt(_rowmean(vc * vc) + EPS) * lg_ref[:, sl] + lb_ref[:, sl]
                z = _nn(w, vn.astype(bf16)) + bs_ref[:, hd:hd + 1]
                mix_ref[rs, sl] = (u * z).astype(bf16)
        mix_ref[:, HW:] = o_ref[...]

    half = pl.BlockSpec((tm, HW), lambda i: (i, 0))
    vec = pl.BlockSpec((1, HW), lambda i: (0, 0))
    return pl.pallas_call(
        body, name="sgu_fwd", grid=(T // tm,),
        in_specs=[half, half, half, vec, vec, pl.BlockSpec((4, CH, CH), lambda i: (0, 0, 0)),
                  pl.BlockSpec((CH, 4), lambda i: (0, 0))],
        out_specs=pl.BlockSpec((tm, 2 * HW), lambda i: (i, 0)), out_shape=SDS((T, 2 * HW), bf16),
        compiler_params=_cp(("arbitrary",), 32),
    )(ua, va, o, lng, lnb, ws, bs_t)


def _sgu_bwd(ua, va, d_a, lng, lnb, ws, ws_t, bs_t):
    tm = 512

    def body(ua_ref, va_ref, d_ref, lg_ref, lb_ref, ws_ref, wst_ref, bs_ref, dsg_ref, dln_ref, dws_ref, db_ref):
        @pl.when(pl.program_id(0) == 0)
        def _():
            dln_ref[...] = jnp.zeros_like(dln_ref)
            dws_ref[...] = jnp.zeros_like(dws_ref)
            db_ref[...] = jnp.zeros_like(db_ref)

        for hd in range(4):
            sl = slice(hd * CH, (hd + 1) * CH)
            w, wt = ws_ref[hd], wst_ref[hd]
            lg = lg_ref[:, sl]
            for cc in range(tm // CH):
                rs = slice(cc * CH, (cc + 1) * CH)
                xa, xv, dd = ua_ref[rs, sl], va_ref[rs, sl], d_ref[rs, sl]
                u, v = _gelu(xa), _gelu(xv)
                vc = v - _rowmean(v)
                rstd = lax.rsqrt(_rowmean(vc * vc) + EPS)
                xh = vc * rstd
                vnb = (xh * lg + lb_ref[:, sl]).astype(bf16)
                z = _nn(w, vnb) + bs_ref[:, hd:hd + 1]
                dz = dd * u
                dzb = dz.astype(bf16)
                dsg_ref[rs, sl] = (dd * z * _gelu_grad(xa)).astype(bf16)
                dws_ref[hd] += _nt(dzb, vnb)
                db_ref[hd] += dz
                dvn = _nn(wt, dzb)
                dln_ref[0:1, sl] += _colsum(dvn * xh)
                dln_ref[1:2, sl] += _colsum(dvn)
                dxh = dvn * lg
                dv = rstd * (dxh - _rowmean(dxh) - xh * _rowmean(dxh * xh))
                dsg_ref[rs, HW + hd * CH:HW + (hd + 1) * CH] = (dv * _gelu_grad(xv)).astype(bf16)

    half = pl.BlockSpec((tm, HW), lambda i: (i, 0))
    vec = pl.BlockSpec((1, HW), lambda i: (0, 0))
    mat = pl.BlockSpec((4, CH, CH), lambda i: (0, 0, 0))
    return pl.pallas_call(
        body, name="sgu_bwd", grid=(T // tm,),
        in_specs=[half, half, half, vec, vec, mat, mat, pl.BlockSpec((CH, 4), lambda i: (0, 0))],
        out_specs=[pl.BlockSpec((tm, 2 * HW), lambda i: (i, 0)), pl.BlockSpec((8, HW), lambda i: (0, 0)), mat, mat],
        out_shape=[SDS((T, 2 * HW), bf16), SDS((8, HW), f32), SDS((4, CH, CH), f32), SDS((4, CH, CH), f32)],
        compiler_params=_cp(("arbitrary",), 32),
    )(ua, va, d_a, lng, lnb, ws, ws_t, bs_t)


def _outproj_fwd(mixed, w_out, h, par):
    tm = 512

    def body(mix_ref, w_ref, h_ref, par_ref, ho_ref, po_ref):
        p = _nn(mix_ref[...], w_ref[...])
        po_ref[...] = p.astype(bf16)
        ho_ref[...] = h_ref[...] + par_ref[2:3, :] * p

    row = pl.BlockSpec((tm, D), lambda i: (i, 0))
    return pl.pallas_call(
        body, name="outproj_fwd", grid=(T // tm,),
        in_specs=[row, pl.BlockSpec((D, D), lambda i: (0, 0)), row, pl.BlockSpec((8, D), lambda i: (0, 0))],
        out_specs=[row, row], out_shape=[SDS((T, D), f32), SDS((T, D), bf16)],
        compiler_params=_cp(("arbitrary",), 32),
    )(mixed, w_out, h, par)


def _outproj_bwd(dh, po, w_out, par):
    tm = 512

    def body(dh_ref, po_ref, w_ref, par_ref, do_ref, da_ref, db_ref, dpar_ref):
        @pl.when(pl.program_id(0) == 0)
        def _():
            dpar_ref[...] = jnp.zeros_like(dpar_ref)

        dh_v = dh_ref[...]
        dob = (par_ref[2:3, :] * dh_v).astype(bf16)
        do_ref[...] = dob
        dpar_ref[2:3, :] += _colsum(dh_v * po_ref[...].astype(f32))
        dm = _nt(dob, w_ref[...])
        da_ref[...] = dm[:, :HW]
        db_ref[...] = dm[:, HW:].astype(bf16)

    row = pl.BlockSpec((tm, D), lambda i: (i, 0))
    half = pl.BlockSpec((tm, HW), lambda i: (i, 0))
    one = pl.BlockSpec((8, D), lambda i: (0, 0))
    return pl.pallas_call(
        body, name="outproj_bwd", grid=(T // tm,),
        in_specs=[row, row, pl.BlockSpec((D, D), lambda i: (0, 0)), one],
        out_specs=[row, half, half, one],
        out_shape=[SDS((T, D), bf16), SDS((T, HW), f32), SDS((T, HW), bf16), SDS((8, D), f32)],
        compiler_params=_cp(("arbitrary",), 32),
    )(dh, po, w_out, par)


def _inproj_bwd_tok(dh, h, par, w_in_t, dsg, dqs, dks, dvs, tabs):
    tm = 256

    def body(dh_ref, h_ref, par_ref, w_ref, dsg_ref, dq0, dq1, dq2, dk0, dk1, dk2, dv0, dv1, dv2,
             cos_ref, slo_ref, shi_ref, dhin_ref, dp_ref, y_ref, dpar_ref):
        @pl.when(pl.program_id(0) == 0)
        def _():
            dpar_ref[...] = jnp.zeros_like(dpar_ref)

        cos, slo, shi = cos_ref[...], slo_ref[...], shi_ref[...]
        dq = (dq0[...] + dq1[...] + dq2[...]) * 0.125
        dk = dk0[...] + dk1[...] + dk2[...]
        dp_ref[:, :2 * HW] = dsg_ref[...]
        dp_ref[:, 2 * HW:3 * HW] = _rope_t(dq, cos, slo, shi).astype(bf16)
        dp_ref[:, 3 * HW:4 * HW] = _rope_t(dk, cos, slo, shi).astype(bf16)
        dp_ref[:, 4 * HW:] = (dv0[...] + dv1[...] + dv2[...]).astype(bf16)
        dy = _nn(dp_ref[...], w_ref[...])
        par_v, h_v = par_ref[...], h_ref[...]
        y_ref[...] = _norm_mod(h_v, par_v)[0].astype(bf16)
        dx, dsh, dsc, dng = _norm_mod_bwd(dy, h_v, par_v)
        dhin_ref[...] = dh_ref[...] + dx
        dpar_ref[0:1, :] += dsh
        dpar_ref[1:2, :] += dsc
        dpar_ref[3:4, :] += dng

    row = pl.BlockSpec((tm, D), lambda i: (i, 0))
    half = pl.BlockSpec((tm, HW), lambda i: (i, 0))
    one = pl.BlockSpec((8, D), lambda i: (0, 0))
    return pl.pallas_call(
        body, name="inproj_bwd_tok", grid=(T // tm,),
        in_specs=[row, row, one, pl.BlockSpec((INW, D), lambda i: (0, 0)), row] + [half] * 12,
        out_specs=[row, pl.BlockSpec((tm, INW), lambda i: (i, 0)), row, one],
        out_shape=[SDS((T, D), f32), SDS((T, INW), bf16), SDS((T, D), bf16), SDS((8, D), f32)],
        compiler_params=_cp(("arbitrary",), 48),
    )(dh, h, par, w_in_t, dsg, *dqs, *dks, *dvs, *tabs)


def _loss_head(h, par, target):
    tm = 512

    def body(h_ref, par_ref, t_ref, dh_ref, acc_ref):
        @pl.when(pl.program_id(0) == 0)
        def _():
            acc_ref[...] = jnp.zeros_like(acc_ref)

        x, g = h_ref[...], par_ref[3:4, :]
        rstd = lax.rsqrt(_rowmean(x * x) + EPS)
        xhat = x * rstd
        err = xhat * g - t_ref[...]
        acc_ref[0:1, :] += _colsum(err * err)
        dy = err * (1.0 / D)
        acc_ref[3:4, :] += _colsum(dy * xhat)
        dxhat = dy * g
        dh_ref[...] = rstd * (dxhat - xhat * _rowmean(dxhat * xhat))

    row = pl.BlockSpec((tm, D), lambda i: (i, 0))
    one = pl.BlockSpec((8, D), lambda i: (0, 0))
    return pl.pallas_call(
        body, name="loss_head", grid=(T // tm,), in_specs=[row, one, row], out_specs=[row, one],
        out_shape=[SDS((T, D), f32), SDS((8, D), f32)], compiler_params=_cp(("arbitrary",), 32),
    )(h, par, target)


ELEMENTWISE_VMEM_BUDGET = 20 << 20


def _block_rows(rows, bytes_per_row):
    cap = ELEMENTWISE_VMEM_BUDGET // bytes_per_row
    if rows <= cap:
        return rows
    return next(b for b in range(cap - cap % 16, 0, -16) if rows % b == 0)


def _sum_slots(land, name):
    _, R, C = land.shape
    br = _block_rows(R, 2 * NDEV * C * land.dtype.itemsize + 2 * C * 4)

    def body(l_ref, o_ref):
        acc = l_ref[0].astype(f32)
        for s in range(1, NDEV):
            acc = acc + l_ref[s].astype(f32)
        o_ref[...] = acc

    return pl.pallas_call(
        body, name=name, grid=(R // br,), in_specs=[pl.BlockSpec((NDEV, br, C), lambda i: (0, i, 0))],
        out_specs=pl.BlockSpec((br, C), lambda i: (i, 0)), out_shape=SDS((R, C), f32),
        compiler_params=_cp(("arbitrary",), 32),
    )(land)


def _sum_partials(land, own, me, name):
    r = land.shape[-2]
    land4, own4 = land.reshape(NDEV, -1, r, D), own.reshape(-1, NDEV, r, D)
    kk = own4.shape[0]

    def body(me_ref, l_ref, own_ref, o_ref):
        mine = own_ref[...].astype(f32)
        acc = jnp.where(me_ref[0] == 0, mine, l_ref[0].astype(f32))
        for s in range(1, NDEV):
            acc = acc + jnp.where(me_ref[0] == s, mine, l_ref[s].astype(f32))
        o_ref[...] = acc

    out = pl.pallas_call(
        body, name=name, out_shape=SDS((kk, r, D), f32),
        grid_spec=pltpu.PrefetchScalarGridSpec(
            num_scalar_prefetch=1, grid=(kk,),
            in_specs=[pl.BlockSpec((NDEV, None, r, D), lambda i, me_ref: (0, i, 0, 0)),
                      pl.BlockSpec((None, None, r, D), lambda i, me_ref: (i, me_ref[0], 0, 0))],
            out_specs=pl.BlockSpec((None, r, D), lambda i, me_ref: (i, 0, 0))),
        compiler_params=_cp(("arbitrary",), 32),
    )(me.reshape(1), land4, own4)
    return out.reshape(own.shape[:-2] + (r, D))


def _adamw(w, g, m, v, name):
    R, C = w.shape
    br = _block_rows(R, 2 * 7 * C * 4)

    def body(w_ref, g_ref, m_ref, v_ref, d_ref, mo_ref, vo_ref):
        gv = g_ref[...]
        m2 = B1 * m_ref[...] + (1.0 - B1) * gv
        v2 = B2 * v_ref[...] + (1.0 - B2) * (gv * gv)
        mo_ref[...] = m2
        vo_ref[...] = v2
        m_hat = m2 / (1.0 - B1 ** STEP)
        v_hat = v2 / (1.0 - B2 ** STEP)
        d_ref[...] = -LR * (m_hat / (jnp.sqrt(v_hat) + AEPS) + WD * w_ref[...])

    blk = pl.BlockSpec((br, C), lambda i: (i, 0))
    return pl.pallas_call(
        body, name=name, grid=(R // br,), in_specs=[blk] * 4, out_specs=[blk] * 3,
        out_shape=[SDS((R, C), f32)] * 3, compiler_params=_cp(("arbitrary",), 32),
    )(w, g, m, v)


def _adamw_nd(w, g, m, v, name):
    shp = w.shape
    r2 = (-1, shp[-1]) if w.ndim > 1 else (8, shp[0] // 8)
    outs = _adamw(w.reshape(r2), g.reshape(r2), m.reshape(r2), v.reshape(r2), name)
    return [o.reshape(shp) for o in outs]


def _par_rows(mod_l, s, gain):
    rows = jnp.pad(mod_l.reshape(9, D)[3 * s:3 * s + 3], ((0, 5), (0, 0)))
    return rows + jnp.pad(gain[None, :], ((3, 4), (0, 0)))


def _pad_rows(a):
    a = a.reshape(-1, CH)
    return jnp.pad(a, ((0, (-a.shape[0]) % 8), (0, 0)))


def _prepare(c, ada_w, ada_b, norm_g):
    me = 4 * lax.axis_index("x") + 2 * lax.axis_index("y") + lax.axis_index("c")

    pay = jnp.pad(c, ((0, 7), (0, 0)))
    pay = jnp.concatenate([pay, jnp.pad(norm_g.reshape(6, OUTS), ((0, 2), (0, D - OUTS)))], axis=0)
    got = _all_gather_small(pay, "gather_c")
    c_all = got[:, 0, :]
    gains = got[:, 8:14, :OUTS].transpose(1, 0, 2).reshape(DEPTH, 3, D)

    ada_b_loc = lax.dynamic_slice(ada_b, (0, me * ADAS), (DEPTH, ADAS)).reshape(DEPTH, 1, ADAS)
    mod_cols = _mod_fwd(c_all, ada_w, ada_b_loc)
    got = _all_gather_small(mod_cols.reshape(DEPTH * NDEV, ADAS), "gather_mod").reshape(NDEV, DEPTH, NDEV, ADAS)
    mod = lax.dynamic_index_in_dim(got, me, axis=2, keepdims=False).transpose(1, 0, 2).reshape(DEPTH, 9 * D)
    pars = [[_par_rows(mod[l], s, gains[l, s]) for s in range(3)] for l in range(DEPTH)]
    return me, c_all, pars


def _fwd_bwd(x2, target, pars, get_w, put_g, sgu_ln_g, sgu_ln_b, sgu_w, sgu_b, final_g):
    tabs = _rope_tables()
    tril = jnp.tril(jnp.ones((CH, CH), dtype=bool))
    ws_m = jnp.where(tril[None, None], sgu_w, 0.0).astype(bf16)
    ws_mt = jnp.swapaxes(ws_m, -1, -2)

    h = x2
    saved = []
    for l in range(DEPTH):
        lng, lnb = sgu_ln_g[l].reshape(1, HW), sgu_ln_b[l].reshape(1, HW)
        bs_t = sgu_b[l].T
        h0 = h
        h1, fo1, g1, u1, a1 = _ffn_fwd(h0, pars[l][0], get_w(4 * l, h0))
        ua, va, q, k, v = _inproj_fwd(h1, pars[l][1], get_w(4 * l + 2, h1), tabs)
        branches = [_attn_fwd(q, k, v, d) for d in PATTERN_DILATIONS]
        o, lse = _attn_combine([b[0] for b in branches], [b[1] for b in branches])
        mixed = _sgu_fwd(ua, va, o, lng, lnb, ws_m[l], bs_t)
        h2, po = _outproj_fwd(mixed, get_w(4 * l + 3, mixed), h1, pars[l][1])
        h3, fo2, g2, u2, a2 = _ffn_fwd(h2, pars[l][2], get_w(4 * l + 1, h2))
        saved.append((h0, h1, h2, fo1, g1, u1, a1, ua, va, q, k, v, mixed, lse, po, fo2, g2, u2, a2))
        h = h3

    par_f = jnp.pad(final_g[None, :], ((3, 4), (0, 0)))
    dh, head = _loss_head(h, par_f, target)

    dmods, dgains, dsgu = [None] * DEPTH, [None] * DEPTH, [None] * DEPTH
    hold = 0.0
    for l in reversed(range(DEPTH)):
        w_f1, w_f2, w_i, w_o = (get_w(4 * l + j, None) for j in (0, 1, 2, 3))
        h0, h1, h2, fo1, g1, u1, a1, ua, va, q, k, v, mixed, lse, po, fo2, g2, u2, a2 = saved[l]
        lng, lnb = sgu_ln_g[l].reshape(1, HW), sgu_ln_b[l].reshape(1, HW)
        bs_t = sgu_b[l].T

        dh, dg_s, du_s, y, dfb, dpar3 = _ffn_bwd_tok(dh, h2, pars[l][2] + hold, fo2, g2, u2, w_f2)
        hold = put_g(4 * l + 1, _ffn_bwd_w(a2, dg_s, du_s, y, dfb))

        dob, d_a, d_b, dpar2g = _outproj_bwd(dh, po, w_o, pars[l][1] + hold)
        hold = put_g(4 * l + 3, _tn_matmul(mixed, dob, 512, "w_out_grad"))
        parts = [_attn_bwd(q, k, v, mixed, d_b, lse, d) for d in PATTERN_DILATIONS]
        dsg, dln, dws, dbl = _sgu_bwd(ua, va, d_a, lng, lnb, ws_m[l], ws_mt[l], bs_t)
        dh, dp, y2, dpar2 = _inproj_bwd_tok(dh, h1, pars[l][1] + hold, w_i, dsg, [p[0] for p in parts],
                                            [p[1] for p in parts], [p[2] for p in parts], tabs)
        hold = put_g(4 * l + 2, _tn_matmul(dp, y2, 640, "w_in_grad"))

        dh, dg_s, du_s, y, dfb, dpar1 = _ffn_bwd_tok(dh, h0, pars[l][0] + hold, fo1, g1, u1, w_f1)
        hold = put_g(4 * l, _ffn_bwd_w(a1, dg_s, du_s, y, dfb))

        dmods[l] = jnp.concatenate([dpar1[0:3], dpar2[0:2], dpar2g[2:3], dpar3[0:3]], axis=0).reshape(9 * D)
        dgains[l] = jnp.stack([dpar1[3], dpar2[3], dpar3[3]])
        dsgu[l] = (dln[0], dln[1], jnp.where(tril[None], dws, 0.0), jnp.sum(dbl, axis=-1))
    return dh, head, dmods, dgains, dsgu


def kernel(x, c, ada_w, ada_b, norm_g, ffn1_wg, ffn1_wu, ffn1_wd, ffn2_wg, ffn2_wu, ffn2_wd, w_in, sgu_ln_g, sgu_ln_b, sgu_w, sgu_b, w_out, final_g, loss_target, m_ada_w, m_ada_b, m_norm_g, m_ffn1_wg, m_ffn1_wu, m_ffn1_wd, m_ffn2_wg, m_ffn2_wu, m_ffn2_wd, m_w_in, m_sgu_ln_g, m_sgu_ln_b, m_sgu_w, m_sgu_b, m_w_out, m_final_g, v_ada_w, v_ada_b, v_norm_g, v_ffn1_wg, v_ffn1_wu, v_ffn1_wd, v_ffn2_wg, v_ffn2_wu, v_ffn2_wd, v_w_in, v_sgu_ln_g, v_sgu_ln_b, v_sgu_w, v_sgu_b, v_w_out, v_final_g):
    me, c_all, pars = _prepare(c, ada_w, ada_b, norm_g)

    tr = lambda w: jnp.swapaxes(w, -1, -2).astype(bf16)
    locs = []
    for l in range(DEPTH):
        locs.append(jnp.stack([tr(ffn1_wg[l]), tr(ffn1_wu[l]), ffn1_wd[l].astype(bf16)]))
        locs.append(jnp.stack([tr(ffn2_wg[l]), tr(ffn2_wu[l]), ffn2_wd[l].astype(bf16)]))
        locs.append(tr(w_in[l]))
        locs.append(w_out[l].astype(bf16))
    locs, pars = lax.optimization_barrier((locs, pars))

    def in_place(loc):
        full = lax.empty(loc.shape[:-2] + (NDEV * loc.shape[-2], D), bf16)
        return lax.dynamic_update_slice_in_dim(full, loc, me * loc.shape[-2], axis=loc.ndim - 2)

    ready = {0: _all_gather_rows([locs[0]], "gather_first")[0]}
    groups = ([2, 3], [1], [4, 6, 7], [5])
    flying, hold = {}, 0.0
    for gi, pieces in enumerate(groups):
        started = _exchange_start(False, [in_place(locs[p]) for p in pieces], "gather_start_%d" % gi)
        flying.update({p: (gi, pieces, started[:-1]) for p in pieces})
        hold = hold + started[-1][0, 0]
    pars[0][0] = pars[0][0] + hold

    def get_w(piece, after):
        if piece not in ready:
            gi, pieces, started = flying[piece]
            ready.update(zip(pieces, _exchange_wait(False, started, after, "gather_wait_%d" % gi)))
        return ready[piece]

    sent = {}

    def put_g(piece, grad):
        land = lax.empty((NDEV,) + grad.shape[:-2] + (grad.shape[-2] // NDEV, D), bf16)
        started = _exchange_start(True, [grad, land], "scatter_start_%d" % piece)
        sent[piece] = started[:-1]
        return started[-1][0, 0]

    dh, head, dmods, dgains, dsgu = _fwd_bwd(x[0], loss_target[0], pars, get_w, put_g, sgu_ln_g, sgu_ln_b, sgu_w, sgu_b,
                                             final_g)
    grad_x = dh[None]

    loss_part = 0.5 * jnp.sum(head[0]) / D
    small = jnp.concatenate([
        _pad_rows(jnp.stack(dmods)), _pad_rows(jnp.stack(dgains)),
        _pad_rows(jnp.stack([s[0] for s in dsgu])), _pad_rows(jnp.stack([s[1] for s in dsgu])),
        _pad_rows(jnp.stack([s[3] for s in dsgu])), _pad_rows(jnp.stack([s[2] for s in dsgu])),
        _pad_rows(head[3]), _pad_rows(jnp.pad(loss_part[None], (0, CH - 1)))], axis=0)
    got = _all_gather_small(small, "gather_small")
    tot = _sum_slots(got, "sum_small")
    n_mod, n_gain, n_sw = DEPTH * 9 * D // CH, DEPTH * 3 * D // CH, DEPTH * 4 * CH
    offs = [0, n_mod, n_mod + n_gain, n_mod + n_gain + 8, n_mod + n_gain + 16, n_mod + n_gain + 24]
    g_ada_b = tot[offs[0]:offs[1]].reshape(DEPTH, 9 * D)
    g_gain_full = tot[offs[1]:offs[2]].reshape(DEPTH, 3, D)
    g_ln_g = tot[offs[2]:offs[3]].reshape(DEPTH, 4, CH)
    g_ln_b = tot[offs[3]:offs[4]].reshape(DEPTH, 4, CH)
    g_sb = tot[offs[4]:offs[5]].reshape(DEPTH, 4, CH)
    g_sw = tot[offs[5]:offs[5] + n_sw].reshape(DEPTH, 4, CH, CH)
    g_final = tot[offs[5] + n_sw:offs[5] + n_sw + 8].reshape(D)
    loss = tot[offs[5] + n_sw + 8, 0]
    g_norm = lax.dynamic_slice(g_gain_full, (0, 0, me * OUTS), (DEPTH, 3, OUTS))

    dmod_all = got[:, offs[0]:offs[1]].reshape(NDEV, DEPTH, 9 * D)
    dmod_cols = lax.dynamic_slice(dmod_all, (0, 0, me * ADAS), (NDEV, DEPTH, ADAS)).transpose(1, 0, 2)
    g_ada_w = _ada_grad((c_all * jax.nn.sigmoid(c_all)).T, dmod_cols)

    sums, after = {}, tot

    def collect(piece, after):
        own, land = _exchange_wait(True, sent[piece], after, "scatter_wait_%d" % piece)
        sums[piece] = _sum_partials(land, own, me, "sum_grads")
        return sums[piece]

    for piece in (5, 7, 6, 4, 1, 3, 2):
        after = collect(piece, after)
    back = lambda t: jnp.swapaxes(t, -1, -2)
    f2 = jnp.stack([sums[4 * l + 1] for l in range(DEPTH)])
    g_w_in = back(jnp.stack([sums[4 * l + 2] for l in range(DEPTH)]))
    g_w_out = jnp.stack([sums[4 * l + 3] for l in range(DEPTH)])

    ws = [ada_w, ada_b, norm_g, ffn1_wg, ffn1_wu, ffn1_wd, ffn2_wg, ffn2_wu, ffn2_wd, w_in, sgu_ln_g, sgu_ln_b, sgu_w,
          sgu_b, w_out, final_g]
    ms = [m_ada_w, m_ada_b, m_norm_g, m_ffn1_wg, m_ffn1_wu, m_ffn1_wd, m_ffn2_wg, m_ffn2_wu, m_ffn2_wd, m_w_in,
          m_sgu_ln_g, m_sgu_ln_b, m_sgu_w, m_sgu_b, m_w_out, m_final_g]
    vs = [v_ada_w, v_ada_b, v_norm_g, v_ffn1_wg, v_ffn1_wu, v_ffn1_wd, v_ffn2_wg, v_ffn2_wu, v_ffn2_wd, v_w_in,
          v_sgu_ln_g, v_sgu_ln_b, v_sgu_w, v_sgu_b, v_w_out, v_final_g]
    gw = [g_ada_w, g_ada_b, g_norm, None, None, None, back(f2[:, 0]), back(f2[:, 1]), f2[:, 2],
          g_w_in, g_ln_g, g_ln_b, g_sw, g_sb, g_w_out, g_final]
    upd = [None] * len(ws)
    for i in (1, 2, 10, 11, 12, 13, 15, 6, 7, 8, 9, 14, 0):
        upd[i] = _adamw_nd(ws[i], gw[i], ms[i], vs[i], "adamw")
    collect(0, upd[0][0])
    f1 = jnp.stack([sums[4 * l] for l in range(DEPTH)])
    gw[3:6] = [back(f1[:, 0]), back(f1[:, 1]), f1[:, 2]]
    for i in (3, 4, 5):
        upd[i] = _adamw_nd(ws[i], gw[i], ms[i], vs[i], "adamw")
    return (loss, grad_x, *gw, *[u[0] for u in upd], *[u[1] for u in upd], *[u[2] for u in upd])
```

```python
import functools
import math

import jax
import jax.numpy as jnp
from jax import lax
from jax.experimental import pallas as pl
from jax.experimental.pallas import tpu as pltpu

f32, bf16 = jnp.float32, jnp.bfloat16
SDS = jax.ShapeDtypeStruct

T, D, F = 4096, 1024, 2816
NDEV, DEPTH = 8, 2
HW = 512
INW = 5 * HW
FS, INS, OUTS, ADAS = F // NDEV, INW // NDEV, D // NDEV, 9 * D // NDEV
CH = 128
PATTERN_DILATIONS = (1, 4, 16)
ROPE_THETA = 10000.0
EPS = 1e-6
LR, B1, B2, AEPS, WD, STEP = 0.001, 0.9, 0.999, 1e-08, 0.01, 10
MESH = pl.DeviceIdType.MESH


def _cp(sems, vmem_mb):
    return pltpu.CompilerParams(dimension_semantics=sems, vmem_limit_bytes=vmem_mb << 20)


def _nn(a, b):
    return lax.dot_general(a, b, (((1,), (0,)), ((), ())), preferred_element_type=f32)


def _nt(a, b):
    return lax.dot_general(a, b, (((1,), (1,)), ((), ())), preferred_element_type=f32)


def _tn(a, b):
    return lax.dot_general(a, b, (((0,), (0,)), ((), ())), preferred_element_type=f32)


def _colsum(a):
    return jnp.sum(a, axis=0, keepdims=True)


def _rowmean(a):
    return jnp.mean(a, axis=-1, keepdims=True)


def _norm_mod(x, par):
    rstd = lax.rsqrt(_rowmean(x * x) + EPS)
    xhat = x * rstd
    n = xhat * par[3:4, :]
    y = n * (1.0 + par[1:2, :]) + par[0:1, :]
    return y, n, xhat, rstd


def _norm_mod_bwd(dy, x, par):
    _, n, xhat, rstd = _norm_mod(x, par)
    dn = dy * (1.0 + par[1:2, :])
    dxhat = dn * par[3:4, :]
    dx = rstd * (dxhat - xhat * _rowmean(dxhat * xhat))
    return dx, _colsum(dy), _colsum(dy * n), _colsum(dn * xhat)


_GK = math.sqrt(2.0 / math.pi)


def _gelu(x):
    return 0.5 * x * (1.0 + jnp.tanh(_GK * (x + 0.044715 * x * x * x)))


def _gelu_grad(x):
    t = jnp.tanh(_GK * (x + 0.044715 * x * x * x))
    return 0.5 * (1.0 + t) + 0.5 * x * (1.0 - t * t) * (_GK * (1.0 + 3.0 * 0.044715 * x * x))


def _rows(ref, idx, r):
    if len(ref.shape) == 3:
        return ref.at[:, pl.ds(idx * r, r), :]
    return ref.at[pl.ds(idx * r, r), :]


def _flip(v, bit):
    return 1 - v if bit else v


def _all_gather_small(x, name):
    R, C = x.shape

    def body(x_ref, out_ref, send_sems, recv_sems):
        mx, my, mc = lax.axis_index("x"), lax.axis_index("y"), lax.axis_index("c")
        me = 4 * mx + 2 * my + mc
        out_ref[me] = x_ref[...]
        sent = []
        for k in range(1, NDEV):
            peer = (_flip(mx, k & 4), _flip(my, k & 2), _flip(mc, k & 1))
            cp = pltpu.make_async_remote_copy(
                src_ref=x_ref, dst_ref=out_ref.at[me], send_sem=send_sems.at[k - 1],
                recv_sem=recv_sems.at[k - 1], device_id=peer, device_id_type=MESH)
            cp.start()
            sent.append(cp)
        for k in range(1, NDEV):
            peer = (_flip(mx, k & 4), _flip(my, k & 2), _flip(mc, k & 1))
            pidx = 4 * peer[0] + 2 * peer[1] + peer[2]
            pltpu.make_async_remote_copy(
                src_ref=x_ref, dst_ref=out_ref.at[pidx], send_sem=send_sems.at[k - 1],
                recv_sem=recv_sems.at[k - 1], device_id=peer, device_id_type=MESH).wait_recv()
        for cp in sent:
            cp.wait_send()

    vm = pl.BlockSpec(memory_space=pltpu.VMEM)
    return pl.pallas_call(
        body, name=name, out_shape=SDS((NDEV, R, C), f32), in_specs=[vm], out_specs=vm,
        scratch_shapes=[pltpu.SemaphoreType.DMA((NDEV - 1,)), pltpu.SemaphoreType.DMA((NDEV - 1,))],
        compiler_params=pltpu.CompilerParams(vmem_limit_bytes=32 << 20),
    )(x)


def _all_gather_rows(locs, name):
    n = len(locs)
    rs = [a.shape[-2] for a in locs]

    def body(*refs):
        src, out = refs[:n], refs[n:2 * n]
        send_sems, recv_sems, loc_sems = refs[2 * n:]
        mx, my, mc = lax.axis_index("x"), lax.axis_index("y"), lax.axis_index("c")
        me, sib = (mx, my, mc), (mx, my, 1 - mc)
        chips = [(1 - mx, my), (mx, 1 - my), (1 - mx, 1 - my)]

        def blk(a, p):
            return _rows(out[a], 4 * p[0] + 2 * p[1] + p[2], rs[a])

        def copy(k, a, block, to, from_src=False):
            return pltpu.make_async_remote_copy(
                src_ref=src[a] if from_src else blk(a, block), dst_ref=blk(a, block),
                send_sem=send_sems.at[k * n + a], recv_sem=recv_sems.at[k * n + a],
                device_id=to, device_id_type=MESH)

        mine = [pltpu.make_async_copy(src[a], blk(a, me), loc_sems.at[a]) for a in range(n)]
        for m in mine:
            m.start()
        first = []
        for j, chip in enumerate(chips):
            first += [copy(1 + j, a, me, (*chip, mc), True) for a in range(n)]
        first += [copy(0, a, me, sib, True) for a in range(n)]
        for cp in first:
            cp.start()
        passed = []
        for j, chip in enumerate(chips):
            for a in range(n):
                copy(1 + j, a, (*chip, mc), me).wait_recv()
            fwd = [copy(4 + j, a, (*chip, mc), sib) for a in range(n)]
            for cp in fwd:
                cp.start()
            passed += fwd
        for a in range(n):
            copy(0, a, sib, me).wait_recv()
        for j, chip in enumerate(chips):
            for a in range(n):
                copy(4 + j, a, (*chip, 1 - mc), me).wait_recv()
        for cp in first + passed:
            cp.wait_send()
        for m in mine:
            m.wait()

    hbm = pl.BlockSpec(memory_space=pl.ANY)
    out_shape = [SDS(a.shape[:-2] + (NDEV * a.shape[-2], a.shape[-1]), a.dtype) for a in locs]
    return pl.pallas_call(
        body, name=name, out_shape=out_shape, in_specs=[hbm] * n, out_specs=[hbm] * n,
        scratch_shapes=[pltpu.SemaphoreType.DMA((7 * n,)), pltpu.SemaphoreType.DMA((7 * n,)),
                        pltpu.SemaphoreType.DMA((n,))],
    )(*locs)


HBM_SPEC = pl.BlockSpec(memory_space=pltpu.HBM)
SEM_SPEC = pl.BlockSpec(memory_space=pltpu.SEMAPHORE)
DATAFLOW_EFFECT = pltpu.SideEffectType.DATAFLOW_SIDE_EFFECTING


def _place_own(loc, me, name="place_own"):
    r, cols = loc.shape[-2:]
    loc3 = loc.reshape(-1, r, cols)
    kk = loc3.shape[0]

    def body(me_ref, src_ref, full_ref, out_ref):
        out_ref[...] = src_ref[...]

    out = pl.pallas_call(
        body, name=name, out_shape=SDS((kk, NDEV * r, cols), loc.dtype),
        grid_spec=pltpu.PrefetchScalarGridSpec(
            num_scalar_prefetch=1, grid=(kk,),
            in_specs=[pl.BlockSpec((None, r, cols), lambda i, me_ref: (i, 0, 0)), pl.BlockSpec(memory_space=pl.ANY)],
            out_specs=pl.BlockSpec((None, r, cols), lambda i, me_ref: (i, me_ref[0], 0))),
        input_output_aliases={2: 0}, compiler_params=_cp(("arbitrary",), 32),
    )(me.reshape(1), loc3, lax.empty((kk, NDEV * r, cols), loc.dtype))
    return out.reshape(loc.shape[:-2] + (NDEV * r, cols))


def _exchange_copies(scatter, bufs, n, send_sems, recv_sems):
    mx, my, mc = lax.axis_index("x"), lax.axis_index("y"), lax.axis_index("c")
    me = 4 * mx + 2 * my + mc
    out = []
    for k in range(1, NDEV):
        peer = (_flip(mx, k & 4), _flip(my, k & 2), _flip(mc, k & 1))
        pidx = 4 * peer[0] + 2 * peer[1] + peer[2]
        for a in range(n):
            r = bufs[a].shape[-2] // NDEV
            if scatter:
                src, dst, arrive = _rows(bufs[a], pidx, r), bufs[n + a].at[me], bufs[n + a].at[pidx]
            else:
                src, dst, arrive = _rows(bufs[a], me, r), _rows(bufs[a], me, r), _rows(bufs[a], pidx, r)
            sems = dict(send_sem=send_sems.at[(k - 1) * n + a], recv_sem=recv_sems.at[(k - 1) * n + a],
                        device_id=peer, device_id_type=MESH)
            out.append((pltpu.make_async_remote_copy(src_ref=src, dst_ref=dst, **sems),
                        pltpu.make_async_remote_copy(src_ref=src, dst_ref=arrive, **sems)))
    return out


def _exchange_start(scatter, arrays, name):
    m = len(arrays)
    n = m // 2 if scatter else m

    def body(*refs):
        send_sems, recv_sems, token = refs[m], refs[m + 1], refs[-1]
        for go, _ in _exchange_copies(scatter, refs[:m], n, send_sems, recv_sems):
            go.start()
        token[...] = jnp.zeros_like(token)

    sems = pltpu.SemaphoreType.DMA((7 * n,))
    return pl.pallas_call(
        body, name=name, in_specs=[HBM_SPEC] * m,
        out_shape=(sems, sems, *[pltpu.HBM(a.shape, a.dtype) for a in arrays], SDS((8, CH), f32)),
        out_specs=(SEM_SPEC, SEM_SPEC, *[HBM_SPEC] * m, pl.BlockSpec(memory_space=pltpu.VMEM)),
        input_output_aliases={a: 2 + a for a in range(m)},
        compiler_params=pltpu.CompilerParams(has_side_effects=DATAFLOW_EFFECT),
    )(*[pltpu.with_memory_space_constraint(a, pltpu.HBM) for a in arrays])


def _exchange_wait(scatter, started, after, name):
    send_sems, recv_sems, *arrays = started
    m = len(arrays)
    n = m // 2 if scatter else m

    def body(*refs):
        for go, arrive in _exchange_copies(scatter, refs[:m], n, refs[m], refs[m + 1]):
            go.wait_send()
            arrive.wait_recv()

    return pl.pallas_call(
        body, name=name, in_specs=[HBM_SPEC] * m + [SEM_SPEC, SEM_SPEC, pl.BlockSpec(memory_space=pl.ANY)],
        out_shape=[pltpu.HBM(a.shape, a.dtype) for a in arrays], out_specs=[HBM_SPEC] * m,
        input_output_aliases={a: a for a in range(m)},
        compiler_params=pltpu.CompilerParams(has_side_effects=DATAFLOW_EFFECT),
    )(*arrays, send_sems, recv_sems, after)


def _mod_fwd(c_all, ada_w, ada_b_loc):
    def body(c_ref, w_ref, b_ref, o_ref):
        ca = c_ref[...]
        ca = ca * jax.nn.sigmoid(ca)
        o_ref[...] = jnp.dot(ca, w_ref[...], precision=lax.Precision.HIGHEST,
                             preferred_element_type=f32) + b_ref[...]

    return pl.pallas_call(
        body, name="mod_fwd", grid=(DEPTH,), out_shape=SDS((DEPTH, NDEV, ADAS), f32),
        in_specs=[pl.BlockSpec((NDEV, D), lambda l: (0, 0)),
                  pl.BlockSpec((None, D, ADAS), lambda l: (l, 0, 0)),
                  pl.BlockSpec((None, 1, ADAS), lambda l: (l, 0, 0))],
        out_specs=pl.BlockSpec((None, NDEV, ADAS), lambda l: (l, 0, 0)),
        compiler_params=_cp(("arbitrary",), 32),
    )(c_all, ada_w, ada_b_loc)


def _ada_grad(cact_t, dmod_cols):
    def body(c_ref, d_ref, o_ref):
        acc = c_ref[:, 0:1] * d_ref[0:1, :]
        for b in range(1, NDEV):
            acc = acc + c_ref[:, b:b + 1] * d_ref[b:b + 1, :]
        o_ref[...] = acc

    tr = 256
    return pl.pallas_call(
        body, name="ada_grad", grid=(DEPTH, D // tr), out_shape=SDS((DEPTH, D, ADAS), f32),
        in_specs=[pl.BlockSpec((tr, NDEV), lambda l, i: (i, 0)),
                  pl.BlockSpec((None, NDEV, ADAS), lambda l, i: (l, 0, 0))],
        out_specs=pl.BlockSpec((None, tr, ADAS), lambda l, i: (l, i, 0)),
        compiler_params=_cp(("arbitrary", "arbitrary"), 32),
    )(cact_t, dmod_cols)


def _ffn_fwd(h, par, w3):
    tm, tf = 1024, 256
    nj = F // tf

    def body(h_ref, par_ref, wg_ref, wu_ref, wd_ref, ho_ref, fo_ref, g_ref, u_ref, a_ref, y_scr, acc):
        j = pl.program_id(1)

        @pl.when(j == 0)
        def _():
            y_scr[...] = _norm_mod(h_ref[...], par_ref[...])[0].astype(bf16)
            acc[...] = jnp.zeros_like(acc)

        y = y_scr[...]
        g = _nt(y, wg_ref[...])
        u = _nt(y, wu_ref[...])
        a = ((g * jax.nn.sigmoid(g)) * u).astype(bf16)
        g_ref[...] = g.astype(bf16)
        u_ref[...] = u.astype(bf16)
        a_ref[...] = a
        acc[...] += _nn(a, wd_ref[...])

        @pl.when(j == nj - 1)
        def _():
            fo_ref[...] = acc[...].astype(bf16)
            ho_ref[...] = h_ref[...] + (0.5 * par_ref[2:3, :]) * acc[...]

    row = pl.BlockSpec((tm, D), lambda i, j: (i, 0))
    hid = pl.BlockSpec((tm, tf), lambda i, j: (i, j))
    wspec = [pl.BlockSpec((None, tf, D), lambda i, j, k=k: (k, j, 0)) for k in range(3)]
    return pl.pallas_call(
        body, name="ffn_fwd", grid=(T // tm, nj),
        in_specs=[row, pl.BlockSpec((8, D), lambda i, j: (0, 0))] + wspec,
        out_specs=[row, row, hid, hid, hid],
        out_shape=[SDS((T, D), f32), SDS((T, D), bf16), SDS((T, F), bf16), SDS((T, F), bf16), SDS((T, F), bf16)],
        scratch_shapes=[pltpu.VMEM((tm, D), bf16), pltpu.VMEM((tm, D), f32)],
        compiler_params=_cp(("arbitrary", "arbitrary"), 52),
    )(h, par, w3, w3, w3)


def _ffn_bwd_tok(dh, h, par, fo, gs, us, w3):
    tm, tf = 512, 256
    nj = F // tf

    def body(dh_ref, h_ref, par_ref, fo_ref, g_ref, u_ref, wg_ref, wu_ref, wd_ref,
             dhin_ref, dg_ref, du_ref, y_ref, dfb_ref, dpar_ref, df_scr, dyacc):
        i, j = pl.program_id(0), pl.program_id(1)

        @pl.when(jnp.logical_and(i == 0, j == 0))
        def _():
            dpar_ref[...] = jnp.zeros_like(dpar_ref)

        @pl.when(j == 0)
        def _():
            dh_v, par_v = dh_ref[...], par_ref[...]
            dfb = ((0.5 * par_v[2:3, :]) * dh_v).astype(bf16)
            df_scr[...] = dfb
            dfb_ref[...] = dfb
            dpar_ref[2:3, :] += 0.5 * _colsum(dh_v * fo_ref[...].astype(f32))
            y_ref[...] = _norm_mod(h_ref[...], par_v)[0].astype(bf16)
            dyacc[...] = jnp.zeros_like(dyacc)

        da = _nt(df_scr[...], wd_ref[...])
        g = g_ref[...].astype(f32)
        u = u_ref[...].astype(f32)
        sig = jax.nn.sigmoid(g)
        du = (da * (g * sig)).astype(bf16)
        dg = (da * u * (sig * (1.0 + g * (1.0 - sig)))).astype(bf16)
        dg_ref[...] = dg
        du_ref[...] = du
        dyacc[...] += _nn(dg, wg_ref[...]) + _nn(du, wu_ref[...])

        @pl.when(j == nj - 1)
        def _():
            dx, dsh, dsc, dng = _norm_mod_bwd(dyacc[...], h_ref[...], par_ref[...])
            dhin_ref[...] = dh_ref[...] + dx
            dpar_ref[0:1, :] += dsh
            dpar_ref[1:2, :] += dsc
            dpar_ref[3:4, :] += dng

    row = pl.BlockSpec((tm, D), lambda i, j: (i, 0))
    hid = pl.BlockSpec((tm, tf), lambda i, j: (i, j))
    one = pl.BlockSpec((8, D), lambda i, j: (0, 0))
    wspec = [pl.BlockSpec((None, tf, D), lambda i, j, k=k: (k, j, 0)) for k in range(3)]
    return pl.pallas_call(
        body, name="ffn_bwd_tok", grid=(T // tm, nj),
        in_specs=[row, row, one, row, hid, hid] + wspec,
        out_specs=[row, hid, hid, row, row, one],
        out_shape=[SDS((T, D), f32), SDS((T, F), bf16), SDS((T, F), bf16), SDS((T, D), bf16),
                   SDS((T, D), bf16), SDS((8, D), f32)],
        scratch_shapes=[pltpu.VMEM((tm, D), bf16), pltpu.VMEM((tm, D), f32)],
        compiler_params=_cp(("arbitrary", "arbitrary"), 48),
    )(dh, h, par, fo, gs, us, w3, w3, w3)


def _ffn_bwd_w(a_s, dg_s, du_s, y, dfb, anchor=None):
    tf, tk = F // 2, 256
    nk = T // tk

    def body(a_ref, dg_ref, du_ref, y_ref, df_ref, *rest):
        o_ref, accg, accu, accd = rest[-4:]
        kk = pl.program_id(1)

        @pl.when(kk == 0)
        def _():
            accg[...] = jnp.zeros_like(accg)
            accu[...] = jnp.zeros_like(accu)
            accd[...] = jnp.zeros_like(accd)

        yv = y_ref[...]
        accg[...] += _tn(dg_ref[...], yv)
        accu[...] += _tn(du_ref[...], yv)
        accd[...] += _tn(a_ref[...], df_ref[...])

        @pl.when(kk == nk - 1)
        def _():
            o_ref[0] = accg[...].astype(bf16)
            o_ref[1] = accu[...].astype(bf16)
            o_ref[2] = accd[...].astype(bf16)

    hid = pl.BlockSpec((tk, tf), lambda j, kk: (kk, j))
    row = pl.BlockSpec((tk, D), lambda j, kk: (kk, 0))
    extra = [] if anchor is None else [anchor]
    return pl.pallas_call(
        body, name="ffn_bwd_w", grid=(F // tf, nk),
        in_specs=[hid, hid, hid, row, row] + [pl.BlockSpec((8, CH), lambda j, kk: (0, 0))] * len(extra),
        out_specs=pl.BlockSpec((3, tf, D), lambda j, kk: (0, j, 0)), out_shape=SDS((3, F, D), bf16),
        scratch_shapes=[pltpu.VMEM((tf, D), f32)] * 3,
        compiler_params=_cp(("arbitrary", "arbitrary"), 48),
    )(a_s, dg_s, du_s, y, dfb, *extra)


def _tn_matmul(a, b, bm, name):
    M, N = a.shape[1], b.shape[1]
    tk = 512
    nk = T // tk

    def body(a_ref, b_ref, o_ref, acc):
        kk = pl.program_id(1)

        @pl.when(kk == 0)
        def _():
            acc[...] = jnp.zeros_like(acc)

        acc[...] += _tn(a_ref[...], b_ref[...])

        @pl.when(kk == nk - 1)
        def _():
            o_ref[...] = acc[...].astype(bf16)

    return pl.pallas_call(
        body, name=name, grid=(M // bm, nk),
        in_specs=[pl.BlockSpec((tk, bm), lambda i, kk: (kk, i)), pl.BlockSpec((tk, N), lambda i, kk: (kk, 0))],
        out_specs=pl.BlockSpec((bm, N), lambda i, kk: (i, 0)), out_shape=SDS((M, N), bf16),
        scratch_shapes=[pltpu.VMEM((bm, N), f32)],
        compiler_params=_cp(("arbitrary", "arbitrary"), 40),
    )(a, b)


def _rope_tables():
    inv = ROPE_THETA ** (-jnp.arange(0, 64, 2, dtype=f32) / 64)
    ang = jnp.arange(T, dtype=f32)[:, None] * inv[None, :]
    ang = jnp.concatenate([ang, ang], axis=-1)
    cos, sin = jnp.tile(jnp.cos(ang), (1, 8)), jnp.tile(jnp.sin(ang), (1, 8))
    low = (jnp.arange(HW) % 64 < 32)[None, :]
    return cos, jnp.where(low, -sin, 0.0), jnp.where(low, 0.0, sin)


def _rope(t, cos, sin_lo, sin_hi):
    return t * cos + pltpu.roll(t, HW - 32, 1) * sin_lo + pltpu.roll(t, 32, 1) * sin_hi


def _rope_t(g, cos, sin_lo, sin_hi):
    return g * cos + pltpu.roll(g * sin_lo, 32, 1) + pltpu.roll(g * sin_hi, HW - 32, 1)


def _inproj_fwd(h, par, w_in_t, tabs):
    tm = 512

    def body(h_ref, par_ref, w_ref, cos_ref, slo_ref, shi_ref, ua_ref, va_ref, q_ref, k_ref, v_ref):
        y = _norm_mod(h_ref[...], par_ref[...])[0].astype(bf16)
        proj = lambda c: _nt(y, w_ref[c * HW:(c + 1) * HW, :])
        ua_ref[...] = proj(0)
        va_ref[...] = proj(1)
        cos, slo, shi = cos_ref[...], slo_ref[...], shi_ref[...]
        q_ref[...] = (_rope(proj(2), cos, slo, shi) * 0.125).astype(bf16)
        k_ref[...] = _rope(proj(3), cos, slo, shi).astype(bf16)
        v_ref[...] = proj(4).astype(bf16)

    row = pl.BlockSpec((tm, D), lambda i: (i, 0))
    half = pl.BlockSpec((tm, HW), lambda i: (i, 0))
    return pl.pallas_call(
        body, name="inproj_fwd", grid=(T // tm,),
        in_specs=[row, pl.BlockSpec((8, D), lambda i: (0, 0)), pl.BlockSpec((INW, D), lambda i: (0, 0)), half, half, half],
        out_specs=[half] * 5,
        out_shape=[SDS((T, HW), f32), SDS((T, HW), f32), SDS((T, HW), bf16), SDS((T, HW), bf16), SDS((T, HW), bf16)],
        compiler_params=_cp(("arbitrary",), 48),
    )(h, par, w_in_t, *tabs)


def _head_masks():
    lane = lax.broadcasted_iota(jnp.int32, (1, CH), 1)
    return lane < 64, lane >= 64


def _attn_fwd(q, k, v, d, anchor=None):
    L = T // d
    nb = L // CH

    def body(q_ref, kc_ref, kp_ref, vc_ref, vp_ref, *rest):
        o_ref, lse_ref = rest[-2:]
        n = pl.program_id(1)
        row = lax.broadcasted_iota(jnp.int32, (CH, CH), 0)
        col = lax.broadcasted_iota(jnp.int32, (CH, CH), 1)
        mask_c = col <= row
        mask_p = jnp.logical_and(col >= row, n > 0)
        lse_tile = jnp.zeros((CH, CH), f32)
        for hp in range(4):
            sl = slice(hp * CH, (hp + 1) * CH)
            qv, kc, kp, vc, vp = q_ref[:, sl], kc_ref[:, sl], kp_ref[:, sl], vc_ref[:, sl], vp_ref[:, sl]
            o_pair = jnp.zeros((CH, CH), f32)
            for hh, hm in enumerate(_head_masks()):
                qm = jnp.where(hm, qv, jnp.zeros_like(qv))
                sc = jnp.where(mask_c, _nt(qm, kc), -jnp.inf)
                sp = jnp.where(mask_p, _nt(qm, kp), -jnp.inf)
                m = jnp.maximum(jnp.max(sc, axis=-1, keepdims=True), jnp.max(sp, axis=-1, keepdims=True))
                pc, pp = jnp.exp(sc - m), jnp.exp(sp - m)
                den = jnp.sum(pc, axis=-1, keepdims=True) + jnp.sum(pp, axis=-1, keepdims=True)
                acc = _nn(pc.astype(bf16), vc) + _nn(pp.astype(bf16), vp)
                o_pair = jnp.where(hm, acc / den, o_pair)
                lse_tile = jnp.where(col == 2 * hp + hh, m + jnp.log(den), lse_tile)
            o_ref[:, sl] = o_pair
        lse_ref[...] = lse_tile

    cur = pl.BlockSpec((CH, HW), lambda r, n: (n, r))
    prev = pl.BlockSpec((CH, HW), lambda r, n: (jnp.maximum(n - 1, 0), r))
    q2, k2, v2 = (t.reshape(L, d * HW) for t in (q, k, v))
    extra = [] if anchor is None else [anchor]
    o, lse = pl.pallas_call(
        body, name="attn_fwd_d%d" % d, grid=(d, nb),
        in_specs=[cur, cur, prev, cur, prev] + [pl.BlockSpec((8, CH), lambda r, n: (0, 0))] * len(extra),
        out_specs=[cur, pl.BlockSpec((CH, CH), lambda r, n: (n, r))],
        out_shape=[SDS((L, d * HW), f32), SDS((L, d * CH), f32)],
        compiler_params=_cp(("arbitrary", "arbitrary"), 32),
    )(q2, k2, k2, v2, v2, *extra)
    return o.reshape(T, HW), lse.reshape(T, CH)


def _attn_combine(os_, lses):
    tm = 512

    def body(o0_ref, o1_ref, o2_ref, l0_ref, l1_ref, l2_ref, o_ref, lse_ref):
        l0, l1, l2 = l0_ref[...], l1_ref[...], l2_ref[...]
        m = jnp.maximum(jnp.maximum(l0, l1), l2)
        e = [jnp.exp(l0 - m), jnp.exp(l1 - m), jnp.exp(l2 - m)]
        s = e[0] + e[1] + e[2]
        w = [ei / s for ei in e]
        lse_ref[...] = m + jnp.log(s)
        lo, _ = _head_masks()
        for hp in range(4):
            sl = slice(hp * CH, (hp + 1) * CH)
            acc = jnp.zeros((tm, CH), f32)
            for wp, op_ref in zip(w, (o0_ref, o1_ref, o2_ref)):
                wexp = jnp.where(lo, wp[:, 2 * hp:2 * hp + 1], wp[:, 2 * hp + 1:2 * hp + 2])
                acc = acc + wexp * op_ref[:, sl]
            o_ref[:, sl] = acc.astype(bf16)

    half = pl.BlockSpec((tm, HW), lambda i: (i, 0))
    stat = pl.BlockSpec((tm, CH), lambda i: (i, 0))
    return pl.pallas_call(
        body, name="attn_combine", grid=(T // tm,), in_specs=[half] * 3 + [stat] * 3, out_specs=[half, stat],
        out_shape=[SDS((T, HW), bf16), SDS((T, CH), f32)],
        compiler_params=_cp(("arbitrary",), 32),
    )(*os_, *lses)


def _attn_bwd(q, k, v, mixed, do, lse, d):
    L = T // d
    nb = L // CH

    def body(qc_ref, qn_ref, kc_ref, kp_ref, vc_ref, vp_ref, oc_ref, on_ref, dc_ref, dn_ref, lc_ref, ln_ref,
             dq_ref, dk_ref, dv_ref):
        n = pl.program_id(1)
        row = lax.broadcasted_iota(jnp.int32, (CH, CH), 0)
        col = lax.broadcasted_iota(jnp.int32, (CH, CH), 1)
        mask_c = col <= row
        mask_p = jnp.logical_and(col >= row, n > 0)
        mask_n = jnp.logical_and(col >= row, n < nb - 1)
        lse_c, lse_n = lc_ref[...], ln_ref[...]
        for hp in range(4):
            sl = slice(hp * CH, (hp + 1) * CH)
            qc, qn, kc, kp, vc, vp = qc_ref[:, sl], qn_ref[:, sl], kc_ref[:, sl], kp_ref[:, sl], vc_ref[:, sl], vp_ref[:, sl]
            doc, don = dc_ref[:, sl], dn_ref[:, sl]
            prod_c = doc.astype(f32) * oc_ref[:, sl].astype(f32)
            prod_n = don.astype(f32) * on_ref[:, sl].astype(f32)
            dq_pair = jnp.zeros((CH, CH), f32)
            dk_pair = jnp.zeros((CH, CH), f32)
            dv_pair = jnp.zeros((CH, CH), f32)
            for hh, hm in enumerate(_head_masks()):
                h_idx = 2 * hp + hh
                zero = jnp.zeros_like(qc)
                qcm, qnm = jnp.where(hm, qc, zero), jnp.where(hm, qn, zero)
                dcm, dnm = jnp.where(hm, doc, zero), jnp.where(hm, don, zero)
                delta_c = jnp.sum(jnp.where(hm, prod_c, 0.0), axis=-1, keepdims=True)
                delta_n = jnp.sum(jnp.where(hm, prod_n, 0.0), axis=-1, keepdims=True)
                lc, ln = lse_c[:, h_idx:h_idx + 1], lse_n[:, h_idx:h_idx + 1]
                p = jnp.where(mask_c, jnp.exp(_nt(qcm, kc) - lc), 0.0)
                ds = (p * (_nt(dcm, vc) - delta_c)).astype(bf16)
                dq_h = _nn(ds, kc)
                dk_h = _tn(ds, qcm)
                dv_h = _tn(p.astype(bf16), dcm)
                p = jnp.where(mask_p, jnp.exp(_nt(qcm, kp) - lc), 0.0)
                ds = (p * (_nt(dcm, vp) - delta_c)).astype(bf16)
                dq_h = dq_h + _nn(ds, kp)
                p = jnp.where(mask_n, jnp.exp(_nt(qnm, kc) - ln), 0.0)
                ds = (p * (_nt(dnm, vc) - delta_n)).astype(bf16)
                dk_h = dk_h + _tn(ds, qnm)
                dv_h = dv_h + _tn(p.astype(bf16), dnm)
                dq_pair = jnp.where(hm, dq_h, dq_pair)
                dk_pair = dk_pair + dk_h
                dv_pair = dv_pair + dv_h
            dq_ref[:, sl] = dq_pair
            dk_ref[:, sl] = dk_pair
            dv_ref[:, sl] = dv_pair

    cur = pl.BlockSpec((CH, HW), lambda r, n: (n, r))
    prev = pl.BlockSpec((CH, HW), lambda r, n: (jnp.maximum(n - 1, 0), r))
    nxt = pl.BlockSpec((CH, HW), lambda r, n: (jnp.minimum(n + 1, nb - 1), r))
    ocur = pl.BlockSpec((CH, HW), lambda r, n: (n, 2 * r + 1))
    onxt = pl.BlockSpec((CH, HW), lambda r, n: (jnp.minimum(n + 1, nb - 1), 2 * r + 1))
    scur = pl.BlockSpec((CH, CH), lambda r, n: (n, r))
    snxt = pl.BlockSpec((CH, CH), lambda r, n: (jnp.minimum(n + 1, nb - 1), r))
    q2, k2, v2, do2 = (t.reshape(L, d * HW) for t in (q, k, v, do))
    m2, l2 = mixed.reshape(L, d * 2 * HW), lse.reshape(L, d * CH)
    outs = pl.pallas_call(
        body, name="attn_bwd_d%d" % d, grid=(d, nb),
        in_specs=[cur, nxt, cur, prev, cur, prev, ocur, onxt, cur, nxt, scur, snxt],
        out_specs=[cur] * 3, out_shape=[SDS((L, d * HW), f32)] * 3,
        compiler_params=_cp(("arbitrary", "arbitrary"), 32),
    )(q2, q2, k2, k2, v2, v2, m2, m2, do2, do2, l2, l2)
    return [t.reshape(T, HW) for t in outs]


def _sgu_fwd(ua, va, o, lng, lnb, ws, bs_t):
    tm = 512

    def body(ua_ref, va_ref, o_ref, lg_ref, lb_ref, ws_ref, bs_ref, mix_ref):
        for hd in range(4):
            sl = slice(hd * CH, (hd + 1) * CH)
            w = ws_ref[hd]
            for cc in range(tm // CH):
                rs = slice(cc * CH, (cc + 1) * CH)
                u = _gelu(ua_ref[rs, sl])
                v = _gelu(va_ref[rs, sl])
                vc = v - _rowmean(v)
                vn = vc * lax.rsqrt(_rowmean(vc * vc) + EPS) * lg_ref[:, sl] + lb_ref[:, sl]
                z = _nn(w, vn.astype(bf16)) + bs_ref[:, hd:hd + 1]
                mix_ref[rs, sl] = (u * z).astype(bf16)
        mix_ref[:, HW:] = o_ref[...]

    half = pl.BlockSpec((tm, HW), lambda i: (i, 0))
    vec = pl.BlockSpec((1, HW), lambda i: (0, 0))
    return pl.pallas_call(
        body, name="sgu_fwd", grid=(T // tm,),
        in_specs=[half, half, half, vec, vec, pl.BlockSpec((4, CH, CH), lambda i: (0, 0, 0)),
                  pl.BlockSpec((CH, 4), lambda i: (0, 0))],
        out_specs=pl.BlockSpec((tm, 2 * HW), lambda i: (i, 0)), out_shape=SDS((T, 2 * HW), bf16),
        compiler_params=_cp(("arbitrary",), 32),
    )(ua, va, o, lng, lnb, ws, bs_t)


def _sgu_bwd(ua, va, d_a, lng, lnb, ws, ws_t, bs_t):
    tm = 512

    def body(ua_ref, va_ref, d_ref, lg_ref, lb_ref, ws_ref, wst_ref, bs_ref, dsg_ref, dln_ref, dws_ref, db_ref):
        @pl.when(pl.program_id(0) == 0)
        def _():
            dln_ref[...] = jnp.zeros_like(dln_ref)
            dws_ref[...] = jnp.zeros_like(dws_ref)
            db_ref[...] = jnp.zeros_like(db_ref)

        for hd in range(4):
            sl = slice(hd * CH, (hd + 1) * CH)
            w, wt = ws_ref[hd], wst_ref[hd]
            lg = lg_ref[:, sl]
            for cc in range(tm // CH):
                rs = slice(cc * CH, (cc + 1) * CH)
                xa, xv, dd = ua_ref[rs, sl], va_ref[rs, sl], d_ref[rs, sl]
                u, v = _gelu(xa), _gelu(xv)
                vc = v - _rowmean(v)
                rstd = lax.rsqrt(_rowmean(vc * vc) + EPS)
                xh = vc * rstd
                vnb = (xh * lg + lb_ref[:, sl]).astype(bf16)
                z = _nn(w, vnb) + bs_ref[:, hd:hd + 1]
                dz = dd * u
                dzb = dz.astype(bf16)
                dsg_ref[rs, sl] = (dd * z * _gelu_grad(xa)).astype(bf16)
                dws_ref[hd] += _nt(dzb, vnb)
                db_ref[hd] += dz
                dvn = _nn(wt, dzb)
                dln_ref[0:1, sl] += _colsum(dvn * xh)
                dln_ref[1:2, sl] += _colsum(dvn)
                dxh = dvn * lg
                dv = rstd * (dxh - _rowmean(dxh) - xh * _rowmean(dxh * xh))
                dsg_ref[rs, HW + hd * CH:HW + (hd + 1) * CH] = (dv * _gelu_grad(xv)).astype(bf16)

    half = pl.BlockSpec((tm, HW), lambda i: (i, 0))
    vec = pl.BlockSpec((1, HW), lambda i: (0, 0))
    mat = pl.BlockSpec((4, CH, CH), lambda i: (0, 0, 0))
    return pl.pallas_call(
        body, name="sgu_bwd", grid=(T // tm,),
        in_specs=[half, half, half, vec, vec, mat, mat, pl.BlockSpec((CH, 4), lambda i: (0, 0))],
        out_specs=[pl.BlockSpec((tm, 2 * HW), lambda i: (i, 0)), pl.BlockSpec((8, HW), lambda i: (0, 0)), mat, mat],
        out_shape=[SDS((T, 2 * HW), bf16), SDS((8, HW), f32), SDS((4, CH, CH), f32), SDS((4, CH, CH), f32)],
        compiler_params=_cp(("arbitrary",), 32),
    )(ua, va, d_a, lng, lnb, ws, ws_t, bs_t)


def _outproj_fwd(mixed, w_out, h, par):
    tm = 512

    def body(mix_ref, w_ref, h_ref, par_ref, ho_ref, po_ref):
        p = _nn(mix_ref[...], w_ref[...])
        po_ref[...] = p.astype(bf16)
        ho_ref[...] = h_ref[...] + par_ref[2:3, :] * p

    row = pl.BlockSpec((tm, D), lambda i: (i, 0))
    return pl.pallas_call(
        body, name="outproj_fwd", grid=(T // tm,),
        in_specs=[row, pl.BlockSpec((D, D), lambda i: (0, 0)), row, pl.BlockSpec((8, D), lambda i: (0, 0))],
        out_specs=[row, row], out_shape=[SDS((T, D), f32), SDS((T, D), bf16)],
        compiler_params=_cp(("arbitrary",), 32),
    )(mixed, w_out, h, par)


def _outproj_bwd(dh, po, w_out, par):
    tm = 512

    def body(dh_ref, po_ref, w_ref, par_ref, do_ref, da_ref, db_ref, dpar_ref):
        @pl.when(pl.program_id(0) == 0)
        def _():
            dpar_ref[...] = jnp.zeros_like(dpar_ref)

        dh_v = dh_ref[...]
        dob = (par_ref[2:3, :] * dh_v).astype(bf16)
        do_ref[...] = dob
        dpar_ref[2:3, :] += _colsum(dh_v * po_ref[...].astype(f32))
        dm = _nt(dob, w_ref[...])
        da_ref[...] = dm[:, :HW]
        db_ref[...] = dm[:, HW:].astype(bf16)

    row = pl.BlockSpec((tm, D), lambda i: (i, 0))
    half = pl.BlockSpec((tm, HW), lambda i: (i, 0))
    one = pl.BlockSpec((8, D), lambda i: (0, 0))
    return pl.pallas_call(
        body, name="outproj_bwd", grid=(T // tm,),
        in_specs=[row, row, pl.BlockSpec((D, D), lambda i: (0, 0)), one],
        out_specs=[row, half, half, one],
        out_shape=[SDS((T, D), bf16), SDS((T, HW), f32), SDS((T, HW), bf16), SDS((8, D), f32)],
        compiler_params=_cp(("arbitrary",), 32),
    )(dh, po, w_out, par)


def _inproj_bwd_tok(dh, h, par, w_in_t, dsg, dqs, dks, dvs, tabs):
    tm = 256

    def body(dh_ref, h_ref, par_ref, w_ref, dsg_ref, dq0, dq1, dq2, dk0, dk1, dk2, dv0, dv1, dv2,
             cos_ref, slo_ref, shi_ref, dhin_ref, dp_ref, y_ref, dpar_ref):
        @pl.when(pl.program_id(0) == 0)
        def _():
            dpar_ref[...] = jnp.zeros_like(dpar_ref)

        cos, slo, shi = cos_ref[...], slo_ref[...], shi_ref[...]
        dq = (dq0[...] + dq1[...] + dq2[...]) * 0.125
        dk = dk0[...] + dk1[...] + dk2[...]
        dp_ref[:, :2 * HW] = dsg_ref[...]
        dp_ref[:, 2 * HW:3 * HW] = _rope_t(dq, cos, slo, shi).astype(bf16)
        dp_ref[:, 3 * HW:4 * HW] = _rope_t(dk, cos, slo, shi).astype(bf16)
        dp_ref[:, 4 * HW:] = (dv0[...] + dv1[...] + dv2[...]).astype(bf16)
        dy = _nn(dp_ref[...], w_ref[...])
        par_v, h_v = par_ref[...], h_ref[...]
        y_ref[...] = _norm_mod(h_v, par_v)[0].astype(bf16)
        dx, dsh, dsc, dng = _norm_mod_bwd(dy, h_v, par_v)
        dhin_ref[...] = dh_ref[...] + dx
        dpar_ref[0:1, :] += dsh
        dpar_ref[1:2, :] += dsc
        dpar_ref[3:4, :] += dng

    row = pl.BlockSpec((tm, D), lambda i: (i, 0))
    half = pl.BlockSpec((tm, HW), lambda i: (i, 0))
    one = pl.BlockSpec((8, D), lambda i: (0, 0))
    return pl.pallas_call(
        body, name="inproj_bwd_tok", grid=(T // tm,),
        in_specs=[row, row, one, pl.BlockSpec((INW, D), lambda i: (0, 0)), row] + [half] * 12,
        out_specs=[row, pl.BlockSpec((tm, INW), lambda i: (i, 0)), row, one],
        out_shape=[SDS((T, D), f32), SDS((T, INW), bf16), SDS((T, D), bf16), SDS((8, D), f32)],
        compiler_params=_cp(("arbitrary",), 48),
    )(dh, h, par, w_in_t, dsg, *dqs, *dks, *dvs, *tabs)


def _loss_head(h, par, target):
    tm = 512

    def body(h_ref, par_ref, t_ref, dh_ref, acc_ref):
        @pl.when(pl.program_id(0) == 0)
        def _():
            acc_ref[...] = jnp.zeros_like(acc_ref)

        x, g = h_ref[...], par_ref[3:4, :]
        rstd = lax.rsqrt(_rowmean(x * x) + EPS)
        xhat = x * rstd
        err = xhat * g - t_ref[...]
        acc_ref[0:1, :] += _colsum(err * err)
        dy = err * (1.0 / D)
        acc_ref[3:4, :] += _colsum(dy * xhat)
        dxhat = dy * g
        dh_ref[...] = rstd * (dxhat - xhat * _rowmean(dxhat * xhat))

    row = pl.BlockSpec((tm, D), lambda i: (i, 0))
    one = pl.BlockSpec((8, D), lambda i: (0, 0))
    return pl.pallas_call(
        body, name="loss_head", grid=(T // tm,), in_specs=[row, one, row], out_specs=[row, one],
        out_shape=[SDS((T, D), f32), SDS((8, D), f32)], compiler_params=_cp(("arbitrary",), 32),
    )(h, par, target)


ELEMENTWISE_VMEM_BUDGET = 20 << 20


def _block_rows(rows, bytes_per_row):
    cap = ELEMENTWISE_VMEM_BUDGET // bytes_per_row
    if rows <= cap:
        return rows
    return next(b for b in range(cap - cap % 16, 0, -16) if rows % b == 0)


def _sum_slots(land, name):
    _, R, C = land.shape
    br = _block_rows(R, 2 * NDEV * C * land.dtype.itemsize + 2 * C * 4)

    def body(l_ref, o_ref):
        acc = l_ref[0].astype(f32)
        for s in range(1, NDEV):
            acc = acc + l_ref[s].astype(f32)
        o_ref[...] = acc

    return pl.pallas_call(
        body, name=name, grid=(R // br,), in_specs=[pl.BlockSpec((NDEV, br, C), lambda i: (0, i, 0))],
        out_specs=pl.BlockSpec((br, C), lambda i: (i, 0)), out_shape=SDS((R, C), f32),
        compiler_params=_cp(("arbitrary",), 32),
    )(land)


def _sum_partials(land, own, me, name):
    r = land.shape[-2]

    def body(me_ref, l_ref, own_ref, o_ref):
        mine = own_ref[...].astype(f32)
        acc = jnp.where(me_ref[0] == 0, mine, l_ref[0].astype(f32))
        for s in range(1, NDEV):
            acc = acc + jnp.where(me_ref[0] == s, mine, l_ref[s].astype(f32))
        o_ref[...] = acc

    if own.ndim == 3:
        kk = own.shape[0]
        specs = dict(grid=(kk,),
                     in_specs=[pl.BlockSpec((NDEV, None, r, D), lambda i, me_ref: (0, i, 0, 0)),
                               pl.BlockSpec((None, r, D), lambda i, me_ref: (i, me_ref[0], 0))],
                     out_specs=pl.BlockSpec((None, r, D), lambda i, me_ref: (i, 0, 0)))
        out_shape = SDS((kk, r, D), f32)
    else:
        specs = dict(grid=(1,),
                     in_specs=[pl.BlockSpec((NDEV, r, D), lambda i, me_ref: (0, 0, 0)),
                               pl.BlockSpec((r, D), lambda i, me_ref: (me_ref[0], 0))],
                     out_specs=pl.BlockSpec((r, D), lambda i, me_ref: (0, 0)))
        out_shape = SDS((r, D), f32)
    return pl.pallas_call(
        body, name=name, out_shape=out_shape,
        grid_spec=pltpu.PrefetchScalarGridSpec(num_scalar_prefetch=1, **specs),
        compiler_params=_cp(("arbitrary",), 32),
    )(me.reshape(1), land, own)


def _adamw(w, g, m, v, name):
    R, C = w.shape
    br = _block_rows(R, 2 * 7 * C * 4)

    def body(w_ref, g_ref, m_ref, v_ref, d_ref, mo_ref, vo_ref):
        gv = g_ref[...]
        m2 = B1 * m_ref[...] + (1.0 - B1) * gv
        v2 = B2 * v_ref[...] + (1.0 - B2) * (gv * gv)
        mo_ref[...] = m2
        vo_ref[...] = v2
        m_hat = m2 / (1.0 - B1 ** STEP)
        v_hat = v2 / (1.0 - B2 ** STEP)
        d_ref[...] = -LR * (m_hat / (jnp.sqrt(v_hat) + AEPS) + WD * w_ref[...])

    blk = pl.BlockSpec((br, C), lambda i: (i, 0))
    return pl.pallas_call(
        body, name=name, grid=(R // br,), in_specs=[blk] * 4, out_specs=[blk] * 3,
        out_shape=[SDS((R, C), f32)] * 3, compiler_params=_cp(("arbitrary",), 32),
    )(w, g, m, v)


def _adamw_nd(w, g, m, v, name):
    shp = w.shape
    r2 = (-1, shp[-1]) if w.ndim > 1 else (8, shp[0] // 8)
    outs = _adamw(w.reshape(r2), g.reshape(r2), m.reshape(r2), v.reshape(r2), name)
    return [o.reshape(shp) for o in outs]


def _par_rows(mod_l, s, gain):
    rows = jnp.pad(mod_l.reshape(9, D)[3 * s:3 * s + 3], ((0, 5), (0, 0)))
    return rows + jnp.pad(gain[None, :], ((3, 4), (0, 0)))


def _pad_rows(a):
    a = a.reshape(-1, CH)
    return jnp.pad(a, ((0, (-a.shape[0]) % 8), (0, 0)))


def _prepare(c, ada_w, ada_b, norm_g):
    me = 4 * lax.axis_index("x") + 2 * lax.axis_index("y") + lax.axis_index("c")

    pay = jnp.pad(c, ((0, 7), (0, 0)))
    pay = jnp.concatenate([pay, jnp.pad(norm_g.reshape(6, OUTS), ((0, 2), (0, D - OUTS)))], axis=0)
    got = _all_gather_small(pay, "gather_c")
    c_all = got[:, 0, :]
    gains = got[:, 8:14, :OUTS].transpose(1, 0, 2).reshape(DEPTH, 3, D)

    ada_b_loc = lax.dynamic_slice(ada_b, (0, me * ADAS), (DEPTH, ADAS)).reshape(DEPTH, 1, ADAS)
    mod_cols = _mod_fwd(c_all, ada_w, ada_b_loc)
    got = _all_gather_small(mod_cols.reshape(DEPTH * NDEV, ADAS), "gather_mod").reshape(NDEV, DEPTH, NDEV, ADAS)
    mod = lax.dynamic_index_in_dim(got, me, axis=2, keepdims=False).transpose(1, 0, 2).reshape(DEPTH, 9 * D)
    pars = [[_par_rows(mod[l], s, gains[l, s]) for s in range(3)] for l in range(DEPTH)]
    return me, c_all, pars


def _fwd_bwd(x2, target, pars, get_w, put_g, small_ready, sgu_ln_g, sgu_ln_b, sgu_w, sgu_b, final_g):
    tabs = _rope_tables()
    tril = jnp.tril(jnp.ones((CH, CH), dtype=bool))
    ws_m = jnp.where(tril[None, None], sgu_w, 0.0).astype(bf16)
    ws_mt = jnp.swapaxes(ws_m, -1, -2)
    behind = lambda rows, token: rows if token is None else rows + token[0, 0]

    h = x2
    saved = []
    for l in range(DEPTH):
        lng, lnb = sgu_ln_g[l].reshape(1, HW), sgu_ln_b[l].reshape(1, HW)
        bs_t = sgu_b[l].T
        h0 = h
        w, token = get_w(4 * l, h0)
        h1, fo1, g1, u1, a1 = _ffn_fwd(h0, behind(pars[l][0], token), w)
        w, token = get_w(4 * l + 2, h1)
        ua, va, q, k, v = _inproj_fwd(h1, behind(pars[l][1], token), w, tabs)
        branches, token = [], None
        for d in PATTERN_DILATIONS:
            branches.append(_attn_fwd(q, k, v, d, token))
            if len(branches) == 2 and l + 1 < DEPTH:
                token = get_w(4 * l + 1, branches[-1][1])[1]
        o, lse = _attn_combine([b[0] for b in branches], [b[1] for b in branches])
        mixed = _sgu_fwd(ua, va, o, lng, lnb, ws_m[l], bs_t)
        w, token = get_w(4 * l + 3, mixed)
        h2, po = _outproj_fwd(mixed, w, h1, behind(pars[l][1], token))
        w, token = get_w(4 * l + 1, h2)
        h3, fo2, g2, u2, a2 = _ffn_fwd(h2, behind(pars[l][2], token), w)
        saved.append((h0, h1, h2, fo1, g1, u1, a1, ua, va, q, k, v, mixed, lse, po, fo2, g2, u2, a2))
        h = h3

    par_f = jnp.pad(final_g[None, :], ((3, 4), (0, 0)))
    dh, head = _loss_head(h, par_f, target)

    dmods, dgains, dsgu = [None] * DEPTH, [None] * DEPTH, [None] * DEPTH
    token = None
    for l in reversed(range(DEPTH)):
        w_f1, w_f2, w_i, w_o = (get_w(4 * l + j, None)[0] for j in (0, 1, 2, 3))
        h0, h1, h2, fo1, g1, u1, a1, ua, va, q, k, v, mixed, lse, po, fo2, g2, u2, a2 = saved[l]
        lng, lnb = sgu_ln_g[l].reshape(1, HW), sgu_ln_b[l].reshape(1, HW)
        bs_t = sgu_b[l].T

        dh, dg_s, du_s, y, dfb, dpar3 = _ffn_bwd_tok(dh, h2, behind(pars[l][2], token), fo2, g2, u2, w_f2)
        token = put_g(4 * l + 1, _ffn_bwd_w(a2, dg_s, du_s, y, dfb))

        dob, d_a, d_b, dpar2g = _outproj_bwd(dh, po, w_o, behind(pars[l][1], token))
        token = put_g(4 * l + 3, _tn_matmul(mixed, dob, 512, "w_out_grad"))
        parts = [_attn_bwd(q, k, v, mixed, d_b, lse, d) for d in PATTERN_DILATIONS]
        dsg, dln, dws, dbl = _sgu_bwd(ua, va, d_a, lng, lnb, ws_m[l], ws_mt[l], bs_t)
        dh, dp, y2, dpar2 = _inproj_bwd_tok(dh, h1, behind(pars[l][1], token), w_i, dsg, [p[0] for p in parts],
                                            [p[1] for p in parts], [p[2] for p in parts], tabs)
        token = put_g(4 * l + 2, _tn_matmul(dp, y2, 640, "w_in_grad"))

        dh, dg_s, du_s, y, dfb, dpar1 = _ffn_bwd_tok(dh, h0, behind(pars[l][0], token), fo1, g1, u1, w_f1)
        dmods[l] = jnp.concatenate([dpar1[0:3], dpar2[0:2], dpar2g[2:3], dpar3[0:3]], axis=0).reshape(9 * D)
        dgains[l] = jnp.stack([dpar1[3], dpar2[3], dpar3[3]])
        dsgu[l] = (dln[0], dln[1], jnp.where(tril[None], dws, 0.0), jnp.sum(dbl, axis=-1))
        token = small_ready(head, dmods, dgains, dsgu) if l == 0 else None
        token = put_g(4 * l, _ffn_bwd_w(a1, dg_s, du_s, y, dfb, token))
    return dh, token


def kernel(x, c, ada_w, ada_b, norm_g, ffn1_wg, ffn1_wu, ffn1_wd, ffn2_wg, ffn2_wu, ffn2_wd, w_in, sgu_ln_g, sgu_ln_b, sgu_w, sgu_b, w_out, final_g, loss_target, m_ada_w, m_ada_b, m_norm_g, m_ffn1_wg, m_ffn1_wu, m_ffn1_wd, m_ffn2_wg, m_ffn2_wu, m_ffn2_wd, m_w_in, m_sgu_ln_g, m_sgu_ln_b, m_sgu_w, m_sgu_b, m_w_out, m_final_g, v_ada_w, v_ada_b, v_norm_g, v_ffn1_wg, v_ffn1_wu, v_ffn1_wd, v_ffn2_wg, v_ffn2_wu, v_ffn2_wd, v_w_in, v_sgu_ln_g, v_sgu_ln_b, v_sgu_w, v_sgu_b, v_w_out, v_final_g):
    me, c_all, pars = _prepare(c, ada_w, ada_b, norm_g)

    tr = lambda w: jnp.swapaxes(w, -1, -2).astype(bf16)
    locs = []
    for l in range(DEPTH):
        locs.append(jnp.stack([tr(ffn1_wg[l]), tr(ffn1_wu[l]), ffn1_wd[l].astype(bf16)]))
        locs.append(jnp.stack([tr(ffn2_wg[l]), tr(ffn2_wu[l]), ffn2_wd[l].astype(bf16)]))
        locs.append(tr(w_in[l]))
        locs.append(w_out[l].astype(bf16))
    locs, pars = lax.optimization_barrier((locs, pars))

    first, locs = lax.optimization_barrier((_all_gather_rows([locs[0]], "gather_first")[0], locs))
    placed = [_place_own(a, me) for a in locs[1:]]
    ready = {0: first}
    groups = ([2, 3], [1], [4], [6, 7], [5])
    flying = {}

    def start_group(gi, behind):
        pieces = groups[gi]
        arrays, _ = lax.optimization_barrier(([placed[p - 1] for p in pieces], behind))
        started = _exchange_start(False, arrays, "gather_start_%d" % gi)
        flying.update({p: (gi, started[:-1]) for p in pieces})
        return started[-1]

    first_token = start_group(0, first)

    def get_w(piece, after):
        token = first_token if piece == 0 else None
        if piece not in ready:
            gi, started = flying[piece]
            got_w = _exchange_wait(False, started, after, "gather_wait_%d" % gi)
            ready.update(zip(groups[gi], got_w))
            if gi + 1 < len(groups):
                token = start_group(gi + 1, got_w)
        return ready[piece], token

    sent = {}

    def put_g(piece, grad):
        land = lax.empty((NDEV,) + grad.shape[:-2] + (grad.shape[-2] // NDEV, D), bf16)
        started = _exchange_start(True, [grad, land], "scatter_start_%d" % piece)
        sent[piece] = started[:-1]
        return started[-1]

    small_sent = []

    def small_ready(head, dmods, dgains, dsgu):
        loss_part = 0.5 * jnp.sum(head[0]) / D
        small = jnp.concatenate([
            _pad_rows(jnp.stack(dmods)), _pad_rows(jnp.stack(dgains)),
            _pad_rows(jnp.stack([s[0] for s in dsgu])), _pad_rows(jnp.stack([s[1] for s in dsgu])),
            _pad_rows(jnp.stack([s[3] for s in dsgu])), _pad_rows(jnp.stack([s[2] for s in dsgu])),
            _pad_rows(head[3]), _pad_rows(jnp.pad(loss_part[None], (0, CH - 1)))], axis=0)
        started = _exchange_start(False, [_place_own(small, me, "place_small")], "small_start")
        small_sent.append(started[:-1])
        return started[-1]

    dh, last = _fwd_bwd(x[0], loss_target[0], pars, get_w, put_g, small_ready, sgu_ln_g, sgu_ln_b, sgu_w, sgu_b, final_g)
    grad_x = dh[None]

    got = _exchange_wait(False, small_sent[0], last, "small_wait")[0].reshape(NDEV, -1, CH)
    tot = _sum_slots(got, "sum_small")
    n_mod, n_gain, n_sw = DEPTH * 9 * D // CH, DEPTH * 3 * D // CH, DEPTH * 4 * CH
    offs = [0, n_mod, n_mod + n_gain, n_mod + n_gain + 8, n_mod + n_gain + 16, n_mod + n_gain + 24]
    g_ada_b = tot[offs[0]:offs[1]].reshape(DEPTH, 9 * D)
    g_gain_full = tot[offs[1]:offs[2]].reshape(DEPTH, 3, D)
    g_ln_g = tot[offs[2]:offs[3]].reshape(DEPTH, 4, CH)
    g_ln_b = tot[offs[3]:offs[4]].reshape(DEPTH, 4, CH)
    g_sb = tot[offs[4]:offs[5]].reshape(DEPTH, 4, CH)
    g_sw = tot[offs[5]:offs[5] + n_sw].reshape(DEPTH, 4, CH, CH)
    g_final = tot[offs[5] + n_sw:offs[5] + n_sw + 8].reshape(D)
    loss = tot[offs[5] + n_sw + 8, 0]
    g_norm = lax.dynamic_slice(g_gain_full, (0, 0, me * OUTS), (DEPTH, 3, OUTS))

    dmod_all = got[:, offs[0]:offs[1]].reshape(NDEV, DEPTH, 9 * D)
    dmod_cols = lax.dynamic_slice(dmod_all, (0, 0, me * ADAS), (NDEV, DEPTH, ADAS)).transpose(1, 0, 2)
    g_ada_w = _ada_grad((c_all * jax.nn.sigmoid(c_all)).T, dmod_cols)

    sums, after = {}, tot

    def collect(piece, after):
        own, land = _exchange_wait(True, sent[piece], after, "scatter_wait_%d" % piece)
        sums[piece] = _sum_partials(land, own, me, "sum_grads")
        return sums[piece]

    for piece in (5, 7, 6, 4, 1, 3, 2):
        after = collect(piece, after)
    back = lambda t: jnp.swapaxes(t, -1, -2)
    f2 = jnp.stack([sums[4 * l + 1] for l in range(DEPTH)])
    g_w_in = back(jnp.stack([sums[4 * l + 2] for l in range(DEPTH)]))
    g_w_out = jnp.stack([sums[4 * l + 3] for l in range(DEPTH)])

    ws = [ada_w, ada_b, norm_g, ffn1_wg, ffn1_wu, ffn1_wd, ffn2_wg, ffn2_wu, ffn2_wd, w_in, sgu_ln_g, sgu_ln_b, sgu_w,
          sgu_b, w_out, final_g]
    ms = [m_ada_w, m_ada_b, m_norm_g, m_ffn1_wg, m_ffn1_wu, m_ffn1_wd, m_ffn2_wg, m_ffn2_wu, m_ffn2_wd, m_w_in,
          m_sgu_ln_g, m_sgu_ln_b, m_sgu_w, m_sgu_b, m_w_out, m_final_g]
    vs = [v_ada_w, v_ada_b, v_norm_g, v_ffn1_wg, v_ffn1_wu, v_ffn1_wd, v_ffn2_wg, v_ffn2_wu, v_ffn2_wd, v_w_in,
          v_sgu_ln_g, v_sgu_ln_b, v_sgu_w, v_sgu_b, v_w_out, v_final_g]
    gw = [g_ada_w, g_ada_b, g_norm, None, None, None, back(f2[:, 0]), back(f2[:, 1]), f2[:, 2],
          g_w_in, g_ln_g, g_ln_b, g_sw, g_sb, g_w_out, g_final]
    upd = [None] * len(ws)
    for i in (1, 2, 10, 11, 12, 13, 15, 6, 7, 8, 9, 14, 0):
        upd[i] = _adamw_nd(ws[i], gw[i], ms[i], vs[i], "adamw")
    collect(0, upd[0][0])
    f1 = jnp.stack([sums[4 * l] for l in range(DEPTH)])
    gw[3:6] = [back(f1[:, 0]), back(f1[:, 1]), f1[:, 2]]
    for i in (3, 4, 5):
        upd[i] = _adamw_nd(ws[i], gw[i], ms[i], vs[i], "adamw")
    return (loss, grad_x, *gw, *[u[0] for u in upd], *[u[1] for u in upd], *[u[2] for u in upd])
```

```python
import functools
import math

import jax
import jax.numpy as jnp
from jax import lax
from jax.experimental import pallas as pl
from jax.experimental.pallas import tpu as pltpu

f32, bf16 = jnp.float32, jnp.bfloat16
SDS = jax.ShapeDtypeStruct

T, D, F = 4096, 1024, 2816
NDEV, DEPTH = 8, 2
HW = 512
INW = 5 * HW
FS, INS, OUTS, ADAS = F // NDEV, INW // NDEV, D // NDEV, 9 * D // NDEV
CH = 128
PATTERN_DILATIONS = (1, 4, 16)
ROPE_THETA = 10000.0
EPS = 1e-6
LR, B1, B2, AEPS, WD, STEP = 0.001, 0.9, 0.999, 1e-08, 0.01, 10
MESH = pl.DeviceIdType.MESH


def _cp(sems, vmem_mb):
    return pltpu.CompilerParams(dimension_semantics=sems, vmem_limit_bytes=vmem_mb << 20)


def _nn(a, b):
    return lax.dot_general(a, b, (((1,), (0,)), ((), ())), preferred_element_type=f32)


def _nt(a, b):
    return lax.dot_general(a, b, (((1,), (1,)), ((), ())), preferred_element_type=f32)


def _tn(a, b):
    return lax.dot_general(a, b, (((0,), (0,)), ((), ())), preferred_element_type=f32)


def _colsum(a):
    return jnp.sum(a, axis=0, keepdims=True)


def _rowmean(a):
    return jnp.mean(a, axis=-1, keepdims=True)


def _norm_mod(x, par):
    rstd = lax.rsqrt(_rowmean(x * x) + EPS)
    xhat = x * rstd
    n = xhat * par[3:4, :]
    y = n * (1.0 + par[1:2, :]) + par[0:1, :]
    return y, n, xhat, rstd


def _norm_mod_bwd(dy, x, par):
    _, n, xhat, rstd = _norm_mod(x, par)
    dn = dy * (1.0 + par[1:2, :])
    dxhat = dn * par[3:4, :]
    dx = rstd * (dxhat - xhat * _rowmean(dxhat * xhat))
    return dx, _colsum(dy), _colsum(dy * n), _colsum(dn * xhat)


_GK = math.sqrt(2.0 / math.pi)


def _gelu(x):
    return 0.5 * x * (1.0 + jnp.tanh(_GK * (x + 0.044715 * x * x * x)))


def _gelu_grad(x):
    t = jnp.tanh(_GK * (x + 0.044715 * x * x * x))
    return 0.5 * (1.0 + t) + 0.5 * x * (1.0 - t * t) * (_GK * (1.0 + 3.0 * 0.044715 * x * x))


def _rows(ref, idx, r):
    if len(ref.shape) == 3:
        return ref.at[:, pl.ds(idx * r, r), :]
    return ref.at[pl.ds(idx * r, r), :]


def _flip(v, bit):
    return 1 - v if bit else v


def _all_gather_small(x, name):
    R, C = x.shape

    def body(x_ref, out_ref, send_sems, recv_sems):
        mx, my, mc = lax.axis_index("x"), lax.axis_index("y"), lax.axis_index("c")
        me = 4 * mx + 2 * my + mc
        out_ref[me] = x_ref[...]
        sent = []
        for k in range(1, NDEV):
            peer = (_flip(mx, k & 4), _flip(my, k & 2), _flip(mc, k & 1))
            cp = pltpu.make_async_remote_copy(
                src_ref=x_ref, dst_ref=out_ref.at[me], send_sem=send_sems.at[k - 1],
                recv_sem=recv_sems.at[k - 1], device_id=peer, device_id_type=MESH)
            cp.start()
            sent.append(cp)
        for k in range(1, NDEV):
            peer = (_flip(mx, k & 4), _flip(my, k & 2), _flip(mc, k & 1))
            pidx = 4 * peer[0] + 2 * peer[1] + peer[2]
            pltpu.make_async_remote_copy(
                src_ref=x_ref, dst_ref=out_ref.at[pidx], send_sem=send_sems.at[k - 1],
                recv_sem=recv_sems.at[k - 1], device_id=peer, device_id_type=MESH).wait_recv()
        for cp in sent:
            cp.wait_send()

    vm = pl.BlockSpec(memory_space=pltpu.VMEM)
    return pl.pallas_call(
        body, name=name, out_shape=SDS((NDEV, R, C), f32), in_specs=[vm], out_specs=vm,
        scratch_shapes=[pltpu.SemaphoreType.DMA((NDEV - 1,)), pltpu.SemaphoreType.DMA((NDEV - 1,))],
        compiler_params=pltpu.CompilerParams(vmem_limit_bytes=32 << 20),
    )(x)


def _all_gather_rows(locs, name):
    n = len(locs)
    rs = [a.shape[-2] for a in locs]

    def body(*refs):
        src, out = refs[:n], refs[n:2 * n]
        send_sems, recv_sems, loc_sems = refs[2 * n:]
        mx, my, mc = lax.axis_index("x"), lax.axis_index("y"), lax.axis_index("c")
        me, sib = (mx, my, mc), (mx, my, 1 - mc)
        chips = [(1 - mx, my), (mx, 1 - my), (1 - mx, 1 - my)]

        def blk(a, p):
            return _rows(out[a], 4 * p[0] + 2 * p[1] + p[2], rs[a])

        def copy(k, a, block, to, from_src=False):
            return pltpu.make_async_remote_copy(
                src_ref=src[a] if from_src else blk(a, block), dst_ref=blk(a, block),
                send_sem=send_sems.at[k * n + a], recv_sem=recv_sems.at[k * n + a],
                device_id=to, device_id_type=MESH)

        mine = [pltpu.make_async_copy(src[a], blk(a, me), loc_sems.at[a]) for a in range(n)]
        for m in mine:
            m.start()
        first = []
        for j, chip in enumerate(chips):
            first += [copy(1 + j, a, me, (*chip, mc), True) for a in range(n)]
        first += [copy(0, a, me, sib, True) for a in range(n)]
        for cp in first:
            cp.start()
        passed = []
        for j, chip in enumerate(chips):
            for a in range(n):
                copy(1 + j, a, (*chip, mc), me).wait_recv()
            fwd = [copy(4 + j, a, (*chip, mc), sib) for a in range(n)]
            for cp in fwd:
                cp.start()
            passed += fwd
        for a in range(n):
            copy(0, a, sib, me).wait_recv()
        for j, chip in enumerate(chips):
            for a in range(n):
                copy(4 + j, a, (*chip, 1 - mc), me).wait_recv()
        for cp in first + passed:
            cp.wait_send()
        for m in mine:
            m.wait()

    hbm = pl.BlockSpec(memory_space=pl.ANY)
    out_shape = [SDS(a.shape[:-2] + (NDEV * a.shape[-2], a.shape[-1]), a.dtype) for a in locs]
    return pl.pallas_call(
        body, name=name, out_shape=out_shape, in_specs=[hbm] * n, out_specs=[hbm] * n,
        scratch_shapes=[pltpu.SemaphoreType.DMA((7 * n,)), pltpu.SemaphoreType.DMA((7 * n,)),
                        pltpu.SemaphoreType.DMA((n,))],
    )(*locs)


HBM_SPEC = pl.BlockSpec(memory_space=pltpu.HBM)
SEM_SPEC = pl.BlockSpec(memory_space=pltpu.SEMAPHORE)
DATAFLOW_EFFECT = pltpu.SideEffectType.DATAFLOW_SIDE_EFFECTING


def _place_own(loc, me, name="place_own"):
    r, cols = loc.shape[-2:]
    loc3 = loc.reshape(-1, r, cols)
    kk = loc3.shape[0]

    def body(me_ref, src_ref, full_ref, out_ref):
        out_ref[...] = src_ref[...]

    out = pl.pallas_call(
        body, name=name, out_shape=SDS((kk, NDEV * r, cols), loc.dtype),
        grid_spec=pltpu.PrefetchScalarGridSpec(
            num_scalar_prefetch=1, grid=(kk,),
            in_specs=[pl.BlockSpec((None, r, cols), lambda i, me_ref: (i, 0, 0)), pl.BlockSpec(memory_space=pl.ANY)],
            out_specs=pl.BlockSpec((None, r, cols), lambda i, me_ref: (i, me_ref[0], 0))),
        input_output_aliases={2: 0}, compiler_params=_cp(("arbitrary",), 32),
    )(me.reshape(1), loc3, lax.empty((kk, NDEV * r, cols), loc.dtype))
    return out.reshape(loc.shape[:-2] + (NDEV * r, cols))


def _exchange_copies(scatter, bufs, n, send_sems, recv_sems):
    mx, my, mc = lax.axis_index("x"), lax.axis_index("y"), lax.axis_index("c")
    me = 4 * mx + 2 * my + mc
    out = []
    for k in range(1, NDEV):
        peer = (_flip(mx, k & 4), _flip(my, k & 2), _flip(mc, k & 1))
        pidx = 4 * peer[0] + 2 * peer[1] + peer[2]
        for a in range(n):
            r = bufs[a].shape[-2] // NDEV
            if scatter:
                src, dst, arrive = _rows(bufs[a], pidx, r), bufs[n + a].at[me], bufs[n + a].at[pidx]
            else:
                src, dst, arrive = _rows(bufs[a], me, r), _rows(bufs[a], me, r), _rows(bufs[a], pidx, r)
            sems = dict(send_sem=send_sems.at[(k - 1) * n + a], recv_sem=recv_sems.at[(k - 1) * n + a],
                        device_id=peer, device_id_type=MESH)
            out.append((pltpu.make_async_remote_copy(src_ref=src, dst_ref=dst, **sems),
                        pltpu.make_async_remote_copy(src_ref=src, dst_ref=arrive, **sems)))
    return out


def _exchange_start(scatter, arrays, name):
    m = len(arrays)
    n = m // 2 if scatter else m

    def body(*refs):
        send_sems, recv_sems, token = refs[m], refs[m + 1], refs[-1]
        for go, _ in _exchange_copies(scatter, refs[:m], n, send_sems, recv_sems):
            go.start()
        token[...] = jnp.zeros_like(token)

    sems = pltpu.SemaphoreType.DMA((7 * n,))
    return pl.pallas_call(
        body, name=name, in_specs=[HBM_SPEC] * m,
        out_shape=(sems, sems, *[pltpu.HBM(a.shape, a.dtype) for a in arrays], SDS((8, CH), f32)),
        out_specs=(SEM_SPEC, SEM_SPEC, *[HBM_SPEC] * m, pl.BlockSpec(memory_space=pltpu.VMEM)),
        input_output_aliases={a: 2 + a for a in range(m)},
        compiler_params=pltpu.CompilerParams(has_side_effects=DATAFLOW_EFFECT),
    )(*[pltpu.with_memory_space_constraint(a, pltpu.HBM) for a in arrays])


def _exchange_wait(scatter, started, after, name):
    send_sems, recv_sems, *arrays = started
    m = len(arrays)
    n = m // 2 if scatter else m

    def body(*refs):
        for go, arrive in _exchange_copies(scatter, refs[:m], n, refs[m], refs[m + 1]):
            go.wait_send()
            arrive.wait_recv()

    return pl.pallas_call(
        body, name=name, in_specs=[HBM_SPEC] * m + [SEM_SPEC, SEM_SPEC, pl.BlockSpec(memory_space=pl.ANY)],
        out_shape=[pltpu.HBM(a.shape, a.dtype) for a in arrays], out_specs=[HBM_SPEC] * m,
        input_output_aliases={a: a for a in range(m)},
        compiler_params=pltpu.CompilerParams(has_side_effects=DATAFLOW_EFFECT),
    )(*arrays, send_sems, recv_sems, after)


def _mod_fwd(c_all, ada_w, ada_b_loc):
    def body(c_ref, w_ref, b_ref, o_ref):
        ca = c_ref[...]
        ca = ca * jax.nn.sigmoid(ca)
        o_ref[...] = jnp.dot(ca, w_ref[...], precision=lax.Precision.HIGHEST,
                             preferred_element_type=f32) + b_ref[...]

    return pl.pallas_call(
        body, name="mod_fwd", grid=(DEPTH,), out_shape=SDS((DEPTH, NDEV, ADAS), f32),
        in_specs=[pl.BlockSpec((NDEV, D), lambda l: (0, 0)),
                  pl.BlockSpec((None, D, ADAS), lambda l: (l, 0, 0)),
                  pl.BlockSpec((None, 1, ADAS), lambda l: (l, 0, 0))],
        out_specs=pl.BlockSpec((None, NDEV, ADAS), lambda l: (l, 0, 0)),
        compiler_params=_cp(("arbitrary",), 32),
    )(c_all, ada_w, ada_b_loc)


def _ada_grad(cact_t, dmod_cols):
    def body(c_ref, d_ref, o_ref):
        acc = c_ref[:, 0:1] * d_ref[0:1, :]
        for b in range(1, NDEV):
            acc = acc + c_ref[:, b:b + 1] * d_ref[b:b + 1, :]
        o_ref[...] = acc

    tr = 256
    return pl.pallas_call(
        body, name="ada_grad", grid=(DEPTH, D // tr), out_shape=SDS((DEPTH, D, ADAS), f32),
        in_specs=[pl.BlockSpec((tr, NDEV), lambda l, i: (i, 0)),
                  pl.BlockSpec((None, NDEV, ADAS), lambda l, i: (l, 0, 0))],
        out_specs=pl.BlockSpec((None, tr, ADAS), lambda l, i: (l, i, 0)),
        compiler_params=_cp(("arbitrary", "arbitrary"), 32),
    )(cact_t, dmod_cols)


def _ffn_fwd(h, par, w3):
    tm, tf = 1024, 256
    nj = F // tf

    def body(h_ref, par_ref, wg_ref, wu_ref, wd_ref, ho_ref, fo_ref, g_ref, u_ref, a_ref, y_scr, acc):
        j = pl.program_id(1)

        @pl.when(j == 0)
        def _():
            y_scr[...] = _norm_mod(h_ref[...], par_ref[...])[0].astype(bf16)
            acc[...] = jnp.zeros_like(acc)

        y = y_scr[...]
        g = _nt(y, wg_ref[...])
        u = _nt(y, wu_ref[...])
        a = ((g * jax.nn.sigmoid(g)) * u).astype(bf16)
        g_ref[...] = g.astype(bf16)
        u_ref[...] = u.astype(bf16)
        a_ref[...] = a
        acc[...] += _nn(a, wd_ref[...])

        @pl.when(j == nj - 1)
        def _():
            fo_ref[...] = acc[...].astype(bf16)
            ho_ref[...] = h_ref[...] + (0.5 * par_ref[2:3, :]) * acc[...]

    row = pl.BlockSpec((tm, D), lambda i, j: (i, 0))
    hid = pl.BlockSpec((tm, tf), lambda i, j: (i, j))
    wspec = [pl.BlockSpec((None, tf, D), lambda i, j, k=k: (k, j, 0)) for k in range(3)]
    return pl.pallas_call(
        body, name="ffn_fwd", grid=(T // tm, nj),
        in_specs=[row, pl.BlockSpec((8, D), lambda i, j: (0, 0))] + wspec,
        out_specs=[row, row, hid, hid, hid],
        out_shape=[SDS((T, D), f32), SDS((T, D), bf16), SDS((T, F), bf16), SDS((T, F), bf16), SDS((T, F), bf16)],
        scratch_shapes=[pltpu.VMEM((tm, D), bf16), pltpu.VMEM((tm, D), f32)],
        compiler_params=_cp(("arbitrary", "arbitrary"), 52),
    )(h, par, w3, w3, w3)


def _ffn_bwd_tok(dh, h, par, fo, gs, us, w3):
    tm, tf = 512, 256
    nj = F // tf

    def body(dh_ref, h_ref, par_ref, fo_ref, g_ref, u_ref, wg_ref, wu_ref, wd_ref,
             dhin_ref, dg_ref, du_ref, y_ref, dfb_ref, dpar_ref, df_scr, dyacc):
        i, j = pl.program_id(0), pl.program_id(1)

        @pl.when(jnp.logical_and(i == 0, j == 0))
        def _():
            dpar_ref[...] = jnp.zeros_like(dpar_ref)

        @pl.when(j == 0)
        def _():
            dh_v, par_v = dh_ref[...], par_ref[...]
            dfb = ((0.5 * par_v[2:3, :]) * dh_v).astype(bf16)
            df_scr[...] = dfb
            dfb_ref[...] = dfb
            dpar_ref[2:3, :] += 0.5 * _colsum(dh_v * fo_ref[...].astype(f32))
            y_ref[...] = _norm_mod(h_ref[...], par_v)[0].astype(bf16)
            dyacc[...] = jnp.zeros_like(dyacc)

        da = _nt(df_scr[...], wd_ref[...])
        g = g_ref[...].astype(f32)
        u = u_ref[...].astype(f32)
        sig = jax.nn.sigmoid(g)
        du = (da * (g * sig)).astype(bf16)
        dg = (da * u * (sig * (1.0 + g * (1.0 - sig)))).astype(bf16)
        dg_ref[...] = dg
        du_ref[...] = du
        dyacc[...] += _nn(dg, wg_ref[...]) + _nn(du, wu_ref[...])

        @pl.when(j == nj - 1)
        def _():
            dx, dsh, dsc, dng = _norm_mod_bwd(dyacc[...], h_ref[...], par_ref[...])
            dhin_ref[...] = dh_ref[...] + dx
            dpar_ref[0:1, :] += dsh
            dpar_ref[1:2, :] += dsc
            dpar_ref[3:4, :] += dng

    row = pl.BlockSpec((tm, D), lambda i, j: (i, 0))
    hid = pl.BlockSpec((tm, tf), lambda i, j: (i, j))
    one = pl.BlockSpec((8, D), lambda i, j: (0, 0))
    wspec = [pl.BlockSpec((None, tf, D), lambda i, j, k=k: (k, j, 0)) for k in range(3)]
    return pl.pallas_call(
        body, name="ffn_bwd_tok", grid=(T // tm, nj),
        in_specs=[row, row, one, row, hid, hid] + wspec,
        out_specs=[row, hid, hid, row, row, one],
        out_shape=[SDS((T, D), f32), SDS((T, F), bf16), SDS((T, F), bf16), SDS((T, D), bf16),
                   SDS((T, D), bf16), SDS((8, D), f32)],
        scratch_shapes=[pltpu.VMEM((tm, D), bf16), pltpu.VMEM((tm, D), f32)],
        compiler_params=_cp(("arbitrary", "arbitrary"), 48),
    )(dh, h, par, fo, gs, us, w3, w3, w3)


def _ffn_bwd_w(a_s, dg_s, du_s, y, dfb, anchor=None):
    tf, tk = F // 2, 256
    nk = T // tk

    def body(a_ref, dg_ref, du_ref, y_ref, df_ref, *rest):
        o_ref, accg, accu, accd = rest[-4:]
        kk = pl.program_id(1)

        @pl.when(kk == 0)
        def _():
            accg[...] = jnp.zeros_like(accg)
            accu[...] = jnp.zeros_like(accu)
            accd[...] = jnp.zeros_like(accd)

        yv = y_ref[...]
        accg[...] += _tn(dg_ref[...], yv)
        accu[...] += _tn(du_ref[...], yv)
        accd[...] += _tn(a_ref[...], df_ref[...])

        @pl.when(kk == nk - 1)
        def _():
            o_ref[0] = accg[...].astype(bf16)
            o_ref[1] = accu[...].astype(bf16)
            o_ref[2] = accd[...].astype(bf16)

    hid = pl.BlockSpec((tk, tf), lambda j, kk: (kk, j))
    row = pl.BlockSpec((tk, D), lambda j, kk: (kk, 0))
    extra = [] if anchor is None else [anchor]
    return pl.pallas_call(
        body, name="ffn_bwd_w", grid=(F // tf, nk),
        in_specs=[hid, hid, hid, row, row] + [pl.BlockSpec((8, CH), lambda j, kk: (0, 0))] * len(extra),
        out_specs=pl.BlockSpec((3, tf, D), lambda j, kk: (0, j, 0)), out_shape=SDS((3, F, D), bf16),
        scratch_shapes=[pltpu.VMEM((tf, D), f32)] * 3,
        compiler_params=_cp(("arbitrary", "arbitrary"), 48),
    )(a_s, dg_s, du_s, y, dfb, *extra)


def _tn_matmul(a, b, bm, name):
    M, N = a.shape[1], b.shape[1]
    tk = 512
    nk = T // tk

    def body(a_ref, b_ref, o_ref, acc):
        kk = pl.program_id(1)

        @pl.when(kk == 0)
        def _():
            acc[...] = jnp.zeros_like(acc)

        acc[...] += _tn(a_ref[...], b_ref[...])

        @pl.when(kk == nk - 1)
        def _():
            o_ref[...] = acc[...].astype(bf16)

    return pl.pallas_call(
        body, name=name, grid=(M // bm, nk),
        in_specs=[pl.BlockSpec((tk, bm), lambda i, kk: (kk, i)), pl.BlockSpec((tk, N), lambda i, kk: (kk, 0))],
        out_specs=pl.BlockSpec((bm, N), lambda i, kk: (i, 0)), out_shape=SDS((M, N), bf16),
        scratch_shapes=[pltpu.VMEM((bm, N), f32)],
        compiler_params=_cp(("arbitrary", "arbitrary"), 40),
    )(a, b)


def _rope_tables():
    inv = ROPE_THETA ** (-jnp.arange(0, 64, 2, dtype=f32) / 64)
    ang = jnp.arange(T, dtype=f32)[:, None] * inv[None, :]
    ang = jnp.concatenate([ang, ang], axis=-1)
    cos, sin = jnp.tile(jnp.cos(ang), (1, 8)), jnp.tile(jnp.sin(ang), (1, 8))
    low = (jnp.arange(HW) % 64 < 32)[None, :]
    return cos, jnp.where(low, -sin, 0.0), jnp.where(low, 0.0, sin)


def _rope(t, cos, sin_lo, sin_hi):
    return t * cos + pltpu.roll(t, HW - 32, 1) * sin_lo + pltpu.roll(t, 32, 1) * sin_hi


def _rope_t(g, cos, sin_lo, sin_hi):
    return g * cos + pltpu.roll(g * sin_lo, 32, 1) + pltpu.roll(g * sin_hi, HW - 32, 1)


def _inproj_fwd(h, par, w_in_t, tabs):
    tm = 512

    def body(h_ref, par_ref, w_ref, cos_ref, slo_ref, shi_ref, ua_ref, va_ref, q_ref, k_ref, v_ref):
        y = _norm_mod(h_ref[...], par_ref[...])[0].astype(bf16)
        proj = lambda c: _nt(y, w_ref[c * HW:(c + 1) * HW, :])
        ua_ref[...] = proj(0)
        va_ref[...] = proj(1)
        cos, slo, shi = cos_ref[...], slo_ref[...], shi_ref[...]
        q_ref[...] = (_rope(proj(2), cos, slo, shi) * 0.125).astype(bf16)
        k_ref[...] = _rope(proj(3), cos, slo, shi).astype(bf16)
        v_ref[...] = proj(4).astype(bf16)

    row = pl.BlockSpec((tm, D), lambda i: (i, 0))
    half = pl.BlockSpec((tm, HW), lambda i: (i, 0))
    return pl.pallas_call(
        body, name="inproj_fwd", grid=(T // tm,),
        in_specs=[row, pl.BlockSpec((8, D), lambda i: (0, 0)), pl.BlockSpec((INW, D), lambda i: (0, 0)), half, half, half],
        out_specs=[half] * 5,
        out_shape=[SDS((T, HW), f32), SDS((T, HW), f32), SDS((T, HW), bf16), SDS((T, HW), bf16), SDS((T, HW), bf16)],
        compiler_params=_cp(("arbitrary",), 48),
    )(h, par, w_in_t, *tabs)


def _head_masks():
    lane = lax.broadcasted_iota(jnp.int32, (1, CH), 1)
    return lane < 64, lane >= 64


def _stack_heads(x):
    lo, hi = _head_masks()
    zero = jnp.zeros_like(x)
    return jnp.concatenate([jnp.where(lo, x, zero), jnp.where(hi, x, zero)], axis=0)


def _band_mask(has_prev):
    row = lax.broadcasted_iota(jnp.int32, (2 * CH, 2 * CH), 0) & (CH - 1)
    col = lax.broadcasted_iota(jnp.int32, (2 * CH, 2 * CH), 1)
    in_prev = jnp.logical_and(jnp.logical_and(col < CH, col >= row), has_prev)
    return jnp.logical_or(in_prev, jnp.logical_and(col >= CH, col - CH <= row))


def _attn_fwd(q, k, v, d, anchor=None):
    L = T // d
    nb = L // CH

    def body(q_ref, kc_ref, kp_ref, vc_ref, vp_ref, *rest):
        o_ref, lse_ref = rest[-2:]
        n = pl.program_id(1)
        col = lax.broadcasted_iota(jnp.int32, (CH, CH), 1)
        mask = _band_mask(n > 0)
        lo, _ = _head_masks()
        lse_tile = jnp.zeros((CH, CH), f32)
        for hp in range(4):
            sl = slice(hp * CH, (hp + 1) * CH)
            kk = jnp.concatenate([kp_ref[:, sl], kc_ref[:, sl]], axis=0)
            vv = jnp.concatenate([vp_ref[:, sl], vc_ref[:, sl]], axis=0)
            s = jnp.where(mask, _nt(_stack_heads(q_ref[:, sl]), kk), -jnp.inf)
            m = jnp.max(s, axis=-1, keepdims=True)
            p = jnp.exp(s - m)
            den = jnp.sum(p, axis=-1, keepdims=True)
            o = _nn(p.astype(bf16), vv) / den
            o_ref[:, sl] = jnp.where(lo, o[:CH], o[CH:])
            lse = m + jnp.log(den)
            lse_tile = jnp.where(col == 2 * hp, lse[:CH], jnp.where(col == 2 * hp + 1, lse[CH:], lse_tile))
        lse_ref[...] = lse_tile

    cur = pl.BlockSpec((CH, HW), lambda r, n: (n, r))
    prev = pl.BlockSpec((CH, HW), lambda r, n: (jnp.maximum(n - 1, 0), r))
    q2, k2, v2 = (t.reshape(L, d * HW) for t in (q, k, v))
    extra = [] if anchor is None else [anchor]
    o, lse = pl.pallas_call(
        body, name="attn_fwd_d%d" % d, grid=(d, nb),
        in_specs=[cur, cur, prev, cur, prev] + [pl.BlockSpec((8, CH), lambda r, n: (0, 0))] * len(extra),
        out_specs=[cur, pl.BlockSpec((CH, CH), lambda r, n: (n, r))],
        out_shape=[SDS((L, d * HW), f32), SDS((L, d * CH), f32)],
        compiler_params=_cp(("arbitrary", "arbitrary"), 32),
    )(q2, k2, k2, v2, v2, *extra)
    return o.reshape(T, HW), lse.reshape(T, CH)


def _attn_combine(os_, lses):
    tm = 512

    def body(o0_ref, o1_ref, o2_ref, l0_ref, l1_ref, l2_ref, o_ref, lse_ref):
        l0, l1, l2 = l0_ref[...], l1_ref[...], l2_ref[...]
        m = jnp.maximum(jnp.maximum(l0, l1), l2)
        e = [jnp.exp(l0 - m), jnp.exp(l1 - m), jnp.exp(l2 - m)]
        s = e[0] + e[1] + e[2]
        w = [ei / s for ei in e]
        lse_ref[...] = m + jnp.log(s)
        lo, _ = _head_masks()
        for hp in range(4):
            sl = slice(hp * CH, (hp + 1) * CH)
            acc = jnp.zeros((tm, CH), f32)
            for wp, op_ref in zip(w, (o0_ref, o1_ref, o2_ref)):
                wexp = jnp.where(lo, wp[:, 2 * hp:2 * hp + 1], wp[:, 2 * hp + 1:2 * hp + 2])
                acc = acc + wexp * op_ref[:, sl]
            o_ref[:, sl] = acc.astype(bf16)

    half = pl.BlockSpec((tm, HW), lambda i: (i, 0))
    stat = pl.BlockSpec((tm, CH), lambda i: (i, 0))
    return pl.pallas_call(
        body, name="attn_combine", grid=(T // tm,), in_specs=[half] * 3 + [stat] * 3, out_specs=[half, stat],
        out_shape=[SDS((T, HW), bf16), SDS((T, CH), f32)],
        compiler_params=_cp(("arbitrary",), 32),
    )(*os_, *lses)


def _attn_bwd(q, k, v, mixed, do, lse, d):
    L = T // d
    nb = L // CH

    def body(qc_ref, qn_ref, kc_ref, kp_ref, vc_ref, vp_ref, oc_ref, on_ref, dc_ref, dn_ref, lc_ref, ln_ref,
             dq_ref, dk_ref, dv_ref):
        n = pl.program_id(1)
        mask_c = _band_mask(n > 0)
        mask_n = _band_mask(n < nb - 1)[:, :CH]
        lo, _ = _head_masks()
        lse_c, lse_n = lc_ref[...], ln_ref[...]
        for hp in range(4):
            sl = slice(hp * CH, (hp + 1) * CH)
            kc, vc = kc_ref[:, sl], vc_ref[:, sl]
            kk = jnp.concatenate([kp_ref[:, sl], kc], axis=0)
            vv = jnp.concatenate([vp_ref[:, sl], vc], axis=0)
            doc, don = dc_ref[:, sl], dn_ref[:, sl]
            qs_c, qs_n, ds_c, ds_n = _stack_heads(qc_ref[:, sl]), _stack_heads(qn_ref[:, sl]), _stack_heads(doc), _stack_heads(don)
            delta_c = jnp.sum(_stack_heads(doc.astype(f32) * oc_ref[:, sl].astype(f32)), axis=-1, keepdims=True)
            delta_n = jnp.sum(_stack_heads(don.astype(f32) * on_ref[:, sl].astype(f32)), axis=-1, keepdims=True)
            heads = lambda t: jnp.concatenate([t[:, 2 * hp:2 * hp + 1], t[:, 2 * hp + 1:2 * hp + 2]], axis=0)
            p1 = jnp.where(mask_c, jnp.exp(_nt(qs_c, kk) - heads(lse_c)), 0.0)
            g1 = (p1 * (_nt(ds_c, vv) - delta_c)).astype(bf16)
            dq = _nn(g1, kk)
            dq_ref[:, sl] = jnp.where(lo, dq[:CH], dq[CH:])
            p2 = jnp.where(mask_n, jnp.exp(_nt(qs_n, kc) - heads(lse_n)), 0.0)
            g2 = (p2 * (_nt(ds_n, vc) - delta_n)).astype(bf16)
            dk_ref[:, sl] = _tn(jnp.concatenate([g1[:, CH:], g2], axis=0), jnp.concatenate([qs_c, qs_n], axis=0))
            dv_ref[:, sl] = _tn(jnp.concatenate([p1[:, CH:].astype(bf16), p2.astype(bf16)], axis=0),
                                jnp.concatenate([ds_c, ds_n], axis=0))

    cur = pl.BlockSpec((CH, HW), lambda r, n: (n, r))
    prev = pl.BlockSpec((CH, HW), lambda r, n: (jnp.maximum(n - 1, 0), r))
    nxt = pl.BlockSpec((CH, HW), lambda r, n: (jnp.minimum(n + 1, nb - 1), r))
    ocur = pl.BlockSpec((CH, HW), lambda r, n: (n, 2 * r + 1))
    onxt = pl.BlockSpec((CH, HW), lambda r, n: (jnp.minimum(n + 1, nb - 1), 2 * r + 1))
    scur = pl.BlockSpec((CH, CH), lambda r, n: (n, r))
    snxt = pl.BlockSpec((CH, CH), lambda r, n: (jnp.minimum(n + 1, nb - 1), r))
    q2, k2, v2, do2 = (t.reshape(L, d * HW) for t in (q, k, v, do))
    m2, l2 = mixed.reshape(L, d * 2 * HW), lse.reshape(L, d * CH)
    outs = pl.pallas_call(
        body, name="attn_bwd_d%d" % d, grid=(d, nb),
        in_specs=[cur, nxt, cur, prev, cur, prev, ocur, onxt, cur, nxt, scur, snxt],
        out_specs=[cur] * 3, out_shape=[SDS((L, d * HW), f32)] * 3,
        compiler_params=_cp(("arbitrary", "arbitrary"), 32),
    )(q2, q2, k2, k2, v2, v2, m2, m2, do2, do2, l2, l2)
    return [t.reshape(T, HW) for t in outs]


def _sgu_fwd(ua, va, o, lng, lnb, ws, bs_t):
    tm = 512

    def body(ua_ref, va_ref, o_ref, lg_ref, lb_ref, ws_ref, bs_ref, mix_ref):
        for hd in range(4):
            sl = slice(hd * CH, (hd + 1) * CH)
            w = ws_ref[hd]
            for cc in range(tm // CH):
                rs = slice(cc * CH, (cc + 1) * CH)
                u = _gelu(ua_ref[rs, sl])
                v = _gelu(va_ref[rs, sl])
                vc = v - _rowmean(v)
                vn = vc * lax.rsqrt(_rowmean(vc * vc) + EPS) * lg_ref[:, sl] + lb_ref[:, sl]
                z = _nn(w, vn.astype(bf16)) + bs_ref[:, hd:hd + 1]
                mix_ref[rs, sl] = (u * z).astype(bf16)
        mix_ref[:, HW:] = o_ref[...]

    half = pl.BlockSpec((tm, HW), lambda i: (i, 0))
    vec = pl.BlockSpec((1, HW), lambda i: (0, 0))
    return pl.pallas_call(
        body, name="sgu_fwd", grid=(T // tm,),
        in_specs=[half, half, half, vec, vec, pl.BlockSpec((4, CH, CH), lambda i: (0, 0, 0)),
                  pl.BlockSpec((CH, 4), lambda i: (0, 0))],
        out_specs=pl.BlockSpec((tm, 2 * HW), lambda i: (i, 0)), out_shape=SDS((T, 2 * HW), bf16),
        compiler_params=_cp(("arbitrary",), 32),
    )(ua, va, o, lng, lnb, ws, bs_t)


def _sgu_bwd(ua, va, d_a, lng, lnb, ws, ws_t, bs_t):
    tm = 512

    def body(ua_ref, va_ref, d_ref, lg_ref, lb_ref, ws_ref, wst_ref, bs_ref, dsg_ref, dln_ref, dws_ref, db_ref):
        @pl.when(pl.program_id(0) == 0)
        def _():
            dln_ref[...] = jnp.zeros_like(dln_ref)
            dws_ref[...] = jnp.zeros_like(dws_ref)
            db_ref[...] = jnp.zeros_like(db_ref)

        for hd in range(4):
            sl = slice(hd * CH, (hd + 1) * CH)
            w, wt = ws_ref[hd], wst_ref[hd]
            lg = lg_ref[:, sl]
            for cc in range(tm // CH):
                rs = slice(cc * CH, (cc + 1) * CH)
                xa, xv, dd = ua_ref[rs, sl], va_ref[rs, sl], d_ref[rs, sl]
                u, v = _gelu(xa), _gelu(xv)
                vc = v - _rowmean(v)
                rstd = lax.rsqrt(_rowmean(vc * vc) + EPS)
                xh = vc * rstd
                vnb = (xh * lg + lb_ref[:, sl]).astype(bf16)
                z = _nn(w, vnb) + bs_ref[:, hd:hd + 1]
                dz = dd * u
                dzb = dz.astype(bf16)
                dsg_ref[rs, sl] = (dd * z * _gelu_grad(xa)).astype(bf16)
                dws_ref[hd] += _nt(dzb, vnb)
                db_ref[hd] += dz
                dvn = _nn(wt, dzb)
                dln_ref[0:1, sl] += _colsum(dvn * xh)
                dln_ref[1:2, sl] += _colsum(dvn)
                dxh = dvn * lg
                dv = rstd * (dxh - _rowmean(dxh) - xh * _rowmean(dxh * xh))
                dsg_ref[rs, HW + hd * CH:HW + (hd + 1) * CH] = (dv * _gelu_grad(xv)).astype(bf16)

    half = pl.BlockSpec((tm, HW), lambda i: (i, 0))
    vec = pl.BlockSpec((1, HW), lambda i: (0, 0))
    mat = pl.BlockSpec((4, CH, CH), lambda i: (0, 0, 0))
    return pl.pallas_call(
        body, name="sgu_bwd", grid=(T // tm,),
        in_specs=[half, half, half, vec, vec, mat, mat, pl.BlockSpec((CH, 4), lambda i: (0, 0))],
        out_specs=[pl.BlockSpec((tm, 2 * HW), lambda i: (i, 0)), pl.BlockSpec((8, HW), lambda i: (0, 0)), mat, mat],
        out_shape=[SDS((T, 2 * HW), bf16), SDS((8, HW), f32), SDS((4, CH, CH), f32), SDS((4, CH, CH), f32)],
        compiler_params=_cp(("arbitrary",), 32),
    )(ua, va, d_a, lng, lnb, ws, ws_t, bs_t)


def _outproj_fwd(mixed, w_out, h, par):
    tm = 512

    def body(mix_ref, w_ref, h_ref, par_ref, ho_ref, po_ref):
        p = _nn(mix_ref[...], w_ref[...])
        po_ref[...] = p.astype(bf16)
        ho_ref[...] = h_ref[...] + par_ref[2:3, :] * p

    row = pl.BlockSpec((tm, D), lambda i: (i, 0))
    return pl.pallas_call(
        body, name="outproj_fwd", grid=(T // tm,),
        in_specs=[row, pl.BlockSpec((D, D), lambda i: (0, 0)), row, pl.BlockSpec((8, D), lambda i: (0, 0))],
        out_specs=[row, row], out_shape=[SDS((T, D), f32), SDS((T, D), bf16)],
        compiler_params=_cp(("arbitrary",), 32),
    )(mixed, w_out, h, par)


def _outproj_bwd(dh, po, w_out, par):
    tm = 512

    def body(dh_ref, po_ref, w_ref, par_ref, do_ref, da_ref, db_ref, dpar_ref):
        @pl.when(pl.program_id(0) == 0)
        def _():
            dpar_ref[...] = jnp.zeros_like(dpar_ref)

        dh_v = dh_ref[...]
        dob = (par_ref[2:3, :] * dh_v).astype(bf16)
        do_ref[...] = dob
        dpar_ref[2:3, :] += _colsum(dh_v * po_ref[...].astype(f32))
        dm = _nt(dob, w_ref[...])
        da_ref[...] = dm[:, :HW]
        db_ref[...] = dm[:, HW:].astype(bf16)

    row = pl.BlockSpec((tm, D), lambda i: (i, 0))
    half = pl.BlockSpec((tm, HW), lambda i: (i, 0))
    one = pl.BlockSpec((8, D), lambda i: (0, 0))
    return pl.pallas_call(
        body, name="outproj_bwd", grid=(T // tm,),
        in_specs=[row, row, pl.BlockSpec((D, D), lambda i: (0, 0)), one],
        out_specs=[row, half, half, one],
        out_shape=[SDS((T, D), bf16), SDS((T, HW), f32), SDS((T, HW), bf16), SDS((8, D), f32)],
        compiler_params=_cp(("arbitrary",), 32),
    )(dh, po, w_out, par)


def _inproj_bwd_tok(dh, h, par, w_in_t, dsg, dqs, dks, dvs, tabs):
    tm = 256

    def body(dh_ref, h_ref, par_ref, w_ref, dsg_ref, dq0, dq1, dq2, dk0, dk1, dk2, dv0, dv1, dv2,
             cos_ref, slo_ref, shi_ref, dhin_ref, dp_ref, y_ref, dpar_ref):
        @pl.when(pl.program_id(0) == 0)
        def _():
            dpar_ref[...] = jnp.zeros_like(dpar_ref)

        cos, slo, shi = cos_ref[...], slo_ref[...], shi_ref[...]
        dq = (dq0[...] + dq1[...] + dq2[...]) * 0.125
        dk = dk0[...] + dk1[...] + dk2[...]
        dp_ref[:, :2 * HW] = dsg_ref[...]
        dp_ref[:, 2 * HW:3 * HW] = _rope_t(dq, cos, slo, shi).astype(bf16)
        dp_ref[:, 3 * HW:4 * HW] = _rope_t(dk, cos, slo, shi).astype(bf16)
        dp_ref[:, 4 * HW:] = (dv0[...] + dv1[...] + dv2[...]).astype(bf16)
        dy = _nn(dp_ref[...], w_ref[...])
        par_v, h_v = par_ref[...], h_ref[...]
        y_ref[...] = _norm_mod(h_v, par_v)[0].astype(bf16)
        dx, dsh, dsc, dng = _norm_mod_bwd(dy, h_v, par_v)
        dhin_ref[...] = dh_ref[...] + dx
        dpar_ref[0:1, :] += dsh
        dpar_ref[1:2, :] += dsc
        dpar_ref[3:4, :] += dng

    row = pl.BlockSpec((tm, D), lambda i: (i, 0))
    half = pl.BlockSpec((tm, HW), lambda i: (i, 0))
    one = pl.BlockSpec((8, D), lambda i: (0, 0))
    return pl.pallas_call(
        body, name="inproj_bwd_tok", grid=(T // tm,),
        in_specs=[row, row, one, pl.BlockSpec((INW, D), lambda i: (0, 0)), row] + [half] * 12,
        out_specs=[row, pl.BlockSpec((tm, INW), lambda i: (i, 0)), row, one],
        out_shape=[SDS((T, D), f32), SDS((T, INW), bf16), SDS((T, D), bf16), SDS((8, D), f32)],
        compiler_params=_cp(("arbitrary",), 48),
    )(dh, h, par, w_in_t, dsg, *dqs, *dks, *dvs, *tabs)


def _loss_head(h, par, target):
    tm = 512

    def body(h_ref, par_ref, t_ref, dh_ref, acc_ref):
        @pl.when(pl.program_id(0) == 0)
        def _():
            acc_ref[...] = jnp.zeros_like(acc_ref)

        x, g = h_ref[...], par_ref[3:4, :]
        rstd = lax.rsqrt(_rowmean(x * x) + EPS)
        xhat = x * rstd
        err = xhat * g - t_ref[...]
        acc_ref[0:1, :] += _colsum(err * err)
        dy = err * (1.0 / D)
        acc_ref[3:4, :] += _colsum(dy * xhat)
        dxhat = dy * g
        dh_ref[...] = rstd * (dxhat - xhat * _rowmean(dxhat * xhat))

    row = pl.BlockSpec((tm, D), lambda i: (i, 0))
    one = pl.BlockSpec((8, D), lambda i: (0, 0))
    return pl.pallas_call(
        body, name="loss_head", grid=(T // tm,), in_specs=[row, one, row], out_specs=[row, one],
        out_shape=[SDS((T, D), f32), SDS((8, D), f32)], compiler_params=_cp(("arbitrary",), 32),
    )(h, par, target)


ELEMENTWISE_VMEM_BUDGET = 20 << 20


def _block_rows(rows, bytes_per_row):
    cap = ELEMENTWISE_VMEM_BUDGET // bytes_per_row
    if rows <= cap:
        return rows
    return next(b for b in range(cap - cap % 16, 0, -16) if rows % b == 0)


def _sum_slots(land, name):
    _, R, C = land.shape
    br = _block_rows(R, 2 * NDEV * C * land.dtype.itemsize + 2 * C * 4)

    def body(l_ref, o_ref):
        acc = l_ref[0].astype(f32)
        for s in range(1, NDEV):
            acc = acc + l_ref[s].astype(f32)
        o_ref[...] = acc

    return pl.pallas_call(
        body, name=name, grid=(R // br,), in_specs=[pl.BlockSpec((NDEV, br, C), lambda i: (0, i, 0))],
        out_specs=pl.BlockSpec((br, C), lambda i: (i, 0)), out_shape=SDS((R, C), f32),
        compiler_params=_cp(("arbitrary",), 32),
    )(land)


def _sum_partials(land, own, me, name):
    r = land.shape[-2]

    def body(me_ref, l_ref, own_ref, o_ref):
        mine = own_ref[...].astype(f32)
        acc = jnp.where(me_ref[0] == 0, mine, l_ref[0].astype(f32))
        for s in range(1, NDEV):
            acc = acc + jnp.where(me_ref[0] == s, mine, l_ref[s].astype(f32))
        o_ref[...] = acc

    if own.ndim == 3:
        kk = own.shape[0]
        specs = dict(grid=(kk,),
                     in_specs=[pl.BlockSpec((NDEV, None, r, D), lambda i, me_ref: (0, i, 0, 0)),
                               pl.BlockSpec((None, r, D), lambda i, me_ref: (i, me_ref[0], 0))],
                     out_specs=pl.BlockSpec((None, r, D), lambda i, me_ref: (i, 0, 0)))
        out_shape = SDS((kk, r, D), f32)
    else:
        specs = dict(grid=(1,),
                     in_specs=[pl.BlockSpec((NDEV, r, D), lambda i, me_ref: (0, 0, 0)),
                               pl.BlockSpec((r, D), lambda i, me_ref: (me_ref[0], 0))],
                     out_specs=pl.BlockSpec((r, D), lambda i, me_ref: (0, 0)))
        out_shape = SDS((r, D), f32)
    return pl.pallas_call(
        body, name=name, out_shape=out_shape,
        grid_spec=pltpu.PrefetchScalarGridSpec(num_scalar_prefetch=1, **specs),
        compiler_params=_cp(("arbitrary",), 32),
    )(me.reshape(1), land, own)


def _adamw(w, g, m, v, name):
    R, C = w.shape
    br = _block_rows(R, 2 * 7 * C * 4)

    def body(w_ref, g_ref, m_ref, v_ref, d_ref, mo_ref, vo_ref):
        gv = g_ref[...]
        m2 = B1 * m_ref[...] + (1.0 - B1) * gv
        v2 = B2 * v_ref[...] + (1.0 - B2) * (gv * gv)
        mo_ref[...] = m2
        vo_ref[...] = v2
        m_hat = m2 / (1.0 - B1 ** STEP)
        v_hat = v2 / (1.0 - B2 ** STEP)
        d_ref[...] = -LR * (m_hat / (jnp.sqrt(v_hat) + AEPS) + WD * w_ref[...])

    blk = pl.BlockSpec((br, C), lambda i: (i, 0))
    return pl.pallas_call(
        body, name=name, grid=(R // br,), in_specs=[blk] * 4, out_specs=[blk] * 3,
        out_shape=[SDS((R, C), f32)] * 3, compiler_params=_cp(("arbitrary",), 32),
    )(w, g, m, v)


def _adamw_nd(w, g, m, v, name):
    shp = w.shape
    r2 = (-1, shp[-1]) if w.ndim > 1 else (8, shp[0] // 8)
    outs = _adamw(w.reshape(r2), g.reshape(r2), m.reshape(r2), v.reshape(r2), name)
    return [o.reshape(shp) for o in outs]


def _par_rows(mod_l, s, gain):
    rows = jnp.pad(mod_l.reshape(9, D)[3 * s:3 * s + 3], ((0, 5), (0, 0)))
    return rows + jnp.pad(gain[None, :], ((3, 4), (0, 0)))


def _pad_rows(a):
    a = a.reshape(-1, CH)
    return jnp.pad(a, ((0, (-a.shape[0]) % 8), (0, 0)))


def _prepare(c, ada_w, ada_b, norm_g):
    me = 4 * lax.axis_index("x") + 2 * lax.axis_index("y") + lax.axis_index("c")

    pay = jnp.pad(c, ((0, 7), (0, 0)))
    pay = jnp.concatenate([pay, jnp.pad(norm_g.reshape(6, OUTS), ((0, 2), (0, D - OUTS)))], axis=0)
    got = _all_gather_small(pay, "gather_c")
    c_all = got[:, 0, :]
    gains = got[:, 8:14, :OUTS].transpose(1, 0, 2).reshape(DEPTH, 3, D)

    ada_b_loc = lax.dynamic_slice(ada_b, (0, me * ADAS), (DEPTH, ADAS)).reshape(DEPTH, 1, ADAS)
    mod_cols = _mod_fwd(c_all, ada_w, ada_b_loc)
    got = _all_gather_small(mod_cols.reshape(DEPTH * NDEV, ADAS), "gather_mod").reshape(NDEV, DEPTH, NDEV, ADAS)
    mod = lax.dynamic_index_in_dim(got, me, axis=2, keepdims=False).transpose(1, 0, 2).reshape(DEPTH, 9 * D)
    pars = [[_par_rows(mod[l], s, gains[l, s]) for s in range(3)] for l in range(DEPTH)]
    return me, c_all, pars


def _fwd_bwd(x2, target, pars, get_w, put_g, small_ready, sgu_ln_g, sgu_ln_b, sgu_w, sgu_b, final_g):
    tabs = _rope_tables()
    tril = jnp.tril(jnp.ones((CH, CH), dtype=bool))
    ws_m = jnp.where(tril[None, None], sgu_w, 0.0).astype(bf16)
    ws_mt = jnp.swapaxes(ws_m, -1, -2)
    behind = lambda rows, token: rows if token is None else rows + token[0, 0]

    h = x2
    saved = []
    for l in range(DEPTH):
        lng, lnb = sgu_ln_g[l].reshape(1, HW), sgu_ln_b[l].reshape(1, HW)
        bs_t = sgu_b[l].T
        h0 = h
        w, token = get_w(4 * l, h0)
        h1, fo1, g1, u1, a1 = _ffn_fwd(h0, behind(pars[l][0], token), w)
        w, token = get_w(4 * l + 2, h1)
        ua, va, q, k, v = _inproj_fwd(h1, behind(pars[l][1], token), w, tabs)
        branches, token = [], None
        for d in PATTERN_DILATIONS:
            branches.append(_attn_fwd(q, k, v, d, token))
            if len(branches) == 2 and l + 1 < DEPTH:
                token = get_w(4 * l + 1, branches[-1][1])[1]
        o, lse = _attn_combine([b[0] for b in branches], [b[1] for b in branches])
        mixed = _sgu_fwd(ua, va, o, lng, lnb, ws_m[l], bs_t)
        w, token = get_w(4 * l + 3, mixed)
        h2, po = _outproj_fwd(mixed, w, h1, behind(pars[l][1], token))
        w, token = get_w(4 * l + 1, h2)
        h3, fo2, g2, u2, a2 = _ffn_fwd(h2, behind(pars[l][2], token), w)
        saved.append((h0, h1, h2, fo1, g1, u1, a1, ua, va, q, k, v, mixed, lse, po, fo2, g2, u2, a2))
        h = h3

    par_f = jnp.pad(final_g[None, :], ((3, 4), (0, 0)))
    dh, head = _loss_head(h, par_f, target)

    dmods, dgains, dsgu = [None] * DEPTH, [None] * DEPTH, [None] * DEPTH
    token = None
    for l in reversed(range(DEPTH)):
        w_f1, w_f2, w_i, w_o = (get_w(4 * l + j, None)[0] for j in (0, 1, 2, 3))
        h0, h1, h2, fo1, g1, u1, a1, ua, va, q, k, v, mixed, lse, po, fo2, g2, u2, a2 = saved[l]
        lng, lnb = sgu_ln_g[l].reshape(1, HW), sgu_ln_b[l].reshape(1, HW)
        bs_t = sgu_b[l].T

        dh, dg_s, du_s, y, dfb, dpar3 = _ffn_bwd_tok(dh, h2, behind(pars[l][2], token), fo2, g2, u2, w_f2)
        token = put_g(4 * l + 1, _ffn_bwd_w(a2, dg_s, du_s, y, dfb))

        dob, d_a, d_b, dpar2g = _outproj_bwd(dh, po, w_o, behind(pars[l][1], token))
        token = put_g(4 * l + 3, _tn_matmul(mixed, dob, 512, "w_out_grad"))
        parts = [_attn_bwd(q, k, v, mixed, d_b, lse, d) for d in PATTERN_DILATIONS]
        dsg, dln, dws, dbl = _sgu_bwd(ua, va, d_a, lng, lnb, ws_m[l], ws_mt[l], bs_t)
        dh, dp, y2, dpar2 = _inproj_bwd_tok(dh, h1, behind(pars[l][1], token), w_i, dsg, [p[0] for p in parts],
                                            [p[1] for p in parts], [p[2] for p in parts], tabs)
        token = put_g(4 * l + 2, _tn_matmul(dp, y2, 640, "w_in_grad"))

        dh, dg_s, du_s, y, dfb, dpar1 = _ffn_bwd_tok(dh, h0, behind(pars[l][0], token), fo1, g1, u1, w_f1)
        dmods[l] = jnp.concatenate([dpar1[0:3], dpar2[0:2], dpar2g[2:3], dpar3[0:3]], axis=0).reshape(9 * D)
        dgains[l] = jnp.stack([dpar1[3], dpar2[3], dpar3[3]])
        dsgu[l] = (dln[0], dln[1], jnp.where(tril[None], dws, 0.0), jnp.sum(dbl, axis=-1))
        token = small_ready(head, dmods, dgains, dsgu) if l == 0 else None
        token = put_g(4 * l, _ffn_bwd_w(a1, dg_s, du_s, y, dfb, token))
    return dh, token


def kernel(x, c, ada_w, ada_b, norm_g, ffn1_wg, ffn1_wu, ffn1_wd, ffn2_wg, ffn2_wu, ffn2_wd, w_in, sgu_ln_g, sgu_ln_b, sgu_w, sgu_b, w_out, final_g, loss_target, m_ada_w, m_ada_b, m_norm_g, m_ffn1_wg, m_ffn1_wu, m_ffn1_wd, m_ffn2_wg, m_ffn2_wu, m_ffn2_wd, m_w_in, m_sgu_ln_g, m_sgu_ln_b, m_sgu_w, m_sgu_b, m_w_out, m_final_g, v_ada_w, v_ada_b, v_norm_g, v_ffn1_wg, v_ffn1_wu, v_ffn1_wd, v_ffn2_wg, v_ffn2_wu, v_ffn2_wd, v_w_in, v_sgu_ln_g, v_sgu_ln_b, v_sgu_w, v_sgu_b, v_w_out, v_final_g):
    me, c_all, pars = _prepare(c, ada_w, ada_b, norm_g)

    tr = lambda w: jnp.swapaxes(w, -1, -2).astype(bf16)
    locs = []
    for l in range(DEPTH):
        locs.append(jnp.stack([tr(ffn1_wg[l]), tr(ffn1_wu[l]), ffn1_wd[l].astype(bf16)]))
        locs.append(jnp.stack([tr(ffn2_wg[l]), tr(ffn2_wu[l]), ffn2_wd[l].astype(bf16)]))
        locs.append(tr(w_in[l]))
        locs.append(w_out[l].astype(bf16))
    locs, pars = lax.optimization_barrier((locs, pars))

    first, locs = lax.optimization_barrier((_all_gather_rows([locs[0]], "gather_first")[0], locs))
    placed = [_place_own(a, me) for a in locs[1:]]
    ready = {0: first}
    groups = ([2, 3], [1], [4], [6, 7], [5])
    flying = {}

    def start_group(gi, behind):
        pieces = groups[gi]
        arrays, _ = lax.optimization_barrier(([placed[p - 1] for p in pieces], behind))
        started = _exchange_start(False, arrays, "gather_start_%d" % gi)
        flying.update({p: (gi, started[:-1]) for p in pieces})
        return started[-1]

    first_token = start_group(0, first)

    def get_w(piece, after):
        token = first_token if piece == 0 else None
        if piece not in ready:
            gi, started = flying[piece]
            got_w = _exchange_wait(False, started, after, "gather_wait_%d" % gi)
            ready.update(zip(groups[gi], got_w))
            if gi + 1 < len(groups):
                token = start_group(gi + 1, got_w)
        return ready[piece], token

    sent = {}

    def put_g(piece, grad):
        land = lax.empty((NDEV,) + grad.shape[:-2] + (grad.shape[-2] // NDEV, D), bf16)
        started = _exchange_start(True, [grad, land], "scatter_start_%d" % piece)
        sent[piece] = started[:-1]
        return started[-1]

    small_sent = []

    def small_ready(head, dmods, dgains, dsgu):
        loss_part = 0.5 * jnp.sum(head[0]) / D
        small = jnp.concatenate([
            _pad_rows(jnp.stack(dmods)), _pad_rows(jnp.stack(dgains)),
            _pad_rows(jnp.stack([s[0] for s in dsgu])), _pad_rows(jnp.stack([s[1] for s in dsgu])),
            _pad_rows(jnp.stack([s[3] for s in dsgu])), _pad_rows(jnp.stack([s[2] for s in dsgu])),
            _pad_rows(head[3]), _pad_rows(jnp.pad(loss_part[None], (0, CH - 1)))], axis=0)
        started = _exchange_start(False, [_place_own(small, me, "place_small")], "small_start")
        small_sent.append(started[:-1])
        return started[-1]

    dh, last = _fwd_bwd(x[0], loss_target[0], pars, get_w, put_g, small_ready, sgu_ln_g, sgu_ln_b, sgu_w, sgu_b, final_g)
    grad_x = dh[None]

    got = _exchange_wait(False, small_sent[0], last, "small_wait")[0].reshape(NDEV, -1, CH)
    tot = _sum_slots(got, "sum_small")
    n_mod, n_gain, n_sw = DEPTH * 9 * D // CH, DEPTH * 3 * D // CH, DEPTH * 4 * CH
    offs = [0, n_mod, n_mod + n_gain, n_mod + n_gain + 8, n_mod + n_gain + 16, n_mod + n_gain + 24]
    g_ada_b = tot[offs[0]:offs[1]].reshape(DEPTH, 9 * D)
    g_gain_full = tot[offs[1]:offs[2]].reshape(DEPTH, 3, D)
    g_ln_g = tot[offs[2]:offs[3]].reshape(DEPTH, 4, CH)
    g_ln_b = tot[offs[3]:offs[4]].reshape(DEPTH, 4, CH)
    g_sb = tot[offs[4]:offs[5]].reshape(DEPTH, 4, CH)
    g_sw = tot[offs[5]:offs[5] + n_sw].reshape(DEPTH, 4, CH, CH)
    g_final = tot[offs[5] + n_sw:offs[5] + n_sw + 8].reshape(D)
    loss = tot[offs[5] + n_sw + 8, 0]
    g_norm = lax.dynamic_slice(g_gain_full, (0, 0, me * OUTS), (DEPTH, 3, OUTS))

    dmod_all = got[:, offs[0]:offs[1]].reshape(NDEV, DEPTH, 9 * D)
    dmod_cols = lax.dynamic_slice(dmod_all, (0, 0, me * ADAS), (NDEV, DEPTH, ADAS)).transpose(1, 0, 2)
    g_ada_w = _ada_grad((c_all * jax.nn.sigmoid(c_all)).T, dmod_cols)

    sums, after = {}, tot

    def collect(piece, after):
        own, land = _exchange_wait(True, sent[piece], after, "scatter_wait_%d" % piece)
        sums[piece] = _sum_partials(land, own, me, "sum_grads")
        return sums[piece]

    for piece in (5, 7, 6, 4, 1, 3, 2):
        after = collect(piece, after)
    back = lambda t: jnp.swapaxes(t, -1, -2)
    f2 = jnp.stack([sums[4 * l + 1] for l in range(DEPTH)])
    g_w_in = back(jnp.stack([sums[4 * l + 2] for l in range(DEPTH)]))
    g_w_out = jnp.stack([sums[4 * l + 3] for l in range(DEPTH)])

    ws = [ada_w, ada_b, norm_g, ffn1_wg, ffn1_wu, ffn1_wd, ffn2_wg, ffn2_wu, ffn2_wd, w_in, sgu_ln_g, sgu_ln_b, sgu_w,
          sgu_b, w_out, final_g]
    ms = [m_ada_w, m_ada_b, m_norm_g, m_ffn1_wg, m_ffn1_wu, m_ffn1_wd, m_ffn2_wg, m_ffn2_wu, m_ffn2_wd, m_w_in,
          m_sgu_ln_g, m_sgu_ln_b, m_sgu_w, m_sgu_b, m_w_out, m_final_g]
    vs = [v_ada_w, v_ada_b, v_norm_g, v_ffn1_wg, v_ffn1_wu, v_ffn1_wd, v_ffn2_wg, v_ffn2_wu, v_ffn2_wd, v_w_in,
          v_sgu_ln_g, v_sgu_ln_b, v_sgu_w, v_sgu_b, v_w_out, v_final_g]
    gw = [g_ada_w, g_ada_b, g_norm, None, None, None, back(f2[:, 0]), back(f2[:, 1]), f2[:, 2],
          g_w_in, g_ln_g, g_ln_b, g_sw, g_sb, g_w_out, g_final]
    upd = [None] * len(ws)
    for i in (1, 2, 10, 11, 12, 13, 15, 6, 7, 8, 9, 14, 0):
        upd[i] = _adamw_nd(ws[i], gw[i], ms[i], vs[i], "adamw")
    collect(0, upd[0][0])
    f1 = jnp.stack([sums[4 * l] for l in range(DEPTH)])
    gw[3:6] = [back(f1[:, 0]), back(f1[:, 1]), f1[:, 2]]
    for i in (3, 4, 5):
        upd[i] = _adamw_nd(ws[i], gw[i], ms[i], vs[i], "adamw")
    return (loss, grad_x, *gw, *[u[0] for u in upd], *[u[1] for u in upd], *[u[2] for u in upd])
```

```python
import functools
import math

import jax
import jax.numpy as jnp
from jax import lax
from jax.experimental import pallas as pl
from jax.experimental.pallas import tpu as pltpu

f32, bf16 = jnp.float32, jnp.bfloat16
SDS = jax.ShapeDtypeStruct

T, D, F = 4096, 1024, 2816
NDEV, DEPTH = 8, 2
HW = 512
INW = 5 * HW
FS, INS, OUTS, ADAS = F // NDEV, INW // NDEV, D // NDEV, 9 * D // NDEV
CH = 128
PATTERN_DILATIONS = (1, 4, 16)
ROPE_THETA = 10000.0
EPS = 1e-6
LR, B1, B2, AEPS, WD, STEP = 0.001, 0.9, 0.999, 1e-08, 0.01, 10
MESH = pl.DeviceIdType.MESH


def _cp(sems, vmem_mb):
    return pltpu.CompilerParams(dimension_semantics=sems, vmem_limit_bytes=vmem_mb << 20)


def _nn(a, b):
    return lax.dot_general(a, b, (((1,), (0,)), ((), ())), preferred_element_type=f32)


def _nt(a, b):
    return lax.dot_general(a, b, (((1,), (1,)), ((), ())), preferred_element_type=f32)


def _tn(a, b):
    return lax.dot_general(a, b, (((0,), (0,)), ((), ())), preferred_element_type=f32)


def _colsum(a):
    return jnp.sum(a, axis=0, keepdims=True)


def _rowmean(a):
    return jnp.mean(a, axis=-1, keepdims=True)


def _norm_mod(x, par):
    rstd = lax.rsqrt(_rowmean(x * x) + EPS)
    xhat = x * rstd
    n = xhat * par[3:4, :]
    y = n * (1.0 + par[1:2, :]) + par[0:1, :]
    return y, n, xhat, rstd


def _norm_mod_bwd(dy, x, par):
    _, n, xhat, rstd = _norm_mod(x, par)
    dn = dy * (1.0 + par[1:2, :])
    dxhat = dn * par[3:4, :]
    dx = rstd * (dxhat - xhat * _rowmean(dxhat * xhat))
    return dx, _colsum(dy), _colsum(dy * n), _colsum(dn * xhat)


_GK = math.sqrt(2.0 / math.pi)


def _gelu(x):
    return 0.5 * x * (1.0 + jnp.tanh(_GK * (x + 0.044715 * x * x * x)))


def _gelu_grad(x):
    t = jnp.tanh(_GK * (x + 0.044715 * x * x * x))
    return 0.5 * (1.0 + t) + 0.5 * x * (1.0 - t * t) * (_GK * (1.0 + 3.0 * 0.044715 * x * x))


def _rows(ref, idx, r):
    if len(ref.shape) == 3:
        return ref.at[:, pl.ds(idx * r, r), :]
    return ref.at[pl.ds(idx * r, r), :]


def _flip(v, bit):
    return 1 - v if bit else v


def _all_gather_small(x, name):
    R, C = x.shape

    def body(x_ref, out_ref, send_sems, recv_sems):
        mx, my, mc = lax.axis_index("x"), lax.axis_index("y"), lax.axis_index("c")
        me = 4 * mx + 2 * my + mc
        out_ref[me] = x_ref[...]
        sent = []
        for k in range(1, NDEV):
            peer = (_flip(mx, k & 4), _flip(my, k & 2), _flip(mc, k & 1))
            cp = pltpu.make_async_remote_copy(
                src_ref=x_ref, dst_ref=out_ref.at[me], send_sem=send_sems.at[k - 1],
                recv_sem=recv_sems.at[k - 1], device_id=peer, device_id_type=MESH)
            cp.start()
            sent.append(cp)
        for k in range(1, NDEV):
            peer = (_flip(mx, k & 4), _flip(my, k & 2), _flip(mc, k & 1))
            pidx = 4 * peer[0] + 2 * peer[1] + peer[2]
            pltpu.make_async_remote_copy(
                src_ref=x_ref, dst_ref=out_ref.at[pidx], send_sem=send_sems.at[k - 1],
                recv_sem=recv_sems.at[k - 1], device_id=peer, device_id_type=MESH).wait_recv()
        for cp in sent:
            cp.wait_send()

    vm = pl.BlockSpec(memory_space=pltpu.VMEM)
    return pl.pallas_call(
        body, name=name, out_shape=SDS((NDEV, R, C), f32), in_specs=[vm], out_specs=vm,
        scratch_shapes=[pltpu.SemaphoreType.DMA((NDEV - 1,)), pltpu.SemaphoreType.DMA((NDEV - 1,))],
        compiler_params=pltpu.CompilerParams(vmem_limit_bytes=32 << 20),
    )(x)


def _all_gather_rows(locs, name):
    n = len(locs)
    rs = [a.shape[-2] for a in locs]

    def body(*refs):
        src, out = refs[:n], refs[n:2 * n]
        send_sems, recv_sems, loc_sems = refs[2 * n:]
        mx, my, mc = lax.axis_index("x"), lax.axis_index("y"), lax.axis_index("c")
        me, sib = (mx, my, mc), (mx, my, 1 - mc)
        chips = [(1 - mx, my), (mx, 1 - my), (1 - mx, 1 - my)]

        def blk(a, p):
            return _rows(out[a], 4 * p[0] + 2 * p[1] + p[2], rs[a])

        def copy(k, a, block, to, from_src=False):
            return pltpu.make_async_remote_copy(
                src_ref=src[a] if from_src else blk(a, block), dst_ref=blk(a, block),
                send_sem=send_sems.at[k * n + a], recv_sem=recv_sems.at[k * n + a],
                device_id=to, device_id_type=MESH)

        mine = [pltpu.make_async_copy(src[a], blk(a, me), loc_sems.at[a]) for a in range(n)]
        for m in mine:
            m.start()
        first = []
        for j, chip in enumerate(chips):
            first += [copy(1 + j, a, me, (*chip, mc), True) for a in range(n)]
        first += [copy(0, a, me, sib, True) for a in range(n)]
        for cp in first:
            cp.start()
        passed = []
        for j, chip in enumerate(chips):
            for a in range(n):
                copy(1 + j, a, (*chip, mc), me).wait_recv()
            fwd = [copy(4 + j, a, (*chip, mc), sib) for a in range(n)]
            for cp in fwd:
                cp.start()
            passed += fwd
        for a in range(n):
            copy(0, a, sib, me).wait_recv()
        for j, chip in enumerate(chips):
            for a in range(n):
                copy(4 + j, a, (*chip, 1 - mc), me).wait_recv()
        for cp in first + passed:
            cp.wait_send()
        for m in mine:
            m.wait()

    hbm = pl.BlockSpec(memory_space=pl.ANY)
    out_shape = [SDS(a.shape[:-2] + (NDEV * a.shape[-2], a.shape[-1]), a.dtype) for a in locs]
    return pl.pallas_call(
        body, name=name, out_shape=out_shape, in_specs=[hbm] * n, out_specs=[hbm] * n,
        scratch_shapes=[pltpu.SemaphoreType.DMA((7 * n,)), pltpu.SemaphoreType.DMA((7 * n,)),
                        pltpu.SemaphoreType.DMA((n,))],
    )(*locs)


HBM_SPEC = pl.BlockSpec(memory_space=pltpu.HBM)
SEM_SPEC = pl.BlockSpec(memory_space=pltpu.SEMAPHORE)
DATAFLOW_EFFECT = pltpu.SideEffectType.DATAFLOW_SIDE_EFFECTING


def _place_own(loc, me, name="place_own"):
    r, cols = loc.shape[-2:]
    loc3 = loc.reshape(-1, r, cols)
    kk = loc3.shape[0]

    def body(me_ref, src_ref, full_ref, out_ref):
        out_ref[...] = src_ref[...]

    out = pl.pallas_call(
        body, name=name, out_shape=SDS((kk, NDEV * r, cols), loc.dtype),
        grid_spec=pltpu.PrefetchScalarGridSpec(
            num_scalar_prefetch=1, grid=(kk,),
            in_specs=[pl.BlockSpec((None, r, cols), lambda i, me_ref: (i, 0, 0)), pl.BlockSpec(memory_space=pl.ANY)],
            out_specs=pl.BlockSpec((None, r, cols), lambda i, me_ref: (i, me_ref[0], 0))),
        input_output_aliases={2: 0}, compiler_params=_cp(("arbitrary",), 32),
    )(me.reshape(1), loc3, lax.empty((kk, NDEV * r, cols), loc.dtype))
    return out.reshape(loc.shape[:-2] + (NDEV * r, cols))


def _exchange_copies(scatter, bufs, n, send_sems, recv_sems):
    mx, my, mc = lax.axis_index("x"), lax.axis_index("y"), lax.axis_index("c")
    me = 4 * mx + 2 * my + mc
    out = []
    for k in range(1, NDEV):
        peer = (_flip(mx, k & 4), _flip(my, k & 2), _flip(mc, k & 1))
        pidx = 4 * peer[0] + 2 * peer[1] + peer[2]
        for a in range(n):
            r = bufs[a].shape[-2] // NDEV
            if scatter:
                src, dst, arrive = _rows(bufs[a], pidx, r), bufs[n + a].at[me], bufs[n + a].at[pidx]
            else:
                src, dst, arrive = _rows(bufs[a], me, r), _rows(bufs[a], me, r), _rows(bufs[a], pidx, r)
            sems = dict(send_sem=send_sems.at[(k - 1) * n + a], recv_sem=recv_sems.at[(k - 1) * n + a],
                        device_id=peer, device_id_type=MESH)
            out.append((pltpu.make_async_remote_copy(src_ref=src, dst_ref=dst, **sems),
                        pltpu.make_async_remote_copy(src_ref=src, dst_ref=arrive, **sems)))
    return out


def _exchange_start(scatter, arrays, name):
    m = len(arrays)
    n = m // 2 if scatter else m

    def body(*refs):
        send_sems, recv_sems, token = refs[m], refs[m + 1], refs[-1]
        for go, _ in _exchange_copies(scatter, refs[:m], n, send_sems, recv_sems):
            go.start()
        token[...] = jnp.zeros_like(token)

    sems = pltpu.SemaphoreType.DMA((7 * n,))
    return pl.pallas_call(
        body, name=name, in_specs=[HBM_SPEC] * m,
        out_shape=(sems, sems, *[pltpu.HBM(a.shape, a.dtype) for a in arrays], SDS((8, CH), f32)),
        out_specs=(SEM_SPEC, SEM_SPEC, *[HBM_SPEC] * m, pl.BlockSpec(memory_space=pltpu.VMEM)),
        input_output_aliases={a: 2 + a for a in range(m)},
        compiler_params=pltpu.CompilerParams(has_side_effects=DATAFLOW_EFFECT),
    )(*[pltpu.with_memory_space_constraint(a, pltpu.HBM) for a in arrays])


def _exchange_wait(scatter, started, after, name):
    send_sems, recv_sems, *arrays = started
    m = len(arrays)
    n = m // 2 if scatter else m

    def body(*refs):
        for go, arrive in _exchange_copies(scatter, refs[:m], n, refs[m], refs[m + 1]):
            go.wait_send()
            arrive.wait_recv()

    return pl.pallas_call(
        body, name=name, in_specs=[HBM_SPEC] * m + [SEM_SPEC, SEM_SPEC, pl.BlockSpec(memory_space=pl.ANY)],
        out_shape=[pltpu.HBM(a.shape, a.dtype) for a in arrays], out_specs=[HBM_SPEC] * m,
        input_output_aliases={a: a for a in range(m)},
        compiler_params=pltpu.CompilerParams(has_side_effects=DATAFLOW_EFFECT),
    )(*arrays, send_sems, recv_sems, after)


def _mod_fwd(c_all, ada_w, ada_b_loc):
    def body(c_ref, w_ref, b_ref, o_ref):
        ca = c_ref[...]
        ca = ca * jax.nn.sigmoid(ca)
        o_ref[...] = jnp.dot(ca, w_ref[...], precision=lax.Precision.HIGHEST,
                             preferred_element_type=f32) + b_ref[...]

    return pl.pallas_call(
        body, name="mod_fwd", grid=(DEPTH,), out_shape=SDS((DEPTH, NDEV, ADAS), f32),
        in_specs=[pl.BlockSpec((NDEV, D), lambda l: (0, 0)),
                  pl.BlockSpec((None, D, ADAS), lambda l: (l, 0, 0)),
                  pl.BlockSpec((None, 1, ADAS), lambda l: (l, 0, 0))],
        out_specs=pl.BlockSpec((None, NDEV, ADAS), lambda l: (l, 0, 0)),
        compiler_params=_cp(("arbitrary",), 32),
    )(c_all, ada_w, ada_b_loc)


def _ada_grad(cact_t, dmod_cols):
    def body(c_ref, d_ref, o_ref):
        acc = c_ref[:, 0:1] * d_ref[0:1, :]
        for b in range(1, NDEV):
            acc = acc + c_ref[:, b:b + 1] * d_ref[b:b + 1, :]
        o_ref[...] = acc

    tr = 256
    return pl.pallas_call(
        body, name="ada_grad", grid=(DEPTH, D // tr), out_shape=SDS((DEPTH, D, ADAS), f32),
        in_specs=[pl.BlockSpec((tr, NDEV), lambda l, i: (i, 0)),
                  pl.BlockSpec((None, NDEV, ADAS), lambda l, i: (l, 0, 0))],
        out_specs=pl.BlockSpec((None, tr, ADAS), lambda l, i: (l, i, 0)),
        compiler_params=_cp(("arbitrary", "arbitrary"), 32),
    )(cact_t, dmod_cols)


def _ffn_fwd(h, par, w3):
    tm, tf = 1024, 256
    nj = F // tf

    def body(h_ref, par_ref, wg_ref, wu_ref, wd_ref, ho_ref, fo_ref, g_ref, u_ref, a_ref, y_scr, acc):
        j = pl.program_id(1)

        @pl.when(j == 0)
        def _():
            y_scr[...] = _norm_mod(h_ref[...], par_ref[...])[0].astype(bf16)
            acc[...] = jnp.zeros_like(acc)

        y = y_scr[...]
        g = _nt(y, wg_ref[...])
        u = _nt(y, wu_ref[...])
        a = ((g * jax.nn.sigmoid(g)) * u).astype(bf16)
        g_ref[...] = g.astype(bf16)
        u_ref[...] = u.astype(bf16)
        a_ref[...] = a
        acc[...] += _nn(a, wd_ref[...])

        @pl.when(j == nj - 1)
        def _():
            fo_ref[...] = acc[...].astype(bf16)
            ho_ref[...] = h_ref[...] + (0.5 * par_ref[2:3, :]) * acc[...]

    row = pl.BlockSpec((tm, D), lambda i, j: (i, 0))
    hid = pl.BlockSpec((tm, tf), lambda i, j: (i, j))
    wspec = [pl.BlockSpec((None, tf, D), lambda i, j, k=k: (k, j, 0)) for k in range(3)]
    return pl.pallas_call(
        body, name="ffn_fwd", grid=(T // tm, nj),
        in_specs=[row, pl.BlockSpec((8, D), lambda i, j: (0, 0))] + wspec,
        out_specs=[row, row, hid, hid, hid],
        out_shape=[SDS((T, D), f32), SDS((T, D), bf16), SDS((T, F), bf16), SDS((T, F), bf16), SDS((T, F), bf16)],
        scratch_shapes=[pltpu.VMEM((tm, D), bf16), pltpu.VMEM((tm, D), f32)],
        compiler_params=_cp(("arbitrary", "arbitrary"), 52),
    )(h, par, w3, w3, w3)


def _ffn_bwd_tok(dh, h, par, fo, gs, us, w3):
    tm, tf = 512, 256
    nj = F // tf

    def body(dh_ref, h_ref, par_ref, fo_ref, g_ref, u_ref, wg_ref, wu_ref, wd_ref,
             dhin_ref, dg_ref, du_ref, y_ref, dfb_ref, dpar_ref, df_scr, dyacc):
        i, j = pl.program_id(0), pl.program_id(1)

        @pl.when(jnp.logical_and(i == 0, j == 0))
        def _():
            dpar_ref[...] = jnp.zeros_like(dpar_ref)

        @pl.when(j == 0)
        def _():
            dh_v, par_v = dh_ref[...], par_ref[...]
            dfb = ((0.5 * par_v[2:3, :]) * dh_v).astype(bf16)
            df_scr[...] = dfb
            dfb_ref[...] = dfb
            dpar_ref[2:3, :] += 0.5 * _colsum(dh_v * fo_ref[...].astype(f32))
            y_ref[...] = _norm_mod(h_ref[...], par_v)[0].astype(bf16)
            dyacc[...] = jnp.zeros_like(dyacc)

        for half in range(2):
            rs = slice(half * (tm // 2), (half + 1) * (tm // 2))
            da = _nt(df_scr[rs, :], wd_ref[...])
            g = g_ref[rs, :].astype(f32)
            u = u_ref[rs, :].astype(f32)
            sig = jax.nn.sigmoid(g)
            du = (da * (g * sig)).astype(bf16)
            dg = (da * u * (sig * (1.0 + g * (1.0 - sig)))).astype(bf16)
            dg_ref[rs, :] = dg
            du_ref[rs, :] = du
            dyacc[rs, :] += _nn(dg, wg_ref[...]) + _nn(du, wu_ref[...])

        @pl.when(j == nj - 1)
        def _():
            dx, dsh, dsc, dng = _norm_mod_bwd(dyacc[...], h_ref[...], par_ref[...])
            dhin_ref[...] = dh_ref[...] + dx
            dpar_ref[0:1, :] += dsh
            dpar_ref[1:2, :] += dsc
            dpar_ref[3:4, :] += dng

    row = pl.BlockSpec((tm, D), lambda i, j: (i, 0))
    hid = pl.BlockSpec((tm, tf), lambda i, j: (i, j))
    one = pl.BlockSpec((8, D), lambda i, j: (0, 0))
    wspec = [pl.BlockSpec((None, tf, D), lambda i, j, k=k: (k, j, 0)) for k in range(3)]
    return pl.pallas_call(
        body, name="ffn_bwd_tok", grid=(T // tm, nj),
        in_specs=[row, row, one, row, hid, hid] + wspec,
        out_specs=[row, hid, hid, row, row, one],
        out_shape=[SDS((T, D), f32), SDS((T, F), bf16), SDS((T, F), bf16), SDS((T, D), bf16),
                   SDS((T, D), bf16), SDS((8, D), f32)],
        scratch_shapes=[pltpu.VMEM((tm, D), bf16), pltpu.VMEM((tm, D), f32)],
        compiler_params=_cp(("arbitrary", "arbitrary"), 48),
    )(dh, h, par, fo, gs, us, w3, w3, w3)


def _ffn_bwd_w(a_s, dg_s, du_s, y, dfb, anchor=None):
    tf, tk = F // 2, 256
    nk = T // tk

    def body(a_ref, dg_ref, du_ref, y_ref, df_ref, *rest):
        o_ref, accg, accu, accd = rest[-4:]
        kk = pl.program_id(1)

        @pl.when(kk == 0)
        def _():
            accg[...] = jnp.zeros_like(accg)
            accu[...] = jnp.zeros_like(accu)
            accd[...] = jnp.zeros_like(accd)

        yv = y_ref[...]
        accg[...] += _tn(dg_ref[...], yv)
        accu[...] += _tn(du_ref[...], yv)
        accd[...] += _tn(a_ref[...], df_ref[...])

        @pl.when(kk == nk - 1)
        def _():
            o_ref[0] = accg[...].astype(bf16)
            o_ref[1] = accu[...].astype(bf16)
            o_ref[2] = accd[...].astype(bf16)

    hid = pl.BlockSpec((tk, tf), lambda j, kk: (kk, j))
    row = pl.BlockSpec((tk, D), lambda j, kk: (kk, 0))
    extra = [] if anchor is None else [anchor]
    return pl.pallas_call(
        body, name="ffn_bwd_w", grid=(F // tf, nk),
        in_specs=[hid, hid, hid, row, row] + [pl.BlockSpec((8, CH), lambda j, kk: (0, 0))] * len(extra),
        out_specs=pl.BlockSpec((3, tf, D), lambda j, kk: (0, j, 0)), out_shape=SDS((3, F, D), bf16),
        scratch_shapes=[pltpu.VMEM((tf, D), f32)] * 3,
        compiler_params=_cp(("arbitrary", "arbitrary"), 48),
    )(a_s, dg_s, du_s, y, dfb, *extra)


def _tn_matmul(a, b, bm, name):
    M, N = a.shape[1], b.shape[1]
    tk = 512
    nk = T // tk

    def body(a_ref, b_ref, o_ref, acc):
        kk = pl.program_id(1)

        @pl.when(kk == 0)
        def _():
            acc[...] = jnp.zeros_like(acc)

        acc[...] += _tn(a_ref[...], b_ref[...])

        @pl.when(kk == nk - 1)
        def _():
            o_ref[...] = acc[...].astype(bf16)

    return pl.pallas_call(
        body, name=name, grid=(M // bm, nk),
        in_specs=[pl.BlockSpec((tk, bm), lambda i, kk: (kk, i)), pl.BlockSpec((tk, N), lambda i, kk: (kk, 0))],
        out_specs=pl.BlockSpec((bm, N), lambda i, kk: (i, 0)), out_shape=SDS((M, N), bf16),
        scratch_shapes=[pltpu.VMEM((bm, N), f32)],
        compiler_params=_cp(("arbitrary", "arbitrary"), 40),
    )(a, b)


def _rope_tables():
    inv = ROPE_THETA ** (-jnp.arange(0, 64, 2, dtype=f32) / 64)
    ang = jnp.arange(T, dtype=f32)[:, None] * inv[None, :]
    ang = jnp.concatenate([ang, ang], axis=-1)
    cos, sin = jnp.tile(jnp.cos(ang), (1, 8)), jnp.tile(jnp.sin(ang), (1, 8))
    low = (jnp.arange(HW) % 64 < 32)[None, :]
    return cos, jnp.where(low, -sin, 0.0), jnp.where(low, 0.0, sin)


def _rope(t, cos, sin_lo, sin_hi):
    return t * cos + pltpu.roll(t, HW - 32, 1) * sin_lo + pltpu.roll(t, 32, 1) * sin_hi


def _rope_t(g, cos, sin_lo, sin_hi):
    return g * cos + pltpu.roll(g * sin_lo, 32, 1) + pltpu.roll(g * sin_hi, HW - 32, 1)


def _inproj_fwd(h, par, w_in_t, tabs):
    tm = 512

    def body(h_ref, par_ref, w_ref, cos_ref, slo_ref, shi_ref, ua_ref, va_ref, q_ref, k_ref, v_ref):
        y = _norm_mod(h_ref[...], par_ref[...])[0].astype(bf16)
        proj = lambda c: _nt(y, w_ref[c * HW:(c + 1) * HW, :])
        ua_ref[...] = proj(0)
        va_ref[...] = proj(1)
        cos, slo, shi = cos_ref[...], slo_ref[...], shi_ref[...]
        q_ref[...] = (_rope(proj(2), cos, slo, shi) * 0.125).astype(bf16)
        k_ref[...] = _rope(proj(3), cos, slo, shi).astype(bf16)
        v_ref[...] = proj(4).astype(bf16)

    row = pl.BlockSpec((tm, D), lambda i: (i, 0))
    half = pl.BlockSpec((tm, HW), lambda i: (i, 0))
    return pl.pallas_call(
        body, name="inproj_fwd", grid=(T // tm,),
        in_specs=[row, pl.BlockSpec((8, D), lambda i: (0, 0)), pl.BlockSpec((INW, D), lambda i: (0, 0)), half, half, half],
        out_specs=[half] * 5,
        out_shape=[SDS((T, HW), f32), SDS((T, HW), f32), SDS((T, HW), bf16), SDS((T, HW), bf16), SDS((T, HW), bf16)],
        compiler_params=_cp(("arbitrary",), 48),
    )(h, par, w_in_t, *tabs)


def _head_masks():
    lane = lax.broadcasted_iota(jnp.int32, (1, CH), 1)
    return lane < 64, lane >= 64


def _stack_heads(x):
    lo, hi = _head_masks()
    zero = jnp.zeros_like(x)
    return jnp.concatenate([jnp.where(lo, x, zero), jnp.where(hi, x, zero)], axis=0)


def _band_mask(has_prev):
    row = lax.broadcasted_iota(jnp.int32, (2 * CH, 2 * CH), 0) & (CH - 1)
    col = lax.broadcasted_iota(jnp.int32, (2 * CH, 2 * CH), 1)
    in_prev = jnp.logical_and(jnp.logical_and(col < CH, col >= row), has_prev)
    return jnp.logical_or(in_prev, jnp.logical_and(col >= CH, col - CH <= row))


def _attn_fwd(q, k, v, d, anchor=None):
    L = T // d
    nb = L // CH

    def body(q_ref, kc_ref, kp_ref, vc_ref, vp_ref, *rest):
        o_ref, lse_ref = rest[-2:]
        n = pl.program_id(1)
        col = lax.broadcasted_iota(jnp.int32, (CH, CH), 1)
        mask = _band_mask(n > 0)
        lo, _ = _head_masks()
        lse_tile = jnp.zeros((CH, CH), f32)
        for hp in range(4):
            sl = slice(hp * CH, (hp + 1) * CH)
            kk = jnp.concatenate([kp_ref[:, sl], kc_ref[:, sl]], axis=0)
            vv = jnp.concatenate([vp_ref[:, sl], vc_ref[:, sl]], axis=0)
            s = jnp.where(mask, _nt(_stack_heads(q_ref[:, sl]), kk), -jnp.inf)
            m = jnp.max(s, axis=-1, keepdims=True)
            p = jnp.exp(s - m)
            den = jnp.sum(p, axis=-1, keepdims=True)
            o = _nn(p.astype(bf16), vv) / den
            o_ref[:, sl] = jnp.where(lo, o[:CH], o[CH:]).astype(bf16)
            lse = m + jnp.log(den)
            lse_tile = jnp.where(col == 2 * hp, lse[:CH], jnp.where(col == 2 * hp + 1, lse[CH:], lse_tile))
        lse_ref[...] = lse_tile

    cur = pl.BlockSpec((CH, HW), lambda r, n: (n, r))
    prev = pl.BlockSpec((CH, HW), lambda r, n: (jnp.maximum(n - 1, 0), r))
    q2, k2, v2 = (t.reshape(L, d * HW) for t in (q, k, v))
    extra = [] if anchor is None else [anchor]
    o, lse = pl.pallas_call(
        body, name="attn_fwd_d%d" % d, grid=(d, nb),
        in_specs=[cur, cur, prev, cur, prev] + [pl.BlockSpec((8, CH), lambda r, n: (0, 0))] * len(extra),
        out_specs=[cur, pl.BlockSpec((CH, CH), lambda r, n: (n, r))],
        out_shape=[SDS((L, d * HW), bf16), SDS((L, d * CH), f32)],
        compiler_params=_cp(("arbitrary", "arbitrary"), 32),
    )(q2, k2, k2, v2, v2, *extra)
    return o.reshape(T, HW), lse.reshape(T, CH)


def _attn_combine(os_, lses):
    tm = 512

    def body(o0_ref, o1_ref, o2_ref, l0_ref, l1_ref, l2_ref, o_ref, lse_ref):
        l0, l1, l2 = l0_ref[...], l1_ref[...], l2_ref[...]
        m = jnp.maximum(jnp.maximum(l0, l1), l2)
        e = [jnp.exp(l0 - m), jnp.exp(l1 - m), jnp.exp(l2 - m)]
        s = e[0] + e[1] + e[2]
        w = [ei / s for ei in e]
        lse_ref[...] = m + jnp.log(s)
        lo, _ = _head_masks()
        for hp in range(4):
            sl = slice(hp * CH, (hp + 1) * CH)
            acc = jnp.zeros((tm, CH), f32)
            for wp, op_ref in zip(w, (o0_ref, o1_ref, o2_ref)):
                wexp = jnp.where(lo, wp[:, 2 * hp:2 * hp + 1], wp[:, 2 * hp + 1:2 * hp + 2])
                acc = acc + wexp * op_ref[:, sl].astype(f32)
            o_ref[:, sl] = acc.astype(bf16)

    half = pl.BlockSpec((tm, HW), lambda i: (i, 0))
    stat = pl.BlockSpec((tm, CH), lambda i: (i, 0))
    return pl.pallas_call(
        body, name="attn_combine", grid=(T // tm,), in_specs=[half] * 3 + [stat] * 3, out_specs=[half, stat],
        out_shape=[SDS((T, HW), bf16), SDS((T, CH), f32)],
        compiler_params=_cp(("arbitrary",), 32),
    )(*os_, *lses)


def _attn_bwd(q, k, v, o, do, lse, d):
    L = T // d
    nb = L // CH

    def body(qc_ref, qn_ref, kc_ref, kp_ref, vc_ref, vp_ref, oc_ref, on_ref, dc_ref, dn_ref, lc_ref, ln_ref,
             dq_ref, dk_ref, dv_ref):
        n = pl.program_id(1)
        mask_c = _band_mask(n > 0)
        mask_n = _band_mask(n < nb - 1)[:, :CH]
        lo, _ = _head_masks()
        lse_c, lse_n = lc_ref[...], ln_ref[...]
        for hp in range(4):
            sl = slice(hp * CH, (hp + 1) * CH)
            kc, vc = kc_ref[:, sl], vc_ref[:, sl]
            kk = jnp.concatenate([kp_ref[:, sl], kc], axis=0)
            vv = jnp.concatenate([vp_ref[:, sl], vc], axis=0)
            doc, don = dc_ref[:, sl], dn_ref[:, sl]
            qs_c, qs_n, ds_c, ds_n = _stack_heads(qc_ref[:, sl]), _stack_heads(qn_ref[:, sl]), _stack_heads(doc), _stack_heads(don)
            delta_c = jnp.sum(_stack_heads(doc.astype(f32) * oc_ref[:, sl].astype(f32)), axis=-1, keepdims=True)
            delta_n = jnp.sum(_stack_heads(don.astype(f32) * on_ref[:, sl].astype(f32)), axis=-1, keepdims=True)
            heads = lambda t: jnp.concatenate([t[:, 2 * hp:2 * hp + 1], t[:, 2 * hp + 1:2 * hp + 2]], axis=0)
            p1 = jnp.where(mask_c, jnp.exp(_nt(qs_c, kk) - heads(lse_c)), 0.0)
            g1 = (p1 * (_nt(ds_c, vv) - delta_c)).astype(bf16)
            dq = _nn(g1, kk)
            dq_ref[:, sl] = jnp.where(lo, dq[:CH], dq[CH:]).astype(bf16)
            p2 = jnp.where(mask_n, jnp.exp(_nt(qs_n, kc) - heads(lse_n)), 0.0)
            g2 = (p2 * (_nt(ds_n, vc) - delta_n)).astype(bf16)
            dk_ref[:, sl] = _tn(jnp.concatenate([g1[:, CH:], g2], axis=0),
                                jnp.concatenate([qs_c, qs_n], axis=0)).astype(bf16)
            dv_ref[:, sl] = _tn(jnp.concatenate([p1[:, CH:].astype(bf16), p2.astype(bf16)], axis=0),
                                jnp.concatenate([ds_c, ds_n], axis=0)).astype(bf16)

    cur = pl.BlockSpec((CH, HW), lambda r, n: (n, r))
    prev = pl.BlockSpec((CH, HW), lambda r, n: (jnp.maximum(n - 1, 0), r))
    nxt = pl.BlockSpec((CH, HW), lambda r, n: (jnp.minimum(n + 1, nb - 1), r))
    scur = pl.BlockSpec((CH, CH), lambda r, n: (n, r))
    snxt = pl.BlockSpec((CH, CH), lambda r, n: (jnp.minimum(n + 1, nb - 1), r))
    q2, k2, v2, o2, do2 = (t.reshape(L, d * HW) for t in (q, k, v, o, do))
    l2 = lse.reshape(L, d * CH)
    outs = pl.pallas_call(
        body, name="attn_bwd_d%d" % d, grid=(d, nb),
        in_specs=[cur, nxt, cur, prev, cur, prev, cur, nxt, cur, nxt, scur, snxt],
        out_specs=[cur] * 3, out_shape=[SDS((L, d * HW), bf16)] * 3,
        compiler_params=_cp(("arbitrary", "arbitrary"), 32),
    )(q2, q2, k2, k2, v2, v2, o2, o2, do2, do2, l2, l2)
    return [t.reshape(T, HW) for t in outs]


def _sgu_fwd(ua, va, o, lng, lnb, ws, bs_t):
    tm = 512

    def body(ua_ref, va_ref, o_ref, lg_ref, lb_ref, ws_ref, bs_ref, mix_ref):
        for hd in range(4):
            sl = slice(hd * CH, (hd + 1) * CH)
            w = ws_ref[hd]
            for cc in range(tm // CH):
                rs = slice(cc * CH, (cc + 1) * CH)
                u = _gelu(ua_ref[rs, sl])
                v = _gelu(va_ref[rs, sl])
                vc = v - _rowmean(v)
                vn = vc * lax.rsqrt(_rowmean(vc * vc) + EPS) * lg_ref[:, sl] + lb_ref[:, sl]
                z = _nn(w, vn.astype(bf16)) + bs_ref[:, hd:hd + 1]
                mix_ref[rs, sl] = (u * z).astype(bf16)
        mix_ref[:, HW:] = o_ref[...]

    half = pl.BlockSpec((tm, HW), lambda i: (i, 0))
    vec = pl.BlockSpec((1, HW), lambda i: (0, 0))
    return pl.pallas_call(
        body, name="sgu_fwd", grid=(T // tm,),
        in_specs=[half, half, half, vec, vec, pl.BlockSpec((4, CH, CH), lambda i: (0, 0, 0)),
                  pl.BlockSpec((CH, 4), lambda i: (0, 0))],
        out_specs=pl.BlockSpec((tm, 2 * HW), lambda i: (i, 0)), out_shape=SDS((T, 2 * HW), bf16),
        compiler_params=_cp(("arbitrary",), 32),
    )(ua, va, o, lng, lnb, ws, bs_t)


def _sgu_bwd(ua, va, d_a, lng, lnb, ws, ws_t, bs_t):
    tm = 512

    def body(ua_ref, va_ref, d_ref, lg_ref, lb_ref, ws_ref, wst_ref, bs_ref, dsg_ref, dln_ref, dws_ref, db_ref):
        @pl.when(pl.program_id(0) == 0)
        def _():
            dln_ref[...] = jnp.zeros_like(dln_ref)
            dws_ref[...] = jnp.zeros_like(dws_ref)
            db_ref[...] = jnp.zeros_like(db_ref)

        for hd in range(4):
            sl = slice(hd * CH, (hd + 1) * CH)
            w, wt = ws_ref[hd], wst_ref[hd]
            lg = lg_ref[:, sl]
            for cc in range(tm // CH):
                rs = slice(cc * CH, (cc + 1) * CH)
                xa, xv, dd = ua_ref[rs, sl], va_ref[rs, sl], d_ref[rs, sl]
                u, v = _gelu(xa), _gelu(xv)
                vc = v - _rowmean(v)
                rstd = lax.rsqrt(_rowmean(vc * vc) + EPS)
                xh = vc * rstd
                vnb = (xh * lg + lb_ref[:, sl]).astype(bf16)
                z = _nn(w, vnb) + bs_ref[:, hd:hd + 1]
                dz = dd * u
                dzb = dz.astype(bf16)
                dsg_ref[rs, sl] = (dd * z * _gelu_grad(xa)).astype(bf16)
                dws_ref[hd] += _nt(dzb, vnb)
                db_ref[hd] += dz
                dvn = _nn(wt, dzb)
                dln_ref[0:1, sl] += _colsum(dvn * xh)
                dln_ref[1:2, sl] += _colsum(dvn)
                dxh = dvn * lg
                dv = rstd * (dxh - _rowmean(dxh) - xh * _rowmean(dxh * xh))
                dsg_ref[rs, HW + hd * CH:HW + (hd + 1) * CH] = (dv * _gelu_grad(xv)).astype(bf16)

    half = pl.BlockSpec((tm, HW), lambda i: (i, 0))
    vec = pl.BlockSpec((1, HW), lambda i: (0, 0))
    mat = pl.BlockSpec((4, CH, CH), lambda i: (0, 0, 0))
    return pl.pallas_call(
        body, name="sgu_bwd", grid=(T // tm,),
        in_specs=[half, half, half, vec, vec, mat, mat, pl.BlockSpec((CH, 4), lambda i: (0, 0))],
        out_specs=[pl.BlockSpec((tm, 2 * HW), lambda i: (i, 0)), pl.BlockSpec((8, HW), lambda i: (0, 0)), mat, mat],
        out_shape=[SDS((T, 2 * HW), bf16), SDS((8, HW), f32), SDS((4, CH, CH), f32), SDS((4, CH, CH), f32)],
        compiler_params=_cp(("arbitrary",), 32),
    )(ua, va, d_a, lng, lnb, ws, ws_t, bs_t)


def _outproj_fwd(mixed, w_out, h, par):
    tm = 512

    def body(mix_ref, w_ref, h_ref, par_ref, ho_ref, po_ref):
        p = _nn(mix_ref[...], w_ref[...])
        po_ref[...] = p.astype(bf16)
        ho_ref[...] = h_ref[...] + par_ref[2:3, :] * p

    row = pl.BlockSpec((tm, D), lambda i: (i, 0))
    return pl.pallas_call(
        body, name="outproj_fwd", grid=(T // tm,),
        in_specs=[row, pl.BlockSpec((D, D), lambda i: (0, 0)), row, pl.BlockSpec((8, D), lambda i: (0, 0))],
        out_specs=[row, row], out_shape=[SDS((T, D), f32), SDS((T, D), bf16)],
        compiler_params=_cp(("arbitrary",), 32),
    )(mixed, w_out, h, par)


def _outproj_bwd(dh, po, w_out, par):
    tm = 512

    def body(dh_ref, po_ref, w_ref, par_ref, do_ref, da_ref, db_ref, dpar_ref):
        @pl.when(pl.program_id(0) == 0)
        def _():
            dpar_ref[...] = jnp.zeros_like(dpar_ref)

        dh_v = dh_ref[...]
        dob = (par_ref[2:3, :] * dh_v).astype(bf16)
        do_ref[...] = dob
        dpar_ref[2:3, :] += _colsum(dh_v * po_ref[...].astype(f32))
        dm = _nt(dob, w_ref[...])
        da_ref[...] = dm[:, :HW]
        db_ref[...] = dm[:, HW:].astype(bf16)

    row = pl.BlockSpec((tm, D), lambda i: (i, 0))
    half = pl.BlockSpec((tm, HW), lambda i: (i, 0))
    one = pl.BlockSpec((8, D), lambda i: (0, 0))
    return pl.pallas_call(
        body, name="outproj_bwd", grid=(T // tm,),
        in_specs=[row, row, pl.BlockSpec((D, D), lambda i: (0, 0)), one],
        out_specs=[row, half, half, one],
        out_shape=[SDS((T, D), bf16), SDS((T, HW), f32), SDS((T, HW), bf16), SDS((8, D), f32)],
        compiler_params=_cp(("arbitrary",), 32),
    )(dh, po, w_out, par)


def _inproj_bwd_tok(dh, h, par, w_in_t, dsg, dqs, dks, dvs, tabs):
    tm = 256

    def body(dh_ref, h_ref, par_ref, w_ref, dsg_ref, dq0, dq1, dq2, dk0, dk1, dk2, dv0, dv1, dv2,
             cos_ref, slo_ref, shi_ref, dhin_ref, dp_ref, y_ref, dpar_ref):
        @pl.when(pl.program_id(0) == 0)
        def _():
            dpar_ref[...] = jnp.zeros_like(dpar_ref)

        cos, slo, shi = cos_ref[...], slo_ref[...], shi_ref[...]
        total = lambda a, b, c: a[...].astype(f32) + b[...].astype(f32) + c[...].astype(f32)
        dq = total(dq0, dq1, dq2) * 0.125
        dk = total(dk0, dk1, dk2)
        dp_ref[:, :2 * HW] = dsg_ref[...]
        dp_ref[:, 2 * HW:3 * HW] = _rope_t(dq, cos, slo, shi).astype(bf16)
        dp_ref[:, 3 * HW:4 * HW] = _rope_t(dk, cos, slo, shi).astype(bf16)
        dp_ref[:, 4 * HW:] = total(dv0, dv1, dv2).astype(bf16)
        dy = _nn(dp_ref[...], w_ref[...])
        par_v, h_v = par_ref[...], h_ref[...]
        y_ref[...] = _norm_mod(h_v, par_v)[0].astype(bf16)
        dx, dsh, dsc, dng = _norm_mod_bwd(dy, h_v, par_v)
        dhin_ref[...] = dh_ref[...] + dx
        dpar_ref[0:1, :] += dsh
        dpar_ref[1:2, :] += dsc
        dpar_ref[3:4, :] += dng

    row = pl.BlockSpec((tm, D), lambda i: (i, 0))
    half = pl.BlockSpec((tm, HW), lambda i: (i, 0))
    one = pl.BlockSpec((8, D), lambda i: (0, 0))
    return pl.pallas_call(
        body, name="inproj_bwd_tok", grid=(T // tm,),
        in_specs=[row, row, one, pl.BlockSpec((INW, D), lambda i: (0, 0)), row] + [half] * 12,
        out_specs=[row, pl.BlockSpec((tm, INW), lambda i: (i, 0)), row, one],
        out_shape=[SDS((T, D), f32), SDS((T, INW), bf16), SDS((T, D), bf16), SDS((8, D), f32)],
        compiler_params=_cp(("arbitrary",), 48),
    )(dh, h, par, w_in_t, dsg, *dqs, *dks, *dvs, *tabs)


def _loss_head(h, par, target):
    tm = 512

    def body(h_ref, par_ref, t_ref, dh_ref, acc_ref):
        @pl.when(pl.program_id(0) == 0)
        def _():
            acc_ref[...] = jnp.zeros_like(acc_ref)

        x, g = h_ref[...], par_ref[3:4, :]
        rstd = lax.rsqrt(_rowmean(x * x) + EPS)
        xhat = x * rstd
        err = xhat * g - t_ref[...]
        acc_ref[0:1, :] += _colsum(err * err)
        dy = err * (1.0 / D)
        acc_ref[3:4, :] += _colsum(dy * xhat)
        dxhat = dy * g
        dh_ref[...] = rstd * (dxhat - xhat * _rowmean(dxhat * xhat))

    row = pl.BlockSpec((tm, D), lambda i: (i, 0))
    one = pl.BlockSpec((8, D), lambda i: (0, 0))
    return pl.pallas_call(
        body, name="loss_head", grid=(T // tm,), in_specs=[row, one, row], out_specs=[row, one],
        out_shape=[SDS((T, D), f32), SDS((8, D), f32)], compiler_params=_cp(("arbitrary",), 32),
    )(h, par, target)


ELEMENTWISE_VMEM_BUDGET = 20 << 20


def _block_rows(rows, bytes_per_row):
    cap = ELEMENTWISE_VMEM_BUDGET // bytes_per_row
    if rows <= cap:
        return rows
    return next(b for b in range(cap - cap % 16, 0, -16) if rows % b == 0)


def _sum_slots(land, name):
    _, R, C = land.shape
    br = _block_rows(R, 2 * NDEV * C * land.dtype.itemsize + 2 * C * 4)

    def body(l_ref, o_ref):
        acc = l_ref[0].astype(f32)
        for s in range(1, NDEV):
            acc = acc + l_ref[s].astype(f32)
        o_ref[...] = acc

    return pl.pallas_call(
        body, name=name, grid=(R // br,), in_specs=[pl.BlockSpec((NDEV, br, C), lambda i: (0, i, 0))],
        out_specs=pl.BlockSpec((br, C), lambda i: (i, 0)), out_shape=SDS((R, C), f32),
        compiler_params=_cp(("arbitrary",), 32),
    )(land)


def _sum_partials(land, own, me, name):
    r = land.shape[-2]

    def body(me_ref, l_ref, own_ref, o_ref):
        mine = own_ref[...].astype(f32)
        acc = jnp.where(me_ref[0] == 0, mine, l_ref[0].astype(f32))
        for s in range(1, NDEV):
            acc = acc + jnp.where(me_ref[0] == s, mine, l_ref[s].astype(f32))
        o_ref[...] = acc

    if own.ndim == 3:
        kk = own.shape[0]
        specs = dict(grid=(kk,),
                     in_specs=[pl.BlockSpec((NDEV, None, r, D), lambda i, me_ref: (0, i, 0, 0)),
                               pl.BlockSpec((None, r, D), lambda i, me_ref: (i, me_ref[0], 0))],
                     out_specs=pl.BlockSpec((None, r, D), lambda i, me_ref: (i, 0, 0)))
        out_shape = SDS((kk, r, D), f32)
    else:
        specs = dict(grid=(1,),
                     in_specs=[pl.BlockSpec((NDEV, r, D), lambda i, me_ref: (0, 0, 0)),
                               pl.BlockSpec((r, D), lambda i, me_ref: (me_ref[0], 0))],
                     out_specs=pl.BlockSpec((r, D), lambda i, me_ref: (0, 0)))
        out_shape = SDS((r, D), f32)
    return pl.pallas_call(
        body, name=name, out_shape=out_shape,
        grid_spec=pltpu.PrefetchScalarGridSpec(num_scalar_prefetch=1, **specs),
        compiler_params=_cp(("arbitrary",), 32),
    )(me.reshape(1), land, own)


def _adamw(w, g, m, v, name):
    R, C = w.shape
    br = _block_rows(R, 2 * 7 * C * 4)

    def body(w_ref, g_ref, m_ref, v_ref, d_ref, mo_ref, vo_ref):
        gv = g_ref[...]
        m2 = B1 * m_ref[...] + (1.0 - B1) * gv
        v2 = B2 * v_ref[...] + (1.0 - B2) * (gv * gv)
        mo_ref[...] = m2
        vo_ref[...] = v2
        m_hat = m2 / (1.0 - B1 ** STEP)
        v_hat = v2 / (1.0 - B2 ** STEP)
        d_ref[...] = -LR * (m_hat / (jnp.sqrt(v_hat) + AEPS) + WD * w_ref[...])

    blk = pl.BlockSpec((br, C), lambda i: (i, 0))
    return pl.pallas_call(
        body, name=name, grid=(R // br,), in_specs=[blk] * 4, out_specs=[blk] * 3,
        out_shape=[SDS((R, C), f32)] * 3, compiler_params=_cp(("arbitrary",), 32),
    )(w, g, m, v)


def _adamw_nd(w, g, m, v, name):
    shp = w.shape
    r2 = (-1, shp[-1]) if w.ndim > 1 else (8, shp[0] // 8)
    outs = _adamw(w.reshape(r2), g.reshape(r2), m.reshape(r2), v.reshape(r2), name)
    return [o.reshape(shp) for o in outs]


def _par_rows(mod_l, s, gain):
    rows = jnp.pad(mod_l.reshape(9, D)[3 * s:3 * s + 3], ((0, 5), (0, 0)))
    return rows + jnp.pad(gain[None, :], ((3, 4), (0, 0)))


def _pad_rows(a):
    a = a.reshape(-1, CH)
    return jnp.pad(a, ((0, (-a.shape[0]) % 8), (0, 0)))


def _prepare(c, ada_w, ada_b, norm_g):
    me = 4 * lax.axis_index("x") + 2 * lax.axis_index("y") + lax.axis_index("c")

    pay = jnp.pad(c, ((0, 7), (0, 0)))
    pay = jnp.concatenate([pay, jnp.pad(norm_g.reshape(6, OUTS), ((0, 2), (0, D - OUTS)))], axis=0)
    got = _all_gather_small(pay, "gather_c")
    c_all = got[:, 0, :]
    gains = got[:, 8:14, :OUTS].transpose(1, 0, 2).reshape(DEPTH, 3, D)

    ada_b_loc = lax.dynamic_slice(ada_b, (0, me * ADAS), (DEPTH, ADAS)).reshape(DEPTH, 1, ADAS)
    mod_cols = _mod_fwd(c_all, ada_w, ada_b_loc)
    got = _all_gather_small(mod_cols.reshape(DEPTH * NDEV, ADAS), "gather_mod").reshape(NDEV, DEPTH, NDEV, ADAS)
    mod = lax.dynamic_index_in_dim(got, me, axis=2, keepdims=False).transpose(1, 0, 2).reshape(DEPTH, 9 * D)
    pars = [[_par_rows(mod[l], s, gains[l, s]) for s in range(3)] for l in range(DEPTH)]
    return me, c_all, pars


def _fwd_bwd(x2, target, pars, get_w, put_g, small_ready, sgu_ln_g, sgu_ln_b, sgu_w, sgu_b, final_g):
    tabs = _rope_tables()
    tril = jnp.tril(jnp.ones((CH, CH), dtype=bool))
    ws_m = jnp.where(tril[None, None], sgu_w, 0.0).astype(bf16)
    ws_mt = jnp.swapaxes(ws_m, -1, -2)
    behind = lambda rows, token: rows if token is None else rows + token[0, 0]

    h = x2
    saved = []
    for l in range(DEPTH):
        lng, lnb = sgu_ln_g[l].reshape(1, HW), sgu_ln_b[l].reshape(1, HW)
        bs_t = sgu_b[l].T
        h0 = h
        w, token = get_w(4 * l, h0)
        h1, fo1, g1, u1, a1 = _ffn_fwd(h0, behind(pars[l][0], token), w)
        w, token = get_w(4 * l + 2, h1)
        ua, va, q, k, v = _inproj_fwd(h1, behind(pars[l][1], token), w, tabs)
        branches, token = [], None
        for d in PATTERN_DILATIONS:
            branches.append(_attn_fwd(q, k, v, d, token))
            if len(branches) == 2 and l + 1 < DEPTH:
                token = get_w(4 * l + 1, branches[-1][1])[1]
        o, lse = _attn_combine([b[0] for b in branches], [b[1] for b in branches])
        mixed = _sgu_fwd(ua, va, o, lng, lnb, ws_m[l], bs_t)
        w, token = get_w(4 * l + 3, mixed)
        h2, po = _outproj_fwd(mixed, w, h1, behind(pars[l][1], token))
        w, token = get_w(4 * l + 1, h2)
        h3, fo2, g2, u2, a2 = _ffn_fwd(h2, behind(pars[l][2], token), w)
        saved.append((h0, h1, h2, fo1, g1, u1, a1, ua, va, q, k, v, o, mixed, lse, po, fo2, g2, u2, a2))
        h = h3

    par_f = jnp.pad(final_g[None, :], ((3, 4), (0, 0)))
    dh, head = _loss_head(h, par_f, target)

    dmods, dgains, dsgu = [None] * DEPTH, [None] * DEPTH, [None] * DEPTH
    token = None
    for l in reversed(range(DEPTH)):
        w_f1, w_f2, w_i, w_o = (get_w(4 * l + j, None)[0] for j in (0, 1, 2, 3))
        h0, h1, h2, fo1, g1, u1, a1, ua, va, q, k, v, o, mixed, lse, po, fo2, g2, u2, a2 = saved[l]
        lng, lnb = sgu_ln_g[l].reshape(1, HW), sgu_ln_b[l].reshape(1, HW)
        bs_t = sgu_b[l].T

        dh, dg_s, du_s, y, dfb, dpar3 = _ffn_bwd_tok(dh, h2, behind(pars[l][2], token), fo2, g2, u2, w_f2)
        token = put_g(4 * l + 1, _ffn_bwd_w(a2, dg_s, du_s, y, dfb))

        dob, d_a, d_b, dpar2g = _outproj_bwd(dh, po, w_o, behind(pars[l][1], token))
        token = put_g(4 * l + 3, _tn_matmul(mixed, dob, 512, "w_out_grad"))
        parts = [_attn_bwd(q, k, v, o, d_b, lse, d) for d in PATTERN_DILATIONS]
        dsg, dln, dws, dbl = _sgu_bwd(ua, va, d_a, lng, lnb, ws_m[l], ws_mt[l], bs_t)
        dh, dp, y2, dpar2 = _inproj_bwd_tok(dh, h1, behind(pars[l][1], token), w_i, dsg, [p[0] for p in parts],
                                            [p[1] for p in parts], [p[2] for p in parts], tabs)
        token = put_g(4 * l + 2, _tn_matmul(dp, y2, 640, "w_in_grad"))

        dh, dg_s, du_s, y, dfb, dpar1 = _ffn_bwd_tok(dh, h0, behind(pars[l][0], token), fo1, g1, u1, w_f1)
        dmods[l] = jnp.concatenate([dpar1[0:3], dpar2[0:2], dpar2g[2:3], dpar3[0:3]], axis=0).reshape(9 * D)
        dgains[l] = jnp.stack([dpar1[3], dpar2[3], dpar3[3]])
        dsgu[l] = (dln[0], dln[1], jnp.where(tril[None], dws, 0.0), jnp.sum(dbl, axis=-1))
        token = small_ready(head, dmods, dgains, dsgu) if l == 0 else None
        token = put_g(4 * l, _ffn_bwd_w(a1, dg_s, du_s, y, dfb, token))
    return dh, token


def kernel(x, c, ada_w, ada_b, norm_g, ffn1_wg, ffn1_wu, ffn1_wd, ffn2_wg, ffn2_wu, ffn2_wd, w_in, sgu_ln_g, sgu_ln_b, sgu_w, sgu_b, w_out, final_g, loss_target, m_ada_w, m_ada_b, m_norm_g, m_ffn1_wg, m_ffn1_wu, m_ffn1_wd, m_ffn2_wg, m_ffn2_wu, m_ffn2_wd, m_w_in, m_sgu_ln_g, m_sgu_ln_b, m_sgu_w, m_sgu_b, m_w_out, m_final_g, v_ada_w, v_ada_b, v_norm_g, v_ffn1_wg, v_ffn1_wu, v_ffn1_wd, v_ffn2_wg, v_ffn2_wu, v_ffn2_wd, v_w_in, v_sgu_ln_g, v_sgu_ln_b, v_sgu_w, v_sgu_b, v_w_out, v_final_g):
    me, c_all, pars = _prepare(c, ada_w, ada_b, norm_g)

    tr = lambda w: jnp.swapaxes(w, -1, -2).astype(bf16)
    locs = []
    for l in range(DEPTH):
        locs.append(jnp.stack([tr(ffn1_wg[l]), tr(ffn1_wu[l]), ffn1_wd[l].astype(bf16)]))
        locs.append(jnp.stack([tr(ffn2_wg[l]), tr(ffn2_wu[l]), ffn2_wd[l].astype(bf16)]))
        locs.append(tr(w_in[l]))
        locs.append(w_out[l].astype(bf16))
    locs, pars = lax.optimization_barrier((locs, pars))

    first, locs = lax.optimization_barrier((_all_gather_rows([locs[0]], "gather_first")[0], locs))
    placed = [_place_own(a, me) for a in locs[1:]]
    ready = {0: first}
    groups = ([2, 3], [1], [4], [6, 7], [5])
    flying = {}

    def start_group(gi, behind):
        pieces = groups[gi]
        arrays, _ = lax.optimization_barrier(([placed[p - 1] for p in pieces], behind))
        started = _exchange_start(False, arrays, "gather_start_%d" % gi)
        flying.update({p: (gi, started[:-1]) for p in pieces})
        return started[-1]

    first_token = start_group(0, first)

    def get_w(piece, after):
        token = first_token if piece == 0 else None
        if piece not in ready:
            gi, started = flying[piece]
            got_w = _exchange_wait(False, started, after, "gather_wait_%d" % gi)
            ready.update(zip(groups[gi], got_w))
            if gi + 1 < len(groups):
                token = start_group(gi + 1, got_w)
        return ready[piece], token

    sent = {}

    def put_g(piece, grad):
        land = lax.empty((NDEV,) + grad.shape[:-2] + (grad.shape[-2] // NDEV, D), bf16)
        started = _exchange_start(True, [grad, land], "scatter_start_%d" % piece)
        sent[piece] = started[:-1]
        return started[-1]

    small_sent = []

    def small_ready(head, dmods, dgains, dsgu):
        loss_part = 0.5 * jnp.sum(head[0]) / D
        small = jnp.concatenate([
            _pad_rows(jnp.stack(dmods)), _pad_rows(jnp.stack(dgains)),
            _pad_rows(jnp.stack([s[0] for s in dsgu])), _pad_rows(jnp.stack([s[1] for s in dsgu])),
            _pad_rows(jnp.stack([s[3] for s in dsgu])), _pad_rows(jnp.stack([s[2] for s in dsgu])),
            _pad_rows(head[3]), _pad_rows(jnp.pad(loss_part[None], (0, CH - 1)))], axis=0)
        started = _exchange_start(False, [_place_own(small, me, "place_small")], "small_start")
        small_sent.append(started[:-1])
        return started[-1]

    dh, last = _fwd_bwd(x[0], loss_target[0], pars, get_w, put_g, small_ready, sgu_ln_g, sgu_ln_b, sgu_w, sgu_b, final_g)
    grad_x = dh[None]

    got = _exchange_wait(False, small_sent[0], last, "small_wait")[0].reshape(NDEV, -1, CH)
    tot = _sum_slots(got, "sum_small")
    n_mod, n_gain, n_sw = DEPTH * 9 * D // CH, DEPTH * 3 * D // CH, DEPTH * 4 * CH
    offs = [0, n_mod, n_mod + n_gain, n_mod + n_gain + 8, n_mod + n_gain + 16, n_mod + n_gain + 24]
    g_ada_b = tot[offs[0]:offs[1]].reshape(DEPTH, 9 * D)
    g_gain_full = tot[offs[1]:offs[2]].reshape(DEPTH, 3, D)
    g_ln_g = tot[offs[2]:offs[3]].reshape(DEPTH, 4, CH)
    g_ln_b = tot[offs[3]:offs[4]].reshape(DEPTH, 4, CH)
    g_sb = tot[offs[4]:offs[5]].reshape(DEPTH, 4, CH)
    g_sw = tot[offs[5]:offs[5] + n_sw].reshape(DEPTH, 4, CH, CH)
    g_final = tot[offs[5] + n_sw:offs[5] + n_sw + 8].reshape(D)
    loss = tot[offs[5] + n_sw + 8, 0]
    g_norm = lax.dynamic_slice(g_gain_full, (0, 0, me * OUTS), (DEPTH, 3, OUTS))

    dmod_all = got[:, offs[0]:offs[1]].reshape(NDEV, DEPTH, 9 * D)
    dmod_cols = lax.dynamic_slice(dmod_all, (0, 0, me * ADAS), (NDEV, DEPTH, ADAS)).transpose(1, 0, 2)
    g_ada_w = _ada_grad((c_all * jax.nn.sigmoid(c_all)).T, dmod_cols)

    sums, after = {}, tot

    def collect(piece, after):
        own, land = _exchange_wait(True, sent[piece], after, "scatter_wait_%d" % piece)
        sums[piece] = _sum_partials(land, own, me, "sum_grads")
        return sums[piece]

    for piece in (5, 7, 6, 4, 1, 3, 2):
        after = collect(piece, after)
    back = lambda t: jnp.swapaxes(t, -1, -2)
    f2 = jnp.stack([sums[4 * l + 1] for l in range(DEPTH)])
    g_w_in = back(jnp.stack([sums[4 * l + 2] for l in range(DEPTH)]))
    g_w_out = jnp.stack([sums[4 * l + 3] for l in range(DEPTH)])

    ws = [ada_w, ada_b, norm_g, ffn1_wg, ffn1_wu, ffn1_wd, ffn2_wg, ffn2_wu, ffn2_wd, w_in, sgu_ln_g, sgu_ln_b, sgu_w,
          sgu_b, w_out, final_g]
    ms = [m_ada_w, m_ada_b, m_norm_g, m_ffn1_wg, m_ffn1_wu, m_ffn1_wd, m_ffn2_wg, m_ffn2_wu, m_ffn2_wd, m_w_in,
          m_sgu_ln_g, m_sgu_ln_b, m_sgu_w, m_sgu_b, m_w_out, m_final_g]
    vs = [v_ada_w, v_ada_b, v_norm_g, v_ffn1_wg, v_ffn1_wu, v_ffn1_wd, v_ffn2_wg, v_ffn2_wu, v_ffn2_wd, v_w_in,
          v_sgu_ln_g, v_sgu_ln_b, v_sgu_w, v_sgu_b, v_w_out, v_final_g]
    gw = [g_ada_w, g_ada_b, g_norm, None, None, None, back(f2[:, 0]), back(f2[:, 1]), f2[:, 2],
          g_w_in, g_ln_g, g_ln_b, g_sw, g_sb, g_w_out, g_final]
    upd = [None] * len(ws)
    for i in (1, 2, 10, 11, 12, 13, 15, 6, 7, 8, 9, 14, 0):
        upd[i] = _adamw_nd(ws[i], gw[i], ms[i], vs[i], "adamw")
    collect(0, upd[0][0])
    f1 = jnp.stack([sums[4 * l] for l in range(DEPTH)])
    gw[3:6] = [back(f1[:, 0]), back(f1[:, 1]), f1[:, 2]]
    for i in (3, 4, 5):
        upd[i] = _adamw_nd(ws[i], gw[i], ms[i], vs[i], "adamw")
    return (loss, grad_x, *gw, *[u[0] for u in upd], *[u[1] for u in upd], *[u[2] for u in upd])
```

```python
import functools
import math

import jax
import jax.numpy as jnp
from jax import lax
from jax.experimental import pallas as pl
from jax.experimental.pallas import tpu as pltpu

f32, bf16 = jnp.float32, jnp.bfloat16
SDS = jax.ShapeDtypeStruct

T, D, F = 4096, 1024, 2816
NDEV, DEPTH = 8, 2
HW = 512
INW = 5 * HW
FS, INS, OUTS, ADAS = F // NDEV, INW // NDEV, D // NDEV, 9 * D // NDEV
CH = 128
PATTERN_DILATIONS = (1, 4, 16)
ROPE_THETA = 10000.0
EPS = 1e-6
LR, B1, B2, AEPS, WD, STEP = 0.001, 0.9, 0.999, 1e-08, 0.01, 10
MESH = pl.DeviceIdType.MESH


def _cp(sems, vmem_mb):
    return pltpu.CompilerParams(dimension_semantics=sems, vmem_limit_bytes=vmem_mb << 20)


def _pallas_hbm(body, *, out_shape, **kw):
    typed = jax.tree.map(lambda s: pltpu.HBM(s.shape, s.dtype), out_shape)
    call = pl.pallas_call(body, out_shape=typed, **kw)

    def run(*operands):
        pin = lambda x: x if x.dtype == jnp.int32 else pltpu.with_memory_space_constraint(x, pltpu.HBM)
        return call(*[pin(x) for x in operands])

    return run


def _nn(a, b):
    return lax.dot_general(a, b, (((1,), (0,)), ((), ())), preferred_element_type=f32)


def _nt(a, b):
    return lax.dot_general(a, b, (((1,), (1,)), ((), ())), preferred_element_type=f32)


def _tn(a, b):
    return lax.dot_general(a, b, (((0,), (0,)), ((), ())), preferred_element_type=f32)


def _colsum(a):
    return jnp.sum(a, axis=0, keepdims=True)


def _rowmean(a):
    return jnp.mean(a, axis=-1, keepdims=True)


def _norm_mod(x, par):
    rstd = lax.rsqrt(_rowmean(x * x) + EPS)
    xhat = x * rstd
    n = xhat * par[3:4, :]
    y = n * (1.0 + par[1:2, :]) + par[0:1, :]
    return y, n, xhat, rstd


def _norm_mod_bwd(dy, x, par):
    _, n, xhat, rstd = _norm_mod(x, par)
    dn = dy * (1.0 + par[1:2, :])
    dxhat = dn * par[3:4, :]
    dx = rstd * (dxhat - xhat * _rowmean(dxhat * xhat))
    return dx, _colsum(dy), _colsum(dy * n), _colsum(dn * xhat)


_GK = math.sqrt(2.0 / math.pi)


def _gelu(x):
    return 0.5 * x * (1.0 + jnp.tanh(_GK * (x + 0.044715 * x * x * x)))


def _gelu_grad(x):
    t = jnp.tanh(_GK * (x + 0.044715 * x * x * x))
    return 0.5 * (1.0 + t) + 0.5 * x * (1.0 - t * t) * (_GK * (1.0 + 3.0 * 0.044715 * x * x))


def _rows(ref, idx, r):
    if len(ref.shape) == 3:
        return ref.at[:, pl.ds(idx * r, r), :]
    return ref.at[pl.ds(idx * r, r), :]


def _flip(v, bit):
    return 1 - v if bit else v


def _all_gather_small(x, name):
    R, C = x.shape

    def body(x_ref, out_ref, send_sems, recv_sems):
        mx, my, mc = lax.axis_index("x"), lax.axis_index("y"), lax.axis_index("c")
        me = 4 * mx + 2 * my + mc
        out_ref[me] = x_ref[...]
        sent = []
        for k in range(1, NDEV):
            peer = (_flip(mx, k & 4), _flip(my, k & 2), _flip(mc, k & 1))
            cp = pltpu.make_async_remote_copy(
                src_ref=x_ref, dst_ref=out_ref.at[me], send_sem=send_sems.at[k - 1],
                recv_sem=recv_sems.at[k - 1], device_id=peer, device_id_type=MESH)
            cp.start()
            sent.append(cp)
        for k in range(1, NDEV):
            peer = (_flip(mx, k & 4), _flip(my, k & 2), _flip(mc, k & 1))
            pidx = 4 * peer[0] + 2 * peer[1] + peer[2]
            pltpu.make_async_remote_copy(
                src_ref=x_ref, dst_ref=out_ref.at[pidx], send_sem=send_sems.at[k - 1],
                recv_sem=recv_sems.at[k - 1], device_id=peer, device_id_type=MESH).wait_recv()
        for cp in sent:
            cp.wait_send()

    vm = pl.BlockSpec(memory_space=pltpu.VMEM)
    return pl.pallas_call(
        body, name=name, out_shape=SDS((NDEV, R, C), f32), in_specs=[vm], out_specs=vm,
        scratch_shapes=[pltpu.SemaphoreType.DMA((NDEV - 1,)), pltpu.SemaphoreType.DMA((NDEV - 1,))],
        compiler_params=pltpu.CompilerParams(vmem_limit_bytes=32 << 20),
    )(x)


def _all_gather_rows(locs, name):
    n = len(locs)
    rs = [a.shape[-2] for a in locs]

    def body(*refs):
        src, out = refs[:n], refs[n:2 * n]
        send_sems, recv_sems, loc_sems = refs[2 * n:]
        mx, my, mc = lax.axis_index("x"), lax.axis_index("y"), lax.axis_index("c")
        me, sib = (mx, my, mc), (mx, my, 1 - mc)
        chips = [(1 - mx, my), (mx, 1 - my), (1 - mx, 1 - my)]

        def blk(a, p):
            return _rows(out[a], 4 * p[0] + 2 * p[1] + p[2], rs[a])

        def copy(k, a, block, to, from_src=False):
            return pltpu.make_async_remote_copy(
                src_ref=src[a] if from_src else blk(a, block), dst_ref=blk(a, block),
                send_sem=send_sems.at[k * n + a], recv_sem=recv_sems.at[k * n + a],
                device_id=to, device_id_type=MESH)

        mine = [pltpu.make_async_copy(src[a], blk(a, me), loc_sems.at[a]) for a in range(n)]
        for m in mine:
            m.start()
        first = []
        for j, chip in enumerate(chips):
            first += [copy(1 + j, a, me, (*chip, mc), True) for a in range(n)]
        first += [copy(0, a, me, sib, True) for a in range(n)]
        for cp in first:
            cp.start()
        passed = []
        for j, chip in enumerate(chips):
            for a in range(n):
                copy(1 + j, a, (*chip, mc), me).wait_recv()
            fwd = [copy(4 + j, a, (*chip, mc), sib) for a in range(n)]
            for cp in fwd:
                cp.start()
            passed += fwd
        for a in range(n):
            copy(0, a, sib, me).wait_recv()
        for j, chip in enumerate(chips):
            for a in range(n):
                copy(4 + j, a, (*chip, 1 - mc), me).wait_recv()
        for cp in first + passed:
            cp.wait_send()
        for m in mine:
            m.wait()

    hbm = pl.BlockSpec(memory_space=pl.ANY)
    out_shape = [SDS(a.shape[:-2] + (NDEV * a.shape[-2], a.shape[-1]), a.dtype) for a in locs]
    return pl.pallas_call(
        body, name=name, out_shape=out_shape, in_specs=[hbm] * n, out_specs=[hbm] * n,
        scratch_shapes=[pltpu.SemaphoreType.DMA((7 * n,)), pltpu.SemaphoreType.DMA((7 * n,)),
                        pltpu.SemaphoreType.DMA((n,))],
    )(*locs)


HBM_SPEC = pl.BlockSpec(memory_space=pltpu.HBM)
SEM_SPEC = pl.BlockSpec(memory_space=pltpu.SEMAPHORE)
DATAFLOW_EFFECT = pltpu.SideEffectType.DATAFLOW_SIDE_EFFECTING


def _place_own(loc, me, name="place_own"):
    r, cols = loc.shape[-2:]
    loc3 = loc.reshape(-1, r, cols)
    kk = loc3.shape[0]

    def body(me_ref, src_ref, full_ref, out_ref):
        out_ref[...] = src_ref[...]

    out = _pallas_hbm(
        body, name=name, out_shape=SDS((kk, NDEV * r, cols), loc.dtype),
        grid_spec=pltpu.PrefetchScalarGridSpec(
            num_scalar_prefetch=1, grid=(kk,),
            in_specs=[pl.BlockSpec((None, r, cols), lambda i, me_ref: (i, 0, 0)), pl.BlockSpec(memory_space=pl.ANY)],
            out_specs=pl.BlockSpec((None, r, cols), lambda i, me_ref: (i, me_ref[0], 0))),
        input_output_aliases={2: 0}, compiler_params=_cp(("arbitrary",), 32),
    )(me.reshape(1), loc3, lax.empty((kk, NDEV * r, cols), loc.dtype))
    return out.reshape(loc.shape[:-2] + (NDEV * r, cols))


def _exchange_copies(scatter, bufs, n, send_sems, recv_sems):
    mx, my, mc = lax.axis_index("x"), lax.axis_index("y"), lax.axis_index("c")
    me = 4 * mx + 2 * my + mc
    out = []
    for k in range(1, NDEV):
        peer = (_flip(mx, k & 4), _flip(my, k & 2), _flip(mc, k & 1))
        pidx = 4 * peer[0] + 2 * peer[1] + peer[2]
        for a in range(n):
            r = bufs[a].shape[-2] // NDEV
            if scatter:
                src, dst, arrive = _rows(bufs[a], pidx, r), bufs[n + a].at[me], bufs[n + a].at[pidx]
            else:
                src, dst, arrive = _rows(bufs[a], me, r), _rows(bufs[a], me, r), _rows(bufs[a], pidx, r)
            sems = dict(send_sem=send_sems.at[(k - 1) * n + a], recv_sem=recv_sems.at[(k - 1) * n + a],
                        device_id=peer, device_id_type=MESH)
            out.append((pltpu.make_async_remote_copy(src_ref=src, dst_ref=dst, **sems),
                        pltpu.make_async_remote_copy(src_ref=src, dst_ref=arrive, **sems)))
    return out


def _exchange_start(scatter, arrays, name):
    m = len(arrays)
    n = m // 2 if scatter else m

    def body(*refs):
        send_sems, recv_sems, token = refs[m], refs[m + 1], refs[-1]
        for go, _ in _exchange_copies(scatter, refs[:m], n, send_sems, recv_sems):
            go.start()
        token[...] = jnp.zeros_like(token)

    sems = pltpu.SemaphoreType.DMA((7 * n,))
    return pl.pallas_call(
        body, name=name, in_specs=[HBM_SPEC] * m,
        out_shape=(sems, sems, *[pltpu.HBM(a.shape, a.dtype) for a in arrays], SDS((8, CH), f32)),
        out_specs=(SEM_SPEC, SEM_SPEC, *[HBM_SPEC] * m, pl.BlockSpec(memory_space=pltpu.VMEM)),
        input_output_aliases={a: 2 + a for a in range(m)},
        compiler_params=pltpu.CompilerParams(has_side_effects=DATAFLOW_EFFECT),
    )(*[pltpu.with_memory_space_constraint(a, pltpu.HBM) for a in arrays])


def _exchange_wait(scatter, started, after, name):
    send_sems, recv_sems, *arrays = started
    m = len(arrays)
    n = m // 2 if scatter else m

    def body(*refs):
        for go, arrive in _exchange_copies(scatter, refs[:m], n, refs[m], refs[m + 1]):
            go.wait_send()
            arrive.wait_recv()

    return pl.pallas_call(
        body, name=name, in_specs=[HBM_SPEC] * m + [SEM_SPEC, SEM_SPEC, pl.BlockSpec(memory_space=pl.ANY)],
        out_shape=[pltpu.HBM(a.shape, a.dtype) for a in arrays], out_specs=[HBM_SPEC] * m,
        input_output_aliases={a: a for a in range(m)},
        compiler_params=pltpu.CompilerParams(has_side_effects=DATAFLOW_EFFECT),
    )(*arrays, send_sems, recv_sems, after)


def _mod_fwd(c_all, ada_w, ada_b_loc):
    def body(c_ref, w_ref, b_ref, o_ref):
        ca = c_ref[...]
        ca = ca * jax.nn.sigmoid(ca)
        o_ref[...] = jnp.dot(ca, w_ref[...], precision=lax.Precision.HIGHEST,
                             preferred_element_type=f32) + b_ref[...]

    return _pallas_hbm(
        body, name="mod_fwd", grid=(DEPTH,), out_shape=SDS((DEPTH, NDEV, ADAS), f32),
        in_specs=[pl.BlockSpec((NDEV, D), lambda l: (0, 0)),
                  pl.BlockSpec((None, D, ADAS), lambda l: (l, 0, 0)),
                  pl.BlockSpec((None, 1, ADAS), lambda l: (l, 0, 0))],
        out_specs=pl.BlockSpec((None, NDEV, ADAS), lambda l: (l, 0, 0)),
        compiler_params=_cp(("arbitrary",), 32),
    )(c_all, ada_w, ada_b_loc)


def _ada_grad(cact_t, dmod_cols):
    def body(c_ref, d_ref, o_ref):
        acc = c_ref[:, 0:1] * d_ref[0:1, :]
        for b in range(1, NDEV):
            acc = acc + c_ref[:, b:b + 1] * d_ref[b:b + 1, :]
        o_ref[...] = acc

    tr = 256
    return _pallas_hbm(
        body, name="ada_grad", grid=(DEPTH, D // tr), out_shape=SDS((DEPTH, D, ADAS), f32),
        in_specs=[pl.BlockSpec((tr, NDEV), lambda l, i: (i, 0)),
                  pl.BlockSpec((None, NDEV, ADAS), lambda l, i: (l, 0, 0))],
        out_specs=pl.BlockSpec((None, tr, ADAS), lambda l, i: (l, i, 0)),
        compiler_params=_cp(("arbitrary", "arbitrary"), 32),
    )(cact_t, dmod_cols)


def _ffn_fwd(h, par, w3):
    tm, tf = 1024, 256
    nj = F // tf

    def body(h_ref, par_ref, wg_ref, wu_ref, wd_ref, ho_ref, fo_ref, g_ref, u_ref, a_ref, y_scr, acc):
        j = pl.program_id(1)

        @pl.when(j == 0)
        def _():
            y_scr[...] = _norm_mod(h_ref[...], par_ref[...])[0].astype(bf16)
            acc[...] = jnp.zeros_like(acc)

        y = y_scr[...]
        g = _nt(y, wg_ref[...])
        u = _nt(y, wu_ref[...])
        a = ((g * jax.nn.sigmoid(g)) * u).astype(bf16)
        g_ref[...] = g.astype(bf16)
        u_ref[...] = u.astype(bf16)
        a_ref[...] = a
        acc[...] += _nn(a, wd_ref[...])

        @pl.when(j == nj - 1)
        def _():
            fo_ref[...] = acc[...].astype(bf16)
            ho_ref[...] = h_ref[...] + (0.5 * par_ref[2:3, :]) * acc[...]

    row = pl.BlockSpec((tm, D), lambda i, j: (i, 0))
    hid = pl.BlockSpec((tm, tf), lambda i, j: (i, j))
    wspec = [pl.BlockSpec((None, tf, D), lambda i, j, k=k: (k, j, 0)) for k in range(3)]
    return _pallas_hbm(
        body, name="ffn_fwd", grid=(T // tm, nj),
        in_specs=[row, pl.BlockSpec((8, D), lambda i, j: (0, 0))] + wspec,
        out_specs=[row, row, hid, hid, hid],
        out_shape=[SDS((T, D), f32), SDS((T, D), bf16), SDS((T, F), bf16), SDS((T, F), bf16), SDS((T, F), bf16)],
        scratch_shapes=[pltpu.VMEM((tm, D), bf16), pltpu.VMEM((tm, D), f32)],
        compiler_params=_cp(("arbitrary", "arbitrary"), 52),
    )(h, par, w3, w3, w3)


def _ffn_bwd_tok(dh, h, par, fo, gs, us, w3):
    tm, tf = 512, 256
    nj = F // tf

    def body(dh_ref, h_ref, par_ref, fo_ref, g_ref, u_ref, wg_ref, wu_ref, wd_ref,
             dhin_ref, dg_ref, du_ref, y_ref, dfb_ref, dpar_ref, df_scr, dyacc):
        i, j = pl.program_id(0), pl.program_id(1)

        @pl.when(jnp.logical_and(i == 0, j == 0))
        def _():
            dpar_ref[...] = jnp.zeros_like(dpar_ref)

        @pl.when(j == 0)
        def _():
            dh_v, par_v = dh_ref[...], par_ref[...]
            dfb = ((0.5 * par_v[2:3, :]) * dh_v).astype(bf16)
            df_scr[...] = dfb
            dfb_ref[...] = dfb
            dpar_ref[2:3, :] += 0.5 * _colsum(dh_v * fo_ref[...].astype(f32))
            y_ref[...] = _norm_mod(h_ref[...], par_v)[0].astype(bf16)
            dyacc[...] = jnp.zeros_like(dyacc)

        for half in range(2):
            rs = slice(half * (tm // 2), (half + 1) * (tm // 2))
            da = _nt(df_scr[rs, :], wd_ref[...])
            g = g_ref[rs, :].astype(f32)
            u = u_ref[rs, :].astype(f32)
            sig = jax.nn.sigmoid(g)
            du = (da * (g * sig)).astype(bf16)
            dg = (da * u * (sig * (1.0 + g * (1.0 - sig)))).astype(bf16)
            dg_ref[rs, :] = dg
            du_ref[rs, :] = du
            dyacc[rs, :] += _nn(dg, wg_ref[...]) + _nn(du, wu_ref[...])

        @pl.when(j == nj - 1)
        def _():
            dx, dsh, dsc, dng = _norm_mod_bwd(dyacc[...], h_ref[...], par_ref[...])
            dhin_ref[...] = dh_ref[...] + dx
            dpar_ref[0:1, :] += dsh
            dpar_ref[1:2, :] += dsc
            dpar_ref[3:4, :] += dng

    row = pl.BlockSpec((tm, D), lambda i, j: (i, 0))
    hid = pl.BlockSpec((tm, tf), lambda i, j: (i, j))
    one = pl.BlockSpec((8, D), lambda i, j: (0, 0))
    wspec = [pl.BlockSpec((None, tf, D), lambda i, j, k=k: (k, j, 0)) for k in range(3)]
    return _pallas_hbm(
        body, name="ffn_bwd_tok", grid=(T // tm, nj),
        in_specs=[row, row, one, row, hid, hid] + wspec,
        out_specs=[row, hid, hid, row, row, one],
        out_shape=[SDS((T, D), f32), SDS((T, F), bf16), SDS((T, F), bf16), SDS((T, D), bf16),
                   SDS((T, D), bf16), SDS((8, D), f32)],
        scratch_shapes=[pltpu.VMEM((tm, D), bf16), pltpu.VMEM((tm, D), f32)],
        compiler_params=_cp(("arbitrary", "arbitrary"), 48),
    )(dh, h, par, fo, gs, us, w3, w3, w3)


def _ffn_bwd_w(a_s, dg_s, du_s, y, dfb, anchor=None):
    tf, tk = F // 2, 256
    nk = T // tk

    def body(a_ref, dg_ref, du_ref, y_ref, df_ref, *rest):
        o_ref, accg, accu, accd = rest[-4:]
        kk = pl.program_id(1)

        @pl.when(kk == 0)
        def _():
            accg[...] = jnp.zeros_like(accg)
            accu[...] = jnp.zeros_like(accu)
            accd[...] = jnp.zeros_like(accd)

        yv = y_ref[...]
        accg[...] += _tn(dg_ref[...], yv)
        accu[...] += _tn(du_ref[...], yv)
        accd[...] += _tn(a_ref[...], df_ref[...])

        @pl.when(kk == nk - 1)
        def _():
            o_ref[0] = accg[...].astype(bf16)
            o_ref[1] = accu[...].astype(bf16)
            o_ref[2] = accd[...].astype(bf16)

    hid = pl.BlockSpec((tk, tf), lambda j, kk: (kk, j))
    row = pl.BlockSpec((tk, D), lambda j, kk: (kk, 0))
    extra = [] if anchor is None else [anchor]
    return _pallas_hbm(
        body, name="ffn_bwd_w", grid=(F // tf, nk),
        in_specs=[hid, hid, hid, row, row] + [pl.BlockSpec((8, CH), lambda j, kk: (0, 0))] * len(extra),
        out_specs=pl.BlockSpec((3, tf, D), lambda j, kk: (0, j, 0)), out_shape=SDS((3, F, D), bf16),
        scratch_shapes=[pltpu.VMEM((tf, D), f32)] * 3,
        compiler_params=_cp(("arbitrary", "arbitrary"), 48),
    )(a_s, dg_s, du_s, y, dfb, *extra)


def _tn_matmul(a, b, bm, name):
    M, N = a.shape[1], b.shape[1]
    tk = 512
    nk = T // tk

    def body(a_ref, b_ref, o_ref, acc):
        kk = pl.program_id(1)

        @pl.when(kk == 0)
        def _():
            acc[...] = jnp.zeros_like(acc)

        acc[...] += _tn(a_ref[...], b_ref[...])

        @pl.when(kk == nk - 1)
        def _():
            o_ref[...] = acc[...].astype(bf16)

    return _pallas_hbm(
        body, name=name, grid=(M // bm, nk),
        in_specs=[pl.BlockSpec((tk, bm), lambda i, kk: (kk, i)), pl.BlockSpec((tk, N), lambda i, kk: (kk, 0))],
        out_specs=pl.BlockSpec((bm, N), lambda i, kk: (i, 0)), out_shape=SDS((M, N), bf16),
        scratch_shapes=[pltpu.VMEM((bm, N), f32)],
        compiler_params=_cp(("arbitrary", "arbitrary"), 40),
    )(a, b)


def _rope_tables():
    inv = ROPE_THETA ** (-jnp.arange(0, 64, 2, dtype=f32) / 64)
    ang = jnp.arange(T, dtype=f32)[:, None] * inv[None, :]
    ang = jnp.concatenate([ang, ang], axis=-1)
    cos, sin = jnp.tile(jnp.cos(ang), (1, 8)), jnp.tile(jnp.sin(ang), (1, 8))
    low = (jnp.arange(HW) % 64 < 32)[None, :]
    return cos, jnp.where(low, -sin, 0.0), jnp.where(low, 0.0, sin)


def _rope(t, cos, sin_lo, sin_hi):
    return t * cos + pltpu.roll(t, HW - 32, 1) * sin_lo + pltpu.roll(t, 32, 1) * sin_hi


def _rope_t(g, cos, sin_lo, sin_hi):
    return g * cos + pltpu.roll(g * sin_lo, 32, 1) + pltpu.roll(g * sin_hi, HW - 32, 1)


def _inproj_fwd(h, par, w_in_t, tabs):
    tm = 512

    def body(h_ref, par_ref, w_ref, cos_ref, slo_ref, shi_ref, ua_ref, va_ref, q_ref, k_ref, v_ref):
        y = _norm_mod(h_ref[...], par_ref[...])[0].astype(bf16)
        proj = lambda c: _nt(y, w_ref[c * HW:(c + 1) * HW, :])
        ua_ref[...] = proj(0)
        va_ref[...] = proj(1)
        cos, slo, shi = cos_ref[...], slo_ref[...], shi_ref[...]
        q_ref[...] = (_rope(proj(2), cos, slo, shi) * 0.125).astype(bf16)
        k_ref[...] = _rope(proj(3), cos, slo, shi).astype(bf16)
        v_ref[...] = proj(4).astype(bf16)

    row = pl.BlockSpec((tm, D), lambda i: (i, 0))
    half = pl.BlockSpec((tm, HW), lambda i: (i, 0))
    return _pallas_hbm(
        body, name="inproj_fwd", grid=(T // tm,),
        in_specs=[row, pl.BlockSpec((8, D), lambda i: (0, 0)), pl.BlockSpec((INW, D), lambda i: (0, 0)), half, half, half],
        out_specs=[half] * 5,
        out_shape=[SDS((T, HW), f32), SDS((T, HW), f32), SDS((T, HW), bf16), SDS((T, HW), bf16), SDS((T, HW), bf16)],
        compiler_params=_cp(("arbitrary",), 48),
    )(h, par, w_in_t, *tabs)


def _head_masks():
    lane = lax.broadcasted_iota(jnp.int32, (1, CH), 1)
    return lane < 64, lane >= 64


def _stack_heads(x):
    lo, hi = _head_masks()
    zero = jnp.zeros_like(x)
    return jnp.concatenate([jnp.where(lo, x, zero), jnp.where(hi, x, zero)], axis=0)


def _band_mask(has_prev):
    row = lax.broadcasted_iota(jnp.int32, (2 * CH, 2 * CH), 0) & (CH - 1)
    col = lax.broadcasted_iota(jnp.int32, (2 * CH, 2 * CH), 1)
    in_prev = jnp.logical_and(jnp.logical_and(col < CH, col >= row), has_prev)
    return jnp.logical_or(in_prev, jnp.logical_and(col >= CH, col - CH <= row))


def _attn_fwd(q, k, v, d, anchor=None):
    L = T // d
    nb = L // CH

    def body(q_ref, kc_ref, kp_ref, vc_ref, vp_ref, *rest):
        o_ref, lse_ref = rest[-2:]
        n = pl.program_id(1)
        col = lax.broadcasted_iota(jnp.int32, (CH, CH), 1)
        mask = _band_mask(n > 0)
        lo, _ = _head_masks()
        lse_tile = jnp.zeros((CH, CH), f32)
        for hp in range(4):
            sl = slice(hp * CH, (hp + 1) * CH)
            kk = jnp.concatenate([kp_ref[:, sl], kc_ref[:, sl]], axis=0)
            vv = jnp.concatenate([vp_ref[:, sl], vc_ref[:, sl]], axis=0)
            s = jnp.where(mask, _nt(_stack_heads(q_ref[:, sl]), kk), -jnp.inf)
            m = jnp.max(s, axis=-1, keepdims=True)
            p = jnp.exp(s - m)
            den = jnp.sum(p, axis=-1, keepdims=True)
            o = _nn(p.astype(bf16), vv) / den
            o_ref[:, sl] = jnp.where(lo, o[:CH], o[CH:]).astype(bf16)
            lse = m + jnp.log(den)
            lse_tile = jnp.where(col == 2 * hp, lse[:CH], jnp.where(col == 2 * hp + 1, lse[CH:], lse_tile))
        lse_ref[...] = lse_tile

    cur = pl.BlockSpec((CH, HW), lambda r, n: (n, r))
    prev = pl.BlockSpec((CH, HW), lambda r, n: (jnp.maximum(n - 1, 0), r))
    q2, k2, v2 = (t.reshape(L, d * HW) for t in (q, k, v))
    extra = [] if anchor is None else [anchor]
    o, lse = _pallas_hbm(
        body, name="attn_fwd_d%d" % d, grid=(d, nb),
        in_specs=[cur, cur, prev, cur, prev] + [pl.BlockSpec((8, CH), lambda r, n: (0, 0))] * len(extra),
        out_specs=[cur, pl.BlockSpec((CH, CH), lambda r, n: (n, r))],
        out_shape=[SDS((L, d * HW), bf16), SDS((L, d * CH), f32)],
        compiler_params=_cp(("arbitrary", "arbitrary"), 32),
    )(q2, k2, k2, v2, v2, *extra)
    return o.reshape(T, HW), lse.reshape(T, CH)


def _attn_combine(os_, lses):
    tm = 512

    def body(o0_ref, o1_ref, o2_ref, l0_ref, l1_ref, l2_ref, o_ref, lse_ref):
        l0, l1, l2 = l0_ref[...], l1_ref[...], l2_ref[...]
        m = jnp.maximum(jnp.maximum(l0, l1), l2)
        e = [jnp.exp(l0 - m), jnp.exp(l1 - m), jnp.exp(l2 - m)]
        s = e[0] + e[1] + e[2]
        w = [ei / s for ei in e]
        lse_ref[...] = m + jnp.log(s)
        lo, _ = _head_masks()
        for hp in range(4):
            sl = slice(hp * CH, (hp + 1) * CH)
            acc = jnp.zeros((tm, CH), f32)
            for wp, op_ref in zip(w, (o0_ref, o1_ref, o2_ref)):
                wexp = jnp.where(lo, wp[:, 2 * hp:2 * hp + 1], wp[:, 2 * hp + 1:2 * hp + 2])
                acc = acc + wexp * op_ref[:, sl].astype(f32)
            o_ref[:, sl] = acc.astype(bf16)

    half = pl.BlockSpec((tm, HW), lambda i: (i, 0))
    stat = pl.BlockSpec((tm, CH), lambda i: (i, 0))
    return _pallas_hbm(
        body, name="attn_combine", grid=(T // tm,), in_specs=[half] * 3 + [stat] * 3, out_specs=[half, stat],
        out_shape=[SDS((T, HW), bf16), SDS((T, CH), f32)],
        compiler_params=_cp(("arbitrary",), 32),
    )(*os_, *lses)


def _attn_bwd(q, k, v, o, do, lse, d):
    L = T // d
    nb = L // CH

    def body(qc_ref, qn_ref, kc_ref, kp_ref, vc_ref, vp_ref, oc_ref, on_ref, dc_ref, dn_ref, lc_ref, ln_ref,
             dq_ref, dk_ref, dv_ref):
        n = pl.program_id(1)
        mask_c = _band_mask(n > 0)
        mask_n = _band_mask(n < nb - 1)[:, :CH]
        lo, _ = _head_masks()
        lse_c, lse_n = lc_ref[...], ln_ref[...]
        for hp in range(4):
            sl = slice(hp * CH, (hp + 1) * CH)
            kc, vc = kc_ref[:, sl], vc_ref[:, sl]
            kk = jnp.concatenate([kp_ref[:, sl], kc], axis=0)
            vv = jnp.concatenate([vp_ref[:, sl], vc], axis=0)
            doc, don = dc_ref[:, sl], dn_ref[:, sl]
            qs_c, qs_n, ds_c, ds_n = _stack_heads(qc_ref[:, sl]), _stack_heads(qn_ref[:, sl]), _stack_heads(doc), _stack_heads(don)
            delta_c = jnp.sum(_stack_heads(doc.astype(f32) * oc_ref[:, sl].astype(f32)), axis=-1, keepdims=True)
            delta_n = jnp.sum(_stack_heads(don.astype(f32) * on_ref[:, sl].astype(f32)), axis=-1, keepdims=True)
            heads = lambda t: jnp.concatenate([t[:, 2 * hp:2 * hp + 1], t[:, 2 * hp + 1:2 * hp + 2]], axis=0)
            p1 = jnp.where(mask_c, jnp.exp(_nt(qs_c, kk) - heads(lse_c)), 0.0)
            g1 = (p1 * (_nt(ds_c, vv) - delta_c)).astype(bf16)
            dq = _nn(g1, kk)
            dq_ref[:, sl] = jnp.where(lo, dq[:CH], dq[CH:]).astype(bf16)
            p2 = jnp.where(mask_n, jnp.exp(_nt(qs_n, kc) - heads(lse_n)), 0.0)
            g2 = (p2 * (_nt(ds_n, vc) - delta_n)).astype(bf16)
            dk_ref[:, sl] = _tn(jnp.concatenate([g1[:, CH:], g2], axis=0),
                                jnp.concatenate([qs_c, qs_n], axis=0)).astype(bf16)
            dv_ref[:, sl] = _tn(jnp.concatenate([p1[:, CH:].astype(bf16), p2.astype(bf16)], axis=0),
                                jnp.concatenate([ds_c, ds_n], axis=0)).astype(bf16)

    cur = pl.BlockSpec((CH, HW), lambda r, n: (n, r))
    prev = pl.BlockSpec((CH, HW), lambda r, n: (jnp.maximum(n - 1, 0), r))
    nxt = pl.BlockSpec((CH, HW), lambda r, n: (jnp.minimum(n + 1, nb - 1), r))
    scur = pl.BlockSpec((CH, CH), lambda r, n: (n, r))
    snxt = pl.BlockSpec((CH, CH), lambda r, n: (jnp.minimum(n + 1, nb - 1), r))
    q2, k2, v2, o2, do2 = (t.reshape(L, d * HW) for t in (q, k, v, o, do))
    l2 = lse.reshape(L, d * CH)
    outs = _pallas_hbm(
        body, name="attn_bwd_d%d" % d, grid=(d, nb),
        in_specs=[cur, nxt, cur, prev, cur, prev, cur, nxt, cur, nxt, scur, snxt],
        out_specs=[cur] * 3, out_shape=[SDS((L, d * HW), bf16)] * 3,
        compiler_params=_cp(("arbitrary", "arbitrary"), 32),
    )(q2, q2, k2, k2, v2, v2, o2, o2, do2, do2, l2, l2)
    return [t.reshape(T, HW) for t in outs]


def _sgu_fwd(ua, va, o, lng, lnb, ws, bs_t):
    tm = 512

    def body(ua_ref, va_ref, o_ref, lg_ref, lb_ref, ws_ref, bs_ref, mix_ref):
        for hd in range(4):
            sl = slice(hd * CH, (hd + 1) * CH)
            w = ws_ref[hd]
            for cc in range(tm // CH):
                rs = slice(cc * CH, (cc + 1) * CH)
                u = _gelu(ua_ref[rs, sl])
                v = _gelu(va_ref[rs, sl])
                vc = v - _rowmean(v)
                vn = vc * lax.rsqrt(_rowmean(vc * vc) + EPS) * lg_ref[:, sl] + lb_ref[:, sl]
                z = _nn(w, vn.astype(bf16)) + bs_ref[:, hd:hd + 1]
                mix_ref[rs, sl] = (u * z).astype(bf16)
        mix_ref[:, HW:] = o_ref[...]

    half = pl.BlockSpec((tm, HW), lambda i: (i, 0))
    vec = pl.BlockSpec((1, HW), lambda i: (0, 0))
    return _pallas_hbm(
        body, name="sgu_fwd", grid=(T // tm,),
        in_specs=[half, half, half, vec, vec, pl.BlockSpec((4, CH, CH), lambda i: (0, 0, 0)),
                  pl.BlockSpec((CH, 4), lambda i: (0, 0))],
        out_specs=pl.BlockSpec((tm, 2 * HW), lambda i: (i, 0)), out_shape=SDS((T, 2 * HW), bf16),
        compiler_params=_cp(("arbitrary",), 32),
    )(ua, va, o, lng, lnb, ws, bs_t)


def _sgu_bwd(ua, va, d_a, lng, lnb, ws, ws_t, bs_t):
    tm = 512

    def body(ua_ref, va_ref, d_ref, lg_ref, lb_ref, ws_ref, wst_ref, bs_ref, dsg_ref, dln_ref, dws_ref, db_ref):
        @pl.when(pl.program_id(0) == 0)
        def _():
            dln_ref[...] = jnp.zeros_like(dln_ref)
            dws_ref[...] = jnp.zeros_like(dws_ref)
            db_ref[...] = jnp.zeros_like(db_ref)

        for hd in range(4):
            sl = slice(hd * CH, (hd + 1) * CH)
            w, wt = ws_ref[hd], wst_ref[hd]
            lg = lg_ref[:, sl]
            for cc in range(tm // CH):
                rs = slice(cc * CH, (cc + 1) * CH)
                xa, xv, dd = ua_ref[rs, sl], va_ref[rs, sl], d_ref[rs, sl]
                u, v = _gelu(xa), _gelu(xv)
                vc = v - _rowmean(v)
                rstd = lax.rsqrt(_rowmean(vc * vc) + EPS)
                xh = vc * rstd
                vnb = (xh * lg + lb_ref[:, sl]).astype(bf16)
                z = _nn(w, vnb) + bs_ref[:, hd:hd + 1]
                dz = dd * u
                dzb = dz.astype(bf16)
                dsg_ref[rs, sl] = (dd * z * _gelu_grad(xa)).astype(bf16)
                dws_ref[hd] += _nt(dzb, vnb)
                db_ref[hd] += dz
                dvn = _nn(wt, dzb)
                dln_ref[0:1, sl] += _colsum(dvn * xh)
                dln_ref[1:2, sl] += _colsum(dvn)
                dxh = dvn * lg
                dv = rstd * (dxh - _rowmean(dxh) - xh * _rowmean(dxh * xh))
                dsg_ref[rs, HW + hd * CH:HW + (hd + 1) * CH] = (dv * _gelu_grad(xv)).astype(bf16)

    half = pl.BlockSpec((tm, HW), lambda i: (i, 0))
    vec = pl.BlockSpec((1, HW), lambda i: (0, 0))
    mat = pl.BlockSpec((4, CH, CH), lambda i: (0, 0, 0))
    return _pallas_hbm(
        body, name="sgu_bwd", grid=(T // tm,),
        in_specs=[half, half, half, vec, vec, mat, mat, pl.BlockSpec((CH, 4), lambda i: (0, 0))],
        out_specs=[pl.BlockSpec((tm, 2 * HW), lambda i: (i, 0)), pl.BlockSpec((8, HW), lambda i: (0, 0)), mat, mat],
        out_shape=[SDS((T, 2 * HW), bf16), SDS((8, HW), f32), SDS((4, CH, CH), f32), SDS((4, CH, CH), f32)],
        compiler_params=_cp(("arbitrary",), 32),
    )(ua, va, d_a, lng, lnb, ws, ws_t, bs_t)


def _outproj_fwd(mixed, w_out, h, par):
    tm = 512

    def body(mix_ref, w_ref, h_ref, par_ref, ho_ref, po_ref):
        p = _nn(mix_ref[...], w_ref[...])
        po_ref[...] = p.astype(bf16)
        ho_ref[...] = h_ref[...] + par_ref[2:3, :] * p

    row = pl.BlockSpec((tm, D), lambda i: (i, 0))
    return _pallas_hbm(
        body, name="outproj_fwd", grid=(T // tm,),
        in_specs=[row, pl.BlockSpec((D, D), lambda i: (0, 0)), row, pl.BlockSpec((8, D), lambda i: (0, 0))],
        out_specs=[row, row], out_shape=[SDS((T, D), f32), SDS((T, D), bf16)],
        compiler_params=_cp(("arbitrary",), 32),
    )(mixed, w_out, h, par)


def _outproj_bwd(dh, po, w_out, par):
    tm = 512

    def body(dh_ref, po_ref, w_ref, par_ref, do_ref, da_ref, db_ref, dpar_ref):
        @pl.when(pl.program_id(0) == 0)
        def _():
            dpar_ref[...] = jnp.zeros_like(dpar_ref)

        dh_v = dh_ref[...]
        dob = (par_ref[2:3, :] * dh_v).astype(bf16)
        do_ref[...] = dob
        dpar_ref[2:3, :] += _colsum(dh_v * po_ref[...].astype(f32))
        dm = _nt(dob, w_ref[...])
        da_ref[...] = dm[:, :HW]
        db_ref[...] = dm[:, HW:].astype(bf16)

    row = pl.BlockSpec((tm, D), lambda i: (i, 0))
    half = pl.BlockSpec((tm, HW), lambda i: (i, 0))
    one = pl.BlockSpec((8, D), lambda i: (0, 0))
    return _pallas_hbm(
        body, name="outproj_bwd", grid=(T // tm,),
        in_specs=[row, row, pl.BlockSpec((D, D), lambda i: (0, 0)), one],
        out_specs=[row, half, half, one],
        out_shape=[SDS((T, D), bf16), SDS((T, HW), f32), SDS((T, HW), bf16), SDS((8, D), f32)],
        compiler_params=_cp(("arbitrary",), 32),
    )(dh, po, w_out, par)


def _inproj_bwd_tok(dh, h, par, w_in_t, dsg, dqs, dks, dvs, tabs):
    tm = 256

    def body(dh_ref, h_ref, par_ref, w_ref, dsg_ref, dq0, dq1, dq2, dk0, dk1, dk2, dv0, dv1, dv2,
             cos_ref, slo_ref, shi_ref, dhin_ref, dp_ref, y_ref, dpar_ref):
        @pl.when(pl.program_id(0) == 0)
        def _():
            dpar_ref[...] = jnp.zeros_like(dpar_ref)

        cos, slo, shi = cos_ref[...], slo_ref[...], shi_ref[...]
        total = lambda a, b, c: a[...].astype(f32) + b[...].astype(f32) + c[...].astype(f32)
        dq = total(dq0, dq1, dq2) * 0.125
        dk = total(dk0, dk1, dk2)
        dp_ref[:, :2 * HW] = dsg_ref[...]
        dp_ref[:, 2 * HW:3 * HW] = _rope_t(dq, cos, slo, shi).astype(bf16)
        dp_ref[:, 3 * HW:4 * HW] = _rope_t(dk, cos, slo, shi).astype(bf16)
        dp_ref[:, 4 * HW:] = total(dv0, dv1, dv2).astype(bf16)
        dy = _nn(dp_ref[...], w_ref[...])
        par_v, h_v = par_ref[...], h_ref[...]
        y_ref[...] = _norm_mod(h_v, par_v)[0].astype(bf16)
        dx, dsh, dsc, dng = _norm_mod_bwd(dy, h_v, par_v)
        dhin_ref[...] = dh_ref[...] + dx
        dpar_ref[0:1, :] += dsh
        dpar_ref[1:2, :] += dsc
        dpar_ref[3:4, :] += dng

    row = pl.BlockSpec((tm, D), lambda i: (i, 0))
    half = pl.BlockSpec((tm, HW), lambda i: (i, 0))
    one = pl.BlockSpec((8, D), lambda i: (0, 0))
    return _pallas_hbm(
        body, name="inproj_bwd_tok", grid=(T // tm,),
        in_specs=[row, row, one, pl.BlockSpec((INW, D), lambda i: (0, 0)), row] + [half] * 12,
        out_specs=[row, pl.BlockSpec((tm, INW), lambda i: (i, 0)), row, one],
        out_shape=[SDS((T, D), f32), SDS((T, INW), bf16), SDS((T, D), bf16), SDS((8, D), f32)],
        compiler_params=_cp(("arbitrary",), 48),
    )(dh, h, par, w_in_t, dsg, *dqs, *dks, *dvs, *tabs)


def _loss_head(h, par, target):
    tm = 512

    def body(h_ref, par_ref, t_ref, dh_ref, acc_ref):
        @pl.when(pl.program_id(0) == 0)
        def _():
            acc_ref[...] = jnp.zeros_like(acc_ref)

        x, g = h_ref[...], par_ref[3:4, :]
        rstd = lax.rsqrt(_rowmean(x * x) + EPS)
        xhat = x * rstd
        err = xhat * g - t_ref[...]
        acc_ref[0:1, :] += _colsum(err * err)
        dy = err * (1.0 / D)
        acc_ref[3:4, :] += _colsum(dy * xhat)
        dxhat = dy * g
        dh_ref[...] = rstd * (dxhat - xhat * _rowmean(dxhat * xhat))

    row = pl.BlockSpec((tm, D), lambda i: (i, 0))
    one = pl.BlockSpec((8, D), lambda i: (0, 0))
    return _pallas_hbm(
        body, name="loss_head", grid=(T // tm,), in_specs=[row, one, row], out_specs=[row, one],
        out_shape=[SDS((T, D), f32), SDS((8, D), f32)], compiler_params=_cp(("arbitrary",), 32),
    )(h, par, target)


ELEMENTWISE_VMEM_BUDGET = 20 << 20


def _block_rows(rows, bytes_per_row):
    cap = ELEMENTWISE_VMEM_BUDGET // bytes_per_row
    if rows <= cap:
        return rows
    return next(b for b in range(cap - cap % 16, 0, -16) if rows % b == 0)


def _sum_slots(land, name):
    _, R, C = land.shape
    br = _block_rows(R, 2 * NDEV * C * land.dtype.itemsize + 2 * C * 4)

    def body(l_ref, o_ref):
        acc = l_ref[0].astype(f32)
        for s in range(1, NDEV):
            acc = acc + l_ref[s].astype(f32)
        o_ref[...] = acc

    return _pallas_hbm(
        body, name=name, grid=(R // br,), in_specs=[pl.BlockSpec((NDEV, br, C), lambda i: (0, i, 0))],
        out_specs=pl.BlockSpec((br, C), lambda i: (i, 0)), out_shape=SDS((R, C), f32),
        compiler_params=_cp(("arbitrary",), 32),
    )(land)


def _sum_partials(land, own, me, name):
    r = land.shape[-2]

    def body(me_ref, l_ref, own_ref, o_ref):
        mine = own_ref[...].astype(f32)
        acc = jnp.where(me_ref[0] == 0, mine, l_ref[0].astype(f32))
        for s in range(1, NDEV):
            acc = acc + jnp.where(me_ref[0] == s, mine, l_ref[s].astype(f32))
        o_ref[...] = acc

    if own.ndim == 3:
        kk = own.shape[0]
        specs = dict(grid=(kk,),
                     in_specs=[pl.BlockSpec((NDEV, None, r, D), lambda i, me_ref: (0, i, 0, 0)),
                               pl.BlockSpec((None, r, D), lambda i, me_ref: (i, me_ref[0], 0))],
                     out_specs=pl.BlockSpec((None, r, D), lambda i, me_ref: (i, 0, 0)))
        out_shape = SDS((kk, r, D), f32)
    else:
        specs = dict(grid=(1,),
                     in_specs=[pl.BlockSpec((NDEV, r, D), lambda i, me_ref: (0, 0, 0)),
                               pl.BlockSpec((r, D), lambda i, me_ref: (me_ref[0], 0))],
                     out_specs=pl.BlockSpec((r, D), lambda i, me_ref: (0, 0)))
        out_shape = SDS((r, D), f32)
    return _pallas_hbm(
        body, name=name, out_shape=out_shape,
        grid_spec=pltpu.PrefetchScalarGridSpec(num_scalar_prefetch=1, **specs),
        compiler_params=_cp(("arbitrary",), 32),
    )(me.reshape(1), land, own)


def _adamw(w, g, m, v, name):
    R, C = w.shape
    br = _block_rows(R, 2 * 7 * C * 4)

    def body(w_ref, g_ref, m_ref, v_ref, d_ref, mo_ref, vo_ref):
        gv = g_ref[...]
        m2 = B1 * m_ref[...] + (1.0 - B1) * gv
        v2 = B2 * v_ref[...] + (1.0 - B2) * (gv * gv)
        mo_ref[...] = m2
        vo_ref[...] = v2
        m_hat = m2 / (1.0 - B1 ** STEP)
        v_hat = v2 / (1.0 - B2 ** STEP)
        d_ref[...] = -LR * (m_hat / (jnp.sqrt(v_hat) + AEPS) + WD * w_ref[...])

    blk = pl.BlockSpec((br, C), lambda i: (i, 0))
    return _pallas_hbm(
        body, name=name, grid=(R // br,), in_specs=[blk] * 4, out_specs=[blk] * 3,
        out_shape=[SDS((R, C), f32)] * 3, compiler_params=_cp(("arbitrary",), 32),
    )(w, g, m, v)


def _adamw_nd(w, g, m, v, name):
    shp = w.shape
    r2 = (-1, shp[-1]) if w.ndim > 1 else (8, shp[0] // 8)
    outs = _adamw(w.reshape(r2), g.reshape(r2), m.reshape(r2), v.reshape(r2), name)
    return [o.reshape(shp) for o in outs]


def _par_rows(mod_l, s, gain):
    rows = jnp.pad(mod_l.reshape(9, D)[3 * s:3 * s + 3], ((0, 5), (0, 0)))
    return rows + jnp.pad(gain[None, :], ((3, 4), (0, 0)))


def _pad_rows(a):
    a = a.reshape(-1, CH)
    return jnp.pad(a, ((0, (-a.shape[0]) % 8), (0, 0)))


def _prepare(c, ada_w, ada_b, norm_g):
    me = 4 * lax.axis_index("x") + 2 * lax.axis_index("y") + lax.axis_index("c")

    pay = jnp.pad(c, ((0, 7), (0, 0)))
    pay = jnp.concatenate([pay, jnp.pad(norm_g.reshape(6, OUTS), ((0, 2), (0, D - OUTS)))], axis=0)
    got = _all_gather_small(pay, "gather_c")
    c_all = got[:, 0, :]
    gains = got[:, 8:14, :OUTS].transpose(1, 0, 2).reshape(DEPTH, 3, D)

    ada_b_loc = lax.dynamic_slice(ada_b, (0, me * ADAS), (DEPTH, ADAS)).reshape(DEPTH, 1, ADAS)
    mod_cols = _mod_fwd(c_all, ada_w, ada_b_loc)
    got = _all_gather_small(mod_cols.reshape(DEPTH * NDEV, ADAS), "gather_mod").reshape(NDEV, DEPTH, NDEV, ADAS)
    mod = lax.dynamic_index_in_dim(got, me, axis=2, keepdims=False).transpose(1, 0, 2).reshape(DEPTH, 9 * D)
    pars = [[_par_rows(mod[l], s, gains[l, s]) for s in range(3)] for l in range(DEPTH)]
    return me, c_all, pars


def _fwd_bwd(x2, target, pars, get_w, put_g, small_ready, sgu_ln_g, sgu_ln_b, sgu_w, sgu_b, final_g):
    tabs = _rope_tables()
    tril = jnp.tril(jnp.ones((CH, CH), dtype=bool))
    ws_m = jnp.where(tril[None, None], sgu_w, 0.0).astype(bf16)
    ws_mt = jnp.swapaxes(ws_m, -1, -2)
    behind = lambda rows, token: rows if token is None else rows + token[0, 0]

    h = x2
    saved = []
    for l in range(DEPTH):
        lng, lnb = sgu_ln_g[l].reshape(1, HW), sgu_ln_b[l].reshape(1, HW)
        bs_t = sgu_b[l].T
        h0 = h
        w, token = get_w(4 * l, h0)
        h1, fo1, g1, u1, a1 = _ffn_fwd(h0, behind(pars[l][0], token), w)
        w, token = get_w(4 * l + 2, h1)
        ua, va, q, k, v = _inproj_fwd(h1, behind(pars[l][1], token), w, tabs)
        branches, token = [], None
        for d in PATTERN_DILATIONS:
            branches.append(_attn_fwd(q, k, v, d, token))
            if len(branches) == 2 and l + 1 < DEPTH:
                token = get_w(4 * l + 1, branches[-1][1])[1]
        o, lse = _attn_combine([b[0] for b in branches], [b[1] for b in branches])
        mixed = _sgu_fwd(ua, va, o, lng, lnb, ws_m[l], bs_t)
        w, token = get_w(4 * l + 3, mixed)
        h2, po = _outproj_fwd(mixed, w, h1, behind(pars[l][1], token))
        w, token = get_w(4 * l + 1, h2)
        h3, fo2, g2, u2, a2 = _ffn_fwd(h2, behind(pars[l][2], token), w)
        saved.append((h0, h1, h2, fo1, g1, u1, a1, ua, va, q, k, v, o, mixed, lse, po, fo2, g2, u2, a2))
        h = h3

    par_f = jnp.pad(final_g[None, :], ((3, 4), (0, 0)))
    dh, head = _loss_head(h, par_f, target)

    dmods, dgains, dsgu = [None] * DEPTH, [None] * DEPTH, [None] * DEPTH
    token = None
    for l in reversed(range(DEPTH)):
        w_f1, w_f2, w_i, w_o = (get_w(4 * l + j, None)[0] for j in (0, 1, 2, 3))
        h0, h1, h2, fo1, g1, u1, a1, ua, va, q, k, v, o, mixed, lse, po, fo2, g2, u2, a2 = saved[l]
        lng, lnb = sgu_ln_g[l].reshape(1, HW), sgu_ln_b[l].reshape(1, HW)
        bs_t = sgu_b[l].T

        dh, dg_s, du_s, y, dfb, dpar3 = _ffn_bwd_tok(dh, h2, behind(pars[l][2], token), fo2, g2, u2, w_f2)
        token = put_g(4 * l + 1, _ffn_bwd_w(a2, dg_s, du_s, y, dfb))

        dob, d_a, d_b, dpar2g = _outproj_bwd(dh, po, w_o, behind(pars[l][1], token))
        token = put_g(4 * l + 3, _tn_matmul(mixed, dob, 512, "w_out_grad"))
        parts = [_attn_bwd(q, k, v, o, d_b, lse, d) for d in PATTERN_DILATIONS]
        dsg, dln, dws, dbl = _sgu_bwd(ua, va, d_a, lng, lnb, ws_m[l], ws_mt[l], bs_t)
        dh, dp, y2, dpar2 = _inproj_bwd_tok(dh, h1, behind(pars[l][1], token), w_i, dsg, [p[0] for p in parts],
                                            [p[1] for p in parts], [p[2] for p in parts], tabs)
        token = put_g(4 * l + 2, _tn_matmul(dp, y2, 640, "w_in_grad"))

        dh, dg_s, du_s, y, dfb, dpar1 = _ffn_bwd_tok(dh, h0, behind(pars[l][0], token), fo1, g1, u1, w_f1)
        dmods[l] = jnp.concatenate([dpar1[0:3], dpar2[0:2], dpar2g[2:3], dpar3[0:3]], axis=0).reshape(9 * D)
        dgains[l] = jnp.stack([dpar1[3], dpar2[3], dpar3[3]])
        dsgu[l] = (dln[0], dln[1], jnp.where(tril[None], dws, 0.0), jnp.sum(dbl, axis=-1))
        token = small_ready(head, dmods, dgains, dsgu) if l == 0 else None
        token = put_g(4 * l, _ffn_bwd_w(a1, dg_s, du_s, y, dfb, token))
    return dh, token


def kernel(x, c, ada_w, ada_b, norm_g, ffn1_wg, ffn1_wu, ffn1_wd, ffn2_wg, ffn2_wu, ffn2_wd, w_in, sgu_ln_g, sgu_ln_b, sgu_w, sgu_b, w_out, final_g, loss_target, m_ada_w, m_ada_b, m_norm_g, m_ffn1_wg, m_ffn1_wu, m_ffn1_wd, m_ffn2_wg, m_ffn2_wu, m_ffn2_wd, m_w_in, m_sgu_ln_g, m_sgu_ln_b, m_sgu_w, m_sgu_b, m_w_out, m_final_g, v_ada_w, v_ada_b, v_norm_g, v_ffn1_wg, v_ffn1_wu, v_ffn1_wd, v_ffn2_wg, v_ffn2_wu, v_ffn2_wd, v_w_in, v_sgu_ln_g, v_sgu_ln_b, v_sgu_w, v_sgu_b, v_w_out, v_final_g):
    me, c_all, pars = _prepare(c, ada_w, ada_b, norm_g)

    tr = lambda w: jnp.swapaxes(w, -1, -2).astype(bf16)
    locs = []
    for l in range(DEPTH):
        locs.append(jnp.stack([tr(ffn1_wg[l]), tr(ffn1_wu[l]), ffn1_wd[l].astype(bf16)]))
        locs.append(jnp.stack([tr(ffn2_wg[l]), tr(ffn2_wu[l]), ffn2_wd[l].astype(bf16)]))
        locs.append(tr(w_in[l]))
        locs.append(w_out[l].astype(bf16))
    locs, pars = lax.optimization_barrier((locs, pars))

    first, locs = lax.optimization_barrier((_all_gather_rows([locs[0]], "gather_first")[0], locs))
    placed = [_place_own(a, me) for a in locs[1:]]
    ready = {0: first}
    groups = ([2, 3], [1], [4], [6, 7], [5])
    flying = {}

    def start_group(gi, behind):
        pieces = groups[gi]
        arrays, _ = lax.optimization_barrier(([placed[p - 1] for p in pieces], behind))
        started = _exchange_start(False, arrays, "gather_start_%d" % gi)
        flying.update({p: (gi, started[:-1]) for p in pieces})
        return started[-1]

    first_token = start_group(0, first)

    def get_w(piece, after):
        token = first_token if piece == 0 else None
        if piece not in ready:
            gi, started = flying[piece]
            got_w = _exchange_wait(False, started, after, "gather_wait_%d" % gi)
            ready.update(zip(groups[gi], got_w))
            if gi + 1 < len(groups):
                token = start_group(gi + 1, got_w)
        return ready[piece], token

    sent = {}

    def put_g(piece, grad):
        land = lax.empty((NDEV,) + grad.shape[:-2] + (grad.shape[-2] // NDEV, D), bf16)
        started = _exchange_start(True, [grad, land], "scatter_start_%d" % piece)
        sent[piece] = started[:-1]
        return started[-1]

    small_sent = []

    def small_ready(head, dmods, dgains, dsgu):
        loss_part = 0.5 * jnp.sum(head[0]) / D
        small = jnp.concatenate([
            _pad_rows(jnp.stack(dmods)), _pad_rows(jnp.stack(dgains)),
            _pad_rows(jnp.stack([s[0] for s in dsgu])), _pad_rows(jnp.stack([s[1] for s in dsgu])),
            _pad_rows(jnp.stack([s[3] for s in dsgu])), _pad_rows(jnp.stack([s[2] for s in dsgu])),
            _pad_rows(head[3]), _pad_rows(jnp.pad(loss_part[None], (0, CH - 1)))], axis=0)
        started = _exchange_start(False, [_place_own(small, me, "place_small")], "small_start")
        small_sent.append(started[:-1])
        return started[-1]

    dh, last = _fwd_bwd(x[0], loss_target[0], pars, get_w, put_g, small_ready, sgu_ln_g, sgu_ln_b, sgu_w, sgu_b, final_g)
    grad_x = dh[None]

    got = _exchange_wait(False, small_sent[0], last, "small_wait")[0].reshape(NDEV, -1, CH)
    tot = _sum_slots(got, "sum_small")
    n_mod, n_gain, n_sw = DEPTH * 9 * D // CH, DEPTH * 3 * D // CH, DEPTH * 4 * CH
    offs = [0, n_mod, n_mod + n_gain, n_mod + n_gain + 8, n_mod + n_gain + 16, n_mod + n_gain + 24]
    g_ada_b = tot[offs[0]:offs[1]].reshape(DEPTH, 9 * D)
    g_gain_full = tot[offs[1]:offs[2]].reshape(DEPTH, 3, D)
    g_ln_g = tot[offs[2]:offs[3]].reshape(DEPTH, 4, CH)
    g_ln_b = tot[offs[3]:offs[4]].reshape(DEPTH, 4, CH)
    g_sb = tot[offs[4]:offs[5]].reshape(DEPTH, 4, CH)
    g_sw = tot[offs[5]:offs[5] + n_sw].reshape(DEPTH, 4, CH, CH)
    g_final = tot[offs[5] + n_sw:offs[5] + n_sw + 8].reshape(D)
    loss = tot[offs[5] + n_sw + 8, 0]
    g_norm = lax.dynamic_slice(g_gain_full, (0, 0, me * OUTS), (DEPTH, 3, OUTS))

    dmod_all = got[:, offs[0]:offs[1]].reshape(NDEV, DEPTH, 9 * D)
    dmod_cols = lax.dynamic_slice(dmod_all, (0, 0, me * ADAS), (NDEV, DEPTH, ADAS)).transpose(1, 0, 2)
    g_ada_w = _ada_grad((c_all * jax.nn.sigmoid(c_all)).T, dmod_cols)

    sums, after = {}, tot

    def collect(piece, after):
        own, land = _exchange_wait(True, sent[piece], after, "scatter_wait_%d" % piece)
        sums[piece] = _sum_partials(land, own, me, "sum_grads")
        return sums[piece]

    for piece in (5, 7, 6, 4, 1, 3, 2):
        after = collect(piece, after)
    back = lambda t: jnp.swapaxes(t, -1, -2)
    f2 = jnp.stack([sums[4 * l + 1] for l in range(DEPTH)])
    g_w_in = back(jnp.stack([sums[4 * l + 2] for l in range(DEPTH)]))
    g_w_out = jnp.stack([sums[4 * l + 3] for l in range(DEPTH)])

    ws = [ada_w, ada_b, norm_g, ffn1_wg, ffn1_wu, ffn1_wd, ffn2_wg, ffn2_wu, ffn2_wd, w_in, sgu_ln_g, sgu_ln_b, sgu_w,
          sgu_b, w_out, final_g]
    ms = [m_ada_w, m_ada_b, m_norm_g, m_ffn1_wg, m_ffn1_wu, m_ffn1_wd, m_ffn2_wg, m_ffn2_wu, m_ffn2_wd, m_w_in,
          m_sgu_ln_g, m_sgu_ln_b, m_sgu_w, m_sgu_b, m_w_out, m_final_g]
    vs = [v_ada_w, v_ada_b, v_norm_g, v_ffn1_wg, v_ffn1_wu, v_ffn1_wd, v_ffn2_wg, v_ffn2_wu, v_ffn2_wd, v_w_in,
          v_sgu_ln_g, v_sgu_ln_b, v_sgu_w, v_sgu_b, v_w_out, v_final_g]
    gw = [g_ada_w, g_ada_b, g_norm, None, None, None, back(f2[:, 0]), back(f2[:, 1]), f2[:, 2],
          g_w_in, g_ln_g, g_ln_b, g_sw, g_sb, g_w_out, g_final]
    upd = [None] * len(ws)
    for i in (1, 2, 10, 11, 12, 13, 15, 6, 7, 8, 9, 14, 0):
        upd[i] = _adamw_nd(ws[i], gw[i], ms[i], vs[i], "adamw")
    collect(0, upd[0][0])
    f1 = jnp.stack([sums[4 * l] for l in range(DEPTH)])
    gw[3:6] = [back(f1[:, 0]), back(f1[:, 1]), f1[:, 2]]
    for i in (3, 4, 5):
        upd[i] = _adamw_nd(ws[i], gw[i], ms[i], vs[i], "adamw")
    return (loss, grad_x, *gw, *[u[0] for u in upd], *[u[1] for u in upd], *[u[2] for u in upd])
```

```python
import functools
import math

import jax
import jax.numpy as jnp
from jax import lax
from jax.experimental import pallas as pl
from jax.experimental.pallas import tpu as pltpu

f32, bf16 = jnp.float32, jnp.bfloat16
SDS = jax.ShapeDtypeStruct

T, D, F = 4096, 1024, 2816
NDEV, DEPTH = 8, 2
HW = 512
INW = 5 * HW
FS, INS, OUTS, ADAS = F // NDEV, INW // NDEV, D // NDEV, 9 * D // NDEV
CH = 128
PATTERN_DILATIONS = (1, 4, 16)
ROPE_THETA = 10000.0
EPS = 1e-6
LR, B1, B2, AEPS, WD, STEP = 0.001, 0.9, 0.999, 1e-08, 0.01, 10
MESH = pl.DeviceIdType.MESH


def _cp(sems, vmem_mb):
    return pltpu.CompilerParams(dimension_semantics=sems, vmem_limit_bytes=vmem_mb << 20)


def _pallas_hbm(body, *, out_shape, **kw):
    typed = jax.tree.map(lambda s: pltpu.HBM(s.shape, s.dtype), out_shape)
    call = pl.pallas_call(body, out_shape=typed, **kw)

    def run(*operands):
        pin = lambda x: x if x.dtype == jnp.int32 else pltpu.with_memory_space_constraint(x, pltpu.HBM)
        return call(*[pin(x) for x in operands])

    return run


def _nn(a, b):
    return lax.dot_general(a, b, (((1,), (0,)), ((), ())), preferred_element_type=f32)


def _nt(a, b):
    return lax.dot_general(a, b, (((1,), (1,)), ((), ())), preferred_element_type=f32)


def _tn(a, b):
    return lax.dot_general(a, b, (((0,), (0,)), ((), ())), preferred_element_type=f32)


def _colsum(a):
    return jnp.sum(a, axis=0, keepdims=True)


def _rowmean(a):
    return jnp.mean(a, axis=-1, keepdims=True)


def _norm_mod(x, par):
    rstd = lax.rsqrt(_rowmean(x * x) + EPS)
    xhat = x * rstd
    n = xhat * par[3:4, :]
    y = n * (1.0 + par[1:2, :]) + par[0:1, :]
    return y, n, xhat, rstd


def _norm_mod_bwd(dy, x, par):
    _, n, xhat, rstd = _norm_mod(x, par)
    dn = dy * (1.0 + par[1:2, :])
    dxhat = dn * par[3:4, :]
    dx = rstd * (dxhat - xhat * _rowmean(dxhat * xhat))
    return dx, _colsum(dy), _colsum(dy * n), _colsum(dn * xhat)


_GK = math.sqrt(2.0 / math.pi)


def _gelu(x):
    return 0.5 * x * (1.0 + jnp.tanh(_GK * (x + 0.044715 * x * x * x)))


def _gelu_grad(x):
    t = jnp.tanh(_GK * (x + 0.044715 * x * x * x))
    return 0.5 * (1.0 + t) + 0.5 * x * (1.0 - t * t) * (_GK * (1.0 + 3.0 * 0.044715 * x * x))


def _rows(ref, idx, r):
    if len(ref.shape) == 3:
        return ref.at[:, pl.ds(idx * r, r), :]
    return ref.at[pl.ds(idx * r, r), :]


def _flip(v, bit):
    return 1 - v if bit else v


def _all_gather_small(x, name):
    R, C = x.shape

    def body(x_ref, out_ref, send_sems, recv_sems):
        mx, my, mc = lax.axis_index("x"), lax.axis_index("y"), lax.axis_index("c")
        me = 4 * mx + 2 * my + mc
        out_ref[me] = x_ref[...]
        sent = []
        for k in range(1, NDEV):
            peer = (_flip(mx, k & 4), _flip(my, k & 2), _flip(mc, k & 1))
            cp = pltpu.make_async_remote_copy(
                src_ref=x_ref, dst_ref=out_ref.at[me], send_sem=send_sems.at[k - 1],
                recv_sem=recv_sems.at[k - 1], device_id=peer, device_id_type=MESH)
            cp.start()
            sent.append(cp)
        for k in range(1, NDEV):
            peer = (_flip(mx, k & 4), _flip(my, k & 2), _flip(mc, k & 1))
            pidx = 4 * peer[0] + 2 * peer[1] + peer[2]
            pltpu.make_async_remote_copy(
                src_ref=x_ref, dst_ref=out_ref.at[pidx], send_sem=send_sems.at[k - 1],
                recv_sem=recv_sems.at[k - 1], device_id=peer, device_id_type=MESH).wait_recv()
        for cp in sent:
            cp.wait_send()

    vm = pl.BlockSpec(memory_space=pltpu.VMEM)
    return pl.pallas_call(
        body, name=name, out_shape=SDS((NDEV, R, C), f32), in_specs=[vm], out_specs=vm,
        scratch_shapes=[pltpu.SemaphoreType.DMA((NDEV - 1,)), pltpu.SemaphoreType.DMA((NDEV - 1,))],
        compiler_params=pltpu.CompilerParams(vmem_limit_bytes=32 << 20),
    )(x)


def _all_gather_rows(locs, name):
    n = len(locs)
    rs = [a.shape[-2] for a in locs]

    def body(*refs):
        src, out = refs[:n], refs[n:2 * n]
        send_sems, recv_sems, loc_sems = refs[2 * n:]
        mx, my, mc = lax.axis_index("x"), lax.axis_index("y"), lax.axis_index("c")
        me, sib = (mx, my, mc), (mx, my, 1 - mc)
        chips = [(1 - mx, my), (mx, 1 - my), (1 - mx, 1 - my)]

        def blk(a, p):
            return _rows(out[a], 4 * p[0] + 2 * p[1] + p[2], rs[a])

        def copy(k, a, block, to, from_src=False):
            return pltpu.make_async_remote_copy(
                src_ref=src[a] if from_src else blk(a, block), dst_ref=blk(a, block),
                send_sem=send_sems.at[k * n + a], recv_sem=recv_sems.at[k * n + a],
                device_id=to, device_id_type=MESH)

        mine = [pltpu.make_async_copy(src[a], blk(a, me), loc_sems.at[a]) for a in range(n)]
        for m in mine:
            m.start()
        first = []
        for j, chip in enumerate(chips):
            first += [copy(1 + j, a, me, (*chip, mc), True) for a in range(n)]
        first += [copy(0, a, me, sib, True) for a in range(n)]
        for cp in first:
            cp.start()
        passed = []
        for j, chip in enumerate(chips):
            for a in range(n):
                copy(1 + j, a, (*chip, mc), me).wait_recv()
            fwd = [copy(4 + j, a, (*chip, mc), sib) for a in range(n)]
            for cp in fwd:
                cp.start()
            passed += fwd
        for a in range(n):
            copy(0, a, sib, me).wait_recv()
        for j, chip in enumerate(chips):
            for a in range(n):
                copy(4 + j, a, (*chip, 1 - mc), me).wait_recv()
        for cp in first + passed:
            cp.wait_send()
        for m in mine:
            m.wait()

    hbm = pl.BlockSpec(memory_space=pl.ANY)
    out_shape = [SDS(a.shape[:-2] + (NDEV * a.shape[-2], a.shape[-1]), a.dtype) for a in locs]
    return pl.pallas_call(
        body, name=name, out_shape=out_shape, in_specs=[hbm] * n, out_specs=[hbm] * n,
        scratch_shapes=[pltpu.SemaphoreType.DMA((7 * n,)), pltpu.SemaphoreType.DMA((7 * n,)),
                        pltpu.SemaphoreType.DMA((n,))],
    )(*locs)


HBM_SPEC = pl.BlockSpec(memory_space=pltpu.HBM)
SEM_SPEC = pl.BlockSpec(memory_space=pltpu.SEMAPHORE)
DATAFLOW_EFFECT = pltpu.SideEffectType.DATAFLOW_SIDE_EFFECTING


def _place_own(loc, me, name="place_own"):
    r, cols = loc.shape[-2:]
    loc3 = loc.reshape(-1, r, cols)
    kk = loc3.shape[0]

    def body(me_ref, src_ref, full_ref, out_ref):
        out_ref[...] = src_ref[...]

    out = _pallas_hbm(
        body, name=name, out_shape=SDS((kk, NDEV * r, cols), loc.dtype),
        grid_spec=pltpu.PrefetchScalarGridSpec(
            num_scalar_prefetch=1, grid=(kk,),
            in_specs=[pl.BlockSpec((None, r, cols), lambda i, me_ref: (i, 0, 0)), pl.BlockSpec(memory_space=pl.ANY)],
            out_specs=pl.BlockSpec((None, r, cols), lambda i, me_ref: (i, me_ref[0], 0))),
        input_output_aliases={2: 0}, compiler_params=_cp(("arbitrary",), 32),
    )(me.reshape(1), loc3, lax.empty((kk, NDEV * r, cols), loc.dtype))
    return out.reshape(loc.shape[:-2] + (NDEV * r, cols))


def _exchange_copies(scatter, bufs, n, send_sems, recv_sems):
    mx, my, mc = lax.axis_index("x"), lax.axis_index("y"), lax.axis_index("c")
    me = 4 * mx + 2 * my + mc
    out = []
    for k in range(1, NDEV):
        peer = (_flip(mx, k & 4), _flip(my, k & 2), _flip(mc, k & 1))
        pidx = 4 * peer[0] + 2 * peer[1] + peer[2]
        for a in range(n):
            r = bufs[a].shape[-2] // NDEV
            if scatter:
                src, dst, arrive = _rows(bufs[a], pidx, r), bufs[n + a].at[me], bufs[n + a].at[pidx]
            else:
                src, dst, arrive = _rows(bufs[a], me, r), _rows(bufs[a], me, r), _rows(bufs[a], pidx, r)
            sems = dict(send_sem=send_sems.at[(k - 1) * n + a], recv_sem=recv_sems.at[(k - 1) * n + a],
                        device_id=peer, device_id_type=MESH)
            out.append((pltpu.make_async_remote_copy(src_ref=src, dst_ref=dst, **sems),
                        pltpu.make_async_remote_copy(src_ref=src, dst_ref=arrive, **sems)))
    return out


def _exchange_start(scatter, arrays, name):
    m = len(arrays)
    n = m // 2 if scatter else m

    def body(*refs):
        send_sems, recv_sems, token = refs[m], refs[m + 1], refs[-1]
        for go, _ in _exchange_copies(scatter, refs[:m], n, send_sems, recv_sems):
            go.start()
        token[...] = jnp.zeros_like(token)

    sems = pltpu.SemaphoreType.DMA((7 * n,))
    return pl.pallas_call(
        body, name=name, in_specs=[HBM_SPEC] * m,
        out_shape=(sems, sems, *[pltpu.HBM(a.shape, a.dtype) for a in arrays], SDS((8, CH), f32)),
        out_specs=(SEM_SPEC, SEM_SPEC, *[HBM_SPEC] * m, pl.BlockSpec(memory_space=pltpu.VMEM)),
        input_output_aliases={a: 2 + a for a in range(m)},
        compiler_params=pltpu.CompilerParams(has_side_effects=DATAFLOW_EFFECT),
    )(*[pltpu.with_memory_space_constraint(a, pltpu.HBM) for a in arrays])


def _exchange_wait(scatter, started, after, name):
    send_sems, recv_sems, *arrays = started
    m = len(arrays)
    n = m // 2 if scatter else m

    def body(*refs):
        for go, arrive in _exchange_copies(scatter, refs[:m], n, refs[m], refs[m + 1]):
            go.wait_send()
            arrive.wait_recv()

    return pl.pallas_call(
        body, name=name, in_specs=[HBM_SPEC] * m + [SEM_SPEC, SEM_SPEC, pl.BlockSpec(memory_space=pl.ANY)],
        out_shape=[pltpu.HBM(a.shape, a.dtype) for a in arrays], out_specs=[HBM_SPEC] * m,
        input_output_aliases={a: a for a in range(m)},
        compiler_params=pltpu.CompilerParams(has_side_effects=DATAFLOW_EFFECT),
    )(*arrays, send_sems, recv_sems, after)


def _mod_fwd(c_all, ada_w, ada_b_loc):
    def body(c_ref, w_ref, b_ref, o_ref):
        ca = c_ref[...]
        ca = ca * jax.nn.sigmoid(ca)
        o_ref[...] = jnp.dot(ca, w_ref[...], precision=lax.Precision.HIGHEST,
                             preferred_element_type=f32) + b_ref[...]

    return _pallas_hbm(
        body, name="mod_fwd", grid=(DEPTH,), out_shape=SDS((DEPTH, NDEV, ADAS), f32),
        in_specs=[pl.BlockSpec((NDEV, D), lambda l: (0, 0)),
                  pl.BlockSpec((None, D, ADAS), lambda l: (l, 0, 0)),
                  pl.BlockSpec((None, 1, ADAS), lambda l: (l, 0, 0))],
        out_specs=pl.BlockSpec((None, NDEV, ADAS), lambda l: (l, 0, 0)),
        compiler_params=_cp(("arbitrary",), 32),
    )(c_all, ada_w, ada_b_loc)


def _ada_grad(cact_t, dmod_cols):
    def body(c_ref, d_ref, o_ref):
        acc = c_ref[:, 0:1] * d_ref[0:1, :]
        for b in range(1, NDEV):
            acc = acc + c_ref[:, b:b + 1] * d_ref[b:b + 1, :]
        o_ref[...] = acc

    tr = 256
    return _pallas_hbm(
        body, name="ada_grad", grid=(DEPTH, D // tr), out_shape=SDS((DEPTH, D, ADAS), f32),
        in_specs=[pl.BlockSpec((tr, NDEV), lambda l, i: (i, 0)),
                  pl.BlockSpec((None, NDEV, ADAS), lambda l, i: (l, 0, 0))],
        out_specs=pl.BlockSpec((None, tr, ADAS), lambda l, i: (l, i, 0)),
        compiler_params=_cp(("arbitrary", "arbitrary"), 32),
    )(cact_t, dmod_cols)


def _ffn_fwd(h, par, w3):
    tm, tf = 1024, 256
    nj = F // tf

    def body(h_ref, par_ref, wg_ref, wu_ref, wd_ref, ho_ref, fo_ref, g_ref, u_ref, a_ref, y_scr, acc):
        j = pl.program_id(1)

        @pl.when(j == 0)
        def _():
            y_scr[...] = _norm_mod(h_ref[...], par_ref[...])[0].astype(bf16)
            acc[...] = jnp.zeros_like(acc)

        y = y_scr[...]
        g = _nt(y, wg_ref[...])
        u = _nt(y, wu_ref[...])
        a = ((g * jax.nn.sigmoid(g)) * u).astype(bf16)
        g_ref[...] = g.astype(bf16)
        u_ref[...] = u.astype(bf16)
        a_ref[...] = a
        acc[...] += _nn(a, wd_ref[...])

        @pl.when(j == nj - 1)
        def _():
            fo_ref[...] = acc[...].astype(bf16)
            ho_ref[...] = h_ref[...] + (0.5 * par_ref[2:3, :]) * acc[...]

    row = pl.BlockSpec((tm, D), lambda i, j: (i, 0))
    hid = pl.BlockSpec((tm, tf), lambda i, j: (i, j))
    wspec = [pl.BlockSpec((None, tf, D), lambda i, j, k=k: (k, j, 0)) for k in range(3)]
    return _pallas_hbm(
        body, name="ffn_fwd", grid=(T // tm, nj),
        in_specs=[row, pl.BlockSpec((8, D), lambda i, j: (0, 0))] + wspec,
        out_specs=[row, row, hid, hid, hid],
        out_shape=[SDS((T, D), f32), SDS((T, D), bf16), SDS((T, F), bf16), SDS((T, F), bf16), SDS((T, F), bf16)],
        scratch_shapes=[pltpu.VMEM((tm, D), bf16), pltpu.VMEM((tm, D), f32)],
        compiler_params=_cp(("arbitrary", "arbitrary"), 52),
    )(h, par, w3, w3, w3)


def _ffn_bwd_tok(dh, h, par, fo, gs, us, w3):
    tm, tf = 512, 256
    nj = F // tf

    def body(dh_ref, h_ref, par_ref, fo_ref, g_ref, u_ref, wg_ref, wu_ref, wd_ref,
             dhin_ref, dg_ref, du_ref, y_ref, dfb_ref, dpar_ref, df_scr, dyacc):
        i, j = pl.program_id(0), pl.program_id(1)

        @pl.when(jnp.logical_and(i == 0, j == 0))
        def _():
            dpar_ref[...] = jnp.zeros_like(dpar_ref)

        @pl.when(j == 0)
        def _():
            dh_v, par_v = dh_ref[...], par_ref[...]
            dfb = ((0.5 * par_v[2:3, :]) * dh_v).astype(bf16)
            df_scr[...] = dfb
            dfb_ref[...] = dfb
            dpar_ref[2:3, :] += 0.5 * _colsum(dh_v * fo_ref[...].astype(f32))
            y_ref[...] = _norm_mod(h_ref[...], par_v)[0].astype(bf16)
            dyacc[...] = jnp.zeros_like(dyacc)

        for half in range(2):
            rs = slice(half * (tm // 2), (half + 1) * (tm // 2))
            da = _nt(df_scr[rs, :], wd_ref[...])
            g = g_ref[rs, :].astype(f32)
            u = u_ref[rs, :].astype(f32)
            sig = jax.nn.sigmoid(g)
            du = (da * (g * sig)).astype(bf16)
            dg = (da * u * (sig * (1.0 + g * (1.0 - sig)))).astype(bf16)
            dg_ref[rs, :] = dg
            du_ref[rs, :] = du
            dyacc[rs, :] += _nn(dg, wg_ref[...]) + _nn(du, wu_ref[...])

        @pl.when(j == nj - 1)
        def _():
            dx, dsh, dsc, dng = _norm_mod_bwd(dyacc[...], h_ref[...], par_ref[...])
            dhin_ref[...] = dh_ref[...] + dx
            dpar_ref[0:1, :] += dsh
            dpar_ref[1:2, :] += dsc
            dpar_ref[3:4, :] += dng

    row = pl.BlockSpec((tm, D), lambda i, j: (i, 0))
    hid = pl.BlockSpec((tm, tf), lambda i, j: (i, j))
    one = pl.BlockSpec((8, D), lambda i, j: (0, 0))
    wspec = [pl.BlockSpec((None, tf, D), lambda i, j, k=k: (k, j, 0)) for k in range(3)]
    return _pallas_hbm(
        body, name="ffn_bwd_tok", grid=(T // tm, nj),
        in_specs=[row, row, one, row, hid, hid] + wspec,
        out_specs=[row, hid, hid, row, row, one],
        out_shape=[SDS((T, D), f32), SDS((T, F), bf16), SDS((T, F), bf16), SDS((T, D), bf16),
                   SDS((T, D), bf16), SDS((8, D), f32)],
        scratch_shapes=[pltpu.VMEM((tm, D), bf16), pltpu.VMEM((tm, D), f32)],
        compiler_params=_cp(("arbitrary", "arbitrary"), 48),
    )(dh, h, par, fo, gs, us, w3, w3, w3)


def _ffn_bwd_w(a_s, dg_s, du_s, y, dfb, anchor=None):
    tf, tk = F // 2, 256
    nk = T // tk

    def body(a_ref, dg_ref, du_ref, y_ref, df_ref, *rest):
        o_ref, accg, accu, accd = rest[-4:]
        kk = pl.program_id(1)

        @pl.when(kk == 0)
        def _():
            accg[...] = jnp.zeros_like(accg)
            accu[...] = jnp.zeros_like(accu)
            accd[...] = jnp.zeros_like(accd)

        yv = y_ref[...]
        accg[...] += _tn(dg_ref[...], yv)
        accu[...] += _tn(du_ref[...], yv)
        accd[...] += _tn(a_ref[...], df_ref[...])

        @pl.when(kk == nk - 1)
        def _():
            o_ref[0] = accg[...].astype(bf16)
            o_ref[1] = accu[...].astype(bf16)
            o_ref[2] = accd[...].astype(bf16)

    hid = pl.BlockSpec((tk, tf), lambda j, kk: (kk, j))
    row = pl.BlockSpec((tk, D), lambda j, kk: (kk, 0))
    extra = [] if anchor is None else [anchor]
    return _pallas_hbm(
        body, name="ffn_bwd_w", grid=(F // tf, nk),
        in_specs=[hid, hid, hid, row, row] + [pl.BlockSpec((8, CH), lambda j, kk: (0, 0))] * len(extra),
        out_specs=pl.BlockSpec((3, tf, D), lambda j, kk: (0, j, 0)), out_shape=SDS((3, F, D), bf16),
        scratch_shapes=[pltpu.VMEM((tf, D), f32)] * 3,
        compiler_params=_cp(("arbitrary", "arbitrary"), 48),
    )(a_s, dg_s, du_s, y, dfb, *extra)


def _tn_matmul(a, b, bm, name):
    M, N = a.shape[1], b.shape[1]
    tk = 512
    nk = T // tk

    def body(a_ref, b_ref, o_ref, acc):
        kk = pl.program_id(1)

        @pl.when(kk == 0)
        def _():
            acc[...] = jnp.zeros_like(acc)

        acc[...] += _tn(a_ref[...], b_ref[...])

        @pl.when(kk == nk - 1)
        def _():
            o_ref[...] = acc[...].astype(bf16)

    return _pallas_hbm(
        body, name=name, grid=(M // bm, nk),
        in_specs=[pl.BlockSpec((tk, bm), lambda i, kk: (kk, i)), pl.BlockSpec((tk, N), lambda i, kk: (kk, 0))],
        out_specs=pl.BlockSpec((bm, N), lambda i, kk: (i, 0)), out_shape=SDS((M, N), bf16),
        scratch_shapes=[pltpu.VMEM((bm, N), f32)],
        compiler_params=_cp(("arbitrary", "arbitrary"), 40),
    )(a, b)


def _rope_tables():
    inv = ROPE_THETA ** (-jnp.arange(0, 64, 2, dtype=f32) / 64)
    ang = jnp.arange(T, dtype=f32)[:, None] * inv[None, :]
    ang = jnp.concatenate([ang, ang], axis=-1)
    cos, sin = jnp.tile(jnp.cos(ang), (1, 8)), jnp.tile(jnp.sin(ang), (1, 8))
    low = (jnp.arange(HW) % 64 < 32)[None, :]
    return cos, jnp.where(low, -sin, 0.0), jnp.where(low, 0.0, sin)


def _rope(t, cos, sin_lo, sin_hi):
    return t * cos + pltpu.roll(t, HW - 32, 1) * sin_lo + pltpu.roll(t, 32, 1) * sin_hi


def _rope_t(g, cos, sin_lo, sin_hi):
    return g * cos + pltpu.roll(g * sin_lo, 32, 1) + pltpu.roll(g * sin_hi, HW - 32, 1)


def _inproj_fwd(h, par, w_in_t, tabs):
    tm = 512

    def body(h_ref, par_ref, w_ref, cos_ref, slo_ref, shi_ref, ua_ref, va_ref, q_ref, k_ref, v_ref):
        y = _norm_mod(h_ref[...], par_ref[...])[0].astype(bf16)
        proj = lambda c: _nt(y, w_ref[c * HW:(c + 1) * HW, :])
        ua_ref[...] = proj(0)
        va_ref[...] = proj(1)
        cos, slo, shi = cos_ref[...], slo_ref[...], shi_ref[...]
        q_ref[...] = _rope(proj(2), cos, slo, shi) * 0.125
        k_ref[...] = _rope(proj(3), cos, slo, shi)
        v_ref[...] = proj(4)

    row = pl.BlockSpec((tm, D), lambda i: (i, 0))
    half = pl.BlockSpec((tm, HW), lambda i: (i, 0))
    return _pallas_hbm(
        body, name="inproj_fwd", grid=(T // tm,),
        in_specs=[row, pl.BlockSpec((8, D), lambda i: (0, 0)), pl.BlockSpec((INW, D), lambda i: (0, 0)), half, half, half],
        out_specs=[half] * 5,
        out_shape=[SDS((T, HW), f32)] * 5,
        compiler_params=_cp(("arbitrary",), 48),
    )(h, par, w_in_t, *tabs)


def _head_masks():
    lane = lax.broadcasted_iota(jnp.int32, (1, CH), 1)
    return lane < 64, lane >= 64


def _lane(t, idx):
    return jnp.sum(jnp.where(lax.broadcasted_iota(jnp.int32, t.shape, 1) == idx, t, 0.0), axis=-1, keepdims=True)


def _stack_heads(x):
    lo, hi = _head_masks()
    zero = jnp.zeros_like(x)
    return jnp.concatenate([jnp.where(lo, x, zero), jnp.where(hi, x, zero)], axis=0)


def _band_mask(has_prev):
    row = lax.broadcasted_iota(jnp.int32, (2 * CH, 2 * CH), 0) & (CH - 1)
    col = lax.broadcasted_iota(jnp.int32, (2 * CH, 2 * CH), 1)
    in_prev = jnp.logical_and(jnp.logical_and(col < CH, col >= row), has_prev)
    return jnp.logical_or(in_prev, jnp.logical_and(col >= CH, col - CH <= row))


def _attn_tiling(d):
    return CH * 16, 16 // d, CH


def _attn_fwd(q, k, v, d, anchor=None):
    rows, blocks, lanes = _attn_tiling(d)
    pairs = lanes // CH

    def body(q_ref, kc_ref, kp_ref, vc_ref, vp_ref, *rest):
        o_ref, lse_ref = rest[-2:]
        c, lb = pl.program_id(0), pl.program_id(1)
        col = lax.broadcasted_iota(jnp.int32, (CH, CH), 1)
        lo, _ = _head_masks()

        @pl.when(lb == 0)
        def _():
            lse_ref[...] = jnp.zeros_like(lse_ref)

        at = lambda r, b: pl.ds(r + b * CH * d, CH, stride=d) if d > 1 else pl.ds(b * CH, CH)
        for r in range(d):
            for b in range(blocks):
                own = at(r, b)
                k_prev, v_prev, before, mask = ((kc_ref, vc_ref, at(r, b - 1), _band_mask(True)) if b > 0 else
                                                (kp_ref, vp_ref, at(r, blocks - 1), _band_mask(c > 0)))
                lse_tile = lse_ref[own, :]
                for pi in range(pairs):
                    sl = slice(pi * CH, (pi + 1) * CH)
                    hp = lb * pairs + pi
                    kk = jnp.concatenate([k_prev[before, sl], kc_ref[own, sl]], axis=0).astype(bf16)
                    vv = jnp.concatenate([v_prev[before, sl], vc_ref[own, sl]], axis=0).astype(bf16)
                    s = jnp.where(mask, _nt(_stack_heads(q_ref[own, sl].astype(bf16)), kk), -jnp.inf)
                    m = jnp.max(s, axis=-1, keepdims=True)
                    p = jnp.exp(s - m)
                    den = jnp.sum(p, axis=-1, keepdims=True)
                    o = _nn(p.astype(bf16), vv) / den
                    o_ref[own, sl] = jnp.where(lo, o[:CH], o[CH:])
                    lse = m + jnp.log(den)
                    lse_tile = jnp.where(col == 2 * hp, lse[:CH], jnp.where(col == 2 * hp + 1, lse[CH:], lse_tile))
                lse_ref[own, :] = lse_tile

    cur = pl.BlockSpec((rows, lanes), lambda c, lb: (c, lb))
    prev = pl.BlockSpec((rows, lanes), lambda c, lb: (jnp.maximum(c - 1, 0), lb))
    extra = [] if anchor is None else [anchor]
    return _pallas_hbm(
        body, name="attn_fwd_d%d" % d, grid=(T // rows, HW // lanes),
        in_specs=[cur, cur, prev, cur, prev] + [pl.BlockSpec((8, CH), lambda c, lb: (0, 0))] * len(extra),
        out_specs=[cur, pl.BlockSpec((rows, CH), lambda c, lb: (c, 0))],
        out_shape=[SDS((T, HW), f32), SDS((T, CH), f32)],
        compiler_params=_cp(("arbitrary", "arbitrary"), 40),
    )(q, k, k, v, v, *extra)


def _attn_combine(os_, lses):
    tm = 512

    def body(o0_ref, o1_ref, o2_ref, l0_ref, l1_ref, l2_ref, o_ref, lse_ref):
        l0, l1, l2 = l0_ref[...], l1_ref[...], l2_ref[...]
        m = jnp.maximum(jnp.maximum(l0, l1), l2)
        e = [jnp.exp(l0 - m), jnp.exp(l1 - m), jnp.exp(l2 - m)]
        s = e[0] + e[1] + e[2]
        w = [ei / s for ei in e]
        lse_ref[...] = m + jnp.log(s)
        lo, _ = _head_masks()
        for hp in range(4):
            sl = slice(hp * CH, (hp + 1) * CH)
            acc = jnp.zeros((tm, CH), f32)
            for wp, op_ref in zip(w, (o0_ref, o1_ref, o2_ref)):
                wexp = jnp.where(lo, wp[:, 2 * hp:2 * hp + 1], wp[:, 2 * hp + 1:2 * hp + 2])
                acc = acc + wexp * op_ref[:, sl].astype(f32)
            o_ref[:, sl] = acc

    half = pl.BlockSpec((tm, HW), lambda i: (i, 0))
    stat = pl.BlockSpec((tm, CH), lambda i: (i, 0))
    return _pallas_hbm(
        body, name="attn_combine", grid=(T // tm,), in_specs=[half] * 3 + [stat] * 3, out_specs=[half, stat],
        out_shape=[SDS((T, HW), f32), SDS((T, CH), f32)],
        compiler_params=_cp(("arbitrary",), 32),
    )(*os_, *lses)


def _attn_bwd(q, k, v, o, do, lse, d):
    rows, blocks, lanes = _attn_tiling(d)
    pairs = lanes // CH
    steps = T // rows

    def body(qc_ref, qn_ref, kc_ref, kp_ref, vc_ref, vp_ref, oc_ref, on_ref, dc_ref, dn_ref, lc_ref, ln_ref,
             dq_ref, dk_ref, dv_ref):
        c, lb = pl.program_id(0), pl.program_id(1)
        lo, _ = _head_masks()
        at = lambda r, b: pl.ds(r + b * CH * d, CH, stride=d) if d > 1 else pl.ds(b * CH, CH)
        for r in range(d):
            for b in range(blocks):
                own = at(r, b)
                k_prev, v_prev, before, mask_c = ((kc_ref, vc_ref, at(r, b - 1), _band_mask(True)) if b > 0 else
                                                  (kp_ref, vp_ref, at(r, blocks - 1), _band_mask(c > 0)))
                q_next, o_next, d_next, l_next, after, has_next = (
                    (qc_ref, oc_ref, dc_ref, lc_ref, at(r, b + 1), True) if b + 1 < blocks else
                    (qn_ref, on_ref, dn_ref, ln_ref, at(r, 0), c < steps - 1))
                mask_n = _band_mask(has_next)[:, :CH]
                lse_c, lse_n = lc_ref[own, :], l_next[after, :]
                for pi in range(pairs):
                    sl = slice(pi * CH, (pi + 1) * CH)
                    hp = lb * pairs + pi
                    kc, vc = kc_ref[own, sl].astype(bf16), vc_ref[own, sl].astype(bf16)
                    kk = jnp.concatenate([k_prev[before, sl].astype(bf16), kc], axis=0)
                    vv = jnp.concatenate([v_prev[before, sl].astype(bf16), vc], axis=0)
                    doc, don = dc_ref[own, sl], d_next[after, sl]
                    qs_c, qs_n = _stack_heads(qc_ref[own, sl].astype(bf16)), _stack_heads(q_next[after, sl].astype(bf16))
                    ds_c, ds_n = _stack_heads(doc.astype(bf16)), _stack_heads(don.astype(bf16))
                    delta_c = jnp.sum(_stack_heads(doc * oc_ref[own, sl]), axis=-1, keepdims=True)
                    delta_n = jnp.sum(_stack_heads(don * o_next[after, sl]), axis=-1, keepdims=True)
                    heads = lambda t: jnp.concatenate([_lane(t, 2 * hp), _lane(t, 2 * hp + 1)], axis=0)
                    p1 = jnp.where(mask_c, jnp.exp(_nt(qs_c, kk) - heads(lse_c)), 0.0)
                    g1 = (p1 * (_nt(ds_c, vv) - delta_c)).astype(bf16)
                    dq = _nn(g1, kk)
                    dq_ref[own, sl] = jnp.where(lo, dq[:CH], dq[CH:])
                    p2 = jnp.where(mask_n, jnp.exp(_nt(qs_n, kc) - heads(lse_n)), 0.0)
                    g2 = (p2 * (_nt(ds_n, vc) - delta_n)).astype(bf16)
                    dk_ref[own, sl] = _tn(jnp.concatenate([g1[:, CH:], g2], axis=0), jnp.concatenate([qs_c, qs_n], axis=0))
                    dv_ref[own, sl] = _tn(jnp.concatenate([p1[:, CH:].astype(bf16), p2.astype(bf16)], axis=0),
                                          jnp.concatenate([ds_c, ds_n], axis=0))

    cur = pl.BlockSpec((rows, lanes), lambda c, lb: (c, lb))
    prev = pl.BlockSpec((rows, lanes), lambda c, lb: (jnp.maximum(c - 1, 0), lb))
    nxt = pl.BlockSpec((rows, lanes), lambda c, lb: (jnp.minimum(c + 1, steps - 1), lb))
    scur = pl.BlockSpec((rows, CH), lambda c, lb: (c, 0))
    snxt = pl.BlockSpec((rows, CH), lambda c, lb: (jnp.minimum(c + 1, steps - 1), 0))
    return _pallas_hbm(
        body, name="attn_bwd_d%d" % d, grid=(steps, HW // lanes),
        in_specs=[cur, nxt, cur, prev, cur, prev, cur, nxt, cur, nxt, scur, snxt],
        out_specs=[cur] * 3, out_shape=[SDS((T, HW), f32)] * 3,
        compiler_params=_cp(("arbitrary", "arbitrary"), 48),
    )(q, q, k, k, v, v, o, o, do, do, lse, lse)


def _sgu_fwd(ua, va, o, lng, lnb, ws, bs_t):
    tm = 512

    def body(ua_ref, va_ref, o_ref, lg_ref, lb_ref, ws_ref, bs_ref, mix_ref):
        for hd in range(4):
            sl = slice(hd * CH, (hd + 1) * CH)
            w = ws_ref[hd]
            for cc in range(tm // CH):
                rs = slice(cc * CH, (cc + 1) * CH)
                u = _gelu(ua_ref[rs, sl])
                v = _gelu(va_ref[rs, sl])
                vc = v - _rowmean(v)
                vn = vc * lax.rsqrt(_rowmean(vc * vc) + EPS) * lg_ref[:, sl] + lb_ref[:, sl]
                z = _nn(w, vn.astype(bf16)) + bs_ref[:, hd:hd + 1]
                mix_ref[rs, sl] = (u * z).astype(bf16)
        mix_ref[:, HW:] = o_ref[...].astype(bf16)

    half = pl.BlockSpec((tm, HW), lambda i: (i, 0))
    vec = pl.BlockSpec((1, HW), lambda i: (0, 0))
    return _pallas_hbm(
        body, name="sgu_fwd", grid=(T // tm,),
        in_specs=[half, half, half, vec, vec, pl.BlockSpec((4, CH, CH), lambda i: (0, 0, 0)),
                  pl.BlockSpec((CH, 4), lambda i: (0, 0))],
        out_specs=pl.BlockSpec((tm, 2 * HW), lambda i: (i, 0)), out_shape=SDS((T, 2 * HW), bf16),
        compiler_params=_cp(("arbitrary",), 32),
    )(ua, va, o, lng, lnb, ws, bs_t)


def _sgu_bwd(ua, va, d_a, lng, lnb, ws, ws_t, bs_t):
    tm = 512

    def body(ua_ref, va_ref, d_ref, lg_ref, lb_ref, ws_ref, wst_ref, bs_ref, dsg_ref, dln_ref, dws_ref, db_ref):
        @pl.when(pl.program_id(0) == 0)
        def _():
            dln_ref[...] = jnp.zeros_like(dln_ref)
            dws_ref[...] = jnp.zeros_like(dws_ref)
            db_ref[...] = jnp.zeros_like(db_ref)

        for hd in range(4):
            sl = slice(hd * CH, (hd + 1) * CH)
            w, wt = ws_ref[hd], wst_ref[hd]
            lg = lg_ref[:, sl]
            for cc in range(tm // CH):
                rs = slice(cc * CH, (cc + 1) * CH)
                xa, xv, dd = ua_ref[rs, sl], va_ref[rs, sl], d_ref[rs, sl]
                u, v = _gelu(xa), _gelu(xv)
                vc = v - _rowmean(v)
                rstd = lax.rsqrt(_rowmean(vc * vc) + EPS)
                xh = vc * rstd
                vnb = (xh * lg + lb_ref[:, sl]).astype(bf16)
                z = _nn(w, vnb) + bs_ref[:, hd:hd + 1]
                dz = dd * u
                dzb = dz.astype(bf16)
                dsg_ref[rs, sl] = (dd * z * _gelu_grad(xa)).astype(bf16)
                dws_ref[hd] += _nt(dzb, vnb)
                db_ref[hd] += dz
                dvn = _nn(wt, dzb)
                dln_ref[0:1, sl] += _colsum(dvn * xh)
                dln_ref[1:2, sl] += _colsum(dvn)
                dxh = dvn * lg
                dv = rstd * (dxh - _rowmean(dxh) - xh * _rowmean(dxh * xh))
                dsg_ref[rs, HW + hd * CH:HW + (hd + 1) * CH] = (dv * _gelu_grad(xv)).astype(bf16)

    half = pl.BlockSpec((tm, HW), lambda i: (i, 0))
    vec = pl.BlockSpec((1, HW), lambda i: (0, 0))
    mat = pl.BlockSpec((4, CH, CH), lambda i: (0, 0, 0))
    return _pallas_hbm(
        body, name="sgu_bwd", grid=(T // tm,),
        in_specs=[half, half, half, vec, vec, mat, mat, pl.BlockSpec((CH, 4), lambda i: (0, 0))],
        out_specs=[pl.BlockSpec((tm, 2 * HW), lambda i: (i, 0)), pl.BlockSpec((8, HW), lambda i: (0, 0)), mat, mat],
        out_shape=[SDS((T, 2 * HW), bf16), SDS((8, HW), f32), SDS((4, CH, CH), f32), SDS((4, CH, CH), f32)],
        compiler_params=_cp(("arbitrary",), 32),
    )(ua, va, d_a, lng, lnb, ws, ws_t, bs_t)


def _outproj_fwd(mixed, w_out, h, par):
    tm = 512

    def body(mix_ref, w_ref, h_ref, par_ref, ho_ref, po_ref):
        p = _nn(mix_ref[...], w_ref[...])
        po_ref[...] = p.astype(bf16)
        ho_ref[...] = h_ref[...] + par_ref[2:3, :] * p

    row = pl.BlockSpec((tm, D), lambda i: (i, 0))
    return _pallas_hbm(
        body, name="outproj_fwd", grid=(T // tm,),
        in_specs=[row, pl.BlockSpec((D, D), lambda i: (0, 0)), row, pl.BlockSpec((8, D), lambda i: (0, 0))],
        out_specs=[row, row], out_shape=[SDS((T, D), f32), SDS((T, D), bf16)],
        compiler_params=_cp(("arbitrary",), 32),
    )(mixed, w_out, h, par)


def _outproj_bwd(dh, po, w_out, par):
    tm = 512

    def body(dh_ref, po_ref, w_ref, par_ref, do_ref, da_ref, db_ref, dpar_ref):
        @pl.when(pl.program_id(0) == 0)
        def _():
            dpar_ref[...] = jnp.zeros_like(dpar_ref)

        dh_v = dh_ref[...]
        dob = (par_ref[2:3, :] * dh_v).astype(bf16)
        do_ref[...] = dob
        dpar_ref[2:3, :] += _colsum(dh_v * po_ref[...].astype(f32))
        dm = _nt(dob, w_ref[...])
        da_ref[...] = dm[:, :HW]
        db_ref[...] = dm[:, HW:]

    row = pl.BlockSpec((tm, D), lambda i: (i, 0))
    half = pl.BlockSpec((tm, HW), lambda i: (i, 0))
    one = pl.BlockSpec((8, D), lambda i: (0, 0))
    return _pallas_hbm(
        body, name="outproj_bwd", grid=(T // tm,),
        in_specs=[row, row, pl.BlockSpec((D, D), lambda i: (0, 0)), one],
        out_specs=[row, half, half, one],
        out_shape=[SDS((T, D), bf16), SDS((T, HW), f32), SDS((T, HW), f32), SDS((8, D), f32)],
        compiler_params=_cp(("arbitrary",), 32),
    )(dh, po, w_out, par)


def _inproj_bwd_tok(dh, h, par, w_in_t, dsg, dqs, dks, dvs, tabs):
    tm = 256

    def body(dh_ref, h_ref, par_ref, w_ref, dsg_ref, dq0, dq1, dq2, dk0, dk1, dk2, dv0, dv1, dv2,
             cos_ref, slo_ref, shi_ref, dhin_ref, dp_ref, y_ref, dpar_ref):
        @pl.when(pl.program_id(0) == 0)
        def _():
            dpar_ref[...] = jnp.zeros_like(dpar_ref)

        cos, slo, shi = cos_ref[...], slo_ref[...], shi_ref[...]
        total = lambda a, b, c: a[...].astype(f32) + b[...].astype(f32) + c[...].astype(f32)
        dq = total(dq0, dq1, dq2) * 0.125
        dk = total(dk0, dk1, dk2)
        dp_ref[:, :2 * HW] = dsg_ref[...]
        dp_ref[:, 2 * HW:3 * HW] = _rope_t(dq, cos, slo, shi).astype(bf16)
        dp_ref[:, 3 * HW:4 * HW] = _rope_t(dk, cos, slo, shi).astype(bf16)
        dp_ref[:, 4 * HW:] = total(dv0, dv1, dv2).astype(bf16)
        dy = _nn(dp_ref[...], w_ref[...])
        par_v, h_v = par_ref[...], h_ref[...]
        y_ref[...] = _norm_mod(h_v, par_v)[0].astype(bf16)
        dx, dsh, dsc, dng = _norm_mod_bwd(dy, h_v, par_v)
        dhin_ref[...] = dh_ref[...] + dx
        dpar_ref[0:1, :] += dsh
        dpar_ref[1:2, :] += dsc
        dpar_ref[3:4, :] += dng

    row = pl.BlockSpec((tm, D), lambda i: (i, 0))
    half = pl.BlockSpec((tm, HW), lambda i: (i, 0))
    one = pl.BlockSpec((8, D), lambda i: (0, 0))
    return _pallas_hbm(
        body, name="inproj_bwd_tok", grid=(T // tm,),
        in_specs=[row, row, one, pl.BlockSpec((INW, D), lambda i: (0, 0)), row] + [half] * 12,
        out_specs=[row, pl.BlockSpec((tm, INW), lambda i: (i, 0)), row, one],
        out_shape=[SDS((T, D), f32), SDS((T, INW), bf16), SDS((T, D), bf16), SDS((8, D), f32)],
        compiler_params=_cp(("arbitrary",), 48),
    )(dh, h, par, w_in_t, dsg, *dqs, *dks, *dvs, *tabs)


def _loss_head(h, par, target):
    tm = 512

    def body(h_ref, par_ref, t_ref, dh_ref, acc_ref):
        @pl.when(pl.program_id(0) == 0)
        def _():
            acc_ref[...] = jnp.zeros_like(acc_ref)

        x, g = h_ref[...], par_ref[3:4, :]
        rstd = lax.rsqrt(_rowmean(x * x) + EPS)
        xhat = x * rstd
        err = xhat * g - t_ref[...]
        acc_ref[0:1, :] += _colsum(err * err)
        dy = err * (1.0 / D)
        acc_ref[3:4, :] += _colsum(dy * xhat)
        dxhat = dy * g
        dh_ref[...] = rstd * (dxhat - xhat * _rowmean(dxhat * xhat))

    row = pl.BlockSpec((tm, D), lambda i: (i, 0))
    one = pl.BlockSpec((8, D), lambda i: (0, 0))
    return _pallas_hbm(
        body, name="loss_head", grid=(T // tm,), in_specs=[row, one, row], out_specs=[row, one],
        out_shape=[SDS((T, D), f32), SDS((8, D), f32)], compiler_params=_cp(("arbitrary",), 32),
    )(h, par, target)


ELEMENTWISE_VMEM_BUDGET = 20 << 20


def _block_rows(rows, bytes_per_row):
    cap = ELEMENTWISE_VMEM_BUDGET // bytes_per_row
    if rows <= cap:
        return rows
    return next(b for b in range(cap - cap % 16, 0, -16) if rows % b == 0)


def _sum_slots(land, name):
    _, R, C = land.shape
    br = _block_rows(R, 2 * NDEV * C * land.dtype.itemsize + 2 * C * 4)

    def body(l_ref, o_ref):
        acc = l_ref[0].astype(f32)
        for s in range(1, NDEV):
            acc = acc + l_ref[s].astype(f32)
        o_ref[...] = acc

    return _pallas_hbm(
        body, name=name, grid=(R // br,), in_specs=[pl.BlockSpec((NDEV, br, C), lambda i: (0, i, 0))],
        out_specs=pl.BlockSpec((br, C), lambda i: (i, 0)), out_shape=SDS((R, C), f32),
        compiler_params=_cp(("arbitrary",), 32),
    )(land)


def _sum_partials(land, own, me, name):
    r = land.shape[-2]

    def body(me_ref, l_ref, own_ref, o_ref):
        mine = own_ref[...].astype(f32)
        acc = jnp.where(me_ref[0] == 0, mine, l_ref[0].astype(f32))
        for s in range(1, NDEV):
            acc = acc + jnp.where(me_ref[0] == s, mine, l_ref[s].astype(f32))
        o_ref[...] = acc

    if own.ndim == 3:
        kk = own.shape[0]
        specs = dict(grid=(kk,),
                     in_specs=[pl.BlockSpec((NDEV, None, r, D), lambda i, me_ref: (0, i, 0, 0)),
                               pl.BlockSpec((None, r, D), lambda i, me_ref: (i, me_ref[0], 0))],
                     out_specs=pl.BlockSpec((None, r, D), lambda i, me_ref: (i, 0, 0)))
        out_shape = SDS((kk, r, D), f32)
    else:
        specs = dict(grid=(1,),
                     in_specs=[pl.BlockSpec((NDEV, r, D), lambda i, me_ref: (0, 0, 0)),
                               pl.BlockSpec((r, D), lambda i, me_ref: (me_ref[0], 0))],
                     out_specs=pl.BlockSpec((r, D), lambda i, me_ref: (0, 0)))
        out_shape = SDS((r, D), f32)
    return _pallas_hbm(
        body, name=name, out_shape=out_shape,
        grid_spec=pltpu.PrefetchScalarGridSpec(num_scalar_prefetch=1, **specs),
        compiler_params=_cp(("arbitrary",), 32),
    )(me.reshape(1), land, own)


def _adamw(w, g, m, v, name):
    R, C = w.shape
    br = _block_rows(R, 2 * 7 * C * 4)

    def body(w_ref, g_ref, m_ref, v_ref, d_ref, mo_ref, vo_ref):
        gv = g_ref[...]
        m2 = B1 * m_ref[...] + (1.0 - B1) * gv
        v2 = B2 * v_ref[...] + (1.0 - B2) * (gv * gv)
        mo_ref[...] = m2
        vo_ref[...] = v2
        m_hat = m2 / (1.0 - B1 ** STEP)
        v_hat = v2 / (1.0 - B2 ** STEP)
        d_ref[...] = -LR * (m_hat / (jnp.sqrt(v_hat) + AEPS) + WD * w_ref[...])

    blk = pl.BlockSpec((br, C), lambda i: (i, 0))
    return _pallas_hbm(
        body, name=name, grid=(R // br,), in_specs=[blk] * 4, out_specs=[blk] * 3,
        out_shape=[SDS((R, C), f32)] * 3, compiler_params=_cp(("arbitrary",), 32),
    )(w, g, m, v)


def _adamw_nd(w, g, m, v, name):
    shp = w.shape
    r2 = (-1, shp[-1]) if w.ndim > 1 else (8, shp[0] // 8)
    outs = _adamw(w.reshape(r2), g.reshape(r2), m.reshape(r2), v.reshape(r2), name)
    return [o.reshape(shp) for o in outs]


def _par_rows(mod_l, s, gain):
    rows = jnp.pad(mod_l.reshape(9, D)[3 * s:3 * s + 3], ((0, 5), (0, 0)))
    return rows + jnp.pad(gain[None, :], ((3, 4), (0, 0)))


def _pad_rows(a):
    a = a.reshape(-1, CH)
    return jnp.pad(a, ((0, (-a.shape[0]) % 8), (0, 0)))


def _prepare(c, ada_w, ada_b, norm_g):
    me = 4 * lax.axis_index("x") + 2 * lax.axis_index("y") + lax.axis_index("c")

    pay = jnp.pad(c, ((0, 7), (0, 0)))
    pay = jnp.concatenate([pay, jnp.pad(norm_g.reshape(6, OUTS), ((0, 2), (0, D - OUTS)))], axis=0)
    got = _all_gather_small(pay, "gather_c")
    c_all = got[:, 0, :]
    gains = got[:, 8:14, :OUTS].transpose(1, 0, 2).reshape(DEPTH, 3, D)

    ada_b_loc = lax.dynamic_slice(ada_b, (0, me * ADAS), (DEPTH, ADAS)).reshape(DEPTH, 1, ADAS)
    mod_cols = _mod_fwd(c_all, ada_w, ada_b_loc)
    got = _all_gather_small(mod_cols.reshape(DEPTH * NDEV, ADAS), "gather_mod").reshape(NDEV, DEPTH, NDEV, ADAS)
    mod = lax.dynamic_index_in_dim(got, me, axis=2, keepdims=False).transpose(1, 0, 2).reshape(DEPTH, 9 * D)
    pars = [[_par_rows(mod[l], s, gains[l, s]) for s in range(3)] for l in range(DEPTH)]
    return me, c_all, pars


def _fwd_bwd(x2, target, pars, get_w, put_g, small_ready, sgu_ln_g, sgu_ln_b, sgu_w, sgu_b, final_g):
    tabs = _rope_tables()
    tril = jnp.tril(jnp.ones((CH, CH), dtype=bool))
    ws_m = jnp.where(tril[None, None], sgu_w, 0.0).astype(bf16)
    ws_mt = jnp.swapaxes(ws_m, -1, -2)
    behind = lambda rows, token: rows if token is None else rows + token[0, 0]

    h = x2
    saved = []
    for l in range(DEPTH):
        lng, lnb = sgu_ln_g[l].reshape(1, HW), sgu_ln_b[l].reshape(1, HW)
        bs_t = sgu_b[l].T
        h0 = h
        w, token = get_w(4 * l, h0)
        h1, fo1, g1, u1, a1 = _ffn_fwd(h0, behind(pars[l][0], token), w)
        w, token = get_w(4 * l + 2, h1)
        ua, va, q, k, v = _inproj_fwd(h1, behind(pars[l][1], token), w, tabs)
        branches, token = [], None
        for d in PATTERN_DILATIONS:
            branches.append(_attn_fwd(q, k, v, d, token))
            if len(branches) == 2 and l + 1 < DEPTH:
                token = get_w(4 * l + 1, branches[-1][1])[1]
        o, lse = _attn_combine([b[0] for b in branches], [b[1] for b in branches])
        mixed = _sgu_fwd(ua, va, o, lng, lnb, ws_m[l], bs_t)
        w, token = get_w(4 * l + 3, mixed)
        h2, po = _outproj_fwd(mixed, w, h1, behind(pars[l][1], token))
        w, token = get_w(4 * l + 1, h2)
        h3, fo2, g2, u2, a2 = _ffn_fwd(h2, behind(pars[l][2], token), w)
        saved.append((h0, h1, h2, fo1, g1, u1, a1, ua, va, q, k, v, o, mixed, lse, po, fo2, g2, u2, a2))
        h = h3

    par_f = jnp.pad(final_g[None, :], ((3, 4), (0, 0)))
    dh, head = _loss_head(h, par_f, target)

    dmods, dgains, dsgu = [None] * DEPTH, [None] * DEPTH, [None] * DEPTH
    token = None
    for l in reversed(range(DEPTH)):
        w_f1, w_f2, w_i, w_o = (get_w(4 * l + j, None)[0] for j in (0, 1, 2, 3))
        h0, h1, h2, fo1, g1, u1, a1, ua, va, q, k, v, o, mixed, lse, po, fo2, g2, u2, a2 = saved[l]
        lng, lnb = sgu_ln_g[l].reshape(1, HW), sgu_ln_b[l].reshape(1, HW)
        bs_t = sgu_b[l].T

        dh, dg_s, du_s, y, dfb, dpar3 = _ffn_bwd_tok(dh, h2, behind(pars[l][2], token), fo2, g2, u2, w_f2)
        token = put_g(4 * l + 1, _ffn_bwd_w(a2, dg_s, du_s, y, dfb))

        dob, d_a, d_b, dpar2g = _outproj_bwd(dh, po, w_o, behind(pars[l][1], token))
        token = put_g(4 * l + 3, _tn_matmul(mixed, dob, 512, "w_out_grad"))
        parts = [_attn_bwd(q, k, v, o, d_b, lse, d) for d in PATTERN_DILATIONS]
        dsg, dln, dws, dbl = _sgu_bwd(ua, va, d_a, lng, lnb, ws_m[l], ws_mt[l], bs_t)
        dh, dp, y2, dpar2 = _inproj_bwd_tok(dh, h1, behind(pars[l][1], token), w_i, dsg, [p[0] for p in parts],
                                            [p[1] for p in parts], [p[2] for p in parts], tabs)
        token = put_g(4 * l + 2, _tn_matmul(dp, y2, 640, "w_in_grad"))

        dh, dg_s, du_s, y, dfb, dpar1 = _ffn_bwd_tok(dh, h0, behind(pars[l][0], token), fo1, g1, u1, w_f1)
        dmods[l] = jnp.concatenate([dpar1[0:3], dpar2[0:2], dpar2g[2:3], dpar3[0:3]], axis=0).reshape(9 * D)
        dgains[l] = jnp.stack([dpar1[3], dpar2[3], dpar3[3]])
        dsgu[l] = (dln[0], dln[1], jnp.where(tril[None], dws, 0.0), jnp.sum(dbl, axis=-1))
        token = small_ready(head, dmods, dgains, dsgu) if l == 0 else None
        token = put_g(4 * l, _ffn_bwd_w(a1, dg_s, du_s, y, dfb, token))
    return dh, token


def kernel(x, c, ada_w, ada_b, norm_g, ffn1_wg, ffn1_wu, ffn1_wd, ffn2_wg, ffn2_wu, ffn2_wd, w_in, sgu_ln_g, sgu_ln_b, sgu_w, sgu_b, w_out, final_g, loss_target, m_ada_w, m_ada_b, m_norm_g, m_ffn1_wg, m_ffn1_wu, m_ffn1_wd, m_ffn2_wg, m_ffn2_wu, m_ffn2_wd, m_w_in, m_sgu_ln_g, m_sgu_ln_b, m_sgu_w, m_sgu_b, m_w_out, m_final_g, v_ada_w, v_ada_b, v_norm_g, v_ffn1_wg, v_ffn1_wu, v_ffn1_wd, v_ffn2_wg, v_ffn2_wu, v_ffn2_wd, v_w_in, v_sgu_ln_g, v_sgu_ln_b, v_sgu_w, v_sgu_b, v_w_out, v_final_g):
    me, c_all, pars = _prepare(c, ada_w, ada_b, norm_g)

    tr = lambda w: jnp.swapaxes(w, -1, -2).astype(bf16)
    locs = []
    for l in range(DEPTH):
        locs.append(jnp.stack([tr(ffn1_wg[l]), tr(ffn1_wu[l]), ffn1_wd[l].astype(bf16)]))
        locs.append(jnp.stack([tr(ffn2_wg[l]), tr(ffn2_wu[l]), ffn2_wd[l].astype(bf16)]))
        locs.append(tr(w_in[l]))
        locs.append(w_out[l].astype(bf16))
    locs, pars = lax.optimization_barrier((locs, pars))

    first, locs = lax.optimization_barrier((_all_gather_rows([locs[0]], "gather_first")[0], locs))
    placed = [_place_own(a, me) for a in locs[1:]]
    ready = {0: first}
    groups = ([2, 3], [1], [4], [6, 7], [5])
    flying = {}

    def start_group(gi, behind):
        pieces = groups[gi]
        arrays, _ = lax.optimization_barrier(([placed[p - 1] for p in pieces], behind))
        started = _exchange_start(False, arrays, "gather_start_%d" % gi)
        flying.update({p: (gi, started[:-1]) for p in pieces})
        return started[-1]

    first_token = start_group(0, first)

    def get_w(piece, after):
        token = first_token if piece == 0 else None
        if piece not in ready:
            gi, started = flying[piece]
            got_w = _exchange_wait(False, started, after, "gather_wait_%d" % gi)
            ready.update(zip(groups[gi], got_w))
            if gi + 1 < len(groups):
                token = start_group(gi + 1, got_w)
        return ready[piece], token

    sent = {}

    def put_g(piece, grad):
        land = lax.empty((NDEV,) + grad.shape[:-2] + (grad.shape[-2] // NDEV, D), bf16)
        started = _exchange_start(True, [grad, land], "scatter_start_%d" % piece)
        sent[piece] = started[:-1]
        return started[-1]

    small_sent = []

    def small_ready(head, dmods, dgains, dsgu):
        loss_part = 0.5 * jnp.sum(head[0]) / D
        small = jnp.concatenate([
            _pad_rows(jnp.stack(dmods)), _pad_rows(jnp.stack(dgains)),
            _pad_rows(jnp.stack([s[0] for s in dsgu])), _pad_rows(jnp.stack([s[1] for s in dsgu])),
            _pad_rows(jnp.stack([s[3] for s in dsgu])), _pad_rows(jnp.stack([s[2] for s in dsgu])),
            _pad_rows(head[3]), _pad_rows(jnp.pad(loss_part[None], (0, CH - 1)))], axis=0)
        started = _exchange_start(False, [_place_own(small, me, "place_small")], "small_start")
        small_sent.append(started[:-1])
        return started[-1]

    dh, last = _fwd_bwd(x[0], loss_target[0], pars, get_w, put_g, small_ready, sgu_ln_g, sgu_ln_b, sgu_w, sgu_b, final_g)
    grad_x = dh[None]

    got = _exchange_wait(False, small_sent[0], last, "small_wait")[0].reshape(NDEV, -1, CH)
    tot = _sum_slots(got, "sum_small")
    n_mod, n_gain, n_sw = DEPTH * 9 * D // CH, DEPTH * 3 * D // CH, DEPTH * 4 * CH
    offs = [0, n_mod, n_mod + n_gain, n_mod + n_gain + 8, n_mod + n_gain + 16, n_mod + n_gain + 24]
    g_ada_b = tot[offs[0]:offs[1]].reshape(DEPTH, 9 * D)
    g_gain_full = tot[offs[1]:offs[2]].reshape(DEPTH, 3, D)
    g_ln_g = tot[offs[2]:offs[3]].reshape(DEPTH, 4, CH)
    g_ln_b = tot[offs[3]:offs[4]].reshape(DEPTH, 4, CH)
    g_sb = tot[offs[4]:offs[5]].reshape(DEPTH, 4, CH)
    g_sw = tot[offs[5]:offs[5] + n_sw].reshape(DEPTH, 4, CH, CH)
    g_final = tot[offs[5] + n_sw:offs[5] + n_sw + 8].reshape(D)
    loss = tot[offs[5] + n_sw + 8, 0]
    g_norm = lax.dynamic_slice(g_gain_full, (0, 0, me * OUTS), (DEPTH, 3, OUTS))

    dmod_all = got[:, offs[0]:offs[1]].reshape(NDEV, DEPTH, 9 * D)
    dmod_cols = lax.dynamic_slice(dmod_all, (0, 0, me * ADAS), (NDEV, DEPTH, ADAS)).transpose(1, 0, 2)
    g_ada_w = _ada_grad((c_all * jax.nn.sigmoid(c_all)).T, dmod_cols)

    sums, after = {}, tot

    def collect(piece, after):
        own, land = _exchange_wait(True, sent[piece], after, "scatter_wait_%d" % piece)
        sums[piece] = _sum_partials(land, own, me, "sum_grads")
        return sums[piece]

    for piece in (5, 7, 6, 4, 1, 3, 2):
        after = collect(piece, after)
    back = lambda t: jnp.swapaxes(t, -1, -2)
    f2 = jnp.stack([sums[4 * l + 1] for l in range(DEPTH)])
    g_w_in = back(jnp.stack([sums[4 * l + 2] for l in range(DEPTH)]))
    g_w_out = jnp.stack([sums[4 * l + 3] for l in range(DEPTH)])

    ws = [ada_w, ada_b, norm_g, ffn1_wg, ffn1_wu, ffn1_wd, ffn2_wg, ffn2_wu, ffn2_wd, w_in, sgu_ln_g, sgu_ln_b, sgu_w,
          sgu_b, w_out, final_g]
    ms = [m_ada_w, m_ada_b, m_norm_g, m_ffn1_wg, m_ffn1_wu, m_ffn1_wd, m_ffn2_wg, m_ffn2_wu, m_ffn2_wd, m_w_in,
          m_sgu_ln_g, m_sgu_ln_b, m_sgu_w, m_sgu_b, m_w_out, m_final_g]
    vs = [v_ada_w, v_ada_b, v_norm_g, v_ffn1_wg, v_ffn1_wu, v_ffn1_wd, v_ffn2_wg, v_ffn2_wu, v_ffn2_wd, v_w_in,
          v_sgu_ln_g, v_sgu_ln_b, v_sgu_w, v_sgu_b, v_w_out, v_final_g]
    gw = [g_ada_w, g_ada_b, g_norm, None, None, None, back(f2[:, 0]), back(f2[:, 1]), f2[:, 2],
          g_w_in, g_ln_g, g_ln_b, g_sw, g_sb, g_w_out, g_final]
    upd = [None] * len(ws)
    for i in (1, 2, 10, 11, 12, 13, 15, 6, 7, 8, 9, 14, 0):
        upd[i] = _adamw_nd(ws[i], gw[i], ms[i], vs[i], "adamw")
    collect(0, upd[0][0])
    f1 = jnp.stack([sums[4 * l] for l in range(DEPTH)])
    gw[3:6] = [back(f1[:, 0]), back(f1[:, 1]), f1[:, 2]]
    for i in (3, 4, 5):
        upd[i] = _adamw_nd(ws[i], gw[i], ms[i], vs[i], "adamw")
    return (loss, grad_x, *gw, *[u[0] for u in upd], *[u[1] for u in upd], *[u[2] for u in upd])
```

```python
import functools
import math

import jax
import jax.numpy as jnp
from jax import lax
from jax.experimental import pallas as pl
from jax.experimental.pallas import tpu as pltpu

f32, bf16 = jnp.float32, jnp.bfloat16
SDS = jax.ShapeDtypeStruct

T, D, F = 4096, 1024, 2816
NDEV, DEPTH = 8, 2
HW = 512
INW = 5 * HW
FS, INS, OUTS, ADAS = F // NDEV, INW // NDEV, D // NDEV, 9 * D // NDEV
CH = 128
PATTERN_DILATIONS = (1, 4, 16)
ROPE_THETA = 10000.0
EPS = 1e-6
LR, B1, B2, AEPS, WD, STEP = 0.001, 0.9, 0.999, 1e-08, 0.01, 10
MESH = pl.DeviceIdType.MESH


def _cp(sems, vmem_mb):
    return pltpu.CompilerParams(dimension_semantics=sems, vmem_limit_bytes=vmem_mb << 20)


def _pallas_hbm(body, *, out_shape, **kw):
    typed = jax.tree.map(lambda s: pltpu.HBM(s.shape, s.dtype), out_shape)
    call = pl.pallas_call(body, out_shape=typed, **kw)

    def run(*operands):
        pin = lambda x: x if x.dtype == jnp.int32 else pltpu.with_memory_space_constraint(x, pltpu.HBM)
        return call(*[pin(x) for x in operands])

    return run


def _nn(a, b):
    return lax.dot_general(a, b, (((1,), (0,)), ((), ())), preferred_element_type=f32)


def _nt(a, b):
    return lax.dot_general(a, b, (((1,), (1,)), ((), ())), preferred_element_type=f32)


def _tn(a, b):
    return lax.dot_general(a, b, (((0,), (0,)), ((), ())), preferred_element_type=f32)


def _colsum(a):
    return jnp.sum(a, axis=0, keepdims=True)


def _rowmean(a):
    return jnp.mean(a, axis=-1, keepdims=True)


def _norm_mod(x, par):
    rstd = lax.rsqrt(_rowmean(x * x) + EPS)
    xhat = x * rstd
    n = xhat * par[3:4, :]
    y = n * (1.0 + par[1:2, :]) + par[0:1, :]
    return y, n, xhat, rstd


def _norm_mod_bwd(dy, x, par):
    _, n, xhat, rstd = _norm_mod(x, par)
    dn = dy * (1.0 + par[1:2, :])
    dxhat = dn * par[3:4, :]
    dx = rstd * (dxhat - xhat * _rowmean(dxhat * xhat))
    return dx, _colsum(dy), _colsum(dy * n), _colsum(dn * xhat)


_GK = math.sqrt(2.0 / math.pi)


def _gelu(x):
    return 0.5 * x * (1.0 + jnp.tanh(_GK * (x + 0.044715 * x * x * x)))


def _gelu_grad(x):
    t = jnp.tanh(_GK * (x + 0.044715 * x * x * x))
    return 0.5 * (1.0 + t) + 0.5 * x * (1.0 - t * t) * (_GK * (1.0 + 3.0 * 0.044715 * x * x))


def _rows(ref, idx, r):
    if len(ref.shape) == 3:
        return ref.at[:, pl.ds(idx * r, r), :]
    return ref.at[pl.ds(idx * r, r), :]


def _flip(v, bit):
    return 1 - v if bit else v


def _all_gather_small(x, name):
    R, C = x.shape

    def body(x_ref, out_ref, send_sems, recv_sems):
        mx, my, mc = lax.axis_index("x"), lax.axis_index("y"), lax.axis_index("c")
        me = 4 * mx + 2 * my + mc
        out_ref[me] = x_ref[...]
        sent = []
        for k in range(1, NDEV):
            peer = (_flip(mx, k & 4), _flip(my, k & 2), _flip(mc, k & 1))
            cp = pltpu.make_async_remote_copy(
                src_ref=x_ref, dst_ref=out_ref.at[me], send_sem=send_sems.at[k - 1],
                recv_sem=recv_sems.at[k - 1], device_id=peer, device_id_type=MESH)
            cp.start()
            sent.append(cp)
        for k in range(1, NDEV):
            peer = (_flip(mx, k & 4), _flip(my, k & 2), _flip(mc, k & 1))
            pidx = 4 * peer[0] + 2 * peer[1] + peer[2]
            pltpu.make_async_remote_copy(
                src_ref=x_ref, dst_ref=out_ref.at[pidx], send_sem=send_sems.at[k - 1],
                recv_sem=recv_sems.at[k - 1], device_id=peer, device_id_type=MESH).wait_recv()
        for cp in sent:
            cp.wait_send()

    vm = pl.BlockSpec(memory_space=pltpu.VMEM)
    return pl.pallas_call(
        body, name=name, out_shape=SDS((NDEV, R, C), f32), in_specs=[vm], out_specs=vm,
        scratch_shapes=[pltpu.SemaphoreType.DMA((NDEV - 1,)), pltpu.SemaphoreType.DMA((NDEV - 1,))],
        compiler_params=pltpu.CompilerParams(vmem_limit_bytes=32 << 20),
    )(x)


def _all_gather_rows(locs, name):
    n = len(locs)
    rs = [a.shape[-2] for a in locs]

    def body(*refs):
        src, out = refs[:n], refs[n:2 * n]
        send_sems, recv_sems, loc_sems = refs[2 * n:]
        mx, my, mc = lax.axis_index("x"), lax.axis_index("y"), lax.axis_index("c")
        me, sib = (mx, my, mc), (mx, my, 1 - mc)
        chips = [(1 - mx, my), (mx, 1 - my), (1 - mx, 1 - my)]

        def blk(a, p):
            return _rows(out[a], 4 * p[0] + 2 * p[1] + p[2], rs[a])

        def copy(k, a, block, to, from_src=False):
            return pltpu.make_async_remote_copy(
                src_ref=src[a] if from_src else blk(a, block), dst_ref=blk(a, block),
                send_sem=send_sems.at[k * n + a], recv_sem=recv_sems.at[k * n + a],
                device_id=to, device_id_type=MESH)

        mine = [pltpu.make_async_copy(src[a], blk(a, me), loc_sems.at[a]) for a in range(n)]
        for m in mine:
            m.start()
        first = []
        for j, chip in enumerate(chips):
            first += [copy(1 + j, a, me, (*chip, mc), True) for a in range(n)]
        first += [copy(0, a, me, sib, True) for a in range(n)]
        for cp in first:
            cp.start()
        passed = []
        for j, chip in enumerate(chips):
            for a in range(n):
                copy(1 + j, a, (*chip, mc), me).wait_recv()
            fwd = [copy(4 + j, a, (*chip, mc), sib) for a in range(n)]
            for cp in fwd:
                cp.start()
            passed += fwd
        for a in range(n):
            copy(0, a, sib, me).wait_recv()
        for j, chip in enumerate(chips):
            for a in range(n):
                copy(4 + j, a, (*chip, 1 - mc), me).wait_recv()
        for cp in first + passed:
            cp.wait_send()
        for m in mine:
            m.wait()

    hbm = pl.BlockSpec(memory_space=pl.ANY)
    out_shape = [SDS(a.shape[:-2] + (NDEV * a.shape[-2], a.shape[-1]), a.dtype) for a in locs]
    return pl.pallas_call(
        body, name=name, out_shape=out_shape, in_specs=[hbm] * n, out_specs=[hbm] * n,
        scratch_shapes=[pltpu.SemaphoreType.DMA((7 * n,)), pltpu.SemaphoreType.DMA((7 * n,)),
                        pltpu.SemaphoreType.DMA((n,))],
    )(*locs)


HBM_SPEC = pl.BlockSpec(memory_space=pltpu.HBM)
SEM_SPEC = pl.BlockSpec(memory_space=pltpu.SEMAPHORE)
DATAFLOW_EFFECT = pltpu.SideEffectType.DATAFLOW_SIDE_EFFECTING


def _place_own(loc, me, name="place_own"):
    r, cols = loc.shape[-2:]
    loc3 = loc.reshape(-1, r, cols)
    kk = loc3.shape[0]

    def body(me_ref, src_ref, full_ref, out_ref):
        out_ref[...] = src_ref[...]

    out = _pallas_hbm(
        body, name=name, out_shape=SDS((kk, NDEV * r, cols), loc.dtype),
        grid_spec=pltpu.PrefetchScalarGridSpec(
            num_scalar_prefetch=1, grid=(kk,),
            in_specs=[pl.BlockSpec((None, r, cols), lambda i, me_ref: (i, 0, 0)), pl.BlockSpec(memory_space=pl.ANY)],
            out_specs=pl.BlockSpec((None, r, cols), lambda i, me_ref: (i, me_ref[0], 0))),
        input_output_aliases={2: 0}, compiler_params=_cp(("arbitrary",), 32),
    )(me.reshape(1), loc3, lax.empty((kk, NDEV * r, cols), loc.dtype))
    return out.reshape(loc.shape[:-2] + (NDEV * r, cols))


EXCHANGE_PEERS = {"gather": (1, 2, 3, 4, 5, 6, 7), "scatter": (1, 2, 3, 4, 5, 6, 7), "chips": (1, 2, 4, 6), "pass": (2, 4, 6)}


def _exchange_copies(mode, bufs, n, send_sems, recv_sems):
    mx, my, mc = lax.axis_index("x"), lax.axis_index("y"), lax.axis_index("c")
    me = 4 * mx + 2 * my + mc
    out = []
    for slot, k in enumerate(EXCHANGE_PEERS[mode]):
        peer = (_flip(mx, k & 4), _flip(my, k & 2), _flip(mc, k & 1))
        pidx = 4 * peer[0] + 2 * peer[1] + peer[2]
        for a in range(n):
            r = bufs[a].shape[-2] // NDEV
            if mode == "scatter":
                src, dst, arrive = _rows(bufs[a], pidx, r), bufs[n + a].at[me], bufs[n + a].at[pidx]
            elif mode == "pass":
                peer = (mx, my, 1 - mc)
                src, dst, arrive = _rows(bufs[a], pidx, r), _rows(bufs[a], pidx, r), _rows(bufs[a], pidx + 1 - 2 * mc, r)
            else:
                src, dst, arrive = _rows(bufs[a], me, r), _rows(bufs[a], me, r), _rows(bufs[a], pidx, r)
            sems = dict(send_sem=send_sems.at[slot * n + a], recv_sem=recv_sems.at[slot * n + a],
                        device_id=peer, device_id_type=MESH)
            out.append((pltpu.make_async_remote_copy(src_ref=src, dst_ref=dst, **sems),
                        pltpu.make_async_remote_copy(src_ref=src, dst_ref=arrive, **sems)))
    return out


def _exchange_start(mode, arrays, name):
    m = len(arrays)
    n = m // 2 if mode == "scatter" else m

    def body(*refs):
        send_sems, recv_sems, token = refs[m], refs[m + 1], refs[-1]
        for go, _ in _exchange_copies(mode, refs[:m], n, send_sems, recv_sems):
            go.start()
        token[...] = jnp.zeros_like(token)

    sems = pltpu.SemaphoreType.DMA((len(EXCHANGE_PEERS[mode]) * n,))
    return pl.pallas_call(
        body, name=name, in_specs=[HBM_SPEC] * m,
        out_shape=(sems, sems, *[pltpu.HBM(a.shape, a.dtype) for a in arrays], SDS((8, CH), f32)),
        out_specs=(SEM_SPEC, SEM_SPEC, *[HBM_SPEC] * m, pl.BlockSpec(memory_space=pltpu.VMEM)),
        input_output_aliases={a: 2 + a for a in range(m)},
        compiler_params=pltpu.CompilerParams(has_side_effects=DATAFLOW_EFFECT),
    )(*[pltpu.with_memory_space_constraint(a, pltpu.HBM) for a in arrays])


def _exchange_wait(mode, started, after, name):
    send_sems, recv_sems, *arrays = started
    m = len(arrays)
    n = m // 2 if mode == "scatter" else m

    def body(*refs):
        for go, arrive in _exchange_copies(mode, refs[:m], n, refs[m], refs[m + 1]):
            go.wait_send()
            arrive.wait_recv()

    return pl.pallas_call(
        body, name=name, in_specs=[HBM_SPEC] * m + [SEM_SPEC, SEM_SPEC, pl.BlockSpec(memory_space=pl.ANY)],
        out_shape=[pltpu.HBM(a.shape, a.dtype) for a in arrays], out_specs=[HBM_SPEC] * m,
        input_output_aliases={a: a for a in range(m)},
        compiler_params=pltpu.CompilerParams(has_side_effects=DATAFLOW_EFFECT),
    )(*arrays, send_sems, recv_sems, after)


def _mod_fwd(c_all, ada_w, ada_b_loc):
    def body(c_ref, w_ref, b_ref, o_ref):
        ca = c_ref[...]
        ca = ca * jax.nn.sigmoid(ca)
        o_ref[...] = jnp.dot(ca, w_ref[...], precision=lax.Precision.HIGHEST,
                             preferred_element_type=f32) + b_ref[...]

    return _pallas_hbm(
        body, name="mod_fwd", grid=(DEPTH,), out_shape=SDS((DEPTH, NDEV, ADAS), f32),
        in_specs=[pl.BlockSpec((NDEV, D), lambda l: (0, 0)),
                  pl.BlockSpec((None, D, ADAS), lambda l: (l, 0, 0)),
                  pl.BlockSpec((None, 1, ADAS), lambda l: (l, 0, 0))],
        out_specs=pl.BlockSpec((None, NDEV, ADAS), lambda l: (l, 0, 0)),
        compiler_params=_cp(("arbitrary",), 32),
    )(c_all, ada_w, ada_b_loc)


def _ada_grad(cact_t, dmod_cols):
    def body(c_ref, d_ref, o_ref):
        acc = c_ref[:, 0:1] * d_ref[0:1, :]
        for b in range(1, NDEV):
            acc = acc + c_ref[:, b:b + 1] * d_ref[b:b + 1, :]
        o_ref[...] = acc

    tr = 256
    return _pallas_hbm(
        body, name="ada_grad", grid=(DEPTH, D // tr), out_shape=SDS((DEPTH, D, ADAS), f32),
        in_specs=[pl.BlockSpec((tr, NDEV), lambda l, i: (i, 0)),
                  pl.BlockSpec((None, NDEV, ADAS), lambda l, i: (l, 0, 0))],
        out_specs=pl.BlockSpec((None, tr, ADAS), lambda l, i: (l, i, 0)),
        compiler_params=_cp(("arbitrary", "arbitrary"), 32),
    )(cact_t, dmod_cols)


def _ffn_fwd(h, par, w3):
    tm, tf = 1024, 256
    nj = F // tf

    def body(h_ref, par_ref, wg_ref, wu_ref, wd_ref, ho_ref, fo_ref, g_ref, u_ref, a_ref, y_scr, acc):
        j = pl.program_id(1)

        @pl.when(j == 0)
        def _():
            y_scr[...] = _norm_mod(h_ref[...], par_ref[...])[0].astype(bf16)
            acc[...] = jnp.zeros_like(acc)

        y = y_scr[...]
        g = _nt(y, wg_ref[...])
        u = _nt(y, wu_ref[...])
        a = ((g * jax.nn.sigmoid(g)) * u).astype(bf16)
        g_ref[...] = g.astype(bf16)
        u_ref[...] = u.astype(bf16)
        a_ref[...] = a
        acc[...] += _nn(a, wd_ref[...])

        @pl.when(j == nj - 1)
        def _():
            fo_ref[...] = acc[...].astype(bf16)
            ho_ref[...] = h_ref[...] + (0.5 * par_ref[2:3, :]) * acc[...]

    row = pl.BlockSpec((tm, D), lambda i, j: (i, 0))
    hid = pl.BlockSpec((tm, tf), lambda i, j: (i, j))
    wspec = [pl.BlockSpec((None, tf, D), lambda i, j, k=k: (k, j, 0)) for k in range(3)]
    return _pallas_hbm(
        body, name="ffn_fwd", grid=(T // tm, nj),
        in_specs=[row, pl.BlockSpec((8, D), lambda i, j: (0, 0))] + wspec,
        out_specs=[row, row, hid, hid, hid],
        out_shape=[SDS((T, D), f32), SDS((T, D), bf16), SDS((T, F), bf16), SDS((T, F), bf16), SDS((T, F), bf16)],
        scratch_shapes=[pltpu.VMEM((tm, D), bf16), pltpu.VMEM((tm, D), f32)],
        compiler_params=_cp(("arbitrary", "arbitrary"), 52),
    )(h, par, w3, w3, w3)


def _ffn_bwd_tok(dh, h, par, fo, gs, us, w3):
    tm, tf = 512, 256
    nj = F // tf

    def body(dh_ref, h_ref, par_ref, fo_ref, g_ref, u_ref, wg_ref, wu_ref, wd_ref,
             dhin_ref, dg_ref, du_ref, y_ref, dfb_ref, dpar_ref, df_scr, dyacc):
        i, j = pl.program_id(0), pl.program_id(1)

        @pl.when(jnp.logical_and(i == 0, j == 0))
        def _():
            dpar_ref[...] = jnp.zeros_like(dpar_ref)

        @pl.when(j == 0)
        def _():
            dh_v, par_v = dh_ref[...], par_ref[...]
            dfb = ((0.5 * par_v[2:3, :]) * dh_v).astype(bf16)
            df_scr[...] = dfb
            dfb_ref[...] = dfb
            dpar_ref[2:3, :] += 0.5 * _colsum(dh_v * fo_ref[...].astype(f32))
            y_ref[...] = _norm_mod(h_ref[...], par_v)[0].astype(bf16)
            dyacc[...] = jnp.zeros_like(dyacc)

        for half in range(2):
            rs = slice(half * (tm // 2), (half + 1) * (tm // 2))
            da = _nt(df_scr[rs, :], wd_ref[...])
            g = g_ref[rs, :].astype(f32)
            u = u_ref[rs, :].astype(f32)
            sig = jax.nn.sigmoid(g)
            du = (da * (g * sig)).astype(bf16)
            dg = (da * u * (sig * (1.0 + g * (1.0 - sig)))).astype(bf16)
            dg_ref[rs, :] = dg
            du_ref[rs, :] = du
            dyacc[rs, :] += _nn(dg, wg_ref[...]) + _nn(du, wu_ref[...])

        @pl.when(j == nj - 1)
        def _():
            dx, dsh, dsc, dng = _norm_mod_bwd(dyacc[...], h_ref[...], par_ref[...])
            dhin_ref[...] = dh_ref[...] + dx
            dpar_ref[0:1, :] += dsh
            dpar_ref[1:2, :] += dsc
            dpar_ref[3:4, :] += dng

    row = pl.BlockSpec((tm, D), lambda i, j: (i, 0))
    hid = pl.BlockSpec((tm, tf), lambda i, j: (i, j))
    one = pl.BlockSpec((8, D), lambda i, j: (0, 0))
    wspec = [pl.BlockSpec((None, tf, D), lambda i, j, k=k: (k, j, 0)) for k in range(3)]
    return _pallas_hbm(
        body, name="ffn_bwd_tok", grid=(T // tm, nj),
        in_specs=[row, row, one, row, hid, hid] + wspec,
        out_specs=[row, hid, hid, row, row, one],
        out_shape=[SDS((T, D), f32), SDS((T, F), bf16), SDS((T, F), bf16), SDS((T, D), bf16),
                   SDS((T, D), bf16), SDS((8, D), f32)],
        scratch_shapes=[pltpu.VMEM((tm, D), bf16), pltpu.VMEM((tm, D), f32)],
        compiler_params=_cp(("arbitrary", "arbitrary"), 48),
    )(dh, h, par, fo, gs, us, w3, w3, w3)


def _ffn_bwd_w(a_s, dg_s, du_s, y, dfb, anchor=None):
    tf, tk = F // 2, 256
    nk = T // tk

    def body(a_ref, dg_ref, du_ref, y_ref, df_ref, *rest):
        o_ref, accg, accu, accd = rest[-4:]
        kk = pl.program_id(1)

        @pl.when(kk == 0)
        def _():
            accg[...] = jnp.zeros_like(accg)
            accu[...] = jnp.zeros_like(accu)
            accd[...] = jnp.zeros_like(accd)

        yv = y_ref[...]
        accg[...] += _tn(dg_ref[...], yv)
        accu[...] += _tn(du_ref[...], yv)
        accd[...] += _tn(a_ref[...], df_ref[...])

        @pl.when(kk == nk - 1)
        def _():
            o_ref[0] = accg[...].astype(bf16)
            o_ref[1] = accu[...].astype(bf16)
            o_ref[2] = accd[...].astype(bf16)

    hid = pl.BlockSpec((tk, tf), lambda j, kk: (kk, j))
    row = pl.BlockSpec((tk, D), lambda j, kk: (kk, 0))
    extra = [] if anchor is None else [anchor]
    return _pallas_hbm(
        body, name="ffn_bwd_w", grid=(F // tf, nk),
        in_specs=[hid, hid, hid, row, row] + [pl.BlockSpec((8, CH), lambda j, kk: (0, 0))] * len(extra),
        out_specs=pl.BlockSpec((3, tf, D), lambda j, kk: (0, j, 0)), out_shape=SDS((3, F, D), bf16),
        scratch_shapes=[pltpu.VMEM((tf, D), f32)] * 3,
        compiler_params=_cp(("arbitrary", "arbitrary"), 48),
    )(a_s, dg_s, du_s, y, dfb, *extra)


def _tn_matmul(a, b, bm, name):
    M, N = a.shape[1], b.shape[1]
    tk = 512
    nk = T // tk

    def body(a_ref, b_ref, o_ref, acc):
        kk = pl.program_id(1)

        @pl.when(kk == 0)
        def _():
            acc[...] = jnp.zeros_like(acc)

        acc[...] += _tn(a_ref[...], b_ref[...])

        @pl.when(kk == nk - 1)
        def _():
            o_ref[...] = acc[...].astype(bf16)

    return _pallas_hbm(
        body, name=name, grid=(M // bm, nk),
        in_specs=[pl.BlockSpec((tk, bm), lambda i, kk: (kk, i)), pl.BlockSpec((tk, N), lambda i, kk: (kk, 0))],
        out_specs=pl.BlockSpec((bm, N), lambda i, kk: (i, 0)), out_shape=SDS((M, N), bf16),
        scratch_shapes=[pltpu.VMEM((bm, N), f32)],
        compiler_params=_cp(("arbitrary", "arbitrary"), 40),
    )(a, b)


def _rope_tables():
    inv = ROPE_THETA ** (-jnp.arange(0, 64, 2, dtype=f32) / 64)
    ang = jnp.arange(T, dtype=f32)[:, None] * inv[None, :]
    ang = jnp.concatenate([ang, ang], axis=-1)
    cos, sin = jnp.tile(jnp.cos(ang), (1, 2)), jnp.tile(jnp.sin(ang), (1, 2))
    low = (jnp.arange(CH) % 64 < 32)[None, :]
    return cos, jnp.where(low, -sin, 0.0), jnp.where(low, 0.0, sin)


def _all_heads(tab_ref):
    return jnp.concatenate([tab_ref[...]] * 4, axis=1)


def _rope(t, cos, sin_lo, sin_hi):
    return t * cos + pltpu.roll(t, HW - 32, 1) * sin_lo + pltpu.roll(t, 32, 1) * sin_hi


def _rope_t(g, cos, sin_lo, sin_hi):
    return g * cos + pltpu.roll(g * sin_lo, 32, 1) + pltpu.roll(g * sin_hi, HW - 32, 1)


def _inproj_fwd(h, par, w_in_t, tabs):
    tm = 512

    def body(h_ref, par_ref, w_ref, cos_ref, slo_ref, shi_ref, ua_ref, va_ref, q_ref, k_ref, v_ref):
        y = _norm_mod(h_ref[...], par_ref[...])[0].astype(bf16)
        proj = lambda c: _nt(y, w_ref[c * HW:(c + 1) * HW, :])
        ua_ref[...] = proj(0)
        va_ref[...] = proj(1)
        cos, slo, shi = _all_heads(cos_ref), _all_heads(slo_ref), _all_heads(shi_ref)
        q_ref[...] = _rope(proj(2), cos, slo, shi) * 0.125
        k_ref[...] = _rope(proj(3), cos, slo, shi)
        v_ref[...] = proj(4)

    row = pl.BlockSpec((tm, D), lambda i: (i, 0))
    half = pl.BlockSpec((tm, HW), lambda i: (i, 0))
    return _pallas_hbm(
        body, name="inproj_fwd", grid=(T // tm,),
        in_specs=[row, pl.BlockSpec((8, D), lambda i: (0, 0)), pl.BlockSpec((INW, D), lambda i: (0, 0))] +
                 [pl.BlockSpec((tm, CH), lambda i: (i, 0))] * 3,
        out_specs=[half] * 5,
        out_shape=[SDS((T, HW), f32)] * 5,
        compiler_params=_cp(("arbitrary",), 48),
    )(h, par, w_in_t, *tabs)


def _head_masks():
    lane = lax.broadcasted_iota(jnp.int32, (1, CH), 1)
    return lane < 64, lane >= 64


def _lane(t, idx):
    return jnp.sum(jnp.where(lax.broadcasted_iota(jnp.int32, t.shape, 1) == idx, t, 0.0), axis=-1, keepdims=True)


def _stack_heads(x):
    lo, hi = _head_masks()
    zero = jnp.zeros_like(x)
    return jnp.concatenate([jnp.where(lo, x, zero), jnp.where(hi, x, zero)], axis=0)


def _band_mask(has_prev):
    row = lax.broadcasted_iota(jnp.int32, (2 * CH, 2 * CH), 0) & (CH - 1)
    col = lax.broadcasted_iota(jnp.int32, (2 * CH, 2 * CH), 1)
    in_prev = jnp.logical_and(jnp.logical_and(col < CH, col >= row), has_prev)
    return jnp.logical_or(in_prev, jnp.logical_and(col >= CH, col - CH <= row))


def _attn_tiling(d):
    return CH * 16, 16 // d, CH


def _attn_fwd(q, k, v, d, anchor=None):
    rows, blocks, lanes = _attn_tiling(d)
    pairs = lanes // CH

    def body(q_ref, kc_ref, kp_ref, vc_ref, vp_ref, *rest):
        o_ref, lse_ref = rest[-2:]
        c, lb = pl.program_id(0), pl.program_id(1)
        col = lax.broadcasted_iota(jnp.int32, (CH, CH), 1)
        lo, _ = _head_masks()

        @pl.when(lb == 0)
        def _():
            lse_ref[...] = jnp.zeros_like(lse_ref)

        at = lambda r, b: pl.ds(r + b * CH * d, CH, stride=d) if d > 1 else pl.ds(b * CH, CH)
        for r in range(d):
            for b in range(blocks):
                own = at(r, b)
                k_prev, v_prev, before, mask = ((kc_ref, vc_ref, at(r, b - 1), _band_mask(True)) if b > 0 else
                                                (kp_ref, vp_ref, at(r, blocks - 1), _band_mask(c > 0)))
                lse_tile = lse_ref[own, :]
                for pi in range(pairs):
                    sl = slice(pi * CH, (pi + 1) * CH)
                    hp = lb * pairs + pi
                    kk = jnp.concatenate([k_prev[before, sl], kc_ref[own, sl]], axis=0).astype(bf16)
                    vv = jnp.concatenate([v_prev[before, sl], vc_ref[own, sl]], axis=0).astype(bf16)
                    s = jnp.where(mask, _nt(_stack_heads(q_ref[own, sl].astype(bf16)), kk), -jnp.inf)
                    m = jnp.max(s, axis=-1, keepdims=True)
                    p = jnp.exp(s - m)
                    den = jnp.sum(p, axis=-1, keepdims=True)
                    o = _nn(p.astype(bf16), vv) / den
                    o_ref[own, sl] = jnp.where(lo, o[:CH], o[CH:])
                    lse = m + jnp.log(den)
                    lse_tile = jnp.where(col == 2 * hp, lse[:CH], jnp.where(col == 2 * hp + 1, lse[CH:], lse_tile))
                lse_ref[own, :] = lse_tile

    cur = pl.BlockSpec((rows, lanes), lambda c, lb: (c, lb))
    prev = pl.BlockSpec((rows, lanes), lambda c, lb: (jnp.maximum(c - 1, 0), lb))
    extra = [] if anchor is None else [anchor]
    return _pallas_hbm(
        body, name="attn_fwd_d%d" % d, grid=(T // rows, HW // lanes),
        in_specs=[cur, cur, prev, cur, prev] + [pl.BlockSpec((8, CH), lambda c, lb: (0, 0))] * len(extra),
        out_specs=[cur, pl.BlockSpec((rows, CH), lambda c, lb: (c, 0))],
        out_shape=[SDS((T, HW), f32), SDS((T, CH), f32)],
        compiler_params=_cp(("arbitrary", "arbitrary"), 40),
    )(q, k, k, v, v, *extra)


def _attn_combine(os_, lses, anchor=None):
    tm = 512

    def body(o0_ref, o1_ref, o2_ref, l0_ref, l1_ref, l2_ref, *rest):
        o_ref, lse_ref = rest[-2:]
        l0, l1, l2 = l0_ref[...], l1_ref[...], l2_ref[...]
        m = jnp.maximum(jnp.maximum(l0, l1), l2)
        e = [jnp.exp(l0 - m), jnp.exp(l1 - m), jnp.exp(l2 - m)]
        s = e[0] + e[1] + e[2]
        w = [ei / s for ei in e]
        lse_ref[...] = m + jnp.log(s)
        lo, _ = _head_masks()
        for hp in range(4):
            sl = slice(hp * CH, (hp + 1) * CH)
            acc = jnp.zeros((tm, CH), f32)
            for wp, op_ref in zip(w, (o0_ref, o1_ref, o2_ref)):
                wexp = jnp.where(lo, wp[:, 2 * hp:2 * hp + 1], wp[:, 2 * hp + 1:2 * hp + 2])
                acc = acc + wexp * op_ref[:, sl].astype(f32)
            o_ref[:, sl] = acc

    half = pl.BlockSpec((tm, HW), lambda i: (i, 0))
    stat = pl.BlockSpec((tm, CH), lambda i: (i, 0))
    extra = [] if anchor is None else [anchor]
    return _pallas_hbm(
        body, name="attn_combine", grid=(T // tm,),
        in_specs=[half] * 3 + [stat] * 3 + [pl.BlockSpec((8, CH), lambda i: (0, 0))] * len(extra),
        out_specs=[half, stat], out_shape=[SDS((T, HW), f32), SDS((T, CH), f32)],
        compiler_params=_cp(("arbitrary",), 32),
    )(*os_, *lses, *extra)


def _attn_bwd(q, k, v, o, do, lse, d, running=None):
    rows, blocks, lanes = _attn_tiling(d)
    pairs = lanes // CH
    steps = T // rows

    def body(qc_ref, qn_ref, kc_ref, kp_ref, vc_ref, vp_ref, oc_ref, on_ref, dc_ref, dn_ref, lc_ref, ln_ref, *rest):
        dq_ref, dk_ref, dv_ref = rest[-3:]
        sofar = (lambda i, rows, sl: rest[i][rows, sl]) if running is not None else (lambda i, rows, sl: 0.0)
        c, lb = pl.program_id(0), pl.program_id(1)
        lo, _ = _head_masks()
        at = lambda r, b: pl.ds(r + b * CH * d, CH, stride=d) if d > 1 else pl.ds(b * CH, CH)
        for r in range(d):
            for b in range(blocks):
                own = at(r, b)
                k_prev, v_prev, before, mask_c = ((kc_ref, vc_ref, at(r, b - 1), _band_mask(True)) if b > 0 else
                                                  (kp_ref, vp_ref, at(r, blocks - 1), _band_mask(c > 0)))
                q_next, o_next, d_next, l_next, after, has_next = (
                    (qc_ref, oc_ref, dc_ref, lc_ref, at(r, b + 1), True) if b + 1 < blocks else
                    (qn_ref, on_ref, dn_ref, ln_ref, at(r, 0), c < steps - 1))
                mask_n = _band_mask(has_next)[:, :CH]
                lse_c, lse_n = lc_ref[own, :], l_next[after, :]
                for pi in range(pairs):
                    sl = slice(pi * CH, (pi + 1) * CH)
                    hp = lb * pairs + pi
                    kc, vc = kc_ref[own, sl].astype(bf16), vc_ref[own, sl].astype(bf16)
                    kk = jnp.concatenate([k_prev[before, sl].astype(bf16), kc], axis=0)
                    vv = jnp.concatenate([v_prev[before, sl].astype(bf16), vc], axis=0)
                    doc, don = dc_ref[own, sl], d_next[after, sl]
                    qs_c, qs_n = _stack_heads(qc_ref[own, sl].astype(bf16)), _stack_heads(q_next[after, sl].astype(bf16))
                    ds_c, ds_n = _stack_heads(doc.astype(bf16)), _stack_heads(don.astype(bf16))
                    delta_c = jnp.sum(_stack_heads(doc * oc_ref[own, sl]), axis=-1, keepdims=True)
                    delta_n = jnp.sum(_stack_heads(don * o_next[after, sl]), axis=-1, keepdims=True)
                    heads = lambda t: jnp.concatenate([_lane(t, 2 * hp), _lane(t, 2 * hp + 1)], axis=0)
                    p1 = jnp.where(mask_c, jnp.exp(_nt(qs_c, kk) - heads(lse_c)), 0.0)
                    g1 = (p1 * (_nt(ds_c, vv) - delta_c)).astype(bf16)
                    dq = _nn(g1, kk)
                    dq_ref[own, sl] = sofar(0, own, sl) + jnp.where(lo, dq[:CH], dq[CH:])
                    p2 = jnp.where(mask_n, jnp.exp(_nt(qs_n, kc) - heads(lse_n)), 0.0)
                    g2 = (p2 * (_nt(ds_n, vc) - delta_n)).astype(bf16)
                    dk_ref[own, sl] = sofar(1, own, sl) + _tn(jnp.concatenate([g1[:, CH:], g2], axis=0),
                                                              jnp.concatenate([qs_c, qs_n], axis=0))
                    dv_ref[own, sl] = sofar(2, own, sl) + _tn(
                        jnp.concatenate([p1[:, CH:].astype(bf16), p2.astype(bf16)], axis=0),
                        jnp.concatenate([ds_c, ds_n], axis=0))

    cur = pl.BlockSpec((rows, lanes), lambda c, lb: (c, lb))
    prev = pl.BlockSpec((rows, lanes), lambda c, lb: (jnp.maximum(c - 1, 0), lb))
    nxt = pl.BlockSpec((rows, lanes), lambda c, lb: (jnp.minimum(c + 1, steps - 1), lb))
    scur = pl.BlockSpec((rows, CH), lambda c, lb: (c, 0))
    snxt = pl.BlockSpec((rows, CH), lambda c, lb: (jnp.minimum(c + 1, steps - 1), 0))
    more = [] if running is None else list(running)
    return _pallas_hbm(
        body, name="attn_bwd_d%d" % d, grid=(steps, HW // lanes),
        in_specs=[cur, nxt, cur, prev, cur, prev, cur, nxt, cur, nxt, scur, snxt] + [cur] * len(more),
        out_specs=[cur] * 3, out_shape=[SDS((T, HW), f32)] * 3,
        input_output_aliases={12 + i: i for i in range(len(more))},
        compiler_params=_cp(("arbitrary", "arbitrary"), 52),
    )(q, q, k, k, v, v, o, o, do, do, lse, lse, *more)


def _causal(w):
    row = lax.broadcasted_iota(jnp.int32, (CH, CH), 0)
    col = lax.broadcasted_iota(jnp.int32, (CH, CH), 1)
    return jnp.where(col <= row, w, 0.0)


def _sgu_fwd(ua, va, o, lng, lnb, ws, bs_t):
    tm = 512

    def body(ua_ref, va_ref, o_ref, lg_ref, lb_ref, ws_ref, bs_ref, mix_ref):
        for hd in range(4):
            sl = slice(hd * CH, (hd + 1) * CH)
            w = _causal(ws_ref[hd]).astype(bf16)
            for cc in range(tm // CH):
                rs = slice(cc * CH, (cc + 1) * CH)
                u = _gelu(ua_ref[rs, sl])
                v = _gelu(va_ref[rs, sl])
                vc = v - _rowmean(v)
                vn = vc * lax.rsqrt(_rowmean(vc * vc) + EPS) * lg_ref[:, sl] + lb_ref[:, sl]
                z = _nn(w, vn.astype(bf16)) + bs_ref[:, hd:hd + 1]
                mix_ref[rs, sl] = (u * z).astype(bf16)
        mix_ref[:, HW:] = o_ref[...].astype(bf16)

    half = pl.BlockSpec((tm, HW), lambda i: (i, 0))
    vec = pl.BlockSpec((1, HW), lambda i: (0, 0))
    return _pallas_hbm(
        body, name="sgu_fwd", grid=(T // tm,),
        in_specs=[half, half, half, vec, vec, pl.BlockSpec((4, CH, CH), lambda i: (0, 0, 0)),
                  pl.BlockSpec((CH, 4), lambda i: (0, 0))],
        out_specs=pl.BlockSpec((tm, 2 * HW), lambda i: (i, 0)), out_shape=SDS((T, 2 * HW), bf16),
        compiler_params=_cp(("arbitrary",), 32),
    )(ua, va, o, lng, lnb, ws, bs_t)


def _sgu_bwd(ua, va, d_a, lng, lnb, ws, bs_t):
    tm = 512

    def body(ua_ref, va_ref, d_ref, lg_ref, lb_ref, ws_ref, bs_ref, dsg_ref, dln_ref, dws_ref, db_ref):
        @pl.when(pl.program_id(0) == 0)
        def _():
            dln_ref[...] = jnp.zeros_like(dln_ref)
            dws_ref[...] = jnp.zeros_like(dws_ref)
            db_ref[...] = jnp.zeros_like(db_ref)

        for hd in range(4):
            sl = slice(hd * CH, (hd + 1) * CH)
            w = _causal(ws_ref[hd])
            w, wt = w.astype(bf16), w.T.astype(bf16)
            lg = lg_ref[:, sl]
            for cc in range(tm // CH):
                rs = slice(cc * CH, (cc + 1) * CH)
                xa, xv, dd = ua_ref[rs, sl], va_ref[rs, sl], d_ref[rs, sl]
                u, v = _gelu(xa), _gelu(xv)
                vc = v - _rowmean(v)
                rstd = lax.rsqrt(_rowmean(vc * vc) + EPS)
                xh = vc * rstd
                vnb = (xh * lg + lb_ref[:, sl]).astype(bf16)
                z = _nn(w, vnb) + bs_ref[:, hd:hd + 1]
                dz = dd * u
                dzb = dz.astype(bf16)
                dsg_ref[rs, sl] = (dd * z * _gelu_grad(xa)).astype(bf16)
                dws_ref[hd] += _nt(dzb, vnb)
                db_ref[hd] += dz
                dvn = _nn(wt, dzb)
                dln_ref[0:1, sl] += _colsum(dvn * xh)
                dln_ref[1:2, sl] += _colsum(dvn)
                dxh = dvn * lg
                dv = rstd * (dxh - _rowmean(dxh) - xh * _rowmean(dxh * xh))
                dsg_ref[rs, HW + hd * CH:HW + (hd + 1) * CH] = (dv * _gelu_grad(xv)).astype(bf16)

    half = pl.BlockSpec((tm, HW), lambda i: (i, 0))
    vec = pl.BlockSpec((1, HW), lambda i: (0, 0))
    mat = pl.BlockSpec((4, CH, CH), lambda i: (0, 0, 0))
    return _pallas_hbm(
        body, name="sgu_bwd", grid=(T // tm,),
        in_specs=[half, half, half, vec, vec, mat, pl.BlockSpec((CH, 4), lambda i: (0, 0))],
        out_specs=[pl.BlockSpec((tm, 2 * HW), lambda i: (i, 0)), pl.BlockSpec((8, HW), lambda i: (0, 0)), mat, mat],
        out_shape=[SDS((T, 2 * HW), bf16), SDS((8, HW), f32), SDS((4, CH, CH), f32), SDS((4, CH, CH), f32)],
        compiler_params=_cp(("arbitrary",), 32),
    )(ua, va, d_a, lng, lnb, ws, bs_t)


def _outproj_fwd(mixed, w_out, h, par):
    tm = 512

    def body(mix_ref, w_ref, h_ref, par_ref, ho_ref, po_ref):
        p = _nn(mix_ref[...], w_ref[...])
        po_ref[...] = p.astype(bf16)
        ho_ref[...] = h_ref[...] + par_ref[2:3, :] * p

    row = pl.BlockSpec((tm, D), lambda i: (i, 0))
    return _pallas_hbm(
        body, name="outproj_fwd", grid=(T // tm,),
        in_specs=[row, pl.BlockSpec((D, D), lambda i: (0, 0)), row, pl.BlockSpec((8, D), lambda i: (0, 0))],
        out_specs=[row, row], out_shape=[SDS((T, D), f32), SDS((T, D), bf16)],
        compiler_params=_cp(("arbitrary",), 32),
    )(mixed, w_out, h, par)


def _outproj_bwd(dh, po, w_out, par):
    tm = 512

    def body(dh_ref, po_ref, w_ref, par_ref, do_ref, da_ref, db_ref, dpar_ref):
        @pl.when(pl.program_id(0) == 0)
        def _():
            dpar_ref[...] = jnp.zeros_like(dpar_ref)

        dh_v = dh_ref[...]
        dob = (par_ref[2:3, :] * dh_v).astype(bf16)
        do_ref[...] = dob
        dpar_ref[2:3, :] += _colsum(dh_v * po_ref[...].astype(f32))
        dm = _nt(dob, w_ref[...])
        da_ref[...] = dm[:, :HW]
        db_ref[...] = dm[:, HW:]

    row = pl.BlockSpec((tm, D), lambda i: (i, 0))
    half = pl.BlockSpec((tm, HW), lambda i: (i, 0))
    one = pl.BlockSpec((8, D), lambda i: (0, 0))
    return _pallas_hbm(
        body, name="outproj_bwd", grid=(T // tm,),
        in_specs=[row, row, pl.BlockSpec((D, D), lambda i: (0, 0)), one],
        out_specs=[row, half, half, one],
        out_shape=[SDS((T, D), bf16), SDS((T, HW), f32), SDS((T, HW), f32), SDS((8, D), f32)],
        compiler_params=_cp(("arbitrary",), 32),
    )(dh, po, w_out, par)


def _inproj_bwd_tok(dh, h, par, w_in_t, dsg, dq, dk, dv, tabs):
    tm = 256

    def body(dh_ref, h_ref, par_ref, w_ref, dsg_ref, dq_ref, dk_ref, dv_ref,
             cos_ref, slo_ref, shi_ref, dhin_ref, dp_ref, y_ref, dpar_ref):
        @pl.when(pl.program_id(0) == 0)
        def _():
            dpar_ref[...] = jnp.zeros_like(dpar_ref)

        cos, slo, shi = _all_heads(cos_ref), _all_heads(slo_ref), _all_heads(shi_ref)
        dp_ref[:, :2 * HW] = dsg_ref[...]
        dp_ref[:, 2 * HW:3 * HW] = _rope_t(dq_ref[...] * 0.125, cos, slo, shi).astype(bf16)
        dp_ref[:, 3 * HW:4 * HW] = _rope_t(dk_ref[...], cos, slo, shi).astype(bf16)
        dp_ref[:, 4 * HW:] = dv_ref[...].astype(bf16)
        dy = _nn(dp_ref[...], w_ref[...])
        par_v, h_v = par_ref[...], h_ref[...]
        y_ref[...] = _norm_mod(h_v, par_v)[0].astype(bf16)
        dx, dsh, dsc, dng = _norm_mod_bwd(dy, h_v, par_v)
        dhin_ref[...] = dh_ref[...] + dx
        dpar_ref[0:1, :] += dsh
        dpar_ref[1:2, :] += dsc
        dpar_ref[3:4, :] += dng

    row = pl.BlockSpec((tm, D), lambda i: (i, 0))
    half = pl.BlockSpec((tm, HW), lambda i: (i, 0))
    one = pl.BlockSpec((8, D), lambda i: (0, 0))
    return _pallas_hbm(
        body, name="inproj_bwd_tok", grid=(T // tm,),
        in_specs=[row, row, one, pl.BlockSpec((INW, D), lambda i: (0, 0)), row] + [half] * 3 +
                 [pl.BlockSpec((tm, CH), lambda i: (i, 0))] * 3,
        out_specs=[row, pl.BlockSpec((tm, INW), lambda i: (i, 0)), row, one],
        out_shape=[SDS((T, D), f32), SDS((T, INW), bf16), SDS((T, D), bf16), SDS((8, D), f32)],
        compiler_params=_cp(("arbitrary",), 48),
    )(dh, h, par, w_in_t, dsg, dq, dk, dv, *tabs)


def _loss_head(h, par, target):
    tm = 512

    def body(h_ref, par_ref, t_ref, dh_ref, acc_ref):
        @pl.when(pl.program_id(0) == 0)
        def _():
            acc_ref[...] = jnp.zeros_like(acc_ref)

        x, g = h_ref[...], par_ref[3:4, :]
        rstd = lax.rsqrt(_rowmean(x * x) + EPS)
        xhat = x * rstd
        err = xhat * g - t_ref[...]
        acc_ref[0:1, :] += _colsum(err * err)
        dy = err * (1.0 / D)
        acc_ref[3:4, :] += _colsum(dy * xhat)
        dxhat = dy * g
        dh_ref[...] = rstd * (dxhat - xhat * _rowmean(dxhat * xhat))

    row = pl.BlockSpec((tm, D), lambda i: (i, 0))
    one = pl.BlockSpec((8, D), lambda i: (0, 0))
    return _pallas_hbm(
        body, name="loss_head", grid=(T // tm,), in_specs=[row, one, row], out_specs=[row, one],
        out_shape=[SDS((T, D), f32), SDS((8, D), f32)], compiler_params=_cp(("arbitrary",), 32),
    )(h, par, target)


ELEMENTWISE_VMEM_BUDGET = 20 << 20


def _block_rows(rows, bytes_per_row):
    cap = ELEMENTWISE_VMEM_BUDGET // bytes_per_row
    if rows <= cap:
        return rows
    return next(b for b in range(cap - cap % 16, 0, -16) if rows % b == 0)


def _sum_slots(land, name):
    _, R, C = land.shape
    br = _block_rows(R, 2 * NDEV * C * land.dtype.itemsize + 2 * C * 4)

    def body(l_ref, o_ref):
        acc = l_ref[0].astype(f32)
        for s in range(1, NDEV):
            acc = acc + l_ref[s].astype(f32)
        o_ref[...] = acc

    return _pallas_hbm(
        body, name=name, grid=(R // br,), in_specs=[pl.BlockSpec((NDEV, br, C), lambda i: (0, i, 0))],
        out_specs=pl.BlockSpec((br, C), lambda i: (i, 0)), out_shape=SDS((R, C), f32),
        compiler_params=_cp(("arbitrary",), 32),
    )(land)


def _sum_partials(land, own, me, name):
    r = land.shape[-2]

    def body(me_ref, l_ref, own_ref, o_ref):
        mine = own_ref[...].astype(f32)
        acc = jnp.where(me_ref[0] == 0, mine, l_ref[0].astype(f32))
        for s in range(1, NDEV):
            acc = acc + jnp.where(me_ref[0] == s, mine, l_ref[s].astype(f32))
        o_ref[...] = acc

    if own.ndim == 3:
        kk = own.shape[0]
        specs = dict(grid=(kk,),
                     in_specs=[pl.BlockSpec((NDEV, None, r, D), lambda i, me_ref: (0, i, 0, 0)),
                               pl.BlockSpec((None, r, D), lambda i, me_ref: (i, me_ref[0], 0))],
                     out_specs=pl.BlockSpec((None, r, D), lambda i, me_ref: (i, 0, 0)))
        out_shape = SDS((kk, r, D), f32)
    else:
        specs = dict(grid=(1,),
                     in_specs=[pl.BlockSpec((NDEV, r, D), lambda i, me_ref: (0, 0, 0)),
                               pl.BlockSpec((r, D), lambda i, me_ref: (me_ref[0], 0))],
                     out_specs=pl.BlockSpec((r, D), lambda i, me_ref: (0, 0)))
        out_shape = SDS((r, D), f32)
    return _pallas_hbm(
        body, name=name, out_shape=out_shape,
        grid_spec=pltpu.PrefetchScalarGridSpec(num_scalar_prefetch=1, **specs),
        compiler_params=_cp(("arbitrary",), 32),
    )(me.reshape(1), land, own)


def _adamw(w, g, m, v, name):
    R, C = w.shape
    br = _block_rows(R, 2 * 7 * C * 4)

    def body(w_ref, g_ref, m_ref, v_ref, d_ref, mo_ref, vo_ref):
        gv = g_ref[...]
        m2 = B1 * m_ref[...] + (1.0 - B1) * gv
        v2 = B2 * v_ref[...] + (1.0 - B2) * (gv * gv)
        mo_ref[...] = m2
        vo_ref[...] = v2
        m_hat = m2 / (1.0 - B1 ** STEP)
        v_hat = v2 / (1.0 - B2 ** STEP)
        d_ref[...] = -LR * (m_hat / (jnp.sqrt(v_hat) + AEPS) + WD * w_ref[...])

    blk = pl.BlockSpec((br, C), lambda i: (i, 0))
    return _pallas_hbm(
        body, name=name, grid=(R // br,), in_specs=[blk] * 4, out_specs=[blk] * 3,
        out_shape=[SDS((R, C), f32)] * 3, compiler_params=_cp(("arbitrary",), 32),
    )(w, g, m, v)


def _adamw_nd(w, g, m, v, name):
    shp = w.shape
    r2 = (-1, shp[-1]) if w.ndim > 1 else (8, shp[0] // 8)
    outs = _adamw(w.reshape(r2), g.reshape(r2), m.reshape(r2), v.reshape(r2), name)
    return [o.reshape(shp) for o in outs]


def _par_rows(mod_l, s, gain):
    rows = jnp.pad(mod_l.reshape(9, D)[3 * s:3 * s + 3], ((0, 5), (0, 0)))
    return rows + jnp.pad(gain[None, :], ((3, 4), (0, 0)))


def _pad_rows(a):
    a = a.reshape(-1, CH)
    return jnp.pad(a, ((0, (-a.shape[0]) % 8), (0, 0)))


def _prepare(c, ada_w, ada_b, norm_g):
    me = 4 * lax.axis_index("x") + 2 * lax.axis_index("y") + lax.axis_index("c")

    pay = jnp.pad(c, ((0, 7), (0, 0)))
    pay = jnp.concatenate([pay, jnp.pad(norm_g.reshape(6, OUTS), ((0, 2), (0, D - OUTS)))], axis=0)
    got = _all_gather_small(pay, "gather_c")
    c_all = got[:, 0, :]
    gains = got[:, 8:14, :OUTS].transpose(1, 0, 2).reshape(DEPTH, 3, D)

    ada_b_loc = lax.dynamic_slice(ada_b, (0, me * ADAS), (DEPTH, ADAS)).reshape(DEPTH, 1, ADAS)
    mod_cols = _mod_fwd(c_all, ada_w, ada_b_loc)
    got = _all_gather_small(mod_cols.reshape(DEPTH * NDEV, ADAS), "gather_mod").reshape(NDEV, DEPTH, NDEV, ADAS)
    mod = lax.dynamic_index_in_dim(got, me, axis=2, keepdims=False).transpose(1, 0, 2).reshape(DEPTH, 9 * D)
    pars = [[_par_rows(mod[l], s, gains[l, s]) for s in range(3)] for l in range(DEPTH)]
    return me, c_all, pars


def _fwd_bwd(x2, target, pars, get_w, put_g, small_ready, sgu_ln_g, sgu_ln_b, sgu_w, sgu_b, final_g):
    tabs = _rope_tables()
    tril = jnp.tril(jnp.ones((CH, CH), dtype=bool))
    behind = lambda rows, token: rows if token is None else rows + token[0, 0]

    h = x2
    saved = []
    for l in range(DEPTH):
        lng, lnb = sgu_ln_g[l].reshape(1, HW), sgu_ln_b[l].reshape(1, HW)
        bs_t = sgu_b[l].T
        h0 = h
        w, token = get_w(4 * l, ("ffn1", l), h0)
        h1, fo1, g1, u1, a1 = _ffn_fwd(h0, behind(pars[l][0], token), w)
        w, token = get_w(4 * l + 2, ("inproj", l), h1)
        ua, va, q, k, v = _inproj_fwd(h1, behind(pars[l][1], token), w, tabs)
        branches = [_attn_fwd(q, k, v, d) for d in PATTERN_DILATIONS]
        token = get_w(None, ("attention", l), branches[-1][1])[1]
        o, lse = _attn_combine([b[0] for b in branches], [b[1] for b in branches], token)
        mixed = _sgu_fwd(ua, va, o, lng, lnb, sgu_w[l], bs_t)
        w, token = get_w(4 * l + 3, ("outproj", l), mixed)
        h2, po = _outproj_fwd(mixed, w, h1, behind(pars[l][1], token))
        w, token = get_w(4 * l + 1, ("ffn2", l), h2)
        h3, fo2, g2, u2, a2 = _ffn_fwd(h2, behind(pars[l][2], token), w)
        saved.append((h0, h1, h2, fo1, g1, u1, a1, ua, va, q, k, v, o, mixed, lse, po, fo2, g2, u2, a2))
        h = h3

    par_f = jnp.pad(final_g[None, :], ((3, 4), (0, 0)))
    dh, head = _loss_head(h, par_f, target)

    dmods, dgains, dsgu = [None] * DEPTH, [None] * DEPTH, [None] * DEPTH
    token = None
    for l in reversed(range(DEPTH)):
        w_f1, w_f2, w_i, w_o = (get_w(4 * l + j, None, None)[0] for j in (0, 1, 2, 3))
        h0, h1, h2, fo1, g1, u1, a1, ua, va, q, k, v, o, mixed, lse, po, fo2, g2, u2, a2 = saved[l]
        lng, lnb = sgu_ln_g[l].reshape(1, HW), sgu_ln_b[l].reshape(1, HW)
        bs_t = sgu_b[l].T

        dh, dg_s, du_s, y, dfb, dpar3 = _ffn_bwd_tok(dh, h2, behind(pars[l][2], token), fo2, g2, u2, w_f2)
        token = put_g(4 * l + 1, _ffn_bwd_w(a2, dg_s, du_s, y, dfb))

        dob, d_a, d_b, dpar2g = _outproj_bwd(dh, po, w_o, behind(pars[l][1], token))
        token = put_g(4 * l + 3, _tn_matmul(mixed, dob, 512, "w_out_grad"))
        dqkv = None
        for d in PATTERN_DILATIONS:
            dqkv = _attn_bwd(q, k, v, o, d_b, lse, d, dqkv)
        dsg, dln, dws, dbl = _sgu_bwd(ua, va, d_a, lng, lnb, sgu_w[l], bs_t)
        dh, dp, y2, dpar2 = _inproj_bwd_tok(dh, h1, behind(pars[l][1], token), w_i, dsg, *dqkv, tabs)
        token = put_g(4 * l + 2, _tn_matmul(dp, y2, 640, "w_in_grad"))

        dh, dg_s, du_s, y, dfb, dpar1 = _ffn_bwd_tok(dh, h0, behind(pars[l][0], token), fo1, g1, u1, w_f1)
        dmods[l] = jnp.concatenate([dpar1[0:3], dpar2[0:2], dpar2g[2:3], dpar3[0:3]], axis=0).reshape(9 * D)
        dgains[l] = jnp.stack([dpar1[3], dpar2[3], dpar3[3]])
        dsgu[l] = (dln[0], dln[1], jnp.where(tril[None], dws, 0.0), jnp.sum(dbl, axis=-1))
        token = small_ready(head, dmods, dgains, dsgu) if l == 0 else None
        token = put_g(4 * l, _ffn_bwd_w(a1, dg_s, du_s, y, dfb, token))
    return dh, token


def kernel(x, c, ada_w, ada_b, norm_g, ffn1_wg, ffn1_wu, ffn1_wd, ffn2_wg, ffn2_wu, ffn2_wd, w_in, sgu_ln_g, sgu_ln_b, sgu_w, sgu_b, w_out, final_g, loss_target, m_ada_w, m_ada_b, m_norm_g, m_ffn1_wg, m_ffn1_wu, m_ffn1_wd, m_ffn2_wg, m_ffn2_wu, m_ffn2_wd, m_w_in, m_sgu_ln_g, m_sgu_ln_b, m_sgu_w, m_sgu_b, m_w_out, m_final_g, v_ada_w, v_ada_b, v_norm_g, v_ffn1_wg, v_ffn1_wu, v_ffn1_wd, v_ffn2_wg, v_ffn2_wu, v_ffn2_wd, v_w_in, v_sgu_ln_g, v_sgu_ln_b, v_sgu_w, v_sgu_b, v_w_out, v_final_g):
    me, c_all, pars = _prepare(c, ada_w, ada_b, norm_g)

    tr = lambda w: jnp.swapaxes(w, -1, -2).astype(bf16)
    locs = []
    for l in range(DEPTH):
        locs.append(jnp.stack([tr(ffn1_wg[l]), tr(ffn1_wu[l]), ffn1_wd[l].astype(bf16)]))
        locs.append(jnp.stack([tr(ffn2_wg[l]), tr(ffn2_wu[l]), ffn2_wd[l].astype(bf16)]))
        locs.append(tr(w_in[l]))
        locs.append(w_out[l].astype(bf16))
    locs, pars = lax.optimization_barrier((locs, pars))

    first, locs = lax.optimization_barrier((_all_gather_rows([locs[0]], "gather_first")[0], locs))
    placed = [_place_own(a, me) for a in locs[1:]]
    ready = {0: first}
    groups = ([2, 3], [1], [4], [6, 7], [5])
    plan = {("inproj", 0): (("wait", "gather", 0), ("start", "chips", 1)),
            ("attention", 0): (("wait", "chips", 1), ("start", "pass", 1), ("start", "chips", 2)),
            ("ffn2", 0): (("wait", "pass", 1),),
            ("ffn1", 1): (("wait", "chips", 2), ("start", "pass", 2), ("start", "gather", 3), ("wait", "pass", 2)),
            ("inproj", 1): (("wait", "gather", 3), ("start", "chips", 4)),
            ("attention", 1): (("wait", "chips", 4), ("start", "pass", 4)),
            ("ffn2", 1): (("wait", "pass", 4),)}
    flying = {}

    def exchange(action, mode, gi, after):
        if action == "start":
            arrays = flying.pop(gi) if mode == "pass" else [placed[p - 1] for p in groups[gi]]
            arrays, _ = lax.optimization_barrier((arrays, after))
            started = _exchange_start(mode, arrays, "%s_start_%d" % (mode, gi))
            flying[gi] = started[:-1]
            return started[-1]
        arrays = _exchange_wait(mode, flying.pop(gi), after, "%s_wait_%d" % (mode, gi))
        if mode == "chips":
            flying[gi] = arrays
        else:
            ready.update(zip(groups[gi], arrays))
        return None

    first_token = exchange("start", "gather", 0, first)

    def get_w(piece, point, after):
        token = first_token if point == ("ffn1", 0) else None
        for action, mode, gi in plan.get(point, ()):
            started = exchange(action, mode, gi, after)
            if started is not None:
                token = after = started
        return ready.get(piece), token

    sent = {}

    def put_g(piece, grad):
        land = lax.empty((NDEV,) + grad.shape[:-2] + (grad.shape[-2] // NDEV, D), bf16)
        started = _exchange_start("scatter", [grad, land], "scatter_start_%d" % piece)
        sent[piece] = started[:-1]
        return started[-1]

    small_sent = []

    def small_ready(head, dmods, dgains, dsgu):
        loss_part = 0.5 * jnp.sum(head[0]) / D
        small = jnp.concatenate([
            _pad_rows(jnp.stack(dmods)), _pad_rows(jnp.stack(dgains)),
            _pad_rows(jnp.stack([s[0] for s in dsgu])), _pad_rows(jnp.stack([s[1] for s in dsgu])),
            _pad_rows(jnp.stack([s[3] for s in dsgu])), _pad_rows(jnp.stack([s[2] for s in dsgu])),
            _pad_rows(head[3]), _pad_rows(jnp.pad(loss_part[None], (0, CH - 1)))], axis=0)
        started = _exchange_start("gather", [_place_own(small, me, "place_small")], "small_start")
        small_sent.append(started[:-1])
        return started[-1]

    dh, last = _fwd_bwd(x[0], loss_target[0], pars, get_w, put_g, small_ready, sgu_ln_g, sgu_ln_b, sgu_w, sgu_b, final_g)
    grad_x = dh[None]

    got = _exchange_wait("gather", small_sent[0], last, "small_wait")[0].reshape(NDEV, -1, CH)
    tot = _sum_slots(got, "sum_small")
    n_mod, n_gain, n_sw = DEPTH * 9 * D // CH, DEPTH * 3 * D // CH, DEPTH * 4 * CH
    offs = [0, n_mod, n_mod + n_gain, n_mod + n_gain + 8, n_mod + n_gain + 16, n_mod + n_gain + 24]
    g_ada_b = tot[offs[0]:offs[1]].reshape(DEPTH, 9 * D)
    g_gain_full = tot[offs[1]:offs[2]].reshape(DEPTH, 3, D)
    g_ln_g = tot[offs[2]:offs[3]].reshape(DEPTH, 4, CH)
    g_ln_b = tot[offs[3]:offs[4]].reshape(DEPTH, 4, CH)
    g_sb = tot[offs[4]:offs[5]].reshape(DEPTH, 4, CH)
    g_sw = tot[offs[5]:offs[5] + n_sw].reshape(DEPTH, 4, CH, CH)
    g_final = tot[offs[5] + n_sw:offs[5] + n_sw + 8].reshape(D)
    loss = tot[offs[5] + n_sw + 8, 0]
    g_norm = lax.dynamic_slice(g_gain_full, (0, 0, me * OUTS), (DEPTH, 3, OUTS))

    dmod_all = got[:, offs[0]:offs[1]].reshape(NDEV, DEPTH, 9 * D)
    dmod_cols = lax.dynamic_slice(dmod_all, (0, 0, me * ADAS), (NDEV, DEPTH, ADAS)).transpose(1, 0, 2)
    g_ada_w = _ada_grad((c_all * jax.nn.sigmoid(c_all)).T, dmod_cols)

    sums, after = {}, tot

    def collect(piece, after):
        own, land = _exchange_wait("scatter", sent[piece], after, "scatter_wait_%d" % piece)
        sums[piece] = _sum_partials(land, own, me, "sum_grads")
        return sums[piece]

    for piece in (5, 7, 6, 4, 1, 3, 2):
        after = collect(piece, after)
    back = lambda t: jnp.swapaxes(t, -1, -2)
    f2 = jnp.stack([sums[4 * l + 1] for l in range(DEPTH)])
    g_w_in = back(jnp.stack([sums[4 * l + 2] for l in range(DEPTH)]))
    g_w_out = jnp.stack([sums[4 * l + 3] for l in range(DEPTH)])

    ws = [ada_w, ada_b, norm_g, ffn1_wg, ffn1_wu, ffn1_wd, ffn2_wg, ffn2_wu, ffn2_wd, w_in, sgu_ln_g, sgu_ln_b, sgu_w,
          sgu_b, w_out, final_g]
    ms = [m_ada_w, m_ada_b, m_norm_g, m_ffn1_wg, m_ffn1_wu, m_ffn1_wd, m_ffn2_wg, m_ffn2_wu, m_ffn2_wd, m_w_in,
          m_sgu_ln_g, m_sgu_ln_b, m_sgu_w, m_sgu_b, m_w_out, m_final_g]
    vs = [v_ada_w, v_ada_b, v_norm_g, v_ffn1_wg, v_ffn1_wu, v_ffn1_wd, v_ffn2_wg, v_ffn2_wu, v_ffn2_wd, v_w_in,
          v_sgu_ln_g, v_sgu_ln_b, v_sgu_w, v_sgu_b, v_w_out, v_final_g]
    gw = [g_ada_w, g_ada_b, g_norm, None, None, None, back(f2[:, 0]), back(f2[:, 1]), f2[:, 2],
          g_w_in, g_ln_g, g_ln_b, g_sw, g_sb, g_w_out, g_final]
    upd = [None] * len(ws)
    for i in (1, 2, 10, 11, 12, 13, 15, 6, 7, 8, 9, 14, 0):
        upd[i] = _adamw_nd(ws[i], gw[i], ms[i], vs[i], "adamw")
    collect(0, upd[0][0])
    f1 = jnp.stack([sums[4 * l] for l in range(DEPTH)])
    gw[3:6] = [back(f1[:, 0]), back(f1[:, 1]), f1[:, 2]]
    for i in (3, 4, 5):
        upd[i] = _adamw_nd(ws[i], gw[i], ms[i], vs[i], "adamw")
    return (loss, grad_x, *gw, *[u[0] for u in upd], *[u[1] for u in upd], *[u[2] for u in upd])
```

```python
import functools
import math

import jax
import jax.numpy as jnp
from jax import lax
from jax.experimental import pallas as pl
from jax.experimental.pallas import tpu as pltpu

f32, bf16 = jnp.float32, jnp.bfloat16
SDS = jax.ShapeDtypeStruct

T, D, F = 4096, 1024, 2816
NDEV, DEPTH = 8, 2
HW = 512
INW = 5 * HW
FS, INS, OUTS, ADAS = F // NDEV, INW // NDEV, D // NDEV, 9 * D // NDEV
CH = 128
PATTERN_DILATIONS = (1, 4, 16)
ROPE_THETA = 10000.0
EPS = 1e-6
LR, B1, B2, AEPS, WD, STEP = 0.001, 0.9, 0.999, 1e-08, 0.01, 10
MESH = pl.DeviceIdType.MESH


def _cp(sems, vmem_mb):
    return pltpu.CompilerParams(dimension_semantics=sems, vmem_limit_bytes=vmem_mb << 20)


def _pallas_hbm(body, *, out_shape, **kw):
    typed = jax.tree.map(lambda s: pltpu.HBM(s.shape, s.dtype), out_shape)
    call = pl.pallas_call(body, out_shape=typed, **kw)

    def run(*operands):
        pin = lambda x: x if x.dtype == jnp.int32 else pltpu.with_memory_space_constraint(x, pltpu.HBM)
        return call(*[pin(x) for x in operands])

    return run


def _nn(a, b):
    return lax.dot_general(a, b, (((1,), (0,)), ((), ())), preferred_element_type=f32)


def _nt(a, b):
    return lax.dot_general(a, b, (((1,), (1,)), ((), ())), preferred_element_type=f32)


def _tn(a, b):
    return lax.dot_general(a, b, (((0,), (0,)), ((), ())), preferred_element_type=f32)


def _colsum(a):
    return jnp.sum(a, axis=0, keepdims=True)


def _rowmean(a):
    return jnp.mean(a, axis=-1, keepdims=True)


def _norm_mod(x, par):
    rstd = lax.rsqrt(_rowmean(x * x) + EPS)
    xhat = x * rstd
    n = xhat * par[3:4, :]
    y = n * (1.0 + par[1:2, :]) + par[0:1, :]
    return y, n, xhat, rstd


def _norm_mod_bwd(dy, x, par):
    _, n, xhat, rstd = _norm_mod(x, par)
    dn = dy * (1.0 + par[1:2, :])
    dxhat = dn * par[3:4, :]
    dx = rstd * (dxhat - xhat * _rowmean(dxhat * xhat))
    return dx, _colsum(dy), _colsum(dy * n), _colsum(dn * xhat)


_GK = math.sqrt(2.0 / math.pi)


def _gelu(x):
    return 0.5 * x * (1.0 + jnp.tanh(_GK * (x + 0.044715 * x * x * x)))


def _gelu_grad(x):
    t = jnp.tanh(_GK * (x + 0.044715 * x * x * x))
    return 0.5 * (1.0 + t) + 0.5 * x * (1.0 - t * t) * (_GK * (1.0 + 3.0 * 0.044715 * x * x))


def _rows(ref, idx, r):
    if len(ref.shape) == 3:
        return ref.at[:, pl.ds(idx * r, r), :]
    return ref.at[pl.ds(idx * r, r), :]


def _flip(v, bit):
    return 1 - v if bit else v


def _all_gather_small(x, name):
    R, C = x.shape

    def body(x_ref, out_ref, send_sems, recv_sems):
        mx, my, mc = lax.axis_index("x"), lax.axis_index("y"), lax.axis_index("c")
        me = 4 * mx + 2 * my + mc
        out_ref[me] = x_ref[...]
        sent = []
        for k in range(1, NDEV):
            peer = (_flip(mx, k & 4), _flip(my, k & 2), _flip(mc, k & 1))
            cp = pltpu.make_async_remote_copy(
                src_ref=x_ref, dst_ref=out_ref.at[me], send_sem=send_sems.at[k - 1],
                recv_sem=recv_sems.at[k - 1], device_id=peer, device_id_type=MESH)
            cp.start()
            sent.append(cp)
        for k in range(1, NDEV):
            peer = (_flip(mx, k & 4), _flip(my, k & 2), _flip(mc, k & 1))
            pidx = 4 * peer[0] + 2 * peer[1] + peer[2]
            pltpu.make_async_remote_copy(
                src_ref=x_ref, dst_ref=out_ref.at[pidx], send_sem=send_sems.at[k - 1],
                recv_sem=recv_sems.at[k - 1], device_id=peer, device_id_type=MESH).wait_recv()
        for cp in sent:
            cp.wait_send()

    vm = pl.BlockSpec(memory_space=pltpu.VMEM)
    return pl.pallas_call(
        body, name=name, out_shape=SDS((NDEV, R, C), f32), in_specs=[vm], out_specs=vm,
        scratch_shapes=[pltpu.SemaphoreType.DMA((NDEV - 1,)), pltpu.SemaphoreType.DMA((NDEV - 1,))],
        compiler_params=pltpu.CompilerParams(vmem_limit_bytes=32 << 20),
    )(x)


def _all_gather_rows(locs, name):
    n = len(locs)
    rs = [a.shape[-2] for a in locs]

    def body(*refs):
        src, out = refs[:n], refs[n:2 * n]
        send_sems, recv_sems, loc_sems = refs[2 * n:]
        mx, my, mc = lax.axis_index("x"), lax.axis_index("y"), lax.axis_index("c")
        me, sib = (mx, my, mc), (mx, my, 1 - mc)
        chips = [(1 - mx, my), (mx, 1 - my), (1 - mx, 1 - my)]

        def blk(a, p):
            return _rows(out[a], 4 * p[0] + 2 * p[1] + p[2], rs[a])

        def copy(k, a, block, to, from_src=False):
            return pltpu.make_async_remote_copy(
                src_ref=src[a] if from_src else blk(a, block), dst_ref=blk(a, block),
                send_sem=send_sems.at[k * n + a], recv_sem=recv_sems.at[k * n + a],
                device_id=to, device_id_type=MESH)

        mine = [pltpu.make_async_copy(src[a], blk(a, me), loc_sems.at[a]) for a in range(n)]
        for m in mine:
            m.start()
        first = []
        for j, chip in enumerate(chips):
            first += [copy(1 + j, a, me, (*chip, mc), True) for a in range(n)]
        first += [copy(0, a, me, sib, True) for a in range(n)]
        for cp in first:
            cp.start()
        passed = []
        for j, chip in enumerate(chips):
            for a in range(n):
                copy(1 + j, a, (*chip, mc), me).wait_recv()
            fwd = [copy(4 + j, a, (*chip, mc), sib) for a in range(n)]
            for cp in fwd:
                cp.start()
            passed += fwd
        for a in range(n):
            copy(0, a, sib, me).wait_recv()
        for j, chip in enumerate(chips):
            for a in range(n):
                copy(4 + j, a, (*chip, 1 - mc), me).wait_recv()
        for cp in first + passed:
            cp.wait_send()
        for m in mine:
            m.wait()

    hbm = pl.BlockSpec(memory_space=pl.ANY)
    out_shape = [SDS(a.shape[:-2] + (NDEV * a.shape[-2], a.shape[-1]), a.dtype) for a in locs]
    return pl.pallas_call(
        body, name=name, out_shape=out_shape, in_specs=[hbm] * n, out_specs=[hbm] * n,
        scratch_shapes=[pltpu.SemaphoreType.DMA((7 * n,)), pltpu.SemaphoreType.DMA((7 * n,)),
                        pltpu.SemaphoreType.DMA((n,))],
    )(*locs)


HBM_SPEC = pl.BlockSpec(memory_space=pltpu.HBM)
SEM_SPEC = pl.BlockSpec(memory_space=pltpu.SEMAPHORE)
DATAFLOW_EFFECT = pltpu.SideEffectType.DATAFLOW_SIDE_EFFECTING


def _place_own(loc, me, name="place_own"):
    r, cols = loc.shape[-2:]
    loc3 = loc.reshape(-1, r, cols)
    kk = loc3.shape[0]

    def body(me_ref, src_ref, full_ref, out_ref):
        out_ref[...] = src_ref[...]

    out = _pallas_hbm(
        body, name=name, out_shape=SDS((kk, NDEV * r, cols), loc.dtype),
        grid_spec=pltpu.PrefetchScalarGridSpec(
            num_scalar_prefetch=1, grid=(kk,),
            in_specs=[pl.BlockSpec((None, r, cols), lambda i, me_ref: (i, 0, 0)), pl.BlockSpec(memory_space=pl.ANY)],
            out_specs=pl.BlockSpec((None, r, cols), lambda i, me_ref: (i, me_ref[0], 0))),
        input_output_aliases={2: 0}, compiler_params=_cp(("arbitrary",), 32),
    )(me.reshape(1), loc3, lax.empty((kk, NDEV * r, cols), loc.dtype))
    return out.reshape(loc.shape[:-2] + (NDEV * r, cols))


EXCHANGE_PEERS = {"gather": (1, 2, 3, 4, 5, 6, 7), "scatter": (1, 2, 3, 4, 5, 6, 7), "chips": (1, 2, 4, 6), "pass": (2, 4, 6)}


def _exchange_copies(mode, bufs, n, send_sems, recv_sems):
    mx, my, mc = lax.axis_index("x"), lax.axis_index("y"), lax.axis_index("c")
    me = 4 * mx + 2 * my + mc
    out = []
    for slot, k in enumerate(EXCHANGE_PEERS[mode]):
        peer = (_flip(mx, k & 4), _flip(my, k & 2), _flip(mc, k & 1))
        pidx = 4 * peer[0] + 2 * peer[1] + peer[2]
        for a in range(n):
            r = bufs[a].shape[-2] // NDEV
            if mode == "scatter":
                src, dst, arrive = _rows(bufs[a], pidx, r), bufs[n + a].at[me], bufs[n + a].at[pidx]
            elif mode == "pass":
                peer = (mx, my, 1 - mc)
                src, dst, arrive = _rows(bufs[a], pidx, r), _rows(bufs[a], pidx, r), _rows(bufs[a], pidx + 1 - 2 * mc, r)
            else:
                src, dst, arrive = _rows(bufs[a], me, r), _rows(bufs[a], me, r), _rows(bufs[a], pidx, r)
            sems = dict(send_sem=send_sems.at[slot * n + a], recv_sem=recv_sems.at[slot * n + a],
                        device_id=peer, device_id_type=MESH)
            out.append((pltpu.make_async_remote_copy(src_ref=src, dst_ref=dst, **sems),
                        pltpu.make_async_remote_copy(src_ref=src, dst_ref=arrive, **sems)))
    return out


def _exchange_start(mode, arrays, name):
    m = len(arrays)
    n = m // 2 if mode == "scatter" else m

    def body(*refs):
        send_sems, recv_sems, token = refs[m], refs[m + 1], refs[-1]
        for go, _ in _exchange_copies(mode, refs[:m], n, send_sems, recv_sems):
            go.start()
        token[...] = jnp.zeros_like(token)

    sems = pltpu.SemaphoreType.DMA((len(EXCHANGE_PEERS[mode]) * n,))
    return pl.pallas_call(
        body, name=name, in_specs=[HBM_SPEC] * m,
        out_shape=(sems, sems, *[pltpu.HBM(a.shape, a.dtype) for a in arrays], SDS((8, CH), f32)),
        out_specs=(SEM_SPEC, SEM_SPEC, *[HBM_SPEC] * m, pl.BlockSpec(memory_space=pltpu.VMEM)),
        input_output_aliases={a: 2 + a for a in range(m)},
        compiler_params=pltpu.CompilerParams(has_side_effects=DATAFLOW_EFFECT),
    )(*[pltpu.with_memory_space_constraint(a, pltpu.HBM) for a in arrays])


def _exchange_wait(mode, started, after, name):
    send_sems, recv_sems, *arrays = started
    m = len(arrays)
    n = m // 2 if mode == "scatter" else m

    def body(*refs):
        for go, arrive in _exchange_copies(mode, refs[:m], n, refs[m], refs[m + 1]):
            go.wait_send()
            arrive.wait_recv()

    return pl.pallas_call(
        body, name=name, in_specs=[HBM_SPEC] * m + [SEM_SPEC, SEM_SPEC, pl.BlockSpec(memory_space=pl.ANY)],
        out_shape=[pltpu.HBM(a.shape, a.dtype) for a in arrays], out_specs=[HBM_SPEC] * m,
        input_output_aliases={a: a for a in range(m)},
        compiler_params=pltpu.CompilerParams(has_side_effects=DATAFLOW_EFFECT),
    )(*arrays, send_sems, recv_sems, after)


def _mod_fwd(c_all, ada_w, ada_b_loc):
    def body(c_ref, w_ref, b_ref, o_ref):
        ca = c_ref[...]
        ca = ca * jax.nn.sigmoid(ca)
        o_ref[...] = jnp.dot(ca, w_ref[...], precision=lax.Precision.HIGHEST,
                             preferred_element_type=f32) + b_ref[...]

    return _pallas_hbm(
        body, name="mod_fwd", grid=(DEPTH,), out_shape=SDS((DEPTH, NDEV, ADAS), f32),
        in_specs=[pl.BlockSpec((NDEV, D), lambda l: (0, 0)),
                  pl.BlockSpec((None, D, ADAS), lambda l: (l, 0, 0)),
                  pl.BlockSpec((None, 1, ADAS), lambda l: (l, 0, 0))],
        out_specs=pl.BlockSpec((None, NDEV, ADAS), lambda l: (l, 0, 0)),
        compiler_params=_cp(("arbitrary",), 32),
    )(c_all, ada_w, ada_b_loc)


def _ada_grad(cact_t, dmod_cols):
    def body(c_ref, d_ref, o_ref):
        acc = c_ref[:, 0:1] * d_ref[0:1, :]
        for b in range(1, NDEV):
            acc = acc + c_ref[:, b:b + 1] * d_ref[b:b + 1, :]
        o_ref[...] = acc

    tr = 256
    return _pallas_hbm(
        body, name="ada_grad", grid=(DEPTH, D // tr), out_shape=SDS((DEPTH, D, ADAS), f32),
        in_specs=[pl.BlockSpec((tr, NDEV), lambda l, i: (i, 0)),
                  pl.BlockSpec((None, NDEV, ADAS), lambda l, i: (l, 0, 0))],
        out_specs=pl.BlockSpec((None, tr, ADAS), lambda l, i: (l, i, 0)),
        compiler_params=_cp(("arbitrary", "arbitrary"), 32),
    )(cact_t, dmod_cols)


def _ffn_fwd(h, par, w3):
    tm, tf = 1024, 256
    nj = F // tf

    def body(h_ref, par_ref, wg_ref, wu_ref, wd_ref, ho_ref, fo_ref, g_ref, u_ref, a_ref, y_scr, acc):
        j = pl.program_id(1)

        @pl.when(j == 0)
        def _():
            y_scr[...] = _norm_mod(h_ref[...], par_ref[...])[0].astype(bf16)
            acc[...] = jnp.zeros_like(acc)

        y = y_scr[...]
        g = _nt(y, wg_ref[...])
        u = _nt(y, wu_ref[...])
        a = ((g * jax.nn.sigmoid(g)) * u).astype(bf16)
        g_ref[...] = g.astype(bf16)
        u_ref[...] = u.astype(bf16)
        a_ref[...] = a
        acc[...] += _nn(a, wd_ref[...])

        @pl.when(j == nj - 1)
        def _():
            fo_ref[...] = acc[...].astype(bf16)
            ho_ref[...] = h_ref[...] + (0.5 * par_ref[2:3, :]) * acc[...]

    row = pl.BlockSpec((tm, D), lambda i, j: (i, 0))
    hid = pl.BlockSpec((tm, tf), lambda i, j: (i, j))
    wspec = [pl.BlockSpec((None, tf, D), lambda i, j, k=k: (k, j, 0)) for k in range(3)]
    return _pallas_hbm(
        body, name="ffn_fwd", grid=(T // tm, nj),
        in_specs=[row, pl.BlockSpec((8, D), lambda i, j: (0, 0))] + wspec,
        out_specs=[row, row, hid, hid, hid],
        out_shape=[SDS((T, D), f32), SDS((T, D), bf16), SDS((T, F), bf16), SDS((T, F), bf16), SDS((T, F), bf16)],
        scratch_shapes=[pltpu.VMEM((tm, D), bf16), pltpu.VMEM((tm, D), f32)],
        compiler_params=_cp(("arbitrary", "arbitrary"), 52),
    )(h, par, w3, w3, w3)


def _ffn_bwd_tok(dh, h, par, fo, gs, us, w3):
    tm, tf = 512, 256
    nj = F // tf

    def body(dh_ref, h_ref, par_ref, fo_ref, g_ref, u_ref, wg_ref, wu_ref, wd_ref,
             dhin_ref, dg_ref, du_ref, y_ref, dfb_ref, dpar_ref, df_scr, dyacc):
        i, j = pl.program_id(0), pl.program_id(1)

        @pl.when(jnp.logical_and(i == 0, j == 0))
        def _():
            dpar_ref[...] = jnp.zeros_like(dpar_ref)

        @pl.when(j == 0)
        def _():
            dh_v, par_v = dh_ref[...], par_ref[...]
            dfb = ((0.5 * par_v[2:3, :]) * dh_v).astype(bf16)
            df_scr[...] = dfb
            dfb_ref[...] = dfb
            dpar_ref[2:3, :] += 0.5 * _colsum(dh_v * fo_ref[...].astype(f32))
            y_ref[...] = _norm_mod(h_ref[...], par_v)[0].astype(bf16)
            dyacc[...] = jnp.zeros_like(dyacc)

        for half in range(2):
            rs = slice(half * (tm // 2), (half + 1) * (tm // 2))
            da = _nt(df_scr[rs, :], wd_ref[...])
            g = g_ref[rs, :].astype(f32)
            u = u_ref[rs, :].astype(f32)
            sig = jax.nn.sigmoid(g)
            du = (da * (g * sig)).astype(bf16)
            dg = (da * u * (sig * (1.0 + g * (1.0 - sig)))).astype(bf16)
            dg_ref[rs, :] = dg
            du_ref[rs, :] = du
            dyacc[rs, :] += _nn(dg, wg_ref[...]) + _nn(du, wu_ref[...])

        @pl.when(j == nj - 1)
        def _():
            dx, dsh, dsc, dng = _norm_mod_bwd(dyacc[...], h_ref[...], par_ref[...])
            dhin_ref[...] = dh_ref[...] + dx
            dpar_ref[0:1, :] += dsh
            dpar_ref[1:2, :] += dsc
            dpar_ref[3:4, :] += dng

    row = pl.BlockSpec((tm, D), lambda i, j: (i, 0))
    hid = pl.BlockSpec((tm, tf), lambda i, j: (i, j))
    one = pl.BlockSpec((8, D), lambda i, j: (0, 0))
    wspec = [pl.BlockSpec((None, tf, D), lambda i, j, k=k: (k, j, 0)) for k in range(3)]
    return _pallas_hbm(
        body, name="ffn_bwd_tok", grid=(T // tm, nj),
        in_specs=[row, row, one, row, hid, hid] + wspec,
        out_specs=[row, hid, hid, row, row, one],
        out_shape=[SDS((T, D), f32), SDS((T, F), bf16), SDS((T, F), bf16), SDS((T, D), bf16),
                   SDS((T, D), bf16), SDS((8, D), f32)],
        scratch_shapes=[pltpu.VMEM((tm, D), bf16), pltpu.VMEM((tm, D), f32)],
        compiler_params=_cp(("arbitrary", "arbitrary"), 48),
    )(dh, h, par, fo, gs, us, w3, w3, w3)


def _ffn_bwd_w(a_s, dg_s, du_s, y, dfb, anchor=None):
    tf, tk = F // 2, 256
    nk = T // tk

    def body(a_ref, dg_ref, du_ref, y_ref, df_ref, *rest):
        o_ref, accg, accu, accd = rest[-4:]
        kk = pl.program_id(1)

        @pl.when(kk == 0)
        def _():
            accg[...] = jnp.zeros_like(accg)
            accu[...] = jnp.zeros_like(accu)
            accd[...] = jnp.zeros_like(accd)

        yv = y_ref[...]
        accg[...] += _tn(dg_ref[...], yv)
        accu[...] += _tn(du_ref[...], yv)
        accd[...] += _tn(a_ref[...], df_ref[...])

        @pl.when(kk == nk - 1)
        def _():
            o_ref[0] = accg[...].astype(bf16)
            o_ref[1] = accu[...].astype(bf16)
            o_ref[2] = accd[...].astype(bf16)

    hid = pl.BlockSpec((tk, tf), lambda j, kk: (kk, j))
    row = pl.BlockSpec((tk, D), lambda j, kk: (kk, 0))
    extra = [] if anchor is None else [anchor]
    return _pallas_hbm(
        body, name="ffn_bwd_w", grid=(F // tf, nk),
        in_specs=[hid, hid, hid, row, row] + [pl.BlockSpec((8, CH), lambda j, kk: (0, 0))] * len(extra),
        out_specs=pl.BlockSpec((3, tf, D), lambda j, kk: (0, j, 0)), out_shape=SDS((3, F, D), bf16),
        scratch_shapes=[pltpu.VMEM((tf, D), f32)] * 3,
        compiler_params=_cp(("arbitrary", "arbitrary"), 48),
    )(a_s, dg_s, du_s, y, dfb, *extra)


def _tn_matmul(a, b, bm, name):
    M, N = a.shape[1], b.shape[1]
    tk = 512
    nk = T // tk

    def body(a_ref, b_ref, o_ref, acc):
        kk = pl.program_id(1)

        @pl.when(kk == 0)
        def _():
            acc[...] = jnp.zeros_like(acc)

        acc[...] += _tn(a_ref[...], b_ref[...])

        @pl.when(kk == nk - 1)
        def _():
            o_ref[...] = acc[...].astype(bf16)

    return _pallas_hbm(
        body, name=name, grid=(M // bm, nk),
        in_specs=[pl.BlockSpec((tk, bm), lambda i, kk: (kk, i)), pl.BlockSpec((tk, N), lambda i, kk: (kk, 0))],
        out_specs=pl.BlockSpec((bm, N), lambda i, kk: (i, 0)), out_shape=SDS((M, N), bf16),
        scratch_shapes=[pltpu.VMEM((bm, N), f32)],
        compiler_params=_cp(("arbitrary", "arbitrary"), 40),
    )(a, b)


def _rope_tables():
    inv = ROPE_THETA ** (-jnp.arange(0, 64, 2, dtype=f32) / 64)
    ang = jnp.arange(T, dtype=f32)[:, None] * inv[None, :]
    ang = jnp.concatenate([ang, ang], axis=-1)
    cos, sin = jnp.tile(jnp.cos(ang), (1, 2)), jnp.tile(jnp.sin(ang), (1, 2))
    low = (jnp.arange(CH) % 64 < 32)[None, :]
    return cos, jnp.where(low, -sin, 0.0), jnp.where(low, 0.0, sin)


def _all_heads(tab_ref):
    return jnp.concatenate([tab_ref[...]] * 4, axis=1)


def _rope(t, cos, sin_lo, sin_hi):
    return t * cos + pltpu.roll(t, HW - 32, 1) * sin_lo + pltpu.roll(t, 32, 1) * sin_hi


def _rope_t(g, cos, sin_lo, sin_hi):
    return g * cos + pltpu.roll(g * sin_lo, 32, 1) + pltpu.roll(g * sin_hi, HW - 32, 1)


def _inproj_fwd(h, par, w_in_t, tabs):
    tm = 512

    def body(h_ref, par_ref, w_ref, cos_ref, slo_ref, shi_ref, ua_ref, va_ref, q_ref, k_ref, v_ref):
        y = _norm_mod(h_ref[...], par_ref[...])[0].astype(bf16)
        proj = lambda c: _nt(y, w_ref[c * HW:(c + 1) * HW, :])
        ua_ref[...] = proj(0)
        va_ref[...] = proj(1)
        cos, slo, shi = _all_heads(cos_ref), _all_heads(slo_ref), _all_heads(shi_ref)
        q_ref[...] = _rope(proj(2), cos, slo, shi) * 0.125
        k_ref[...] = _rope(proj(3), cos, slo, shi)
        v_ref[...] = proj(4)

    row = pl.BlockSpec((tm, D), lambda i: (i, 0))
    half = pl.BlockSpec((tm, HW), lambda i: (i, 0))
    return _pallas_hbm(
        body, name="inproj_fwd", grid=(T // tm,),
        in_specs=[row, pl.BlockSpec((8, D), lambda i: (0, 0)), pl.BlockSpec((INW, D), lambda i: (0, 0))] +
                 [pl.BlockSpec((tm, CH), lambda i: (i, 0))] * 3,
        out_specs=[half] * 5,
        out_shape=[SDS((T, HW), f32)] * 5,
        compiler_params=_cp(("arbitrary",), 48),
    )(h, par, w_in_t, *tabs)


def _head_masks():
    lane = lax.broadcasted_iota(jnp.int32, (1, CH), 1)
    return lane < 64, lane >= 64


def _lane(t, idx):
    return jnp.sum(jnp.where(lax.broadcasted_iota(jnp.int32, t.shape, 1) == idx, t, 0.0), axis=-1, keepdims=True)


def _stack_heads(x):
    lo, hi = _head_masks()
    zero = jnp.zeros_like(x)
    return jnp.concatenate([jnp.where(lo, x, zero), jnp.where(hi, x, zero)], axis=0)


def _band_mask(has_prev):
    row = lax.broadcasted_iota(jnp.int32, (2 * CH, 2 * CH), 0) & (CH - 1)
    col = lax.broadcasted_iota(jnp.int32, (2 * CH, 2 * CH), 1)
    in_prev = jnp.logical_and(jnp.logical_and(col < CH, col >= row), has_prev)
    return jnp.logical_or(in_prev, jnp.logical_and(col >= CH, col - CH <= row))


def _attn_tiling(d):
    return CH * 16, 16 // d, CH


def _attn_fwd(q, k, v, d):
    rows, blocks, lanes = _attn_tiling(d)
    pairs = lanes // CH

    def body(q_ref, kc_ref, kp_ref, vc_ref, vp_ref, o_ref, lse_ref):
        c, lb = pl.program_id(0), pl.program_id(1)
        col = lax.broadcasted_iota(jnp.int32, (CH, CH), 1)
        lo, _ = _head_masks()

        @pl.when(lb == 0)
        def _():
            lse_ref[...] = jnp.zeros_like(lse_ref)

        at = lambda r, b: pl.ds(r + b * CH * d, CH, stride=d) if d > 1 else pl.ds(b * CH, CH)
        for r in range(d):
            for b in range(blocks):
                own = at(r, b)
                k_prev, v_prev, before, mask = ((kc_ref, vc_ref, at(r, b - 1), _band_mask(True)) if b > 0 else
                                                (kp_ref, vp_ref, at(r, blocks - 1), _band_mask(c > 0)))
                lse_tile = lse_ref[own, :]
                for pi in range(pairs):
                    sl = slice(pi * CH, (pi + 1) * CH)
                    hp = lb * pairs + pi
                    kk = jnp.concatenate([k_prev[before, sl], kc_ref[own, sl]], axis=0).astype(bf16)
                    vv = jnp.concatenate([v_prev[before, sl], vc_ref[own, sl]], axis=0).astype(bf16)
                    s = jnp.where(mask, _nt(_stack_heads(q_ref[own, sl].astype(bf16)), kk), -jnp.inf)
                    m = jnp.max(s, axis=-1, keepdims=True)
                    p = jnp.exp(s - m)
                    den = jnp.sum(p, axis=-1, keepdims=True)
                    o = _nn(p.astype(bf16), vv) / den
                    o_ref[own, sl] = jnp.where(lo, o[:CH], o[CH:])
                    lse = m + jnp.log(den)
                    lse_tile = jnp.where(col == 2 * hp, lse[:CH], jnp.where(col == 2 * hp + 1, lse[CH:], lse_tile))
                lse_ref[own, :] = lse_tile

    cur = pl.BlockSpec((rows, lanes), lambda c, lb: (c, lb))
    prev = pl.BlockSpec((rows, lanes), lambda c, lb: (jnp.maximum(c - 1, 0), lb))
    return _pallas_hbm(
        body, name="attn_fwd_d%d" % d, grid=(T // rows, HW // lanes), in_specs=[cur, cur, prev, cur, prev],
        out_specs=[cur, pl.BlockSpec((rows, CH), lambda c, lb: (c, 0))],
        out_shape=[SDS((T, HW), f32), SDS((T, CH), f32)],
        compiler_params=_cp(("arbitrary", "arbitrary"), 40),
    )(q, k, k, v, v)


def _attn_combine(os_, lses):
    tm = 512

    def body(o0_ref, o1_ref, o2_ref, l0_ref, l1_ref, l2_ref, o_ref, lse_ref):
        l0, l1, l2 = l0_ref[...], l1_ref[...], l2_ref[...]
        m = jnp.maximum(jnp.maximum(l0, l1), l2)
        e = [jnp.exp(l0 - m), jnp.exp(l1 - m), jnp.exp(l2 - m)]
        s = e[0] + e[1] + e[2]
        w = [ei / s for ei in e]
        lse_ref[...] = m + jnp.log(s)
        lo, _ = _head_masks()
        for hp in range(4):
            sl = slice(hp * CH, (hp + 1) * CH)
            acc = jnp.zeros((tm, CH), f32)
            for wp, op_ref in zip(w, (o0_ref, o1_ref, o2_ref)):
                wexp = jnp.where(lo, wp[:, 2 * hp:2 * hp + 1], wp[:, 2 * hp + 1:2 * hp + 2])
                acc = acc + wexp * op_ref[:, sl].astype(f32)
            o_ref[:, sl] = acc

    half = pl.BlockSpec((tm, HW), lambda i: (i, 0))
    stat = pl.BlockSpec((tm, CH), lambda i: (i, 0))
    return _pallas_hbm(
        body, name="attn_combine", grid=(T // tm,), in_specs=[half] * 3 + [stat] * 3,
        out_specs=[half, stat], out_shape=[SDS((T, HW), f32), SDS((T, CH), f32)],
        compiler_params=_cp(("arbitrary",), 32),
    )(*os_, *lses)


def _attn_bwd(q, k, v, o, do, lse, d, running=None):
    rows, blocks, lanes = _attn_tiling(d)
    pairs = lanes // CH
    steps = T // rows

    def body(qc_ref, qn_ref, kc_ref, kp_ref, vc_ref, vp_ref, oc_ref, on_ref, dc_ref, dn_ref, lc_ref, ln_ref, *rest):
        dq_ref, dk_ref, dv_ref = rest[-3:]
        sofar = (lambda i, rows, sl: rest[i][rows, sl]) if running is not None else (lambda i, rows, sl: 0.0)
        c, lb = pl.program_id(0), pl.program_id(1)
        lo, _ = _head_masks()
        at = lambda r, b: pl.ds(r + b * CH * d, CH, stride=d) if d > 1 else pl.ds(b * CH, CH)
        for r in range(d):
            for b in range(blocks):
                own = at(r, b)
                k_prev, v_prev, before, mask_c = ((kc_ref, vc_ref, at(r, b - 1), _band_mask(True)) if b > 0 else
                                                  (kp_ref, vp_ref, at(r, blocks - 1), _band_mask(c > 0)))
                q_next, o_next, d_next, l_next, after, has_next = (
                    (qc_ref, oc_ref, dc_ref, lc_ref, at(r, b + 1), True) if b + 1 < blocks else
                    (qn_ref, on_ref, dn_ref, ln_ref, at(r, 0), c < steps - 1))
                mask_n = _band_mask(has_next)[:, :CH]
                lse_c, lse_n = lc_ref[own, :], l_next[after, :]
                for pi in range(pairs):
                    sl = slice(pi * CH, (pi + 1) * CH)
                    hp = lb * pairs + pi
                    kc, vc = kc_ref[own, sl].astype(bf16), vc_ref[own, sl].astype(bf16)
                    kk = jnp.concatenate([k_prev[before, sl].astype(bf16), kc], axis=0)
                    vv = jnp.concatenate([v_prev[before, sl].astype(bf16), vc], axis=0)
                    doc, don = dc_ref[own, sl], d_next[after, sl]
                    qs_c, qs_n = _stack_heads(qc_ref[own, sl].astype(bf16)), _stack_heads(q_next[after, sl].astype(bf16))
                    ds_c, ds_n = _stack_heads(doc.astype(bf16)), _stack_heads(don.astype(bf16))
                    delta_c = jnp.sum(_stack_heads(doc * oc_ref[own, sl]), axis=-1, keepdims=True)
                    delta_n = jnp.sum(_stack_heads(don * o_next[after, sl]), axis=-1, keepdims=True)
                    heads = lambda t: jnp.concatenate([_lane(t, 2 * hp), _lane(t, 2 * hp + 1)], axis=0)
                    p1 = jnp.where(mask_c, jnp.exp(_nt(qs_c, kk) - heads(lse_c)), 0.0)
                    g1 = (p1 * (_nt(ds_c, vv) - delta_c)).astype(bf16)
                    dq = _nn(g1, kk)
                    dq_ref[own, sl] = sofar(0, own, sl) + jnp.where(lo, dq[:CH], dq[CH:])
                    p2 = jnp.where(mask_n, jnp.exp(_nt(qs_n, kc) - heads(lse_n)), 0.0)
                    g2 = (p2 * (_nt(ds_n, vc) - delta_n)).astype(bf16)
                    dk_ref[own, sl] = sofar(1, own, sl) + _tn(jnp.concatenate([g1[:, CH:], g2], axis=0),
                                                              jnp.concatenate([qs_c, qs_n], axis=0))
                    dv_ref[own, sl] = sofar(2, own, sl) + _tn(
                        jnp.concatenate([p1[:, CH:].astype(bf16), p2.astype(bf16)], axis=0),
                        jnp.concatenate([ds_c, ds_n], axis=0))

    cur = pl.BlockSpec((rows, lanes), lambda c, lb: (c, lb))
    prev = pl.BlockSpec((rows, lanes), lambda c, lb: (jnp.maximum(c - 1, 0), lb))
    nxt = pl.BlockSpec((rows, lanes), lambda c, lb: (jnp.minimum(c + 1, steps - 1), lb))
    scur = pl.BlockSpec((rows, CH), lambda c, lb: (c, 0))
    snxt = pl.BlockSpec((rows, CH), lambda c, lb: (jnp.minimum(c + 1, steps - 1), 0))
    more = [] if running is None else list(running)
    return _pallas_hbm(
        body, name="attn_bwd_d%d" % d, grid=(steps, HW // lanes),
        in_specs=[cur, nxt, cur, prev, cur, prev, cur, nxt, cur, nxt, scur, snxt] + [cur] * len(more),
        out_specs=[cur] * 3, out_shape=[SDS((T, HW), f32)] * 3,
        input_output_aliases={12 + i: i for i in range(len(more))},
        compiler_params=_cp(("arbitrary", "arbitrary"), 52),
    )(q, q, k, k, v, v, o, o, do, do, lse, lse, *more)


def _causal(w):
    row = lax.broadcasted_iota(jnp.int32, (CH, CH), 0)
    col = lax.broadcasted_iota(jnp.int32, (CH, CH), 1)
    return jnp.where(col <= row, w, 0.0)


def _sgu_fwd(ua, va, o, lng, lnb, ws, bs_t):
    tm = 512

    def body(ua_ref, va_ref, o_ref, lg_ref, lb_ref, ws_ref, bs_ref, mix_ref):
        for hd in range(4):
            sl = slice(hd * CH, (hd + 1) * CH)
            w = _causal(ws_ref[hd]).astype(bf16)
            for cc in range(tm // CH):
                rs = slice(cc * CH, (cc + 1) * CH)
                u = _gelu(ua_ref[rs, sl])
                v = _gelu(va_ref[rs, sl])
                vc = v - _rowmean(v)
                vn = vc * lax.rsqrt(_rowmean(vc * vc) + EPS) * lg_ref[:, sl] + lb_ref[:, sl]
                z = _nn(w, vn.astype(bf16)) + bs_ref[:, hd:hd + 1]
                mix_ref[rs, sl] = (u * z).astype(bf16)
        mix_ref[:, HW:] = o_ref[...].astype(bf16)

    half = pl.BlockSpec((tm, HW), lambda i: (i, 0))
    vec = pl.BlockSpec((1, HW), lambda i: (0, 0))
    return _pallas_hbm(
        body, name="sgu_fwd", grid=(T // tm,),
        in_specs=[half, half, half, vec, vec, pl.BlockSpec((4, CH, CH), lambda i: (0, 0, 0)),
                  pl.BlockSpec((CH, 4), lambda i: (0, 0))],
        out_specs=pl.BlockSpec((tm, 2 * HW), lambda i: (i, 0)), out_shape=SDS((T, 2 * HW), bf16),
        compiler_params=_cp(("arbitrary",), 32),
    )(ua, va, o, lng, lnb, ws, bs_t)


def _sgu_bwd(ua, va, d_a, lng, lnb, ws, bs_t):
    tm = 512

    def body(ua_ref, va_ref, d_ref, lg_ref, lb_ref, ws_ref, bs_ref, dsg_ref, dln_ref, dws_ref, db_ref):
        @pl.when(pl.program_id(0) == 0)
        def _():
            dln_ref[...] = jnp.zeros_like(dln_ref)
            dws_ref[...] = jnp.zeros_like(dws_ref)
            db_ref[...] = jnp.zeros_like(db_ref)

        for hd in range(4):
            sl = slice(hd * CH, (hd + 1) * CH)
            w = _causal(ws_ref[hd])
            w, wt = w.astype(bf16), w.T.astype(bf16)
            lg = lg_ref[:, sl]
            for cc in range(tm // CH):
                rs = slice(cc * CH, (cc + 1) * CH)
                xa, xv, dd = ua_ref[rs, sl], va_ref[rs, sl], d_ref[rs, sl]
                u, v = _gelu(xa), _gelu(xv)
                vc = v - _rowmean(v)
                rstd = lax.rsqrt(_rowmean(vc * vc) + EPS)
                xh = vc * rstd
                vnb = (xh * lg + lb_ref[:, sl]).astype(bf16)
                z = _nn(w, vnb) + bs_ref[:, hd:hd + 1]
                dz = dd * u
                dzb = dz.astype(bf16)
                dsg_ref[rs, sl] = (dd * z * _gelu_grad(xa)).astype(bf16)
                dws_ref[hd] += _nt(dzb, vnb)
                db_ref[hd] += dz
                dvn = _nn(wt, dzb)
                dln_ref[0:1, sl] += _colsum(dvn * xh)
                dln_ref[1:2, sl] += _colsum(dvn)
                dxh = dvn * lg
                dv = rstd * (dxh - _rowmean(dxh) - xh * _rowmean(dxh * xh))
                dsg_ref[rs, HW + hd * CH:HW + (hd + 1) * CH] = (dv * _gelu_grad(xv)).astype(bf16)

    half = pl.BlockSpec((tm, HW), lambda i: (i, 0))
    vec = pl.BlockSpec((1, HW), lambda i: (0, 0))
    mat = pl.BlockSpec((4, CH, CH), lambda i: (0, 0, 0))
    return _pallas_hbm(
        body, name="sgu_bwd", grid=(T // tm,),
        in_specs=[half, half, half, vec, vec, mat, pl.BlockSpec((CH, 4), lambda i: (0, 0))],
        out_specs=[pl.BlockSpec((tm, 2 * HW), lambda i: (i, 0)), pl.BlockSpec((8, HW), lambda i: (0, 0)), mat, mat],
        out_shape=[SDS((T, 2 * HW), bf16), SDS((8, HW), f32), SDS((4, CH, CH), f32), SDS((4, CH, CH), f32)],
        compiler_params=_cp(("arbitrary",), 32),
    )(ua, va, d_a, lng, lnb, ws, bs_t)


def _outproj_fwd(mixed, w_out, h, par):
    tm = 512

    def body(mix_ref, w_ref, h_ref, par_ref, ho_ref, po_ref):
        p = _nn(mix_ref[...], w_ref[...])
        po_ref[...] = p.astype(bf16)
        ho_ref[...] = h_ref[...] + par_ref[2:3, :] * p

    row = pl.BlockSpec((tm, D), lambda i: (i, 0))
    return _pallas_hbm(
        body, name="outproj_fwd", grid=(T // tm,),
        in_specs=[row, pl.BlockSpec((D, D), lambda i: (0, 0)), row, pl.BlockSpec((8, D), lambda i: (0, 0))],
        out_specs=[row, row], out_shape=[SDS((T, D), f32), SDS((T, D), bf16)],
        compiler_params=_cp(("arbitrary",), 32),
    )(mixed, w_out, h, par)


def _outproj_bwd(dh, po, w_out, par):
    tm = 512

    def body(dh_ref, po_ref, w_ref, par_ref, do_ref, da_ref, db_ref, dpar_ref):
        @pl.when(pl.program_id(0) == 0)
        def _():
            dpar_ref[...] = jnp.zeros_like(dpar_ref)

        dh_v = dh_ref[...]
        dob = (par_ref[2:3, :] * dh_v).astype(bf16)
        do_ref[...] = dob
        dpar_ref[2:3, :] += _colsum(dh_v * po_ref[...].astype(f32))
        dm = _nt(dob, w_ref[...])
        da_ref[...] = dm[:, :HW]
        db_ref[...] = dm[:, HW:]

    row = pl.BlockSpec((tm, D), lambda i: (i, 0))
    half = pl.BlockSpec((tm, HW), lambda i: (i, 0))
    one = pl.BlockSpec((8, D), lambda i: (0, 0))
    return _pallas_hbm(
        body, name="outproj_bwd", grid=(T // tm,),
        in_specs=[row, row, pl.BlockSpec((D, D), lambda i: (0, 0)), one],
        out_specs=[row, half, half, one],
        out_shape=[SDS((T, D), bf16), SDS((T, HW), f32), SDS((T, HW), f32), SDS((8, D), f32)],
        compiler_params=_cp(("arbitrary",), 32),
    )(dh, po, w_out, par)


def _inproj_bwd_tok(dh, h, par, w_in_t, dsg, dq, dk, dv, tabs):
    tm = 256

    def body(dh_ref, h_ref, par_ref, w_ref, dsg_ref, dq_ref, dk_ref, dv_ref,
             cos_ref, slo_ref, shi_ref, dhin_ref, dp_ref, y_ref, dpar_ref):
        @pl.when(pl.program_id(0) == 0)
        def _():
            dpar_ref[...] = jnp.zeros_like(dpar_ref)

        cos, slo, shi = _all_heads(cos_ref), _all_heads(slo_ref), _all_heads(shi_ref)
        dp_ref[:, :2 * HW] = dsg_ref[...]
        dp_ref[:, 2 * HW:3 * HW] = _rope_t(dq_ref[...] * 0.125, cos, slo, shi).astype(bf16)
        dp_ref[:, 3 * HW:4 * HW] = _rope_t(dk_ref[...], cos, slo, shi).astype(bf16)
        dp_ref[:, 4 * HW:] = dv_ref[...].astype(bf16)
        dy = _nn(dp_ref[...], w_ref[...])
        par_v, h_v = par_ref[...], h_ref[...]
        y_ref[...] = _norm_mod(h_v, par_v)[0].astype(bf16)
        dx, dsh, dsc, dng = _norm_mod_bwd(dy, h_v, par_v)
        dhin_ref[...] = dh_ref[...] + dx
        dpar_ref[0:1, :] += dsh
        dpar_ref[1:2, :] += dsc
        dpar_ref[3:4, :] += dng

    row = pl.BlockSpec((tm, D), lambda i: (i, 0))
    half = pl.BlockSpec((tm, HW), lambda i: (i, 0))
    one = pl.BlockSpec((8, D), lambda i: (0, 0))
    return _pallas_hbm(
        body, name="inproj_bwd_tok", grid=(T // tm,),
        in_specs=[row, row, one, pl.BlockSpec((INW, D), lambda i: (0, 0)), row] + [half] * 3 +
                 [pl.BlockSpec((tm, CH), lambda i: (i, 0))] * 3,
        out_specs=[row, pl.BlockSpec((tm, INW), lambda i: (i, 0)), row, one],
        out_shape=[SDS((T, D), f32), SDS((T, INW), bf16), SDS((T, D), bf16), SDS((8, D), f32)],
        compiler_params=_cp(("arbitrary",), 48),
    )(dh, h, par, w_in_t, dsg, dq, dk, dv, *tabs)


def _loss_head(h, par, target):
    tm = 512

    def body(h_ref, par_ref, t_ref, dh_ref, acc_ref):
        @pl.when(pl.program_id(0) == 0)
        def _():
            acc_ref[...] = jnp.zeros_like(acc_ref)

        x, g = h_ref[...], par_ref[3:4, :]
        rstd = lax.rsqrt(_rowmean(x * x) + EPS)
        xhat = x * rstd
        err = xhat * g - t_ref[...]
        acc_ref[0:1, :] += _colsum(err * err)
        dy = err * (1.0 / D)
        acc_ref[3:4, :] += _colsum(dy * xhat)
        dxhat = dy * g
        dh_ref[...] = rstd * (dxhat - xhat * _rowmean(dxhat * xhat))

    row = pl.BlockSpec((tm, D), lambda i: (i, 0))
    one = pl.BlockSpec((8, D), lambda i: (0, 0))
    return _pallas_hbm(
        body, name="loss_head", grid=(T // tm,), in_specs=[row, one, row], out_specs=[row, one],
        out_shape=[SDS((T, D), f32), SDS((8, D), f32)], compiler_params=_cp(("arbitrary",), 32),
    )(h, par, target)


ELEMENTWISE_VMEM_BUDGET = 20 << 20


def _block_rows(rows, bytes_per_row):
    cap = ELEMENTWISE_VMEM_BUDGET // bytes_per_row
    if rows <= cap:
        return rows
    return next(b for b in range(cap - cap % 16, 0, -16) if rows % b == 0)


def _sum_slots(land, name):
    _, R, C = land.shape
    br = _block_rows(R, 2 * NDEV * C * land.dtype.itemsize + 2 * C * 4)

    def body(l_ref, o_ref):
        acc = l_ref[0].astype(f32)
        for s in range(1, NDEV):
            acc = acc + l_ref[s].astype(f32)
        o_ref[...] = acc

    return _pallas_hbm(
        body, name=name, grid=(R // br,), in_specs=[pl.BlockSpec((NDEV, br, C), lambda i: (0, i, 0))],
        out_specs=pl.BlockSpec((br, C), lambda i: (i, 0)), out_shape=SDS((R, C), f32),
        compiler_params=_cp(("arbitrary",), 32),
    )(land)


def _sum_partials(land, own, me, name):
    r = land.shape[-2]

    def body(me_ref, l_ref, own_ref, o_ref):
        mine = own_ref[...].astype(f32)
        acc = jnp.where(me_ref[0] == 0, mine, l_ref[0].astype(f32))
        for s in range(1, NDEV):
            acc = acc + jnp.where(me_ref[0] == s, mine, l_ref[s].astype(f32))
        o_ref[...] = acc

    if own.ndim == 3:
        kk = own.shape[0]
        specs = dict(grid=(kk,),
                     in_specs=[pl.BlockSpec((NDEV, None, r, D), lambda i, me_ref: (0, i, 0, 0)),
                               pl.BlockSpec((None, r, D), lambda i, me_ref: (i, me_ref[0], 0))],
                     out_specs=pl.BlockSpec((None, r, D), lambda i, me_ref: (i, 0, 0)))
        out_shape = SDS((kk, r, D), f32)
    else:
        specs = dict(grid=(1,),
                     in_specs=[pl.BlockSpec((NDEV, r, D), lambda i, me_ref: (0, 0, 0)),
                               pl.BlockSpec((r, D), lambda i, me_ref: (me_ref[0], 0))],
                     out_specs=pl.BlockSpec((r, D), lambda i, me_ref: (0, 0)))
        out_shape = SDS((r, D), f32)
    return _pallas_hbm(
        body, name=name, out_shape=out_shape,
        grid_spec=pltpu.PrefetchScalarGridSpec(num_scalar_prefetch=1, **specs),
        compiler_params=_cp(("arbitrary",), 32),
    )(me.reshape(1), land, own)


def _adamw(w, g, m, v, name):
    R, C = w.shape
    br = _block_rows(R, 2 * 7 * C * 4)

    def body(w_ref, g_ref, m_ref, v_ref, d_ref, mo_ref, vo_ref):
        gv = g_ref[...]
        m2 = B1 * m_ref[...] + (1.0 - B1) * gv
        v2 = B2 * v_ref[...] + (1.0 - B2) * (gv * gv)
        mo_ref[...] = m2
        vo_ref[...] = v2
        m_hat = m2 / (1.0 - B1 ** STEP)
        v_hat = v2 / (1.0 - B2 ** STEP)
        d_ref[...] = -LR * (m_hat / (jnp.sqrt(v_hat) + AEPS) + WD * w_ref[...])

    blk = pl.BlockSpec((br, C), lambda i: (i, 0))
    return _pallas_hbm(
        body, name=name, grid=(R // br,), in_specs=[blk] * 4, out_specs=[blk] * 3,
        out_shape=[SDS((R, C), f32)] * 3, compiler_params=_cp(("arbitrary",), 32),
    )(w, g, m, v)


def _adamw_nd(w, g, m, v, name, swapped=False):
    if swapped:
        outs = _adamw_nd(*(jnp.swapaxes(a, -1, -2) for a in (w, g, m, v)), name)
        return [jnp.swapaxes(o, -1, -2) for o in outs]
    shp = w.shape
    r2 = (-1, shp[-1]) if w.ndim > 1 else (8, shp[0] // 8)
    outs = _adamw(w.reshape(r2), g.reshape(r2), m.reshape(r2), v.reshape(r2), name)
    return [o.reshape(shp) for o in outs]


def _par_rows(mod_l, s, gain):
    rows = jnp.pad(mod_l.reshape(9, D)[3 * s:3 * s + 3], ((0, 5), (0, 0)))
    return rows + jnp.pad(gain[None, :], ((3, 4), (0, 0)))


def _pad_rows(a):
    a = a.reshape(-1, CH)
    return jnp.pad(a, ((0, (-a.shape[0]) % 8), (0, 0)))


def _prepare(c, ada_w, ada_b, norm_g):
    me = 4 * lax.axis_index("x") + 2 * lax.axis_index("y") + lax.axis_index("c")

    pay = jnp.pad(c, ((0, 7), (0, 0)))
    pay = jnp.concatenate([pay, jnp.pad(norm_g.reshape(6, OUTS), ((0, 2), (0, D - OUTS)))], axis=0)
    got = _all_gather_small(pay, "gather_c")
    c_all = got[:, 0, :]
    gains = got[:, 8:14, :OUTS].transpose(1, 0, 2).reshape(DEPTH, 3, D)

    ada_b_loc = lax.dynamic_slice(ada_b, (0, me * ADAS), (DEPTH, ADAS)).reshape(DEPTH, 1, ADAS)
    mod_cols = _mod_fwd(c_all, ada_w, ada_b_loc)
    got = _all_gather_small(mod_cols.reshape(DEPTH * NDEV, ADAS), "gather_mod").reshape(NDEV, DEPTH, NDEV, ADAS)
    mod = lax.dynamic_index_in_dim(got, me, axis=2, keepdims=False).transpose(1, 0, 2).reshape(DEPTH, 9 * D)
    pars = [[_par_rows(mod[l], s, gains[l, s]) for s in range(3)] for l in range(DEPTH)]
    return me, c_all, pars


def _fwd_bwd(x2, target, pars, get_w, put_g, small_ready, sgu_ln_g, sgu_ln_b, sgu_w, sgu_b, final_g):
    tabs = _rope_tables()
    tril = jnp.tril(jnp.ones((CH, CH), dtype=bool))
    behind = lambda rows, token: rows if token is None else rows + token[0, 0]

    h = x2
    saved = []
    for l in range(DEPTH):
        lng, lnb = sgu_ln_g[l].reshape(1, HW), sgu_ln_b[l].reshape(1, HW)
        bs_t = sgu_b[l].T
        h0 = h
        w, token = get_w(4 * l, ("ffn1", l), h0)
        h1, fo1, g1, u1, a1 = _ffn_fwd(h0, behind(pars[l][0], token), w)
        w, token = get_w(4 * l + 2, ("inproj", l), h1)
        ua, va, q, k, v = _inproj_fwd(h1, behind(pars[l][1], token), w, tabs)
        branches = [_attn_fwd(q, k, v, d) for d in PATTERN_DILATIONS]
        o, lse = _attn_combine([b[0] for b in branches], [b[1] for b in branches])
        mixed = _sgu_fwd(ua, va, o, lng, lnb, sgu_w[l], bs_t)
        w, token = get_w(4 * l + 3, ("outproj", l), mixed)
        h2, po = _outproj_fwd(mixed, w, h1, behind(pars[l][1], token))
        w, token = get_w(4 * l + 1, ("ffn2", l), h2)
        h3, fo2, g2, u2, a2 = _ffn_fwd(h2, behind(pars[l][2], token), w)
        saved.append((h0, h1, h2, fo1, g1, u1, a1, ua, va, q, k, v, o, mixed, lse, po, fo2, g2, u2, a2))
        h = h3

    par_f = jnp.pad(final_g[None, :], ((3, 4), (0, 0)))
    dh, head = _loss_head(h, par_f, target)

    dmods, dgains, dsgu = [None] * DEPTH, [None] * DEPTH, [None] * DEPTH
    token = None
    for l in reversed(range(DEPTH)):
        w_f1, w_f2, w_i, w_o = (get_w(4 * l + j, None, None)[0] for j in (0, 1, 2, 3))
        h0, h1, h2, fo1, g1, u1, a1, ua, va, q, k, v, o, mixed, lse, po, fo2, g2, u2, a2 = saved[l]
        lng, lnb = sgu_ln_g[l].reshape(1, HW), sgu_ln_b[l].reshape(1, HW)
        bs_t = sgu_b[l].T

        dh, dg_s, du_s, y, dfb, dpar3 = _ffn_bwd_tok(dh, h2, behind(pars[l][2], token), fo2, g2, u2, w_f2)
        token = put_g(4 * l + 1, _ffn_bwd_w(a2, dg_s, du_s, y, dfb))

        dob, d_a, d_b, dpar2g = _outproj_bwd(dh, po, w_o, behind(pars[l][1], token))
        token = put_g(4 * l + 3, _tn_matmul(mixed, dob, 512, "w_out_grad"))
        dqkv = None
        for d in PATTERN_DILATIONS:
            dqkv = _attn_bwd(q, k, v, o, d_b, lse, d, dqkv)
        dsg, dln, dws, dbl = _sgu_bwd(ua, va, d_a, lng, lnb, sgu_w[l], bs_t)
        dh, dp, y2, dpar2 = _inproj_bwd_tok(dh, h1, behind(pars[l][1], token), w_i, dsg, *dqkv, tabs)
        token = put_g(4 * l + 2, _tn_matmul(dp, y2, 640, "w_in_grad"))

        dh, dg_s, du_s, y, dfb, dpar1 = _ffn_bwd_tok(dh, h0, behind(pars[l][0], token), fo1, g1, u1, w_f1)
        dmods[l] = jnp.concatenate([dpar1[0:3], dpar2[0:2], dpar2g[2:3], dpar3[0:3]], axis=0).reshape(9 * D)
        dgains[l] = jnp.stack([dpar1[3], dpar2[3], dpar3[3]])
        dsgu[l] = (dln[0], dln[1], jnp.where(tril[None], dws, 0.0), jnp.sum(dbl, axis=-1))
        token = small_ready(head, dmods, dgains, dsgu) if l == 0 else None
        token = put_g(4 * l, _ffn_bwd_w(a1, dg_s, du_s, y, dfb, token))
    return dh, token


def kernel(x, c, ada_w, ada_b, norm_g, ffn1_wg, ffn1_wu, ffn1_wd, ffn2_wg, ffn2_wu, ffn2_wd, w_in, sgu_ln_g, sgu_ln_b, sgu_w, sgu_b, w_out, final_g, loss_target, m_ada_w, m_ada_b, m_norm_g, m_ffn1_wg, m_ffn1_wu, m_ffn1_wd, m_ffn2_wg, m_ffn2_wu, m_ffn2_wd, m_w_in, m_sgu_ln_g, m_sgu_ln_b, m_sgu_w, m_sgu_b, m_w_out, m_final_g, v_ada_w, v_ada_b, v_norm_g, v_ffn1_wg, v_ffn1_wu, v_ffn1_wd, v_ffn2_wg, v_ffn2_wu, v_ffn2_wd, v_w_in, v_sgu_ln_g, v_sgu_ln_b, v_sgu_w, v_sgu_b, v_w_out, v_final_g):
    me, c_all, pars = _prepare(c, ada_w, ada_b, norm_g)

    tr = lambda w: jnp.swapaxes(w, -1, -2).astype(bf16)
    locs = []
    for l in range(DEPTH):
        locs.append(jnp.stack([tr(ffn1_wg[l]), tr(ffn1_wu[l]), ffn1_wd[l].astype(bf16)]))
        locs.append(jnp.stack([tr(ffn2_wg[l]), tr(ffn2_wu[l]), ffn2_wd[l].astype(bf16)]))
        locs.append(tr(w_in[l]))
        locs.append(w_out[l].astype(bf16))
    locs, pars = lax.optimization_barrier((locs, pars))

    first, locs = lax.optimization_barrier((_all_gather_rows([locs[0]], "gather_first")[0], locs))
    placed = [_place_own(a, me) for a in locs[1:]]
    ready = {0: first}
    groups = ([2, 3], [1], [4], [6, 7], [5])
    plan = {("inproj", 0): (("wait", "gather", 0), ("start", "chips", 1)),
            ("outproj", 0): (("wait", "chips", 1), ("start", "pass", 1), ("start", "chips", 2)),
            ("ffn2", 0): (("wait", "pass", 1),),
            ("ffn1", 1): (("wait", "chips", 2), ("start", "pass", 2), ("start", "gather", 3), ("wait", "pass", 2)),
            ("inproj", 1): (("wait", "gather", 3), ("start", "chips", 4)),
            ("outproj", 1): (("wait", "chips", 4), ("start", "pass", 4)),
            ("ffn2", 1): (("wait", "pass", 4),)}
    flying = {}

    def exchange(action, mode, gi, after):
        if action == "start":
            arrays = flying.pop(gi) if mode == "pass" else [placed[p - 1] for p in groups[gi]]
            arrays, _ = lax.optimization_barrier((arrays, after))
            started = _exchange_start(mode, arrays, "%s_start_%d" % (mode, gi))
            flying[gi] = started[:-1]
            return started[-1]
        arrays = _exchange_wait(mode, flying.pop(gi), after, "%s_wait_%d" % (mode, gi))
        if mode == "chips":
            flying[gi] = arrays
        else:
            ready.update(zip(groups[gi], arrays))
        return None

    first_token = exchange("start", "gather", 0, first)

    def get_w(piece, point, after):
        token = first_token if point == ("ffn1", 0) else None
        for action, mode, gi in plan.get(point, ()):
            started = exchange(action, mode, gi, after)
            if started is not None:
                token = after = started
        return ready.get(piece), token

    sent = {}

    def put_g(piece, grad):
        land = lax.empty((NDEV,) + grad.shape[:-2] + (grad.shape[-2] // NDEV, D), bf16)
        started = _exchange_start("scatter", [grad, land], "scatter_start_%d" % piece)
        sent[piece] = started[:-1]
        return started[-1]

    small_sent = []

    def small_ready(head, dmods, dgains, dsgu):
        loss_part = 0.5 * jnp.sum(head[0]) / D
        small = jnp.concatenate([
            _pad_rows(jnp.stack(dmods)), _pad_rows(jnp.stack(dgains)),
            _pad_rows(jnp.stack([s[0] for s in dsgu])), _pad_rows(jnp.stack([s[1] for s in dsgu])),
            _pad_rows(jnp.stack([s[3] for s in dsgu])), _pad_rows(jnp.stack([s[2] for s in dsgu])),
            _pad_rows(head[3]), _pad_rows(jnp.pad(loss_part[None], (0, CH - 1)))], axis=0)
        started = _exchange_start("gather", [_place_own(small, me, "place_small")], "small_start")
        small_sent.append(started[:-1])
        return started[-1]

    dh, last = _fwd_bwd(x[0], loss_target[0], pars, get_w, put_g, small_ready, sgu_ln_g, sgu_ln_b, sgu_w, sgu_b, final_g)
    grad_x = dh[None]

    got = _exchange_wait("gather", small_sent[0], last, "small_wait")[0].reshape(NDEV, -1, CH)
    tot = _sum_slots(got, "sum_small")
    n_mod, n_gain, n_sw = DEPTH * 9 * D // CH, DEPTH * 3 * D // CH, DEPTH * 4 * CH
    offs = [0, n_mod, n_mod + n_gain, n_mod + n_gain + 8, n_mod + n_gain + 16, n_mod + n_gain + 24]
    g_ada_b = tot[offs[0]:offs[1]].reshape(DEPTH, 9 * D)
    g_gain_full = tot[offs[1]:offs[2]].reshape(DEPTH, 3, D)
    g_ln_g = tot[offs[2]:offs[3]].reshape(DEPTH, 4, CH)
    g_ln_b = tot[offs[3]:offs[4]].reshape(DEPTH, 4, CH)
    g_sb = tot[offs[4]:offs[5]].reshape(DEPTH, 4, CH)
    g_sw = tot[offs[5]:offs[5] + n_sw].reshape(DEPTH, 4, CH, CH)
    g_final = tot[offs[5] + n_sw:offs[5] + n_sw + 8].reshape(D)
    loss = tot[offs[5] + n_sw + 8, 0]
    g_norm = lax.dynamic_slice(g_gain_full, (0, 0, me * OUTS), (DEPTH, 3, OUTS))

    dmod_all = got[:, offs[0]:offs[1]].reshape(NDEV, DEPTH, 9 * D)
    dmod_cols = lax.dynamic_slice(dmod_all, (0, 0, me * ADAS), (NDEV, DEPTH, ADAS)).transpose(1, 0, 2)
    g_ada_w = _ada_grad((c_all * jax.nn.sigmoid(c_all)).T, dmod_cols)

    sums, after = {}, tot

    def collect(piece, after):
        own, land = _exchange_wait("scatter", sent[piece], after, "scatter_wait_%d" % piece)
        sums[piece] = _sum_partials(land, own, me, "sum_grads")
        return sums[piece]

    for piece in (5, 7, 6, 4, 1, 3, 2):
        after = collect(piece, after)
    back = lambda t: jnp.swapaxes(t, -1, -2)
    f2 = jnp.stack([sums[4 * l + 1] for l in range(DEPTH)])
    g_w_in = back(jnp.stack([sums[4 * l + 2] for l in range(DEPTH)]))
    g_w_out = jnp.stack([sums[4 * l + 3] for l in range(DEPTH)])

    ws = [ada_w, ada_b, norm_g, ffn1_wg, ffn1_wu, ffn1_wd, ffn2_wg, ffn2_wu, ffn2_wd, w_in, sgu_ln_g, sgu_ln_b, sgu_w,
          sgu_b, w_out, final_g]
    ms = [m_ada_w, m_ada_b, m_norm_g, m_ffn1_wg, m_ffn1_wu, m_ffn1_wd, m_ffn2_wg, m_ffn2_wu, m_ffn2_wd, m_w_in,
          m_sgu_ln_g, m_sgu_ln_b, m_sgu_w, m_sgu_b, m_w_out, m_final_g]
    vs = [v_ada_w, v_ada_b, v_norm_g, v_ffn1_wg, v_ffn1_wu, v_ffn1_wd, v_ffn2_wg, v_ffn2_wu, v_ffn2_wd, v_w_in,
          v_sgu_ln_g, v_sgu_ln_b, v_sgu_w, v_sgu_b, v_w_out, v_final_g]
    gw = [g_ada_w, g_ada_b, g_norm, None, None, None, back(f2[:, 0]), back(f2[:, 1]), f2[:, 2],
          g_w_in, g_ln_g, g_ln_b, g_sw, g_sb, g_w_out, g_final]
    upd = [None] * len(ws)
    lane_narrow = (3, 4, 6, 7, 9)
    for i in (1, 2, 10, 11, 12, 13, 15, 6, 7, 8, 9, 14, 0):
        upd[i] = _adamw_nd(ws[i], gw[i], ms[i], vs[i], "adamw", i in lane_narrow)
    collect(0, upd[0][0])
    f1 = jnp.stack([sums[4 * l] for l in range(DEPTH)])
    gw[3:6] = [back(f1[:, 0]), back(f1[:, 1]), f1[:, 2]]
    for i in (3, 4, 5):
        upd[i] = _adamw_nd(ws[i], gw[i], ms[i], vs[i], "adamw", i in lane_narrow)
    return (loss, grad_x, *gw, *[u[0] for u in upd], *[u[1] for u in upd], *[u[2] for u in upd])
```

```python
import functools
import math

import jax
import jax.numpy as jnp
from jax import lax
from jax.experimental import pallas as pl
from jax.experimental.pallas import tpu as pltpu

f32, bf16 = jnp.float32, jnp.bfloat16
SDS = jax.ShapeDtypeStruct

T, D, F = 4096, 1024, 2816
NDEV, DEPTH = 8, 2
HW = 512
INW = 5 * HW
FS, INS, OUTS, ADAS = F // NDEV, INW // NDEV, D // NDEV, 9 * D // NDEV
CH = 128
PATTERN_DILATIONS = (1, 4, 16)
ROPE_THETA = 10000.0
EPS = 1e-6
LR, B1, B2, AEPS, WD, STEP = 0.001, 0.9, 0.999, 1e-08, 0.01, 10
MESH = pl.DeviceIdType.MESH


def _cp(sems, vmem_mb):
    return pltpu.CompilerParams(dimension_semantics=sems, vmem_limit_bytes=vmem_mb << 20)


def _pallas_hbm(body, *, out_shape, **kw):
    typed = jax.tree.map(lambda s: pltpu.HBM(s.shape, s.dtype), out_shape)
    call = pl.pallas_call(body, out_shape=typed, **kw)

    def run(*operands):
        pin = lambda x: x if x.dtype == jnp.int32 else pltpu.with_memory_space_constraint(x, pltpu.HBM)
        return call(*[pin(x) for x in operands])

    return run


def _nn(a, b):
    return lax.dot_general(a, b, (((1,), (0,)), ((), ())), preferred_element_type=f32)


def _nt(a, b):
    return lax.dot_general(a, b, (((1,), (1,)), ((), ())), preferred_element_type=f32)


def _tn(a, b):
    return lax.dot_general(a, b, (((0,), (0,)), ((), ())), preferred_element_type=f32)


def _colsum(a):
    return jnp.sum(a, axis=0, keepdims=True)


def _rowmean(a):
    return jnp.mean(a, axis=-1, keepdims=True)


def _norm_mod(x, par):
    rstd = lax.rsqrt(_rowmean(x * x) + EPS)
    xhat = x * rstd
    n = xhat * par[3:4, :]
    y = n * (1.0 + par[1:2, :]) + par[0:1, :]
    return y, n, xhat, rstd


def _norm_mod_bwd(dy, x, par):
    _, n, xhat, rstd = _norm_mod(x, par)
    dn = dy * (1.0 + par[1:2, :])
    dxhat = dn * par[3:4, :]
    dx = rstd * (dxhat - xhat * _rowmean(dxhat * xhat))
    return dx, _colsum(dy), _colsum(dy * n), _colsum(dn * xhat)


_GK = math.sqrt(2.0 / math.pi)


def _gelu(x):
    return 0.5 * x * (1.0 + jnp.tanh(_GK * (x + 0.044715 * x * x * x)))


def _gelu_grad(x):
    t = jnp.tanh(_GK * (x + 0.044715 * x * x * x))
    return 0.5 * (1.0 + t) + 0.5 * x * (1.0 - t * t) * (_GK * (1.0 + 3.0 * 0.044715 * x * x))


def _rows(ref, idx, r):
    if len(ref.shape) == 3:
        return ref.at[:, pl.ds(idx * r, r), :]
    return ref.at[pl.ds(idx * r, r), :]


def _flip(v, bit):
    return 1 - v if bit else v


def _all_gather_small(x, name):
    R, C = x.shape

    def body(x_ref, out_ref, send_sems, recv_sems):
        mx, my, mc = lax.axis_index("x"), lax.axis_index("y"), lax.axis_index("c")
        me = 4 * mx + 2 * my + mc
        out_ref[me] = x_ref[...]
        sent = []
        for k in range(1, NDEV):
            peer = (_flip(mx, k & 4), _flip(my, k & 2), _flip(mc, k & 1))
            cp = pltpu.make_async_remote_copy(
                src_ref=x_ref, dst_ref=out_ref.at[me], send_sem=send_sems.at[k - 1],
                recv_sem=recv_sems.at[k - 1], device_id=peer, device_id_type=MESH)
            cp.start()
            sent.append(cp)
        for k in range(1, NDEV):
            peer = (_flip(mx, k & 4), _flip(my, k & 2), _flip(mc, k & 1))
            pidx = 4 * peer[0] + 2 * peer[1] + peer[2]
            pltpu.make_async_remote_copy(
                src_ref=x_ref, dst_ref=out_ref.at[pidx], send_sem=send_sems.at[k - 1],
                recv_sem=recv_sems.at[k - 1], device_id=peer, device_id_type=MESH).wait_recv()
        for cp in sent:
            cp.wait_send()

    vm = pl.BlockSpec(memory_space=pltpu.VMEM)
    return pl.pallas_call(
        body, name=name, out_shape=SDS((NDEV, R, C), f32), in_specs=[vm], out_specs=vm,
        scratch_shapes=[pltpu.SemaphoreType.DMA((NDEV - 1,)), pltpu.SemaphoreType.DMA((NDEV - 1,))],
        compiler_params=pltpu.CompilerParams(vmem_limit_bytes=32 << 20),
    )(x)


def _all_gather_rows(locs, name):
    n = len(locs)
    rs = [a.shape[-2] for a in locs]

    def body(*refs):
        src, out = refs[:n], refs[n:2 * n]
        send_sems, recv_sems, loc_sems = refs[2 * n:]
        mx, my, mc = lax.axis_index("x"), lax.axis_index("y"), lax.axis_index("c")
        me, sib = (mx, my, mc), (mx, my, 1 - mc)
        chips = [(1 - mx, my), (mx, 1 - my), (1 - mx, 1 - my)]

        def blk(a, p):
            return _rows(out[a], 4 * p[0] + 2 * p[1] + p[2], rs[a])

        def copy(k, a, block, to, from_src=False):
            return pltpu.make_async_remote_copy(
                src_ref=src[a] if from_src else blk(a, block), dst_ref=blk(a, block),
                send_sem=send_sems.at[k * n + a], recv_sem=recv_sems.at[k * n + a],
                device_id=to, device_id_type=MESH)

        mine = [pltpu.make_async_copy(src[a], blk(a, me), loc_sems.at[a]) for a in range(n)]
        for m in mine:
            m.start()
        first = []
        for j, chip in enumerate(chips):
            first += [copy(1 + j, a, me, (*chip, mc), True) for a in range(n)]
        first += [copy(0, a, me, sib, True) for a in range(n)]
        for cp in first:
            cp.start()
        passed = []
        for j, chip in enumerate(chips):
            for a in range(n):
                copy(1 + j, a, (*chip, mc), me).wait_recv()
            fwd = [copy(4 + j, a, (*chip, mc), sib) for a in range(n)]
            for cp in fwd:
                cp.start()
            passed += fwd
        for a in range(n):
            copy(0, a, sib, me).wait_recv()
        for j, chip in enumerate(chips):
            for a in range(n):
                copy(4 + j, a, (*chip, 1 - mc), me).wait_recv()
        for cp in first + passed:
            cp.wait_send()
        for m in mine:
            m.wait()

    hbm = pl.BlockSpec(memory_space=pl.ANY)
    out_shape = [SDS(a.shape[:-2] + (NDEV * a.shape[-2], a.shape[-1]), a.dtype) for a in locs]
    return pl.pallas_call(
        body, name=name, out_shape=out_shape, in_specs=[hbm] * n, out_specs=[hbm] * n,
        scratch_shapes=[pltpu.SemaphoreType.DMA((7 * n,)), pltpu.SemaphoreType.DMA((7 * n,)),
                        pltpu.SemaphoreType.DMA((n,))],
    )(*locs)


HBM_SPEC = pl.BlockSpec(memory_space=pltpu.HBM)
SEM_SPEC = pl.BlockSpec(memory_space=pltpu.SEMAPHORE)
DATAFLOW_EFFECT = pltpu.SideEffectType.DATAFLOW_SIDE_EFFECTING


def _place_own(loc, me, name="place_own"):
    r, cols = loc.shape[-2:]
    loc3 = loc.reshape(-1, r, cols)
    kk = loc3.shape[0]

    def body(me_ref, src_ref, full_ref, out_ref):
        out_ref[...] = src_ref[...]

    out = _pallas_hbm(
        body, name=name, out_shape=SDS((kk, NDEV * r, cols), loc.dtype),
        grid_spec=pltpu.PrefetchScalarGridSpec(
            num_scalar_prefetch=1, grid=(kk,),
            in_specs=[pl.BlockSpec((None, r, cols), lambda i, me_ref: (i, 0, 0)), pl.BlockSpec(memory_space=pl.ANY)],
            out_specs=pl.BlockSpec((None, r, cols), lambda i, me_ref: (i, me_ref[0], 0))),
        input_output_aliases={2: 0}, compiler_params=_cp(("arbitrary",), 32),
    )(me.reshape(1), loc3, lax.empty((kk, NDEV * r, cols), loc.dtype))
    return out.reshape(loc.shape[:-2] + (NDEV * r, cols))


EXCHANGE_PEERS = {"gather": (1, 2, 3, 4, 5, 6, 7), "scatter": (1, 2, 3, 4, 5, 6, 7), "chips": (1, 2, 4, 6), "pass": (2, 4, 6)}


def _exchange_copies(mode, bufs, n, send_sems, recv_sems):
    mx, my, mc = lax.axis_index("x"), lax.axis_index("y"), lax.axis_index("c")
    me = 4 * mx + 2 * my + mc
    out = []
    for slot, k in enumerate(EXCHANGE_PEERS[mode]):
        peer = (_flip(mx, k & 4), _flip(my, k & 2), _flip(mc, k & 1))
        pidx = 4 * peer[0] + 2 * peer[1] + peer[2]
        for a in range(n):
            r = bufs[a].shape[-2] // NDEV
            if mode == "scatter":
                src, dst, arrive = _rows(bufs[a], pidx, r), bufs[n + a].at[me], bufs[n + a].at[pidx]
            elif mode == "pass":
                peer = (mx, my, 1 - mc)
                src, dst, arrive = _rows(bufs[a], pidx, r), _rows(bufs[a], pidx, r), _rows(bufs[a], pidx + 1 - 2 * mc, r)
            else:
                src, dst, arrive = _rows(bufs[a], me, r), _rows(bufs[a], me, r), _rows(bufs[a], pidx, r)
            sems = dict(send_sem=send_sems.at[slot * n + a], recv_sem=recv_sems.at[slot * n + a],
                        device_id=peer, device_id_type=MESH)
            out.append((pltpu.make_async_remote_copy(src_ref=src, dst_ref=dst, **sems),
                        pltpu.make_async_remote_copy(src_ref=src, dst_ref=arrive, **sems)))
    return out


def _exchange_start(mode, arrays, name):
    m = len(arrays)
    n = m // 2 if mode == "scatter" else m

    def body(*refs):
        send_sems, recv_sems, token = refs[m], refs[m + 1], refs[-1]
        for go, _ in _exchange_copies(mode, refs[:m], n, send_sems, recv_sems):
            go.start()
        token[...] = jnp.zeros_like(token)

    sems = pltpu.SemaphoreType.DMA((len(EXCHANGE_PEERS[mode]) * n,))
    return pl.pallas_call(
        body, name=name, in_specs=[HBM_SPEC] * m,
        out_shape=(sems, sems, *[pltpu.HBM(a.shape, a.dtype) for a in arrays], SDS((8, CH), f32)),
        out_specs=(SEM_SPEC, SEM_SPEC, *[HBM_SPEC] * m, pl.BlockSpec(memory_space=pltpu.VMEM)),
        input_output_aliases={a: 2 + a for a in range(m)},
        compiler_params=pltpu.CompilerParams(has_side_effects=DATAFLOW_EFFECT),
    )(*[pltpu.with_memory_space_constraint(a, pltpu.HBM) for a in arrays])


def _exchange_wait(mode, started, after, name):
    send_sems, recv_sems, *arrays = started
    m = len(arrays)
    n = m // 2 if mode == "scatter" else m

    def body(*refs):
        for go, arrive in _exchange_copies(mode, refs[:m], n, refs[m], refs[m + 1]):
            go.wait_send()
            arrive.wait_recv()

    return pl.pallas_call(
        body, name=name, in_specs=[HBM_SPEC] * m + [SEM_SPEC, SEM_SPEC, pl.BlockSpec(memory_space=pl.ANY)],
        out_shape=[pltpu.HBM(a.shape, a.dtype) for a in arrays], out_specs=[HBM_SPEC] * m,
        input_output_aliases={a: a for a in range(m)},
        compiler_params=pltpu.CompilerParams(has_side_effects=DATAFLOW_EFFECT),
    )(*arrays, send_sems, recv_sems, after)


def _mod_fwd(c_all, ada_w, ada_b_loc):
    def body(c_ref, w_ref, b_ref, o_ref):
        ca = c_ref[...]
        ca = ca * jax.nn.sigmoid(ca)
        o_ref[...] = jnp.dot(ca, w_ref[...], precision=lax.Precision.HIGHEST,
                             preferred_element_type=f32) + b_ref[...]

    return _pallas_hbm(
        body, name="mod_fwd", grid=(DEPTH,), out_shape=SDS((DEPTH, NDEV, ADAS), f32),
        in_specs=[pl.BlockSpec((NDEV, D), lambda l: (0, 0)),
                  pl.BlockSpec((None, D, ADAS), lambda l: (l, 0, 0)),
                  pl.BlockSpec((None, 1, ADAS), lambda l: (l, 0, 0))],
        out_specs=pl.BlockSpec((None, NDEV, ADAS), lambda l: (l, 0, 0)),
        compiler_params=_cp(("arbitrary",), 32),
    )(c_all, ada_w, ada_b_loc)


def _ada_grad(cact_t, dmod_cols):
    def body(c_ref, d_ref, o_ref):
        acc = c_ref[:, 0:1] * d_ref[0:1, :]
        for b in range(1, NDEV):
            acc = acc + c_ref[:, b:b + 1] * d_ref[b:b + 1, :]
        o_ref[...] = acc

    tr = 256
    return _pallas_hbm(
        body, name="ada_grad", grid=(DEPTH, D // tr), out_shape=SDS((DEPTH, D, ADAS), f32),
        in_specs=[pl.BlockSpec((tr, NDEV), lambda l, i: (i, 0)),
                  pl.BlockSpec((None, NDEV, ADAS), lambda l, i: (l, 0, 0))],
        out_specs=pl.BlockSpec((None, tr, ADAS), lambda l, i: (l, i, 0)),
        compiler_params=_cp(("arbitrary", "arbitrary"), 32),
    )(cact_t, dmod_cols)


def _ffn_fwd(h, par, w3):
    tm, tf = 1024, 256
    nj = F // tf

    def body(h_ref, par_ref, wg_ref, wu_ref, wd_ref, ho_ref, fo_ref, g_ref, u_ref, a_ref, y_scr, acc):
        j = pl.program_id(1)

        @pl.when(j == 0)
        def _():
            y_scr[...] = _norm_mod(h_ref[...], par_ref[...])[0].astype(bf16)
            acc[...] = jnp.zeros_like(acc)

        y = y_scr[...]
        g = _nt(y, wg_ref[...])
        u = _nt(y, wu_ref[...])
        a = ((g * jax.nn.sigmoid(g)) * u).astype(bf16)
        g_ref[...] = g.astype(bf16)
        u_ref[...] = u.astype(bf16)
        a_ref[...] = a
        acc[...] += _nn(a, wd_ref[...])

        @pl.when(j == nj - 1)
        def _():
            fo_ref[...] = acc[...].astype(bf16)
            ho_ref[...] = h_ref[...] + (0.5 * par_ref[2:3, :]) * acc[...]

    row = pl.BlockSpec((tm, D), lambda i, j: (i, 0))
    hid = pl.BlockSpec((tm, tf), lambda i, j: (i, j))
    wspec = [pl.BlockSpec((None, tf, D), lambda i, j, k=k: (k, j, 0)) for k in range(3)]
    return _pallas_hbm(
        body, name="ffn_fwd", grid=(T // tm, nj),
        in_specs=[row, pl.BlockSpec((8, D), lambda i, j: (0, 0))] + wspec,
        out_specs=[row, row, hid, hid, hid],
        out_shape=[SDS((T, D), f32), SDS((T, D), bf16), SDS((T, F), bf16), SDS((T, F), bf16), SDS((T, F), bf16)],
        scratch_shapes=[pltpu.VMEM((tm, D), bf16), pltpu.VMEM((tm, D), f32)],
        compiler_params=_cp(("arbitrary", "arbitrary"), 52),
    )(h, par, w3, w3, w3)


def _ffn_bwd_tok(dh, h, par, fo, gs, us, w3):
    tm, tf = 1024, 256
    nj = F // tf

    def body(dh_ref, h_ref, par_ref, fo_ref, g_ref, u_ref, wg_ref, wu_ref, wd_ref,
             dhin_ref, dg_ref, du_ref, y_ref, dfb_ref, dpar_ref, df_scr, dyacc):
        i, j = pl.program_id(0), pl.program_id(1)

        @pl.when(jnp.logical_and(i == 0, j == 0))
        def _():
            dpar_ref[...] = jnp.zeros_like(dpar_ref)

        @pl.when(j == 0)
        def _():
            dh_v, par_v = dh_ref[...], par_ref[...]
            dfb = ((0.5 * par_v[2:3, :]) * dh_v).astype(bf16)
            df_scr[...] = dfb
            dfb_ref[...] = dfb
            dpar_ref[2:3, :] += 0.5 * _colsum(dh_v * fo_ref[...].astype(f32))
            y_ref[...] = _norm_mod(h_ref[...], par_v)[0].astype(bf16)
            dyacc[...] = jnp.zeros_like(dyacc)

        for half in range(2):
            rs = slice(half * (tm // 2), (half + 1) * (tm // 2))
            da = _nt(df_scr[rs, :], wd_ref[...])
            g = g_ref[rs, :].astype(f32)
            u = u_ref[rs, :].astype(f32)
            sig = jax.nn.sigmoid(g)
            du = (da * (g * sig)).astype(bf16)
            dg = (da * u * (sig * (1.0 + g * (1.0 - sig)))).astype(bf16)
            dg_ref[rs, :] = dg
            du_ref[rs, :] = du
            dyacc[rs, :] += _nn(dg, wg_ref[...]) + _nn(du, wu_ref[...])

        @pl.when(j == nj - 1)
        def _():
            dx, dsh, dsc, dng = _norm_mod_bwd(dyacc[...], h_ref[...], par_ref[...])
            dhin_ref[...] = dh_ref[...] + dx
            dpar_ref[0:1, :] += dsh
            dpar_ref[1:2, :] += dsc
            dpar_ref[3:4, :] += dng

    row = pl.BlockSpec((tm, D), lambda i, j: (i, 0))
    hid = pl.BlockSpec((tm, tf), lambda i, j: (i, j))
    one = pl.BlockSpec((8, D), lambda i, j: (0, 0))
    wspec = [pl.BlockSpec((None, tf, D), lambda i, j, k=k: (k, j, 0)) for k in range(3)]
    return _pallas_hbm(
        body, name="ffn_bwd_tok", grid=(T // tm, nj),
        in_specs=[row, row, one, row, hid, hid] + wspec,
        out_specs=[row, hid, hid, row, row, one],
        out_shape=[SDS((T, D), f32), SDS((T, F), bf16), SDS((T, F), bf16), SDS((T, D), bf16),
                   SDS((T, D), bf16), SDS((8, D), f32)],
        scratch_shapes=[pltpu.VMEM((tm, D), bf16), pltpu.VMEM((tm, D), f32)],
        compiler_params=_cp(("arbitrary", "arbitrary"), 60),
    )(dh, h, par, fo, gs, us, w3, w3, w3)


def _ffn_bwd_w(a_s, dg_s, du_s, y, dfb, anchor=None):
    tf, tk = F // 2, 512
    nk = T // tk

    def body(a_ref, dg_ref, du_ref, y_ref, df_ref, *rest):
        o_ref, accg, accu, accd = rest[-4:]
        kk = pl.program_id(1)

        @pl.when(kk == 0)
        def _():
            accg[...] = jnp.zeros_like(accg)
            accu[...] = jnp.zeros_like(accu)
            accd[...] = jnp.zeros_like(accd)

        yv = y_ref[...]
        accg[...] += _tn(dg_ref[...], yv)
        accu[...] += _tn(du_ref[...], yv)
        accd[...] += _tn(a_ref[...], df_ref[...])

        @pl.when(kk == nk - 1)
        def _():
            o_ref[0] = accg[...].astype(bf16)
            o_ref[1] = accu[...].astype(bf16)
            o_ref[2] = accd[...].astype(bf16)

    hid = pl.BlockSpec((tk, tf), lambda j, kk: (kk, j))
    row = pl.BlockSpec((tk, D), lambda j, kk: (kk, 0))
    extra = [] if anchor is None else [anchor]
    return _pallas_hbm(
        body, name="ffn_bwd_w", grid=(F // tf, nk),
        in_specs=[hid, hid, hid, row, row] + [pl.BlockSpec((8, CH), lambda j, kk: (0, 0))] * len(extra),
        out_specs=pl.BlockSpec((3, tf, D), lambda j, kk: (0, j, 0)), out_shape=SDS((3, F, D), bf16),
        scratch_shapes=[pltpu.VMEM((tf, D), f32)] * 3,
        compiler_params=_cp(("arbitrary", "arbitrary"), 58),
    )(a_s, dg_s, du_s, y, dfb, *extra)


def _tn_matmul(a, b, bm, name):
    M, N = a.shape[1], b.shape[1]
    tk = 1024
    nk = T // tk

    def body(a_ref, b_ref, o_ref, acc):
        kk = pl.program_id(1)

        @pl.when(kk == 0)
        def _():
            acc[...] = jnp.zeros_like(acc)

        acc[...] += _tn(a_ref[...], b_ref[...])

        @pl.when(kk == nk - 1)
        def _():
            o_ref[...] = acc[...].astype(bf16)

    return _pallas_hbm(
        body, name=name, grid=(M // bm, nk),
        in_specs=[pl.BlockSpec((tk, bm), lambda i, kk: (kk, i)), pl.BlockSpec((tk, N), lambda i, kk: (kk, 0))],
        out_specs=pl.BlockSpec((bm, N), lambda i, kk: (i, 0)), out_shape=SDS((M, N), bf16),
        scratch_shapes=[pltpu.VMEM((bm, N), f32)],
        compiler_params=_cp(("arbitrary", "arbitrary"), 40),
    )(a, b)


def _rope_tables():
    inv = ROPE_THETA ** (-jnp.arange(0, 64, 2, dtype=f32) / 64)
    ang = jnp.arange(T, dtype=f32)[:, None] * inv[None, :]
    ang = jnp.concatenate([ang, ang], axis=-1)
    cos, sin = jnp.tile(jnp.cos(ang), (1, 2)), jnp.tile(jnp.sin(ang), (1, 2))
    low = (jnp.arange(CH) % 64 < 32)[None, :]
    return cos, jnp.where(low, -sin, 0.0), jnp.where(low, 0.0, sin)


def _all_heads(tab_ref):
    return jnp.concatenate([tab_ref[...]] * 4, axis=1)


def _rope(t, cos, sin_lo, sin_hi):
    return t * cos + pltpu.roll(t, HW - 32, 1) * sin_lo + pltpu.roll(t, 32, 1) * sin_hi


def _rope_t(g, cos, sin_lo, sin_hi):
    return g * cos + pltpu.roll(g * sin_lo, 32, 1) + pltpu.roll(g * sin_hi, HW - 32, 1)


def _inproj_fwd(h, par, w_in_t, tabs):
    tm = 512

    def body(h_ref, par_ref, w_ref, cos_ref, slo_ref, shi_ref, ua_ref, va_ref, q_ref, k_ref, v_ref):
        y = _norm_mod(h_ref[...], par_ref[...])[0].astype(bf16)
        proj = lambda c: _nt(y, w_ref[c * HW:(c + 1) * HW, :])
        ua_ref[...] = proj(0)
        va_ref[...] = proj(1)
        cos, slo, shi = _all_heads(cos_ref), _all_heads(slo_ref), _all_heads(shi_ref)
        q_ref[...] = _rope(proj(2), cos, slo, shi) * 0.125
        k_ref[...] = _rope(proj(3), cos, slo, shi)
        v_ref[...] = proj(4)

    row = pl.BlockSpec((tm, D), lambda i: (i, 0))
    half = pl.BlockSpec((tm, HW), lambda i: (i, 0))
    return _pallas_hbm(
        body, name="inproj_fwd", grid=(T // tm,),
        in_specs=[row, pl.BlockSpec((8, D), lambda i: (0, 0)), pl.BlockSpec((INW, D), lambda i: (0, 0))] +
                 [pl.BlockSpec((tm, CH), lambda i: (i, 0))] * 3,
        out_specs=[half] * 5,
        out_shape=[SDS((T, HW), f32)] * 5,
        compiler_params=_cp(("arbitrary",), 48),
    )(h, par, w_in_t, *tabs)


def _head_masks():
    lane = lax.broadcasted_iota(jnp.int32, (1, CH), 1)
    return lane < 64, lane >= 64


def _lane(t, idx):
    return jnp.sum(jnp.where(lax.broadcasted_iota(jnp.int32, t.shape, 1) == idx, t, 0.0), axis=-1, keepdims=True)


def _stack_heads(x):
    lo, hi = _head_masks()
    zero = jnp.zeros_like(x)
    return jnp.concatenate([jnp.where(lo, x, zero), jnp.where(hi, x, zero)], axis=0)


def _band_mask(has_prev):
    row = lax.broadcasted_iota(jnp.int32, (2 * CH, 2 * CH), 0) & (CH - 1)
    col = lax.broadcasted_iota(jnp.int32, (2 * CH, 2 * CH), 1)
    in_prev = jnp.logical_and(jnp.logical_and(col < CH, col >= row), has_prev)
    return jnp.logical_or(in_prev, jnp.logical_and(col >= CH, col - CH <= row))


def _attn_tiling(d):
    return CH * 16, 16 // d, CH


def _attn_fwd(q, k, v, d):
    rows, blocks, lanes = _attn_tiling(d)
    pairs = lanes // CH

    def body(q_ref, kc_ref, kp_ref, vc_ref, vp_ref, o_ref, lse_ref):
        c, lb = pl.program_id(0), pl.program_id(1)
        col = lax.broadcasted_iota(jnp.int32, (CH, CH), 1)
        lo, _ = _head_masks()

        @pl.when(lb == 0)
        def _():
            lse_ref[...] = jnp.zeros_like(lse_ref)

        at = lambda r, b: pl.ds(r + b * CH * d, CH, stride=d) if d > 1 else pl.ds(b * CH, CH)
        for r in range(d):
            for b in range(blocks):
                own = at(r, b)
                k_prev, v_prev, before, mask = ((kc_ref, vc_ref, at(r, b - 1), _band_mask(True)) if b > 0 else
                                                (kp_ref, vp_ref, at(r, blocks - 1), _band_mask(c > 0)))
                lse_tile = lse_ref[own, :]
                for pi in range(pairs):
                    sl = slice(pi * CH, (pi + 1) * CH)
                    hp = lb * pairs + pi
                    kk = jnp.concatenate([k_prev[before, sl], kc_ref[own, sl]], axis=0).astype(bf16)
                    vv = jnp.concatenate([v_prev[before, sl], vc_ref[own, sl]], axis=0).astype(bf16)
                    s = jnp.where(mask, _nt(_stack_heads(q_ref[own, sl].astype(bf16)), kk), -jnp.inf)
                    m = jnp.max(s, axis=-1, keepdims=True)
                    p = jnp.exp(s - m)
                    den = jnp.sum(p, axis=-1, keepdims=True)
                    o = _nn(p.astype(bf16), vv) / den
                    o_ref[own, sl] = jnp.where(lo, o[:CH], o[CH:])
                    lse = m + jnp.log(den)
                    lse_tile = jnp.where(col == 2 * hp, lse[:CH], jnp.where(col == 2 * hp + 1, lse[CH:], lse_tile))
                lse_ref[own, :] = lse_tile

    cur = pl.BlockSpec((rows, lanes), lambda c, lb: (c, lb))
    prev = pl.BlockSpec((rows, lanes), lambda c, lb: (jnp.maximum(c - 1, 0), lb))
    return _pallas_hbm(
        body, name="attn_fwd_d%d" % d, grid=(T // rows, HW // lanes), in_specs=[cur, cur, prev, cur, prev],
        out_specs=[cur, pl.BlockSpec((rows, CH), lambda c, lb: (c, 0))],
        out_shape=[SDS((T, HW), f32), SDS((T, CH), f32)],
        compiler_params=_cp(("arbitrary", "arbitrary"), 40),
    )(q, k, k, v, v)


def _attn_combine(os_, lses):
    tm = 512

    def body(o0_ref, o1_ref, o2_ref, l0_ref, l1_ref, l2_ref, o_ref, lse_ref):
        l0, l1, l2 = l0_ref[...], l1_ref[...], l2_ref[...]
        m = jnp.maximum(jnp.maximum(l0, l1), l2)
        e = [jnp.exp(l0 - m), jnp.exp(l1 - m), jnp.exp(l2 - m)]
        s = e[0] + e[1] + e[2]
        w = [ei / s for ei in e]
        lse_ref[...] = m + jnp.log(s)
        lo, _ = _head_masks()
        for hp in range(4):
            sl = slice(hp * CH, (hp + 1) * CH)
            acc = jnp.zeros((tm, CH), f32)
            for wp, op_ref in zip(w, (o0_ref, o1_ref, o2_ref)):
                wexp = jnp.where(lo, wp[:, 2 * hp:2 * hp + 1], wp[:, 2 * hp + 1:2 * hp + 2])
                acc = acc + wexp * op_ref[:, sl].astype(f32)
            o_ref[:, sl] = acc

    half = pl.BlockSpec((tm, HW), lambda i: (i, 0))
    stat = pl.BlockSpec((tm, CH), lambda i: (i, 0))
    return _pallas_hbm(
        body, name="attn_combine", grid=(T // tm,), in_specs=[half] * 3 + [stat] * 3,
        out_specs=[half, stat], out_shape=[SDS((T, HW), f32), SDS((T, CH), f32)],
        compiler_params=_cp(("arbitrary",), 32),
    )(*os_, *lses)


def _attn_bwd(q, k, v, o, do, lse, d, running=None):
    rows, blocks, lanes = _attn_tiling(d)
    pairs = lanes // CH
    steps = T // rows

    def body(qc_ref, qn_ref, kc_ref, kp_ref, vc_ref, vp_ref, oc_ref, on_ref, dc_ref, dn_ref, lc_ref, ln_ref, *rest):
        dq_ref, dk_ref, dv_ref = rest[-3:]
        sofar = (lambda i, rows, sl: rest[i][rows, sl]) if running is not None else (lambda i, rows, sl: 0.0)
        c, lb = pl.program_id(0), pl.program_id(1)
        lo, _ = _head_masks()
        at = lambda r, b: pl.ds(r + b * CH * d, CH, stride=d) if d > 1 else pl.ds(b * CH, CH)
        for r in range(d):
            for b in range(blocks):
                own = at(r, b)
                k_prev, v_prev, before, mask_c = ((kc_ref, vc_ref, at(r, b - 1), _band_mask(True)) if b > 0 else
                                                  (kp_ref, vp_ref, at(r, blocks - 1), _band_mask(c > 0)))
                q_next, o_next, d_next, l_next, after, has_next = (
                    (qc_ref, oc_ref, dc_ref, lc_ref, at(r, b + 1), True) if b + 1 < blocks else
                    (qn_ref, on_ref, dn_ref, ln_ref, at(r, 0), c < steps - 1))
                mask_n = _band_mask(has_next)[:, :CH]
                lse_c, lse_n = lc_ref[own, :], l_next[after, :]
                for pi in range(pairs):
                    sl = slice(pi * CH, (pi + 1) * CH)
                    hp = lb * pairs + pi
                    kc, vc = kc_ref[own, sl].astype(bf16), vc_ref[own, sl].astype(bf16)
                    kk = jnp.concatenate([k_prev[before, sl].astype(bf16), kc], axis=0)
                    vv = jnp.concatenate([v_prev[before, sl].astype(bf16), vc], axis=0)
                    doc, don = dc_ref[own, sl], d_next[after, sl]
                    qs_c, qs_n = _stack_heads(qc_ref[own, sl].astype(bf16)), _stack_heads(q_next[after, sl].astype(bf16))
                    ds_c, ds_n = _stack_heads(doc.astype(bf16)), _stack_heads(don.astype(bf16))
                    delta_c = jnp.sum(_stack_heads(doc * oc_ref[own, sl]), axis=-1, keepdims=True)
                    delta_n = jnp.sum(_stack_heads(don * o_next[after, sl]), axis=-1, keepdims=True)
                    heads = lambda t: jnp.concatenate([_lane(t, 2 * hp), _lane(t, 2 * hp + 1)], axis=0)
                    p1 = jnp.where(mask_c, jnp.exp(_nt(qs_c, kk) - heads(lse_c)), 0.0)
                    g1 = (p1 * (_nt(ds_c, vv) - delta_c)).astype(bf16)
                    dq = _nn(g1, kk)
                    dq_ref[own, sl] = sofar(0, own, sl) + jnp.where(lo, dq[:CH], dq[CH:])
                    p2 = jnp.where(mask_n, jnp.exp(_nt(qs_n, kc) - heads(lse_n)), 0.0)
                    g2 = (p2 * (_nt(ds_n, vc) - delta_n)).astype(bf16)
                    dk_ref[own, sl] = sofar(1, own, sl) + _tn(jnp.concatenate([g1[:, CH:], g2], axis=0),
                                                              jnp.concatenate([qs_c, qs_n], axis=0))
                    dv_ref[own, sl] = sofar(2, own, sl) + _tn(
                        jnp.concatenate([p1[:, CH:].astype(bf16), p2.astype(bf16)], axis=0),
                        jnp.concatenate([ds_c, ds_n], axis=0))

    cur = pl.BlockSpec((rows, lanes), lambda c, lb: (c, lb))
    prev = pl.BlockSpec((rows, lanes), lambda c, lb: (jnp.maximum(c - 1, 0), lb))
    nxt = pl.BlockSpec((rows, lanes), lambda c, lb: (jnp.minimum(c + 1, steps - 1), lb))
    scur = pl.BlockSpec((rows, CH), lambda c, lb: (c, 0))
    snxt = pl.BlockSpec((rows, CH), lambda c, lb: (jnp.minimum(c + 1, steps - 1), 0))
    more = [] if running is None else list(running)
    return _pallas_hbm(
        body, name="attn_bwd_d%d" % d, grid=(steps, HW // lanes),
        in_specs=[cur, nxt, cur, prev, cur, prev, cur, nxt, cur, nxt, scur, snxt] + [cur] * len(more),
        out_specs=[cur] * 3, out_shape=[SDS((T, HW), f32)] * 3,
        input_output_aliases={12 + i: i for i in range(len(more))},
        compiler_params=_cp(("arbitrary", "arbitrary"), 52),
    )(q, q, k, k, v, v, o, o, do, do, lse, lse, *more)


def _causal(w):
    row = lax.broadcasted_iota(jnp.int32, (CH, CH), 0)
    col = lax.broadcasted_iota(jnp.int32, (CH, CH), 1)
    return jnp.where(col <= row, w, 0.0)


def _sgu_fwd(ua, va, o, lng, lnb, ws, bs_t):
    tm = 512

    def body(ua_ref, va_ref, o_ref, lg_ref, lb_ref, ws_ref, bs_ref, mix_ref):
        for hd in range(4):
            sl = slice(hd * CH, (hd + 1) * CH)
            w = _causal(ws_ref[hd]).astype(bf16)
            for cc in range(tm // CH):
                rs = slice(cc * CH, (cc + 1) * CH)
                u = _gelu(ua_ref[rs, sl])
                v = _gelu(va_ref[rs, sl])
                vc = v - _rowmean(v)
                vn = vc * lax.rsqrt(_rowmean(vc * vc) + EPS) * lg_ref[:, sl] + lb_ref[:, sl]
                z = _nn(w, vn.astype(bf16)) + bs_ref[:, hd:hd + 1]
                mix_ref[rs, sl] = (u * z).astype(bf16)
        mix_ref[:, HW:] = o_ref[...].astype(bf16)

    half = pl.BlockSpec((tm, HW), lambda i: (i, 0))
    vec = pl.BlockSpec((1, HW), lambda i: (0, 0))
    return _pallas_hbm(
        body, name="sgu_fwd", grid=(T // tm,),
        in_specs=[half, half, half, vec, vec, pl.BlockSpec((4, CH, CH), lambda i: (0, 0, 0)),
                  pl.BlockSpec((CH, 4), lambda i: (0, 0))],
        out_specs=pl.BlockSpec((tm, 2 * HW), lambda i: (i, 0)), out_shape=SDS((T, 2 * HW), bf16),
        compiler_params=_cp(("arbitrary",), 32),
    )(ua, va, o, lng, lnb, ws, bs_t)


def _sgu_bwd(ua, va, d_a, lng, lnb, ws, bs_t):
    tm = 512

    def body(ua_ref, va_ref, d_ref, lg_ref, lb_ref, ws_ref, bs_ref, dsg_ref, dln_ref, dws_ref, db_ref):
        @pl.when(pl.program_id(0) == 0)
        def _():
            dln_ref[...] = jnp.zeros_like(dln_ref)
            dws_ref[...] = jnp.zeros_like(dws_ref)
            db_ref[...] = jnp.zeros_like(db_ref)

        for hd in range(4):
            sl = slice(hd * CH, (hd + 1) * CH)
            w = _causal(ws_ref[hd])
            w, wt = w.astype(bf16), w.T.astype(bf16)
            lg = lg_ref[:, sl]
            for cc in range(tm // CH):
                rs = slice(cc * CH, (cc + 1) * CH)
                xa, xv, dd = ua_ref[rs, sl], va_ref[rs, sl], d_ref[rs, sl]
                u, v = _gelu(xa), _gelu(xv)
                vc = v - _rowmean(v)
                rstd = lax.rsqrt(_rowmean(vc * vc) + EPS)
                xh = vc * rstd
                vnb = (xh * lg + lb_ref[:, sl]).astype(bf16)
                z = _nn(w, vnb) + bs_ref[:, hd:hd + 1]
                dz = dd * u
                dzb = dz.astype(bf16)
                dsg_ref[rs, sl] = (dd * z * _gelu_grad(xa)).astype(bf16)
                dws_ref[hd] += _nt(dzb, vnb)
                db_ref[hd] += dz
                dvn = _nn(wt, dzb)
                dln_ref[0:1, sl] += _colsum(dvn * xh)
                dln_ref[1:2, sl] += _colsum(dvn)
                dxh = dvn * lg
                dv = rstd * (dxh - _rowmean(dxh) - xh * _rowmean(dxh * xh))
                dsg_ref[rs, HW + hd * CH:HW + (hd + 1) * CH] = (dv * _gelu_grad(xv)).astype(bf16)

    half = pl.BlockSpec((tm, HW), lambda i: (i, 0))
    vec = pl.BlockSpec((1, HW), lambda i: (0, 0))
    mat = pl.BlockSpec((4, CH, CH), lambda i: (0, 0, 0))
    return _pallas_hbm(
        body, name="sgu_bwd", grid=(T // tm,),
        in_specs=[half, half, half, vec, vec, mat, pl.BlockSpec((CH, 4), lambda i: (0, 0))],
        out_specs=[pl.BlockSpec((tm, 2 * HW), lambda i: (i, 0)), pl.BlockSpec((8, HW), lambda i: (0, 0)), mat, mat],
        out_shape=[SDS((T, 2 * HW), bf16), SDS((8, HW), f32), SDS((4, CH, CH), f32), SDS((4, CH, CH), f32)],
        compiler_params=_cp(("arbitrary",), 32),
    )(ua, va, d_a, lng, lnb, ws, bs_t)


def _outproj_fwd(mixed, w_out, h, par):
    tm = 1024

    def body(mix_ref, w_ref, h_ref, par_ref, ho_ref, po_ref):
        p = _nn(mix_ref[...], w_ref[...])
        po_ref[...] = p.astype(bf16)
        ho_ref[...] = h_ref[...] + par_ref[2:3, :] * p

    row = pl.BlockSpec((tm, D), lambda i: (i, 0))
    return _pallas_hbm(
        body, name="outproj_fwd", grid=(T // tm,),
        in_specs=[row, pl.BlockSpec((D, D), lambda i: (0, 0)), row, pl.BlockSpec((8, D), lambda i: (0, 0))],
        out_specs=[row, row], out_shape=[SDS((T, D), f32), SDS((T, D), bf16)],
        compiler_params=_cp(("arbitrary",), 48),
    )(mixed, w_out, h, par)


def _outproj_bwd(dh, po, w_out, par):
    tm = 1024

    def body(dh_ref, po_ref, w_ref, par_ref, do_ref, da_ref, db_ref, dpar_ref):
        @pl.when(pl.program_id(0) == 0)
        def _():
            dpar_ref[...] = jnp.zeros_like(dpar_ref)

        dh_v = dh_ref[...]
        dob = (par_ref[2:3, :] * dh_v).astype(bf16)
        do_ref[...] = dob
        dpar_ref[2:3, :] += _colsum(dh_v * po_ref[...].astype(f32))
        dm = _nt(dob, w_ref[...])
        da_ref[...] = dm[:, :HW]
        db_ref[...] = dm[:, HW:]

    row = pl.BlockSpec((tm, D), lambda i: (i, 0))
    half = pl.BlockSpec((tm, HW), lambda i: (i, 0))
    one = pl.BlockSpec((8, D), lambda i: (0, 0))
    return _pallas_hbm(
        body, name="outproj_bwd", grid=(T // tm,),
        in_specs=[row, row, pl.BlockSpec((D, D), lambda i: (0, 0)), one],
        out_specs=[row, half, half, one],
        out_shape=[SDS((T, D), bf16), SDS((T, HW), f32), SDS((T, HW), f32), SDS((8, D), f32)],
        compiler_params=_cp(("arbitrary",), 48),
    )(dh, po, w_out, par)


def _inproj_bwd_tok(dh, h, par, w_in_t, dsg, dq, dk, dv, tabs):
    tm = 512

    def body(dh_ref, h_ref, par_ref, w_ref, dsg_ref, dq_ref, dk_ref, dv_ref,
             cos_ref, slo_ref, shi_ref, dhin_ref, dp_ref, y_ref, dpar_ref):
        @pl.when(pl.program_id(0) == 0)
        def _():
            dpar_ref[...] = jnp.zeros_like(dpar_ref)

        cos, slo, shi = _all_heads(cos_ref), _all_heads(slo_ref), _all_heads(shi_ref)
        dp_ref[:, :2 * HW] = dsg_ref[...]
        dp_ref[:, 2 * HW:3 * HW] = _rope_t(dq_ref[...] * 0.125, cos, slo, shi).astype(bf16)
        dp_ref[:, 3 * HW:4 * HW] = _rope_t(dk_ref[...], cos, slo, shi).astype(bf16)
        dp_ref[:, 4 * HW:] = dv_ref[...].astype(bf16)
        dy = _nn(dp_ref[...], w_ref[...])
        par_v, h_v = par_ref[...], h_ref[...]
        y_ref[...] = _norm_mod(h_v, par_v)[0].astype(bf16)
        dx, dsh, dsc, dng = _norm_mod_bwd(dy, h_v, par_v)
        dhin_ref[...] = dh_ref[...] + dx
        dpar_ref[0:1, :] += dsh
        dpar_ref[1:2, :] += dsc
        dpar_ref[3:4, :] += dng

    row = pl.BlockSpec((tm, D), lambda i: (i, 0))
    half = pl.BlockSpec((tm, HW), lambda i: (i, 0))
    one = pl.BlockSpec((8, D), lambda i: (0, 0))
    return _pallas_hbm(
        body, name="inproj_bwd_tok", grid=(T // tm,),
        in_specs=[row, row, one, pl.BlockSpec((INW, D), lambda i: (0, 0)), row] + [half] * 3 +
                 [pl.BlockSpec((tm, CH), lambda i: (i, 0))] * 3,
        out_specs=[row, pl.BlockSpec((tm, INW), lambda i: (i, 0)), row, one],
        out_shape=[SDS((T, D), f32), SDS((T, INW), bf16), SDS((T, D), bf16), SDS((8, D), f32)],
        compiler_params=_cp(("arbitrary",), 56),
    )(dh, h, par, w_in_t, dsg, dq, dk, dv, *tabs)


def _loss_head(h, par, target):
    tm = 512

    def body(h_ref, par_ref, t_ref, dh_ref, acc_ref):
        @pl.when(pl.program_id(0) == 0)
        def _():
            acc_ref[...] = jnp.zeros_like(acc_ref)

        x, g = h_ref[...], par_ref[3:4, :]
        rstd = lax.rsqrt(_rowmean(x * x) + EPS)
        xhat = x * rstd
        err = xhat * g - t_ref[...]
        acc_ref[0:1, :] += _colsum(err * err)
        dy = err * (1.0 / D)
        acc_ref[3:4, :] += _colsum(dy * xhat)
        dxhat = dy * g
        dh_ref[...] = rstd * (dxhat - xhat * _rowmean(dxhat * xhat))

    row = pl.BlockSpec((tm, D), lambda i: (i, 0))
    one = pl.BlockSpec((8, D), lambda i: (0, 0))
    return _pallas_hbm(
        body, name="loss_head", grid=(T // tm,), in_specs=[row, one, row], out_specs=[row, one],
        out_shape=[SDS((T, D), f32), SDS((8, D), f32)], compiler_params=_cp(("arbitrary",), 32),
    )(h, par, target)


ELEMENTWISE_VMEM_BUDGET = 20 << 20


def _block_rows(rows, bytes_per_row):
    cap = ELEMENTWISE_VMEM_BUDGET // bytes_per_row
    if rows <= cap:
        return rows
    return next(b for b in range(cap - cap % 16, 0, -16) if rows % b == 0)


def _sum_slots(land, name):
    _, R, C = land.shape
    br = _block_rows(R, 2 * NDEV * C * land.dtype.itemsize + 2 * C * 4)

    def body(l_ref, o_ref):
        acc = l_ref[0].astype(f32)
        for s in range(1, NDEV):
            acc = acc + l_ref[s].astype(f32)
        o_ref[...] = acc

    return _pallas_hbm(
        body, name=name, grid=(R // br,), in_specs=[pl.BlockSpec((NDEV, br, C), lambda i: (0, i, 0))],
        out_specs=pl.BlockSpec((br, C), lambda i: (i, 0)), out_shape=SDS((R, C), f32),
        compiler_params=_cp(("arbitrary",), 32),
    )(land)


def _sum_partials(land, own, me, name):
    r = land.shape[-2]

    def body(me_ref, l_ref, own_ref, o_ref):
        mine = own_ref[...].astype(f32)
        acc = jnp.where(me_ref[0] == 0, mine, l_ref[0].astype(f32))
        for s in range(1, NDEV):
            acc = acc + jnp.where(me_ref[0] == s, mine, l_ref[s].astype(f32))
        o_ref[...] = acc

    if own.ndim == 3:
        kk = own.shape[0]
        specs = dict(grid=(kk,),
                     in_specs=[pl.BlockSpec((NDEV, None, r, D), lambda i, me_ref: (0, i, 0, 0)),
                               pl.BlockSpec((None, r, D), lambda i, me_ref: (i, me_ref[0], 0))],
                     out_specs=pl.BlockSpec((None, r, D), lambda i, me_ref: (i, 0, 0)))
        out_shape = SDS((kk, r, D), f32)
    else:
        specs = dict(grid=(1,),
                     in_specs=[pl.BlockSpec((NDEV, r, D), lambda i, me_ref: (0, 0, 0)),
                               pl.BlockSpec((r, D), lambda i, me_ref: (me_ref[0], 0))],
                     out_specs=pl.BlockSpec((r, D), lambda i, me_ref: (0, 0)))
        out_shape = SDS((r, D), f32)
    return _pallas_hbm(
        body, name=name, out_shape=out_shape,
        grid_spec=pltpu.PrefetchScalarGridSpec(num_scalar_prefetch=1, **specs),
        compiler_params=_cp(("arbitrary",), 32),
    )(me.reshape(1), land, own)


def _adamw(w, g, m, v, name):
    R, C = w.shape
    br = _block_rows(R, 2 * 7 * C * 4)

    def body(w_ref, g_ref, m_ref, v_ref, d_ref, mo_ref, vo_ref):
        gv = g_ref[...]
        m2 = B1 * m_ref[...] + (1.0 - B1) * gv
        v2 = B2 * v_ref[...] + (1.0 - B2) * (gv * gv)
        mo_ref[...] = m2
        vo_ref[...] = v2
        m_hat = m2 / (1.0 - B1 ** STEP)
        v_hat = v2 / (1.0 - B2 ** STEP)
        d_ref[...] = -LR * (m_hat / (jnp.sqrt(v_hat) + AEPS) + WD * w_ref[...])

    blk = pl.BlockSpec((br, C), lambda i: (i, 0))
    return _pallas_hbm(
        body, name=name, grid=(R // br,), in_specs=[blk] * 4, out_specs=[blk] * 3,
        out_shape=[SDS((R, C), f32)] * 3, compiler_params=_cp(("arbitrary",), 32),
    )(w, g, m, v)


def _adamw_nd(w, g, m, v, name, swapped=False):
    if swapped:
        outs = _adamw_nd(*(jnp.swapaxes(a, -1, -2) for a in (w, g, m, v)), name)
        return [jnp.swapaxes(o, -1, -2) for o in outs]
    shp = w.shape
    r2 = (-1, shp[-1]) if w.ndim > 1 else (8, shp[0] // 8)
    outs = _adamw(w.reshape(r2), g.reshape(r2), m.reshape(r2), v.reshape(r2), name)
    return [o.reshape(shp) for o in outs]


def _par_rows(mod_l, s, gain):
    rows = jnp.pad(mod_l.reshape(9, D)[3 * s:3 * s + 3], ((0, 5), (0, 0)))
    return rows + jnp.pad(gain[None, :], ((3, 4), (0, 0)))


def _pad_rows(a):
    a = a.reshape(-1, CH)
    return jnp.pad(a, ((0, (-a.shape[0]) % 8), (0, 0)))


def _prepare(me, c, ada_w, ada_b, norm_g):
    pay = jnp.pad(c, ((0, 7), (0, 0)))
    pay = jnp.concatenate([pay, jnp.pad(norm_g.reshape(6, OUTS), ((0, 2), (0, D - OUTS)))], axis=0)
    got = _all_gather_small(pay, "gather_c")
    c_all = got[:, 0, :]
    gains = got[:, 8:14, :OUTS].transpose(1, 0, 2).reshape(DEPTH, 3, D)

    ada_b_loc = lax.dynamic_slice(ada_b, (0, me * ADAS), (DEPTH, ADAS)).reshape(DEPTH, 1, ADAS)
    mod_cols = _mod_fwd(c_all, ada_w, ada_b_loc)
    got = _all_gather_small(mod_cols.reshape(DEPTH * NDEV, ADAS), "gather_mod").reshape(NDEV, DEPTH, NDEV, ADAS)
    mod = lax.dynamic_index_in_dim(got, me, axis=2, keepdims=False).transpose(1, 0, 2).reshape(DEPTH, 9 * D)
    pars = [[_par_rows(mod[l], s, gains[l, s]) for s in range(3)] for l in range(DEPTH)]
    return c_all, pars


def _fwd_bwd(x2, target, pars, get_w, put_g, small_ready, sgu_ln_g, sgu_ln_b, sgu_w, sgu_b, final_g):
    tabs = _rope_tables()
    tril = jnp.tril(jnp.ones((CH, CH), dtype=bool))
    behind = lambda rows, token: rows if token is None else rows + token[0, 0]

    h = x2
    saved = []
    for l in range(DEPTH):
        lng, lnb = sgu_ln_g[l].reshape(1, HW), sgu_ln_b[l].reshape(1, HW)
        bs_t = sgu_b[l].T
        h0 = h
        w, token = get_w(4 * l, ("ffn1", l), h0)
        h1, fo1, g1, u1, a1 = _ffn_fwd(h0, behind(pars[l][0], token), w)
        w, token = get_w(4 * l + 2, ("inproj", l), h1)
        ua, va, q, k, v = _inproj_fwd(h1, behind(pars[l][1], token), w, tabs)
        branches = [_attn_fwd(q, k, v, d) for d in PATTERN_DILATIONS]
        o, lse = _attn_combine([b[0] for b in branches], [b[1] for b in branches])
        mixed = _sgu_fwd(ua, va, o, lng, lnb, sgu_w[l], bs_t)
        w, token = get_w(4 * l + 3, ("outproj", l), mixed)
        h2, po = _outproj_fwd(mixed, w, h1, behind(pars[l][1], token))
        w, token = get_w(4 * l + 1, ("ffn2", l), h2)
        h3, fo2, g2, u2, a2 = _ffn_fwd(h2, behind(pars[l][2], token), w)
        saved.append((h0, h1, h2, fo1, g1, u1, a1, ua, va, q, k, v, o, mixed, lse, po, fo2, g2, u2, a2))
        h = h3

    par_f = jnp.pad(final_g[None, :], ((3, 4), (0, 0)))
    dh, head = _loss_head(h, par_f, target)

    dmods, dgains, dsgu = [None] * DEPTH, [None] * DEPTH, [None] * DEPTH
    token = None
    for l in reversed(range(DEPTH)):
        w_f1, w_f2, w_i, w_o = (get_w(4 * l + j, None, None)[0] for j in (0, 1, 2, 3))
        h0, h1, h2, fo1, g1, u1, a1, ua, va, q, k, v, o, mixed, lse, po, fo2, g2, u2, a2 = saved[l]
        lng, lnb = sgu_ln_g[l].reshape(1, HW), sgu_ln_b[l].reshape(1, HW)
        bs_t = sgu_b[l].T

        dh, dg_s, du_s, y, dfb, dpar3 = _ffn_bwd_tok(dh, h2, behind(pars[l][2], token), fo2, g2, u2, w_f2)
        token = put_g(4 * l + 1, _ffn_bwd_w(a2, dg_s, du_s, y, dfb))

        dob, d_a, d_b, dpar2g = _outproj_bwd(dh, po, w_o, behind(pars[l][1], token))
        token = put_g(4 * l + 3, _tn_matmul(mixed, dob, 512, "w_out_grad"))
        dqkv = None
        for d in PATTERN_DILATIONS:
            dqkv = _attn_bwd(q, k, v, o, d_b, lse, d, dqkv)
        dsg, dln, dws, dbl = _sgu_bwd(ua, va, d_a, lng, lnb, sgu_w[l], bs_t)
        dh, dp, y2, dpar2 = _inproj_bwd_tok(dh, h1, behind(pars[l][1], token), w_i, dsg, *dqkv, tabs)
        token = put_g(4 * l + 2, _tn_matmul(dp, y2, 640, "w_in_grad"))

        dh, dg_s, du_s, y, dfb, dpar1 = _ffn_bwd_tok(dh, h0, behind(pars[l][0], token), fo1, g1, u1, w_f1)
        dmods[l] = jnp.concatenate([dpar1[0:3], dpar2[0:2], dpar2g[2:3], dpar3[0:3]], axis=0).reshape(9 * D)
        dgains[l] = jnp.stack([dpar1[3], dpar2[3], dpar3[3]])
        dsgu[l] = (dln[0], dln[1], jnp.where(tril[None], dws, 0.0), jnp.sum(dbl, axis=-1))
        token = small_ready(head, dmods, dgains, dsgu) if l == 0 else None
        token = put_g(4 * l, _ffn_bwd_w(a1, dg_s, du_s, y, dfb, token))
    return dh, token


def kernel(x, c, ada_w, ada_b, norm_g, ffn1_wg, ffn1_wu, ffn1_wd, ffn2_wg, ffn2_wu, ffn2_wd, w_in, sgu_ln_g, sgu_ln_b, sgu_w, sgu_b, w_out, final_g, loss_target, m_ada_w, m_ada_b, m_norm_g, m_ffn1_wg, m_ffn1_wu, m_ffn1_wd, m_ffn2_wg, m_ffn2_wu, m_ffn2_wd, m_w_in, m_sgu_ln_g, m_sgu_ln_b, m_sgu_w, m_sgu_b, m_w_out, m_final_g, v_ada_w, v_ada_b, v_norm_g, v_ffn1_wg, v_ffn1_wu, v_ffn1_wd, v_ffn2_wg, v_ffn2_wu, v_ffn2_wd, v_w_in, v_sgu_ln_g, v_sgu_ln_b, v_sgu_w, v_sgu_b, v_w_out, v_final_g):
    me = 4 * lax.axis_index("x") + 2 * lax.axis_index("y") + lax.axis_index("c")

    tr = lambda w: jnp.swapaxes(w, -1, -2).astype(bf16)
    locs = []
    for l in range(DEPTH):
        locs.append(jnp.stack([tr(ffn1_wg[l]), tr(ffn1_wu[l]), ffn1_wd[l].astype(bf16)]))
        locs.append(jnp.stack([tr(ffn2_wg[l]), tr(ffn2_wu[l]), ffn2_wd[l].astype(bf16)]))
        locs.append(tr(w_in[l]))
        locs.append(w_out[l].astype(bf16))

    c_all, pars = _prepare(me, c, ada_w, ada_b, norm_g)
    locs, pars = lax.optimization_barrier((locs, pars))

    first, locs = lax.optimization_barrier((_all_gather_rows([locs[0]], "gather_first")[0], locs))
    placed = [_place_own(a, me) for a in locs[1:]]
    ready = {0: first}
    groups = ([2, 3], [1], [4], [6, 7], [5])
    plan = {("inproj", 0): (("wait", "gather", 0), ("start", "chips", 1)),
            ("outproj", 0): (("wait", "chips", 1), ("start", "pass", 1), ("start", "chips", 2)),
            ("ffn2", 0): (("wait", "pass", 1),),
            ("ffn1", 1): (("wait", "chips", 2), ("start", "pass", 2), ("start", "gather", 3), ("wait", "pass", 2)),
            ("inproj", 1): (("wait", "gather", 3), ("start", "chips", 4)),
            ("outproj", 1): (("wait", "chips", 4), ("start", "pass", 4)),
            ("ffn2", 1): (("wait", "pass", 4),)}
    flying = {}

    def exchange(action, mode, gi, after):
        if action == "start":
            arrays = flying.pop(gi) if mode == "pass" else [placed[p - 1] for p in groups[gi]]
            arrays, _ = lax.optimization_barrier((arrays, after))
            started = _exchange_start(mode, arrays, "%s_start_%d" % (mode, gi))
            flying[gi] = started[:-1]
            return started[-1]
        arrays = _exchange_wait(mode, flying.pop(gi), after, "%s_wait_%d" % (mode, gi))
        if mode == "chips":
            flying[gi] = arrays
        else:
            ready.update(zip(groups[gi], arrays))
        return None

    first_token = exchange("start", "gather", 0, first)

    def get_w(piece, point, after):
        token = first_token if point == ("ffn1", 0) else None
        for action, mode, gi in plan.get(point, ()):
            started = exchange(action, mode, gi, after)
            if started is not None:
                token = after = started
        return ready.get(piece), token

    sent = {}

    def put_g(piece, grad):
        land = lax.empty((NDEV,) + grad.shape[:-2] + (grad.shape[-2] // NDEV, D), bf16)
        started = _exchange_start("scatter", [grad, land], "scatter_start_%d" % piece)
        sent[piece] = started[:-1]
        return started[-1]

    small_sent = []

    def small_ready(head, dmods, dgains, dsgu):
        loss_part = 0.5 * jnp.sum(head[0]) / D
        small = jnp.concatenate([
            _pad_rows(jnp.stack(dmods)), _pad_rows(jnp.stack(dgains)),
            _pad_rows(jnp.stack([s[0] for s in dsgu])), _pad_rows(jnp.stack([s[1] for s in dsgu])),
            _pad_rows(jnp.stack([s[3] for s in dsgu])), _pad_rows(jnp.stack([s[2] for s in dsgu])),
            _pad_rows(head[3]), _pad_rows(jnp.pad(loss_part[None], (0, CH - 1)))], axis=0)
        started = _exchange_start("gather", [_place_own(small, me, "place_small")], "small_start")
        small_sent.append(started[:-1])
        return started[-1]

    dh, last = _fwd_bwd(x[0], loss_target[0], pars, get_w, put_g, small_ready, sgu_ln_g, sgu_ln_b, sgu_w, sgu_b, final_g)
    grad_x = dh[None]

    got = _exchange_wait("gather", small_sent[0], last, "small_wait")[0].reshape(NDEV, -1, CH)
    tot = _sum_slots(got, "sum_small")
    n_mod, n_gain, n_sw = DEPTH * 9 * D // CH, DEPTH * 3 * D // CH, DEPTH * 4 * CH
    offs = [0, n_mod, n_mod + n_gain, n_mod + n_gain + 8, n_mod + n_gain + 16, n_mod + n_gain + 24]
    g_ada_b = tot[offs[0]:offs[1]].reshape(DEPTH, 9 * D)
    g_gain_full = tot[offs[1]:offs[2]].reshape(DEPTH, 3, D)
    g_ln_g = tot[offs[2]:offs[3]].reshape(DEPTH, 4, CH)
    g_ln_b = tot[offs[3]:offs[4]].reshape(DEPTH, 4, CH)
    g_sb = tot[offs[4]:offs[5]].reshape(DEPTH, 4, CH)
    g_sw = tot[offs[5]:offs[5] + n_sw].reshape(DEPTH, 4, CH, CH)
    g_final = tot[offs[5] + n_sw:offs[5] + n_sw + 8].reshape(D)
    loss = tot[offs[5] + n_sw + 8, 0]
    g_norm = lax.dynamic_slice(g_gain_full, (0, 0, me * OUTS), (DEPTH, 3, OUTS))

    dmod_all = got[:, offs[0]:offs[1]].reshape(NDEV, DEPTH, 9 * D)
    dmod_cols = lax.dynamic_slice(dmod_all, (0, 0, me * ADAS), (NDEV, DEPTH, ADAS)).transpose(1, 0, 2)
    g_ada_w = _ada_grad((c_all * jax.nn.sigmoid(c_all)).T, dmod_cols)

    sums, after = {}, tot

    def collect(piece, after):
        own, land = _exchange_wait("scatter", sent[piece], after, "scatter_wait_%d" % piece)
        sums[piece] = _sum_partials(land, own, me, "sum_grads")
        return sums[piece]

    for piece in (5, 7, 6, 4, 1, 3, 2):
        after = collect(piece, after)
    back = lambda t: jnp.swapaxes(t, -1, -2)
    f2 = jnp.stack([sums[4 * l + 1] for l in range(DEPTH)])
    g_w_in = back(jnp.stack([sums[4 * l + 2] for l in range(DEPTH)]))
    g_w_out = jnp.stack([sums[4 * l + 3] for l in range(DEPTH)])

    ws = [ada_w, ada_b, norm_g, ffn1_wg, ffn1_wu, ffn1_wd, ffn2_wg, ffn2_wu, ffn2_wd, w_in, sgu_ln_g, sgu_ln_b, sgu_w,
          sgu_b, w_out, final_g]
    ms = [m_ada_w, m_ada_b, m_norm_g, m_ffn1_wg, m_ffn1_wu, m_ffn1_wd, m_ffn2_wg, m_ffn2_wu, m_ffn2_wd, m_w_in,
          m_sgu_ln_g, m_sgu_ln_b, m_sgu_w, m_sgu_b, m_w_out, m_final_g]
    vs = [v_ada_w, v_ada_b, v_norm_g, v_ffn1_wg, v_ffn1_wu, v_ffn1_wd, v_ffn2_wg, v_ffn2_wu, v_ffn2_wd, v_w_in,
          v_sgu_ln_g, v_sgu_ln_b, v_sgu_w, v_sgu_b, v_w_out, v_final_g]
    gw = [g_ada_w, g_ada_b, g_norm, None, None, None, back(f2[:, 0]), back(f2[:, 1]), f2[:, 2],
          g_w_in, g_ln_g, g_ln_b, g_sw, g_sb, g_w_out, g_final]
    upd = [None] * len(ws)
    lane_narrow = (3, 4, 6, 7, 9)
    for i in (1, 2, 10, 11, 12, 13, 15, 6, 7, 8, 9, 14, 0):
        upd[i] = _adamw_nd(ws[i], gw[i], ms[i], vs[i], "adamw", i in lane_narrow)
    collect(0, upd[0][0])
    f1 = jnp.stack([sums[4 * l] for l in range(DEPTH)])
    gw[3:6] = [back(f1[:, 0]), back(f1[:, 1]), f1[:, 2]]
    for i in (3, 4, 5):
        upd[i] = _adamw_nd(ws[i], gw[i], ms[i], vs[i], "adamw", i in lane_narrow)
    return (loss, grad_x, *gw, *[u[0] for u in upd], *[u[1] for u in upd], *[u[2] for u in upd])
```

```python
import functools
import math

import jax
import jax.numpy as jnp
from jax import lax
from jax.experimental import pallas as pl
from jax.experimental.pallas import tpu as pltpu

f32, bf16 = jnp.float32, jnp.bfloat16
SDS = jax.ShapeDtypeStruct

T, D, F = 4096, 1024, 2816
NDEV, DEPTH = 8, 2
HW = 512
INW = 5 * HW
FS, INS, OUTS, ADAS = F // NDEV, INW // NDEV, D // NDEV, 9 * D // NDEV
CH = 128
PATTERN_DILATIONS = (1, 4, 16)
ROPE_THETA = 10000.0
EPS = 1e-6
LR, B1, B2, AEPS, WD, STEP = 0.001, 0.9, 0.999, 1e-08, 0.01, 10
MESH = pl.DeviceIdType.MESH


def _cp(sems, vmem_mb):
    return pltpu.CompilerParams(dimension_semantics=sems, vmem_limit_bytes=vmem_mb << 20)


def _pallas_hbm(body, *, out_shape, **kw):
    typed = jax.tree.map(lambda s: pltpu.HBM(s.shape, s.dtype), out_shape)
    call = pl.pallas_call(body, out_shape=typed, **kw)

    def run(*operands):
        pin = lambda x: x if x.dtype == jnp.int32 else pltpu.with_memory_space_constraint(x, pltpu.HBM)
        return call(*[pin(x) for x in operands])

    return run


def _nn(a, b):
    return lax.dot_general(a, b, (((1,), (0,)), ((), ())), preferred_element_type=f32)


def _nt(a, b):
    return lax.dot_general(a, b, (((1,), (1,)), ((), ())), preferred_element_type=f32)


def _tn(a, b):
    return lax.dot_general(a, b, (((0,), (0,)), ((), ())), preferred_element_type=f32)


def _colsum(a):
    return jnp.sum(a, axis=0, keepdims=True)


def _rowmean(a):
    return jnp.mean(a, axis=-1, keepdims=True)


def _norm_mod(x, par):
    rstd = lax.rsqrt(_rowmean(x * x) + EPS)
    xhat = x * rstd
    n = xhat * par[3:4, :]
    y = n * (1.0 + par[1:2, :]) + par[0:1, :]
    return y, n, xhat, rstd


def _norm_mod_bwd(dy, x, par):
    _, n, xhat, rstd = _norm_mod(x, par)
    dn = dy * (1.0 + par[1:2, :])
    dxhat = dn * par[3:4, :]
    dx = rstd * (dxhat - xhat * _rowmean(dxhat * xhat))
    return dx, _colsum(dy), _colsum(dy * n), _colsum(dn * xhat)


_GK = math.sqrt(2.0 / math.pi)


def _gelu(x):
    return 0.5 * x * (1.0 + jnp.tanh(_GK * (x + 0.044715 * x * x * x)))


def _gelu_grad(x):
    t = jnp.tanh(_GK * (x + 0.044715 * x * x * x))
    return 0.5 * (1.0 + t) + 0.5 * x * (1.0 - t * t) * (_GK * (1.0 + 3.0 * 0.044715 * x * x))


def _rows(ref, idx, r):
    if len(ref.shape) == 3:
        return ref.at[:, pl.ds(idx * r, r), :]
    return ref.at[pl.ds(idx * r, r), :]


def _flip(v, bit):
    return 1 - v if bit else v


def _all_gather_small(x, name):
    R, C = x.shape

    def body(x_ref, out_ref, send_sems, recv_sems):
        mx, my, mc = lax.axis_index("x"), lax.axis_index("y"), lax.axis_index("c")
        me = 4 * mx + 2 * my + mc
        out_ref[me] = x_ref[...]
        sent = []
        for k in range(1, NDEV):
            peer = (_flip(mx, k & 4), _flip(my, k & 2), _flip(mc, k & 1))
            cp = pltpu.make_async_remote_copy(
                src_ref=x_ref, dst_ref=out_ref.at[me], send_sem=send_sems.at[k - 1],
                recv_sem=recv_sems.at[k - 1], device_id=peer, device_id_type=MESH)
            cp.start()
            sent.append(cp)
        for k in range(1, NDEV):
            peer = (_flip(mx, k & 4), _flip(my, k & 2), _flip(mc, k & 1))
            pidx = 4 * peer[0] + 2 * peer[1] + peer[2]
            pltpu.make_async_remote_copy(
                src_ref=x_ref, dst_ref=out_ref.at[pidx], send_sem=send_sems.at[k - 1],
                recv_sem=recv_sems.at[k - 1], device_id=peer, device_id_type=MESH).wait_recv()
        for cp in sent:
            cp.wait_send()

    vm = pl.BlockSpec(memory_space=pltpu.VMEM)
    return pl.pallas_call(
        body, name=name, out_shape=SDS((NDEV, R, C), f32), in_specs=[vm], out_specs=vm,
        scratch_shapes=[pltpu.SemaphoreType.DMA((NDEV - 1,)), pltpu.SemaphoreType.DMA((NDEV - 1,))],
        compiler_params=pltpu.CompilerParams(vmem_limit_bytes=32 << 20),
    )(x)


def _all_gather_rows(locs, name):
    n = len(locs)
    rs = [a.shape[-2] for a in locs]

    def body(*refs):
        src, out = refs[:n], refs[n:2 * n]
        send_sems, recv_sems, loc_sems = refs[2 * n:]
        mx, my, mc = lax.axis_index("x"), lax.axis_index("y"), lax.axis_index("c")
        me, sib = (mx, my, mc), (mx, my, 1 - mc)
        chips = [(1 - mx, my), (mx, 1 - my), (1 - mx, 1 - my)]

        def blk(a, p):
            return _rows(out[a], 4 * p[0] + 2 * p[1] + p[2], rs[a])

        def copy(k, a, block, to, from_src=False):
            return pltpu.make_async_remote_copy(
                src_ref=src[a] if from_src else blk(a, block), dst_ref=blk(a, block),
                send_sem=send_sems.at[k * n + a], recv_sem=recv_sems.at[k * n + a],
                device_id=to, device_id_type=MESH)

        mine = [pltpu.make_async_copy(src[a], blk(a, me), loc_sems.at[a]) for a in range(n)]
        for m in mine:
            m.start()
        first = []
        for j, chip in enumerate(chips):
            first += [copy(1 + j, a, me, (*chip, mc), True) for a in range(n)]
        first += [copy(0, a, me, sib, True) for a in range(n)]
        for cp in first:
            cp.start()
        passed = []
        for j, chip in enumerate(chips):
            for a in range(n):
                copy(1 + j, a, (*chip, mc), me).wait_recv()
            fwd = [copy(4 + j, a, (*chip, mc), sib) for a in range(n)]
            for cp in fwd:
                cp.start()
            passed += fwd
        for a in range(n):
            copy(0, a, sib, me).wait_recv()
        for j, chip in enumerate(chips):
            for a in range(n):
                copy(4 + j, a, (*chip, 1 - mc), me).wait_recv()
        for cp in first + passed:
            cp.wait_send()
        for m in mine:
            m.wait()

    hbm = pl.BlockSpec(memory_space=pl.ANY)
    out_shape = [SDS(a.shape[:-2] + (NDEV * a.shape[-2], a.shape[-1]), a.dtype) for a in locs]
    return pl.pallas_call(
        body, name=name, out_shape=out_shape, in_specs=[hbm] * n, out_specs=[hbm] * n,
        scratch_shapes=[pltpu.SemaphoreType.DMA((7 * n,)), pltpu.SemaphoreType.DMA((7 * n,)),
                        pltpu.SemaphoreType.DMA((n,))],
    )(*locs)


HBM_SPEC = pl.BlockSpec(memory_space=pltpu.HBM)
SEM_SPEC = pl.BlockSpec(memory_space=pltpu.SEMAPHORE)
DATAFLOW_EFFECT = pltpu.SideEffectType.DATAFLOW_SIDE_EFFECTING


def _place_own(loc, me, name="place_own"):
    r, cols = loc.shape[-2:]
    loc3 = loc.reshape(-1, r, cols)
    kk = loc3.shape[0]

    def body(me_ref, src_ref, full_ref, out_ref):
        out_ref[...] = src_ref[...]

    out = _pallas_hbm(
        body, name=name, out_shape=SDS((kk, NDEV * r, cols), loc.dtype),
        grid_spec=pltpu.PrefetchScalarGridSpec(
            num_scalar_prefetch=1, grid=(kk,),
            in_specs=[pl.BlockSpec((None, r, cols), lambda i, me_ref: (i, 0, 0)), pl.BlockSpec(memory_space=pl.ANY)],
            out_specs=pl.BlockSpec((None, r, cols), lambda i, me_ref: (i, me_ref[0], 0))),
        input_output_aliases={2: 0}, compiler_params=_cp(("arbitrary",), 32),
    )(me.reshape(1), loc3, lax.empty((kk, NDEV * r, cols), loc.dtype))
    return out.reshape(loc.shape[:-2] + (NDEV * r, cols))


EXCHANGE_PEERS = {"gather": (1, 2, 3, 4, 5, 6, 7), "scatter": (1, 2, 3, 4, 5, 6, 7), "chips": (1, 2, 4, 6), "pass": (2, 4, 6)}


def _exchange_copies(mode, bufs, n, send_sems, recv_sems):
    mx, my, mc = lax.axis_index("x"), lax.axis_index("y"), lax.axis_index("c")
    me = 4 * mx + 2 * my + mc
    out = []
    for slot, k in enumerate(EXCHANGE_PEERS[mode]):
        peer = (_flip(mx, k & 4), _flip(my, k & 2), _flip(mc, k & 1))
        pidx = 4 * peer[0] + 2 * peer[1] + peer[2]
        for a in range(n):
            r = bufs[a].shape[-2] // NDEV
            if mode == "scatter":
                src, dst, arrive = _rows(bufs[a], pidx, r), bufs[n + a].at[me], bufs[n + a].at[pidx]
            elif mode == "pass":
                peer = (mx, my, 1 - mc)
                src, dst, arrive = _rows(bufs[a], pidx, r), _rows(bufs[a], pidx, r), _rows(bufs[a], pidx + 1 - 2 * mc, r)
            else:
                src, dst, arrive = _rows(bufs[a], me, r), _rows(bufs[a], me, r), _rows(bufs[a], pidx, r)
            sems = dict(send_sem=send_sems.at[slot * n + a], recv_sem=recv_sems.at[slot * n + a],
                        device_id=peer, device_id_type=MESH)
            out.append((pltpu.make_async_remote_copy(src_ref=src, dst_ref=dst, **sems),
                        pltpu.make_async_remote_copy(src_ref=src, dst_ref=arrive, **sems)))
    return out


def _exchange_start(mode, arrays, name):
    m = len(arrays)
    n = m // 2 if mode == "scatter" else m

    def body(*refs):
        send_sems, recv_sems, token = refs[m], refs[m + 1], refs[-1]
        for go, _ in _exchange_copies(mode, refs[:m], n, send_sems, recv_sems):
            go.start()
        token[...] = jnp.zeros_like(token)

    sems = pltpu.SemaphoreType.DMA((len(EXCHANGE_PEERS[mode]) * n,))
    return pl.pallas_call(
        body, name=name, in_specs=[HBM_SPEC] * m,
        out_shape=(sems, sems, *[pltpu.HBM(a.shape, a.dtype) for a in arrays], SDS((8, CH), f32)),
        out_specs=(SEM_SPEC, SEM_SPEC, *[HBM_SPEC] * m, pl.BlockSpec(memory_space=pltpu.VMEM)),
        input_output_aliases={a: 2 + a for a in range(m)},
        compiler_params=pltpu.CompilerParams(has_side_effects=DATAFLOW_EFFECT),
    )(*[pltpu.with_memory_space_constraint(a, pltpu.HBM) for a in arrays])


def _exchange_wait(mode, started, after, name):
    send_sems, recv_sems, *arrays = started
    m = len(arrays)
    n = m // 2 if mode == "scatter" else m

    def body(*refs):
        for go, arrive in _exchange_copies(mode, refs[:m], n, refs[m], refs[m + 1]):
            go.wait_send()
            arrive.wait_recv()

    return pl.pallas_call(
        body, name=name, in_specs=[HBM_SPEC] * m + [SEM_SPEC, SEM_SPEC, pl.BlockSpec(memory_space=pl.ANY)],
        out_shape=[pltpu.HBM(a.shape, a.dtype) for a in arrays], out_specs=[HBM_SPEC] * m,
        input_output_aliases={a: a for a in range(m)},
        compiler_params=pltpu.CompilerParams(has_side_effects=DATAFLOW_EFFECT),
    )(*arrays, send_sems, recv_sems, after)


def _mod_fwd(c_all, ada_w, ada_b_loc):
    def body(c_ref, w_ref, b_ref, o_ref):
        ca = c_ref[...]
        ca = ca * jax.nn.sigmoid(ca)
        o_ref[...] = jnp.dot(ca, w_ref[...], precision=lax.Precision.HIGHEST,
                             preferred_element_type=f32) + b_ref[...]

    return _pallas_hbm(
        body, name="mod_fwd", grid=(DEPTH,), out_shape=SDS((DEPTH, NDEV, ADAS), f32),
        in_specs=[pl.BlockSpec((NDEV, D), lambda l: (0, 0)),
                  pl.BlockSpec((None, D, ADAS), lambda l: (l, 0, 0)),
                  pl.BlockSpec((None, 1, ADAS), lambda l: (l, 0, 0))],
        out_specs=pl.BlockSpec((None, NDEV, ADAS), lambda l: (l, 0, 0)),
        compiler_params=_cp(("arbitrary",), 32),
    )(c_all, ada_w, ada_b_loc)


def _ada_grad(cact_t, dmod_cols):
    def body(c_ref, d_ref, o_ref):
        acc = c_ref[:, 0:1] * d_ref[0:1, :]
        for b in range(1, NDEV):
            acc = acc + c_ref[:, b:b + 1] * d_ref[b:b + 1, :]
        o_ref[...] = acc

    tr = 256
    return _pallas_hbm(
        body, name="ada_grad", grid=(DEPTH, D // tr), out_shape=SDS((DEPTH, D, ADAS), f32),
        in_specs=[pl.BlockSpec((tr, NDEV), lambda l, i: (i, 0)),
                  pl.BlockSpec((None, NDEV, ADAS), lambda l, i: (l, 0, 0))],
        out_specs=pl.BlockSpec((None, tr, ADAS), lambda l, i: (l, i, 0)),
        compiler_params=_cp(("arbitrary", "arbitrary"), 32),
    )(cact_t, dmod_cols)


def _ffn_fwd(h, par, w3):
    tm, tf = 1024, 256
    nj = F // tf

    def body(h_ref, par_ref, wg_ref, wu_ref, wd_ref, ho_ref, fo_ref, g_ref, u_ref, a_ref, y_scr, acc):
        j = pl.program_id(1)

        @pl.when(j == 0)
        def _():
            y_scr[...] = _norm_mod(h_ref[...], par_ref[...])[0].astype(bf16)
            acc[...] = jnp.zeros_like(acc)

        y = y_scr[...]
        g = _nt(y, wg_ref[...])
        u = _nt(y, wu_ref[...])
        a = ((g * jax.nn.sigmoid(g)) * u).astype(bf16)
        g_ref[...] = g.astype(bf16)
        u_ref[...] = u.astype(bf16)
        a_ref[...] = a
        acc[...] += _nn(a, wd_ref[...])

        @pl.when(j == nj - 1)
        def _():
            fo_ref[...] = acc[...].astype(bf16)
            ho_ref[...] = h_ref[...] + (0.5 * par_ref[2:3, :]) * acc[...]

    row = pl.BlockSpec((tm, D), lambda i, j: (i, 0))
    hid = pl.BlockSpec((tm, tf), lambda i, j: (i, j))
    wspec = [pl.BlockSpec((None, tf, D), lambda i, j, k=k: (k, j, 0)) for k in range(3)]
    return _pallas_hbm(
        body, name="ffn_fwd", grid=(T // tm, nj),
        in_specs=[row, pl.BlockSpec((8, D), lambda i, j: (0, 0))] + wspec,
        out_specs=[row, row, hid, hid, hid],
        out_shape=[SDS((T, D), f32), SDS((T, D), bf16), SDS((T, F), bf16), SDS((T, F), bf16), SDS((T, F), bf16)],
        scratch_shapes=[pltpu.VMEM((tm, D), bf16), pltpu.VMEM((tm, D), f32)],
        compiler_params=_cp(("arbitrary", "arbitrary"), 52),
    )(h, par, w3, w3, w3)


def _ffn_bwd_tok(dh, h, par, fo, gs, us, w3):
    tm, tf = 1024, 256
    nj = F // tf

    def body(dh_ref, h_ref, par_ref, fo_ref, g_ref, u_ref, wg_ref, wu_ref, wd_ref, wdn_ref,
             dhin_ref, dg_ref, du_ref, y_ref, dfb_ref, dpar_ref, df_scr, dyacc, da_scr):
        i, j = pl.program_id(0), pl.program_id(1)

        @pl.when(jnp.logical_and(i == 0, j == 0))
        def _():
            dpar_ref[...] = jnp.zeros_like(dpar_ref)

        @pl.when(j == 0)
        def _():
            dh_v, par_v = dh_ref[...], par_ref[...]
            dfb = ((0.5 * par_v[2:3, :]) * dh_v).astype(bf16)
            df_scr[...] = dfb
            dfb_ref[...] = dfb
            dpar_ref[2:3, :] += 0.5 * _colsum(dh_v * fo_ref[...].astype(f32))
            y_ref[...] = _norm_mod(h_ref[...], par_v)[0].astype(bf16)
            dyacc[...] = jnp.zeros_like(dyacc)
            da_scr[...] = _nt(dfb, wd_ref[...])

        for half in range(2):
            rs = slice(half * (tm // 2), (half + 1) * (tm // 2))
            da = da_scr[rs, :]
            da_scr[rs, :] = _nt(df_scr[rs, :], wdn_ref[...])
            g = g_ref[rs, :].astype(f32)
            u = u_ref[rs, :].astype(f32)
            sig = jax.nn.sigmoid(g)
            du = (da * (g * sig)).astype(bf16)
            dg = (da * u * (sig * (1.0 + g * (1.0 - sig)))).astype(bf16)
            dg_ref[rs, :] = dg
            du_ref[rs, :] = du
            dyacc[rs, :] += _nn(dg, wg_ref[...]) + _nn(du, wu_ref[...])

        @pl.when(j == nj - 1)
        def _():
            dx, dsh, dsc, dng = _norm_mod_bwd(dyacc[...], h_ref[...], par_ref[...])
            dhin_ref[...] = dh_ref[...] + dx
            dpar_ref[0:1, :] += dsh
            dpar_ref[1:2, :] += dsc
            dpar_ref[3:4, :] += dng

    row = pl.BlockSpec((tm, D), lambda i, j: (i, 0))
    hid = pl.BlockSpec((tm, tf), lambda i, j: (i, j))
    one = pl.BlockSpec((8, D), lambda i, j: (0, 0))
    wspec = [pl.BlockSpec((None, tf, D), lambda i, j, k=k: (k, j, 0)) for k in range(3)]
    wspec.append(pl.BlockSpec((None, tf, D), lambda i, j: (2, jnp.minimum(j + 1, nj - 1), 0)))
    return _pallas_hbm(
        body, name="ffn_bwd_tok", grid=(T // tm, nj),
        in_specs=[row, row, one, row, hid, hid] + wspec,
        out_specs=[row, hid, hid, row, row, one],
        out_shape=[SDS((T, D), f32), SDS((T, F), bf16), SDS((T, F), bf16), SDS((T, D), bf16),
                   SDS((T, D), bf16), SDS((8, D), f32)],
        scratch_shapes=[pltpu.VMEM((tm, D), bf16), pltpu.VMEM((tm, D), f32), pltpu.VMEM((tm, tf), f32)],
        compiler_params=_cp(("arbitrary", "arbitrary"), 60),
    )(dh, h, par, fo, gs, us, w3, w3, w3, w3)


def _ffn_bwd_w(a_s, dg_s, du_s, y, dfb, anchor=None):
    tf, tk = F // 2, 512
    nk = T // tk

    def body(a_ref, dg_ref, du_ref, y_ref, df_ref, *rest):
        o_ref, accg, accu, accd = rest[-4:]
        kk = pl.program_id(1)

        @pl.when(kk == 0)
        def _():
            accg[...] = jnp.zeros_like(accg)
            accu[...] = jnp.zeros_like(accu)
            accd[...] = jnp.zeros_like(accd)

        yv = y_ref[...]
        accg[...] += _tn(dg_ref[...], yv)
        accu[...] += _tn(du_ref[...], yv)
        accd[...] += _tn(a_ref[...], df_ref[...])

        @pl.when(kk == nk - 1)
        def _():
            o_ref[0] = accg[...].astype(bf16)
            o_ref[1] = accu[...].astype(bf16)
            o_ref[2] = accd[...].astype(bf16)

    hid = pl.BlockSpec((tk, tf), lambda j, kk: (kk, j))
    row = pl.BlockSpec((tk, D), lambda j, kk: (kk, 0))
    extra = [] if anchor is None else [anchor]
    return _pallas_hbm(
        body, name="ffn_bwd_w", grid=(F // tf, nk),
        in_specs=[hid, hid, hid, row, row] + [pl.BlockSpec((8, CH), lambda j, kk: (0, 0))] * len(extra),
        out_specs=pl.BlockSpec((3, tf, D), lambda j, kk: (0, j, 0)), out_shape=SDS((3, F, D), bf16),
        scratch_shapes=[pltpu.VMEM((tf, D), f32)] * 3,
        compiler_params=_cp(("arbitrary", "arbitrary"), 58),
    )(a_s, dg_s, du_s, y, dfb, *extra)


def _tn_matmul(a, b, bm, name):
    M, N = a.shape[1], b.shape[1]
    tk = 1024
    nk = T // tk

    def body(a_ref, b_ref, o_ref, acc):
        kk = pl.program_id(1)

        @pl.when(kk == 0)
        def _():
            acc[...] = jnp.zeros_like(acc)

        acc[...] += _tn(a_ref[...], b_ref[...])

        @pl.when(kk == nk - 1)
        def _():
            o_ref[...] = acc[...].astype(bf16)

    return _pallas_hbm(
        body, name=name, grid=(M // bm, nk),
        in_specs=[pl.BlockSpec((tk, bm), lambda i, kk: (kk, i)), pl.BlockSpec((tk, N), lambda i, kk: (kk, 0))],
        out_specs=pl.BlockSpec((bm, N), lambda i, kk: (i, 0)), out_shape=SDS((M, N), bf16),
        scratch_shapes=[pltpu.VMEM((bm, N), f32)],
        compiler_params=_cp(("arbitrary", "arbitrary"), 40),
    )(a, b)


def _rope_tables():
    inv = ROPE_THETA ** (-jnp.arange(0, 64, 2, dtype=f32) / 64)
    ang = jnp.arange(T, dtype=f32)[:, None] * inv[None, :]
    ang = jnp.concatenate([ang, ang], axis=-1)
    cos, sin = jnp.tile(jnp.cos(ang), (1, 2)), jnp.tile(jnp.sin(ang), (1, 2))
    low = (jnp.arange(CH) % 64 < 32)[None, :]
    return cos, jnp.where(low, -sin, 0.0), jnp.where(low, 0.0, sin)


def _all_heads(tab_ref):
    return jnp.concatenate([tab_ref[...]] * 4, axis=1)


def _rope(t, cos, sin_lo, sin_hi):
    return t * cos + pltpu.roll(t, HW - 32, 1) * sin_lo + pltpu.roll(t, 32, 1) * sin_hi


def _rope_t(g, cos, sin_lo, sin_hi):
    return g * cos + pltpu.roll(g * sin_lo, 32, 1) + pltpu.roll(g * sin_hi, HW - 32, 1)


def _inproj_fwd(h, par, w_in_t, tabs):
    tm = 512

    def body(h_ref, par_ref, w_ref, cos_ref, slo_ref, shi_ref, ua_ref, va_ref, q_ref, k_ref, v_ref):
        y = _norm_mod(h_ref[...], par_ref[...])[0].astype(bf16)
        proj = lambda c: _nt(y, w_ref[c * HW:(c + 1) * HW, :])
        ua_ref[...] = proj(0)
        va_ref[...] = proj(1)
        cos, slo, shi = _all_heads(cos_ref), _all_heads(slo_ref), _all_heads(shi_ref)
        q_ref[...] = _rope(proj(2), cos, slo, shi) * 0.125
        k_ref[...] = _rope(proj(3), cos, slo, shi)
        v_ref[...] = proj(4)

    row = pl.BlockSpec((tm, D), lambda i: (i, 0))
    half = pl.BlockSpec((tm, HW), lambda i: (i, 0))
    return _pallas_hbm(
        body, name="inproj_fwd", grid=(T // tm,),
        in_specs=[row, pl.BlockSpec((8, D), lambda i: (0, 0)), pl.BlockSpec((INW, D), lambda i: (0, 0))] +
                 [pl.BlockSpec((tm, CH), lambda i: (i, 0))] * 3,
        out_specs=[half] * 5,
        out_shape=[SDS((T, HW), f32)] * 5,
        compiler_params=_cp(("arbitrary",), 48),
    )(h, par, w_in_t, *tabs)


def _head_masks():
    lane = lax.broadcasted_iota(jnp.int32, (1, CH), 1)
    return lane < 64, lane >= 64


def _lane(t, idx):
    return jnp.sum(jnp.where(lax.broadcasted_iota(jnp.int32, t.shape, 1) == idx, t, 0.0), axis=-1, keepdims=True)


def _stack_heads(x):
    lo, hi = _head_masks()
    zero = jnp.zeros_like(x)
    return jnp.concatenate([jnp.where(lo, x, zero), jnp.where(hi, x, zero)], axis=0)


def _band_mask(has_prev):
    row = lax.broadcasted_iota(jnp.int32, (2 * CH, 2 * CH), 0) & (CH - 1)
    col = lax.broadcasted_iota(jnp.int32, (2 * CH, 2 * CH), 1)
    in_prev = jnp.logical_and(jnp.logical_and(col < CH, col >= row), has_prev)
    return jnp.logical_or(in_prev, jnp.logical_and(col >= CH, col - CH <= row))


def _attn_tiling(d):
    return CH * 16, 16 // d, CH


def _attn_fwd(q, k, v, d):
    rows, blocks, lanes = _attn_tiling(d)
    pairs = lanes // CH

    def body(q_ref, kc_ref, kp_ref, vc_ref, vp_ref, o_ref, lse_ref):
        c, lb = pl.program_id(0), pl.program_id(1)
        col = lax.broadcasted_iota(jnp.int32, (CH, CH), 1)
        lo, _ = _head_masks()

        @pl.when(lb == 0)
        def _():
            lse_ref[...] = jnp.zeros_like(lse_ref)

        at = lambda r, b: pl.ds(r + b * CH * d, CH, stride=d) if d > 1 else pl.ds(b * CH, CH)
        for r in range(d):
            for b in range(blocks):
                own = at(r, b)
                k_prev, v_prev, before, mask = ((kc_ref, vc_ref, at(r, b - 1), _band_mask(True)) if b > 0 else
                                                (kp_ref, vp_ref, at(r, blocks - 1), _band_mask(c > 0)))
                lse_tile = lse_ref[own, :]
                for pi in range(pairs):
                    sl = slice(pi * CH, (pi + 1) * CH)
                    hp = lb * pairs + pi
                    kk = jnp.concatenate([k_prev[before, sl], kc_ref[own, sl]], axis=0).astype(bf16)
                    vv = jnp.concatenate([v_prev[before, sl], vc_ref[own, sl]], axis=0).astype(bf16)
                    s = jnp.where(mask, _nt(_stack_heads(q_ref[own, sl].astype(bf16)), kk), -jnp.inf)
                    m = jnp.max(s, axis=-1, keepdims=True)
                    p = jnp.exp(s - m)
                    den = jnp.sum(p, axis=-1, keepdims=True)
                    o = _nn(p.astype(bf16), vv) / den
                    o_ref[own, sl] = jnp.where(lo, o[:CH], o[CH:])
                    lse = m + jnp.log(den)
                    lse_tile = jnp.where(col == 2 * hp, lse[:CH], jnp.where(col == 2 * hp + 1, lse[CH:], lse_tile))
                lse_ref[own, :] = lse_tile

    cur = pl.BlockSpec((rows, lanes), lambda c, lb: (c, lb))
    prev = pl.BlockSpec((rows, lanes), lambda c, lb: (jnp.maximum(c - 1, 0), lb))
    return _pallas_hbm(
        body, name="attn_fwd_d%d" % d, grid=(T // rows, HW // lanes), in_specs=[cur, cur, prev, cur, prev],
        out_specs=[cur, pl.BlockSpec((rows, CH), lambda c, lb: (c, 0))],
        out_shape=[SDS((T, HW), f32), SDS((T, CH), f32)],
        compiler_params=_cp(("arbitrary", "arbitrary"), 40),
    )(q, k, k, v, v)


def _attn_combine(os_, lses):
    tm = 512

    def body(o0_ref, o1_ref, o2_ref, l0_ref, l1_ref, l2_ref, o_ref, lse_ref):
        l0, l1, l2 = l0_ref[...], l1_ref[...], l2_ref[...]
        m = jnp.maximum(jnp.maximum(l0, l1), l2)
        e = [jnp.exp(l0 - m), jnp.exp(l1 - m), jnp.exp(l2 - m)]
        s = e[0] + e[1] + e[2]
        w = [ei / s for ei in e]
        lse_ref[...] = m + jnp.log(s)
        lo, _ = _head_masks()
        for hp in range(4):
            sl = slice(hp * CH, (hp + 1) * CH)
            acc = jnp.zeros((tm, CH), f32)
            for wp, op_ref in zip(w, (o0_ref, o1_ref, o2_ref)):
                wexp = jnp.where(lo, wp[:, 2 * hp:2 * hp + 1], wp[:, 2 * hp + 1:2 * hp + 2])
                acc = acc + wexp * op_ref[:, sl].astype(f32)
            o_ref[:, sl] = acc

    half = pl.BlockSpec((tm, HW), lambda i: (i, 0))
    stat = pl.BlockSpec((tm, CH), lambda i: (i, 0))
    return _pallas_hbm(
        body, name="attn_combine", grid=(T // tm,), in_specs=[half] * 3 + [stat] * 3,
        out_specs=[half, stat], out_shape=[SDS((T, HW), f32), SDS((T, CH), f32)],
        compiler_params=_cp(("arbitrary",), 32),
    )(*os_, *lses)


def _attn_bwd(q, k, v, o, do, lse, d, running=None):
    rows, blocks, lanes = _attn_tiling(d)
    pairs = lanes // CH
    steps = T // rows

    def body(qc_ref, qn_ref, kc_ref, kp_ref, vc_ref, vp_ref, oc_ref, on_ref, dc_ref, dn_ref, lc_ref, ln_ref, *rest):
        dq_ref, dk_ref, dv_ref = rest[-3:]
        sofar = (lambda i, rows, sl: rest[i][rows, sl]) if running is not None else (lambda i, rows, sl: 0.0)
        c, lb = pl.program_id(0), pl.program_id(1)
        lo, _ = _head_masks()
        at = lambda r, b: pl.ds(r + b * CH * d, CH, stride=d) if d > 1 else pl.ds(b * CH, CH)
        for r in range(d):
            for b in range(blocks):
                own = at(r, b)
                k_prev, v_prev, before, mask_c = ((kc_ref, vc_ref, at(r, b - 1), _band_mask(True)) if b > 0 else
                                                  (kp_ref, vp_ref, at(r, blocks - 1), _band_mask(c > 0)))
                q_next, o_next, d_next, l_next, after, has_next = (
                    (qc_ref, oc_ref, dc_ref, lc_ref, at(r, b + 1), True) if b + 1 < blocks else
                    (qn_ref, on_ref, dn_ref, ln_ref, at(r, 0), c < steps - 1))
                mask_n = _band_mask(has_next)[:, :CH]
                lse_c, lse_n = lc_ref[own, :], l_next[after, :]
                for pi in range(pairs):
                    sl = slice(pi * CH, (pi + 1) * CH)
                    hp = lb * pairs + pi
                    kc, vc = kc_ref[own, sl].astype(bf16), vc_ref[own, sl].astype(bf16)
                    kk = jnp.concatenate([k_prev[before, sl].astype(bf16), kc], axis=0)
                    vv = jnp.concatenate([v_prev[before, sl].astype(bf16), vc], axis=0)
                    doc, don = dc_ref[own, sl], d_next[after, sl]
                    qs_c, qs_n = _stack_heads(qc_ref[own, sl].astype(bf16)), _stack_heads(q_next[after, sl].astype(bf16))
                    ds_c, ds_n = _stack_heads(doc.astype(bf16)), _stack_heads(don.astype(bf16))
                    delta_c = jnp.sum(_stack_heads(doc * oc_ref[own, sl]), axis=-1, keepdims=True)
                    delta_n = jnp.sum(_stack_heads(don * o_next[after, sl]), axis=-1, keepdims=True)
                    heads = lambda t: jnp.concatenate([_lane(t, 2 * hp), _lane(t, 2 * hp + 1)], axis=0)
                    p1 = jnp.where(mask_c, jnp.exp(_nt(qs_c, kk) - heads(lse_c)), 0.0)
                    g1 = (p1 * (_nt(ds_c, vv) - delta_c)).astype(bf16)
                    dq = _nn(g1, kk)
                    dq_ref[own, sl] = sofar(0, own, sl) + jnp.where(lo, dq[:CH], dq[CH:])
                    p2 = jnp.where(mask_n, jnp.exp(_nt(qs_n, kc) - heads(lse_n)), 0.0)
                    g2 = (p2 * (_nt(ds_n, vc) - delta_n)).astype(bf16)
                    dk_ref[own, sl] = sofar(1, own, sl) + _tn(jnp.concatenate([g1[:, CH:], g2], axis=0),
                                                              jnp.concatenate([qs_c, qs_n], axis=0))
                    dv_ref[own, sl] = sofar(2, own, sl) + _tn(
                        jnp.concatenate([p1[:, CH:].astype(bf16), p2.astype(bf16)], axis=0),
                        jnp.concatenate([ds_c, ds_n], axis=0))

    cur = pl.BlockSpec((rows, lanes), lambda c, lb: (c, lb))
    prev = pl.BlockSpec((rows, lanes), lambda c, lb: (jnp.maximum(c - 1, 0), lb))
    nxt = pl.BlockSpec((rows, lanes), lambda c, lb: (jnp.minimum(c + 1, steps - 1), lb))
    scur = pl.BlockSpec((rows, CH), lambda c, lb: (c, 0))
    snxt = pl.BlockSpec((rows, CH), lambda c, lb: (jnp.minimum(c + 1, steps - 1), 0))
    more = [] if running is None else list(running)
    return _pallas_hbm(
        body, name="attn_bwd_d%d" % d, grid=(steps, HW // lanes),
        in_specs=[cur, nxt, cur, prev, cur, prev, cur, nxt, cur, nxt, scur, snxt] + [cur] * len(more),
        out_specs=[cur] * 3, out_shape=[SDS((T, HW), f32)] * 3,
        input_output_aliases={12 + i: i for i in range(len(more))},
        compiler_params=_cp(("arbitrary", "arbitrary"), 52),
    )(q, q, k, k, v, v, o, o, do, do, lse, lse, *more)


def _causal(w):
    row = lax.broadcasted_iota(jnp.int32, (CH, CH), 0)
    col = lax.broadcasted_iota(jnp.int32, (CH, CH), 1)
    return jnp.where(col <= row, w, 0.0)


def _sgu_fwd(ua, va, o, lng, lnb, ws, bs_t):
    tm = 512

    def body(ua_ref, va_ref, o_ref, lg_ref, lb_ref, ws_ref, bs_ref, mix_ref):
        for hd in range(4):
            sl = slice(hd * CH, (hd + 1) * CH)
            w = _causal(ws_ref[hd]).astype(bf16)
            for cc in range(tm // CH):
                rs = slice(cc * CH, (cc + 1) * CH)
                u = _gelu(ua_ref[rs, sl])
                v = _gelu(va_ref[rs, sl])
                vc = v - _rowmean(v)
                vn = vc * lax.rsqrt(_rowmean(vc * vc) + EPS) * lg_ref[:, sl] + lb_ref[:, sl]
                z = _nn(w, vn.astype(bf16)) + bs_ref[:, hd:hd + 1]
                mix_ref[rs, sl] = (u * z).astype(bf16)
        mix_ref[:, HW:] = o_ref[...].astype(bf16)

    half = pl.BlockSpec((tm, HW), lambda i: (i, 0))
    vec = pl.BlockSpec((1, HW), lambda i: (0, 0))
    return _pallas_hbm(
        body, name="sgu_fwd", grid=(T // tm,),
        in_specs=[half, half, half, vec, vec, pl.BlockSpec((4, CH, CH), lambda i: (0, 0, 0)),
                  pl.BlockSpec((CH, 4), lambda i: (0, 0))],
        out_specs=pl.BlockSpec((tm, 2 * HW), lambda i: (i, 0)), out_shape=SDS((T, 2 * HW), bf16),
        compiler_params=_cp(("arbitrary",), 32),
    )(ua, va, o, lng, lnb, ws, bs_t)


def _sgu_bwd(ua, va, d_a, lng, lnb, ws, bs_t):
    tm = 512

    def body(ua_ref, va_ref, d_ref, lg_ref, lb_ref, ws_ref, bs_ref, dsg_ref, dln_ref, dws_ref, db_ref):
        @pl.when(pl.program_id(0) == 0)
        def _():
            dln_ref[...] = jnp.zeros_like(dln_ref)
            dws_ref[...] = jnp.zeros_like(dws_ref)
            db_ref[...] = jnp.zeros_like(db_ref)

        for hd in range(4):
            sl = slice(hd * CH, (hd + 1) * CH)
            w = _causal(ws_ref[hd])
            w, wt = w.astype(bf16), w.T.astype(bf16)
            lg = lg_ref[:, sl]
            for cc in range(tm // CH):
                rs = slice(cc * CH, (cc + 1) * CH)
                xa, xv, dd = ua_ref[rs, sl], va_ref[rs, sl], d_ref[rs, sl]
                u, v = _gelu(xa), _gelu(xv)
                vc = v - _rowmean(v)
                rstd = lax.rsqrt(_rowmean(vc * vc) + EPS)
                xh = vc * rstd
                vnb = (xh * lg + lb_ref[:, sl]).astype(bf16)
                z = _nn(w, vnb) + bs_ref[:, hd:hd + 1]
                dz = dd * u
                dzb = dz.astype(bf16)
                dsg_ref[rs, sl] = (dd * z * _gelu_grad(xa)).astype(bf16)
                dws_ref[hd] += _nt(dzb, vnb)
                db_ref[hd] += dz
                dvn = _nn(wt, dzb)
                dln_ref[0:1, sl] += _colsum(dvn * xh)
                dln_ref[1:2, sl] += _colsum(dvn)
                dxh = dvn * lg
                dv = rstd * (dxh - _rowmean(dxh) - xh * _rowmean(dxh * xh))
                dsg_ref[rs, HW + hd * CH:HW + (hd + 1) * CH] = (dv * _gelu_grad(xv)).astype(bf16)

    half = pl.BlockSpec((tm, HW), lambda i: (i, 0))
    vec = pl.BlockSpec((1, HW), lambda i: (0, 0))
    mat = pl.BlockSpec((4, CH, CH), lambda i: (0, 0, 0))
    return _pallas_hbm(
        body, name="sgu_bwd", grid=(T // tm,),
        in_specs=[half, half, half, vec, vec, mat, pl.BlockSpec((CH, 4), lambda i: (0, 0))],
        out_specs=[pl.BlockSpec((tm, 2 * HW), lambda i: (i, 0)), pl.BlockSpec((8, HW), lambda i: (0, 0)), mat, mat],
        out_shape=[SDS((T, 2 * HW), bf16), SDS((8, HW), f32), SDS((4, CH, CH), f32), SDS((4, CH, CH), f32)],
        compiler_params=_cp(("arbitrary",), 32),
    )(ua, va, d_a, lng, lnb, ws, bs_t)


def _outproj_fwd(mixed, w_out, h, par):
    tm = 1024

    def body(mix_ref, w_ref, h_ref, par_ref, ho_ref, po_ref):
        p = _nn(mix_ref[...], w_ref[...])
        po_ref[...] = p.astype(bf16)
        ho_ref[...] = h_ref[...] + par_ref[2:3, :] * p

    row = pl.BlockSpec((tm, D), lambda i: (i, 0))
    return _pallas_hbm(
        body, name="outproj_fwd", grid=(T // tm,),
        in_specs=[row, pl.BlockSpec((D, D), lambda i: (0, 0)), row, pl.BlockSpec((8, D), lambda i: (0, 0))],
        out_specs=[row, row], out_shape=[SDS((T, D), f32), SDS((T, D), bf16)],
        compiler_params=_cp(("arbitrary",), 48),
    )(mixed, w_out, h, par)


def _outproj_bwd(dh, po, w_out, par):
    tm = 1024

    def body(dh_ref, po_ref, w_ref, par_ref, do_ref, da_ref, db_ref, dpar_ref):
        @pl.when(pl.program_id(0) == 0)
        def _():
            dpar_ref[...] = jnp.zeros_like(dpar_ref)

        dh_v = dh_ref[...]
        dob = (par_ref[2:3, :] * dh_v).astype(bf16)
        do_ref[...] = dob
        dpar_ref[2:3, :] += _colsum(dh_v * po_ref[...].astype(f32))
        dm = _nt(dob, w_ref[...])
        da_ref[...] = dm[:, :HW]
        db_ref[...] = dm[:, HW:]

    row = pl.BlockSpec((tm, D), lambda i: (i, 0))
    half = pl.BlockSpec((tm, HW), lambda i: (i, 0))
    one = pl.BlockSpec((8, D), lambda i: (0, 0))
    return _pallas_hbm(
        body, name="outproj_bwd", grid=(T // tm,),
        in_specs=[row, row, pl.BlockSpec((D, D), lambda i: (0, 0)), one],
        out_specs=[row, half, half, one],
        out_shape=[SDS((T, D), bf16), SDS((T, HW), f32), SDS((T, HW), f32), SDS((8, D), f32)],
        compiler_params=_cp(("arbitrary",), 48),
    )(dh, po, w_out, par)


def _inproj_bwd_tok(dh, h, par, w_in_t, dsg, dq, dk, dv, tabs):
    tm = 512

    def body(dh_ref, h_ref, par_ref, w_ref, dsg_ref, dq_ref, dk_ref, dv_ref,
             cos_ref, slo_ref, shi_ref, dhin_ref, dp_ref, y_ref, dpar_ref):
        @pl.when(pl.program_id(0) == 0)
        def _():
            dpar_ref[...] = jnp.zeros_like(dpar_ref)

        cos, slo, shi = _all_heads(cos_ref), _all_heads(slo_ref), _all_heads(shi_ref)
        dp_ref[:, :2 * HW] = dsg_ref[...]
        dp_ref[:, 2 * HW:3 * HW] = _rope_t(dq_ref[...] * 0.125, cos, slo, shi).astype(bf16)
        dp_ref[:, 3 * HW:4 * HW] = _rope_t(dk_ref[...], cos, slo, shi).astype(bf16)
        dp_ref[:, 4 * HW:] = dv_ref[...].astype(bf16)
        dy = _nn(dp_ref[...], w_ref[...])
        par_v, h_v = par_ref[...], h_ref[...]
        y_ref[...] = _norm_mod(h_v, par_v)[0].astype(bf16)
        dx, dsh, dsc, dng = _norm_mod_bwd(dy, h_v, par_v)
        dhin_ref[...] = dh_ref[...] + dx
        dpar_ref[0:1, :] += dsh
        dpar_ref[1:2, :] += dsc
        dpar_ref[3:4, :] += dng

    row = pl.BlockSpec((tm, D), lambda i: (i, 0))
    half = pl.BlockSpec((tm, HW), lambda i: (i, 0))
    one = pl.BlockSpec((8, D), lambda i: (0, 0))
    return _pallas_hbm(
        body, name="inproj_bwd_tok", grid=(T // tm,),
        in_specs=[row, row, one, pl.BlockSpec((INW, D), lambda i: (0, 0)), row] + [half] * 3 +
                 [pl.BlockSpec((tm, CH), lambda i: (i, 0))] * 3,
        out_specs=[row, pl.BlockSpec((tm, INW), lambda i: (i, 0)), row, one],
        out_shape=[SDS((T, D), f32), SDS((T, INW), bf16), SDS((T, D), bf16), SDS((8, D), f32)],
        compiler_params=_cp(("arbitrary",), 56),
    )(dh, h, par, w_in_t, dsg, dq, dk, dv, *tabs)


def _loss_head(h, par, target):
    tm = 512

    def body(h_ref, par_ref, t_ref, dh_ref, acc_ref):
        @pl.when(pl.program_id(0) == 0)
        def _():
            acc_ref[...] = jnp.zeros_like(acc_ref)

        x, g = h_ref[...], par_ref[3:4, :]
        rstd = lax.rsqrt(_rowmean(x * x) + EPS)
        xhat = x * rstd
        err = xhat * g - t_ref[...]
        acc_ref[0:1, :] += _colsum(err * err)
        dy = err * (1.0 / D)
        acc_ref[3:4, :] += _colsum(dy * xhat)
        dxhat = dy * g
        dh_ref[...] = rstd * (dxhat - xhat * _rowmean(dxhat * xhat))

    row = pl.BlockSpec((tm, D), lambda i: (i, 0))
    one = pl.BlockSpec((8, D), lambda i: (0, 0))
    return _pallas_hbm(
        body, name="loss_head", grid=(T // tm,), in_specs=[row, one, row], out_specs=[row, one],
        out_shape=[SDS((T, D), f32), SDS((8, D), f32)], compiler_params=_cp(("arbitrary",), 32),
    )(h, par, target)


ELEMENTWISE_VMEM_BUDGET = 20 << 20


def _block_rows(rows, bytes_per_row):
    cap = ELEMENTWISE_VMEM_BUDGET // bytes_per_row
    if rows <= cap:
        return rows
    return next(b for b in range(cap - cap % 16, 0, -16) if rows % b == 0)


def _sum_slots(land, name):
    _, R, C = land.shape
    br = _block_rows(R, 2 * NDEV * C * land.dtype.itemsize + 2 * C * 4)

    def body(l_ref, o_ref):
        acc = l_ref[0].astype(f32)
        for s in range(1, NDEV):
            acc = acc + l_ref[s].astype(f32)
        o_ref[...] = acc

    return _pallas_hbm(
        body, name=name, grid=(R // br,), in_specs=[pl.BlockSpec((NDEV, br, C), lambda i: (0, i, 0))],
        out_specs=pl.BlockSpec((br, C), lambda i: (i, 0)), out_shape=SDS((R, C), f32),
        compiler_params=_cp(("arbitrary",), 32),
    )(land)


def _sum_partials(land, own, me, name):
    r = land.shape[-2]

    def body(me_ref, l_ref, own_ref, o_ref):
        mine = own_ref[...].astype(f32)
        acc = jnp.where(me_ref[0] == 0, mine, l_ref[0].astype(f32))
        for s in range(1, NDEV):
            acc = acc + jnp.where(me_ref[0] == s, mine, l_ref[s].astype(f32))
        o_ref[...] = acc

    if own.ndim == 3:
        kk = own.shape[0]
        specs = dict(grid=(kk,),
                     in_specs=[pl.BlockSpec((NDEV, None, r, D), lambda i, me_ref: (0, i, 0, 0)),
                               pl.BlockSpec((None, r, D), lambda i, me_ref: (i, me_ref[0], 0))],
                     out_specs=pl.BlockSpec((None, r, D), lambda i, me_ref: (i, 0, 0)))
        out_shape = SDS((kk, r, D), f32)
    else:
        specs = dict(grid=(1,),
                     in_specs=[pl.BlockSpec((NDEV, r, D), lambda i, me_ref: (0, 0, 0)),
                               pl.BlockSpec((r, D), lambda i, me_ref: (me_ref[0], 0))],
                     out_specs=pl.BlockSpec((r, D), lambda i, me_ref: (0, 0)))
        out_shape = SDS((r, D), f32)
    return _pallas_hbm(
        body, name=name, out_shape=out_shape,
        grid_spec=pltpu.PrefetchScalarGridSpec(num_scalar_prefetch=1, **specs),
        compiler_params=_cp(("arbitrary",), 32),
    )(me.reshape(1), land, own)


def _adamw(w, g, m, v, name):
    R, C = w.shape
    br = _block_rows(R, 2 * 7 * C * 4)

    def body(w_ref, g_ref, m_ref, v_ref, d_ref, mo_ref, vo_ref):
        gv = g_ref[...]
        m2 = B1 * m_ref[...] + (1.0 - B1) * gv
        v2 = B2 * v_ref[...] + (1.0 - B2) * (gv * gv)
        mo_ref[...] = m2
        vo_ref[...] = v2
        m_hat = m2 / (1.0 - B1 ** STEP)
        v_hat = v2 / (1.0 - B2 ** STEP)
        d_ref[...] = -LR * (m_hat / (jnp.sqrt(v_hat) + AEPS) + WD * w_ref[...])

    blk = pl.BlockSpec((br, C), lambda i: (i, 0))
    return _pallas_hbm(
        body, name=name, grid=(R // br,), in_specs=[blk] * 4, out_specs=[blk] * 3,
        out_shape=[SDS((R, C), f32)] * 3, compiler_params=_cp(("arbitrary",), 32),
    )(w, g, m, v)


def _adamw_nd(w, g, m, v, name, swapped=False):
    if swapped:
        outs = _adamw_nd(*(jnp.swapaxes(a, -1, -2) for a in (w, g, m, v)), name)
        return [jnp.swapaxes(o, -1, -2) for o in outs]
    shp = w.shape
    r2 = (-1, shp[-1]) if w.ndim > 1 else (8, shp[0] // 8)
    outs = _adamw(w.reshape(r2), g.reshape(r2), m.reshape(r2), v.reshape(r2), name)
    return [o.reshape(shp) for o in outs]


def _par_rows(mod_l, s, gain):
    rows = jnp.pad(mod_l.reshape(9, D)[3 * s:3 * s + 3], ((0, 5), (0, 0)))
    return rows + jnp.pad(gain[None, :], ((3, 4), (0, 0)))


def _pad_rows(a):
    a = a.reshape(-1, CH)
    return jnp.pad(a, ((0, (-a.shape[0]) % 8), (0, 0)))


def _prepare(me, c, ada_w, ada_b, norm_g):
    pay = jnp.pad(c, ((0, 7), (0, 0)))
    pay = jnp.concatenate([pay, jnp.pad(norm_g.reshape(6, OUTS), ((0, 2), (0, D - OUTS)))], axis=0)
    got = _all_gather_small(pay, "gather_c")
    c_all = got[:, 0, :]
    gains = got[:, 8:14, :OUTS].transpose(1, 0, 2).reshape(DEPTH, 3, D)

    ada_b_loc = lax.dynamic_slice(ada_b, (0, me * ADAS), (DEPTH, ADAS)).reshape(DEPTH, 1, ADAS)
    mod_cols = _mod_fwd(c_all, ada_w, ada_b_loc)
    got = _all_gather_small(mod_cols.reshape(DEPTH * NDEV, ADAS), "gather_mod").reshape(NDEV, DEPTH, NDEV, ADAS)
    mod = lax.dynamic_index_in_dim(got, me, axis=2, keepdims=False).transpose(1, 0, 2).reshape(DEPTH, 9 * D)
    pars = [[_par_rows(mod[l], s, gains[l, s]) for s in range(3)] for l in range(DEPTH)]
    return c_all, pars


def _fwd_bwd(x2, target, pars, get_w, put_g, small_ready, sgu_ln_g, sgu_ln_b, sgu_w, sgu_b, final_g):
    tabs = _rope_tables()
    tril = jnp.tril(jnp.ones((CH, CH), dtype=bool))
    behind = lambda rows, token: rows if token is None else rows + token[0, 0]

    h = x2
    saved = []
    for l in range(DEPTH):
        lng, lnb = sgu_ln_g[l].reshape(1, HW), sgu_ln_b[l].reshape(1, HW)
        bs_t = sgu_b[l].T
        h0 = h
        w, token = get_w(4 * l, ("ffn1", l), h0)
        h1, fo1, g1, u1, a1 = _ffn_fwd(h0, behind(pars[l][0], token), w)
        w, token = get_w(4 * l + 2, ("inproj", l), h1)
        ua, va, q, k, v = _inproj_fwd(h1, behind(pars[l][1], token), w, tabs)
        branches = [_attn_fwd(q, k, v, d) for d in PATTERN_DILATIONS]
        o, lse = _attn_combine([b[0] for b in branches], [b[1] for b in branches])
        mixed = _sgu_fwd(ua, va, o, lng, lnb, sgu_w[l], bs_t)
        w, token = get_w(4 * l + 3, ("outproj", l), mixed)
        h2, po = _outproj_fwd(mixed, w, h1, behind(pars[l][1], token))
        w, token = get_w(4 * l + 1, ("ffn2", l), h2)
        h3, fo2, g2, u2, a2 = _ffn_fwd(h2, behind(pars[l][2], token), w)
        saved.append((h0, h1, h2, fo1, g1, u1, a1, ua, va, q, k, v, o, mixed, lse, po, fo2, g2, u2, a2))
        h = h3

    par_f = jnp.pad(final_g[None, :], ((3, 4), (0, 0)))
    dh, head = _loss_head(h, par_f, target)

    dmods, dgains, dsgu = [None] * DEPTH, [None] * DEPTH, [None] * DEPTH
    token = None
    for l in reversed(range(DEPTH)):
        w_f1, w_f2, w_i, w_o = (get_w(4 * l + j, None, None)[0] for j in (0, 1, 2, 3))
        h0, h1, h2, fo1, g1, u1, a1, ua, va, q, k, v, o, mixed, lse, po, fo2, g2, u2, a2 = saved[l]
        lng, lnb = sgu_ln_g[l].reshape(1, HW), sgu_ln_b[l].reshape(1, HW)
        bs_t = sgu_b[l].T

        dh, dg_s, du_s, y, dfb, dpar3 = _ffn_bwd_tok(dh, h2, behind(pars[l][2], token), fo2, g2, u2, w_f2)
        token = put_g(4 * l + 1, _ffn_bwd_w(a2, dg_s, du_s, y, dfb))

        dob, d_a, d_b, dpar2g = _outproj_bwd(dh, po, w_o, behind(pars[l][1], token))
        token = put_g(4 * l + 3, _tn_matmul(mixed, dob, 512, "w_out_grad"))
        dqkv = None
        for d in PATTERN_DILATIONS:
            dqkv = _attn_bwd(q, k, v, o, d_b, lse, d, dqkv)
        dsg, dln, dws, dbl = _sgu_bwd(ua, va, d_a, lng, lnb, sgu_w[l], bs_t)
        dh, dp, y2, dpar2 = _inproj_bwd_tok(dh, h1, behind(pars[l][1], token), w_i, dsg, *dqkv, tabs)
        token = put_g(4 * l + 2, _tn_matmul(dp, y2, 640, "w_in_grad"))

        dh, dg_s, du_s, y, dfb, dpar1 = _ffn_bwd_tok(dh, h0, behind(pars[l][0], token), fo1, g1, u1, w_f1)
        dmods[l] = jnp.concatenate([dpar1[0:3], dpar2[0:2], dpar2g[2:3], dpar3[0:3]], axis=0).reshape(9 * D)
        dgains[l] = jnp.stack([dpar1[3], dpar2[3], dpar3[3]])
        dsgu[l] = (dln[0], dln[1], jnp.where(tril[None], dws, 0.0), jnp.sum(dbl, axis=-1))
        token = small_ready(head, dmods, dgains, dsgu) if l == 0 else None
        token = put_g(4 * l, _ffn_bwd_w(a1, dg_s, du_s, y, dfb, token))
    return dh, token


def kernel(x, c, ada_w, ada_b, norm_g, ffn1_wg, ffn1_wu, ffn1_wd, ffn2_wg, ffn2_wu, ffn2_wd, w_in, sgu_ln_g, sgu_ln_b, sgu_w, sgu_b, w_out, final_g, loss_target, m_ada_w, m_ada_b, m_norm_g, m_ffn1_wg, m_ffn1_wu, m_ffn1_wd, m_ffn2_wg, m_ffn2_wu, m_ffn2_wd, m_w_in, m_sgu_ln_g, m_sgu_ln_b, m_sgu_w, m_sgu_b, m_w_out, m_final_g, v_ada_w, v_ada_b, v_norm_g, v_ffn1_wg, v_ffn1_wu, v_ffn1_wd, v_ffn2_wg, v_ffn2_wu, v_ffn2_wd, v_w_in, v_sgu_ln_g, v_sgu_ln_b, v_sgu_w, v_sgu_b, v_w_out, v_final_g):
    me = 4 * lax.axis_index("x") + 2 * lax.axis_index("y") + lax.axis_index("c")

    tr = lambda w: jnp.swapaxes(w, -1, -2).astype(bf16)
    locs = []
    for l in range(DEPTH):
        locs.append(jnp.stack([tr(ffn1_wg[l]), tr(ffn1_wu[l]), ffn1_wd[l].astype(bf16)]))
        locs.append(jnp.stack([tr(ffn2_wg[l]), tr(ffn2_wu[l]), ffn2_wd[l].astype(bf16)]))
        locs.append(tr(w_in[l]))
        locs.append(w_out[l].astype(bf16))

    c_all, pars = _prepare(me, c, ada_w, ada_b, norm_g)
    locs, pars = lax.optimization_barrier((locs, pars))

    first, locs = lax.optimization_barrier((_all_gather_rows([locs[0]], "gather_first")[0], locs))
    placed = [_place_own(a, me) for a in locs[1:]]
    ready = {0: first}
    groups = ([2, 3], [1], [4], [6, 7], [5])
    plan = {("inproj", 0): (("wait", "gather", 0), ("start", "chips", 1)),
            ("outproj", 0): (("wait", "chips", 1), ("start", "pass", 1), ("start", "chips", 2)),
            ("ffn2", 0): (("wait", "pass", 1),),
            ("ffn1", 1): (("wait", "chips", 2), ("start", "pass", 2), ("start", "gather", 3), ("wait", "pass", 2)),
            ("inproj", 1): (("wait", "gather", 3), ("start", "chips", 4)),
            ("outproj", 1): (("wait", "chips", 4), ("start", "pass", 4)),
            ("ffn2", 1): (("wait", "pass", 4),)}
    flying = {}

    def exchange(action, mode, gi, after):
        if action == "start":
            arrays = flying.pop(gi) if mode == "pass" else [placed[p - 1] for p in groups[gi]]
            arrays, _ = lax.optimization_barrier((arrays, after))
            started = _exchange_start(mode, arrays, "%s_start_%d" % (mode, gi))
            flying[gi] = started[:-1]
            return started[-1]
        arrays = _exchange_wait(mode, flying.pop(gi), after, "%s_wait_%d" % (mode, gi))
        if mode == "chips":
            flying[gi] = arrays
        else:
            ready.update(zip(groups[gi], arrays))
        return None

    first_token = exchange("start", "gather", 0, first)

    def get_w(piece, point, after):
        token = first_token if point == ("ffn1", 0) else None
        for action, mode, gi in plan.get(point, ()):
            started = exchange(action, mode, gi, after)
            if started is not None:
                token = after = started
        return ready.get(piece), token

    sent = {}

    def put_g(piece, grad):
        land = lax.empty((NDEV,) + grad.shape[:-2] + (grad.shape[-2] // NDEV, D), bf16)
        started = _exchange_start("scatter", [grad, land], "scatter_start_%d" % piece)
        sent[piece] = started[:-1]
        return started[-1]

    small_sent = []

    def small_ready(head, dmods, dgains, dsgu):
        loss_part = 0.5 * jnp.sum(head[0]) / D
        small = jnp.concatenate([
            _pad_rows(jnp.stack(dmods)), _pad_rows(jnp.stack(dgains)),
            _pad_rows(jnp.stack([s[0] for s in dsgu])), _pad_rows(jnp.stack([s[1] for s in dsgu])),
            _pad_rows(jnp.stack([s[3] for s in dsgu])), _pad_rows(jnp.stack([s[2] for s in dsgu])),
            _pad_rows(head[3]), _pad_rows(jnp.pad(loss_part[None], (0, CH - 1)))], axis=0)
        started = _exchange_start("gather", [_place_own(small, me, "place_small")], "small_start")
        small_sent.append(started[:-1])
        return started[-1]

    dh, last = _fwd_bwd(x[0], loss_target[0], pars, get_w, put_g, small_ready, sgu_ln_g, sgu_ln_b, sgu_w, sgu_b, final_g)
    grad_x = dh[None]

    got = _exchange_wait("gather", small_sent[0], last, "small_wait")[0].reshape(NDEV, -1, CH)
    tot = _sum_slots(got, "sum_small")
    n_mod, n_gain, n_sw = DEPTH * 9 * D // CH, DEPTH * 3 * D // CH, DEPTH * 4 * CH
    offs = [0, n_mod, n_mod + n_gain, n_mod + n_gain + 8, n_mod + n_gain + 16, n_mod + n_gain + 24]
    g_ada_b = tot[offs[0]:offs[1]].reshape(DEPTH, 9 * D)
    g_gain_full = tot[offs[1]:offs[2]].reshape(DEPTH, 3, D)
    g_ln_g = tot[offs[2]:offs[3]].reshape(DEPTH, 4, CH)
    g_ln_b = tot[offs[3]:offs[4]].reshape(DEPTH, 4, CH)
    g_sb = tot[offs[4]:offs[5]].reshape(DEPTH, 4, CH)
    g_sw = tot[offs[5]:offs[5] + n_sw].reshape(DEPTH, 4, CH, CH)
    g_final = tot[offs[5] + n_sw:offs[5] + n_sw + 8].reshape(D)
    loss = tot[offs[5] + n_sw + 8, 0]
    g_norm = lax.dynamic_slice(g_gain_full, (0, 0, me * OUTS), (DEPTH, 3, OUTS))

    dmod_all = got[:, offs[0]:offs[1]].reshape(NDEV, DEPTH, 9 * D)
    dmod_cols = lax.dynamic_slice(dmod_all, (0, 0, me * ADAS), (NDEV, DEPTH, ADAS)).transpose(1, 0, 2)
    g_ada_w = _ada_grad((c_all * jax.nn.sigmoid(c_all)).T, dmod_cols)

    sums, after = {}, tot

    def collect(piece, after):
        own, land = _exchange_wait("scatter", sent[piece], after, "scatter_wait_%d" % piece)
        sums[piece] = _sum_partials(land, own, me, "sum_grads")
        return sums[piece]

    for piece in (5, 7, 6, 4, 1, 3, 2):
        after = collect(piece, after)
    back = lambda t: jnp.swapaxes(t, -1, -2)
    f2 = jnp.stack([sums[4 * l + 1] for l in range(DEPTH)])
    g_w_in = back(jnp.stack([sums[4 * l + 2] for l in range(DEPTH)]))
    g_w_out = jnp.stack([sums[4 * l + 3] for l in range(DEPTH)])

    ws = [ada_w, ada_b, norm_g, ffn1_wg, ffn1_wu, ffn1_wd, ffn2_wg, ffn2_wu, ffn2_wd, w_in, sgu_ln_g, sgu_ln_b, sgu_w,
          sgu_b, w_out, final_g]
    ms = [m_ada_w, m_ada_b, m_norm_g, m_ffn1_wg, m_ffn1_wu, m_ffn1_wd, m_ffn2_wg, m_ffn2_wu, m_ffn2_wd, m_w_in,
          m_sgu_ln_g, m_sgu_ln_b, m_sgu_w, m_sgu_b, m_w_out, m_final_g]
    vs = [v_ada_w, v_ada_b, v_norm_g, v_ffn1_wg, v_ffn1_wu, v_ffn1_wd, v_ffn2_wg, v_ffn2_wu, v_ffn2_wd, v_w_in,
          v_sgu_ln_g, v_sgu_ln_b, v_sgu_w, v_sgu_b, v_w_out, v_final_g]
    gw = [g_ada_w, g_ada_b, g_norm, None, None, None, back(f2[:, 0]), back(f2[:, 1]), f2[:, 2],
          g_w_in, g_ln_g, g_ln_b, g_sw, g_sb, g_w_out, g_final]
    upd = [None] * len(ws)
    lane_narrow = (3, 4, 6, 7, 9)
    for i in (1, 2, 10, 11, 12, 13, 15, 6, 7, 8, 9, 14, 0):
        upd[i] = _adamw_nd(ws[i], gw[i], ms[i], vs[i], "adamw", i in lane_narrow)
    collect(0, upd[0][0])
    f1 = jnp.stack([sums[4 * l] for l in range(DEPTH)])
    gw[3:6] = [back(f1[:, 0]), back(f1[:, 1]), f1[:, 2]]
    for i in (3, 4, 5):
        upd[i] = _adamw_nd(ws[i], gw[i], ms[i], vs[i], "adamw", i in lane_narrow)
    return (loss, grad_x, *gw, *[u[0] for u in upd], *[u[1] for u in upd], *[u[2] for u in upd])
```

```python
import math

import jax
import jax.numpy as jnp
from jax import lax
from jax.experimental import pallas as pl
from jax.experimental.pallas import tpu as pltpu

f32, bf16 = jnp.float32, jnp.bfloat16
SDS = jax.ShapeDtypeStruct

T, D, F = 4096, 1024, 2816
NDEV, DEPTH = 8, 2
HW = 512
INW = 5 * HW
FS, INS, OUTS, ADAS = F // NDEV, INW // NDEV, D // NDEV, 9 * D // NDEV
CH = 128
PATTERN_DILATIONS = (1, 4, 16)
ROPE_THETA = 10000.0
EPS = 1e-6
LR, B1, B2, AEPS, WD, STEP = 0.001, 0.9, 0.999, 1e-08, 0.01, 10
MESH = pl.DeviceIdType.MESH


def _cp(sems, vmem_mb):
    return pltpu.CompilerParams(dimension_semantics=sems, vmem_limit_bytes=vmem_mb << 20)


def _pallas_hbm(body, *, out_shape, **kw):
    typed = jax.tree.map(lambda s: pltpu.HBM(s.shape, s.dtype), out_shape)
    call = pl.pallas_call(body, out_shape=typed, **kw)

    def run(*operands):
        pin = lambda x: x if x.dtype == jnp.int32 else pltpu.with_memory_space_constraint(x, pltpu.HBM)
        return call(*[pin(x) for x in operands])

    return run


def _nn(a, b):
    return lax.dot_general(a, b, (((1,), (0,)), ((), ())), preferred_element_type=f32)


def _nt(a, b):
    return lax.dot_general(a, b, (((1,), (1,)), ((), ())), preferred_element_type=f32)


def _tn(a, b):
    return lax.dot_general(a, b, (((0,), (0,)), ((), ())), preferred_element_type=f32)


def _colsum(a):
    return jnp.sum(a, axis=0, keepdims=True)


def _rowmean(a):
    return jnp.mean(a, axis=-1, keepdims=True)


def _norm_mod(x, par):
    rstd = lax.rsqrt(_rowmean(x * x) + EPS)
    xhat = x * rstd
    n = xhat * par[3:4, :]
    y = n * (1.0 + par[1:2, :]) + par[0:1, :]
    return y, n, xhat, rstd


def _norm_mod_bwd(dy, x, par):
    _, n, xhat, rstd = _norm_mod(x, par)
    dn = dy * (1.0 + par[1:2, :])
    dxhat = dn * par[3:4, :]
    dx = rstd * (dxhat - xhat * _rowmean(dxhat * xhat))
    return dx, _colsum(dy), _colsum(dy * n), _colsum(dn * xhat)


_GK = math.sqrt(2.0 / math.pi)


def _gelu(x):
    return 0.5 * x * (1.0 + jnp.tanh(_GK * (x + 0.044715 * x * x * x)))


def _gelu_grad(x):
    t = jnp.tanh(_GK * (x + 0.044715 * x * x * x))
    return 0.5 * (1.0 + t) + 0.5 * x * (1.0 - t * t) * (_GK * (1.0 + 3.0 * 0.044715 * x * x))


def _rows(ref, idx, r):
    if len(ref.shape) == 3:
        return ref.at[:, pl.ds(idx * r, r), :]
    return ref.at[pl.ds(idx * r, r), :]


def _flip(v, bit):
    return 1 - v if bit else v


def _all_gather_small(x, name):
    R, C = x.shape

    def body(x_ref, out_ref, send_sems, recv_sems):
        mx, my, mc = lax.axis_index("x"), lax.axis_index("y"), lax.axis_index("c")
        me = 4 * mx + 2 * my + mc
        out_ref[me] = x_ref[...]
        sent = []
        for k in range(1, NDEV):
            peer = (_flip(mx, k & 4), _flip(my, k & 2), _flip(mc, k & 1))
            cp = pltpu.make_async_remote_copy(
                src_ref=x_ref, dst_ref=out_ref.at[me], send_sem=send_sems.at[k - 1],
                recv_sem=recv_sems.at[k - 1], device_id=peer, device_id_type=MESH)
            cp.start()
            sent.append(cp)
        for k in range(1, NDEV):
            peer = (_flip(mx, k & 4), _flip(my, k & 2), _flip(mc, k & 1))
            pidx = 4 * peer[0] + 2 * peer[1] + peer[2]
            pltpu.make_async_remote_copy(
                src_ref=x_ref, dst_ref=out_ref.at[pidx], send_sem=send_sems.at[k - 1],
                recv_sem=recv_sems.at[k - 1], device_id=peer, device_id_type=MESH).wait_recv()
        for cp in sent:
            cp.wait_send()

    vm = pl.BlockSpec(memory_space=pltpu.VMEM)
    return pl.pallas_call(
        body, name=name, out_shape=SDS((NDEV, R, C), f32), in_specs=[vm], out_specs=vm,
        scratch_shapes=[pltpu.SemaphoreType.DMA((NDEV - 1,)), pltpu.SemaphoreType.DMA((NDEV - 1,))],
        compiler_params=pltpu.CompilerParams(vmem_limit_bytes=32 << 20),
    )(x)


def _all_gather_rows(locs, name):
    n = len(locs)
    rs = [a.shape[-2] for a in locs]

    def body(*refs):
        src, out = refs[:n], refs[n:2 * n]
        send_sems, recv_sems, loc_sems = refs[2 * n:]
        mx, my, mc = lax.axis_index("x"), lax.axis_index("y"), lax.axis_index("c")
        me, sib = (mx, my, mc), (mx, my, 1 - mc)
        chips = [(1 - mx, my), (mx, 1 - my), (1 - mx, 1 - my)]

        def blk(a, p):
            return _rows(out[a], 4 * p[0] + 2 * p[1] + p[2], rs[a])

        def copy(k, a, block, to, from_src=False):
            return pltpu.make_async_remote_copy(
                src_ref=src[a] if from_src else blk(a, block), dst_ref=blk(a, block),
                send_sem=send_sems.at[k * n + a], recv_sem=recv_sems.at[k * n + a],
                device_id=to, device_id_type=MESH)

        mine = [pltpu.make_async_copy(src[a], blk(a, me), loc_sems.at[a]) for a in range(n)]
        for m in mine:
            m.start()
        first = []
        for j, chip in enumerate(chips):
            first += [copy(1 + j, a, me, (*chip, mc), True) for a in range(n)]
        first += [copy(0, a, me, sib, True) for a in range(n)]
        for cp in first:
            cp.start()
        passed = []
        for j, chip in enumerate(chips):
            for a in range(n):
                copy(1 + j, a, (*chip, mc), me).wait_recv()
            fwd = [copy(4 + j, a, (*chip, mc), sib) for a in range(n)]
            for cp in fwd:
                cp.start()
            passed += fwd
        for a in range(n):
            copy(0, a, sib, me).wait_recv()
        for j, chip in enumerate(chips):
            for a in range(n):
                copy(4 + j, a, (*chip, 1 - mc), me).wait_recv()
        for cp in first + passed:
            cp.wait_send()
        for m in mine:
            m.wait()

    hbm = pl.BlockSpec(memory_space=pl.ANY)
    out_shape = [SDS(a.shape[:-2] + (NDEV * a.shape[-2], a.shape[-1]), a.dtype) for a in locs]
    return pl.pallas_call(
        body, name=name, out_shape=out_shape, in_specs=[hbm] * n, out_specs=[hbm] * n,
        scratch_shapes=[pltpu.SemaphoreType.DMA((7 * n,)), pltpu.SemaphoreType.DMA((7 * n,)),
                        pltpu.SemaphoreType.DMA((n,))],
    )(*locs)


HBM_SPEC = pl.BlockSpec(memory_space=pltpu.HBM)
SEM_SPEC = pl.BlockSpec(memory_space=pltpu.SEMAPHORE)
DATAFLOW_EFFECT = pltpu.SideEffectType.DATAFLOW_SIDE_EFFECTING


def _place_own(loc, me, name="place_own"):
    r, cols = loc.shape[-2:]
    loc3 = loc.reshape(-1, r, cols)
    kk = loc3.shape[0]

    def body(me_ref, src_ref, full_ref, out_ref):
        out_ref[...] = src_ref[...]

    out = _pallas_hbm(
        body, name=name, out_shape=SDS((kk, NDEV * r, cols), loc.dtype),
        grid_spec=pltpu.PrefetchScalarGridSpec(
            num_scalar_prefetch=1, grid=(kk,),
            in_specs=[pl.BlockSpec((None, r, cols), lambda i, me_ref: (i, 0, 0)), pl.BlockSpec(memory_space=pl.ANY)],
            out_specs=pl.BlockSpec((None, r, cols), lambda i, me_ref: (i, me_ref[0], 0))),
        input_output_aliases={2: 0}, compiler_params=_cp(("arbitrary",), 32),
    )(me.reshape(1), loc3, lax.empty((kk, NDEV * r, cols), loc.dtype))
    return out.reshape(loc.shape[:-2] + (NDEV * r, cols))


EXCHANGE_PEERS = {"gather": (1, 2, 3, 4, 5, 6, 7), "scatter": (1, 2, 3, 4, 5, 6, 7), "chips": (1, 2, 4, 6), "pass": (2, 4, 6)}


def _exchange_copies(mode, bufs, n, send_sems, recv_sems):
    mx, my, mc = lax.axis_index("x"), lax.axis_index("y"), lax.axis_index("c")
    me = 4 * mx + 2 * my + mc
    out = []
    for slot, k in enumerate(EXCHANGE_PEERS[mode]):
        peer = (_flip(mx, k & 4), _flip(my, k & 2), _flip(mc, k & 1))
        pidx = 4 * peer[0] + 2 * peer[1] + peer[2]
        for a in range(n):
            r = bufs[a].shape[-2] // NDEV
            if mode == "scatter":
                src, dst, arrive = _rows(bufs[a], pidx, r), bufs[n + a].at[me], bufs[n + a].at[pidx]
            elif mode == "pass":
                peer = (mx, my, 1 - mc)
                src, dst, arrive = _rows(bufs[a], pidx, r), _rows(bufs[a], pidx, r), _rows(bufs[a], pidx + 1 - 2 * mc, r)
            else:
                src, dst, arrive = _rows(bufs[a], me, r), _rows(bufs[a], me, r), _rows(bufs[a], pidx, r)
            sems = dict(send_sem=send_sems.at[slot * n + a], recv_sem=recv_sems.at[slot * n + a],
                        device_id=peer, device_id_type=MESH)
            out.append((pltpu.make_async_remote_copy(src_ref=src, dst_ref=dst, **sems),
                        pltpu.make_async_remote_copy(src_ref=src, dst_ref=arrive, **sems)))
    return out


def _exchange_start(mode, arrays, name):
    m = len(arrays)
    n = m // 2 if mode == "scatter" else m

    def body(*refs):
        send_sems, recv_sems, token = refs[m], refs[m + 1], refs[-1]
        for go, _ in _exchange_copies(mode, refs[:m], n, send_sems, recv_sems):
            go.start()
        token[...] = jnp.zeros_like(token)

    sems = pltpu.SemaphoreType.DMA((len(EXCHANGE_PEERS[mode]) * n,))
    return pl.pallas_call(
        body, name=name, in_specs=[HBM_SPEC] * m,
        out_shape=(sems, sems, *[pltpu.HBM(a.shape, a.dtype) for a in arrays], SDS((8, CH), f32)),
        out_specs=(SEM_SPEC, SEM_SPEC, *[HBM_SPEC] * m, pl.BlockSpec(memory_space=pltpu.VMEM)),
        input_output_aliases={a: 2 + a for a in range(m)},
        compiler_params=pltpu.CompilerParams(has_side_effects=DATAFLOW_EFFECT),
    )(*[pltpu.with_memory_space_constraint(a, pltpu.HBM) for a in arrays])


def _exchange_wait(mode, started, after, name):
    send_sems, recv_sems, *arrays = started
    m = len(arrays)
    n = m // 2 if mode == "scatter" else m

    def body(*refs):
        for go, arrive in _exchange_copies(mode, refs[:m], n, refs[m], refs[m + 1]):
            go.wait_send()
            arrive.wait_recv()

    return pl.pallas_call(
        body, name=name, in_specs=[HBM_SPEC] * m + [SEM_SPEC, SEM_SPEC, pl.BlockSpec(memory_space=pl.ANY)],
        out_shape=[pltpu.HBM(a.shape, a.dtype) for a in arrays], out_specs=[HBM_SPEC] * m,
        input_output_aliases={a: a for a in range(m)},
        compiler_params=pltpu.CompilerParams(has_side_effects=DATAFLOW_EFFECT),
    )(*arrays, send_sems, recv_sems, after)


def _mod_fwd(c_all, ada_w, ada_b_loc):
    def body(c_ref, w_ref, b_ref, o_ref):
        ca = c_ref[...]
        ca = ca * jax.nn.sigmoid(ca)
        o_ref[...] = jnp.dot(ca, w_ref[...], precision=lax.Precision.HIGHEST,
                             preferred_element_type=f32) + b_ref[...]

    return _pallas_hbm(
        body, name="mod_fwd", grid=(DEPTH,), out_shape=SDS((DEPTH, NDEV, ADAS), f32),
        in_specs=[pl.BlockSpec((NDEV, D), lambda l: (0, 0)),
                  pl.BlockSpec((None, D, ADAS), lambda l: (l, 0, 0)),
                  pl.BlockSpec((None, 1, ADAS), lambda l: (l, 0, 0))],
        out_specs=pl.BlockSpec((None, NDEV, ADAS), lambda l: (l, 0, 0)),
        compiler_params=_cp(("arbitrary",), 32),
    )(c_all, ada_w, ada_b_loc)


def _ada_grad(cact_t, dmod_cols):
    def body(c_ref, d_ref, o_ref):
        acc = c_ref[:, 0:1] * d_ref[0:1, :]
        for b in range(1, NDEV):
            acc = acc + c_ref[:, b:b + 1] * d_ref[b:b + 1, :]
        o_ref[...] = acc

    tr = 256
    return _pallas_hbm(
        body, name="ada_grad", grid=(DEPTH, D // tr), out_shape=SDS((DEPTH, D, ADAS), f32),
        in_specs=[pl.BlockSpec((tr, NDEV), lambda l, i: (i, 0)),
                  pl.BlockSpec((None, NDEV, ADAS), lambda l, i: (l, 0, 0))],
        out_specs=pl.BlockSpec((None, tr, ADAS), lambda l, i: (l, i, 0)),
        compiler_params=_cp(("arbitrary", "arbitrary"), 32),
    )(cact_t, dmod_cols)


def _ffn_fwd(h, par, w3):
    tm, tf = 1024, 256
    nj = F // tf

    def body(h_ref, par_ref, wg_ref, wu_ref, wd_ref, ho_ref, fo_ref, g_ref, u_ref, a_ref, y_scr, acc):
        j = pl.program_id(1)

        @pl.when(j == 0)
        def _():
            y_scr[...] = _norm_mod(h_ref[...], par_ref[...])[0].astype(bf16)
            acc[...] = jnp.zeros_like(acc)

        y = y_scr[...]
        g = _nt(y, wg_ref[...])
        u = _nt(y, wu_ref[...])
        a = ((g * jax.nn.sigmoid(g)) * u).astype(bf16)
        g_ref[...] = g.astype(bf16)
        u_ref[...] = u.astype(bf16)
        a_ref[...] = a
        acc[...] += _nn(a, wd_ref[...])

        @pl.when(j == nj - 1)
        def _():
            fo_ref[...] = acc[...].astype(bf16)
            ho_ref[...] = h_ref[...] + (0.5 * par_ref[2:3, :]) * acc[...]

    row = pl.BlockSpec((tm, D), lambda i, j: (i, 0))
    hid = pl.BlockSpec((tm, tf), lambda i, j: (i, j))
    wspec = [pl.BlockSpec((None, tf, D), lambda i, j, k=k: (k, j, 0)) for k in range(3)]
    return _pallas_hbm(
        body, name="ffn_fwd", grid=(T // tm, nj),
        in_specs=[row, pl.BlockSpec((8, D), lambda i, j: (0, 0))] + wspec,
        out_specs=[row, row, hid, hid, hid],
        out_shape=[SDS((T, D), f32), SDS((T, D), bf16), SDS((T, F), bf16), SDS((T, F), bf16), SDS((T, F), bf16)],
        scratch_shapes=[pltpu.VMEM((tm, D), bf16), pltpu.VMEM((tm, D), f32)],
        compiler_params=_cp(("arbitrary", "arbitrary"), 52),
    )(h, par, w3, w3, w3)


def _ffn_bwd_tok(dh, h, par, fo, gs, us, w3):
    tm, tf = 1024, 256
    nj = F // tf

    def body(dh_ref, h_ref, par_ref, fo_ref, g_ref, u_ref, wg_ref, wu_ref, wd_ref, wdn_ref,
             dhin_ref, dg_ref, du_ref, y_ref, dfb_ref, dpar_ref, df_scr, dyacc, da_scr):
        i, j = pl.program_id(0), pl.program_id(1)

        @pl.when(jnp.logical_and(i == 0, j == 0))
        def _():
            dpar_ref[...] = jnp.zeros_like(dpar_ref)

        @pl.when(j == 0)
        def _():
            dh_v, par_v = dh_ref[...], par_ref[...]
            dfb = ((0.5 * par_v[2:3, :]) * dh_v).astype(bf16)
            df_scr[...] = dfb
            dfb_ref[...] = dfb
            dpar_ref[2:3, :] += 0.5 * _colsum(dh_v * fo_ref[...].astype(f32))
            y_ref[...] = _norm_mod(h_ref[...], par_v)[0].astype(bf16)
            dyacc[...] = jnp.zeros_like(dyacc)
            da_scr[...] = _nt(dfb, wd_ref[...])

        for half in range(2):
            rs = slice(half * (tm // 2), (half + 1) * (tm // 2))
            da = da_scr[rs, :]
            da_scr[rs, :] = _nt(df_scr[rs, :], wdn_ref[...])
            g = g_ref[rs, :].astype(f32)
            u = u_ref[rs, :].astype(f32)
            sig = jax.nn.sigmoid(g)
            du = (da * (g * sig)).astype(bf16)
            dg = (da * u * (sig * (1.0 + g * (1.0 - sig)))).astype(bf16)
            dg_ref[rs, :] = dg
            du_ref[rs, :] = du
            dyacc[rs, :] += _nn(dg, wg_ref[...]) + _nn(du, wu_ref[...])

        @pl.when(j == nj - 1)
        def _():
            dx, dsh, dsc, dng = _norm_mod_bwd(dyacc[...], h_ref[...], par_ref[...])
            dhin_ref[...] = dh_ref[...] + dx
            dpar_ref[0:1, :] += dsh
            dpar_ref[1:2, :] += dsc
            dpar_ref[3:4, :] += dng

    row = pl.BlockSpec((tm, D), lambda i, j: (i, 0))
    hid = pl.BlockSpec((tm, tf), lambda i, j: (i, j))
    one = pl.BlockSpec((8, D), lambda i, j: (0, 0))
    wspec = [pl.BlockSpec((None, tf, D), lambda i, j, k=k: (k, j, 0)) for k in range(3)]
    wspec.append(pl.BlockSpec((None, tf, D), lambda i, j: (2, jnp.minimum(j + 1, nj - 1), 0)))
    return _pallas_hbm(
        body, name="ffn_bwd_tok", grid=(T // tm, nj),
        in_specs=[row, row, one, row, hid, hid] + wspec,
        out_specs=[row, hid, hid, row, row, one],
        out_shape=[SDS((T, D), f32), SDS((T, F), bf16), SDS((T, F), bf16), SDS((T, D), bf16),
                   SDS((T, D), bf16), SDS((8, D), f32)],
        scratch_shapes=[pltpu.VMEM((tm, D), bf16), pltpu.VMEM((tm, D), f32), pltpu.VMEM((tm, tf), f32)],
        compiler_params=_cp(("arbitrary", "arbitrary"), 60),
    )(dh, h, par, fo, gs, us, w3, w3, w3, w3)


def _ffn_bwd_w(a_s, dg_s, du_s, y, dfb, anchor=None, part="all"):
    tf, tk = F // 2, 512
    nk = T // tk
    hid = pl.BlockSpec((tk, tf), lambda j, kk: (kk, j))
    row = pl.BlockSpec((tk, D), lambda j, kk: (kk, 0))
    pairs = {"all": ((1, 3), (2, 3), (0, 4)), "gate_up": ((1, 3), (2, 3)), "down": ((0, 4),)}[part]
    ops, specs = (a_s, dg_s, du_s, y, dfb), (hid, hid, hid, row, row)
    used = sorted({i for pr in pairs for i in pr})
    pairs = [(used.index(i), used.index(j)) for i, j in pairs]
    n = len(pairs)

    def body(*refs):
        ins, o_ref, accs = refs[:len(used)], refs[-n - 1], refs[-n:]
        kk = pl.program_id(1)

        @pl.when(kk == 0)
        def _():
            for acc in accs:
                acc[...] = jnp.zeros_like(acc)

        for acc, (i, j) in zip(accs, pairs):
            acc[...] += _tn(ins[i][...], ins[j][...])

        @pl.when(kk == nk - 1)
        def _():
            for t, acc in enumerate(accs):
                o_ref[t] = acc[...].astype(bf16)

    extra = [] if anchor is None else [anchor]
    return _pallas_hbm(
        body, name="ffn_bwd_w" if part == "all" else "ffn_bwd_w_" + part, grid=(F // tf, nk),
        in_specs=[specs[i] for i in used] + [pl.BlockSpec((8, CH), lambda j, kk: (0, 0))] * len(extra),
        out_specs=pl.BlockSpec((n, tf, D), lambda j, kk: (0, j, 0)), out_shape=SDS((n, F, D), bf16),
        scratch_shapes=[pltpu.VMEM((tf, D), f32)] * n,
        compiler_params=_cp(("arbitrary", "arbitrary"), 58),
    )(*[ops[i] for i in used], *extra)


def _tn_matmul(a, b, bm, name):
    M, N = a.shape[1], b.shape[1]
    tk = 1024
    nk = T // tk

    def body(a_ref, b_ref, o_ref, acc):
        kk = pl.program_id(1)

        @pl.when(kk == 0)
        def _():
            acc[...] = jnp.zeros_like(acc)

        acc[...] += _tn(a_ref[...], b_ref[...])

        @pl.when(kk == nk - 1)
        def _():
            o_ref[...] = acc[...].astype(bf16)

    return _pallas_hbm(
        body, name=name, grid=(M // bm, nk),
        in_specs=[pl.BlockSpec((tk, bm), lambda i, kk: (kk, i)), pl.BlockSpec((tk, N), lambda i, kk: (kk, 0))],
        out_specs=pl.BlockSpec((bm, N), lambda i, kk: (i, 0)), out_shape=SDS((M, N), bf16),
        scratch_shapes=[pltpu.VMEM((bm, N), f32)],
        compiler_params=_cp(("arbitrary", "arbitrary"), 40),
    )(a, b)


def _rope_tables():
    inv = ROPE_THETA ** (-jnp.arange(0, 64, 2, dtype=f32) / 64)
    ang = jnp.arange(T, dtype=f32)[:, None] * inv[None, :]
    ang = jnp.concatenate([ang, ang], axis=-1)
    cos, sin = jnp.tile(jnp.cos(ang), (1, 2)), jnp.tile(jnp.sin(ang), (1, 2))
    low = (jnp.arange(CH) % 64 < 32)[None, :]
    return cos, jnp.where(low, -sin, 0.0), jnp.where(low, 0.0, sin)


def _all_heads(tab_ref):
    return jnp.concatenate([tab_ref[...]] * 4, axis=1)


def _rope(t, cos, sin_lo, sin_hi):
    return t * cos + pltpu.roll(t, HW - 32, 1) * sin_lo + pltpu.roll(t, 32, 1) * sin_hi


def _rope_t(g, cos, sin_lo, sin_hi):
    return g * cos + pltpu.roll(g * sin_lo, 32, 1) + pltpu.roll(g * sin_hi, HW - 32, 1)


def _inproj_fwd(h, par, w_in_t, tabs):
    tm = 512

    def body(h_ref, par_ref, w_ref, cos_ref, slo_ref, shi_ref, ua_ref, va_ref, q_ref, k_ref, v_ref):
        y = _norm_mod(h_ref[...], par_ref[...])[0].astype(bf16)
        proj = lambda c: _nt(y, w_ref[c * HW:(c + 1) * HW, :])
        ua_ref[...] = proj(0)
        va_ref[...] = proj(1)
        cos, slo, shi = _all_heads(cos_ref), _all_heads(slo_ref), _all_heads(shi_ref)
        q_ref[...] = _rope(proj(2), cos, slo, shi) * 0.125
        k_ref[...] = _rope(proj(3), cos, slo, shi)
        v_ref[...] = proj(4)

    row = pl.BlockSpec((tm, D), lambda i: (i, 0))
    half = pl.BlockSpec((tm, HW), lambda i: (i, 0))
    return _pallas_hbm(
        body, name="inproj_fwd", grid=(T // tm,),
        in_specs=[row, pl.BlockSpec((8, D), lambda i: (0, 0)), pl.BlockSpec((INW, D), lambda i: (0, 0))] +
                 [pl.BlockSpec((tm, CH), lambda i: (i, 0))] * 3,
        out_specs=[half] * 5,
        out_shape=[SDS((T, HW), f32)] * 5,
        compiler_params=_cp(("arbitrary",), 48),
    )(h, par, w_in_t, *tabs)


def _head_masks():
    lane = lax.broadcasted_iota(jnp.int32, (1, CH), 1)
    return lane < 64, lane >= 64


def _lane(t, idx):
    return jnp.sum(jnp.where(lax.broadcasted_iota(jnp.int32, t.shape, 1) == idx, t, 0.0), axis=-1, keepdims=True)


def _stack_heads(x):
    lo, hi = _head_masks()
    zero = jnp.zeros_like(x)
    return jnp.concatenate([jnp.where(lo, x, zero), jnp.where(hi, x, zero)], axis=0)


def _band_mask(has_prev):
    row = lax.broadcasted_iota(jnp.int32, (2 * CH, 2 * CH), 0) & (CH - 1)
    col = lax.broadcasted_iota(jnp.int32, (2 * CH, 2 * CH), 1)
    in_prev = jnp.logical_and(jnp.logical_and(col < CH, col >= row), has_prev)
    return jnp.logical_or(in_prev, jnp.logical_and(col >= CH, col - CH <= row))


def _attn_tiling(d):
    return CH * 16, 16 // d, CH


def _attn_fwd(q, k, v, d):
    rows, blocks, lanes = _attn_tiling(d)
    pairs = lanes // CH

    def body(q_ref, kc_ref, kp_ref, vc_ref, vp_ref, o_ref, lse_ref):
        c, lb = pl.program_id(0), pl.program_id(1)
        col = lax.broadcasted_iota(jnp.int32, (CH, CH), 1)
        lo, _ = _head_masks()

        @pl.when(lb == 0)
        def _():
            lse_ref[...] = jnp.zeros_like(lse_ref)

        at = lambda r, b: pl.ds(r + b * CH * d, CH, stride=d) if d > 1 else pl.ds(b * CH, CH)
        for r in range(d):
            for b in range(blocks):
                own = at(r, b)
                k_prev, v_prev, before, mask = ((kc_ref, vc_ref, at(r, b - 1), _band_mask(True)) if b > 0 else
                                                (kp_ref, vp_ref, at(r, blocks - 1), _band_mask(c > 0)))
                lse_tile = lse_ref[own, :]
                for pi in range(pairs):
                    sl = slice(pi * CH, (pi + 1) * CH)
                    hp = lb * pairs + pi
                    kk = jnp.concatenate([k_prev[before, sl], kc_ref[own, sl]], axis=0).astype(bf16)
                    vv = jnp.concatenate([v_prev[before, sl], vc_ref[own, sl]], axis=0).astype(bf16)
                    s = jnp.where(mask, _nt(_stack_heads(q_ref[own, sl].astype(bf16)), kk), -jnp.inf)
                    m = jnp.max(s, axis=-1, keepdims=True)
                    p = jnp.exp(s - m)
                    den = jnp.sum(p, axis=-1, keepdims=True)
                    o = _nn(p.astype(bf16), vv) / den
                    o_ref[own, sl] = jnp.where(lo, o[:CH], o[CH:])
                    lse = m + jnp.log(den)
                    lse_tile = jnp.where(col == 2 * hp, lse[:CH], jnp.where(col == 2 * hp + 1, lse[CH:], lse_tile))
                lse_ref[own, :] = lse_tile

    cur = pl.BlockSpec((rows, lanes), lambda c, lb: (c, lb))
    prev = pl.BlockSpec((rows, lanes), lambda c, lb: (jnp.maximum(c - 1, 0), lb))
    return _pallas_hbm(
        body, name="attn_fwd_d%d" % d, grid=(T // rows, HW // lanes), in_specs=[cur, cur, prev, cur, prev],
        out_specs=[cur, pl.BlockSpec((rows, CH), lambda c, lb: (c, 0))],
        out_shape=[SDS((T, HW), f32), SDS((T, CH), f32)],
        compiler_params=_cp(("arbitrary", "arbitrary"), 40),
    )(q, k, k, v, v)


def _attn_combine(os_, lses):
    tm = 512

    def body(o0_ref, o1_ref, o2_ref, l0_ref, l1_ref, l2_ref, o_ref, lse_ref):
        l0, l1, l2 = l0_ref[...], l1_ref[...], l2_ref[...]
        m = jnp.maximum(jnp.maximum(l0, l1), l2)
        e = [jnp.exp(l0 - m), jnp.exp(l1 - m), jnp.exp(l2 - m)]
        s = e[0] + e[1] + e[2]
        w = [ei / s for ei in e]
        lse_ref[...] = m + jnp.log(s)
        lo, _ = _head_masks()
        for hp in range(4):
            sl = slice(hp * CH, (hp + 1) * CH)
            acc = jnp.zeros((tm, CH), f32)
            for wp, op_ref in zip(w, (o0_ref, o1_ref, o2_ref)):
                wexp = jnp.where(lo, wp[:, 2 * hp:2 * hp + 1], wp[:, 2 * hp + 1:2 * hp + 2])
                acc = acc + wexp * op_ref[:, sl].astype(f32)
            o_ref[:, sl] = acc

    half = pl.BlockSpec((tm, HW), lambda i: (i, 0))
    stat = pl.BlockSpec((tm, CH), lambda i: (i, 0))
    return _pallas_hbm(
        body, name="attn_combine", grid=(T // tm,), in_specs=[half] * 3 + [stat] * 3,
        out_specs=[half, stat], out_shape=[SDS((T, HW), f32), SDS((T, CH), f32)],
        compiler_params=_cp(("arbitrary",), 32),
    )(*os_, *lses)


def _attn_bwd(q, k, v, o, do, lse, d, running=None):
    rows, blocks, lanes = _attn_tiling(d)
    pairs = lanes // CH
    steps = T // rows

    def body(qc_ref, qn_ref, kc_ref, kp_ref, vc_ref, vp_ref, oc_ref, on_ref, dc_ref, dn_ref, lc_ref, ln_ref, *rest):
        dq_ref, dk_ref, dv_ref = rest[-3:]
        sofar = (lambda i, rows, sl: rest[i][rows, sl]) if running is not None else (lambda i, rows, sl: 0.0)
        c, lb = pl.program_id(0), pl.program_id(1)
        lo, _ = _head_masks()
        at = lambda r, b: pl.ds(r + b * CH * d, CH, stride=d) if d > 1 else pl.ds(b * CH, CH)
        for r in range(d):
            for b in range(blocks):
                own = at(r, b)
                k_prev, v_prev, before, mask_c = ((kc_ref, vc_ref, at(r, b - 1), _band_mask(True)) if b > 0 else
                                                  (kp_ref, vp_ref, at(r, blocks - 1), _band_mask(c > 0)))
                q_next, o_next, d_next, l_next, after, has_next = (
                    (qc_ref, oc_ref, dc_ref, lc_ref, at(r, b + 1), True) if b + 1 < blocks else
                    (qn_ref, on_ref, dn_ref, ln_ref, at(r, 0), c < steps - 1))
                mask_n = _band_mask(has_next)[:, :CH]
                lse_c, lse_n = lc_ref[own, :], l_next[after, :]
                for pi in range(pairs):
                    sl = slice(pi * CH, (pi + 1) * CH)
                    hp = lb * pairs + pi
                    kc, vc = kc_ref[own, sl].astype(bf16), vc_ref[own, sl].astype(bf16)
                    kk = jnp.concatenate([k_prev[before, sl].astype(bf16), kc], axis=0)
                    vv = jnp.concatenate([v_prev[before, sl].astype(bf16), vc], axis=0)
                    doc, don = dc_ref[own, sl], d_next[after, sl]
                    qs_c, qs_n = _stack_heads(qc_ref[own, sl].astype(bf16)), _stack_heads(q_next[after, sl].astype(bf16))
                    ds_c, ds_n = _stack_heads(doc.astype(bf16)), _stack_heads(don.astype(bf16))
                    delta_c = jnp.sum(_stack_heads(doc * oc_ref[own, sl]), axis=-1, keepdims=True)
                    delta_n = jnp.sum(_stack_heads(don * o_next[after, sl]), axis=-1, keepdims=True)
                    heads = lambda t: jnp.concatenate([_lane(t, 2 * hp), _lane(t, 2 * hp + 1)], axis=0)
                    p1 = jnp.where(mask_c, jnp.exp(_nt(qs_c, kk) - heads(lse_c)), 0.0)
                    g1 = (p1 * (_nt(ds_c, vv) - delta_c)).astype(bf16)
                    dq = _nn(g1, kk)
                    dq_ref[own, sl] = sofar(0, own, sl) + jnp.where(lo, dq[:CH], dq[CH:])
                    p2 = jnp.where(mask_n, jnp.exp(_nt(qs_n, kc) - heads(lse_n)), 0.0)
                    g2 = (p2 * (_nt(ds_n, vc) - delta_n)).astype(bf16)
                    dk_ref[own, sl] = sofar(1, own, sl) + _tn(jnp.concatenate([g1[:, CH:], g2], axis=0),
                                                              jnp.concatenate([qs_c, qs_n], axis=0))
                    dv_ref[own, sl] = sofar(2, own, sl) + _tn(
                        jnp.concatenate([p1[:, CH:].astype(bf16), p2.astype(bf16)], axis=0),
                        jnp.concatenate([ds_c, ds_n], axis=0))

    cur = pl.BlockSpec((rows, lanes), lambda c, lb: (c, lb))
    prev = pl.BlockSpec((rows, lanes), lambda c, lb: (jnp.maximum(c - 1, 0), lb))
    nxt = pl.BlockSpec((rows, lanes), lambda c, lb: (jnp.minimum(c + 1, steps - 1), lb))
    scur = pl.BlockSpec((rows, CH), lambda c, lb: (c, 0))
    snxt = pl.BlockSpec((rows, CH), lambda c, lb: (jnp.minimum(c + 1, steps - 1), 0))
    more = [] if running is None else list(running)
    return _pallas_hbm(
        body, name="attn_bwd_d%d" % d, grid=(steps, HW // lanes),
        in_specs=[cur, nxt, cur, prev, cur, prev, cur, nxt, cur, nxt, scur, snxt] + [cur] * len(more),
        out_specs=[cur] * 3, out_shape=[SDS((T, HW), f32)] * 3,
        input_output_aliases={12 + i: i for i in range(len(more))},
        compiler_params=_cp(("arbitrary", "arbitrary"), 52),
    )(q, q, k, k, v, v, o, o, do, do, lse, lse, *more)


def _causal(w):
    row = lax.broadcasted_iota(jnp.int32, (CH, CH), 0)
    col = lax.broadcasted_iota(jnp.int32, (CH, CH), 1)
    return jnp.where(col <= row, w, 0.0)


def _sgu_fwd(ua, va, o, lng, lnb, ws, bs_t):
    tm = 512

    def body(ua_ref, va_ref, o_ref, lg_ref, lb_ref, ws_ref, bs_ref, mix_ref):
        for hd in range(4):
            sl = slice(hd * CH, (hd + 1) * CH)
            w = _causal(ws_ref[hd]).astype(bf16)
            for cc in range(tm // CH):
                rs = slice(cc * CH, (cc + 1) * CH)
                u = _gelu(ua_ref[rs, sl])
                v = _gelu(va_ref[rs, sl])
                vc = v - _rowmean(v)
                vn = vc * lax.rsqrt(_rowmean(vc * vc) + EPS) * lg_ref[:, sl] + lb_ref[:, sl]
                z = _nn(w, vn.astype(bf16)) + bs_ref[:, hd:hd + 1]
                mix_ref[rs, sl] = (u * z).astype(bf16)
        mix_ref[:, HW:] = o_ref[...].astype(bf16)

    half = pl.BlockSpec((tm, HW), lambda i: (i, 0))
    vec = pl.BlockSpec((1, HW), lambda i: (0, 0))
    return _pallas_hbm(
        body, name="sgu_fwd", grid=(T // tm,),
        in_specs=[half, half, half, vec, vec, pl.BlockSpec((4, CH, CH), lambda i: (0, 0, 0)),
                  pl.BlockSpec((CH, 4), lambda i: (0, 0))],
        out_specs=pl.BlockSpec((tm, 2 * HW), lambda i: (i, 0)), out_shape=SDS((T, 2 * HW), bf16),
        compiler_params=_cp(("arbitrary",), 32),
    )(ua, va, o, lng, lnb, ws, bs_t)


def _sgu_bwd(ua, va, d_a, lng, lnb, ws, bs_t):
    tm = 512

    def body(ua_ref, va_ref, d_ref, lg_ref, lb_ref, ws_ref, bs_ref, dsg_ref, dln_ref, dws_ref, db_ref):
        @pl.when(pl.program_id(0) == 0)
        def _():
            dln_ref[...] = jnp.zeros_like(dln_ref)
            dws_ref[...] = jnp.zeros_like(dws_ref)
            db_ref[...] = jnp.zeros_like(db_ref)

        for hd in range(4):
            sl = slice(hd * CH, (hd + 1) * CH)
            w = _causal(ws_ref[hd])
            w, wt = w.astype(bf16), w.T.astype(bf16)
            lg = lg_ref[:, sl]
            for cc in range(tm // CH):
                rs = slice(cc * CH, (cc + 1) * CH)
                xa, xv, dd = ua_ref[rs, sl], va_ref[rs, sl], d_ref[rs, sl]
                u, v = _gelu(xa), _gelu(xv)
                vc = v - _rowmean(v)
                rstd = lax.rsqrt(_rowmean(vc * vc) + EPS)
                xh = vc * rstd
                vnb = (xh * lg + lb_ref[:, sl]).astype(bf16)
                z = _nn(w, vnb) + bs_ref[:, hd:hd + 1]
                dz = dd * u
                dzb = dz.astype(bf16)
                dsg_ref[rs, sl] = (dd * z * _gelu_grad(xa)).astype(bf16)
                dws_ref[hd] += _nt(dzb, vnb)
                db_ref[hd] += dz
                dvn = _nn(wt, dzb)
                dln_ref[0:1, sl] += _colsum(dvn * xh)
                dln_ref[1:2, sl] += _colsum(dvn)
                dxh = dvn * lg
                dv = rstd * (dxh - _rowmean(dxh) - xh * _rowmean(dxh * xh))
                dsg_ref[rs, HW + hd * CH:HW + (hd + 1) * CH] = (dv * _gelu_grad(xv)).astype(bf16)

    half = pl.BlockSpec((tm, HW), lambda i: (i, 0))
    vec = pl.BlockSpec((1, HW), lambda i: (0, 0))
    mat = pl.BlockSpec((4, CH, CH), lambda i: (0, 0, 0))
    return _pallas_hbm(
        body, name="sgu_bwd", grid=(T // tm,),
        in_specs=[half, half, half, vec, vec, mat, pl.BlockSpec((CH, 4), lambda i: (0, 0))],
        out_specs=[pl.BlockSpec((tm, 2 * HW), lambda i: (i, 0)), pl.BlockSpec((8, HW), lambda i: (0, 0)), mat, mat],
        out_shape=[SDS((T, 2 * HW), bf16), SDS((8, HW), f32), SDS((4, CH, CH), f32), SDS((4, CH, CH), f32)],
        compiler_params=_cp(("arbitrary",), 32),
    )(ua, va, d_a, lng, lnb, ws, bs_t)


def _outproj_fwd(mixed, w_out, h, par):
    tm = 1024

    def body(mix_ref, w_ref, h_ref, par_ref, ho_ref, po_ref):
        p = _nn(mix_ref[...], w_ref[...])
        po_ref[...] = p.astype(bf16)
        ho_ref[...] = h_ref[...] + par_ref[2:3, :] * p

    row = pl.BlockSpec((tm, D), lambda i: (i, 0))
    return _pallas_hbm(
        body, name="outproj_fwd", grid=(T // tm,),
        in_specs=[row, pl.BlockSpec((D, D), lambda i: (0, 0)), row, pl.BlockSpec((8, D), lambda i: (0, 0))],
        out_specs=[row, row], out_shape=[SDS((T, D), f32), SDS((T, D), bf16)],
        compiler_params=_cp(("arbitrary",), 48),
    )(mixed, w_out, h, par)


def _outproj_bwd(dh, po, w_out, par):
    tm = 1024

    def body(dh_ref, po_ref, w_ref, par_ref, do_ref, da_ref, db_ref, dpar_ref):
        @pl.when(pl.program_id(0) == 0)
        def _():
            dpar_ref[...] = jnp.zeros_like(dpar_ref)

        dh_v = dh_ref[...]
        dob = (par_ref[2:3, :] * dh_v).astype(bf16)
        do_ref[...] = dob
        dpar_ref[2:3, :] += _colsum(dh_v * po_ref[...].astype(f32))
        dm = _nt(dob, w_ref[...])
        da_ref[...] = dm[:, :HW]
        db_ref[...] = dm[:, HW:]

    row = pl.BlockSpec((tm, D), lambda i: (i, 0))
    half = pl.BlockSpec((tm, HW), lambda i: (i, 0))
    one = pl.BlockSpec((8, D), lambda i: (0, 0))
    return _pallas_hbm(
        body, name="outproj_bwd", grid=(T // tm,),
        in_specs=[row, row, pl.BlockSpec((D, D), lambda i: (0, 0)), one],
        out_specs=[row, half, half, one],
        out_shape=[SDS((T, D), bf16), SDS((T, HW), f32), SDS((T, HW), f32), SDS((8, D), f32)],
        compiler_params=_cp(("arbitrary",), 48),
    )(dh, po, w_out, par)


def _inproj_bwd_tok(dh, h, par, w_in_t, dsg, dq, dk, dv, tabs):
    tm = 512

    def body(dh_ref, h_ref, par_ref, w_ref, dsg_ref, dq_ref, dk_ref, dv_ref,
             cos_ref, slo_ref, shi_ref, dhin_ref, dp_ref, y_ref, dpar_ref):
        @pl.when(pl.program_id(0) == 0)
        def _():
            dpar_ref[...] = jnp.zeros_like(dpar_ref)

        cos, slo, shi = _all_heads(cos_ref), _all_heads(slo_ref), _all_heads(shi_ref)
        dp_ref[:, :2 * HW] = dsg_ref[...]
        dp_ref[:, 2 * HW:3 * HW] = _rope_t(dq_ref[...] * 0.125, cos, slo, shi).astype(bf16)
        dp_ref[:, 3 * HW:4 * HW] = _rope_t(dk_ref[...], cos, slo, shi).astype(bf16)
        dp_ref[:, 4 * HW:] = dv_ref[...].astype(bf16)
        dy = _nn(dp_ref[...], w_ref[...])
        par_v, h_v = par_ref[...], h_ref[...]
        y_ref[...] = _norm_mod(h_v, par_v)[0].astype(bf16)
        dx, dsh, dsc, dng = _norm_mod_bwd(dy, h_v, par_v)
        dhin_ref[...] = dh_ref[...] + dx
        dpar_ref[0:1, :] += dsh
        dpar_ref[1:2, :] += dsc
        dpar_ref[3:4, :] += dng

    row = pl.BlockSpec((tm, D), lambda i: (i, 0))
    half = pl.BlockSpec((tm, HW), lambda i: (i, 0))
    one = pl.BlockSpec((8, D), lambda i: (0, 0))
    return _pallas_hbm(
        body, name="inproj_bwd_tok", grid=(T // tm,),
        in_specs=[row, row, one, pl.BlockSpec((INW, D), lambda i: (0, 0)), row] + [half] * 3 +
                 [pl.BlockSpec((tm, CH), lambda i: (i, 0))] * 3,
        out_specs=[row, pl.BlockSpec((tm, INW), lambda i: (i, 0)), row, one],
        out_shape=[SDS((T, D), f32), SDS((T, INW), bf16), SDS((T, D), bf16), SDS((8, D), f32)],
        compiler_params=_cp(("arbitrary",), 56),
    )(dh, h, par, w_in_t, dsg, dq, dk, dv, *tabs)


def _loss_head(h, par, target):
    tm = 512

    def body(h_ref, par_ref, t_ref, dh_ref, acc_ref):
        @pl.when(pl.program_id(0) == 0)
        def _():
            acc_ref[...] = jnp.zeros_like(acc_ref)

        x, g = h_ref[...], par_ref[3:4, :]
        rstd = lax.rsqrt(_rowmean(x * x) + EPS)
        xhat = x * rstd
        err = xhat * g - t_ref[...]
        acc_ref[0:1, :] += _colsum(err * err)
        dy = err * (1.0 / D)
        acc_ref[3:4, :] += _colsum(dy * xhat)
        dxhat = dy * g
        dh_ref[...] = rstd * (dxhat - xhat * _rowmean(dxhat * xhat))

    row = pl.BlockSpec((tm, D), lambda i: (i, 0))
    one = pl.BlockSpec((8, D), lambda i: (0, 0))
    return _pallas_hbm(
        body, name="loss_head", grid=(T // tm,), in_specs=[row, one, row], out_specs=[row, one],
        out_shape=[SDS((T, D), f32), SDS((8, D), f32)], compiler_params=_cp(("arbitrary",), 32),
    )(h, par, target)


ELEMENTWISE_VMEM_BUDGET = 20 << 20


def _block_rows(rows, bytes_per_row):
    cap = ELEMENTWISE_VMEM_BUDGET // bytes_per_row
    if rows <= cap:
        return rows
    return next(b for b in range(cap - cap % 16, 0, -16) if rows % b == 0)


def _sum_slots(land, name):
    _, R, C = land.shape
    br = _block_rows(R, 2 * NDEV * C * land.dtype.itemsize + 2 * C * 4)

    def body(l_ref, o_ref):
        acc = l_ref[0].astype(f32)
        for s in range(1, NDEV):
            acc = acc + l_ref[s].astype(f32)
        o_ref[...] = acc

    return _pallas_hbm(
        body, name=name, grid=(R // br,), in_specs=[pl.BlockSpec((NDEV, br, C), lambda i: (0, i, 0))],
        out_specs=pl.BlockSpec((br, C), lambda i: (i, 0)), out_shape=SDS((R, C), f32),
        compiler_params=_cp(("arbitrary",), 32),
    )(land)


def _sum_partials(land, own, me, name):
    r = land.shape[-2]

    def body(me_ref, l_ref, own_ref, o_ref):
        mine = own_ref[...].astype(f32)
        acc = jnp.where(me_ref[0] == 0, mine, l_ref[0].astype(f32))
        for s in range(1, NDEV):
            acc = acc + jnp.where(me_ref[0] == s, mine, l_ref[s].astype(f32))
        o_ref[...] = acc

    if own.ndim == 3:
        kk = own.shape[0]
        specs = dict(grid=(kk,),
                     in_specs=[pl.BlockSpec((NDEV, None, r, D), lambda i, me_ref: (0, i, 0, 0)),
                               pl.BlockSpec((None, r, D), lambda i, me_ref: (i, me_ref[0], 0))],
                     out_specs=pl.BlockSpec((None, r, D), lambda i, me_ref: (i, 0, 0)))
        out_shape = SDS((kk, r, D), f32)
    else:
        specs = dict(grid=(1,),
                     in_specs=[pl.BlockSpec((NDEV, r, D), lambda i, me_ref: (0, 0, 0)),
                               pl.BlockSpec((r, D), lambda i, me_ref: (me_ref[0], 0))],
                     out_specs=pl.BlockSpec((r, D), lambda i, me_ref: (0, 0)))
        out_shape = SDS((r, D), f32)
    return _pallas_hbm(
        body, name=name, out_shape=out_shape,
        grid_spec=pltpu.PrefetchScalarGridSpec(num_scalar_prefetch=1, **specs),
        compiler_params=_cp(("arbitrary",), 32),
    )(me.reshape(1), land, own)


def _adamw(w, g, m, v, name):
    R, C = w.shape
    br = _block_rows(R, 2 * 7 * C * 4)

    def body(w_ref, g_ref, m_ref, v_ref, d_ref, mo_ref, vo_ref):
        gv = g_ref[...]
        m2 = B1 * m_ref[...] + (1.0 - B1) * gv
        v2 = B2 * v_ref[...] + (1.0 - B2) * (gv * gv)
        mo_ref[...] = m2
        vo_ref[...] = v2
        m_hat = m2 / (1.0 - B1 ** STEP)
        v_hat = v2 / (1.0 - B2 ** STEP)
        d_ref[...] = -LR * (m_hat / (jnp.sqrt(v_hat) + AEPS) + WD * w_ref[...])

    blk = pl.BlockSpec((br, C), lambda i: (i, 0))
    return _pallas_hbm(
        body, name=name, grid=(R // br,), in_specs=[blk] * 4, out_specs=[blk] * 3,
        out_shape=[SDS((R, C), f32)] * 3, compiler_params=_cp(("arbitrary",), 32),
    )(w, g, m, v)


def _adamw_nd(w, g, m, v, name, swapped=False):
    if swapped:
        outs = _adamw_nd(*(jnp.swapaxes(a, -1, -2) for a in (w, g, m, v)), name)
        return [jnp.swapaxes(o, -1, -2) for o in outs]
    shp = w.shape
    r2 = (-1, shp[-1]) if w.ndim > 1 else (8, shp[0] // 8)
    outs = _adamw(w.reshape(r2), g.reshape(r2), m.reshape(r2), v.reshape(r2), name)
    return [o.reshape(shp) for o in outs]


def _par_rows(mod_l, s, gain):
    rows = jnp.pad(mod_l.reshape(9, D)[3 * s:3 * s + 3], ((0, 5), (0, 0)))
    return rows + jnp.pad(gain[None, :], ((3, 4), (0, 0)))


def _pad_rows(a):
    a = a.reshape(-1, CH)
    return jnp.pad(a, ((0, (-a.shape[0]) % 8), (0, 0)))


def _prepare(me, c, ada_w, ada_b, norm_g):
    pay = jnp.pad(c, ((0, 7), (0, 0)))
    pay = jnp.concatenate([pay, jnp.pad(norm_g.reshape(6, OUTS), ((0, 2), (0, D - OUTS)))], axis=0)
    got = _all_gather_small(pay, "gather_c")
    c_all = got[:, 0, :]
    gains = got[:, 8:14, :OUTS].transpose(1, 0, 2).reshape(DEPTH, 3, D)

    ada_b_loc = lax.dynamic_slice(ada_b, (0, me * ADAS), (DEPTH, ADAS)).reshape(DEPTH, 1, ADAS)
    mod_cols = _mod_fwd(c_all, ada_w, ada_b_loc)
    got = _all_gather_small(mod_cols.reshape(DEPTH * NDEV, ADAS), "gather_mod").reshape(NDEV, DEPTH, NDEV, ADAS)
    mod = lax.dynamic_index_in_dim(got, me, axis=2, keepdims=False).transpose(1, 0, 2).reshape(DEPTH, 9 * D)
    pars = [[_par_rows(mod[l], s, gains[l, s]) for s in range(3)] for l in range(DEPTH)]
    return c_all, pars


def _fwd_bwd(x2, target, pars, get_w, put_g, small_ready, sgu_ln_g, sgu_ln_b, sgu_w, sgu_b, final_g):
    tabs = _rope_tables()
    tril = jnp.tril(jnp.ones((CH, CH), dtype=bool))
    behind = lambda rows, token: rows if token is None else rows + token[0, 0]

    h = x2
    saved = []
    for l in range(DEPTH):
        lng, lnb = sgu_ln_g[l].reshape(1, HW), sgu_ln_b[l].reshape(1, HW)
        bs_t = sgu_b[l].T
        h0 = h
        w, token = get_w(4 * l, ("ffn1", l), h0)
        h1, fo1, g1, u1, a1 = _ffn_fwd(h0, behind(pars[l][0], token), w)
        w, token = get_w(4 * l + 2, ("inproj", l), h1)
        ua, va, q, k, v = _inproj_fwd(h1, behind(pars[l][1], token), w, tabs)
        branches = [_attn_fwd(q, k, v, d) for d in PATTERN_DILATIONS]
        o, lse = _attn_combine([b[0] for b in branches], [b[1] for b in branches])
        mixed = _sgu_fwd(ua, va, o, lng, lnb, sgu_w[l], bs_t)
        w, token = get_w(4 * l + 3, ("outproj", l), mixed)
        h2, po = _outproj_fwd(mixed, w, h1, behind(pars[l][1], token))
        w, token = get_w(4 * l + 1, ("ffn2", l), h2)
        h3, fo2, g2, u2, a2 = _ffn_fwd(h2, behind(pars[l][2], token), w)
        saved.append((h0, h1, h2, fo1, g1, u1, a1, ua, va, q, k, v, o, mixed, lse, po, fo2, g2, u2, a2))
        h = h3

    par_f = jnp.pad(final_g[None, :], ((3, 4), (0, 0)))
    dh, head = _loss_head(h, par_f, target)

    dmods, dgains, dsgu = [None] * DEPTH, [None] * DEPTH, [None] * DEPTH
    token = None
    for l in reversed(range(DEPTH)):
        w_f1, w_f2, w_i, w_o = (get_w(4 * l + j, None, None)[0] for j in (0, 1, 2, 3))
        h0, h1, h2, fo1, g1, u1, a1, ua, va, q, k, v, o, mixed, lse, po, fo2, g2, u2, a2 = saved[l]
        lng, lnb = sgu_ln_g[l].reshape(1, HW), sgu_ln_b[l].reshape(1, HW)
        bs_t = sgu_b[l].T

        dh, dg_s, du_s, y, dfb, dpar3 = _ffn_bwd_tok(dh, h2, behind(pars[l][2], token), fo2, g2, u2, w_f2)
        token = put_g(4 * l + 1, _ffn_bwd_w(a2, dg_s, du_s, y, dfb))

        dob, d_a, d_b, dpar2g = _outproj_bwd(dh, po, w_o, behind(pars[l][1], token))
        token = put_g(4 * l + 3, _tn_matmul(mixed, dob, 512, "w_out_grad"))
        dqkv = None
        for d in PATTERN_DILATIONS:
            dqkv = _attn_bwd(q, k, v, o, d_b, lse, d, dqkv)
        dsg, dln, dws, dbl = _sgu_bwd(ua, va, d_a, lng, lnb, sgu_w[l], bs_t)
        dh, dp, y2, dpar2 = _inproj_bwd_tok(dh, h1, behind(pars[l][1], token), w_i, dsg, *dqkv, tabs)
        token = put_g(4 * l + 2, _tn_matmul(dp, y2, 640, "w_in_grad"))

        dh, dg_s, du_s, y, dfb, dpar1 = _ffn_bwd_tok(dh, h0, behind(pars[l][0], token), fo1, g1, u1, w_f1)
        dmods[l] = jnp.concatenate([dpar1[0:3], dpar2[0:2], dpar2g[2:3], dpar3[0:3]], axis=0).reshape(9 * D)
        dgains[l] = jnp.stack([dpar1[3], dpar2[3], dpar3[3]])
        dsgu[l] = (dln[0], dln[1], jnp.where(tril[None], dws, 0.0), jnp.sum(dbl, axis=-1))
        if l > 0:
            token = put_g(4 * l, _ffn_bwd_w(a1, dg_s, du_s, y, dfb))
            continue
        token = small_ready(head, dmods, dgains, dsgu)
        token = put_g(4 * DEPTH, _ffn_bwd_w(a1, dg_s, du_s, y, dfb, token, "down"))
        token = put_g(0, _ffn_bwd_w(a1, dg_s, du_s, y, dfb, token, "gate_up"))
    return dh, token


def kernel(x, c, ada_w, ada_b, norm_g, ffn1_wg, ffn1_wu, ffn1_wd, ffn2_wg, ffn2_wu, ffn2_wd, w_in, sgu_ln_g, sgu_ln_b, sgu_w, sgu_b, w_out, final_g, loss_target, m_ada_w, m_ada_b, m_norm_g, m_ffn1_wg, m_ffn1_wu, m_ffn1_wd, m_ffn2_wg, m_ffn2_wu, m_ffn2_wd, m_w_in, m_sgu_ln_g, m_sgu_ln_b, m_sgu_w, m_sgu_b, m_w_out, m_final_g, v_ada_w, v_ada_b, v_norm_g, v_ffn1_wg, v_ffn1_wu, v_ffn1_wd, v_ffn2_wg, v_ffn2_wu, v_ffn2_wd, v_w_in, v_sgu_ln_g, v_sgu_ln_b, v_sgu_w, v_sgu_b, v_w_out, v_final_g):
    me = 4 * lax.axis_index("x") + 2 * lax.axis_index("y") + lax.axis_index("c")

    tr = lambda w: jnp.swapaxes(w, -1, -2).astype(bf16)
    locs = []
    for l in range(DEPTH):
        locs.append(jnp.stack([tr(ffn1_wg[l]), tr(ffn1_wu[l]), ffn1_wd[l].astype(bf16)]))
        locs.append(jnp.stack([tr(ffn2_wg[l]), tr(ffn2_wu[l]), ffn2_wd[l].astype(bf16)]))
        locs.append(tr(w_in[l]))
        locs.append(w_out[l].astype(bf16))

    c_all, pars = _prepare(me, c, ada_w, ada_b, norm_g)
    locs, pars = lax.optimization_barrier((locs, pars))

    first, locs = lax.optimization_barrier((_all_gather_rows([locs[0]], "gather_first")[0], locs))
    placed = [_place_own(a, me) for a in locs[1:]]
    ready = {0: first}
    groups = ([2, 3], [1], [4], [6, 7], [5])
    plan = {("inproj", 0): (("wait", "gather", 0), ("start", "chips", 1)),
            ("outproj", 0): (("wait", "chips", 1), ("start", "pass", 1), ("start", "chips", 2)),
            ("ffn2", 0): (("wait", "pass", 1),),
            ("ffn1", 1): (("wait", "chips", 2), ("start", "pass", 2), ("start", "gather", 3), ("wait", "pass", 2)),
            ("inproj", 1): (("wait", "gather", 3), ("start", "chips", 4)),
            ("outproj", 1): (("wait", "chips", 4), ("start", "pass", 4)),
            ("ffn2", 1): (("wait", "pass", 4),)}
    flying = {}

    def exchange(action, mode, gi, after):
        if action == "start":
            arrays = flying.pop(gi) if mode == "pass" else [placed[p - 1] for p in groups[gi]]
            arrays, _ = lax.optimization_barrier((arrays, after))
            started = _exchange_start(mode, arrays, "%s_start_%d" % (mode, gi))
            flying[gi] = started[:-1]
            return started[-1]
        arrays = _exchange_wait(mode, flying.pop(gi), after, "%s_wait_%d" % (mode, gi))
        if mode == "chips":
            flying[gi] = arrays
        else:
            ready.update(zip(groups[gi], arrays))
        return None

    first_token = exchange("start", "gather", 0, first)

    def get_w(piece, point, after):
        token = first_token if point == ("ffn1", 0) else None
        for action, mode, gi in plan.get(point, ()):
            started = exchange(action, mode, gi, after)
            if started is not None:
                token = after = started
        return ready.get(piece), token

    sent = {}

    def put_g(piece, grad):
        land = lax.empty((NDEV,) + grad.shape[:-2] + (grad.shape[-2] // NDEV, D), bf16)
        started = _exchange_start("scatter", [grad, land], "scatter_start_%d" % piece)
        sent[piece] = started[:-1]
        return started[-1]

    small_sent = []

    def small_ready(head, dmods, dgains, dsgu):
        loss_part = 0.5 * jnp.sum(head[0]) / D
        small = jnp.concatenate([
            _pad_rows(jnp.stack(dmods)), _pad_rows(jnp.stack(dgains)),
            _pad_rows(jnp.stack([s[0] for s in dsgu])), _pad_rows(jnp.stack([s[1] for s in dsgu])),
            _pad_rows(jnp.stack([s[3] for s in dsgu])), _pad_rows(jnp.stack([s[2] for s in dsgu])),
            _pad_rows(head[3]), _pad_rows(jnp.pad(loss_part[None], (0, CH - 1)))], axis=0)
        started = _exchange_start("gather", [_place_own(small, me, "place_small")], "small_start")
        small_sent.append(started[:-1])
        return started[-1]

    dh, last = _fwd_bwd(x[0], loss_target[0], pars, get_w, put_g, small_ready, sgu_ln_g, sgu_ln_b, sgu_w, sgu_b, final_g)
    grad_x = dh[None]

    got = _exchange_wait("gather", small_sent[0], last, "small_wait")[0].reshape(NDEV, -1, CH)
    tot = _sum_slots(got, "sum_small")
    n_mod, n_gain, n_sw = DEPTH * 9 * D // CH, DEPTH * 3 * D // CH, DEPTH * 4 * CH
    offs = [0, n_mod, n_mod + n_gain, n_mod + n_gain + 8, n_mod + n_gain + 16, n_mod + n_gain + 24]
    g_ada_b = tot[offs[0]:offs[1]].reshape(DEPTH, 9 * D)
    g_gain_full = tot[offs[1]:offs[2]].reshape(DEPTH, 3, D)
    g_ln_g = tot[offs[2]:offs[3]].reshape(DEPTH, 4, CH)
    g_ln_b = tot[offs[3]:offs[4]].reshape(DEPTH, 4, CH)
    g_sb = tot[offs[4]:offs[5]].reshape(DEPTH, 4, CH)
    g_sw = tot[offs[5]:offs[5] + n_sw].reshape(DEPTH, 4, CH, CH)
    g_final = tot[offs[5] + n_sw:offs[5] + n_sw + 8].reshape(D)
    loss = tot[offs[5] + n_sw + 8, 0]
    g_norm = lax.dynamic_slice(g_gain_full, (0, 0, me * OUTS), (DEPTH, 3, OUTS))

    dmod_all = got[:, offs[0]:offs[1]].reshape(NDEV, DEPTH, 9 * D)
    dmod_cols = lax.dynamic_slice(dmod_all, (0, 0, me * ADAS), (NDEV, DEPTH, ADAS)).transpose(1, 0, 2)
    g_ada_w = _ada_grad((c_all * jax.nn.sigmoid(c_all)).T, dmod_cols)

    sums, after = {}, tot

    def collect(piece, after):
        own, land = _exchange_wait("scatter", sent[piece], after, "scatter_wait_%d" % piece)
        sums[piece] = _sum_partials(land, own, me, "sum_grads")
        return sums[piece]

    for piece in (5, 7, 6, 4, 1, 3, 2):
        after = collect(piece, after)
    back = lambda t: jnp.swapaxes(t, -1, -2)
    f2 = jnp.stack([sums[4 * l + 1] for l in range(DEPTH)])
    g_w_in = back(jnp.stack([sums[4 * l + 2] for l in range(DEPTH)]))
    g_w_out = jnp.stack([sums[4 * l + 3] for l in range(DEPTH)])

    ws = [ada_w, ada_b, norm_g, ffn1_wg, ffn1_wu, ffn1_wd, ffn2_wg, ffn2_wu, ffn2_wd, w_in, sgu_ln_g, sgu_ln_b, sgu_w,
          sgu_b, w_out, final_g]
    ms = [m_ada_w, m_ada_b, m_norm_g, m_ffn1_wg, m_ffn1_wu, m_ffn1_wd, m_ffn2_wg, m_ffn2_wu, m_ffn2_wd, m_w_in,
          m_sgu_ln_g, m_sgu_ln_b, m_sgu_w, m_sgu_b, m_w_out, m_final_g]
    vs = [v_ada_w, v_ada_b, v_norm_g, v_ffn1_wg, v_ffn1_wu, v_ffn1_wd, v_ffn2_wg, v_ffn2_wu, v_ffn2_wd, v_w_in,
          v_sgu_ln_g, v_sgu_ln_b, v_sgu_w, v_sgu_b, v_w_out, v_final_g]
    gw = [g_ada_w, g_ada_b, g_norm, None, None, None, back(f2[:, 0]), back(f2[:, 1]), f2[:, 2],
          g_w_in, g_ln_g, g_ln_b, g_sw, g_sb, g_w_out, g_final]
    upd = [None] * len(ws)
    lane_narrow = (3, 4, 6, 7, 9)
    for i in (1, 2, 10, 11, 12, 13, 15, 6, 7, 8, 9, 14, 0):
        upd[i] = _adamw_nd(ws[i], gw[i], ms[i], vs[i], "adamw", i in lane_narrow)
    collect(4 * DEPTH, upd[0][0])
    gw[5] = jnp.stack([sums[4 * DEPTH][0]] + [sums[4 * l][2] for l in range(1, DEPTH)])
    upd[5] = _adamw_nd(ws[5], gw[5], ms[5], vs[5], "adamw", False)
    collect(0, upd[5][0])
    for i in (3, 4):
        gw[i] = back(jnp.stack([sums[4 * l][i - 3] for l in range(DEPTH)]))
        upd[i] = _adamw_nd(ws[i], gw[i], ms[i], vs[i], "adamw", i in lane_narrow)
    return (loss, grad_x, *gw, *[u[0] for u in upd], *[u[1] for u in upd], *[u[2] for u in upd])
```

```python
import math

import jax
import jax.numpy as jnp
from jax import lax
from jax.experimental import pallas as pl
from jax.experimental.pallas import tpu as pltpu

f32, bf16 = jnp.float32, jnp.bfloat16
SDS = jax.ShapeDtypeStruct

T, D, F = 4096, 1024, 2816
NDEV, DEPTH = 8, 2
HW = 512
INW = 5 * HW
FS, INS, OUTS, ADAS = F // NDEV, INW // NDEV, D // NDEV, 9 * D // NDEV
CH = 128
PATTERN_DILATIONS = (1, 4, 16)
ROPE_THETA = 10000.0
EPS = 1e-6
LR, B1, B2, AEPS, WD, STEP = 0.001, 0.9, 0.999, 1e-08, 0.01, 10
MESH = pl.DeviceIdType.MESH


def _cp(sems, vmem_mb):
    return pltpu.CompilerParams(dimension_semantics=sems, vmem_limit_bytes=vmem_mb << 20)


def _pallas_hbm(body, *, out_shape, **kw):
    typed = jax.tree.map(lambda s: pltpu.HBM(s.shape, s.dtype), out_shape)

    def run(*operands):
        pin = lambda x: x if x.dtype == jnp.int32 else pltpu.with_memory_space_constraint(x, pltpu.HBM)
        fn, specs, n = body, kw, len(operands)
        if _PENDING:
            operands = operands + (_PENDING.pop(),)
            specs = dict(kw, in_specs=list(kw["in_specs"]) + [pl.BlockSpec((8, CH), lambda *idx: (0, 0))])
            fn = lambda *refs: body(*refs[:n], *refs[n + 1:])
        return pl.pallas_call(fn, out_shape=typed, **specs)(*[pin(x) for x in operands])

    return run


_PENDING = []


def _behind(rows, token):
    if token is not None:
        _PENDING.append(token)
    return rows


def _nn(a, b):
    return lax.dot_general(a, b, (((1,), (0,)), ((), ())), preferred_element_type=f32)


def _nt(a, b):
    return lax.dot_general(a, b, (((1,), (1,)), ((), ())), preferred_element_type=f32)


def _tn(a, b):
    return lax.dot_general(a, b, (((0,), (0,)), ((), ())), preferred_element_type=f32)


def _colsum(a):
    return jnp.sum(a, axis=0, keepdims=True)


def _rowmean(a):
    return jnp.mean(a, axis=-1, keepdims=True)


def _norm_mod(x, par):
    rstd = lax.rsqrt(_rowmean(x * x) + EPS)
    xhat = x * rstd
    n = xhat * par[3:4, :]
    y = n * (1.0 + par[1:2, :]) + par[0:1, :]
    return y, n, xhat, rstd


def _norm_mod_bwd(dy, x, par):
    _, n, xhat, rstd = _norm_mod(x, par)
    dn = dy * (1.0 + par[1:2, :])
    dxhat = dn * par[3:4, :]
    dx = rstd * (dxhat - xhat * _rowmean(dxhat * xhat))
    return dx, _colsum(dy), _colsum(dy * n), _colsum(dn * xhat)


_GK = math.sqrt(2.0 / math.pi)


def _gelu(x):
    return 0.5 * x * (1.0 + jnp.tanh(_GK * (x + 0.044715 * x * x * x)))


def _gelu_grad(x):
    t = jnp.tanh(_GK * (x + 0.044715 * x * x * x))
    return 0.5 * (1.0 + t) + 0.5 * x * (1.0 - t * t) * (_GK * (1.0 + 3.0 * 0.044715 * x * x))


def _rows(ref, idx, r):
    if len(ref.shape) == 3:
        return ref.at[:, pl.ds(idx * r, r), :]
    return ref.at[pl.ds(idx * r, r), :]


def _flip(v, bit):
    return 1 - v if bit else v


def _all_gather_small(x, name):
    R, C = x.shape

    def body(x_ref, out_ref, send_sems, recv_sems):
        mx, my, mc = lax.axis_index("x"), lax.axis_index("y"), lax.axis_index("c")
        me = 4 * mx + 2 * my + mc
        out_ref[me] = x_ref[...]
        sent = []
        for k in range(1, NDEV):
            peer = (_flip(mx, k & 4), _flip(my, k & 2), _flip(mc, k & 1))
            cp = pltpu.make_async_remote_copy(
                src_ref=x_ref, dst_ref=out_ref.at[me], send_sem=send_sems.at[k - 1],
                recv_sem=recv_sems.at[k - 1], device_id=peer, device_id_type=MESH)
            cp.start()
            sent.append(cp)
        for k in range(1, NDEV):
            peer = (_flip(mx, k & 4), _flip(my, k & 2), _flip(mc, k & 1))
            pidx = 4 * peer[0] + 2 * peer[1] + peer[2]
            pltpu.make_async_remote_copy(
                src_ref=x_ref, dst_ref=out_ref.at[pidx], send_sem=send_sems.at[k - 1],
                recv_sem=recv_sems.at[k - 1], device_id=peer, device_id_type=MESH).wait_recv()
        for cp in sent:
            cp.wait_send()

    vm = pl.BlockSpec(memory_space=pltpu.VMEM)
    return pl.pallas_call(
        body, name=name, out_shape=SDS((NDEV, R, C), f32), in_specs=[vm], out_specs=vm,
        scratch_shapes=[pltpu.SemaphoreType.DMA((NDEV - 1,)), pltpu.SemaphoreType.DMA((NDEV - 1,))],
        compiler_params=pltpu.CompilerParams(vmem_limit_bytes=32 << 20),
    )(x)


def _all_gather_rows(locs, name):
    n = len(locs)
    rs = [a.shape[-2] for a in locs]

    def body(*refs):
        src, out = refs[:n], refs[n:2 * n]
        send_sems, recv_sems, loc_sems = refs[2 * n:]
        mx, my, mc = lax.axis_index("x"), lax.axis_index("y"), lax.axis_index("c")
        me, sib = (mx, my, mc), (mx, my, 1 - mc)
        chips = [(1 - mx, my), (mx, 1 - my), (1 - mx, 1 - my)]

        def blk(a, p):
            return _rows(out[a], 4 * p[0] + 2 * p[1] + p[2], rs[a])

        def copy(k, a, block, to, from_src=False):
            return pltpu.make_async_remote_copy(
                src_ref=src[a] if from_src else blk(a, block), dst_ref=blk(a, block),
                send_sem=send_sems.at[k * n + a], recv_sem=recv_sems.at[k * n + a],
                device_id=to, device_id_type=MESH)

        mine = [pltpu.make_async_copy(src[a], blk(a, me), loc_sems.at[a]) for a in range(n)]
        for m in mine:
            m.start()
        first = []
        for j, chip in enumerate(chips):
            first += [copy(1 + j, a, me, (*chip, mc), True) for a in range(n)]
        first += [copy(0, a, me, sib, True) for a in range(n)]
        for cp in first:
            cp.start()
        passed = []
        for j, chip in enumerate(chips):
            for a in range(n):
                copy(1 + j, a, (*chip, mc), me).wait_recv()
            fwd = [copy(4 + j, a, (*chip, mc), sib) for a in range(n)]
            for cp in fwd:
                cp.start()
            passed += fwd
        for a in range(n):
            copy(0, a, sib, me).wait_recv()
        for j, chip in enumerate(chips):
            for a in range(n):
                copy(4 + j, a, (*chip, 1 - mc), me).wait_recv()
        for cp in first + passed:
            cp.wait_send()
        for m in mine:
            m.wait()

    hbm = pl.BlockSpec(memory_space=pl.ANY)
    out_shape = [SDS(a.shape[:-2] + (NDEV * a.shape[-2], a.shape[-1]), a.dtype) for a in locs]
    return pl.pallas_call(
        body, name=name, out_shape=out_shape, in_specs=[hbm] * n, out_specs=[hbm] * n,
        scratch_shapes=[pltpu.SemaphoreType.DMA((7 * n,)), pltpu.SemaphoreType.DMA((7 * n,)),
                        pltpu.SemaphoreType.DMA((n,))],
    )(*locs)


HBM_SPEC = pl.BlockSpec(memory_space=pltpu.HBM)
SEM_SPEC = pl.BlockSpec(memory_space=pltpu.SEMAPHORE)
DATAFLOW_EFFECT = pltpu.SideEffectType.DATAFLOW_SIDE_EFFECTING


def _place_own(loc, me, name="place_own"):
    r, cols = loc.shape[-2:]
    loc3 = loc.reshape(-1, r, cols)
    kk = loc3.shape[0]

    def body(me_ref, src_ref, full_ref, out_ref):
        out_ref[...] = src_ref[...]

    out = _pallas_hbm(
        body, name=name, out_shape=SDS((kk, NDEV * r, cols), loc.dtype),
        grid_spec=pltpu.PrefetchScalarGridSpec(
            num_scalar_prefetch=1, grid=(kk,),
            in_specs=[pl.BlockSpec((None, r, cols), lambda i, me_ref: (i, 0, 0)), pl.BlockSpec(memory_space=pl.ANY)],
            out_specs=pl.BlockSpec((None, r, cols), lambda i, me_ref: (i, me_ref[0], 0))),
        input_output_aliases={2: 0}, compiler_params=_cp(("arbitrary",), 32),
    )(me.reshape(1), loc3, lax.empty((kk, NDEV * r, cols), loc.dtype))
    return out.reshape(loc.shape[:-2] + (NDEV * r, cols))


EXCHANGE_PEERS = {"gather": (1, 2, 3, 4, 5, 6, 7), "scatter": (1, 2, 3, 4, 5, 6, 7), "chips": (1, 2, 4, 6), "pass": (2, 4, 6)}


def _exchange_copies(mode, bufs, n, send_sems, recv_sems):
    mx, my, mc = lax.axis_index("x"), lax.axis_index("y"), lax.axis_index("c")
    me = 4 * mx + 2 * my + mc
    out = []
    for slot, k in enumerate(EXCHANGE_PEERS[mode]):
        peer = (_flip(mx, k & 4), _flip(my, k & 2), _flip(mc, k & 1))
        pidx = 4 * peer[0] + 2 * peer[1] + peer[2]
        for a in range(n):
            r = bufs[a].shape[-2] // NDEV
            if mode == "scatter":
                src, dst, arrive = _rows(bufs[a], pidx, r), bufs[n + a].at[me], bufs[n + a].at[pidx]
            elif mode == "pass":
                peer = (mx, my, 1 - mc)
                src, dst, arrive = _rows(bufs[a], pidx, r), _rows(bufs[a], pidx, r), _rows(bufs[a], pidx + 1 - 2 * mc, r)
            else:
                src, dst, arrive = _rows(bufs[a], me, r), _rows(bufs[a], me, r), _rows(bufs[a], pidx, r)
            sems = dict(send_sem=send_sems.at[slot * n + a], recv_sem=recv_sems.at[slot * n + a],
                        device_id=peer, device_id_type=MESH)
            out.append((pltpu.make_async_remote_copy(src_ref=src, dst_ref=dst, **sems),
                        pltpu.make_async_remote_copy(src_ref=src, dst_ref=arrive, **sems)))
    return out


def _exchange_start(mode, arrays, name):
    m = len(arrays)
    n = m // 2 if mode == "scatter" else m

    def body(*refs):
        send_sems, recv_sems, token = refs[m], refs[m + 1], refs[-1]
        for go, _ in _exchange_copies(mode, refs[:m], n, send_sems, recv_sems):
            go.start()
        token[...] = jnp.zeros_like(token)

    sems = pltpu.SemaphoreType.DMA((len(EXCHANGE_PEERS[mode]) * n,))
    return pl.pallas_call(
        body, name=name, in_specs=[HBM_SPEC] * m,
        out_shape=(sems, sems, *[pltpu.HBM(a.shape, a.dtype) for a in arrays], SDS((8, CH), f32)),
        out_specs=(SEM_SPEC, SEM_SPEC, *[HBM_SPEC] * m, pl.BlockSpec(memory_space=pltpu.VMEM)),
        input_output_aliases={a: 2 + a for a in range(m)},
        compiler_params=pltpu.CompilerParams(has_side_effects=DATAFLOW_EFFECT),
    )(*[pltpu.with_memory_space_constraint(a, pltpu.HBM) for a in arrays])


def _exchange_wait(mode, started, after, name):
    send_sems, recv_sems, *arrays = started
    m = len(arrays)
    n = m // 2 if mode == "scatter" else m

    def body(*refs):
        for go, arrive in _exchange_copies(mode, refs[:m], n, refs[m], refs[m + 1]):
            go.wait_send()
            arrive.wait_recv()

    return pl.pallas_call(
        body, name=name, in_specs=[HBM_SPEC] * m + [SEM_SPEC, SEM_SPEC, pl.BlockSpec(memory_space=pl.ANY)],
        out_shape=[pltpu.HBM(a.shape, a.dtype) for a in arrays], out_specs=[HBM_SPEC] * m,
        input_output_aliases={a: a for a in range(m)},
        compiler_params=pltpu.CompilerParams(has_side_effects=DATAFLOW_EFFECT),
    )(*arrays, send_sems, recv_sems, after)


def _mod_fwd(c_all, ada_w, ada_b_loc):
    def body(c_ref, w_ref, b_ref, o_ref):
        ca = c_ref[...]
        ca = ca * jax.nn.sigmoid(ca)
        o_ref[...] = jnp.dot(ca, w_ref[...], precision=lax.Precision.HIGHEST,
                             preferred_element_type=f32) + b_ref[...]

    return _pallas_hbm(
        body, name="mod_fwd", grid=(DEPTH,), out_shape=SDS((DEPTH, NDEV, ADAS), f32),
        in_specs=[pl.BlockSpec((NDEV, D), lambda l: (0, 0)),
                  pl.BlockSpec((None, D, ADAS), lambda l: (l, 0, 0)),
                  pl.BlockSpec((None, 1, ADAS), lambda l: (l, 0, 0))],
        out_specs=pl.BlockSpec((None, NDEV, ADAS), lambda l: (l, 0, 0)),
        compiler_params=_cp(("arbitrary",), 32),
    )(c_all, ada_w, ada_b_loc)


def _ada_grad(cact_t, dmod_cols):
    def body(c_ref, d_ref, o_ref):
        acc = c_ref[:, 0:1] * d_ref[0:1, :]
        for b in range(1, NDEV):
            acc = acc + c_ref[:, b:b + 1] * d_ref[b:b + 1, :]
        o_ref[...] = acc

    tr = 256
    return _pallas_hbm(
        body, name="ada_grad", grid=(DEPTH, D // tr), out_shape=SDS((DEPTH, D, ADAS), f32),
        in_specs=[pl.BlockSpec((tr, NDEV), lambda l, i: (i, 0)),
                  pl.BlockSpec((None, NDEV, ADAS), lambda l, i: (l, 0, 0))],
        out_specs=pl.BlockSpec((None, tr, ADAS), lambda l, i: (l, i, 0)),
        compiler_params=_cp(("arbitrary", "arbitrary"), 32),
    )(cact_t, dmod_cols)


def _ffn_fwd(h, par, w3):
    tm, tf = 1024, 256
    nj = F // tf

    def body(h_ref, par_ref, wg_ref, wu_ref, wd_ref, ho_ref, fo_ref, g_ref, u_ref, a_ref, y_scr, acc):
        j = pl.program_id(1)

        @pl.when(j == 0)
        def _():
            y_scr[...] = _norm_mod(h_ref[...], par_ref[...])[0].astype(bf16)
            acc[...] = jnp.zeros_like(acc)

        y = y_scr[...]
        g = _nt(y, wg_ref[...])
        u = _nt(y, wu_ref[...])
        a = ((g * jax.nn.sigmoid(g)) * u).astype(bf16)
        g_ref[...] = g.astype(bf16)
        u_ref[...] = u.astype(bf16)
        a_ref[...] = a
        acc[...] += _nn(a, wd_ref[...])

        @pl.when(j == nj - 1)
        def _():
            fo_ref[...] = acc[...].astype(bf16)
            ho_ref[...] = h_ref[...] + (0.5 * par_ref[2:3, :]) * acc[...]

    row = pl.BlockSpec((tm, D), lambda i, j: (i, 0))
    hid = pl.BlockSpec((tm, tf), lambda i, j: (i, j))
    wspec = [pl.BlockSpec((None, tf, D), lambda i, j, k=k: (k, j, 0)) for k in range(3)]
    return _pallas_hbm(
        body, name="ffn_fwd", grid=(T // tm, nj),
        in_specs=[row, pl.BlockSpec((8, D), lambda i, j: (0, 0))] + wspec,
        out_specs=[row, row, hid, hid, hid],
        out_shape=[SDS((T, D), f32), SDS((T, D), bf16), SDS((T, F), bf16), SDS((T, F), bf16), SDS((T, F), bf16)],
        scratch_shapes=[pltpu.VMEM((tm, D), bf16), pltpu.VMEM((tm, D), f32)],
        compiler_params=_cp(("arbitrary", "arbitrary"), 52),
    )(h, par, w3, w3, w3)


def _ffn_bwd_tok(dh, h, par, fo, gs, us, w3):
    tm, tf = 1024, 256
    nj = F // tf

    def body(dh_ref, h_ref, par_ref, fo_ref, g_ref, u_ref, wg_ref, wu_ref, wd_ref, wdn_ref,
             dhin_ref, dg_ref, du_ref, y_ref, dfb_ref, dpar_ref, df_scr, dyacc, da_scr):
        i, j = pl.program_id(0), pl.program_id(1)

        @pl.when(jnp.logical_and(i == 0, j == 0))
        def _():
            dpar_ref[...] = jnp.zeros_like(dpar_ref)

        @pl.when(j == 0)
        def _():
            dh_v, par_v = dh_ref[...], par_ref[...]
            dfb = ((0.5 * par_v[2:3, :]) * dh_v).astype(bf16)
            df_scr[...] = dfb
            dfb_ref[...] = dfb
            dpar_ref[2:3, :] += 0.5 * _colsum(dh_v * fo_ref[...].astype(f32))
            y_ref[...] = _norm_mod(h_ref[...], par_v)[0].astype(bf16)
            dyacc[...] = jnp.zeros_like(dyacc)
            da_scr[...] = _nt(dfb, wd_ref[...])

        for half in range(2):
            rs = slice(half * (tm // 2), (half + 1) * (tm // 2))
            da = da_scr[rs, :]
            da_scr[rs, :] = _nt(df_scr[rs, :], wdn_ref[...])
            g = g_ref[rs, :].astype(f32)
            u = u_ref[rs, :].astype(f32)
            sig = jax.nn.sigmoid(g)
            du = (da * (g * sig)).astype(bf16)
            dg = (da * u * (sig * (1.0 + g * (1.0 - sig)))).astype(bf16)
            dg_ref[rs, :] = dg
            du_ref[rs, :] = du
            dyacc[rs, :] += _nn(dg, wg_ref[...]) + _nn(du, wu_ref[...])

        @pl.when(j == nj - 1)
        def _():
            dx, dsh, dsc, dng = _norm_mod_bwd(dyacc[...], h_ref[...], par_ref[...])
            dhin_ref[...] = dh_ref[...] + dx
            dpar_ref[0:1, :] += dsh
            dpar_ref[1:2, :] += dsc
            dpar_ref[3:4, :] += dng

    row = pl.BlockSpec((tm, D), lambda i, j: (i, 0))
    hid = pl.BlockSpec((tm, tf), lambda i, j: (i, j))
    one = pl.BlockSpec((8, D), lambda i, j: (0, 0))
    wspec = [pl.BlockSpec((None, tf, D), lambda i, j, k=k: (k, j, 0)) for k in range(3)]
    wspec.append(pl.BlockSpec((None, tf, D), lambda i, j: (2, jnp.minimum(j + 1, nj - 1), 0)))
    return _pallas_hbm(
        body, name="ffn_bwd_tok", grid=(T // tm, nj),
        in_specs=[row, row, one, row, hid, hid] + wspec,
        out_specs=[row, hid, hid, row, row, one],
        out_shape=[SDS((T, D), f32), SDS((T, F), bf16), SDS((T, F), bf16), SDS((T, D), bf16),
                   SDS((T, D), bf16), SDS((8, D), f32)],
        scratch_shapes=[pltpu.VMEM((tm, D), bf16), pltpu.VMEM((tm, D), f32), pltpu.VMEM((tm, tf), f32)],
        compiler_params=_cp(("arbitrary", "arbitrary"), 60),
    )(dh, h, par, fo, gs, us, w3, w3, w3, w3)


def _ffn_bwd_w(a_s, dg_s, du_s, y, dfb, anchor=None, part="all"):
    tf, tk = F // 2, 512
    nk = T // tk
    hid = pl.BlockSpec((tk, tf), lambda j, kk: (kk, j))
    row = pl.BlockSpec((tk, D), lambda j, kk: (kk, 0))
    pairs = {"all": ((1, 3), (2, 3), (0, 4)), "gate_up": ((1, 3), (2, 3)), "down": ((0, 4),)}[part]
    ops, specs = (a_s, dg_s, du_s, y, dfb), (hid, hid, hid, row, row)
    used = sorted({i for pr in pairs for i in pr})
    pairs = [(used.index(i), used.index(j)) for i, j in pairs]
    n = len(pairs)

    def body(*refs):
        ins, o_ref, accs = refs[:len(used)], refs[-n - 1], refs[-n:]
        kk = pl.program_id(1)

        @pl.when(kk == 0)
        def _():
            for acc in accs:
                acc[...] = jnp.zeros_like(acc)

        for acc, (i, j) in zip(accs, pairs):
            acc[...] += _tn(ins[i][...], ins[j][...])

        @pl.when(kk == nk - 1)
        def _():
            for t, acc in enumerate(accs):
                o_ref[t] = acc[...].astype(bf16)

    extra = [] if anchor is None else [anchor]
    return _pallas_hbm(
        body, name="ffn_bwd_w" if part == "all" else "ffn_bwd_w_" + part, grid=(F // tf, nk),
        in_specs=[specs[i] for i in used] + [pl.BlockSpec((8, CH), lambda j, kk: (0, 0))] * len(extra),
        out_specs=pl.BlockSpec((n, tf, D), lambda j, kk: (0, j, 0)), out_shape=SDS((n, F, D), bf16),
        scratch_shapes=[pltpu.VMEM((tf, D), f32)] * n,
        compiler_params=_cp(("arbitrary", "arbitrary"), 58),
    )(*[ops[i] for i in used], *extra)


def _tn_matmul(a, b, bm, name):
    M, N = a.shape[1], b.shape[1]
    tk = 1024
    nk = T // tk

    def body(a_ref, b_ref, o_ref, acc):
        kk = pl.program_id(1)

        @pl.when(kk == 0)
        def _():
            acc[...] = jnp.zeros_like(acc)

        acc[...] += _tn(a_ref[...], b_ref[...])

        @pl.when(kk == nk - 1)
        def _():
            o_ref[...] = acc[...].astype(bf16)

    return _pallas_hbm(
        body, name=name, grid=(M // bm, nk),
        in_specs=[pl.BlockSpec((tk, bm), lambda i, kk: (kk, i)), pl.BlockSpec((tk, N), lambda i, kk: (kk, 0))],
        out_specs=pl.BlockSpec((bm, N), lambda i, kk: (i, 0)), out_shape=SDS((M, N), bf16),
        scratch_shapes=[pltpu.VMEM((bm, N), f32)],
        compiler_params=_cp(("arbitrary", "arbitrary"), 40),
    )(a, b)


def _rope_tables():
    inv = ROPE_THETA ** (-jnp.arange(0, 64, 2, dtype=f32) / 64)
    ang = jnp.arange(T, dtype=f32)[:, None] * inv[None, :]
    ang = jnp.concatenate([ang, ang], axis=-1)
    cos, sin = jnp.tile(jnp.cos(ang), (1, 2)), jnp.tile(jnp.sin(ang), (1, 2))
    low = (jnp.arange(CH) % 64 < 32)[None, :]
    return cos, jnp.where(low, -sin, 0.0), jnp.where(low, 0.0, sin)


def _all_heads(tab_ref):
    return jnp.concatenate([tab_ref[...]] * 4, axis=1)


def _rope(t, cos, sin_lo, sin_hi):
    return t * cos + pltpu.roll(t, HW - 32, 1) * sin_lo + pltpu.roll(t, 32, 1) * sin_hi


def _rope_t(g, cos, sin_lo, sin_hi):
    return g * cos + pltpu.roll(g * sin_lo, 32, 1) + pltpu.roll(g * sin_hi, HW - 32, 1)


def _inproj_fwd(h, par, w_in_t, tabs):
    tm = 512

    def body(h_ref, par_ref, w_ref, cos_ref, slo_ref, shi_ref, ua_ref, va_ref, q_ref, k_ref, v_ref):
        y = _norm_mod(h_ref[...], par_ref[...])[0].astype(bf16)
        proj = lambda c: _nt(y, w_ref[c * HW:(c + 1) * HW, :])
        ua_ref[...] = proj(0)
        va_ref[...] = proj(1)
        cos, slo, shi = _all_heads(cos_ref), _all_heads(slo_ref), _all_heads(shi_ref)
        q_ref[...] = _rope(proj(2), cos, slo, shi) * 0.125
        k_ref[...] = _rope(proj(3), cos, slo, shi)
        v_ref[...] = proj(4)

    row = pl.BlockSpec((tm, D), lambda i: (i, 0))
    half = pl.BlockSpec((tm, HW), lambda i: (i, 0))
    return _pallas_hbm(
        body, name="inproj_fwd", grid=(T // tm,),
        in_specs=[row, pl.BlockSpec((8, D), lambda i: (0, 0)), pl.BlockSpec((INW, D), lambda i: (0, 0))] +
                 [pl.BlockSpec((tm, CH), lambda i: (i, 0))] * 3,
        out_specs=[half] * 5,
        out_shape=[SDS((T, HW), f32)] * 5,
        compiler_params=_cp(("arbitrary",), 48),
    )(h, par, w_in_t, *tabs)


def _head_masks():
    lane = lax.broadcasted_iota(jnp.int32, (1, CH), 1)
    return lane < 64, lane >= 64


def _lane(t, idx):
    return jnp.sum(jnp.where(lax.broadcasted_iota(jnp.int32, t.shape, 1) == idx, t, 0.0), axis=-1, keepdims=True)


def _stack_heads(x):
    lo, hi = _head_masks()
    zero = jnp.zeros_like(x)
    return jnp.concatenate([jnp.where(lo, x, zero), jnp.where(hi, x, zero)], axis=0)


def _band_mask(has_prev):
    row = lax.broadcasted_iota(jnp.int32, (2 * CH, 2 * CH), 0) & (CH - 1)
    col = lax.broadcasted_iota(jnp.int32, (2 * CH, 2 * CH), 1)
    in_prev = jnp.logical_and(jnp.logical_and(col < CH, col >= row), has_prev)
    return jnp.logical_or(in_prev, jnp.logical_and(col >= CH, col - CH <= row))


def _attn_tiling(d):
    return CH * 16, 16 // d, CH


def _attn_fwd(q, k, v, d):
    rows, blocks, lanes = _attn_tiling(d)
    pairs = lanes // CH

    def body(q_ref, kc_ref, kp_ref, vc_ref, vp_ref, o_ref, lse_ref):
        c, lb = pl.program_id(0), pl.program_id(1)
        col = lax.broadcasted_iota(jnp.int32, (CH, CH), 1)
        lo, _ = _head_masks()

        @pl.when(lb == 0)
        def _():
            lse_ref[...] = jnp.zeros_like(lse_ref)

        at = lambda r, b: pl.ds(r + b * CH * d, CH, stride=d) if d > 1 else pl.ds(b * CH, CH)
        for r in range(d):
            for b in range(blocks):
                own = at(r, b)
                k_prev, v_prev, before, mask = ((kc_ref, vc_ref, at(r, b - 1), _band_mask(True)) if b > 0 else
                                                (kp_ref, vp_ref, at(r, blocks - 1), _band_mask(c > 0)))
                lse_tile = lse_ref[own, :]
                for pi in range(pairs):
                    sl = slice(pi * CH, (pi + 1) * CH)
                    hp = lb * pairs + pi
                    kk = jnp.concatenate([k_prev[before, sl], kc_ref[own, sl]], axis=0).astype(bf16)
                    vv = jnp.concatenate([v_prev[before, sl], vc_ref[own, sl]], axis=0).astype(bf16)
                    s = jnp.where(mask, _nt(_stack_heads(q_ref[own, sl].astype(bf16)), kk), -jnp.inf)
                    m = jnp.max(s, axis=-1, keepdims=True)
                    p = jnp.exp(s - m)
                    den = jnp.sum(p, axis=-1, keepdims=True)
                    o = _nn(p.astype(bf16), vv) / den
                    o_ref[own, sl] = jnp.where(lo, o[:CH], o[CH:])
                    lse = m + jnp.log(den)
                    lse_tile = jnp.where(col == 2 * hp, lse[:CH], jnp.where(col == 2 * hp + 1, lse[CH:], lse_tile))
                lse_ref[own, :] = lse_tile

    cur = pl.BlockSpec((rows, lanes), lambda c, lb: (c, lb))
    prev = pl.BlockSpec((rows, lanes), lambda c, lb: (jnp.maximum(c - 1, 0), lb))
    return _pallas_hbm(
        body, name="attn_fwd_d%d" % d, grid=(T // rows, HW // lanes), in_specs=[cur, cur, prev, cur, prev],
        out_specs=[cur, pl.BlockSpec((rows, CH), lambda c, lb: (c, 0))],
        out_shape=[SDS((T, HW), f32), SDS((T, CH), f32)],
        compiler_params=_cp(("arbitrary", "arbitrary"), 40),
    )(q, k, k, v, v)


def _attn_combine(os_, lses):
    tm = 512

    def body(o0_ref, o1_ref, o2_ref, l0_ref, l1_ref, l2_ref, o_ref, lse_ref):
        l0, l1, l2 = l0_ref[...], l1_ref[...], l2_ref[...]
        m = jnp.maximum(jnp.maximum(l0, l1), l2)
        e = [jnp.exp(l0 - m), jnp.exp(l1 - m), jnp.exp(l2 - m)]
        s = e[0] + e[1] + e[2]
        w = [ei / s for ei in e]
        lse_ref[...] = m + jnp.log(s)
        lo, _ = _head_masks()
        for hp in range(4):
            sl = slice(hp * CH, (hp + 1) * CH)
            acc = jnp.zeros((tm, CH), f32)
            for wp, op_ref in zip(w, (o0_ref, o1_ref, o2_ref)):
                wexp = jnp.where(lo, wp[:, 2 * hp:2 * hp + 1], wp[:, 2 * hp + 1:2 * hp + 2])
                acc = acc + wexp * op_ref[:, sl].astype(f32)
            o_ref[:, sl] = acc

    half = pl.BlockSpec((tm, HW), lambda i: (i, 0))
    stat = pl.BlockSpec((tm, CH), lambda i: (i, 0))
    return _pallas_hbm(
        body, name="attn_combine", grid=(T // tm,), in_specs=[half] * 3 + [stat] * 3,
        out_specs=[half, stat], out_shape=[SDS((T, HW), f32), SDS((T, CH), f32)],
        compiler_params=_cp(("arbitrary",), 32),
    )(*os_, *lses)


def _attn_bwd(q, k, v, o, do, lse, d, running=None):
    rows, blocks, lanes = _attn_tiling(d)
    pairs = lanes // CH
    steps = T // rows

    def body(qc_ref, qn_ref, kc_ref, kp_ref, vc_ref, vp_ref, oc_ref, on_ref, dc_ref, dn_ref, lc_ref, ln_ref, *rest):
        dq_ref, dk_ref, dv_ref = rest[-3:]
        sofar = (lambda i, rows, sl: rest[i][rows, sl]) if running is not None else (lambda i, rows, sl: 0.0)
        c, lb = pl.program_id(0), pl.program_id(1)
        lo, _ = _head_masks()
        at = lambda r, b: pl.ds(r + b * CH * d, CH, stride=d) if d > 1 else pl.ds(b * CH, CH)
        for r in range(d):
            for b in range(blocks):
                own = at(r, b)
                k_prev, v_prev, before, mask_c = ((kc_ref, vc_ref, at(r, b - 1), _band_mask(True)) if b > 0 else
                                                  (kp_ref, vp_ref, at(r, blocks - 1), _band_mask(c > 0)))
                q_next, o_next, d_next, l_next, after, has_next = (
                    (qc_ref, oc_ref, dc_ref, lc_ref, at(r, b + 1), True) if b + 1 < blocks else
                    (qn_ref, on_ref, dn_ref, ln_ref, at(r, 0), c < steps - 1))
                mask_n = _band_mask(has_next)[:, :CH]
                lse_c, lse_n = lc_ref[own, :], l_next[after, :]
                for pi in range(pairs):
                    sl = slice(pi * CH, (pi + 1) * CH)
                    hp = lb * pairs + pi
                    kc, vc = kc_ref[own, sl].astype(bf16), vc_ref[own, sl].astype(bf16)
                    kk = jnp.concatenate([k_prev[before, sl].astype(bf16), kc], axis=0)
                    vv = jnp.concatenate([v_prev[before, sl].astype(bf16), vc], axis=0)
                    doc, don = dc_ref[own, sl], d_next[after, sl]
                    qs_c, qs_n = _stack_heads(qc_ref[own, sl].astype(bf16)), _stack_heads(q_next[after, sl].astype(bf16))
                    ds_c, ds_n = _stack_heads(doc.astype(bf16)), _stack_heads(don.astype(bf16))
                    delta_c = jnp.sum(_stack_heads(doc * oc_ref[own, sl]), axis=-1, keepdims=True)
                    delta_n = jnp.sum(_stack_heads(don * o_next[after, sl]), axis=-1, keepdims=True)
                    heads = lambda t: jnp.concatenate([_lane(t, 2 * hp), _lane(t, 2 * hp + 1)], axis=0)
                    p1 = jnp.where(mask_c, jnp.exp(_nt(qs_c, kk) - heads(lse_c)), 0.0)
                    g1 = (p1 * (_nt(ds_c, vv) - delta_c)).astype(bf16)
                    dq = _nn(g1, kk)
                    dq_ref[own, sl] = sofar(0, own, sl) + jnp.where(lo, dq[:CH], dq[CH:])
                    p2 = jnp.where(mask_n, jnp.exp(_nt(qs_n, kc) - heads(lse_n)), 0.0)
                    g2 = (p2 * (_nt(ds_n, vc) - delta_n)).astype(bf16)
                    dk_ref[own, sl] = sofar(1, own, sl) + _tn(jnp.concatenate([g1[:, CH:], g2], axis=0),
                                                              jnp.concatenate([qs_c, qs_n], axis=0))
                    dv_ref[own, sl] = sofar(2, own, sl) + _tn(
                        jnp.concatenate([p1[:, CH:].astype(bf16), p2.astype(bf16)], axis=0),
                        jnp.concatenate([ds_c, ds_n], axis=0))

    cur = pl.BlockSpec((rows, lanes), lambda c, lb: (c, lb))
    prev = pl.BlockSpec((rows, lanes), lambda c, lb: (jnp.maximum(c - 1, 0), lb))
    nxt = pl.BlockSpec((rows, lanes), lambda c, lb: (jnp.minimum(c + 1, steps - 1), lb))
    scur = pl.BlockSpec((rows, CH), lambda c, lb: (c, 0))
    snxt = pl.BlockSpec((rows, CH), lambda c, lb: (jnp.minimum(c + 1, steps - 1), 0))
    more = [] if running is None else list(running)
    return _pallas_hbm(
        body, name="attn_bwd_d%d" % d, grid=(steps, HW // lanes),
        in_specs=[cur, nxt, cur, prev, cur, prev, cur, nxt, cur, nxt, scur, snxt] + [cur] * len(more),
        out_specs=[cur] * 3, out_shape=[SDS((T, HW), f32)] * 3,
        input_output_aliases={12 + i: i for i in range(len(more))},
        compiler_params=_cp(("arbitrary", "arbitrary"), 52),
    )(q, q, k, k, v, v, o, o, do, do, lse, lse, *more)


def _causal(w):
    row = lax.broadcasted_iota(jnp.int32, (CH, CH), 0)
    col = lax.broadcasted_iota(jnp.int32, (CH, CH), 1)
    return jnp.where(col <= row, w, 0.0)


def _sgu_fwd(ua, va, o, lng, lnb, ws, bs_t):
    tm = 512

    def body(ua_ref, va_ref, o_ref, lg_ref, lb_ref, ws_ref, bs_ref, mix_ref):
        for hd in range(4):
            sl = slice(hd * CH, (hd + 1) * CH)
            w = _causal(ws_ref[hd]).astype(bf16)
            for cc in range(tm // CH):
                rs = slice(cc * CH, (cc + 1) * CH)
                u = _gelu(ua_ref[rs, sl])
                v = _gelu(va_ref[rs, sl])
                vc = v - _rowmean(v)
                vn = vc * lax.rsqrt(_rowmean(vc * vc) + EPS) * lg_ref[:, sl] + lb_ref[:, sl]
                z = _nn(w, vn.astype(bf16)) + bs_ref[:, hd:hd + 1]
                mix_ref[rs, sl] = (u * z).astype(bf16)
        mix_ref[:, HW:] = o_ref[...].astype(bf16)

    half = pl.BlockSpec((tm, HW), lambda i: (i, 0))
    vec = pl.BlockSpec((1, HW), lambda i: (0, 0))
    return _pallas_hbm(
        body, name="sgu_fwd", grid=(T // tm,),
        in_specs=[half, half, half, vec, vec, pl.BlockSpec((4, CH, CH), lambda i: (0, 0, 0)),
                  pl.BlockSpec((CH, 4), lambda i: (0, 0))],
        out_specs=pl.BlockSpec((tm, 2 * HW), lambda i: (i, 0)), out_shape=SDS((T, 2 * HW), bf16),
        compiler_params=_cp(("arbitrary",), 32),
    )(ua, va, o, lng, lnb, ws, bs_t)


def _sgu_bwd(ua, va, d_a, lng, lnb, ws, bs_t):
    tm = 512

    def body(ua_ref, va_ref, d_ref, lg_ref, lb_ref, ws_ref, bs_ref, dsg_ref, dln_ref, dws_ref, db_ref):
        @pl.when(pl.program_id(0) == 0)
        def _():
            dln_ref[...] = jnp.zeros_like(dln_ref)
            dws_ref[...] = jnp.zeros_like(dws_ref)
            db_ref[...] = jnp.zeros_like(db_ref)

        for hd in range(4):
            sl = slice(hd * CH, (hd + 1) * CH)
            w = _causal(ws_ref[hd])
            w, wt = w.astype(bf16), w.T.astype(bf16)
            lg = lg_ref[:, sl]
            for cc in range(tm // CH):
                rs = slice(cc * CH, (cc + 1) * CH)
                xa, xv, dd = ua_ref[rs, sl], va_ref[rs, sl], d_ref[rs, sl]
                u, v = _gelu(xa), _gelu(xv)
                vc = v - _rowmean(v)
                rstd = lax.rsqrt(_rowmean(vc * vc) + EPS)
                xh = vc * rstd
                vnb = (xh * lg + lb_ref[:, sl]).astype(bf16)
                z = _nn(w, vnb) + bs_ref[:, hd:hd + 1]
                dz = dd * u
                dzb = dz.astype(bf16)
                dsg_ref[rs, sl] = (dd * z * _gelu_grad(xa)).astype(bf16)
                dws_ref[hd] += _nt(dzb, vnb)
                db_ref[hd] += dz
                dvn = _nn(wt, dzb)
                dln_ref[0:1, sl] += _colsum(dvn * xh)
                dln_ref[1:2, sl] += _colsum(dvn)
                dxh = dvn * lg
                dv = rstd * (dxh - _rowmean(dxh) - xh * _rowmean(dxh * xh))
                dsg_ref[rs, HW + hd * CH:HW + (hd + 1) * CH] = (dv * _gelu_grad(xv)).astype(bf16)

    half = pl.BlockSpec((tm, HW), lambda i: (i, 0))
    vec = pl.BlockSpec((1, HW), lambda i: (0, 0))
    mat = pl.BlockSpec((4, CH, CH), lambda i: (0, 0, 0))
    return _pallas_hbm(
        body, name="sgu_bwd", grid=(T // tm,),
        in_specs=[half, half, half, vec, vec, mat, pl.BlockSpec((CH, 4), lambda i: (0, 0))],
        out_specs=[pl.BlockSpec((tm, 2 * HW), lambda i: (i, 0)), pl.BlockSpec((8, HW), lambda i: (0, 0)), mat, mat],
        out_shape=[SDS((T, 2 * HW), bf16), SDS((8, HW), f32), SDS((4, CH, CH), f32), SDS((4, CH, CH), f32)],
        compiler_params=_cp(("arbitrary",), 32),
    )(ua, va, d_a, lng, lnb, ws, bs_t)


def _outproj_fwd(mixed, w_out, h, par):
    tm = 1024

    def body(mix_ref, w_ref, h_ref, par_ref, ho_ref, po_ref):
        p = _nn(mix_ref[...], w_ref[...])
        po_ref[...] = p.astype(bf16)
        ho_ref[...] = h_ref[...] + par_ref[2:3, :] * p

    row = pl.BlockSpec((tm, D), lambda i: (i, 0))
    return _pallas_hbm(
        body, name="outproj_fwd", grid=(T // tm,),
        in_specs=[row, pl.BlockSpec((D, D), lambda i: (0, 0)), row, pl.BlockSpec((8, D), lambda i: (0, 0))],
        out_specs=[row, row], out_shape=[SDS((T, D), f32), SDS((T, D), bf16)],
        compiler_params=_cp(("arbitrary",), 48),
    )(mixed, w_out, h, par)


def _outproj_bwd(dh, po, w_out, par):
    tm = 1024

    def body(dh_ref, po_ref, w_ref, par_ref, do_ref, da_ref, db_ref, dpar_ref):
        @pl.when(pl.program_id(0) == 0)
        def _():
            dpar_ref[...] = jnp.zeros_like(dpar_ref)

        dh_v = dh_ref[...]
        dob = (par_ref[2:3, :] * dh_v).astype(bf16)
        do_ref[...] = dob
        dpar_ref[2:3, :] += _colsum(dh_v * po_ref[...].astype(f32))
        dm = _nt(dob, w_ref[...])
        da_ref[...] = dm[:, :HW]
        db_ref[...] = dm[:, HW:]

    row = pl.BlockSpec((tm, D), lambda i: (i, 0))
    half = pl.BlockSpec((tm, HW), lambda i: (i, 0))
    one = pl.BlockSpec((8, D), lambda i: (0, 0))
    return _pallas_hbm(
        body, name="outproj_bwd", grid=(T // tm,),
        in_specs=[row, row, pl.BlockSpec((D, D), lambda i: (0, 0)), one],
        out_specs=[row, half, half, one],
        out_shape=[SDS((T, D), bf16), SDS((T, HW), f32), SDS((T, HW), f32), SDS((8, D), f32)],
        compiler_params=_cp(("arbitrary",), 48),
    )(dh, po, w_out, par)


def _inproj_bwd_tok(dh, h, par, w_in_t, dsg, dq, dk, dv, tabs):
    tm = 512

    def body(dh_ref, h_ref, par_ref, w_ref, dsg_ref, dq_ref, dk_ref, dv_ref,
             cos_ref, slo_ref, shi_ref, dhin_ref, dp_ref, y_ref, dpar_ref):
        @pl.when(pl.program_id(0) == 0)
        def _():
            dpar_ref[...] = jnp.zeros_like(dpar_ref)

        cos, slo, shi = _all_heads(cos_ref), _all_heads(slo_ref), _all_heads(shi_ref)
        dp_ref[:, :2 * HW] = dsg_ref[...]
        dp_ref[:, 2 * HW:3 * HW] = _rope_t(dq_ref[...] * 0.125, cos, slo, shi).astype(bf16)
        dp_ref[:, 3 * HW:4 * HW] = _rope_t(dk_ref[...], cos, slo, shi).astype(bf16)
        dp_ref[:, 4 * HW:] = dv_ref[...].astype(bf16)
        dy = _nn(dp_ref[...], w_ref[...])
        par_v, h_v = par_ref[...], h_ref[...]
        y_ref[...] = _norm_mod(h_v, par_v)[0].astype(bf16)
        dx, dsh, dsc, dng = _norm_mod_bwd(dy, h_v, par_v)
        dhin_ref[...] = dh_ref[...] + dx
        dpar_ref[0:1, :] += dsh
        dpar_ref[1:2, :] += dsc
        dpar_ref[3:4, :] += dng

    row = pl.BlockSpec((tm, D), lambda i: (i, 0))
    half = pl.BlockSpec((tm, HW), lambda i: (i, 0))
    one = pl.BlockSpec((8, D), lambda i: (0, 0))
    return _pallas_hbm(
        body, name="inproj_bwd_tok", grid=(T // tm,),
        in_specs=[row, row, one, pl.BlockSpec((INW, D), lambda i: (0, 0)), row] + [half] * 3 +
                 [pl.BlockSpec((tm, CH), lambda i: (i, 0))] * 3,
        out_specs=[row, pl.BlockSpec((tm, INW), lambda i: (i, 0)), row, one],
        out_shape=[SDS((T, D), f32), SDS((T, INW), bf16), SDS((T, D), bf16), SDS((8, D), f32)],
        compiler_params=_cp(("arbitrary",), 56),
    )(dh, h, par, w_in_t, dsg, dq, dk, dv, *tabs)


def _loss_head(h, par, target):
    tm = 512

    def body(h_ref, par_ref, t_ref, dh_ref, acc_ref):
        @pl.when(pl.program_id(0) == 0)
        def _():
            acc_ref[...] = jnp.zeros_like(acc_ref)

        x, g = h_ref[...], par_ref[3:4, :]
        rstd = lax.rsqrt(_rowmean(x * x) + EPS)
        xhat = x * rstd
        err = xhat * g - t_ref[...]
        acc_ref[0:1, :] += _colsum(err * err)
        dy = err * (1.0 / D)
        acc_ref[3:4, :] += _colsum(dy * xhat)
        dxhat = dy * g
        dh_ref[...] = rstd * (dxhat - xhat * _rowmean(dxhat * xhat))

    row = pl.BlockSpec((tm, D), lambda i: (i, 0))
    one = pl.BlockSpec((8, D), lambda i: (0, 0))
    return _pallas_hbm(
        body, name="loss_head", grid=(T // tm,), in_specs=[row, one, row], out_specs=[row, one],
        out_shape=[SDS((T, D), f32), SDS((8, D), f32)], compiler_params=_cp(("arbitrary",), 32),
    )(h, par, target)


ELEMENTWISE_VMEM_BUDGET = 20 << 20


def _block_rows(rows, bytes_per_row):
    cap = ELEMENTWISE_VMEM_BUDGET // bytes_per_row
    if rows <= cap:
        return rows
    return next(b for b in range(cap - cap % 16, 0, -16) if rows % b == 0)


def _sum_slots(land, name):
    _, R, C = land.shape
    br = _block_rows(R, 2 * NDEV * C * land.dtype.itemsize + 2 * C * 4)

    def body(l_ref, o_ref):
        acc = l_ref[0].astype(f32)
        for s in range(1, NDEV):
            acc = acc + l_ref[s].astype(f32)
        o_ref[...] = acc

    return _pallas_hbm(
        body, name=name, grid=(R // br,), in_specs=[pl.BlockSpec((NDEV, br, C), lambda i: (0, i, 0))],
        out_specs=pl.BlockSpec((br, C), lambda i: (i, 0)), out_shape=SDS((R, C), f32),
        compiler_params=_cp(("arbitrary",), 32),
    )(land)


def _sum_partials(land, own, me, name):
    r = land.shape[-2]

    def body(me_ref, l_ref, own_ref, o_ref):
        mine = own_ref[...].astype(f32)
        acc = jnp.where(me_ref[0] == 0, mine, l_ref[0].astype(f32))
        for s in range(1, NDEV):
            acc = acc + jnp.where(me_ref[0] == s, mine, l_ref[s].astype(f32))
        o_ref[...] = acc

    if own.ndim == 3:
        kk = own.shape[0]
        specs = dict(grid=(kk,),
                     in_specs=[pl.BlockSpec((NDEV, None, r, D), lambda i, me_ref: (0, i, 0, 0)),
                               pl.BlockSpec((None, r, D), lambda i, me_ref: (i, me_ref[0], 0))],
                     out_specs=pl.BlockSpec((None, r, D), lambda i, me_ref: (i, 0, 0)))
        out_shape = SDS((kk, r, D), f32)
    else:
        specs = dict(grid=(1,),
                     in_specs=[pl.BlockSpec((NDEV, r, D), lambda i, me_ref: (0, 0, 0)),
                               pl.BlockSpec((r, D), lambda i, me_ref: (me_ref[0], 0))],
                     out_specs=pl.BlockSpec((r, D), lambda i, me_ref: (0, 0)))
        out_shape = SDS((r, D), f32)
    return _pallas_hbm(
        body, name=name, out_shape=out_shape,
        grid_spec=pltpu.PrefetchScalarGridSpec(num_scalar_prefetch=1, **specs),
        compiler_params=_cp(("arbitrary",), 32),
    )(me.reshape(1), land, own)


def _adamw(w, g, m, v, name):
    R, C = w.shape
    br = _block_rows(R, 2 * 7 * C * 4)

    def body(w_ref, g_ref, m_ref, v_ref, d_ref, mo_ref, vo_ref):
        gv = g_ref[...]
        m2 = B1 * m_ref[...] + (1.0 - B1) * gv
        v2 = B2 * v_ref[...] + (1.0 - B2) * (gv * gv)
        mo_ref[...] = m2
        vo_ref[...] = v2
        m_hat = m2 / (1.0 - B1 ** STEP)
        v_hat = v2 / (1.0 - B2 ** STEP)
        d_ref[...] = -LR * (m_hat / (jnp.sqrt(v_hat) + AEPS) + WD * w_ref[...])

    blk = pl.BlockSpec((br, C), lambda i: (i, 0))
    return _pallas_hbm(
        body, name=name, grid=(R // br,), in_specs=[blk] * 4, out_specs=[blk] * 3,
        out_shape=[SDS((R, C), f32)] * 3, compiler_params=_cp(("arbitrary",), 32),
    )(w, g, m, v)


def _adamw_nd(w, g, m, v, name, swapped=False):
    if swapped:
        outs = _adamw_nd(*(jnp.swapaxes(a, -1, -2) for a in (w, g, m, v)), name)
        return [jnp.swapaxes(o, -1, -2) for o in outs]
    shp = w.shape
    r2 = (-1, shp[-1]) if w.ndim > 1 else (8, shp[0] // 8)
    outs = _adamw(w.reshape(r2), g.reshape(r2), m.reshape(r2), v.reshape(r2), name)
    return [o.reshape(shp) for o in outs]


def _par_rows(mod_l, s, gain):
    rows = jnp.pad(mod_l.reshape(9, D)[3 * s:3 * s + 3], ((0, 5), (0, 0)))
    return rows + jnp.pad(gain[None, :], ((3, 4), (0, 0)))


def _pad_rows(a):
    a = a.reshape(-1, CH)
    return jnp.pad(a, ((0, (-a.shape[0]) % 8), (0, 0)))


def _prepare(me, c, ada_w, ada_b, norm_g):
    pay = jnp.pad(c, ((0, 7), (0, 0)))
    pay = jnp.concatenate([pay, jnp.pad(norm_g.reshape(6, OUTS), ((0, 2), (0, D - OUTS)))], axis=0)
    got = _all_gather_small(pay, "gather_c")
    c_all = got[:, 0, :]
    gains = got[:, 8:14, :OUTS].transpose(1, 0, 2).reshape(DEPTH, 3, D)

    ada_b_loc = lax.dynamic_slice(ada_b, (0, me * ADAS), (DEPTH, ADAS)).reshape(DEPTH, 1, ADAS)
    mod_cols = _mod_fwd(c_all, ada_w, ada_b_loc)
    got = _all_gather_small(mod_cols.reshape(DEPTH * NDEV, ADAS), "gather_mod").reshape(NDEV, DEPTH, NDEV, ADAS)
    mod = lax.dynamic_index_in_dim(got, me, axis=2, keepdims=False).transpose(1, 0, 2).reshape(DEPTH, 9 * D)
    pars = [[_par_rows(mod[l], s, gains[l, s]) for s in range(3)] for l in range(DEPTH)]
    return c_all, pars


def _fwd_bwd(x2, target, pars, get_w, put_g, small_ready, sgu_ln_g, sgu_ln_b, sgu_w, sgu_b, final_g):
    tabs = _rope_tables()
    tril = jnp.tril(jnp.ones((CH, CH), dtype=bool))
    behind = _behind

    h = x2
    saved = []
    for l in range(DEPTH):
        lng, lnb = sgu_ln_g[l].reshape(1, HW), sgu_ln_b[l].reshape(1, HW)
        bs_t = sgu_b[l].T
        h0 = h
        w, token = get_w(4 * l, ("ffn1", l), h0)
        h1, fo1, g1, u1, a1 = _ffn_fwd(h0, behind(pars[l][0], token), w)
        w, token = get_w(4 * l + 2, ("inproj", l), h1)
        ua, va, q, k, v = _inproj_fwd(h1, behind(pars[l][1], token), w, tabs)
        branches = [_attn_fwd(q, k, v, d) for d in PATTERN_DILATIONS]
        o, lse = _attn_combine([b[0] for b in branches], [b[1] for b in branches])
        mixed = _sgu_fwd(ua, va, o, lng, lnb, sgu_w[l], bs_t)
        w, token = get_w(4 * l + 3, ("outproj", l), mixed)
        h2, po = _outproj_fwd(mixed, w, h1, behind(pars[l][1], token))
        w, token = get_w(4 * l + 1, ("ffn2", l), h2)
        h3, fo2, g2, u2, a2 = _ffn_fwd(h2, behind(pars[l][2], token), w)
        saved.append((h0, h1, h2, fo1, g1, u1, a1, ua, va, q, k, v, o, mixed, lse, po, fo2, g2, u2, a2))
        h = h3

    par_f = jnp.pad(final_g[None, :], ((3, 4), (0, 0)))
    dh, head = _loss_head(h, par_f, target)

    dmods, dgains, dsgu = [None] * DEPTH, [None] * DEPTH, [None] * DEPTH
    token = None
    for l in reversed(range(DEPTH)):
        w_f1, w_f2, w_i, w_o = (get_w(4 * l + j, None, None)[0] for j in (0, 1, 2, 3))
        h0, h1, h2, fo1, g1, u1, a1, ua, va, q, k, v, o, mixed, lse, po, fo2, g2, u2, a2 = saved[l]
        lng, lnb = sgu_ln_g[l].reshape(1, HW), sgu_ln_b[l].reshape(1, HW)
        bs_t = sgu_b[l].T

        dh, dg_s, du_s, y, dfb, dpar3 = _ffn_bwd_tok(dh, h2, behind(pars[l][2], token), fo2, g2, u2, w_f2)
        token = put_g(4 * l + 1, _ffn_bwd_w(a2, dg_s, du_s, y, dfb))

        dob, d_a, d_b, dpar2g = _outproj_bwd(dh, po, w_o, behind(pars[l][1], token))
        token = put_g(4 * l + 3, _tn_matmul(mixed, dob, 512, "w_out_grad"))
        dqkv = None
        for d in PATTERN_DILATIONS:
            dqkv = _attn_bwd(q, k, v, o, d_b, lse, d, dqkv)
        dsg, dln, dws, dbl = _sgu_bwd(ua, va, d_a, lng, lnb, sgu_w[l], bs_t)
        dh, dp, y2, dpar2 = _inproj_bwd_tok(dh, h1, behind(pars[l][1], token), w_i, dsg, *dqkv, tabs)
        token = put_g(4 * l + 2, _tn_matmul(dp, y2, 640, "w_in_grad"))

        dh, dg_s, du_s, y, dfb, dpar1 = _ffn_bwd_tok(dh, h0, behind(pars[l][0], token), fo1, g1, u1, w_f1)
        dmods[l] = jnp.concatenate([dpar1[0:3], dpar2[0:2], dpar2g[2:3], dpar3[0:3]], axis=0).reshape(9 * D)
        dgains[l] = jnp.stack([dpar1[3], dpar2[3], dpar3[3]])
        dsgu[l] = (dln[0], dln[1], jnp.where(tril[None], dws, 0.0), jnp.sum(dbl, axis=-1))
        if l > 0:
            token = put_g(4 * l, _ffn_bwd_w(a1, dg_s, du_s, y, dfb))
            continue
        token = small_ready(head, dmods, dgains, dsgu)
        token = put_g(4 * DEPTH, _ffn_bwd_w(a1, dg_s, du_s, y, dfb, token, "down"))
        token = put_g(0, _ffn_bwd_w(a1, dg_s, du_s, y, dfb, token, "gate_up"))
    return dh, token


def kernel(x, c, ada_w, ada_b, norm_g, ffn1_wg, ffn1_wu, ffn1_wd, ffn2_wg, ffn2_wu, ffn2_wd, w_in, sgu_ln_g, sgu_ln_b, sgu_w, sgu_b, w_out, final_g, loss_target, m_ada_w, m_ada_b, m_norm_g, m_ffn1_wg, m_ffn1_wu, m_ffn1_wd, m_ffn2_wg, m_ffn2_wu, m_ffn2_wd, m_w_in, m_sgu_ln_g, m_sgu_ln_b, m_sgu_w, m_sgu_b, m_w_out, m_final_g, v_ada_w, v_ada_b, v_norm_g, v_ffn1_wg, v_ffn1_wu, v_ffn1_wd, v_ffn2_wg, v_ffn2_wu, v_ffn2_wd, v_w_in, v_sgu_ln_g, v_sgu_ln_b, v_sgu_w, v_sgu_b, v_w_out, v_final_g):
    me = 4 * lax.axis_index("x") + 2 * lax.axis_index("y") + lax.axis_index("c")

    tr = lambda w: jnp.swapaxes(w, -1, -2).astype(bf16)
    locs = []
    for l in range(DEPTH):
        locs.append(jnp.stack([tr(ffn1_wg[l]), tr(ffn1_wu[l]), ffn1_wd[l].astype(bf16)]))
        locs.append(jnp.stack([tr(ffn2_wg[l]), tr(ffn2_wu[l]), ffn2_wd[l].astype(bf16)]))
        locs.append(tr(w_in[l]))
        locs.append(w_out[l].astype(bf16))

    c_all, pars = _prepare(me, c, ada_w, ada_b, norm_g)
    locs, pars = lax.optimization_barrier((locs, pars))

    first, locs = lax.optimization_barrier((_all_gather_rows([locs[0]], "gather_first")[0], locs))
    placed = [_place_own(a, me) for a in locs[1:]]
    ready = {0: first}
    groups = ([2, 3], [1], [4], [6, 7], [5])
    plan = {("inproj", 0): (("wait", "gather", 0), ("start", "chips", 1)),
            ("outproj", 0): (("wait", "chips", 1), ("start", "pass", 1), ("start", "chips", 2)),
            ("ffn2", 0): (("wait", "pass", 1),),
            ("ffn1", 1): (("wait", "chips", 2), ("start", "pass", 2), ("start", "gather", 3), ("wait", "pass", 2)),
            ("inproj", 1): (("wait", "gather", 3), ("start", "chips", 4)),
            ("outproj", 1): (("wait", "chips", 4), ("start", "pass", 4)),
            ("ffn2", 1): (("wait", "pass", 4),)}
    flying = {}

    def exchange(action, mode, gi, after):
        if action == "start":
            arrays = flying.pop(gi) if mode == "pass" else [placed[p - 1] for p in groups[gi]]
            arrays, _ = lax.optimization_barrier((arrays, after))
            started = _exchange_start(mode, arrays, "%s_start_%d" % (mode, gi))
            flying[gi] = started[:-1]
            return started[-1]
        arrays = _exchange_wait(mode, flying.pop(gi), after, "%s_wait_%d" % (mode, gi))
        if mode == "chips":
            flying[gi] = arrays
        else:
            ready.update(zip(groups[gi], arrays))
        return None

    first_token = exchange("start", "gather", 0, first)

    def get_w(piece, point, after):
        token = first_token if point == ("ffn1", 0) else None
        for action, mode, gi in plan.get(point, ()):
            started = exchange(action, mode, gi, after)
            if started is not None:
                token = after = started
        return ready.get(piece), token

    sent = {}

    def put_g(piece, grad):
        land = lax.empty((NDEV,) + grad.shape[:-2] + (grad.shape[-2] // NDEV, D), bf16)
        started = _exchange_start("scatter", [grad, land], "scatter_start_%d" % piece)
        sent[piece] = started[:-1]
        return started[-1]

    small_sent = []

    def small_ready(head, dmods, dgains, dsgu):
        loss_part = 0.5 * jnp.sum(head[0]) / D
        small = jnp.concatenate([
            _pad_rows(jnp.stack(dmods)), _pad_rows(jnp.stack(dgains)),
            _pad_rows(jnp.stack([s[0] for s in dsgu])), _pad_rows(jnp.stack([s[1] for s in dsgu])),
            _pad_rows(jnp.stack([s[3] for s in dsgu])), _pad_rows(jnp.stack([s[2] for s in dsgu])),
            _pad_rows(head[3]), _pad_rows(jnp.pad(loss_part[None], (0, CH - 1)))], axis=0)
        started = _exchange_start("gather", [_place_own(small, me, "place_small")], "small_start")
        small_sent.append(started[:-1])
        return started[-1]

    dh, last = _fwd_bwd(x[0], loss_target[0], pars, get_w, put_g, small_ready, sgu_ln_g, sgu_ln_b, sgu_w, sgu_b, final_g)
    grad_x = dh[None]

    got = _exchange_wait("gather", small_sent[0], last, "small_wait")[0].reshape(NDEV, -1, CH)
    tot = _sum_slots(got, "sum_small")
    n_mod, n_gain, n_sw = DEPTH * 9 * D // CH, DEPTH * 3 * D // CH, DEPTH * 4 * CH
    offs = [0, n_mod, n_mod + n_gain, n_mod + n_gain + 8, n_mod + n_gain + 16, n_mod + n_gain + 24]
    g_ada_b = tot[offs[0]:offs[1]].reshape(DEPTH, 9 * D)
    g_gain_full = tot[offs[1]:offs[2]].reshape(DEPTH, 3, D)
    g_ln_g = tot[offs[2]:offs[3]].reshape(DEPTH, 4, CH)
    g_ln_b = tot[offs[3]:offs[4]].reshape(DEPTH, 4, CH)
    g_sb = tot[offs[4]:offs[5]].reshape(DEPTH, 4, CH)
    g_sw = tot[offs[5]:offs[5] + n_sw].reshape(DEPTH, 4, CH, CH)
    g_final = tot[offs[5] + n_sw:offs[5] + n_sw + 8].reshape(D)
    loss = tot[offs[5] + n_sw + 8, 0]
    g_norm = lax.dynamic_slice(g_gain_full, (0, 0, me * OUTS), (DEPTH, 3, OUTS))

    dmod_all = got[:, offs[0]:offs[1]].reshape(NDEV, DEPTH, 9 * D)
    dmod_cols = lax.dynamic_slice(dmod_all, (0, 0, me * ADAS), (NDEV, DEPTH, ADAS)).transpose(1, 0, 2)
    g_ada_w = _ada_grad((c_all * jax.nn.sigmoid(c_all)).T, dmod_cols)

    sums, after = {}, tot

    def collect(piece, after):
        own, land = _exchange_wait("scatter", sent[piece], after, "scatter_wait_%d" % piece)
        sums[piece] = _sum_partials(land, own, me, "sum_grads")
        return sums[piece]

    for piece in (5, 7, 6, 4, 1, 3, 2):
        after = collect(piece, after)
    back = lambda t: jnp.swapaxes(t, -1, -2)
    f2 = jnp.stack([sums[4 * l + 1] for l in range(DEPTH)])
    g_w_in = back(jnp.stack([sums[4 * l + 2] for l in range(DEPTH)]))
    g_w_out = jnp.stack([sums[4 * l + 3] for l in range(DEPTH)])

    ws = [ada_w, ada_b, norm_g, ffn1_wg, ffn1_wu, ffn1_wd, ffn2_wg, ffn2_wu, ffn2_wd, w_in, sgu_ln_g, sgu_ln_b, sgu_w,
          sgu_b, w_out, final_g]
    ms = [m_ada_w, m_ada_b, m_norm_g, m_ffn1_wg, m_ffn1_wu, m_ffn1_wd, m_ffn2_wg, m_ffn2_wu, m_ffn2_wd, m_w_in,
          m_sgu_ln_g, m_sgu_ln_b, m_sgu_w, m_sgu_b, m_w_out, m_final_g]
    vs = [v_ada_w, v_ada_b, v_norm_g, v_ffn1_wg, v_ffn1_wu, v_ffn1_wd, v_ffn2_wg, v_ffn2_wu, v_ffn2_wd, v_w_in,
          v_sgu_ln_g, v_sgu_ln_b, v_sgu_w, v_sgu_b, v_w_out, v_final_g]
    gw = [g_ada_w, g_ada_b, g_norm, None, None, None, back(f2[:, 0]), back(f2[:, 1]), f2[:, 2],
          g_w_in, g_ln_g, g_ln_b, g_sw, g_sb, g_w_out, g_final]
    upd = [None] * len(ws)
    lane_narrow = (3, 4, 6, 7, 9)
    for i in (1, 2, 10, 11, 12, 13, 15, 6, 7, 8, 9, 14, 0):
        upd[i] = _adamw_nd(ws[i], gw[i], ms[i], vs[i], "adamw", i in lane_narrow)
    collect(4 * DEPTH, upd[0][0])
    gw[5] = jnp.stack([sums[4 * DEPTH][0]] + [sums[4 * l][2] for l in range(1, DEPTH)])
    upd[5] = _adamw_nd(ws[5], gw[5], ms[5], vs[5], "adamw", False)
    collect(0, upd[5][0])
    for i in (3, 4):
        gw[i] = back(jnp.stack([sums[4 * l][i - 3] for l in range(DEPTH)]))
        upd[i] = _adamw_nd(ws[i], gw[i], ms[i], vs[i], "adamw", i in lane_narrow)
    return (loss, grad_x, *gw, *[u[0] for u in upd], *[u[1] for u in upd], *[u[2] for u in upd])
```

```python
import math

import jax
import jax.numpy as jnp
from jax import lax
from jax.experimental import pallas as pl
from jax.experimental.pallas import tpu as pltpu

f32, bf16 = jnp.float32, jnp.bfloat16
SDS = jax.ShapeDtypeStruct

T, D, F = 4096, 1024, 2816
NDEV, DEPTH = 8, 2
HW = 512
INW = 5 * HW
FS, INS, OUTS, ADAS = F // NDEV, INW // NDEV, D // NDEV, 9 * D // NDEV
CH = 128
PATTERN_DILATIONS = (1, 4, 16)
ROPE_THETA = 10000.0
EPS = 1e-6
LR, B1, B2, AEPS, WD, STEP = 0.001, 0.9, 0.999, 1e-08, 0.01, 10
MESH = pl.DeviceIdType.MESH


def _cp(sems, vmem_mb):
    return pltpu.CompilerParams(dimension_semantics=sems, vmem_limit_bytes=vmem_mb << 20)


def _pallas_hbm(body, *, out_shape, **kw):
    typed = jax.tree.map(lambda s: pltpu.HBM(s.shape, s.dtype), out_shape)

    def run(*operands):
        pin = lambda x: x if x.dtype == jnp.int32 else pltpu.with_memory_space_constraint(x, pltpu.HBM)
        fn, specs, n = body, kw, len(operands)
        if _PENDING:
            operands = operands + (_PENDING.pop(),)
            specs = dict(kw, in_specs=list(kw["in_specs"]) + [pl.BlockSpec((8, CH), lambda *idx: (0, 0))])
            fn = lambda *refs: body(*refs[:n], *refs[n + 1:])
        return pl.pallas_call(fn, out_shape=typed, **specs)(*[pin(x) for x in operands])

    return run


_PENDING = []


def _behind(rows, token):
    if token is not None:
        _PENDING.append(token)
    return rows


def _nn(a, b):
    return lax.dot_general(a, b, (((1,), (0,)), ((), ())), preferred_element_type=f32)


def _nt(a, b):
    return lax.dot_general(a, b, (((1,), (1,)), ((), ())), preferred_element_type=f32)


def _tn(a, b):
    return lax.dot_general(a, b, (((0,), (0,)), ((), ())), preferred_element_type=f32)


def _colsum(a):
    return jnp.sum(a, axis=0, keepdims=True)


def _rowmean(a):
    return jnp.mean(a, axis=-1, keepdims=True)


def _norm_mod(x, par):
    rstd = lax.rsqrt(_rowmean(x * x) + EPS)
    xhat = x * rstd
    n = xhat * par[3:4, :]
    y = n * (1.0 + par[1:2, :]) + par[0:1, :]
    return y, n, xhat, rstd


def _norm_mod_bwd(dy, x, par):
    _, n, xhat, rstd = _norm_mod(x, par)
    dn = dy * (1.0 + par[1:2, :])
    dxhat = dn * par[3:4, :]
    dx = rstd * (dxhat - xhat * _rowmean(dxhat * xhat))
    return dx, _colsum(dy), _colsum(dy * n), _colsum(dn * xhat)


_GK = math.sqrt(2.0 / math.pi)


def _gelu(x):
    return 0.5 * x * (1.0 + jnp.tanh(_GK * (x + 0.044715 * x * x * x)))


def _gelu_grad(x):
    t = jnp.tanh(_GK * (x + 0.044715 * x * x * x))
    return 0.5 * (1.0 + t) + 0.5 * x * (1.0 - t * t) * (_GK * (1.0 + 3.0 * 0.044715 * x * x))


def _rows(ref, idx, r):
    if len(ref.shape) == 3:
        return ref.at[:, pl.ds(idx * r, r), :]
    return ref.at[pl.ds(idx * r, r), :]


def _flip(v, bit):
    return 1 - v if bit else v


def _all_gather_small(x, name):
    R, C = x.shape

    def body(x_ref, out_ref, send_sems, recv_sems):
        mx, my, mc = lax.axis_index("x"), lax.axis_index("y"), lax.axis_index("c")
        me = 4 * mx + 2 * my + mc
        out_ref[me] = x_ref[...]
        sent = []
        for k in range(1, NDEV):
            peer = (_flip(mx, k & 4), _flip(my, k & 2), _flip(mc, k & 1))
            cp = pltpu.make_async_remote_copy(
                src_ref=x_ref, dst_ref=out_ref.at[me], send_sem=send_sems.at[k - 1],
                recv_sem=recv_sems.at[k - 1], device_id=peer, device_id_type=MESH)
            cp.start()
            sent.append(cp)
        for k in range(1, NDEV):
            peer = (_flip(mx, k & 4), _flip(my, k & 2), _flip(mc, k & 1))
            pidx = 4 * peer[0] + 2 * peer[1] + peer[2]
            pltpu.make_async_remote_copy(
                src_ref=x_ref, dst_ref=out_ref.at[pidx], send_sem=send_sems.at[k - 1],
                recv_sem=recv_sems.at[k - 1], device_id=peer, device_id_type=MESH).wait_recv()
        for cp in sent:
            cp.wait_send()

    vm = pl.BlockSpec(memory_space=pltpu.VMEM)
    return pl.pallas_call(
        body, name=name, out_shape=SDS((NDEV, R, C), f32), in_specs=[vm], out_specs=vm,
        scratch_shapes=[pltpu.SemaphoreType.DMA((NDEV - 1,)), pltpu.SemaphoreType.DMA((NDEV - 1,))],
        compiler_params=pltpu.CompilerParams(vmem_limit_bytes=32 << 20),
    )(x)


def _all_gather_rows(locs, name):
    n = len(locs)
    rs = [a.shape[-2] for a in locs]

    def body(*refs):
        src, out = refs[:n], refs[n:2 * n]
        send_sems, recv_sems, loc_sems = refs[2 * n:]
        mx, my, mc = lax.axis_index("x"), lax.axis_index("y"), lax.axis_index("c")
        me, sib = (mx, my, mc), (mx, my, 1 - mc)
        chips = [(1 - mx, my), (mx, 1 - my), (1 - mx, 1 - my)]

        def blk(a, p):
            return _rows(out[a], 4 * p[0] + 2 * p[1] + p[2], rs[a])

        def copy(k, a, block, to, from_src=False):
            return pltpu.make_async_remote_copy(
                src_ref=src[a] if from_src else blk(a, block), dst_ref=blk(a, block),
                send_sem=send_sems.at[k * n + a], recv_sem=recv_sems.at[k * n + a],
                device_id=to, device_id_type=MESH)

        mine = [pltpu.make_async_copy(src[a], blk(a, me), loc_sems.at[a]) for a in range(n)]
        for m in mine:
            m.start()
        first = []
        for j, chip in enumerate(chips):
            first += [copy(1 + j, a, me, (*chip, mc), True) for a in range(n)]
        first += [copy(0, a, me, sib, True) for a in range(n)]
        for cp in first:
            cp.start()
        passed = []
        for j, chip in enumerate(chips):
            for a in range(n):
                copy(1 + j, a, (*chip, mc), me).wait_recv()
            fwd = [copy(4 + j, a, (*chip, mc), sib) for a in range(n)]
            for cp in fwd:
                cp.start()
            passed += fwd
        for a in range(n):
            copy(0, a, sib, me).wait_recv()
        for j, chip in enumerate(chips):
            for a in range(n):
                copy(4 + j, a, (*chip, 1 - mc), me).wait_recv()
        for cp in first + passed:
            cp.wait_send()
        for m in mine:
            m.wait()

    hbm = pl.BlockSpec(memory_space=pl.ANY)
    out_shape = [SDS(a.shape[:-2] + (NDEV * a.shape[-2], a.shape[-1]), a.dtype) for a in locs]
    return pl.pallas_call(
        body, name=name, out_shape=out_shape, in_specs=[hbm] * n, out_specs=[hbm] * n,
        scratch_shapes=[pltpu.SemaphoreType.DMA((7 * n,)), pltpu.SemaphoreType.DMA((7 * n,)),
                        pltpu.SemaphoreType.DMA((n,))],
    )(*locs)


HBM_SPEC = pl.BlockSpec(memory_space=pltpu.HBM)
SEM_SPEC = pl.BlockSpec(memory_space=pltpu.SEMAPHORE)
DATAFLOW_EFFECT = pltpu.SideEffectType.DATAFLOW_SIDE_EFFECTING


def _place_own(loc, me, name="place_own"):
    r, cols = loc.shape[-2:]
    loc3 = loc.reshape(-1, r, cols)
    kk = loc3.shape[0]

    def body(me_ref, src_ref, full_ref, out_ref):
        out_ref[...] = src_ref[...]

    out = _pallas_hbm(
        body, name=name, out_shape=SDS((kk, NDEV * r, cols), loc.dtype),
        grid_spec=pltpu.PrefetchScalarGridSpec(
            num_scalar_prefetch=1, grid=(kk,),
            in_specs=[pl.BlockSpec((None, r, cols), lambda i, me_ref: (i, 0, 0)), pl.BlockSpec(memory_space=pl.ANY)],
            out_specs=pl.BlockSpec((None, r, cols), lambda i, me_ref: (i, me_ref[0], 0))),
        input_output_aliases={2: 0}, compiler_params=_cp(("arbitrary",), 32),
    )(me.reshape(1), loc3, lax.empty((kk, NDEV * r, cols), loc.dtype))
    return out.reshape(loc.shape[:-2] + (NDEV * r, cols))


EXCHANGE_PEERS = {"gather": (1, 2, 3, 4, 5, 6, 7), "scatter": (1, 2, 3, 4, 5, 6, 7), "chips": (1, 2, 4, 6), "pass": (2, 4, 6)}


def _exchange_copies(mode, bufs, n, send_sems, recv_sems):
    mx, my, mc = lax.axis_index("x"), lax.axis_index("y"), lax.axis_index("c")
    me = 4 * mx + 2 * my + mc
    out = []
    for slot, k in enumerate(EXCHANGE_PEERS[mode]):
        peer = (_flip(mx, k & 4), _flip(my, k & 2), _flip(mc, k & 1))
        pidx = 4 * peer[0] + 2 * peer[1] + peer[2]
        for a in range(n):
            r = bufs[a].shape[-2] // NDEV
            if mode == "scatter":
                src, dst, arrive = _rows(bufs[a], pidx, r), bufs[n + a].at[me], bufs[n + a].at[pidx]
            elif mode == "pass":
                peer = (mx, my, 1 - mc)
                src, dst, arrive = _rows(bufs[a], pidx, r), _rows(bufs[a], pidx, r), _rows(bufs[a], pidx + 1 - 2 * mc, r)
            else:
                src, dst, arrive = _rows(bufs[a], me, r), _rows(bufs[a], me, r), _rows(bufs[a], pidx, r)
            sems = dict(send_sem=send_sems.at[slot * n + a], recv_sem=recv_sems.at[slot * n + a],
                        device_id=peer, device_id_type=MESH)
            out.append((pltpu.make_async_remote_copy(src_ref=src, dst_ref=dst, **sems),
                        pltpu.make_async_remote_copy(src_ref=src, dst_ref=arrive, **sems)))
    return out


def _exchange_start(mode, arrays, name):
    m = len(arrays)
    n = m // 2 if mode == "scatter" else m

    def body(*refs):
        send_sems, recv_sems, token = refs[m], refs[m + 1], refs[-1]
        for go, _ in _exchange_copies(mode, refs[:m], n, send_sems, recv_sems):
            go.start()
        token[...] = jnp.zeros_like(token)

    sems = pltpu.SemaphoreType.DMA((len(EXCHANGE_PEERS[mode]) * n,))
    return pl.pallas_call(
        body, name=name, in_specs=[HBM_SPEC] * m,
        out_shape=(sems, sems, *[pltpu.HBM(a.shape, a.dtype) for a in arrays], SDS((8, CH), f32)),
        out_specs=(SEM_SPEC, SEM_SPEC, *[HBM_SPEC] * m, pl.BlockSpec(memory_space=pltpu.VMEM)),
        input_output_aliases={a: 2 + a for a in range(m)},
        compiler_params=pltpu.CompilerParams(has_side_effects=DATAFLOW_EFFECT),
    )(*[pltpu.with_memory_space_constraint(a, pltpu.HBM) for a in arrays])


def _exchange_wait(mode, started, after, name):
    send_sems, recv_sems, *arrays = started
    m = len(arrays)
    n = m // 2 if mode == "scatter" else m

    def body(*refs):
        for go, arrive in _exchange_copies(mode, refs[:m], n, refs[m], refs[m + 1]):
            go.wait_send()
            arrive.wait_recv()

    return pl.pallas_call(
        body, name=name, in_specs=[HBM_SPEC] * m + [SEM_SPEC, SEM_SPEC, pl.BlockSpec(memory_space=pl.ANY)],
        out_shape=[pltpu.HBM(a.shape, a.dtype) for a in arrays], out_specs=[HBM_SPEC] * m,
        input_output_aliases={a: a for a in range(m)},
        compiler_params=pltpu.CompilerParams(has_side_effects=DATAFLOW_EFFECT),
    )(*arrays, send_sems, recv_sems, after)


def _mod_fwd(c_all, ada_w, ada_b_loc):
    def body(c_ref, w_ref, b_ref, o_ref):
        ca = c_ref[...]
        ca = ca * jax.nn.sigmoid(ca)
        o_ref[...] = jnp.dot(ca, w_ref[...], precision=lax.Precision.HIGHEST,
                             preferred_element_type=f32) + b_ref[...]

    return _pallas_hbm(
        body, name="mod_fwd", grid=(DEPTH,), out_shape=SDS((DEPTH, NDEV, ADAS), f32),
        in_specs=[pl.BlockSpec((NDEV, D), lambda l: (0, 0)),
                  pl.BlockSpec((None, D, ADAS), lambda l: (l, 0, 0)),
                  pl.BlockSpec((None, 1, ADAS), lambda l: (l, 0, 0))],
        out_specs=pl.BlockSpec((None, NDEV, ADAS), lambda l: (l, 0, 0)),
        compiler_params=_cp(("arbitrary",), 32),
    )(c_all, ada_w, ada_b_loc)


def _ada_grad(cact_t, dmod_cols):
    def body(c_ref, d_ref, o_ref):
        acc = c_ref[:, 0:1] * d_ref[0:1, :]
        for b in range(1, NDEV):
            acc = acc + c_ref[:, b:b + 1] * d_ref[b:b + 1, :]
        o_ref[...] = acc

    tr = 256
    return _pallas_hbm(
        body, name="ada_grad", grid=(DEPTH, D // tr), out_shape=SDS((DEPTH, D, ADAS), f32),
        in_specs=[pl.BlockSpec((tr, NDEV), lambda l, i: (i, 0)),
                  pl.BlockSpec((None, NDEV, ADAS), lambda l, i: (l, 0, 0))],
        out_specs=pl.BlockSpec((None, tr, ADAS), lambda l, i: (l, i, 0)),
        compiler_params=_cp(("arbitrary", "arbitrary"), 32),
    )(cact_t, dmod_cols)


def _ffn_fwd(h, par, w3):
    tm, tf = 1024, 256
    nj = F // tf

    def body(h_ref, par_ref, wg_ref, wu_ref, wd_ref, ho_ref, fo_ref, g_ref, u_ref, a_ref, y_scr, acc):
        j = pl.program_id(1)

        @pl.when(j == 0)
        def _():
            y_scr[...] = _norm_mod(h_ref[...], par_ref[...])[0].astype(bf16)
            acc[...] = jnp.zeros_like(acc)

        y = y_scr[...]
        g = _nt(y, wg_ref[...])
        u = _nt(y, wu_ref[...])
        a = ((g * jax.nn.sigmoid(g)) * u).astype(bf16)
        g_ref[...] = g.astype(bf16)
        u_ref[...] = u.astype(bf16)
        a_ref[...] = a
        acc[...] += _nn(a, wd_ref[...])

        @pl.when(j == nj - 1)
        def _():
            fo_ref[...] = acc[...].astype(bf16)
            ho_ref[...] = h_ref[...] + (0.5 * par_ref[2:3, :]) * acc[...]

    row = pl.BlockSpec((tm, D), lambda i, j: (i, 0))
    hid = pl.BlockSpec((tm, tf), lambda i, j: (i, j))
    wspec = [pl.BlockSpec((None, tf, D), lambda i, j, k=k: (k, j, 0)) for k in range(3)]
    return _pallas_hbm(
        body, name="ffn_fwd", grid=(T // tm, nj),
        in_specs=[row, pl.BlockSpec((8, D), lambda i, j: (0, 0))] + wspec,
        out_specs=[row, row, hid, hid, hid],
        out_shape=[SDS((T, D), f32), SDS((T, D), bf16), SDS((T, F), bf16), SDS((T, F), bf16), SDS((T, F), bf16)],
        scratch_shapes=[pltpu.VMEM((tm, D), bf16), pltpu.VMEM((tm, D), f32)],
        compiler_params=_cp(("arbitrary", "arbitrary"), 52),
    )(h, par, w3, w3, w3)


def _ffn_bwd_tok(dh, h, par, fo, gs, us, w3):
    tm, tf = 1024, 256
    nj = F // tf

    def body(dh_ref, h_ref, par_ref, fo_ref, g_ref, u_ref, wg_ref, wu_ref, wd_ref, wdn_ref,
             dhin_ref, dg_ref, du_ref, y_ref, dfb_ref, dpar_ref, df_scr, dyacc, da_scr):
        i, j = pl.program_id(0), pl.program_id(1)

        @pl.when(jnp.logical_and(i == 0, j == 0))
        def _():
            dpar_ref[...] = jnp.zeros_like(dpar_ref)

        @pl.when(j == 0)
        def _():
            dh_v, par_v = dh_ref[...], par_ref[...]
            dfb = ((0.5 * par_v[2:3, :]) * dh_v).astype(bf16)
            df_scr[...] = dfb
            dfb_ref[...] = dfb
            dpar_ref[2:3, :] += 0.5 * _colsum(dh_v * fo_ref[...].astype(f32))
            y_ref[...] = _norm_mod(h_ref[...], par_v)[0].astype(bf16)
            dyacc[...] = jnp.zeros_like(dyacc)
            da_scr[...] = _nt(dfb, wd_ref[...])

        for half in range(2):
            rs = slice(half * (tm // 2), (half + 1) * (tm // 2))
            da = da_scr[rs, :]
            da_scr[rs, :] = _nt(df_scr[rs, :], wdn_ref[...])
            g = g_ref[rs, :].astype(f32)
            u = u_ref[rs, :].astype(f32)
            sig = jax.nn.sigmoid(g)
            du = (da * (g * sig)).astype(bf16)
            dg = (da * u * (sig * (1.0 + g * (1.0 - sig)))).astype(bf16)
            dg_ref[rs, :] = dg
            du_ref[rs, :] = du
            dyacc[rs, :] += _nn(dg, wg_ref[...]) + _nn(du, wu_ref[...])

        @pl.when(j == nj - 1)
        def _():
            dx, dsh, dsc, dng = _norm_mod_bwd(dyacc[...], h_ref[...], par_ref[...])
            dhin_ref[...] = dh_ref[...] + dx
            dpar_ref[0:1, :] += dsh
            dpar_ref[1:2, :] += dsc
            dpar_ref[3:4, :] += dng

    row = pl.BlockSpec((tm, D), lambda i, j: (i, 0))
    hid = pl.BlockSpec((tm, tf), lambda i, j: (i, j))
    one = pl.BlockSpec((8, D), lambda i, j: (0, 0))
    wspec = [pl.BlockSpec((None, tf, D), lambda i, j, k=k: (k, j, 0)) for k in range(3)]
    wspec.append(pl.BlockSpec((None, tf, D), lambda i, j: (2, jnp.minimum(j + 1, nj - 1), 0)))
    return _pallas_hbm(
        body, name="ffn_bwd_tok", grid=(T // tm, nj),
        in_specs=[row, row, one, row, hid, hid] + wspec,
        out_specs=[row, hid, hid, row, row, one],
        out_shape=[SDS((T, D), f32), SDS((T, F), bf16), SDS((T, F), bf16), SDS((T, D), bf16),
                   SDS((T, D), bf16), SDS((8, D), f32)],
        scratch_shapes=[pltpu.VMEM((tm, D), bf16), pltpu.VMEM((tm, D), f32), pltpu.VMEM((tm, tf), f32)],
        compiler_params=_cp(("arbitrary", "arbitrary"), 60),
    )(dh, h, par, fo, gs, us, w3, w3, w3, w3)


def _ffn_bwd_w(a_s, dg_s, du_s, y, dfb, anchor=None, part="all"):
    tf, tk = F // 2, 512
    nk = T // tk
    hid = pl.BlockSpec((tk, tf), lambda j, kk: (kk, j))
    row = pl.BlockSpec((tk, D), lambda j, kk: (kk, 0))
    pairs = {"all": ((1, 3), (2, 3), (0, 4)), "gate_up": ((1, 3), (2, 3)), "down": ((0, 4),)}[part]
    ops, specs = (a_s, dg_s, du_s, y, dfb), (hid, hid, hid, row, row)
    used = sorted({i for pr in pairs for i in pr})
    pairs = [(used.index(i), used.index(j)) for i, j in pairs]
    n = len(pairs)

    def body(*refs):
        ins, o_ref, accs = refs[:len(used)], refs[-n - 1], refs[-n:]
        kk = pl.program_id(1)

        @pl.when(kk == 0)
        def _():
            for acc in accs:
                acc[...] = jnp.zeros_like(acc)

        for acc, (i, j) in zip(accs, pairs):
            acc[...] += _tn(ins[i][...], ins[j][...])

        @pl.when(kk == nk - 1)
        def _():
            for t, acc in enumerate(accs):
                o_ref[t] = acc[...].astype(bf16)

    extra = [] if anchor is None else [anchor]
    return _pallas_hbm(
        body, name="ffn_bwd_w" if part == "all" else "ffn_bwd_w_" + part, grid=(F // tf, nk),
        in_specs=[specs[i] for i in used] + [pl.BlockSpec((8, CH), lambda j, kk: (0, 0))] * len(extra),
        out_specs=pl.BlockSpec((n, tf, D), lambda j, kk: (0, j, 0)), out_shape=SDS((n, F, D), bf16),
        scratch_shapes=[pltpu.VMEM((tf, D), f32)] * n,
        compiler_params=_cp(("arbitrary", "arbitrary"), 58),
    )(*[ops[i] for i in used], *extra)


def _tn_matmul(a, b, bm, name):
    M, N = a.shape[1], b.shape[1]
    tk = 1024
    nk = T // tk

    def body(a_ref, b_ref, o_ref, acc):
        kk = pl.program_id(1)

        @pl.when(kk == 0)
        def _():
            acc[...] = jnp.zeros_like(acc)

        acc[...] += _tn(a_ref[...], b_ref[...])

        @pl.when(kk == nk - 1)
        def _():
            o_ref[...] = acc[...].astype(bf16)

    return _pallas_hbm(
        body, name=name, grid=(M // bm, nk),
        in_specs=[pl.BlockSpec((tk, bm), lambda i, kk: (kk, i)), pl.BlockSpec((tk, N), lambda i, kk: (kk, 0))],
        out_specs=pl.BlockSpec((bm, N), lambda i, kk: (i, 0)), out_shape=SDS((M, N), bf16),
        scratch_shapes=[pltpu.VMEM((bm, N), f32)],
        compiler_params=_cp(("arbitrary", "arbitrary"), 40),
    )(a, b)


def _rope_tables():
    inv = ROPE_THETA ** (-jnp.arange(0, 64, 2, dtype=f32) / 64)
    ang = jnp.arange(T, dtype=f32)[:, None] * inv[None, :]
    ang = jnp.concatenate([ang, ang], axis=-1)
    cos, sin = jnp.tile(jnp.cos(ang), (1, 2)), jnp.tile(jnp.sin(ang), (1, 2))
    low = (jnp.arange(CH) % 64 < 32)[None, :]
    return cos, jnp.where(low, -sin, 0.0), jnp.where(low, 0.0, sin)


def _all_heads(tab_ref):
    return jnp.concatenate([tab_ref[...]] * 4, axis=1)


def _rope(t, cos, sin_lo, sin_hi):
    return t * cos + pltpu.roll(t, HW - 32, 1) * sin_lo + pltpu.roll(t, 32, 1) * sin_hi


def _rope_t(g, cos, sin_lo, sin_hi):
    return g * cos + pltpu.roll(g * sin_lo, 32, 1) + pltpu.roll(g * sin_hi, HW - 32, 1)


def _inproj_fwd(h, par, w_in_t, tabs):
    tm = 512

    def body(h_ref, par_ref, w_ref, cos_ref, slo_ref, shi_ref, ua_ref, va_ref, q_ref, k_ref, v_ref):
        y = _norm_mod(h_ref[...], par_ref[...])[0].astype(bf16)
        proj = lambda c: _nt(y, w_ref[c * HW:(c + 1) * HW, :])
        ua_ref[...] = proj(0)
        va_ref[...] = proj(1)
        cos, slo, shi = _all_heads(cos_ref), _all_heads(slo_ref), _all_heads(shi_ref)
        q_ref[...] = _rope(proj(2), cos, slo, shi) * 0.125
        k_ref[...] = _rope(proj(3), cos, slo, shi)
        v_ref[...] = proj(4)

    row = pl.BlockSpec((tm, D), lambda i: (i, 0))
    half = pl.BlockSpec((tm, HW), lambda i: (i, 0))
    return _pallas_hbm(
        body, name="inproj_fwd", grid=(T // tm,),
        in_specs=[row, pl.BlockSpec((8, D), lambda i: (0, 0)), pl.BlockSpec((INW, D), lambda i: (0, 0))] +
                 [pl.BlockSpec((tm, CH), lambda i: (i, 0))] * 3,
        out_specs=[half] * 5,
        out_shape=[SDS((T, HW), f32)] * 5,
        compiler_params=_cp(("arbitrary",), 48),
    )(h, par, w_in_t, *tabs)


def _head_masks():
    lane = lax.broadcasted_iota(jnp.int32, (1, CH), 1)
    return lane < 64, lane >= 64


def _lane(t, idx):
    return jnp.sum(jnp.where(lax.broadcasted_iota(jnp.int32, t.shape, 1) == idx, t, 0.0), axis=-1, keepdims=True)


def _stack_heads(x):
    lo, hi = _head_masks()
    zero = jnp.zeros_like(x)
    return jnp.concatenate([jnp.where(lo, x, zero), jnp.where(hi, x, zero)], axis=0)


def _band_mask(has_prev):
    row = lax.broadcasted_iota(jnp.int32, (2 * CH, 2 * CH), 0) & (CH - 1)
    col = lax.broadcasted_iota(jnp.int32, (2 * CH, 2 * CH), 1)
    in_prev = jnp.logical_and(jnp.logical_and(col < CH, col >= row), has_prev)
    return jnp.logical_or(in_prev, jnp.logical_and(col >= CH, col - CH <= row))


def _attn_tiling(d):
    return CH * 16, 16 // d, CH


def _attn_fwd(q, k, v, d):
    rows, blocks, lanes = _attn_tiling(d)
    pairs = lanes // CH

    def body(q_ref, kc_ref, kp_ref, vc_ref, vp_ref, o_ref, lse_ref):
        c, lb = pl.program_id(0), pl.program_id(1)
        col = lax.broadcasted_iota(jnp.int32, (CH, CH), 1)
        lo, _ = _head_masks()

        @pl.when(lb == 0)
        def _():
            lse_ref[...] = jnp.zeros_like(lse_ref)

        at = lambda r, b: pl.ds(r + b * CH * d, CH, stride=d) if d > 1 else pl.ds(b * CH, CH)
        for r in range(d):
            for b in range(blocks):
                own = at(r, b)
                k_prev, v_prev, before, mask = ((kc_ref, vc_ref, at(r, b - 1), _band_mask(True)) if b > 0 else
                                                (kp_ref, vp_ref, at(r, blocks - 1), _band_mask(c > 0)))
                lse_tile = lse_ref[own, :]
                for pi in range(pairs):
                    sl = slice(pi * CH, (pi + 1) * CH)
                    hp = lb * pairs + pi
                    kk = jnp.concatenate([k_prev[before, sl], kc_ref[own, sl]], axis=0).astype(bf16)
                    vv = jnp.concatenate([v_prev[before, sl], vc_ref[own, sl]], axis=0).astype(bf16)
                    s = jnp.where(mask, _nt(_stack_heads(q_ref[own, sl].astype(bf16)), kk), -jnp.inf)
                    m = jnp.max(s, axis=-1, keepdims=True)
                    p = jnp.exp(s - m)
                    den = jnp.sum(p, axis=-1, keepdims=True)
                    o = _nn(p.astype(bf16), vv) / den
                    o_ref[own, sl] = jnp.where(lo, o[:CH], o[CH:])
                    lse = m + jnp.log(den)
                    lse_tile = jnp.where(col == 2 * hp, lse[:CH], jnp.where(col == 2 * hp + 1, lse[CH:], lse_tile))
                lse_ref[own, :] = lse_tile

    cur = pl.BlockSpec((rows, lanes), lambda c, lb: (c, lb))
    prev = pl.BlockSpec((rows, lanes), lambda c, lb: (jnp.maximum(c - 1, 0), lb))
    return _pallas_hbm(
        body, name="attn_fwd_d%d" % d, grid=(T // rows, HW // lanes), in_specs=[cur, cur, prev, cur, prev],
        out_specs=[cur, pl.BlockSpec((rows, CH), lambda c, lb: (c, 0))],
        out_shape=[SDS((T, HW), f32), SDS((T, CH), f32)],
        compiler_params=_cp(("arbitrary", "arbitrary"), 40),
    )(q, k, k, v, v)


def _attn_combine(os_, lses):
    tm = 512

    def body(o0_ref, o1_ref, o2_ref, l0_ref, l1_ref, l2_ref, o_ref, lse_ref):
        l0, l1, l2 = l0_ref[...], l1_ref[...], l2_ref[...]
        m = jnp.maximum(jnp.maximum(l0, l1), l2)
        e = [jnp.exp(l0 - m), jnp.exp(l1 - m), jnp.exp(l2 - m)]
        s = e[0] + e[1] + e[2]
        w = [ei / s for ei in e]
        lse_ref[...] = m + jnp.log(s)
        lo, _ = _head_masks()
        for hp in range(4):
            sl = slice(hp * CH, (hp + 1) * CH)
            acc = jnp.zeros((tm, CH), f32)
            for wp, op_ref in zip(w, (o0_ref, o1_ref, o2_ref)):
                wexp = jnp.where(lo, wp[:, 2 * hp:2 * hp + 1], wp[:, 2 * hp + 1:2 * hp + 2])
                acc = acc + wexp * op_ref[:, sl].astype(f32)
            o_ref[:, sl] = acc

    half = pl.BlockSpec((tm, HW), lambda i: (i, 0))
    stat = pl.BlockSpec((tm, CH), lambda i: (i, 0))
    return _pallas_hbm(
        body, name="attn_combine", grid=(T // tm,), in_specs=[half] * 3 + [stat] * 3,
        out_specs=[half, stat], out_shape=[SDS((T, HW), f32), SDS((T, CH), f32)],
        compiler_params=_cp(("arbitrary",), 32),
    )(*os_, *lses)


def _attn_bwd(q, k, v, o, do, lse, d, running=None):
    rows, blocks, lanes = _attn_tiling(d)
    pairs = lanes // CH
    steps = T // rows

    def body(qc_ref, qn_ref, kc_ref, kp_ref, vc_ref, vp_ref, oc_ref, on_ref, dc_ref, dn_ref, lc_ref, ln_ref, *rest):
        dq_ref, dk_ref, dv_ref = rest[-3:]
        sofar = (lambda i, rows, sl: rest[i][rows, sl]) if running is not None else (lambda i, rows, sl: 0.0)
        c, lb = pl.program_id(0), pl.program_id(1)
        lo, _ = _head_masks()
        at = lambda r, b: pl.ds(r + b * CH * d, CH, stride=d) if d > 1 else pl.ds(b * CH, CH)
        for r in range(d):
            for b in range(blocks):
                own = at(r, b)
                k_prev, v_prev, before, mask_c = ((kc_ref, vc_ref, at(r, b - 1), _band_mask(True)) if b > 0 else
                                                  (kp_ref, vp_ref, at(r, blocks - 1), _band_mask(c > 0)))
                q_next, o_next, d_next, l_next, after, has_next = (
                    (qc_ref, oc_ref, dc_ref, lc_ref, at(r, b + 1), True) if b + 1 < blocks else
                    (qn_ref, on_ref, dn_ref, ln_ref, at(r, 0), c < steps - 1))
                mask_n = _band_mask(has_next)[:, :CH]
                lse_c, lse_n = lc_ref[own, :], l_next[after, :]
                for pi in range(pairs):
                    sl = slice(pi * CH, (pi + 1) * CH)
                    hp = lb * pairs + pi
                    kc, vc = kc_ref[own, sl].astype(bf16), vc_ref[own, sl].astype(bf16)
                    kk = jnp.concatenate([k_prev[before, sl].astype(bf16), kc], axis=0)
                    vv = jnp.concatenate([v_prev[before, sl].astype(bf16), vc], axis=0)
                    doc, don = dc_ref[own, sl], d_next[after, sl]
                    qs_c, qs_n = _stack_heads(qc_ref[own, sl].astype(bf16)), _stack_heads(q_next[after, sl].astype(bf16))
                    ds_c, ds_n = _stack_heads(doc.astype(bf16)), _stack_heads(don.astype(bf16))
                    delta_c = jnp.sum(_stack_heads(doc * oc_ref[own, sl]), axis=-1, keepdims=True)
                    delta_n = jnp.sum(_stack_heads(don * o_next[after, sl]), axis=-1, keepdims=True)
                    heads = lambda t: jnp.concatenate([_lane(t, 2 * hp), _lane(t, 2 * hp + 1)], axis=0)
                    p1 = jnp.where(mask_c, jnp.exp(_nt(qs_c, kk) - heads(lse_c)), 0.0)
                    g1 = (p1 * (_nt(ds_c, vv) - delta_c)).astype(bf16)
                    dq = _nn(g1, kk)
                    dq_ref[own, sl] = sofar(0, own, sl) + jnp.where(lo, dq[:CH], dq[CH:])
                    p2 = jnp.where(mask_n, jnp.exp(_nt(qs_n, kc) - heads(lse_n)), 0.0)
                    g2 = (p2 * (_nt(ds_n, vc) - delta_n)).astype(bf16)
                    dk_ref[own, sl] = sofar(1, own, sl) + _tn(jnp.concatenate([g1[:, CH:], g2], axis=0),
                                                              jnp.concatenate([qs_c, qs_n], axis=0))
                    dv_ref[own, sl] = sofar(2, own, sl) + _tn(
                        jnp.concatenate([p1[:, CH:].astype(bf16), p2.astype(bf16)], axis=0),
                        jnp.concatenate([ds_c, ds_n], axis=0))

    cur = pl.BlockSpec((rows, lanes), lambda c, lb: (c, lb))
    prev = pl.BlockSpec((rows, lanes), lambda c, lb: (jnp.maximum(c - 1, 0), lb))
    nxt = pl.BlockSpec((rows, lanes), lambda c, lb: (jnp.minimum(c + 1, steps - 1), lb))
    scur = pl.BlockSpec((rows, CH), lambda c, lb: (c, 0))
    snxt = pl.BlockSpec((rows, CH), lambda c, lb: (jnp.minimum(c + 1, steps - 1), 0))
    more = [] if running is None else list(running)
    return _pallas_hbm(
        body, name="attn_bwd_d%d" % d, grid=(steps, HW // lanes),
        in_specs=[cur, nxt, cur, prev, cur, prev, cur, nxt, cur, nxt, scur, snxt] + [cur] * len(more),
        out_specs=[cur] * 3, out_shape=[SDS((T, HW), f32)] * 3,
        input_output_aliases={12 + i: i for i in range(len(more))},
        compiler_params=_cp(("arbitrary", "arbitrary"), 52),
    )(q, q, k, k, v, v, o, o, do, do, lse, lse, *more)


def _causal(w):
    row = lax.broadcasted_iota(jnp.int32, (CH, CH), 0)
    col = lax.broadcasted_iota(jnp.int32, (CH, CH), 1)
    return jnp.where(col <= row, w, 0.0)


def _sgu_fwd(ua, va, o, lng, lnb, ws, bs_t):
    tm = 512

    def body(ua_ref, va_ref, o_ref, lg_ref, lb_ref, ws_ref, bs_ref, mix_ref):
        for hd in range(4):
            sl = slice(hd * CH, (hd + 1) * CH)
            w = _causal(ws_ref[hd]).astype(bf16)
            for cc in range(tm // CH):
                rs = slice(cc * CH, (cc + 1) * CH)
                u = _gelu(ua_ref[rs, sl])
                v = _gelu(va_ref[rs, sl])
                vc = v - _rowmean(v)
                vn = vc * lax.rsqrt(_rowmean(vc * vc) + EPS) * lg_ref[:, sl] + lb_ref[:, sl]
                z = _nn(w, vn.astype(bf16)) + bs_ref[:, hd:hd + 1]
                mix_ref[rs, sl] = (u * z).astype(bf16)
        mix_ref[:, HW:] = o_ref[...].astype(bf16)

    half = pl.BlockSpec((tm, HW), lambda i: (i, 0))
    vec = pl.BlockSpec((1, HW), lambda i: (0, 0))
    return _pallas_hbm(
        body, name="sgu_fwd", grid=(T // tm,),
        in_specs=[half, half, half, vec, vec, pl.BlockSpec((4, CH, CH), lambda i: (0, 0, 0)),
                  pl.BlockSpec((CH, 4), lambda i: (0, 0))],
        out_specs=pl.BlockSpec((tm, 2 * HW), lambda i: (i, 0)), out_shape=SDS((T, 2 * HW), bf16),
        compiler_params=_cp(("arbitrary",), 32),
    )(ua, va, o, lng, lnb, ws, bs_t)


def _sgu_bwd(ua, va, d_a, lng, lnb, ws, bs_t):
    tm = 512

    def body(ua_ref, va_ref, d_ref, lg_ref, lb_ref, ws_ref, bs_ref, dsg_ref, dln_ref, dws_ref, db_ref):
        @pl.when(pl.program_id(0) == 0)
        def _():
            dln_ref[...] = jnp.zeros_like(dln_ref)
            dws_ref[...] = jnp.zeros_like(dws_ref)
            db_ref[...] = jnp.zeros_like(db_ref)

        for hd in range(4):
            sl = slice(hd * CH, (hd + 1) * CH)
            w = _causal(ws_ref[hd])
            w, wt = w.astype(bf16), w.T.astype(bf16)
            lg = lg_ref[:, sl]
            for cc in range(tm // CH):
                rs = slice(cc * CH, (cc + 1) * CH)
                xa, xv, dd = ua_ref[rs, sl], va_ref[rs, sl], d_ref[rs, sl]
                u, v = _gelu(xa), _gelu(xv)
                vc = v - _rowmean(v)
                rstd = lax.rsqrt(_rowmean(vc * vc) + EPS)
                xh = vc * rstd
                vnb = (xh * lg + lb_ref[:, sl]).astype(bf16)
                z = _nn(w, vnb) + bs_ref[:, hd:hd + 1]
                dz = dd * u
                dzb = dz.astype(bf16)
                dsg_ref[rs, sl] = (dd * z * _gelu_grad(xa)).astype(bf16)
                dws_ref[hd] += _nt(dzb, vnb)
                db_ref[hd] += dz
                dvn = _nn(wt, dzb)
                dln_ref[0:1, sl] += _colsum(dvn * xh)
                dln_ref[1:2, sl] += _colsum(dvn)
                dxh = dvn * lg
                dv = rstd * (dxh - _rowmean(dxh) - xh * _rowmean(dxh * xh))
                dsg_ref[rs, HW + hd * CH:HW + (hd + 1) * CH] = (dv * _gelu_grad(xv)).astype(bf16)

    half = pl.BlockSpec((tm, HW), lambda i: (i, 0))
    vec = pl.BlockSpec((1, HW), lambda i: (0, 0))
    mat = pl.BlockSpec((4, CH, CH), lambda i: (0, 0, 0))
    return _pallas_hbm(
        body, name="sgu_bwd", grid=(T // tm,),
        in_specs=[half, half, half, vec, vec, mat, pl.BlockSpec((CH, 4), lambda i: (0, 0))],
        out_specs=[pl.BlockSpec((tm, 2 * HW), lambda i: (i, 0)), pl.BlockSpec((8, HW), lambda i: (0, 0)), mat, mat],
        out_shape=[SDS((T, 2 * HW), bf16), SDS((8, HW), f32), SDS((4, CH, CH), f32), SDS((4, CH, CH), f32)],
        compiler_params=_cp(("arbitrary",), 32),
    )(ua, va, d_a, lng, lnb, ws, bs_t)


def _outproj_fwd(mixed, w_out, h, par):
    tm = 1024

    def body(mix_ref, w_ref, h_ref, par_ref, ho_ref, po_ref):
        p = _nn(mix_ref[...], w_ref[...])
        po_ref[...] = p.astype(bf16)
        ho_ref[...] = h_ref[...] + par_ref[2:3, :] * p

    row = pl.BlockSpec((tm, D), lambda i: (i, 0))
    return _pallas_hbm(
        body, name="outproj_fwd", grid=(T // tm,),
        in_specs=[row, pl.BlockSpec((D, D), lambda i: (0, 0)), row, pl.BlockSpec((8, D), lambda i: (0, 0))],
        out_specs=[row, row], out_shape=[SDS((T, D), f32), SDS((T, D), bf16)],
        compiler_params=_cp(("arbitrary",), 48),
    )(mixed, w_out, h, par)


def _outproj_bwd(dh, po, w_out, par):
    tm = 1024

    def body(dh_ref, po_ref, w_ref, par_ref, do_ref, da_ref, db_ref, dpar_ref):
        @pl.when(pl.program_id(0) == 0)
        def _():
            dpar_ref[...] = jnp.zeros_like(dpar_ref)

        dh_v = dh_ref[...]
        dob = (par_ref[2:3, :] * dh_v).astype(bf16)
        do_ref[...] = dob
        dpar_ref[2:3, :] += _colsum(dh_v * po_ref[...].astype(f32))
        dm = _nt(dob, w_ref[...])
        da_ref[...] = dm[:, :HW]
        db_ref[...] = dm[:, HW:]

    row = pl.BlockSpec((tm, D), lambda i: (i, 0))
    half = pl.BlockSpec((tm, HW), lambda i: (i, 0))
    one = pl.BlockSpec((8, D), lambda i: (0, 0))
    return _pallas_hbm(
        body, name="outproj_bwd", grid=(T // tm,),
        in_specs=[row, row, pl.BlockSpec((D, D), lambda i: (0, 0)), one],
        out_specs=[row, half, half, one],
        out_shape=[SDS((T, D), bf16), SDS((T, HW), f32), SDS((T, HW), f32), SDS((8, D), f32)],
        compiler_params=_cp(("arbitrary",), 48),
    )(dh, po, w_out, par)


def _inproj_bwd_tok(dh, h, par, w_in_t, dsg, dq, dk, dv, tabs):
    tm = 512

    def body(dh_ref, h_ref, par_ref, w_ref, dsg_ref, dq_ref, dk_ref, dv_ref,
             cos_ref, slo_ref, shi_ref, dhin_ref, dp_ref, y_ref, dpar_ref):
        @pl.when(pl.program_id(0) == 0)
        def _():
            dpar_ref[...] = jnp.zeros_like(dpar_ref)

        cos, slo, shi = _all_heads(cos_ref), _all_heads(slo_ref), _all_heads(shi_ref)
        dp_ref[:, :2 * HW] = dsg_ref[...]
        dp_ref[:, 2 * HW:3 * HW] = _rope_t(dq_ref[...] * 0.125, cos, slo, shi).astype(bf16)
        dp_ref[:, 3 * HW:4 * HW] = _rope_t(dk_ref[...], cos, slo, shi).astype(bf16)
        dp_ref[:, 4 * HW:] = dv_ref[...].astype(bf16)
        dy = _nn(dp_ref[...], w_ref[...])
        par_v, h_v = par_ref[...], h_ref[...]
        y_ref[...] = _norm_mod(h_v, par_v)[0].astype(bf16)
        dx, dsh, dsc, dng = _norm_mod_bwd(dy, h_v, par_v)
        dhin_ref[...] = dh_ref[...] + dx
        dpar_ref[0:1, :] += dsh
        dpar_ref[1:2, :] += dsc
        dpar_ref[3:4, :] += dng

    row = pl.BlockSpec((tm, D), lambda i: (i, 0))
    half = pl.BlockSpec((tm, HW), lambda i: (i, 0))
    one = pl.BlockSpec((8, D), lambda i: (0, 0))
    return _pallas_hbm(
        body, name="inproj_bwd_tok", grid=(T // tm,),
        in_specs=[row, row, one, pl.BlockSpec((INW, D), lambda i: (0, 0)), row] + [half] * 3 +
                 [pl.BlockSpec((tm, CH), lambda i: (i, 0))] * 3,
        out_specs=[row, pl.BlockSpec((tm, INW), lambda i: (i, 0)), row, one],
        out_shape=[SDS((T, D), f32), SDS((T, INW), bf16), SDS((T, D), bf16), SDS((8, D), f32)],
        compiler_params=_cp(("arbitrary",), 56),
    )(dh, h, par, w_in_t, dsg, dq, dk, dv, *tabs)


def _loss_head(h, par, target):
    tm = 512

    def body(h_ref, par_ref, t_ref, dh_ref, acc_ref):
        @pl.when(pl.program_id(0) == 0)
        def _():
            acc_ref[...] = jnp.zeros_like(acc_ref)

        x, g = h_ref[...], par_ref[3:4, :]
        rstd = lax.rsqrt(_rowmean(x * x) + EPS)
        xhat = x * rstd
        err = xhat * g - t_ref[...]
        acc_ref[0:1, :] += _colsum(err * err)
        dy = err * (1.0 / D)
        acc_ref[3:4, :] += _colsum(dy * xhat)
        dxhat = dy * g
        dh_ref[...] = rstd * (dxhat - xhat * _rowmean(dxhat * xhat))

    row = pl.BlockSpec((tm, D), lambda i: (i, 0))
    one = pl.BlockSpec((8, D), lambda i: (0, 0))
    return _pallas_hbm(
        body, name="loss_head", grid=(T // tm,), in_specs=[row, one, row], out_specs=[row, one],
        out_shape=[SDS((T, D), f32), SDS((8, D), f32)], compiler_params=_cp(("arbitrary",), 32),
    )(h, par, target)


ELEMENTWISE_VMEM_BUDGET = 20 << 20


def _block_rows(rows, bytes_per_row):
    cap = ELEMENTWISE_VMEM_BUDGET // bytes_per_row
    if rows <= cap:
        return rows
    return next(b for b in range(cap - cap % 16, 0, -16) if rows % b == 0)


def _sum_slots(land, name):
    _, R, C = land.shape
    br = _block_rows(R, 2 * NDEV * C * land.dtype.itemsize + 2 * C * 4)

    def body(l_ref, o_ref):
        acc = l_ref[0].astype(f32)
        for s in range(1, NDEV):
            acc = acc + l_ref[s].astype(f32)
        o_ref[...] = acc

    return _pallas_hbm(
        body, name=name, grid=(R // br,), in_specs=[pl.BlockSpec((NDEV, br, C), lambda i: (0, i, 0))],
        out_specs=pl.BlockSpec((br, C), lambda i: (i, 0)), out_shape=SDS((R, C), f32),
        compiler_params=_cp(("arbitrary",), 32),
    )(land)


def _sum_partials(land, own, me, name):
    r = land.shape[-2]

    def body(me_ref, l_ref, own_ref, o_ref):
        mine = own_ref[...].astype(f32)
        acc = jnp.where(me_ref[0] == 0, mine, l_ref[0].astype(f32))
        for s in range(1, NDEV):
            acc = acc + jnp.where(me_ref[0] == s, mine, l_ref[s].astype(f32))
        o_ref[...] = acc

    if own.ndim == 3:
        kk = own.shape[0]
        specs = dict(grid=(kk,),
                     in_specs=[pl.BlockSpec((NDEV, None, r, D), lambda i, me_ref: (0, i, 0, 0)),
                               pl.BlockSpec((None, r, D), lambda i, me_ref: (i, me_ref[0], 0))],
                     out_specs=pl.BlockSpec((None, r, D), lambda i, me_ref: (i, 0, 0)))
        out_shape = SDS((kk, r, D), f32)
    else:
        specs = dict(grid=(1,),
                     in_specs=[pl.BlockSpec((NDEV, r, D), lambda i, me_ref: (0, 0, 0)),
                               pl.BlockSpec((r, D), lambda i, me_ref: (me_ref[0], 0))],
                     out_specs=pl.BlockSpec((r, D), lambda i, me_ref: (0, 0)))
        out_shape = SDS((r, D), f32)
    return _pallas_hbm(
        body, name=name, out_shape=out_shape,
        grid_spec=pltpu.PrefetchScalarGridSpec(num_scalar_prefetch=1, **specs),
        compiler_params=_cp(("arbitrary",), 32),
    )(me.reshape(1), land, own)


def _adamw(w, g, m, v, name):
    R, C = w.shape
    br = _block_rows(R, 2 * 7 * C * 4)

    def body(w_ref, g_ref, m_ref, v_ref, d_ref, mo_ref, vo_ref):
        gv = g_ref[...]
        m2 = B1 * m_ref[...] + (1.0 - B1) * gv
        v2 = B2 * v_ref[...] + (1.0 - B2) * (gv * gv)
        mo_ref[...] = m2
        vo_ref[...] = v2
        m_hat = m2 / (1.0 - B1 ** STEP)
        v_hat = v2 / (1.0 - B2 ** STEP)
        d_ref[...] = -LR * (m_hat / (jnp.sqrt(v_hat) + AEPS) + WD * w_ref[...])

    blk = pl.BlockSpec((br, C), lambda i: (i, 0))
    return _pallas_hbm(
        body, name=name, grid=(R // br,), in_specs=[blk] * 4, out_specs=[blk] * 3,
        out_shape=[SDS((R, C), f32)] * 3, compiler_params=_cp(("arbitrary",), 32),
    )(w, g, m, v)


def _adamw_nd(w, g, m, v, name, swapped=False):
    if swapped:
        outs = _adamw_nd(*(jnp.swapaxes(a, -1, -2) for a in (w, g, m, v)), name)
        return [jnp.swapaxes(o, -1, -2) for o in outs]
    shp = w.shape
    r2 = (-1, shp[-1]) if w.ndim > 1 else (8, shp[0] // 8)
    outs = _adamw(w.reshape(r2), g.reshape(r2), m.reshape(r2), v.reshape(r2), name)
    return [o.reshape(shp) for o in outs]


def _par_rows(mod_l, s, gain):
    rows = jnp.pad(mod_l.reshape(9, D)[3 * s:3 * s + 3], ((0, 5), (0, 0)))
    return rows + jnp.pad(gain[None, :], ((3, 4), (0, 0)))


def _pad_rows(a):
    a = a.reshape(-1, CH)
    return jnp.pad(a, ((0, (-a.shape[0]) % 8), (0, 0)))


def _prepare(me, c, ada_w, ada_b, norm_g):
    pay = jnp.pad(c, ((0, 7), (0, 0)))
    pay = jnp.concatenate([pay, jnp.pad(norm_g.reshape(6, OUTS), ((0, 2), (0, D - OUTS)))], axis=0)
    got = _all_gather_small(pay, "gather_c")
    c_all = got[:, 0, :]
    gains = got[:, 8:14, :OUTS].transpose(1, 0, 2).reshape(DEPTH, 3, D)

    ada_b_loc = lax.dynamic_slice(ada_b, (0, me * ADAS), (DEPTH, ADAS)).reshape(DEPTH, 1, ADAS)
    mod_cols = _mod_fwd(c_all, ada_w, ada_b_loc)
    got = _all_gather_small(mod_cols.reshape(DEPTH * NDEV, ADAS), "gather_mod").reshape(NDEV, DEPTH, NDEV, ADAS)
    mod = lax.dynamic_index_in_dim(got, me, axis=2, keepdims=False).transpose(1, 0, 2).reshape(DEPTH, 9 * D)
    pars = [[_par_rows(mod[l], s, gains[l, s]) for s in range(3)] for l in range(DEPTH)]
    return c_all, pars


def _fwd_bwd(x2, target, pars, get_w, put_g, small_ready, sgu_ln_g, sgu_ln_b, sgu_w, sgu_b, final_g):
    tabs = _rope_tables()
    tril = jnp.tril(jnp.ones((CH, CH), dtype=bool))
    behind = _behind

    h = x2
    saved = []
    for l in range(DEPTH):
        lng, lnb = sgu_ln_g[l].reshape(1, HW), sgu_ln_b[l].reshape(1, HW)
        bs_t = sgu_b[l].T
        h0 = h
        w, token = get_w(4 * l, ("ffn1", l), h0)
        h1, fo1, g1, u1, a1 = _ffn_fwd(h0, behind(pars[l][0], token), w)
        w, token = get_w(4 * l + 2, ("inproj", l), h1)
        ua, va, q, k, v = _inproj_fwd(h1, behind(pars[l][1], token), w, tabs)
        branches = [_attn_fwd(q, k, v, d) for d in PATTERN_DILATIONS]
        o, lse = _attn_combine([b[0] for b in branches], [b[1] for b in branches])
        mixed = _sgu_fwd(ua, va, o, lng, lnb, sgu_w[l], bs_t)
        w, token = get_w(4 * l + 3, ("outproj", l), mixed)
        h2, po = _outproj_fwd(mixed, w, h1, behind(pars[l][1], token))
        w, token = get_w(4 * l + 1, ("ffn2", l), h2)
        h3, fo2, g2, u2, a2 = _ffn_fwd(h2, behind(pars[l][2], token), w)
        saved.append((h0, h1, h2, fo1, g1, u1, a1, ua, va, q, k, v, o, mixed, lse, po, fo2, g2, u2, a2))
        h = h3

    par_f = jnp.pad(final_g[None, :], ((3, 4), (0, 0)))
    dh, head = _loss_head(h, par_f, target)

    dmods, dgains, dsgu = [None] * DEPTH, [None] * DEPTH, [None] * DEPTH
    token = None
    for l in reversed(range(DEPTH)):
        w_f1, w_f2, w_i, w_o = (get_w(4 * l + j, None, None)[0] for j in (0, 1, 2, 3))
        h0, h1, h2, fo1, g1, u1, a1, ua, va, q, k, v, o, mixed, lse, po, fo2, g2, u2, a2 = saved[l]
        lng, lnb = sgu_ln_g[l].reshape(1, HW), sgu_ln_b[l].reshape(1, HW)
        bs_t = sgu_b[l].T

        dh, dg_s, du_s, y, dfb, dpar3 = _ffn_bwd_tok(dh, h2, behind(pars[l][2], token), fo2, g2, u2, w_f2)
        token = put_g(4 * l + 1, _ffn_bwd_w(a2, dg_s, du_s, y, dfb))

        dob, d_a, d_b, dpar2g = _outproj_bwd(dh, po, w_o, behind(pars[l][1], token))
        token = put_g(4 * l + 3, _tn_matmul(mixed, dob, 512, "w_out_grad"))
        dqkv = None
        for d in PATTERN_DILATIONS:
            dqkv = _attn_bwd(q, k, v, o, d_b, lse, d, dqkv)
        dsg, dln, dws, dbl = _sgu_bwd(ua, va, d_a, lng, lnb, sgu_w[l], bs_t)
        dh, dp, y2, dpar2 = _inproj_bwd_tok(dh, h1, behind(pars[l][1], token), w_i, dsg, *dqkv, tabs)
        token = put_g(4 * l + 2, _tn_matmul(dp, y2, 1280, "w_in_grad"))

        dh, dg_s, du_s, y, dfb, dpar1 = _ffn_bwd_tok(dh, h0, behind(pars[l][0], token), fo1, g1, u1, w_f1)
        dmods[l] = jnp.concatenate([dpar1[0:3], dpar2[0:2], dpar2g[2:3], dpar3[0:3]], axis=0).reshape(9 * D)
        dgains[l] = jnp.stack([dpar1[3], dpar2[3], dpar3[3]])
        dsgu[l] = (dln[0], dln[1], jnp.where(tril[None], dws, 0.0), jnp.sum(dbl, axis=-1))
        if l > 0:
            token = put_g(4 * l, _ffn_bwd_w(a1, dg_s, du_s, y, dfb))
            continue
        token = small_ready(head, dmods, dgains, dsgu)
        token = put_g(4 * DEPTH, _ffn_bwd_w(a1, dg_s, du_s, y, dfb, token, "down"))
        token = put_g(0, _ffn_bwd_w(a1, dg_s, du_s, y, dfb, token, "gate_up"))
    return dh, token


def kernel(x, c, ada_w, ada_b, norm_g, ffn1_wg, ffn1_wu, ffn1_wd, ffn2_wg, ffn2_wu, ffn2_wd, w_in, sgu_ln_g, sgu_ln_b, sgu_w, sgu_b, w_out, final_g, loss_target, m_ada_w, m_ada_b, m_norm_g, m_ffn1_wg, m_ffn1_wu, m_ffn1_wd, m_ffn2_wg, m_ffn2_wu, m_ffn2_wd, m_w_in, m_sgu_ln_g, m_sgu_ln_b, m_sgu_w, m_sgu_b, m_w_out, m_final_g, v_ada_w, v_ada_b, v_norm_g, v_ffn1_wg, v_ffn1_wu, v_ffn1_wd, v_ffn2_wg, v_ffn2_wu, v_ffn2_wd, v_w_in, v_sgu_ln_g, v_sgu_ln_b, v_sgu_w, v_sgu_b, v_w_out, v_final_g):
    me = 4 * lax.axis_index("x") + 2 * lax.axis_index("y") + lax.axis_index("c")

    tr = lambda w: jnp.swapaxes(w, -1, -2).astype(bf16)
    locs = []
    for l in range(DEPTH):
        locs.append(jnp.stack([tr(ffn1_wg[l]), tr(ffn1_wu[l]), ffn1_wd[l].astype(bf16)]))
        locs.append(jnp.stack([tr(ffn2_wg[l]), tr(ffn2_wu[l]), ffn2_wd[l].astype(bf16)]))
        locs.append(tr(w_in[l]))
        locs.append(w_out[l].astype(bf16))

    c_all, pars = _prepare(me, c, ada_w, ada_b, norm_g)
    locs, pars = lax.optimization_barrier((locs, pars))

    first, locs = lax.optimization_barrier((_all_gather_rows([locs[0]], "gather_first")[0], locs))
    placed = [_place_own(a, me) for a in locs[1:]]
    ready = {0: first}
    groups = ([2, 3], [1], [4], [6, 7], [5])
    plan = {("inproj", 0): (("wait", "gather", 0), ("start", "chips", 1)),
            ("outproj", 0): (("wait", "chips", 1), ("start", "pass", 1), ("start", "chips", 2)),
            ("ffn2", 0): (("wait", "pass", 1),),
            ("ffn1", 1): (("wait", "chips", 2), ("start", "pass", 2), ("start", "gather", 3), ("wait", "pass", 2)),
            ("inproj", 1): (("wait", "gather", 3), ("start", "chips", 4)),
            ("outproj", 1): (("wait", "chips", 4), ("start", "pass", 4)),
            ("ffn2", 1): (("wait", "pass", 4),)}
    flying = {}

    def exchange(action, mode, gi, after):
        if action == "start":
            arrays = flying.pop(gi) if mode == "pass" else [placed[p - 1] for p in groups[gi]]
            arrays, _ = lax.optimization_barrier((arrays, after))
            started = _exchange_start(mode, arrays, "%s_start_%d" % (mode, gi))
            flying[gi] = started[:-1]
            return started[-1]
        arrays = _exchange_wait(mode, flying.pop(gi), after, "%s_wait_%d" % (mode, gi))
        if mode == "chips":
            flying[gi] = arrays
        else:
            ready.update(zip(groups[gi], arrays))
        return None

    first_token = exchange("start", "gather", 0, first)

    def get_w(piece, point, after):
        token = first_token if point == ("ffn1", 0) else None
        for action, mode, gi in plan.get(point, ()):
            started = exchange(action, mode, gi, after)
            if started is not None:
                token = after = started
        return ready.get(piece), token

    sent = {}

    def put_g(piece, grad):
        land = lax.empty((NDEV,) + grad.shape[:-2] + (grad.shape[-2] // NDEV, D), bf16)
        started = _exchange_start("scatter", [grad, land], "scatter_start_%d" % piece)
        sent[piece] = started[:-1]
        return started[-1]

    small_sent = []

    def small_ready(head, dmods, dgains, dsgu):
        loss_part = 0.5 * jnp.sum(head[0]) / D
        small = jnp.concatenate([
            _pad_rows(jnp.stack(dmods)), _pad_rows(jnp.stack(dgains)),
            _pad_rows(jnp.stack([s[0] for s in dsgu])), _pad_rows(jnp.stack([s[1] for s in dsgu])),
            _pad_rows(jnp.stack([s[3] for s in dsgu])), _pad_rows(jnp.stack([s[2] for s in dsgu])),
            _pad_rows(head[3]), _pad_rows(jnp.pad(loss_part[None], (0, CH - 1)))], axis=0)
        started = _exchange_start("gather", [_place_own(small, me, "place_small")], "small_start")
        small_sent.append(started[:-1])
        return started[-1]

    dh, last = _fwd_bwd(x[0], loss_target[0], pars, get_w, put_g, small_ready, sgu_ln_g, sgu_ln_b, sgu_w, sgu_b, final_g)
    grad_x = dh[None]

    got = _exchange_wait("gather", small_sent[0], last, "small_wait")[0].reshape(NDEV, -1, CH)
    tot = _sum_slots(got, "sum_small")
    n_mod, n_gain, n_sw = DEPTH * 9 * D // CH, DEPTH * 3 * D // CH, DEPTH * 4 * CH
    offs = [0, n_mod, n_mod + n_gain, n_mod + n_gain + 8, n_mod + n_gain + 16, n_mod + n_gain + 24]
    g_ada_b = tot[offs[0]:offs[1]].reshape(DEPTH, 9 * D)
    g_gain_full = tot[offs[1]:offs[2]].reshape(DEPTH, 3, D)
    g_ln_g = tot[offs[2]:offs[3]].reshape(DEPTH, 4, CH)
    g_ln_b = tot[offs[3]:offs[4]].reshape(DEPTH, 4, CH)
    g_sb = tot[offs[4]:offs[5]].reshape(DEPTH, 4, CH)
    g_sw = tot[offs[5]:offs[5] + n_sw].reshape(DEPTH, 4, CH, CH)
    g_final = tot[offs[5] + n_sw:offs[5] + n_sw + 8].reshape(D)
    loss = tot[offs[5] + n_sw + 8, 0]
    g_norm = lax.dynamic_slice(g_gain_full, (0, 0, me * OUTS), (DEPTH, 3, OUTS))

    dmod_all = got[:, offs[0]:offs[1]].reshape(NDEV, DEPTH, 9 * D)
    dmod_cols = lax.dynamic_slice(dmod_all, (0, 0, me * ADAS), (NDEV, DEPTH, ADAS)).transpose(1, 0, 2)
    g_ada_w = _ada_grad((c_all * jax.nn.sigmoid(c_all)).T, dmod_cols)

    sums, after = {}, tot

    def collect(piece, after):
        own, land = _exchange_wait("scatter", sent[piece], after, "scatter_wait_%d" % piece)
        sums[piece] = _sum_partials(land, own, me, "sum_grads")
        return sums[piece]

    for piece in (5, 7, 6, 4, 1, 3, 2):
        after = collect(piece, after)
    back = lambda t: jnp.swapaxes(t, -1, -2)
    f2 = jnp.stack([sums[4 * l + 1] for l in range(DEPTH)])
    g_w_in = back(jnp.stack([sums[4 * l + 2] for l in range(DEPTH)]))
    g_w_out = jnp.stack([sums[4 * l + 3] for l in range(DEPTH)])

    ws = [ada_w, ada_b, norm_g, ffn1_wg, ffn1_wu, ffn1_wd, ffn2_wg, ffn2_wu, ffn2_wd, w_in, sgu_ln_g, sgu_ln_b, sgu_w,
          sgu_b, w_out, final_g]
    ms = [m_ada_w, m_ada_b, m_norm_g, m_ffn1_wg, m_ffn1_wu, m_ffn1_wd, m_ffn2_wg, m_ffn2_wu, m_ffn2_wd, m_w_in,
          m_sgu_ln_g, m_sgu_ln_b, m_sgu_w, m_sgu_b, m_w_out, m_final_g]
    vs = [v_ada_w, v_ada_b, v_norm_g, v_ffn1_wg, v_ffn1_wu, v_ffn1_wd, v_ffn2_wg, v_ffn2_wu, v_ffn2_wd, v_w_in,
          v_sgu_ln_g, v_sgu_ln_b, v_sgu_w, v_sgu_b, v_w_out, v_final_g]
    gw = [g_ada_w, g_ada_b, g_norm, None, None, None, back(f2[:, 0]), back(f2[:, 1]), f2[:, 2],
          g_w_in, g_ln_g, g_ln_b, g_sw, g_sb, g_w_out, g_final]
    upd = [None] * len(ws)
    lane_narrow = (3, 4, 6, 7, 9)
    for i in (1, 2, 10, 11, 12, 13, 15, 6, 7, 8, 9, 14, 0):
        upd[i] = _adamw_nd(ws[i], gw[i], ms[i], vs[i], "adamw", i in lane_narrow)
    collect(4 * DEPTH, upd[0][0])
    gw[5] = jnp.stack([sums[4 * DEPTH][0]] + [sums[4 * l][2] for l in range(1, DEPTH)])
    upd[5] = _adamw_nd(ws[5], gw[5], ms[5], vs[5], "adamw", False)
    collect(0, upd[5][0])
    for i in (3, 4):
        gw[i] = back(jnp.stack([sums[4 * l][i - 3] for l in range(DEPTH)]))
        upd[i] = _adamw_nd(ws[i], gw[i], ms[i], vs[i], "adamw", i in lane_narrow)
    return (loss, grad_x, *gw, *[u[0] for u in upd], *[u[1] for u in upd], *[u[2] for u in upd])
```
